```python
import jax
import jax.numpy as jnp
from jax import lax
import numpy as np

D_MODEL = 1024
BATCH = 8
SEQ = 8192
DEPTH = 1

D_LRU = 1024
LRU_BLOCKS = 4
LRU_BLOCK_W = D_LRU // LRU_BLOCKS
CONV_W = 4
LRU_C = 8.0
D_HGRN = 1024
HGRN_EXPAND = 128
HGRN_HEADS = D_HGRN // HGRN_EXPAND
HGRN_DK = HGRN_EXPAND
HGRN_DV = D_HGRN // HGRN_HEADS
CHUNK = 64
D_MIX = D_LRU + D_HGRN
D_IN = 2 * D_LRU + 4 * D_HGRN
EPS = 1e-6

kernel_name = 'hymba_style_rglru_hgrn2_block'


def rmsnorm(x, w):
    xf = x.astype(jnp.float32)
    y = xf * lax.rsqrt(jnp.mean(xf * xf, axis=-1, keepdims=True) + EPS)
    return (y * w.astype(jnp.float32)).astype(x.dtype)


def causal_depthwise_conv(x, w, b):
    seq = x.shape[1]
    xp = jnp.pad(x, ((0, 0), (CONV_W - 1, 0), (0, 0)))
    y = b
    for k in range(CONV_W):
        y = y + w[k] * xp[:, k:k + seq, :]
    return y


def rg_lru(x, w_a, b_a, w_x, b_x, lam):
    bsz, seq, _ = x.shape
    xf = x.astype(jnp.float32)
    xb = xf.reshape(bsz, seq, LRU_BLOCKS, LRU_BLOCK_W)
    r = jax.nn.sigmoid(jnp.einsum('bsnh,nhk->bsnk', xb, w_a.astype(jnp.float32)) + b_a.astype(jnp.float32)).reshape(bsz, seq, D_LRU)
    i = jax.nn.sigmoid(jnp.einsum('bsnh,nhk->bsnk', xb, w_x.astype(jnp.float32)) + b_x.astype(jnp.float32)).reshape(bsz, seq, D_LRU)
    log_a = -LRU_C * r * jax.nn.softplus(-lam.astype(jnp.float32))
    a = jnp.exp(log_a)
    u = jnp.sqrt(-jnp.expm1(2.0 * log_a)) * (i * xf)

    def combine(left, right):
        a_l, h_l = left
        a_r, h_r = right
        return a_l * a_r, a_r * h_l + h_r

    _, h = lax.associative_scan(combine, (a, u), axis=1)
    return h.astype(x.dtype)


def hgrn2_chunked(q, k, v, g):
    bsz, seq = q.shape[0], q.shape[1]
    n_chunks = seq // CHUNK

    def to_chunks(t):
        return t.reshape(bsz, n_chunks, CHUNK, HGRN_HEADS, t.shape[-1]).transpose(1, 0, 3, 2, 4)

    mask = jnp.tril(jnp.ones((CHUNK, CHUNK), dtype=bool))

    def step(state, inp):
        qc, kc, vc, gc = inp
        b = jnp.cumsum(gc, axis=2)
        o_inter = jnp.einsum('bhtk,bhkv->bhtv', qc * jnp.exp(b), state)
        diff = b[:, :, :, None, :] - b[:, :, None, :, :]
        decay = jnp.where(mask[:, :, None], jnp.exp(jnp.minimum(diff, 0.0)), 0.0)
        scores = jnp.einsum('bhtk,bhsk,bhtsk->bhts', qc, kc, decay)
        o_intra = jnp.einsum('bhts,bhsv->bhtv', scores, vc)
        b_last = b[:, :, -1:, :]
        new_state = jnp.exp(b_last[:, :, 0, :, None]) * state + jnp.einsum('bhsk,bhsv->bhkv', kc * jnp.exp(b_last - b), vc)
        return new_state, o_inter + o_intra

    s0 = jnp.zeros((bsz, HGRN_HEADS, HGRN_DK, HGRN_DV), jnp.float32)
    _, o = lax.scan(step, s0, (to_chunks(q), to_chunks(k), to_chunks(v), to_chunks(g)))
    return o.transpose(1, 0, 3, 2, 4).reshape(bsz, seq, HGRN_HEADS, HGRN_DV)


def _fwd_setup_inputs(seed: int = 0) -> dict:
    key = jax.random.key(seed)
    ks = jax.random.split(key, 14)
    f32 = jnp.float32

    def nrm(k, shape, scale):
        return scale * jax.random.normal(k, shape, f32)

    x = nrm(ks[0], (BATCH, SEQ, D_MODEL), 1.0)
    pre_norm_w = 1.0 + nrm(ks[1], (DEPTH, D_MODEL), 0.05)
    w_in = nrm(ks[2], (DEPTH, D_MODEL, D_IN), D_MODEL ** -0.5)
    conv_w = nrm(ks[3], (DEPTH, CONV_W, D_LRU), CONV_W ** -0.5)
    conv_b = nrm(ks[4], (DEPTH, D_LRU), 0.01)
    lru_w_a = nrm(ks[5], (DEPTH, LRU_BLOCKS, LRU_BLOCK_W, LRU_BLOCK_W), LRU_BLOCK_W ** -0.5)
    lru_b_a = nrm(ks[6], (DEPTH, LRU_BLOCKS, LRU_BLOCK_W), 0.01)
    lru_w_x = nrm(ks[7], (DEPTH, LRU_BLOCKS, LRU_BLOCK_W, LRU_BLOCK_W), LRU_BLOCK_W ** -0.5)
    lru_b_x = nrm(ks[8], (DEPTH, LRU_BLOCKS, LRU_BLOCK_W), 0.01)
    a_pow_c = jax.random.uniform(ks[9], (DEPTH, D_LRU), f32, 0.9, 0.999)
    s = a_pow_c ** (1.0 / LRU_C)
    lru_lambda = jnp.log(s) - jnp.log1p(-s)
    hgrn_lb_logits = nrm(ks[10], (DEPTH + 1, D_HGRN), 0.1)
    hgrn_gnorm_w = 1.0 + nrm(ks[11], (DEPTH, D_HGRN), 0.05)
    w_out = nrm(ks[12], (DEPTH, D_MIX, D_MODEL), D_MIX ** -0.5)
    post_norm_w = 1.0 + nrm(ks[13], (DEPTH, D_MODEL), 0.05)
    return {'x': x, 'pre_norm_w': pre_norm_w, 'w_in': w_in, 'conv_w': conv_w, 'conv_b': conv_b,
            'lru_w_a': lru_w_a, 'lru_b_a': lru_b_a, 'lru_w_x': lru_w_x, 'lru_b_x': lru_b_x,
            'lru_lambda': lru_lambda, 'hgrn_lb_logits': hgrn_lb_logits, 'hgrn_gnorm_w': hgrn_gnorm_w,
            'w_out': w_out, 'post_norm_w': post_norm_w}


def _fwd_reference(x, pre_norm_w, w_in, conv_w, conv_b, lru_w_a, lru_b_a, lru_w_x, lru_b_x,
              lru_lambda, hgrn_lb_logits, hgrn_gnorm_w, w_out, post_norm_w):
    bsz, seq, _ = x.shape
    dt = x.dtype
    f32 = jnp.float32
    lower_bounds = jnp.cumsum(jax.nn.softmax(hgrn_lb_logits.astype(f32), axis=0), axis=0)
    split_at = [D_LRU, 2 * D_LRU, 2 * D_LRU + D_HGRN, 2 * D_LRU + 2 * D_HGRN, 2 * D_LRU + 3 * D_HGRN]

    def heads(t):
        return t.reshape(bsz, seq, HGRN_HEADS, -1)

    h = x
    for layer in range(DEPTH):
        u = rmsnorm(h, pre_norm_w[layer])
        p = jnp.einsum('bsd,de->bse', u, w_in[layer])
        lru_x, lru_gate, q, f_raw, v, hgrn_gate = jnp.split(p, split_at, axis=-1)

        xc = causal_depthwise_conv(lru_x, conv_w[layer], conv_b[layer])
        y_lru = rg_lru(xc, lru_w_a[layer], lru_b_a[layer], lru_w_x[layer], lru_b_x[layer],
                       lru_lambda[layer]) * jax.nn.silu(lru_gate)

        lb = lower_bounds[layer]
        f = lb + (1.0 - lb) * jax.nn.sigmoid(f_raw.astype(f32))
        o = hgrn2_chunked(heads(jax.nn.silu(q.astype(f32))), heads(1.0 - f),
                          heads(v.astype(f32)), heads(jnp.log(f)))
        o = rmsnorm(o, hgrn_gnorm_w[layer].reshape(HGRN_HEADS, HGRN_DV)).reshape(bsz, seq, D_HGRN)
        y_hgrn = (o * jax.nn.silu(hgrn_gate.astype(f32))).astype(dt)

        y = jnp.einsum('bsm,md->bsd', jnp.concatenate([y_lru.astype(dt), y_hgrn], axis=-1), w_out[layer])
        h = h + rmsnorm(y, post_norm_w[layer]).astype(dt)
    return h


import jax as _jax
import jax.numpy as _jnp

TWIN_FORMAT = 'train_step'
FWD_PARAMS = ['x', 'pre_norm_w', 'w_in', 'conv_w', 'conv_b', 'lru_w_a', 'lru_b_a', 'lru_w_x', 'lru_b_x', 'lru_lambda', 'hgrn_lb_logits', 'hgrn_gnorm_w', 'w_out', 'post_norm_w']
TWIN_WEIGHTS = ['pre_norm_w', 'w_in', 'conv_w', 'conv_b', 'lru_w_a', 'lru_b_a', 'lru_w_x', 'lru_b_x', 'lru_lambda', 'hgrn_lb_logits', 'hgrn_gnorm_w', 'w_out', 'post_norm_w']
TWIN_DIFF_INPUT = 'x'
TWIN_INPUTS = ['x', 'pre_norm_w', 'w_in', 'conv_w', 'conv_b', 'lru_w_a', 'lru_b_a', 'lru_w_x', 'lru_b_x', 'lru_lambda', 'hgrn_lb_logits', 'hgrn_gnorm_w', 'w_out', 'post_norm_w', 'loss_target', 'm_pre_norm_w', 'm_w_in', 'm_conv_w', 'm_conv_b', 'm_lru_w_a', 'm_lru_b_a', 'm_lru_w_x', 'm_lru_b_x', 'm_lru_lambda', 'm_hgrn_lb_logits', 'm_hgrn_gnorm_w', 'm_w_out', 'm_post_norm_w', 'v_pre_norm_w', 'v_w_in', 'v_conv_w', 'v_conv_b', 'v_lru_w_a', 'v_lru_b_a', 'v_lru_w_x', 'v_lru_b_x', 'v_lru_lambda', 'v_hgrn_lb_logits', 'v_hgrn_gnorm_w', 'v_w_out', 'v_post_norm_w']
TWIN_OUTPUTS = ['loss', 'grad_x', 'grad_pre_norm_w', 'grad_w_in', 'grad_conv_w', 'grad_conv_b', 'grad_lru_w_a', 'grad_lru_b_a', 'grad_lru_w_x', 'grad_lru_b_x', 'grad_lru_lambda', 'grad_hgrn_lb_logits', 'grad_hgrn_gnorm_w', 'grad_w_out', 'grad_post_norm_w', 'delta_pre_norm_w', 'delta_w_in', 'delta_conv_w', 'delta_conv_b', 'delta_lru_w_a', 'delta_lru_b_a', 'delta_lru_w_x', 'delta_lru_b_x', 'delta_lru_lambda', 'delta_hgrn_lb_logits', 'delta_hgrn_gnorm_w', 'delta_w_out', 'delta_post_norm_w', 'new_m_pre_norm_w', 'new_m_w_in', 'new_m_conv_w', 'new_m_conv_b', 'new_m_lru_w_a', 'new_m_lru_b_a', 'new_m_lru_w_x', 'new_m_lru_b_x', 'new_m_lru_lambda', 'new_m_hgrn_lb_logits', 'new_m_hgrn_gnorm_w', 'new_m_w_out', 'new_m_post_norm_w', 'new_v_pre_norm_w', 'new_v_w_in', 'new_v_conv_w', 'new_v_conv_b', 'new_v_lru_w_a', 'new_v_lru_b_a', 'new_v_lru_w_x', 'new_v_lru_b_x', 'new_v_lru_lambda', 'new_v_hgrn_lb_logits', 'new_v_hgrn_gnorm_w', 'new_v_w_out', 'new_v_post_norm_w']
TWIN_LEAF_KINDS = {'loss': 'loss', 'grad_x': 'grad_x', 'grad_pre_norm_w': 'grad_w', 'grad_w_in': 'grad_w', 'grad_conv_w': 'grad_w', 'grad_conv_b': 'grad_w', 'grad_lru_w_a': 'grad_w', 'grad_lru_b_a': 'grad_w', 'grad_lru_w_x': 'grad_w', 'grad_lru_b_x': 'grad_w', 'grad_lru_lambda': 'grad_w', 'grad_hgrn_lb_logits': 'grad_w', 'grad_hgrn_gnorm_w': 'grad_w', 'grad_w_out': 'grad_w', 'grad_post_norm_w': 'grad_w', 'delta_pre_norm_w': 'delta_w', 'delta_w_in': 'delta_w', 'delta_conv_w': 'delta_w', 'delta_conv_b': 'delta_w', 'delta_lru_w_a': 'delta_w', 'delta_lru_b_a': 'delta_w', 'delta_lru_w_x': 'delta_w', 'delta_lru_b_x': 'delta_w', 'delta_lru_lambda': 'delta_w', 'delta_hgrn_lb_logits': 'delta_w', 'delta_hgrn_gnorm_w': 'delta_w', 'delta_w_out': 'delta_w', 'delta_post_norm_w': 'delta_w', 'new_m_pre_norm_w': 'new_m', 'new_m_w_in': 'new_m', 'new_m_conv_w': 'new_m', 'new_m_conv_b': 'new_m', 'new_m_lru_w_a': 'new_m', 'new_m_lru_b_a': 'new_m', 'new_m_lru_w_x': 'new_m', 'new_m_lru_b_x': 'new_m', 'new_m_lru_lambda': 'new_m', 'new_m_hgrn_lb_logits': 'new_m', 'new_m_hgrn_gnorm_w': 'new_m', 'new_m_w_out': 'new_m', 'new_m_post_norm_w': 'new_m', 'new_v_pre_norm_w': 'new_v', 'new_v_w_in': 'new_v', 'new_v_conv_w': 'new_v', 'new_v_conv_b': 'new_v', 'new_v_lru_w_a': 'new_v', 'new_v_lru_b_a': 'new_v', 'new_v_lru_w_x': 'new_v', 'new_v_lru_b_x': 'new_v', 'new_v_lru_lambda': 'new_v', 'new_v_hgrn_lb_logits': 'new_v', 'new_v_hgrn_gnorm_w': 'new_v', 'new_v_w_out': 'new_v', 'new_v_post_norm_w': 'new_v'}


def _forward(args):
    return _fwd_reference(*[args[k] for k in FWD_PARAMS])


def _output_shape():
    out = _jax.eval_shape(lambda: _forward(_fwd_setup_inputs(0)))
    return out.shape, out.dtype

N_MICROBATCH = 1
ADAM_LR = 0.001
ADAM_B1 = 0.9
ADAM_B2 = 0.999
ADAM_EPS = 1e-08
ADAM_WD = 0.01
ADAM_STEP = 10
PER_EXAMPLE_BATCH_AXIS = {'x': 0, 'loss_target': 0}
SHARED_INPUTS = []
_WEIGHT_DTYPES = {'pre_norm_w': _jnp.float32, 'w_in': _jnp.float32, 'conv_w': _jnp.float32, 'conv_b': _jnp.float32, 'lru_w_a': _jnp.float32, 'lru_b_a': _jnp.float32, 'lru_w_x': _jnp.float32, 'lru_b_x': _jnp.float32, 'lru_lambda': _jnp.float32, 'hgrn_lb_logits': _jnp.float32, 'hgrn_gnorm_w': _jnp.float32, 'w_out': _jnp.float32, 'post_norm_w': _jnp.float32}
MOMENT_SCALE = {'pre_norm_w': 6.471270e-01, 'w_in': 2.279589e-01, 'conv_w': 2.685555e-01, 'conv_b': 8.999837e+00, 'lru_w_a': 1.010121e-01, 'lru_b_a': 8.509672e-02, 'lru_w_x': 1.855524e-01, 'lru_b_x': 9.479808e-02, 'lru_lambda': 1.533695e-01, 'hgrn_lb_logits': 3.001870e-02, 'hgrn_gnorm_w': 3.702705e-01, 'w_out': 4.208168e-01, 'post_norm_w': 6.405844e+01}


def _to_microbatches(a, axis):
    t = _jnp.moveaxis(a, axis, 0)
    t = t.reshape((N_MICROBATCH, t.shape[0] // N_MICROBATCH) + t.shape[1:])
    return _jnp.moveaxis(t, 1, axis + 1)


def setup_inputs(seed: int = 0) -> dict:
    inp = _fwd_setup_inputs(seed)
    key = _jax.random.fold_in(_jax.random.key(seed), 7919)
    shape, _ = _output_shape()
    out = dict(inp)
    out["loss_target"] = _jax.random.normal(_jax.random.fold_in(key, 0), shape, _jnp.float32)
    for i, name in enumerate(TWIN_WEIGHTS):
        w = inp[name].astype(_jnp.float32)
        if MOMENT_SCALE is None:
            s = _jnp.sqrt(_jnp.mean(_jnp.square(w)) + 1e-30)
        else:
            s = MOMENT_SCALE[name]
        km, kv = _jax.random.split(_jax.random.fold_in(key, i + 1))
        out[name] = w
        out["m_" + name] = s * _jax.random.normal(km, w.shape, _jnp.float32)
        out["v_" + name] = (s * s) * _jax.random.uniform(kv, w.shape, _jnp.float32, 0.5, 1.5)
    if N_MICROBATCH > 1:
        for name, axis in PER_EXAMPLE_BATCH_AXIS.items():
            out[name] = _to_microbatches(out[name], axis)
    return {'x': out['x'], 'pre_norm_w': out['pre_norm_w'], 'w_in': out['w_in'], 'conv_w': out['conv_w'], 'conv_b': out['conv_b'], 'lru_w_a': out['lru_w_a'], 'lru_b_a': out['lru_b_a'], 'lru_w_x': out['lru_w_x'], 'lru_b_x': out['lru_b_x'], 'lru_lambda': out['lru_lambda'], 'hgrn_lb_logits': out['hgrn_lb_logits'], 'hgrn_gnorm_w': out['hgrn_gnorm_w'], 'w_out': out['w_out'], 'post_norm_w': out['post_norm_w'], 'loss_target': out['loss_target'], 'm_pre_norm_w': out['m_pre_norm_w'], 'm_w_in': out['m_w_in'], 'm_conv_w': out['m_conv_w'], 'm_conv_b': out['m_conv_b'], 'm_lru_w_a': out['m_lru_w_a'], 'm_lru_b_a': out['m_lru_b_a'], 'm_lru_w_x': out['m_lru_w_x'], 'm_lru_b_x': out['m_lru_b_x'], 'm_lru_lambda': out['m_lru_lambda'], 'm_hgrn_lb_logits': out['m_hgrn_lb_logits'], 'm_hgrn_gnorm_w': out['m_hgrn_gnorm_w'], 'm_w_out': out['m_w_out'], 'm_post_norm_w': out['m_post_norm_w'], 'v_pre_norm_w': out['v_pre_norm_w'], 'v_w_in': out['v_w_in'], 'v_conv_w': out['v_conv_w'], 'v_conv_b': out['v_conv_b'], 'v_lru_w_a': out['v_lru_w_a'], 'v_lru_b_a': out['v_lru_b_a'], 'v_lru_w_x': out['v_lru_w_x'], 'v_lru_b_x': out['v_lru_b_x'], 'v_lru_lambda': out['v_lru_lambda'], 'v_hgrn_lb_logits': out['v_hgrn_lb_logits'], 'v_hgrn_gnorm_w': out['v_hgrn_gnorm_w'], 'v_w_out': out['v_w_out'], 'v_post_norm_w': out['v_post_norm_w']}


def _loss(weights, diff, rest, loss_target):
    with _jax.named_scope("forward"):
        args = {**rest, TWIN_DIFF_INPUT: diff, **{k: w.astype(_WEIGHT_DTYPES[k]) for k, w in weights.items()}}
        y = _forward(args)
    with _jax.named_scope("loss_head"):
        err = _jnp.square(y.astype(_jnp.float32) - loss_target)
        return 0.5 * _jnp.sum(_jnp.mean(err, axis=-1)) if err.ndim else 0.5 * err


def _adamw(w, g, m, v):
    m = ADAM_B1 * m + (1.0 - ADAM_B1) * g
    v = ADAM_B2 * v + (1.0 - ADAM_B2) * _jnp.square(g)
    m_hat = m / (1.0 - ADAM_B1 ** ADAM_STEP)
    v_hat = v / (1.0 - ADAM_B2 ** ADAM_STEP)
    delta = -ADAM_LR * (m_hat / (_jnp.sqrt(v_hat) + ADAM_EPS) + ADAM_WD * w)
    return delta, m, v


def reference(x, pre_norm_w, w_in, conv_w, conv_b, lru_w_a, lru_b_a, lru_w_x, lru_b_x, lru_lambda, hgrn_lb_logits, hgrn_gnorm_w, w_out, post_norm_w, loss_target, m_pre_norm_w, m_w_in, m_conv_w, m_conv_b, m_lru_w_a, m_lru_b_a, m_lru_w_x, m_lru_b_x, m_lru_lambda, m_hgrn_lb_logits, m_hgrn_gnorm_w, m_w_out, m_post_norm_w, v_pre_norm_w, v_w_in, v_conv_w, v_conv_b, v_lru_w_a, v_lru_b_a, v_lru_w_x, v_lru_b_x, v_lru_lambda, v_hgrn_lb_logits, v_hgrn_gnorm_w, v_w_out, v_post_norm_w):
    given = dict(x=x, pre_norm_w=pre_norm_w, w_in=w_in, conv_w=conv_w, conv_b=conv_b, lru_w_a=lru_w_a, lru_b_a=lru_b_a, lru_w_x=lru_w_x, lru_b_x=lru_b_x, lru_lambda=lru_lambda, hgrn_lb_logits=hgrn_lb_logits, hgrn_gnorm_w=hgrn_gnorm_w, w_out=w_out, post_norm_w=post_norm_w, loss_target=loss_target, m_pre_norm_w=m_pre_norm_w, m_w_in=m_w_in, m_conv_w=m_conv_w, m_conv_b=m_conv_b, m_lru_w_a=m_lru_w_a, m_lru_b_a=m_lru_b_a, m_lru_w_x=m_lru_w_x, m_lru_b_x=m_lru_b_x, m_lru_lambda=m_lru_lambda, m_hgrn_lb_logits=m_hgrn_lb_logits, m_hgrn_gnorm_w=m_hgrn_gnorm_w, m_w_out=m_w_out, m_post_norm_w=m_post_norm_w, v_pre_norm_w=v_pre_norm_w, v_w_in=v_w_in, v_conv_w=v_conv_w, v_conv_b=v_conv_b, v_lru_w_a=v_lru_w_a, v_lru_b_a=v_lru_b_a, v_lru_w_x=v_lru_w_x, v_lru_b_x=v_lru_b_x, v_lru_lambda=v_lru_lambda, v_hgrn_lb_logits=v_hgrn_lb_logits, v_hgrn_gnorm_w=v_hgrn_gnorm_w, v_w_out=v_w_out, v_post_norm_w=v_post_norm_w)
    weights = {n: given[n] for n in TWIN_WEIGHTS}
    shared = {n: given[n] for n in SHARED_INPUTS}
    per_example = {n: given[n] for n in ['x']}
    grad_fn = _jax.value_and_grad(_loss, argnums=(0, 1))

    def one_microbatch(ex, loss_target):
        ex = dict(ex)
        diff = ex.pop(TWIN_DIFF_INPUT)
        return grad_fn(weights, diff, {**shared, **ex}, loss_target)

    if N_MICROBATCH == 1:
        loss, (grad_w, grad_x) = one_microbatch(per_example, given["loss_target"])
    else:
        def body(carry, xs):
            loss_sum, grad_sum = carry
            l_k, (gw_k, gx_k) = one_microbatch(xs[0], xs[1])
            with _jax.named_scope("update"):
                return (loss_sum + l_k, _jax.tree.map(_jnp.add, grad_sum, gw_k)), gx_k

        init = (_jnp.zeros((), _jnp.float32), _jax.tree.map(_jnp.zeros_like, weights))
        (loss, grad_w), grad_x = _jax.lax.scan(body, init, (per_example, given["loss_target"]))
    with _jax.named_scope("update"):
        delta_w, new_m, new_v = {}, {}, {}
        for n in TWIN_WEIGHTS:
            delta_w[n], new_m[n], new_v[n] = _adamw(weights[n], grad_w[n], given["m_" + n], given["v_" + n])
    return (loss, grad_x, *[grad_w[n] for n in TWIN_WEIGHTS], *[delta_w[n] for n in TWIN_WEIGHTS],
            *[new_m[n] for n in TWIN_WEIGHTS], *[new_v[n] for n in TWIN_WEIGHTS])
```

```python
import functools

import jax
import jax.numpy as jnp
from jax import lax
from jax.experimental import pallas as pl
from jax.experimental.pallas import tpu as pltpu

F32 = jnp.float32
BF16 = jnp.bfloat16
_MXU_DTYPE = jnp.bfloat16

D_MODEL = 1024
N_SEG = 6
UNIT = 512
N_UNIT = 12
LRU_BLOCKS = 4
LRU_BW = 256
LRU_C = 8.0
HEADS = 8
HEAD_D = 128
CHUNK = 64
EPS = 1e-6
N_CHIPS = 4
EXP_CLAMP = 80.0

ADAM_LR = 0.001
ADAM_B1 = 0.9
ADAM_B2 = 0.999
ADAM_EPS = 1e-08
ADAM_WD = 0.01
ADAM_STEP = 10

SMALL_ROWS = 16
MESH = pl.DeviceIdType.MESH


def _sigmoid(x):
    return 1.0 / (1.0 + jnp.exp(-x))


def _mx(x):
    return x.astype(_MXU_DTYPE)


def _dot(a, b):
    return jnp.dot(_mx(a), _mx(b), preferred_element_type=F32)


def _dot_nt(a, b):
    return lax.dot_general(_mx(a), _mx(b), (((1,), (1,)), ((), ())), preferred_element_type=F32)


def _dot_tn(a, b):
    return lax.dot_general(_mx(a), _mx(b), (((0,), (0,)), ((), ())), preferred_element_type=F32)


def _rows(shape):
    return lax.broadcasted_iota(jnp.int32, shape, 0)


def _cols(shape):
    return lax.broadcasted_iota(jnp.int32, shape, 1)


def _softplus_neg(lam):
    z = -lam
    e = jnp.exp(-jnp.abs(z))
    series = e * (1.0 - e * (0.5 - e * (1.0 / 3.0 - 0.25 * e)))
    return jnp.maximum(z, 0.0) + jnp.where(e < 1e-2, series, jnp.log(1.0 + e))


def _neg_expm1(y):
    series = -y * (1.0 + y * (0.5 + y * (1.0 / 6.0 + y * (1.0 / 24.0))))
    return jnp.where(y > -0.03, series, 1.0 - jnp.exp(y))


def _split3(x):
    hi = x.astype(BF16)
    r1 = x - hi.astype(F32)
    mid = r1.astype(BF16)
    lo = (r1 - mid.astype(F32)).astype(BF16)
    return hi, mid, lo


def _tri_sum(tri, x):
    hi, mid, lo = _split3(x)
    d = lambda p: jnp.dot(tri, p, preferred_element_type=F32)
    return d(hi) + d(mid) + d(lo)


def _shift_down(x, halo, s):
    if s == 0:
        return x
    r = pltpu.roll(x, s, 0)
    top = jnp.where(_rows(halo.shape) < s, pltpu.roll(halo, s, 0), r[0:8])
    return jnp.concatenate([top, r[8:]], axis=0)


def _shift_up(x, head, s):
    if s == 0:
        return x
    n = x.shape[0]
    r = pltpu.roll(x, n - s, 0)
    bottom = jnp.where(_rows(head.shape) >= 8 - s, pltpu.roll(head, 8 - s, 0), r[n - 8:n])
    return jnp.concatenate([r[:n - 8], bottom], axis=0)


def _lru_gates(lx, halo, cw, cb, wa, wx, ba, bx, sp):
    xs = [_shift_down(lx, halo, 3 - k) for k in range(4)]
    xc = cb + cw[0:1] * xs[0] + cw[1:2] * xs[1] + cw[2:3] * xs[2] + cw[3:4] * xs[3]
    zr, zi = [], []
    for n in range(LRU_BLOCKS):
        xb = _mx(xc[:, n * LRU_BW:(n + 1) * LRU_BW])
        zr.append(jnp.dot(xb, wa[n], preferred_element_type=F32))
        zi.append(jnp.dot(xb, wx[n], preferred_element_type=F32))
    r = _sigmoid(jnp.concatenate(zr, axis=1) + ba)
    ig = _sigmoid(jnp.concatenate(zi, axis=1) + bx)
    la = (-LRU_C * sp) * r
    a = jnp.exp(la)
    mult = jnp.sqrt(_neg_expm1(2.0 * la))
    return xs, xc, r, ig, a, mult


def _hgrn_prep(q, fr, lb, tri):
    sig = _sigmoid(fr)
    f = lb + (1.0 - lb) * sig
    k = 1.0 - f
    sq = _sigmoid(q)
    qs = q * sq
    b = _tri_sum(tri, jnp.log(f))
    bm = b[CHUNK // 2 - 1:CHUNK // 2]
    bc = b[CHUNK - 1:CHUNK]
    eb = jnp.exp(b)
    em1 = jnp.exp(jnp.minimum(b - bm, EXP_CLAMP))
    em2 = jnp.exp(jnp.minimum(bm - b, EXP_CLAMP))
    ee = jnp.exp(bc - b)
    ebc = jnp.exp(bc)
    return sig, f, k, sq, qs, eb, em1, em2, ee, ebc


def _adamw(w, g, m, v):
    m = ADAM_B1 * m + (1.0 - ADAM_B1) * g
    v = ADAM_B2 * v + (1.0 - ADAM_B2) * jnp.square(g)
    m_hat = m / (1.0 - ADAM_B1 ** ADAM_STEP)
    v_hat = v / (1.0 - ADAM_B2 ** ADAM_STEP)
    delta = -ADAM_LR * (m_hat / (jnp.sqrt(v_hat) + ADAM_EPS) + ADAM_WD * w)
    return delta, m, v


def _full(shape):
    nd = len(shape)
    return pl.BlockSpec(shape, lambda *_: (0,) * nd)


def _seq_params():
    return pltpu.CompilerParams(dimension_semantics=("arbitrary",))


def _rmsnorm_pre(x, w, tm):
    t = x.shape[0]

    def body(x_ref, w_ref, u_ref):
        xv = x_ref[...]
        rstd = lax.rsqrt(jnp.mean(xv * xv, axis=-1, keepdims=True) + EPS)
        u_ref[...] = (xv * rstd * w_ref[...]).astype(_MXU_DTYPE)

    return pl.pallas_call(
        body, name="rmsnorm_pre", grid=(t // tm,),
        in_specs=[pl.BlockSpec((tm, D_MODEL), lambda i: (i, 0)), _full((1, D_MODEL))],
        out_specs=pl.BlockSpec((tm, D_MODEL), lambda i: (i, 0)),
        out_shape=jax.ShapeDtypeStruct((t, D_MODEL), _MXU_DTYPE),
        compiler_params=pltpu.CompilerParams(dimension_semantics=("parallel",)),
    )(x, w)


def _in_proj(u, wg, tm):
    t = u.shape[0]

    def body(u_ref, w_ref, p_ref):
        p_ref[...] = jnp.dot(u_ref[...], w_ref[...], preferred_element_type=F32)

    return pl.pallas_call(
        body, name="in_proj", grid=(N_UNIT, t // tm),
        in_specs=[pl.BlockSpec((tm, D_MODEL), lambda m, i: (i, 0)),
                  pl.BlockSpec((None, D_MODEL, UNIT), lambda m, i: (m, 0, 0))],
        out_specs=pl.BlockSpec((None, tm, UNIT), lambda m, i: (m // 2, i, m % 2)),
        out_shape=jax.ShapeDtypeStruct((N_SEG, t, D_MODEL), F32),
        compiler_params=pltpu.CompilerParams(dimension_semantics=("arbitrary", "arbitrary")),
    )(u, wg)


def _lru_fwd(p, wa, wx, ba, bx, lam, cw, cb, tm):
    t = p.shape[1]
    ng = tm // 8

    def body(lx_ref, halo_ref, lg_ref, wa_ref, wx_ref, ba_ref, bx_ref, lam_ref, cw_ref, cb_ref,
             h_ref, y_ref, a_s, u_s, hc_s):
        i = pl.program_id(0)

        @pl.when(i == 0)
        def _():
            hc_s[...] = jnp.zeros_like(hc_s)

        sp = _softplus_neg(lam_ref[...])
        halo = jnp.where(i == 0, 0.0, halo_ref[...])
        _, xc, _, ig, a, mult = _lru_gates(lx_ref[...], halo, cw_ref[...], cb_ref[...], wa_ref[...],
                                           wx_ref[...], ba_ref[...], bx_ref[...], sp)
        a_s[...] = a
        u_s[...] = mult * (ig * xc)
        row8 = _rows((8, D_MODEL))

        def group(g, hc):
            rows = pl.ds(pl.multiple_of(g * 8, 8), 8)
            av = a_s[rows, :]
            uv = u_s[rows, :]
            for s in (1, 2, 4):
                a_sh = jnp.where(row8 >= s, pltpu.roll(av, s, 0), 1.0)
                u_sh = jnp.where(row8 >= s, pltpu.roll(uv, s, 0), 0.0)
                uv = av * u_sh + uv
                av = av * a_sh
            hh = av * hc + uv
            h_ref[rows, :] = hh
            return jnp.broadcast_to(hh[7:8, :], (8, D_MODEL))

        hc_s[...] = lax.fori_loop(0, ng, group, hc_s[...])
        lg = lg_ref[...]
        y_ref[...] = (h_ref[...] * (lg * _sigmoid(lg))).astype(_MXU_DTYPE)

    seg = lambda s: pl.BlockSpec((None, tm, D_MODEL), lambda i: (s, i, 0))
    return pl.pallas_call(
        body, name="lru_fwd", grid=(t // tm,),
        in_specs=[seg(0),
                  pl.BlockSpec((None, 8, D_MODEL), lambda i: (0, jnp.maximum(i * (tm // 8) - 1, 0), 0)),
                  seg(1),
                  _full((LRU_BLOCKS, LRU_BW, LRU_BW)), _full((LRU_BLOCKS, LRU_BW, LRU_BW)),
                  _full((1, D_MODEL)), _full((1, D_MODEL)), _full((1, D_MODEL)),
                  _full((4, D_MODEL)), _full((1, D_MODEL))],
        out_specs=[pl.BlockSpec((tm, D_MODEL), lambda i: (i, 0)),
                   pl.BlockSpec((None, tm, D_MODEL), lambda i: (0, i, 0))],
        out_shape=[jax.ShapeDtypeStruct((t, D_MODEL), F32),
                   jax.ShapeDtypeStruct((2, t, D_MODEL), _MXU_DTYPE)],
        scratch_shapes=[pltpu.VMEM((tm, D_MODEL), F32), pltpu.VMEM((tm, D_MODEL), F32),
                        pltpu.VMEM((8, D_MODEL), F32)],
        compiler_params=_seq_params(),
    )(p, p, p, wa, wx, ba, bx, lam, cw, cb)


def _hgrn_fwd(p, logits, gw, y2, tm):
    t = p.shape[1]
    nc = tm // CHUNK

    def body(qf_ref, vh_ref, lg_ref, gw_ref, y_in, y_ref, o_ref, st_ref, state):
        del y_in
        i = pl.program_id(0)

        @pl.when(i == 0)
        def _():
            state[...] = jnp.zeros_like(state)

        l0 = lg_ref[0:1, :]
        l1 = lg_ref[1:2, :]
        lb = 1.0 / (1.0 + jnp.exp(l1 - l0))
        gwv = gw_ref[...]
        tri = (_rows((CHUNK, CHUNK)) >= _cols((CHUNK, CHUNK))).astype(BF16)
        causal = _rows((CHUNK, CHUNK)) >= _cols((CHUNK, CHUNK))

        def chunk(c, carry):
            rows = pl.ds(pl.multiple_of(c * CHUNK, CHUNK), CHUNK)
            q = qf_ref[0, rows, :]
            fr = qf_ref[1, rows, :]
            v = vh_ref[0, rows, :]
            hg = vh_ref[1, rows, :]
            _, _, k, _, qs, eb, em1, em2, ee, ebc = _hgrn_prep(q, fr, lb, tri)
            qi, qm, km, ke = qs * eb, qs * em1, k * em2, k * ee
            gate = hg * _sigmoid(hg)
            for h in range(HEADS):
                ln = slice(h * HEAD_D, (h + 1) * HEAD_D)
                s0 = state[h]
                st_ref[c, h] = s0
                att = jnp.where(causal, _dot_nt(qm[:, ln], km[:, ln]), 0.0)
                o = _dot_nt(qi[:, ln], s0) + _dot(att, v[:, ln])
                state[h] = s0 * ebc[:, ln] + _dot_tn(v[:, ln], ke[:, ln])
                rstd = lax.rsqrt(jnp.mean(o * o, axis=-1, keepdims=True) + EPS)
                o_ref[rows, ln] = o
                y_ref[rows, ln] = (o * rstd * gwv[:, ln] * gate[:, ln]).astype(_MXU_DTYPE)
            return carry

        lax.fori_loop(0, nc, chunk, 0)

    pair = lambda s: pl.BlockSpec((2, tm, D_MODEL), lambda i: (s, i, 0))
    return pl.pallas_call(
        body, name="hgrn_fwd", grid=(t // tm,),
        in_specs=[pair(1), pair(2), _full((2, D_MODEL)), _full((1, D_MODEL)),
                  pl.BlockSpec(memory_space=pl.ANY)],
        out_specs=[pl.BlockSpec((None, tm, D_MODEL), lambda i: (1, i, 0)),
                   pl.BlockSpec((tm, D_MODEL), lambda i: (i, 0)),
                   pl.BlockSpec((nc, HEADS, HEAD_D, HEAD_D), lambda i: (i, 0, 0, 0))],
        out_shape=[jax.ShapeDtypeStruct((2, t, D_MODEL), _MXU_DTYPE),
                   jax.ShapeDtypeStruct((t, D_MODEL), F32),
                   jax.ShapeDtypeStruct((t // CHUNK, HEADS, HEAD_D, HEAD_D), F32)],
        scratch_shapes=[pltpu.VMEM((HEADS, HEAD_D, HEAD_D), F32)],
        input_output_aliases={4: 0},
        compiler_params=_seq_params(),
    )(p, p, logits, gw, y2)


def _out_fused(y2, wo, x, tgt, wpost, tm):
    t = x.shape[0]
    inv_d = 1.0 / D_MODEL

    def body(y_ref, wo_ref, x_ref, t_ref, w_ref, dout_ref, dy_ref, sq_ref, dw_ref):
        i = pl.program_id(0)

        @pl.when(i == 0)
        def _():
            sq_ref[...] = jnp.zeros_like(sq_ref)
            dw_ref[...] = jnp.zeros_like(dw_ref)

        y = (jnp.dot(y_ref[0], wo_ref[0:D_MODEL, :], preferred_element_type=F32)
             + jnp.dot(y_ref[1], wo_ref[D_MODEL:2 * D_MODEL, :], preferred_element_type=F32))
        w = w_ref[...]
        rstd = lax.rsqrt(jnp.mean(y * y, axis=-1, keepdims=True) + EPS)
        n = y * rstd
        err = (x_ref[...] + n * w) - t_ref[...]
        sq_ref[...] += jnp.sum(err * err, axis=0, keepdims=True)
        dout = err * inv_d
        dout_ref[...] = dout
        dw_ref[...] += jnp.sum(dout * n, axis=0, keepdims=True)
        dn = dout * w
        dy_ref[...] = (rstd * (dn - n * jnp.mean(dn * n, axis=-1, keepdims=True))).astype(_MXU_DTYPE)

    row = pl.BlockSpec((tm, D_MODEL), lambda i: (i, 0))
    return pl.pallas_call(
        body, name="out_fused", grid=(t // tm,),
        in_specs=[pl.BlockSpec((2, tm, D_MODEL), lambda i: (0, i, 0)), _full((2 * D_MODEL, D_MODEL)),
                  row, row, _full((1, D_MODEL))],
        out_specs=[row, row, _full((1, D_MODEL)), _full((1, D_MODEL))],
        out_shape=[jax.ShapeDtypeStruct((t, D_MODEL), F32), jax.ShapeDtypeStruct((t, D_MODEL), _MXU_DTYPE),
                   jax.ShapeDtypeStruct((1, D_MODEL), F32), jax.ShapeDtypeStruct((1, D_MODEL), F32)],
        compiler_params=_seq_params(),
    )(y2, wo, x, tgt, wpost)


def _dw_out(y2, dy, tt):
    t = dy.shape[0]

    def body(y_ref, dy_ref, g_ref):
        @pl.when(pl.program_id(1) == 0)
        def _():
            g_ref[...] = jnp.zeros_like(g_ref)

        g_ref[...] += _dot_tn(y_ref[...], dy_ref[...])

    return pl.pallas_call(
        body, name="dw_out", grid=(N_CHIPS, t // tt),
        in_specs=[pl.BlockSpec((None, tt, UNIT), lambda c, k: (c // 2, k, c % 2)),
                  pl.BlockSpec((tt, D_MODEL), lambda c, k: (k, 0))],
        out_specs=pl.BlockSpec((None, UNIT, D_MODEL), lambda c, k: (c, 0, 0)),
        out_shape=jax.ShapeDtypeStruct((N_CHIPS, UNIT, D_MODEL), F32),
        compiler_params=pltpu.CompilerParams(dimension_semantics=("arbitrary", "arbitrary")),
    )(y2, dy)


def _lru_bwd(dy, wo, p, h, wa, wx, ba, bx, lam, cw, cb, tm):
    t = dy.shape[0]
    nt = t // tm
    ng = tm // 8

    def body(dy_ref, wo_ref, lx_ref, halo_ref, lg_ref, h_ref, hhalo_ref, wa_ref, wx_ref, ba_ref, bx_ref,
             lam_ref, cw_ref, cb_ref,
             dp_ref, dwa_ref, dwx_ref, dba_ref, dbx_ref, dsp_ref, dcw_ref, dcb_ref,
             c_s, in_s, dh_s, dhc_s, afirst_s, dxc_s):
        i = pl.program_id(0)
        first = i == nt - 1

        @pl.when(i == 0)
        def _():
            for ref in (dwa_ref, dwx_ref, dba_ref, dbx_ref, dsp_ref, dcw_ref, dcb_ref, dhc_s, afirst_s, dxc_s):
                ref[...] = jnp.zeros_like(ref)

        sp = _softplus_neg(lam_ref[...])
        cwv = cw_ref[...]
        wav = wa_ref[...]
        wxv = wx_ref[...]
        halo = jnp.where(first, 0.0, halo_ref[...])
        xs, xc, r, ig, a, mult = _lru_gates(lx_ref[...], halo, cwv, cb_ref[...], wav, wxv,
                                            ba_ref[...], bx_ref[...], sp)
        d_y = _dot_nt(dy_ref[...], wo_ref[...])
        lg = lg_ref[...]
        sl = _sigmoid(lg)
        hv = h_ref[...]
        d_lg = d_y * hv * (sl * (1.0 + lg * (1.0 - sl)))

        c_s[...] = _shift_up(a, afirst_s[...], 1)
        in_s[...] = d_y * (lg * sl)
        row8 = _rows((8, D_MODEL))

        def group(gg, carry):
            rows = pl.ds(pl.multiple_of((ng - 1 - gg) * 8, 8), 8)
            cv = c_s[rows, :]
            uv = in_s[rows, :]
            for s in (1, 2, 4):
                c_sh = jnp.where(row8 < 8 - s, pltpu.roll(cv, 8 - s, 0), 1.0)
                u_sh = jnp.where(row8 < 8 - s, pltpu.roll(uv, 8 - s, 0), 0.0)
                uv = cv * u_sh + uv
                cv = cv * c_sh
            hh = cv * carry + uv
            dh_s[rows, :] = hh
            return jnp.broadcast_to(hh[0:1, :], (8, D_MODEL))

        dhc_s[...] = lax.fori_loop(0, ng, group, dhc_s[...])
        afirst_s[...] = a[0:8]
        dh = dh_s[...]

        h_prev = _shift_down(hv, jnp.where(first, 0.0, hhalo_ref[...]), 1)
        gx = ig * xc
        d_mult = dh * gx
        d_i = dh * (mult * xc)
        d_xc = dh * (mult * ig)
        d_la = (dh * h_prev) * a - d_mult * (a * a) / mult
        dsp_ref[...] += jnp.sum(d_la * r, axis=0, keepdims=True) * (-LRU_C)
        d_zr = (d_la * (-LRU_C * sp)) * (r * (1.0 - r))
        d_zi = d_i * (ig * (1.0 - ig))
        dba_ref[...] += jnp.sum(d_zr, axis=0, keepdims=True)
        dbx_ref[...] += jnp.sum(d_zi, axis=0, keepdims=True)
        back = []
        for n in range(LRU_BLOCKS):
            ln = slice(n * LRU_BW, (n + 1) * LRU_BW)
            xb = _mx(xc[:, ln])
            zr_n = _mx(d_zr[:, ln])
            zi_n = _mx(d_zi[:, ln])
            dwa_ref[n] += _dot_tn(xb, zr_n)
            dwx_ref[n] += _dot_tn(xb, zi_n)
            back.append(_dot_nt(zr_n, wav[n]) + _dot_nt(zi_n, wxv[n]))
        d_xc = d_xc + jnp.concatenate(back, axis=1)
        dcb_ref[...] += jnp.sum(d_xc, axis=0, keepdims=True)
        for k in range(4):
            dcw_ref[k:k + 1, :] += jnp.sum(d_xc * xs[k], axis=0, keepdims=True)
        head = dxc_s[...]
        d_lx = cwv[3:4] * d_xc
        for k in range(3):
            d_lx = d_lx + cwv[k:k + 1] * _shift_up(d_xc, head, 3 - k)
        dxc_s[...] = d_xc[0:8]
        dp_ref[0] = d_lx.astype(_MXU_DTYPE)
        dp_ref[1] = d_lg.astype(_MXU_DTYPE)

    rev = lambda i: nt - 1 - i
    seg = lambda s: pl.BlockSpec((None, tm, D_MODEL), lambda i: (s, rev(i), 0))
    halo_row = lambda i: jnp.maximum(rev(i) * (tm // 8) - 1, 0)
    row = pl.BlockSpec((tm, D_MODEL), lambda i: (rev(i), 0))
    vec = _full((1, D_MODEL))
    wblk = _full((LRU_BLOCKS, LRU_BW, LRU_BW))
    return pl.pallas_call(
        body, name="lru_bwd", grid=(nt,),
        in_specs=[row, pl.BlockSpec((D_MODEL, D_MODEL), lambda i: (0, 0)),
                  seg(0), pl.BlockSpec((None, 8, D_MODEL), lambda i: (0, halo_row(i), 0)), seg(1),
                  row, pl.BlockSpec((8, D_MODEL), lambda i: (halo_row(i), 0)),
                  wblk, wblk, vec, vec, vec, _full((4, D_MODEL)), vec],
        out_specs=[pl.BlockSpec((2, tm, D_MODEL), lambda i: (2, rev(i), 0)),
                   wblk, wblk, vec, vec, vec, _full((4, D_MODEL)), vec],
        out_shape=[jax.ShapeDtypeStruct((N_SEG, t, D_MODEL), _MXU_DTYPE),
                   jax.ShapeDtypeStruct((LRU_BLOCKS, LRU_BW, LRU_BW), F32),
                   jax.ShapeDtypeStruct((LRU_BLOCKS, LRU_BW, LRU_BW), F32),
                   jax.ShapeDtypeStruct((1, D_MODEL), F32), jax.ShapeDtypeStruct((1, D_MODEL), F32),
                   jax.ShapeDtypeStruct((1, D_MODEL), F32), jax.ShapeDtypeStruct((4, D_MODEL), F32),
                   jax.ShapeDtypeStruct((1, D_MODEL), F32)],
        scratch_shapes=[pltpu.VMEM((tm, D_MODEL), F32), pltpu.VMEM((tm, D_MODEL), F32),
                        pltpu.VMEM((tm, D_MODEL), F32), pltpu.VMEM((8, D_MODEL), F32),
                        pltpu.VMEM((8, D_MODEL), F32), pltpu.VMEM((8, D_MODEL), F32)],
        compiler_params=_seq_params(),
    )(dy, wo, p, p, p, h, h, wa, wx, ba, bx, lam, cw, cb)


def _hgrn_bwd(dy, wo, p, o, states, logits, gw, dp, tm):
    t = dy.shape[0]
    nt = t // tm
    nc = tm // CHUNK

    def body(dy_ref, wo_ref, qf_ref, vh_ref, o_ref, st_ref, lg_ref, gw_ref, dp_in,
             dp_ref, dgw_ref, dlb_ref, dyh_s, dstate, snext):
        del dp_in
        i = pl.program_id(0)

        @pl.when(i == 0)
        def _():
            for ref in (dgw_ref, dlb_ref, dstate, snext):
                ref[...] = jnp.zeros_like(ref)

        dyh_s[...] = _dot_nt(dy_ref[...], wo_ref[...])
        l0 = lg_ref[0:1, :]
        l1 = lg_ref[1:2, :]
        lb = 1.0 / (1.0 + jnp.exp(l1 - l0))
        gwv = gw_ref[...]
        rr = _rows((CHUNK, CHUNK))
        cc = _cols((CHUNK, CHUNK))
        causal = rr >= cc
        tri = causal.astype(BF16)
        tri_up = (rr <= cc).astype(BF16)

        def chunk(c2, carry):
            c = nc - 1 - c2
            rows = pl.ds(pl.multiple_of(c * CHUNK, CHUNK), CHUNK)
            q = qf_ref[0, rows, :]
            fr = qf_ref[1, rows, :]
            v = vh_ref[0, rows, :]
            hg = vh_ref[1, rows, :]
            ov = o_ref[rows, :]
            dyh = dyh_s[rows, :]
            sig, f, k, sq, qs, eb, em1, em2, ee, ebc = _hgrn_prep(q, fr, lb, tri)
            qi, qm, km, ke = _mx(qs * eb), _mx(qs * em1), _mx(k * em2), _mx(k * ee)
            sg = _sigmoid(hg)
            gate = hg * sg
            d_on = dyh * gate
            dqs_p, dk_p, dv_p, dhg_p, db_p, ex_p = [], [], [], [], [], []
            dgw_acc = []
            for h in range(HEADS):
                ln = slice(h * HEAD_D, (h + 1) * HEAD_D)
                o_h = ov[:, ln]
                rstd = lax.rsqrt(jnp.mean(o_h * o_h, axis=-1, keepdims=True) + EPS)
                n = o_h * rstd
                dhg_p.append(dyh[:, ln] * (n * gwv[:, ln]) * (sg[:, ln] * (1.0 + hg[:, ln] * (1.0 - sg[:, ln]))))
                dgw_acc.append(jnp.sum(d_on[:, ln] * n, axis=0, keepdims=True))
                dn = d_on[:, ln] * gwv[:, ln]
                d_o = rstd * (dn - n * jnp.mean(dn * n, axis=-1, keepdims=True))
                s0 = st_ref[c, h]
                ds = dstate[h]
                att = jnp.where(causal, _dot_nt(qm[:, ln], km[:, ln]), 0.0)
                d_att = jnp.where(causal, _dot_nt(d_o, v[:, ln]), 0.0)
                dv_p.append(_dot_tn(att, d_o) + _dot_nt(ke[:, ln], ds))
                d_qm = _dot(d_att, km[:, ln])
                d_km = _dot_tn(d_att, qm[:, ln])
                d_qi = _dot(d_o, s0)
                d_ke = _dot(v[:, ln], ds)
                dqs_h = d_qi * eb[:, ln] + d_qm * em1[:, ln]
                dk_h = d_km * em2[:, ln] + d_ke * ee[:, ln]
                dqs_p.append(dqs_h)
                dk_p.append(dk_h)
                db_p.append((qi[:, ln].astype(F32) * d_qi + qm[:, ln].astype(F32) * d_qm)
                            - (km[:, ln].astype(F32) * d_km + ke[:, ln].astype(F32) * d_ke))
                ex_p.append(jnp.sum(ds * snext[h], axis=0, keepdims=True))
                dstate[h] = ds * ebc[:, ln] + _dot_tn(d_o, qi[:, ln])
                snext[h] = s0
            cat = lambda parts: jnp.concatenate(parts, axis=1)
            dqs, dk, db = cat(dqs_p), cat(dk_p), cat(db_p)
            dg = _tri_sum(tri_up, db) + cat(ex_p)
            df = dg / f - dk
            dgw_ref[...] += cat(dgw_acc)
            dlb_ref[...] += jnp.sum(df * (1.0 - sig), axis=0, keepdims=True)
            dp_ref[0, rows, :] = (dqs * (sq * (1.0 + q * (1.0 - sq)))).astype(_MXU_DTYPE)
            dp_ref[1, rows, :] = (df * ((1.0 - lb) * sig * (1.0 - sig))).astype(_MXU_DTYPE)
            dp_ref[2, rows, :] = cat(dv_p).astype(_MXU_DTYPE)
            dp_ref[3, rows, :] = cat(dhg_p).astype(_MXU_DTYPE)
            return carry

        lax.fori_loop(0, nc, chunk, 0)

    rev = lambda i: nt - 1 - i
    pair = lambda s: pl.BlockSpec((2, tm, D_MODEL), lambda i: (s, rev(i), 0))
    row = pl.BlockSpec((tm, D_MODEL), lambda i: (rev(i), 0))
    vec = _full((1, D_MODEL))
    return pl.pallas_call(
        body, name="hgrn_bwd", grid=(nt,),
        in_specs=[row, pl.BlockSpec((D_MODEL, D_MODEL), lambda i: (1, 0)), pair(1), pair(2), row,
                  pl.BlockSpec((nc, HEADS, HEAD_D, HEAD_D), lambda i: (rev(i), 0, 0, 0)),
                  _full((2, D_MODEL)), vec, pl.BlockSpec(memory_space=pl.ANY)],
        out_specs=[pl.BlockSpec((4, tm, D_MODEL), lambda i: (0, rev(i), 0)), vec, vec],
        out_shape=[jax.ShapeDtypeStruct((N_SEG, t, D_MODEL), _MXU_DTYPE),
                   jax.ShapeDtypeStruct((1, D_MODEL), F32), jax.ShapeDtypeStruct((1, D_MODEL), F32)],
        scratch_shapes=[pltpu.VMEM((tm, D_MODEL), F32), pltpu.VMEM((HEADS, HEAD_D, HEAD_D), F32),
                        pltpu.VMEM((HEADS, HEAD_D, HEAD_D), F32)],
        input_output_aliases={8: 0},
        compiler_params=_seq_params(),
    )(dy, wo, p, p, o, states, logits, gw, dp)


def _dp_unit(m):
    return ((m // 2 + 2) % N_SEG) * 2 + m % 2


def _dw_in(u, dp, tt):
    t = u.shape[0]

    def body(u_ref, dp_ref, g_ref):
        @pl.when(pl.program_id(1) == 0)
        def _():
            g_ref[...] = jnp.zeros_like(g_ref)

        g_ref[...] += _dot_tn(u_ref[...], dp_ref[...])

    return pl.pallas_call(
        body, name="dw_in", grid=(N_UNIT, t // tt),
        in_specs=[pl.BlockSpec((tt, D_MODEL), lambda m, k: (k, 0)),
                  pl.BlockSpec((None, tt, UNIT), lambda m, k: (m // 2, k, m % 2))],
        out_specs=pl.BlockSpec((None, D_MODEL, UNIT), lambda m, k: (_dp_unit(m), 0, 0)),
        out_shape=jax.ShapeDtypeStruct((N_UNIT, D_MODEL, UNIT), F32),
        compiler_params=pltpu.CompilerParams(dimension_semantics=("arbitrary", "arbitrary")),
    )(u, dp)


def _dx(dp, wg, x, dout, wpre, tm):
    t = x.shape[0]

    def body(dp_ref, w_ref, x_ref, dout_ref, wpre_ref, gx_ref, dw_ref, acc):
        i = pl.program_id(0)
        m = pl.program_id(1)

        @pl.when(jnp.logical_and(i == 0, m == 0))
        def _():
            dw_ref[...] = jnp.zeros_like(dw_ref)

        @pl.when(m == 0)
        def _():
            acc[...] = jnp.zeros_like(acc)

        acc[...] += _dot_nt(dp_ref[...], w_ref[...])

        @pl.when(m == N_UNIT - 1)
        def _():
            du = acc[...]
            xv = x_ref[...]
            rstd = lax.rsqrt(jnp.mean(xv * xv, axis=-1, keepdims=True) + EPS)
            n = xv * rstd
            dw_ref[...] += jnp.sum(du * n, axis=0, keepdims=True)
            dn = du * wpre_ref[...]
            gx_ref[...] = dout_ref[...] + rstd * (dn - n * jnp.mean(dn * n, axis=-1, keepdims=True))

    row = pl.BlockSpec((tm, D_MODEL), lambda i, m: (i, 0))
    return pl.pallas_call(
        body, name="dx", grid=(t // tm, N_UNIT),
        in_specs=[pl.BlockSpec((None, tm, UNIT), lambda i, m: (m // 2, i, m % 2)),
                  pl.BlockSpec((None, D_MODEL, UNIT), lambda i, m: (_dp_unit(m), 0, 0)),
                  row, row, _full((1, D_MODEL))],
        out_specs=[row, _full((1, D_MODEL))],
        out_shape=[jax.ShapeDtypeStruct((t, D_MODEL), F32), jax.ShapeDtypeStruct((1, D_MODEL), F32)],
        scratch_shapes=[pltpu.VMEM((tm, D_MODEL), F32)],
        compiler_params=pltpu.CompilerParams(dimension_semantics=("arbitrary", "arbitrary")),
    )(dp, wg, x, dout, wpre)


def _chip_exchange(arrays, scatter, name):
    n = len(arrays)
    slab_shapes = [a.shape[1:] if sc else a.shape for a, sc in zip(arrays, scatter)]

    def body(*refs):
        ins, outs = refs[:n], refs[n:2 * n]
        send_sems, recv_sems, local_sems = refs[2 * n:]
        x, y, c = lax.axis_index("x"), lax.axis_index("y"), lax.axis_index("c")
        me = 2 * x + y
        peers = [(1 - x, y), (x, 1 - y), (1 - x, 1 - y)]

        def part(a, q):
            return ins[a].at[q] if scatter[a] else ins[a]

        def remote(a, j):
            px, py = peers[j]
            return pltpu.make_async_remote_copy(
                src_ref=part(a, 2 * px + py), dst_ref=outs[a].at[me],
                send_sem=send_sems.at[a * 3 + j], recv_sem=recv_sems.at[a * 3 + j],
                device_id=(px, py, c), device_id_type=MESH)

        def arrival(a, j):
            px, py = peers[j]
            return pltpu.make_async_remote_copy(
                src_ref=part(a, me), dst_ref=outs[a].at[2 * px + py],
                send_sem=send_sems.at[a * 3 + j], recv_sem=recv_sems.at[a * 3 + j],
                device_id=(px, py, c), device_id_type=MESH)

        local = [pltpu.make_async_copy(part(a, me), outs[a].at[me], local_sems.at[a]) for a in range(n)]
        for a in range(n):
            local[a].start()
            for j in range(3):
                remote(a, j).start()
        for a in range(n):
            for j in range(3):
                arrival(a, j).wait_recv()
        for a in range(n):
            for j in range(3):
                remote(a, j).wait_send()
            local[a].wait()

    any_spec = pl.BlockSpec(memory_space=pl.ANY)
    return pl.pallas_call(
        body, name=name,
        in_specs=[any_spec] * n, out_specs=[any_spec] * n,
        out_shape=[jax.ShapeDtypeStruct((N_CHIPS,) + tuple(s), a.dtype) for s, a in zip(slab_shapes, arrays)],
        scratch_shapes=[pltpu.SemaphoreType.DMA((3 * n,)), pltpu.SemaphoreType.DMA((3 * n,)),
                        pltpu.SemaphoreType.DMA((n,))],
    )(*arrays)


def _core_exchange(arrays, name):
    n = len(arrays)

    def body(*refs):
        ins, outs = refs[:n], refs[n:2 * n]
        send_sems, recv_sems, local_sems = refs[2 * n:]
        x, y, c = lax.axis_index("x"), lax.axis_index("y"), lax.axis_index("c")

        def remote(a):
            return pltpu.make_async_remote_copy(
                src_ref=ins[a], dst_ref=outs[a].at[c], send_sem=send_sems.at[a], recv_sem=recv_sems.at[a],
                device_id=(x, y, 1 - c), device_id_type=MESH)

        def arrival(a):
            return pltpu.make_async_remote_copy(
                src_ref=ins[a], dst_ref=outs[a].at[1 - c], send_sem=send_sems.at[a], recv_sem=recv_sems.at[a],
                device_id=(x, y, 1 - c), device_id_type=MESH)

        local = [pltpu.make_async_copy(ins[a], outs[a].at[c], local_sems.at[a]) for a in range(n)]
        for a in range(n):
            local[a].start()
            remote(a).start()
        for a in range(n):
            arrival(a).wait_recv()
        for a in range(n):
            remote(a).wait_send()
            local[a].wait()

    any_spec = pl.BlockSpec(memory_space=pl.ANY)
    return pl.pallas_call(
        body, name=name,
        in_specs=[any_spec] * n, out_specs=[any_spec] * n,
        out_shape=[jax.ShapeDtypeStruct((2,) + a.shape, a.dtype) for a in arrays],
        scratch_shapes=[pltpu.SemaphoreType.DMA((n,)), pltpu.SemaphoreType.DMA((n,)),
                        pltpu.SemaphoreType.DMA((n,))],
    )(*arrays)


def _sum_slots(r, name, tr=512):
    s, rows, cols = r.shape
    tr = min(tr, rows)

    def body(r_ref, o_ref):
        acc = r_ref[0]
        for j in range(1, s):
            acc = acc + r_ref[j]
        o_ref[...] = acc

    return pl.pallas_call(
        body, name=name, grid=(rows // tr,),
        in_specs=[pl.BlockSpec((s, tr, cols), lambda i: (0, i, 0))],
        out_specs=pl.BlockSpec((tr, cols), lambda i: (i, 0)),
        out_shape=jax.ShapeDtypeStruct((rows, cols), F32),
        compiler_params=pltpu.CompilerParams(dimension_semantics=("parallel",)),
    )(r)


def _adamw_big(q, w, m, v, name, grid, q_spec, w_spec):
    def body(q_ref, w_ref, m_ref, v_ref, g_ref, d_ref, nm_ref, nv_ref):
        g = q_ref[0] + q_ref[1]
        g_ref[...] = g
        d_ref[...], nm_ref[...], nv_ref[...] = _adamw(w_ref[...], g, m_ref[...], v_ref[...])

    shp = jax.ShapeDtypeStruct(w.shape, F32)
    return pl.pallas_call(
        body, name=name, grid=grid,
        in_specs=[q_spec, w_spec, w_spec, w_spec], out_specs=[w_spec] * 4, out_shape=[shp] * 4,
        compiler_params=pltpu.CompilerParams(dimension_semantics=("parallel",) * len(grid)),
    )(q, w, m, v)


def _small_finish(q, lam, logits):
    def body(q_ref, lam_ref, lg_ref, o_ref):
        g = q_ref[0] + q_ref[1]
        o_ref[...] = g
        o_ref[8:9, :] = g[8:9, :] * (-_sigmoid(-lam_ref[...]))
        lb = 1.0 / (1.0 + jnp.exp(lg_ref[1:2, :] - lg_ref[0:1, :]))
        d0 = g[9:10, :] * (lb * (1.0 - lb))
        o_ref[9:10, :] = d0
        o_ref[10:11, :] = -d0

    return pl.pallas_call(
        body, name="small_finish",
        in_specs=[_full((2, SMALL_ROWS, D_MODEL)), _full((1, D_MODEL)), _full((2, D_MODEL))],
        out_specs=_full((SMALL_ROWS, D_MODEL)),
        out_shape=jax.ShapeDtypeStruct((SMALL_ROWS, D_MODEL), F32),
    )(q, lam, logits)


def _adamw_small(ws, gs, ms, vs):
    n = len(ws)

    def body(*refs):
        w_r, g_r, m_r, v_r = refs[:n], refs[n:2 * n], refs[2 * n:3 * n], refs[3 * n:4 * n]
        d_o, m_o, v_o = refs[4 * n:5 * n], refs[5 * n:6 * n], refs[6 * n:7 * n]
        for j in range(n):
            d_o[j][...], m_o[j][...], v_o[j][...] = _adamw(w_r[j][...], g_r[j][...], m_r[j][...], v_r[j][...])

    specs = [_full(w.shape) for w in ws]
    shapes = [jax.ShapeDtypeStruct(w.shape, F32) for w in ws]
    outs = pl.pallas_call(
        body, name="adamw_small",
        in_specs=specs * 4, out_specs=specs * 3, out_shape=shapes * 3,
    )(*ws, *gs, *ms, *vs)
    return outs[:n], outs[n:2 * n], outs[2 * n:]


def _local_step(x, tgt, pre_w, wg, cw, cb, wa, wx, ba, bx, lam, logits, gw, wo, post_w, tm, tmm):
    u = _rmsnorm_pre(x, pre_w, tmm)
    p = _in_proj(u, wg, tmm)
    h, y2 = _lru_fwd(p, wa, wx, ba, bx, lam, cw, cb, tm)
    y2, o, states = _hgrn_fwd(p, logits, gw, y2, tm)
    dout, dy, sq, d_post = _out_fused(y2, wo, x, tgt, post_w, tmm)
    g_out = _dw_out(y2, dy, tmm)
    dp, d_wa, d_wx, d_ba, d_bx, d_sp, d_cw, d_cb = _lru_bwd(dy, wo, p, h, wa, wx, ba, bx, lam, cw, cb, tm)
    dp, d_gw, d_lb = _hgrn_bwd(dy, wo, p, o, states, logits, gw, dp, tm)
    g_in = _dw_in(u, dp, tmm)
    grad_x, d_pre = _dx(dp, wg, x, dout, pre_w, tmm)
    small = jnp.concatenate([d_pre, d_cw, d_cb, d_ba, d_bx, d_sp, d_lb,
                             jnp.zeros((1, D_MODEL), F32), d_gw, d_post,
                             jnp.zeros((SMALL_ROWS - 13, D_MODEL), F32)], axis=0)
    return sq, grad_x, g_in, g_out, d_wa, d_wx, small


def kernel(x, pre_norm_w, w_in, conv_w, conv_b, lru_w_a, lru_b_a, lru_w_x, lru_b_x, lru_lambda, hgrn_lb_logits, hgrn_gnorm_w, w_out, post_norm_w, loss_target, m_pre_norm_w, m_w_in, m_conv_w, m_conv_b, m_lru_w_a, m_lru_b_a, m_lru_w_x, m_lru_b_x, m_lru_lambda, m_hgrn_lb_logits, m_hgrn_gnorm_w, m_w_out, m_post_norm_w, v_pre_norm_w, v_w_in, v_conv_w, v_conv_b, v_lru_w_a, v_lru_b_a, v_lru_w_x, v_lru_b_x, v_lru_lambda, v_hgrn_lb_logits, v_hgrn_gnorm_w, v_w_out, v_post_norm_w):
    t = x.shape[1]
    tm = min(256, t)
    tmm = min(512, t)
    chip = 2 * lax.axis_index("x") + lax.axis_index("y")

    win_units = w_in[0].astype(_MXU_DTYPE).reshape(D_MODEL, 3, UNIT).transpose(1, 0, 2)
    small_w = jnp.concatenate([conv_w[0], lru_b_a[0], lru_b_x[0]], axis=1)
    wg4, wo4, wa4, wx4, sw4 = _chip_exchange(
        [win_units, w_out[0].astype(_MXU_DTYPE), lru_w_a[0].astype(_MXU_DTYPE), lru_w_x[0].astype(_MXU_DTYPE), small_w],
        [False] * 5, "gather_weights")
    wg = wg4.reshape(N_UNIT, D_MODEL, UNIT)
    wo = wo4.reshape(2 * D_MODEL, D_MODEL)
    wa = wa4.transpose(1, 0, 2, 3).reshape(LRU_BLOCKS, LRU_BW, LRU_BW)
    wx = wx4.transpose(1, 0, 2, 3).reshape(LRU_BLOCKS, LRU_BW, LRU_BW)
    cw = sw4[:, :, 0:256].transpose(1, 0, 2).reshape(4, D_MODEL)
    ba = sw4[:, :, 256:320].transpose(1, 0, 2).reshape(1, D_MODEL)
    bx = sw4[:, :, 320:384].transpose(1, 0, 2).reshape(1, D_MODEL)

    sq, grad_x, g_in, g_out, d_wa, d_wx, small = _local_step(
        x[0], loss_target[0], pre_norm_w, wg, cw, conv_b, wa, wx, ba, bx, lru_lambda, hgrn_lb_logits,
        hgrn_gnorm_w, wo, post_norm_w, tm, tmm)
    loss = lax.psum(jnp.sum(sq) * (0.5 / D_MODEL), ("x", "y", "c"))

    to_chips = lambda g: g.reshape(LRU_BLOCKS, N_CHIPS, 64, LRU_BW).transpose(1, 0, 2, 3)
    r_in, r_out, r_wa, r_wx, r_small = _chip_exchange(
        [g_in.reshape(N_CHIPS, 3, D_MODEL, UNIT), g_out, to_chips(d_wa), to_chips(d_wx), small],
        [True, True, True, True, False], "reduce_chips")
    s_in = _sum_slots(r_in.reshape(N_CHIPS, 3 * D_MODEL, UNIT), "sum_in")
    s_out = _sum_slots(r_out, "sum_out")
    s_wa = _sum_slots(r_wa.reshape(N_CHIPS, LRU_BW, LRU_BW), "sum_wa")
    s_wx = _sum_slots(r_wx.reshape(N_CHIPS, LRU_BW, LRU_BW), "sum_wx")
    s_small = _sum_slots(r_small, "sum_small")
    q_in, q_out, q_wa, q_wx, q_small = _core_exchange([s_in, s_out, s_wa, s_wx, s_small], "reduce_cores")

    g_w_in, d_w_in, nm_w_in, nv_w_in = _adamw_big(
        q_in.reshape(2, 3, D_MODEL, UNIT), w_in[0], m_w_in[0], v_w_in[0], "adamw_w_in", (3, 2),
        pl.BlockSpec((2, None, 512, UNIT), lambda k, i: (0, k, i, 0)),
        pl.BlockSpec((512, UNIT), lambda k, i: (i, k)))
    g_w_out, d_w_out, nm_w_out, nv_w_out = _adamw_big(
        q_out, w_out[0], m_w_out[0], v_w_out[0], "adamw_w_out", (2,),
        pl.BlockSpec((2, 256, D_MODEL), lambda i: (0, i, 0)), pl.BlockSpec((256, D_MODEL), lambda i: (i, 0)))
    sq2 = lambda a: a.reshape(LRU_BW, LRU_BW)
    lru_specs = ((1,), pl.BlockSpec((2, LRU_BW, LRU_BW), lambda i: (0, 0, 0)),
                 pl.BlockSpec((LRU_BW, LRU_BW), lambda i: (0, 0)))
    g_wa, d_wa2, nm_wa, nv_wa = _adamw_big(q_wa, sq2(lru_w_a), sq2(m_lru_w_a), sq2(v_lru_w_a), "adamw_wa", *lru_specs)
    g_wx, d_wx2, nm_wx, nv_wx = _adamw_big(q_wx, sq2(lru_w_x), sq2(m_lru_w_x), sq2(v_lru_w_x), "adamw_wx", *lru_specs)

    gs = _small_finish(q_small, lru_lambda, hgrn_lb_logits)
    g_pre = gs[0:1]
    g_cw = lax.dynamic_slice(gs[1:5], (0, chip * 256), (4, 256))
    g_cb = gs[5:6]
    g_ba = lax.dynamic_slice(gs[6].reshape(LRU_BLOCKS, N_CHIPS, 64), (0, chip, 0), (LRU_BLOCKS, 1, 64)).reshape(4, 64)
    g_bx = lax.dynamic_slice(gs[7].reshape(LRU_BLOCKS, N_CHIPS, 64), (0, chip, 0), (LRU_BLOCKS, 1, 64)).reshape(4, 64)
    g_lam = gs[8:9]
    g_lb = gs[9:11]
    g_gw = gs[11:12]
    g_post = gs[12:13]
    small_g = [g_pre, g_cw, g_cb, g_ba, g_bx, g_lam, g_lb, g_gw, g_post]
    two_d = lambda a: a.reshape(a.shape[-2:])
    small_w_list = [pre_norm_w, conv_w, conv_b, lru_b_a, lru_b_x, lru_lambda, hgrn_lb_logits, hgrn_gnorm_w, post_norm_w]
    small_m_list = [m_pre_norm_w, m_conv_w, m_conv_b, m_lru_b_a, m_lru_b_x, m_lru_lambda, m_hgrn_lb_logits,
                    m_hgrn_gnorm_w, m_post_norm_w]
    small_v_list = [v_pre_norm_w, v_conv_w, v_conv_b, v_lru_b_a, v_lru_b_x, v_lru_lambda, v_hgrn_lb_logits,
                    v_hgrn_gnorm_w, v_post_norm_w]
    sd, sm, sv = _adamw_small([two_d(a) for a in small_w_list], small_g,
                              [two_d(a) for a in small_m_list], [two_d(a) for a in small_v_list])

    def shaped(vals, refs):
        return [val.reshape(ref.shape) for val, ref in zip(vals, refs)]

    s_g = shaped(small_g, small_w_list)
    s_d = shaped(sd, small_w_list)
    s_m = shaped(sm, small_w_list)
    s_v = shaped(sv, small_w_list)

    def ordered(small, big_in, big_wa, big_wx, big_out):
        pre, cw_, cb_, ba_, bx_, lam_, lb_, gw_, post_ = small
        return [pre, big_in.reshape(w_in.shape), cw_, cb_, big_wa.reshape(lru_w_a.shape), ba_,
                big_wx.reshape(lru_w_x.shape), bx_, lam_, lb_, gw_, big_out.reshape(w_out.shape), post_]

    grads = ordered(s_g, g_w_in, g_wa, g_wx, g_w_out)
    deltas = ordered(s_d, d_w_in, d_wa2, d_wx2, d_w_out)
    new_m = ordered(s_m, nm_w_in, nm_wa, nm_wx, nm_w_out)
    new_v = ordered(s_v, nv_w_in, nv_wa, nv_wx, nv_w_out)
    return (loss, grad_x.reshape(x.shape), *grads, *deltas, *new_m, *new_v)
```

```python
import functools

import jax
import jax.numpy as jnp
from jax import lax
from jax.experimental import pallas as pl
from jax.experimental.pallas import tpu as pltpu

F32 = jnp.float32
BF16 = jnp.bfloat16
_MXU_DTYPE = jnp.bfloat16
_WIRE_DTYPE = jnp.bfloat16

D_MODEL = 1024
N_SEG = 6
UNIT = 512
N_UNIT = 12
LRU_BLOCKS = 4
LRU_BW = 256
LRU_C = 8.0
HEADS = 8
HEAD_D = 128
CHUNK = 64
EPS = 1e-6
N_CHIPS = 4
EXP_CLAMP = 80.0

ADAM_LR = 0.001
ADAM_B1 = 0.9
ADAM_B2 = 0.999
ADAM_EPS = 1e-08
ADAM_WD = 0.01
ADAM_STEP = 10

SMALL_ROWS = 16
MESH = pl.DeviceIdType.MESH


def _sigmoid(x):
    return 1.0 / (1.0 + jnp.exp(-x))


def _mx(x):
    return x.astype(_MXU_DTYPE)


def _dot(a, b):
    return jnp.dot(_mx(a), _mx(b), preferred_element_type=F32)


def _dot_nt(a, b):
    return lax.dot_general(_mx(a), _mx(b), (((1,), (1,)), ((), ())), preferred_element_type=F32)


def _dot_tn(a, b):
    return lax.dot_general(_mx(a), _mx(b), (((0,), (0,)), ((), ())), preferred_element_type=F32)


def _rows(shape):
    return lax.broadcasted_iota(jnp.int32, shape, 0)


def _cols(shape):
    return lax.broadcasted_iota(jnp.int32, shape, 1)


def _softplus_neg(lam):
    z = -lam
    e = jnp.exp(-jnp.abs(z))
    series = e * (1.0 - e * (0.5 - e * (1.0 / 3.0 - 0.25 * e)))
    return jnp.maximum(z, 0.0) + jnp.where(e < 1e-2, series, jnp.log(1.0 + e))


def _neg_expm1(y):
    series = -y * (1.0 + y * (0.5 + y * (1.0 / 6.0 + y * (1.0 / 24.0))))
    return jnp.where(y > -0.03, series, 1.0 - jnp.exp(y))


def _split3(x):
    hi = x.astype(BF16)
    r1 = x - hi.astype(F32)
    mid = r1.astype(BF16)
    lo = (r1 - mid.astype(F32)).astype(BF16)
    return hi, mid, lo


def _tri_sum(tri, x):
    hi, mid, lo = _split3(x)
    d = lambda p: jnp.dot(tri, p, preferred_element_type=F32)
    return d(hi) + d(mid) + d(lo)


def _shift_down(x, halo, s):
    if s == 0:
        return x
    r = pltpu.roll(x, s, 0)
    top = jnp.where(_rows(halo.shape) < s, pltpu.roll(halo, s, 0), r[0:8])
    return jnp.concatenate([top, r[8:]], axis=0)


def _shift_up(x, head, s):
    if s == 0:
        return x
    n = x.shape[0]
    r = pltpu.roll(x, n - s, 0)
    bottom = jnp.where(_rows(head.shape) >= 8 - s, pltpu.roll(head, 8 - s, 0), r[n - 8:n])
    return jnp.concatenate([r[:n - 8], bottom], axis=0)


def _lru_gates(lx, halo, cw, cb, wa, wx, ba, bx, sp):
    xs = [_shift_down(lx, halo, 3 - k) for k in range(4)]
    xc = cb + cw[0:1] * xs[0] + cw[1:2] * xs[1] + cw[2:3] * xs[2] + cw[3:4] * xs[3]
    zr, zi = [], []
    for n in range(LRU_BLOCKS):
        xb = _mx(xc[:, n * LRU_BW:(n + 1) * LRU_BW])
        zr.append(jnp.dot(xb, wa[n], preferred_element_type=F32))
        zi.append(jnp.dot(xb, wx[n], preferred_element_type=F32))
    r = _sigmoid(jnp.concatenate(zr, axis=1) + ba)
    ig = _sigmoid(jnp.concatenate(zi, axis=1) + bx)
    la = (-LRU_C * sp) * r
    a = jnp.exp(la)
    mult = jnp.sqrt(_neg_expm1(2.0 * la))
    return xs, xc, r, ig, a, mult


def _hgrn_prep(q, fr, lb, tri):
    sig = _sigmoid(fr)
    f = lb + (1.0 - lb) * sig
    k = 1.0 - f
    sq = _sigmoid(q)
    qs = q * sq
    b = _tri_sum(tri, jnp.log(f))
    bm = b[CHUNK // 2 - 1:CHUNK // 2]
    bc = b[CHUNK - 1:CHUNK]
    eb = jnp.exp(b)
    em1 = jnp.exp(jnp.minimum(b - bm, EXP_CLAMP))
    em2 = jnp.exp(jnp.minimum(bm - b, EXP_CLAMP))
    ee = jnp.exp(bc - b)
    ebc = jnp.exp(bc)
    return sig, f, k, sq, qs, eb, em1, em2, ee, ebc


def _adamw(w, g, m, v):
    m = ADAM_B1 * m + (1.0 - ADAM_B1) * g
    v = ADAM_B2 * v + (1.0 - ADAM_B2) * jnp.square(g)
    m_hat = m / (1.0 - ADAM_B1 ** ADAM_STEP)
    v_hat = v / (1.0 - ADAM_B2 ** ADAM_STEP)
    delta = -ADAM_LR * (m_hat / (jnp.sqrt(v_hat) + ADAM_EPS) + ADAM_WD * w)
    return delta, m, v


def _full(shape):
    nd = len(shape)
    return pl.BlockSpec(shape, lambda *_: (0,) * nd)


def _seq_params():
    return pltpu.CompilerParams(dimension_semantics=("arbitrary",))


def _rmsnorm_pre(x, w, tm):
    t = x.shape[0]

    def body(x_ref, w_ref, u_ref):
        xv = x_ref[...]
        rstd = lax.rsqrt(jnp.mean(xv * xv, axis=-1, keepdims=True) + EPS)
        u_ref[...] = (xv * rstd * w_ref[...]).astype(_MXU_DTYPE)

    return pl.pallas_call(
        body, name="rmsnorm_pre", grid=(t // tm,),
        in_specs=[pl.BlockSpec((tm, D_MODEL), lambda i: (i, 0)), _full((1, D_MODEL))],
        out_specs=pl.BlockSpec((tm, D_MODEL), lambda i: (i, 0)),
        out_shape=jax.ShapeDtypeStruct((t, D_MODEL), _MXU_DTYPE),
        compiler_params=pltpu.CompilerParams(dimension_semantics=("parallel",)),
    )(x, w)


def _in_proj(u, wg, tm):
    t = u.shape[0]

    def body(u_ref, w_ref, p_ref):
        p_ref[...] = jnp.dot(u_ref[...], w_ref[...], preferred_element_type=F32)

    return pl.pallas_call(
        body, name="in_proj", grid=(N_UNIT, t // tm),
        in_specs=[pl.BlockSpec((tm, D_MODEL), lambda m, i: (i, 0)),
                  pl.BlockSpec((None, D_MODEL, UNIT), lambda m, i: (m, 0, 0))],
        out_specs=pl.BlockSpec((None, tm, UNIT), lambda m, i: (m // 2, i, m % 2)),
        out_shape=jax.ShapeDtypeStruct((N_SEG, t, D_MODEL), F32),
        compiler_params=pltpu.CompilerParams(dimension_semantics=("arbitrary", "arbitrary")),
    )(u, wg)


def _lru_fwd(p, wa, wx, ba, bx, lam, cw, cb, tm):
    t = p.shape[1]
    ng = tm // 8

    def body(lx_ref, halo_ref, lg_ref, wa_ref, wx_ref, ba_ref, bx_ref, lam_ref, cw_ref, cb_ref,
             h_ref, y_ref, a_s, u_s, hc_s):
        i = pl.program_id(0)

        @pl.when(i == 0)
        def _():
            hc_s[...] = jnp.zeros_like(hc_s)

        sp = _softplus_neg(lam_ref[...])
        halo = jnp.where(i == 0, 0.0, halo_ref[...])
        _, xc, _, ig, a, mult = _lru_gates(lx_ref[...], halo, cw_ref[...], cb_ref[...], wa_ref[...],
                                           wx_ref[...], ba_ref[...], bx_ref[...], sp)
        a_s[...] = a
        u_s[...] = mult * (ig * xc)
        row8 = _rows((8, D_MODEL))

        def group(g, hc):
            rows = pl.ds(pl.multiple_of(g * 8, 8), 8)
            av = a_s[rows, :]
            uv = u_s[rows, :]
            for s in (1, 2, 4):
                a_sh = jnp.where(row8 >= s, pltpu.roll(av, s, 0), 1.0)
                u_sh = jnp.where(row8 >= s, pltpu.roll(uv, s, 0), 0.0)
                uv = av * u_sh + uv
                av = av * a_sh
            hh = av * hc + uv
            h_ref[rows, :] = hh
            return jnp.broadcast_to(hh[7:8, :], (8, D_MODEL))

        hc_s[...] = lax.fori_loop(0, ng, group, hc_s[...])
        lg = lg_ref[...]
        y_ref[...] = (h_ref[...] * (lg * _sigmoid(lg))).astype(_MXU_DTYPE)

    seg = lambda s: pl.BlockSpec((None, tm, D_MODEL), lambda i: (s, i, 0))
    return pl.pallas_call(
        body, name="lru_fwd", grid=(t // tm,),
        in_specs=[seg(0),
                  pl.BlockSpec((None, 8, D_MODEL), lambda i: (0, jnp.maximum(i * (tm // 8) - 1, 0), 0)),
                  seg(1),
                  _full((LRU_BLOCKS, LRU_BW, LRU_BW)), _full((LRU_BLOCKS, LRU_BW, LRU_BW)),
                  _full((1, D_MODEL)), _full((1, D_MODEL)), _full((1, D_MODEL)),
                  _full((4, D_MODEL)), _full((1, D_MODEL))],
        out_specs=[pl.BlockSpec((tm, D_MODEL), lambda i: (i, 0)),
                   pl.BlockSpec((None, tm, D_MODEL), lambda i: (0, i, 0))],
        out_shape=[jax.ShapeDtypeStruct((t, D_MODEL), F32),
                   jax.ShapeDtypeStruct((2, t, D_MODEL), _MXU_DTYPE)],
        scratch_shapes=[pltpu.VMEM((tm, D_MODEL), F32), pltpu.VMEM((tm, D_MODEL), F32),
                        pltpu.VMEM((8, D_MODEL), F32)],
        compiler_params=_seq_params(),
    )(p, p, p, wa, wx, ba, bx, lam, cw, cb)


def _hgrn_fwd(p, logits, gw, y2, tm):
    t = p.shape[1]
    nc = tm // CHUNK

    def body(qf_ref, vh_ref, lg_ref, gw_ref, y_in, y_ref, o_ref, st_ref, state):
        del y_in
        i = pl.program_id(0)

        @pl.when(i == 0)
        def _():
            state[...] = jnp.zeros_like(state)

        l0 = lg_ref[0:1, :]
        l1 = lg_ref[1:2, :]
        lb = 1.0 / (1.0 + jnp.exp(l1 - l0))
        gwv = gw_ref[...]
        tri = (_rows((CHUNK, CHUNK)) >= _cols((CHUNK, CHUNK))).astype(BF16)
        causal = _rows((CHUNK, CHUNK)) >= _cols((CHUNK, CHUNK))

        def chunk(c, carry):
            rows = pl.ds(pl.multiple_of(c * CHUNK, CHUNK), CHUNK)
            q = qf_ref[0, rows, :]
            fr = qf_ref[1, rows, :]
            v = vh_ref[0, rows, :]
            hg = vh_ref[1, rows, :]
            _, _, k, _, qs, eb, em1, em2, ee, ebc = _hgrn_prep(q, fr, lb, tri)
            qi, qm, km, ke = qs * eb, qs * em1, k * em2, k * ee
            gate = hg * _sigmoid(hg)
            for h in range(HEADS):
                ln = slice(h * HEAD_D, (h + 1) * HEAD_D)
                s0 = state[h]
                st_ref[c, h] = s0
                att = jnp.where(causal, _dot_nt(qm[:, ln], km[:, ln]), 0.0)
                o = _dot_nt(qi[:, ln], s0) + _dot(att, v[:, ln])
                state[h] = s0 * ebc[:, ln] + _dot_tn(v[:, ln], ke[:, ln])
                rstd = lax.rsqrt(jnp.mean(o * o, axis=-1, keepdims=True) + EPS)
                o_ref[rows, ln] = o
                y_ref[rows, ln] = (o * rstd * gwv[:, ln] * gate[:, ln]).astype(_MXU_DTYPE)
            return carry

        lax.fori_loop(0, nc, chunk, 0)

    pair = lambda s: pl.BlockSpec((2, tm, D_MODEL), lambda i: (s, i, 0))
    return pl.pallas_call(
        body, name="hgrn_fwd", grid=(t // tm,),
        in_specs=[pair(1), pair(2), _full((2, D_MODEL)), _full((1, D_MODEL)),
                  pl.BlockSpec(memory_space=pl.ANY)],
        out_specs=[pl.BlockSpec((None, tm, D_MODEL), lambda i: (1, i, 0)),
                   pl.BlockSpec((tm, D_MODEL), lambda i: (i, 0)),
                   pl.BlockSpec((nc, HEADS, HEAD_D, HEAD_D), lambda i: (i, 0, 0, 0))],
        out_shape=[jax.ShapeDtypeStruct((2, t, D_MODEL), _MXU_DTYPE),
                   jax.ShapeDtypeStruct((t, D_MODEL), F32),
                   jax.ShapeDtypeStruct((t // CHUNK, HEADS, HEAD_D, HEAD_D), F32)],
        scratch_shapes=[pltpu.VMEM((HEADS, HEAD_D, HEAD_D), F32)],
        input_output_aliases={4: 0},
        compiler_params=_seq_params(),
    )(p, p, logits, gw, y2)


def _out_fused(y2, wo, x, tgt, wpost, tm):
    t = x.shape[0]
    inv_d = 1.0 / D_MODEL

    def body(y_ref, wo_ref, x_ref, t_ref, w_ref, dout_ref, dy_ref, sq_ref, dw_ref):
        i = pl.program_id(0)

        @pl.when(i == 0)
        def _():
            sq_ref[...] = jnp.zeros_like(sq_ref)
            dw_ref[...] = jnp.zeros_like(dw_ref)

        y = (jnp.dot(y_ref[0], wo_ref[0:D_MODEL, :], preferred_element_type=F32)
             + jnp.dot(y_ref[1], wo_ref[D_MODEL:2 * D_MODEL, :], preferred_element_type=F32))
        w = w_ref[...]
        rstd = lax.rsqrt(jnp.mean(y * y, axis=-1, keepdims=True) + EPS)
        n = y * rstd
        err = (x_ref[...] + n * w) - t_ref[...]
        sq_ref[...] += jnp.sum(err * err, axis=0, keepdims=True)
        dout = err * inv_d
        dout_ref[...] = dout
        dw_ref[...] += jnp.sum(dout * n, axis=0, keepdims=True)
        dn = dout * w
        dy_ref[...] = (rstd * (dn - n * jnp.mean(dn * n, axis=-1, keepdims=True))).astype(_MXU_DTYPE)

    row = pl.BlockSpec((tm, D_MODEL), lambda i: (i, 0))
    return pl.pallas_call(
        body, name="out_fused", grid=(t // tm,),
        in_specs=[pl.BlockSpec((2, tm, D_MODEL), lambda i: (0, i, 0)), _full((2 * D_MODEL, D_MODEL)),
                  row, row, _full((1, D_MODEL))],
        out_specs=[row, row, _full((1, D_MODEL)), _full((1, D_MODEL))],
        out_shape=[jax.ShapeDtypeStruct((t, D_MODEL), F32), jax.ShapeDtypeStruct((t, D_MODEL), _MXU_DTYPE),
                   jax.ShapeDtypeStruct((1, D_MODEL), F32), jax.ShapeDtypeStruct((1, D_MODEL), F32)],
        compiler_params=_seq_params(),
    )(y2, wo, x, tgt, wpost)


def _dw_out(y2, dy, tt):
    t = dy.shape[0]
    hr = UNIT // 2

    def body(y_ref, dy_ref, g_ref):
        @pl.when(pl.program_id(1) == 0)
        def _():
            g_ref[...] = jnp.zeros_like(g_ref)

        r = _dot_tn(y_ref[...], dy_ref[...])
        g_ref[0] += r[0:hr]
        g_ref[1] += r[hr:UNIT]

    return pl.pallas_call(
        body, name="dw_out", grid=(N_CHIPS, t // tt),
        in_specs=[pl.BlockSpec((None, tt, UNIT), lambda c, k: (c // 2, k, c % 2)),
                  pl.BlockSpec((tt, D_MODEL), lambda c, k: (k, 0))],
        out_specs=pl.BlockSpec((2, None, hr, D_MODEL), lambda c, k: (0, c, 0, 0)),
        out_shape=jax.ShapeDtypeStruct((2, N_CHIPS, hr, D_MODEL), F32),
        compiler_params=pltpu.CompilerParams(dimension_semantics=("arbitrary", "arbitrary")),
    )(y2, dy)


def _lru_bwd(dy, wo, p, h, wa, wx, ba, bx, lam, cw, cb, tm):
    t = dy.shape[0]
    nt = t // tm
    ng = tm // 8

    def body(dy_ref, wo_ref, lx_ref, halo_ref, lg_ref, h_ref, hhalo_ref, wa_ref, wx_ref, ba_ref, bx_ref,
             lam_ref, cw_ref, cb_ref,
             dp_ref, dwa_ref, dwx_ref, dba_ref, dbx_ref, dsp_ref, dcw_ref, dcb_ref,
             c_s, in_s, dh_s, dhc_s, afirst_s, dxc_s):
        i = pl.program_id(0)
        first = i == nt - 1

        @pl.when(i == 0)
        def _():
            for ref in (dwa_ref, dwx_ref, dba_ref, dbx_ref, dsp_ref, dcw_ref, dcb_ref, dhc_s, afirst_s, dxc_s):
                ref[...] = jnp.zeros_like(ref)

        sp = _softplus_neg(lam_ref[...])
        cwv = cw_ref[...]
        wav = wa_ref[...]
        wxv = wx_ref[...]
        halo = jnp.where(first, 0.0, halo_ref[...])
        xs, xc, r, ig, a, mult = _lru_gates(lx_ref[...], halo, cwv, cb_ref[...], wav, wxv,
                                            ba_ref[...], bx_ref[...], sp)
        d_y = _dot_nt(dy_ref[...], wo_ref[...])
        lg = lg_ref[...]
        sl = _sigmoid(lg)
        hv = h_ref[...]
        d_lg = d_y * hv * (sl * (1.0 + lg * (1.0 - sl)))

        c_s[...] = _shift_up(a, afirst_s[...], 1)
        in_s[...] = d_y * (lg * sl)
        row8 = _rows((8, D_MODEL))

        def group(gg, carry):
            rows = pl.ds(pl.multiple_of((ng - 1 - gg) * 8, 8), 8)
            cv = c_s[rows, :]
            uv = in_s[rows, :]
            for s in (1, 2, 4):
                c_sh = jnp.where(row8 < 8 - s, pltpu.roll(cv, 8 - s, 0), 1.0)
                u_sh = jnp.where(row8 < 8 - s, pltpu.roll(uv, 8 - s, 0), 0.0)
                uv = cv * u_sh + uv
                cv = cv * c_sh
            hh = cv * carry + uv
            dh_s[rows, :] = hh
            return jnp.broadcast_to(hh[0:1, :], (8, D_MODEL))

        dhc_s[...] = lax.fori_loop(0, ng, group, dhc_s[...])
        afirst_s[...] = a[0:8]
        dh = dh_s[...]

        h_prev = _shift_down(hv, jnp.where(first, 0.0, hhalo_ref[...]), 1)
        gx = ig * xc
        d_mult = dh * gx
        d_i = dh * (mult * xc)
        d_xc = dh * (mult * ig)
        d_la = (dh * h_prev) * a - d_mult * (a * a) / mult
        dsp_ref[...] += jnp.sum(d_la * r, axis=0, keepdims=True) * (-LRU_C)
        d_zr = (d_la * (-LRU_C * sp)) * (r * (1.0 - r))
        d_zi = d_i * (ig * (1.0 - ig))
        dba_ref[...] += jnp.sum(d_zr, axis=0, keepdims=True)
        dbx_ref[...] += jnp.sum(d_zi, axis=0, keepdims=True)
        back = []
        for n in range(LRU_BLOCKS):
            ln = slice(n * LRU_BW, (n + 1) * LRU_BW)
            xb = _mx(xc[:, ln])
            zr_n = _mx(d_zr[:, ln])
            zi_n = _mx(d_zi[:, ln])
            dwa_ref[n] += _dot_tn(xb, zr_n)
            dwx_ref[n] += _dot_tn(xb, zi_n)
            back.append(_dot_nt(zr_n, wav[n]) + _dot_nt(zi_n, wxv[n]))
        d_xc = d_xc + jnp.concatenate(back, axis=1)
        dcb_ref[...] += jnp.sum(d_xc, axis=0, keepdims=True)
        for k in range(4):
            dcw_ref[k:k + 1, :] += jnp.sum(d_xc * xs[k], axis=0, keepdims=True)
        head = dxc_s[...]
        d_lx = cwv[3:4] * d_xc
        for k in range(3):
            d_lx = d_lx + cwv[k:k + 1] * _shift_up(d_xc, head, 3 - k)
        dxc_s[...] = d_xc[0:8]
        dp_ref[0] = d_lx.astype(_MXU_DTYPE)
        dp_ref[1] = d_lg.astype(_MXU_DTYPE)

    rev = lambda i: nt - 1 - i
    seg = lambda s: pl.BlockSpec((None, tm, D_MODEL), lambda i: (s, rev(i), 0))
    halo_row = lambda i: jnp.maximum(rev(i) * (tm // 8) - 1, 0)
    row = pl.BlockSpec((tm, D_MODEL), lambda i: (rev(i), 0))
    vec = _full((1, D_MODEL))
    wblk = _full((LRU_BLOCKS, LRU_BW, LRU_BW))
    return pl.pallas_call(
        body, name="lru_bwd", grid=(nt,),
        in_specs=[row, pl.BlockSpec((D_MODEL, D_MODEL), lambda i: (0, 0)),
                  seg(0), pl.BlockSpec((None, 8, D_MODEL), lambda i: (0, halo_row(i), 0)), seg(1),
                  row, pl.BlockSpec((8, D_MODEL), lambda i: (halo_row(i), 0)),
                  wblk, wblk, vec, vec, vec, _full((4, D_MODEL)), vec],
        out_specs=[pl.BlockSpec((2, tm, D_MODEL), lambda i: (2, rev(i), 0)),
                   wblk, wblk, vec, vec, vec, _full((4, D_MODEL)), vec],
        out_shape=[jax.ShapeDtypeStruct((N_SEG, t, D_MODEL), _MXU_DTYPE),
                   jax.ShapeDtypeStruct((LRU_BLOCKS, LRU_BW, LRU_BW), F32),
                   jax.ShapeDtypeStruct((LRU_BLOCKS, LRU_BW, LRU_BW), F32),
                   jax.ShapeDtypeStruct((1, D_MODEL), F32), jax.ShapeDtypeStruct((1, D_MODEL), F32),
                   jax.ShapeDtypeStruct((1, D_MODEL), F32), jax.ShapeDtypeStruct((4, D_MODEL), F32),
                   jax.ShapeDtypeStruct((1, D_MODEL), F32)],
        scratch_shapes=[pltpu.VMEM((tm, D_MODEL), F32), pltpu.VMEM((tm, D_MODEL), F32),
                        pltpu.VMEM((tm, D_MODEL), F32), pltpu.VMEM((8, D_MODEL), F32),
                        pltpu.VMEM((8, D_MODEL), F32), pltpu.VMEM((8, D_MODEL), F32)],
        compiler_params=_seq_params(),
    )(dy, wo, p, p, p, h, h, wa, wx, ba, bx, lam, cw, cb)


def _hgrn_bwd(dy, wo, p, o, states, logits, gw, dp, tm):
    t = dy.shape[0]
    nt = t // tm
    nc = tm // CHUNK

    def body(dy_ref, wo_ref, qf_ref, vh_ref, o_ref, st_ref, lg_ref, gw_ref, dp_in,
             dp_ref, dgw_ref, dlb_ref, dyh_s, dstate, snext):
        del dp_in
        i = pl.program_id(0)

        @pl.when(i == 0)
        def _():
            for ref in (dgw_ref, dlb_ref, dstate, snext):
                ref[...] = jnp.zeros_like(ref)

        dyh_s[...] = _dot_nt(dy_ref[...], wo_ref[...])
        l0 = lg_ref[0:1, :]
        l1 = lg_ref[1:2, :]
        lb = 1.0 / (1.0 + jnp.exp(l1 - l0))
        gwv = gw_ref[...]
        rr = _rows((CHUNK, CHUNK))
        cc = _cols((CHUNK, CHUNK))
        causal = rr >= cc
        tri = causal.astype(BF16)
        tri_up = (rr <= cc).astype(BF16)

        def chunk(c2, carry):
            c = nc - 1 - c2
            rows = pl.ds(pl.multiple_of(c * CHUNK, CHUNK), CHUNK)
            q = qf_ref[0, rows, :]
            fr = qf_ref[1, rows, :]
            v = vh_ref[0, rows, :]
            hg = vh_ref[1, rows, :]
            ov = o_ref[rows, :]
            dyh = dyh_s[rows, :]
            sig, f, k, sq, qs, eb, em1, em2, ee, ebc = _hgrn_prep(q, fr, lb, tri)
            qi, qm, km, ke = _mx(qs * eb), _mx(qs * em1), _mx(k * em2), _mx(k * ee)
            sg = _sigmoid(hg)
            gate = hg * sg
            d_on = dyh * gate
            dqs_p, dk_p, dv_p, dhg_p, db_p, ex_p = [], [], [], [], [], []
            dgw_acc = []
            for h in range(HEADS):
                ln = slice(h * HEAD_D, (h + 1) * HEAD_D)
                o_h = ov[:, ln]
                rstd = lax.rsqrt(jnp.mean(o_h * o_h, axis=-1, keepdims=True) + EPS)
                n = o_h * rstd
                dhg_p.append(dyh[:, ln] * (n * gwv[:, ln]) * (sg[:, ln] * (1.0 + hg[:, ln] * (1.0 - sg[:, ln]))))
                dgw_acc.append(jnp.sum(d_on[:, ln] * n, axis=0, keepdims=True))
                dn = d_on[:, ln] * gwv[:, ln]
                d_o = rstd * (dn - n * jnp.mean(dn * n, axis=-1, keepdims=True))
                s0 = st_ref[c, h]
                ds = dstate[h]
                att = jnp.where(causal, _dot_nt(qm[:, ln], km[:, ln]), 0.0)
                d_att = jnp.where(causal, _dot_nt(d_o, v[:, ln]), 0.0)
                dv_p.append(_dot_tn(att, d_o) + _dot_nt(ke[:, ln], ds))
                d_qm = _dot(d_att, km[:, ln])
                d_km = _dot_tn(d_att, qm[:, ln])
                d_qi = _dot(d_o, s0)
                d_ke = _dot(v[:, ln], ds)
                dqs_h = d_qi * eb[:, ln] + d_qm * em1[:, ln]
                dk_h = d_km * em2[:, ln] + d_ke * ee[:, ln]
                dqs_p.append(dqs_h)
                dk_p.append(dk_h)
                db_p.append((qi[:, ln].astype(F32) * d_qi + qm[:, ln].astype(F32) * d_qm)
                            - (km[:, ln].astype(F32) * d_km + ke[:, ln].astype(F32) * d_ke))
                ex_p.append(jnp.sum(ds * snext[h], axis=0, keepdims=True))
                dstate[h] = ds * ebc[:, ln] + _dot_tn(d_o, qi[:, ln])
                snext[h] = s0
            cat = lambda parts: jnp.concatenate(parts, axis=1)
            dqs, dk, db = cat(dqs_p), cat(dk_p), cat(db_p)
            dg = _tri_sum(tri_up, db) + cat(ex_p)
            df = dg / f - dk
            dgw_ref[...] += cat(dgw_acc)
            dlb_ref[...] += jnp.sum(df * (1.0 - sig), axis=0, keepdims=True)
            dp_ref[0, rows, :] = (dqs * (sq * (1.0 + q * (1.0 - sq)))).astype(_MXU_DTYPE)
            dp_ref[1, rows, :] = (df * ((1.0 - lb) * sig * (1.0 - sig))).astype(_MXU_DTYPE)
            dp_ref[2, rows, :] = cat(dv_p).astype(_MXU_DTYPE)
            dp_ref[3, rows, :] = cat(dhg_p).astype(_MXU_DTYPE)
            return carry

        lax.fori_loop(0, nc, chunk, 0)

    rev = lambda i: nt - 1 - i
    pair = lambda s: pl.BlockSpec((2, tm, D_MODEL), lambda i: (s, rev(i), 0))
    row = pl.BlockSpec((tm, D_MODEL), lambda i: (rev(i), 0))
    vec = _full((1, D_MODEL))
    return pl.pallas_call(
        body, name="hgrn_bwd", grid=(nt,),
        in_specs=[row, pl.BlockSpec((D_MODEL, D_MODEL), lambda i: (1, 0)), pair(1), pair(2), row,
                  pl.BlockSpec((nc, HEADS, HEAD_D, HEAD_D), lambda i: (rev(i), 0, 0, 0)),
                  _full((2, D_MODEL)), vec, pl.BlockSpec(memory_space=pl.ANY)],
        out_specs=[pl.BlockSpec((4, tm, D_MODEL), lambda i: (0, rev(i), 0)), vec, vec],
        out_shape=[jax.ShapeDtypeStruct((N_SEG, t, D_MODEL), _MXU_DTYPE),
                   jax.ShapeDtypeStruct((1, D_MODEL), F32), jax.ShapeDtypeStruct((1, D_MODEL), F32)],
        scratch_shapes=[pltpu.VMEM((tm, D_MODEL), F32), pltpu.VMEM((HEADS, HEAD_D, HEAD_D), F32),
                        pltpu.VMEM((HEADS, HEAD_D, HEAD_D), F32)],
        input_output_aliases={8: 0},
        compiler_params=_seq_params(),
    )(dy, wo, p, p, o, states, logits, gw, dp)


def _dp_unit(m):
    return ((m // 2 + 2) % N_SEG) * 2 + m % 2


def _dw_in(u, dp, tt):
    t = u.shape[0]
    hr = D_MODEL // 2

    def body(u_ref, dp_ref, g_ref):
        @pl.when(pl.program_id(1) == 0)
        def _():
            g_ref[...] = jnp.zeros_like(g_ref)

        r = _dot_tn(u_ref[...], dp_ref[...])
        g_ref[0] += r[0:hr]
        g_ref[1] += r[hr:D_MODEL]

    return pl.pallas_call(
        body, name="dw_in", grid=(N_UNIT, t // tt),
        in_specs=[pl.BlockSpec((tt, D_MODEL), lambda m, k: (k, 0)),
                  pl.BlockSpec((None, tt, UNIT), lambda m, k: (m // 2, k, m % 2))],
        out_specs=pl.BlockSpec((2, None, hr, UNIT), lambda m, k: (0, _dp_unit(m), 0, 0)),
        out_shape=jax.ShapeDtypeStruct((2, N_UNIT, hr, UNIT), F32),
        compiler_params=pltpu.CompilerParams(dimension_semantics=("arbitrary", "arbitrary")),
    )(u, dp)


def _dx(dp, wg, x, dout, wpre, tm):
    t = x.shape[0]

    def body(dp_ref, w_ref, x_ref, dout_ref, wpre_ref, gx_ref, dw_ref, acc):
        i = pl.program_id(0)
        m = pl.program_id(1)

        @pl.when(jnp.logical_and(i == 0, m == 0))
        def _():
            dw_ref[...] = jnp.zeros_like(dw_ref)

        @pl.when(m == 0)
        def _():
            acc[...] = jnp.zeros_like(acc)

        acc[...] += _dot_nt(dp_ref[...], w_ref[...])

        @pl.when(m == N_UNIT - 1)
        def _():
            du = acc[...]
            xv = x_ref[...]
            rstd = lax.rsqrt(jnp.mean(xv * xv, axis=-1, keepdims=True) + EPS)
            n = xv * rstd
            dw_ref[...] += jnp.sum(du * n, axis=0, keepdims=True)
            dn = du * wpre_ref[...]
            gx_ref[...] = dout_ref[...] + rstd * (dn - n * jnp.mean(dn * n, axis=-1, keepdims=True))

    row = pl.BlockSpec((tm, D_MODEL), lambda i, m: (i, 0))
    return pl.pallas_call(
        body, name="dx", grid=(t // tm, N_UNIT),
        in_specs=[pl.BlockSpec((None, tm, UNIT), lambda i, m: (m // 2, i, m % 2)),
                  pl.BlockSpec((None, D_MODEL, UNIT), lambda i, m: (_dp_unit(m), 0, 0)),
                  row, row, _full((1, D_MODEL))],
        out_specs=[row, _full((1, D_MODEL))],
        out_shape=[jax.ShapeDtypeStruct((t, D_MODEL), F32), jax.ShapeDtypeStruct((1, D_MODEL), F32)],
        scratch_shapes=[pltpu.VMEM((tm, D_MODEL), F32)],
        compiler_params=pltpu.CompilerParams(dimension_semantics=("arbitrary", "arbitrary")),
    )(dp, wg, x, dout, wpre)


CHUNK_BYTES = 512 * 1024
MAX_CHUNKS = 16


def _n_chunks(shape, dtype, rows):
    nbytes = jnp.dtype(dtype).itemsize
    for d in shape:
        nbytes *= d
    k = max(1, min(MAX_CHUNKS, rows, nbytes // CHUNK_BYTES))
    while rows % k:
        k -= 1
    return k


def _gather_weights(shards, split):
    n = len(shards)

    def body(*refs):
        ins, outs = refs[:n], refs[n:2 * n]
        send_sems, recv_sems, fsend_sems, frecv_sems, local_sems = refs[2 * n:]
        x, y, c = lax.axis_index("x"), lax.axis_index("y"), lax.axis_index("c")
        me = 2 * x + y
        peers = [(1 - x, y), (x, 1 - y), (1 - x, 1 - y)]
        half = lambda a: c if split[a] else 0

        def fetch(a, j, slot):
            px, py = peers[j]
            return pltpu.make_async_remote_copy(
                src_ref=ins[a].at[half(a)], dst_ref=outs[a].at[slot, half(a)],
                send_sem=send_sems.at[a * 3 + j], recv_sem=recv_sems.at[a * 3 + j],
                device_id=(px, py, c), device_id_type=MESH)

        def forward(a, j, h):
            px, py = peers[j]
            piece = outs[a].at[2 * px + py, h]
            return pltpu.make_async_remote_copy(
                src_ref=piece, dst_ref=piece, send_sem=fsend_sems.at[a * 3 + j], recv_sem=frecv_sems.at[a * 3 + j],
                device_id=(x, y, 1 - c), device_id_type=MESH)

        local = [pltpu.make_async_copy(ins[a], outs[a].at[me], local_sems.at[a]) for a in range(n)]
        for a in range(n):
            local[a].start()
            for j in range(3):
                fetch(a, j, me).start()
        for a in range(n):
            for j in range(3):
                px, py = peers[j]
                fetch(a, j, 2 * px + py).wait_recv()
                if split[a]:
                    forward(a, j, c).start()
        for a in range(n):
            for j in range(3):
                if split[a]:
                    forward(a, j, 1 - c).wait_recv()
        for a in range(n):
            for j in range(3):
                fetch(a, j, me).wait_send()
                if split[a]:
                    forward(a, j, c).wait_send()
            local[a].wait()

    any_spec = pl.BlockSpec(memory_space=pl.ANY)
    return pl.pallas_call(
        body, name="gather_weights",
        in_specs=[any_spec] * n, out_specs=[any_spec] * n,
        out_shape=[jax.ShapeDtypeStruct((N_CHIPS,) + a.shape, a.dtype) for a in shards],
        scratch_shapes=[pltpu.SemaphoreType.DMA((3 * n,)), pltpu.SemaphoreType.DMA((3 * n,)),
                        pltpu.SemaphoreType.DMA((3 * n,)), pltpu.SemaphoreType.DMA((3 * n,)),
                        pltpu.SemaphoreType.DMA((n,))],
    )(*shards)


def _chip_exchange(arrays, scatter, name):
    n = len(arrays)
    slab_shapes = [a.shape[1:] if sc else a.shape for a, sc in zip(arrays, scatter)]

    def body(*refs):
        ins, outs = refs[:n], refs[n:2 * n]
        send_sems, recv_sems, local_sems = refs[2 * n:]
        x, y, c = lax.axis_index("x"), lax.axis_index("y"), lax.axis_index("c")
        me = 2 * x + y
        peers = [(1 - x, y), (x, 1 - y), (1 - x, 1 - y)]

        def part(a, q):
            return ins[a].at[q] if scatter[a] else ins[a]

        def remote(a, j):
            px, py = peers[j]
            return pltpu.make_async_remote_copy(
                src_ref=part(a, 2 * px + py), dst_ref=outs[a].at[me],
                send_sem=send_sems.at[a * 3 + j], recv_sem=recv_sems.at[a * 3 + j],
                device_id=(px, py, c), device_id_type=MESH)

        def arrival(a, j):
            px, py = peers[j]
            return pltpu.make_async_remote_copy(
                src_ref=part(a, me), dst_ref=outs[a].at[2 * px + py],
                send_sem=send_sems.at[a * 3 + j], recv_sem=recv_sems.at[a * 3 + j],
                device_id=(px, py, c), device_id_type=MESH)

        local = [pltpu.make_async_copy(part(a, me), outs[a].at[me], local_sems.at[a]) for a in range(n)]
        for a in range(n):
            local[a].start()
            for j in range(3):
                remote(a, j).start()
        for a in range(n):
            for j in range(3):
                arrival(a, j).wait_recv()
        for a in range(n):
            for j in range(3):
                remote(a, j).wait_send()
            local[a].wait()

    any_spec = pl.BlockSpec(memory_space=pl.ANY)
    return pl.pallas_call(
        body, name=name,
        in_specs=[any_spec] * n, out_specs=[any_spec] * n,
        out_shape=[jax.ShapeDtypeStruct((N_CHIPS,) + tuple(s), a.dtype) for s, a in zip(slab_shapes, arrays)],
        scratch_shapes=[pltpu.SemaphoreType.DMA((3 * n,)), pltpu.SemaphoreType.DMA((3 * n,)),
                        pltpu.SemaphoreType.DMA((n,))],
    )(*arrays)


def _core_swap(arrays, name):
    n = len(arrays)
    ks = [_n_chunks(a.shape[1:], a.dtype, a.shape[1]) for a in arrays]
    offs = [sum(ks[:a]) for a in range(n)]

    def body(*refs):
        ins, outs = refs[:n], refs[n:2 * n]
        send_sems, recv_sems = refs[2 * n:]
        x, y, c = lax.axis_index("x"), lax.axis_index("y"), lax.axis_index("c")

        def copy(a, j):
            step = arrays[a].shape[1] // ks[a]
            rows = pl.ds(j * step, step)
            return pltpu.make_async_remote_copy(
                src_ref=ins[a].at[1 - c, rows], dst_ref=outs[a].at[rows],
                send_sem=send_sems.at[offs[a] + j], recv_sem=recv_sems.at[offs[a] + j],
                device_id=(x, y, 1 - c), device_id_type=MESH)

        pieces = [(a, j) for a in range(n) for j in range(ks[a])]
        for a, j in pieces:
            copy(a, j).start()
        for a, j in pieces:
            copy(a, j).wait_recv()
        for a, j in pieces:
            copy(a, j).wait_send()

    any_spec = pl.BlockSpec(memory_space=pl.ANY)
    return pl.pallas_call(
        body, name=name,
        in_specs=[any_spec] * n, out_specs=[any_spec] * n,
        out_shape=[jax.ShapeDtypeStruct(a.shape[1:], a.dtype) for a in arrays],
        scratch_shapes=[pltpu.SemaphoreType.DMA((sum(ks),)), pltpu.SemaphoreType.DMA((sum(ks),))],
    )(*arrays)


def _core_exchange(arrays, name):
    n = len(arrays)
    ks = [_n_chunks(a.shape, a.dtype, a.shape[0]) for a in arrays]
    offs = [sum(ks[:a]) for a in range(n)]

    def body(*refs):
        ins, outs = refs[:n], refs[n:2 * n]
        send_sems, recv_sems, local_sems = refs[2 * n:]
        x, y, c = lax.axis_index("x"), lax.axis_index("y"), lax.axis_index("c")

        def copy(a, j, slot):
            step = arrays[a].shape[0] // ks[a]
            rows = pl.ds(j * step, step)
            return pltpu.make_async_remote_copy(
                src_ref=ins[a].at[rows], dst_ref=outs[a].at[slot, rows],
                send_sem=send_sems.at[offs[a] + j], recv_sem=recv_sems.at[offs[a] + j],
                device_id=(x, y, 1 - c), device_id_type=MESH)

        local = [pltpu.make_async_copy(ins[a], outs[a].at[c], local_sems.at[a]) for a in range(n)]
        pieces = [(a, j) for a in range(n) for j in range(ks[a])]
        for a in range(n):
            local[a].start()
        for a, j in pieces:
            copy(a, j, c).start()
        for a, j in pieces:
            copy(a, j, 1 - c).wait_recv()
        for a, j in pieces:
            copy(a, j, c).wait_send()
        for a in range(n):
            local[a].wait()

    any_spec = pl.BlockSpec(memory_space=pl.ANY)
    return pl.pallas_call(
        body, name=name,
        in_specs=[any_spec] * n, out_specs=[any_spec] * n,
        out_shape=[jax.ShapeDtypeStruct((2,) + a.shape, a.dtype) for a in arrays],
        scratch_shapes=[pltpu.SemaphoreType.DMA((sum(ks),)), pltpu.SemaphoreType.DMA((sum(ks),)),
                        pltpu.SemaphoreType.DMA((n,))],
    )(*arrays)


def _sum_own_half(g, b, name, out_dtype, tr=512):
    _, rows, cols = g.shape
    tr = min(tr, rows)
    core = lax.axis_index("c").reshape(1)

    def body(c_ref, g_ref, b_ref, o_ref):
        del c_ref
        o_ref[...] = (g_ref[...] + b_ref[...]).astype(out_dtype)

    return pl.pallas_call(
        body, name=name,
        grid_spec=pltpu.PrefetchScalarGridSpec(
            num_scalar_prefetch=1, grid=(rows // tr,),
            in_specs=[pl.BlockSpec((None, tr, cols), lambda i, c_ref: (c_ref[0], i, 0)),
                      pl.BlockSpec((tr, cols), lambda i, c_ref: (i, 0))],
            out_specs=pl.BlockSpec((tr, cols), lambda i, c_ref: (i, 0))),
        out_shape=jax.ShapeDtypeStruct((rows, cols), out_dtype),
        compiler_params=pltpu.CompilerParams(dimension_semantics=("parallel",)),
    )(core, g, b)


def _sum_slots(r, name, tr=512):
    s, rows, cols = r.shape
    tr = min(tr, rows)

    def body(r_ref, o_ref):
        acc = r_ref[0].astype(F32)
        for j in range(1, s):
            acc = acc + r_ref[j].astype(F32)
        o_ref[...] = acc

    return pl.pallas_call(
        body, name=name, grid=(rows // tr,),
        in_specs=[pl.BlockSpec((s, tr, cols), lambda i: (0, i, 0))],
        out_specs=pl.BlockSpec((tr, cols), lambda i: (i, 0)),
        out_shape=jax.ShapeDtypeStruct((rows, cols), F32),
        compiler_params=pltpu.CompilerParams(dimension_semantics=("parallel",)),
    )(r)


def _adamw_big(g, w, m, v, name, grid, g_spec, w_spec):
    def body(q_ref, w_ref, m_ref, v_ref, g_ref, d_ref, nm_ref, nv_ref):
        gv = q_ref[...]
        g_ref[...] = gv
        d_ref[...], nm_ref[...], nv_ref[...] = _adamw(w_ref[...], gv, m_ref[...], v_ref[...])

    shp = jax.ShapeDtypeStruct(w.shape, F32)
    return pl.pallas_call(
        body, name=name, grid=grid,
        in_specs=[g_spec, w_spec, w_spec, w_spec], out_specs=[w_spec] * 4, out_shape=[shp] * 4,
        compiler_params=pltpu.CompilerParams(dimension_semantics=("parallel",) * len(grid)),
    )(g, w, m, v)


def _small_finish(q, lam, logits):
    def body(q_ref, lam_ref, lg_ref, o_ref):
        g = q_ref[0] + q_ref[1]
        o_ref[...] = g
        o_ref[8:9, :] = g[8:9, :] * (-_sigmoid(-lam_ref[...]))
        lb = 1.0 / (1.0 + jnp.exp(lg_ref[1:2, :] - lg_ref[0:1, :]))
        d0 = g[9:10, :] * (lb * (1.0 - lb))
        o_ref[9:10, :] = d0
        o_ref[10:11, :] = -d0

    return pl.pallas_call(
        body, name="small_finish",
        in_specs=[_full((2, SMALL_ROWS, D_MODEL)), _full((1, D_MODEL)), _full((2, D_MODEL))],
        out_specs=_full((SMALL_ROWS, D_MODEL)),
        out_shape=jax.ShapeDtypeStruct((SMALL_ROWS, D_MODEL), F32),
    )(q, lam, logits)


def _adamw_small(ws, gs, ms, vs):
    n = len(ws)

    def body(*refs):
        w_r, g_r, m_r, v_r = refs[:n], refs[n:2 * n], refs[2 * n:3 * n], refs[3 * n:4 * n]
        d_o, m_o, v_o = refs[4 * n:5 * n], refs[5 * n:6 * n], refs[6 * n:7 * n]
        for j in range(n):
            d_o[j][...], m_o[j][...], v_o[j][...] = _adamw(w_r[j][...], g_r[j][...], m_r[j][...], v_r[j][...])

    specs = [_full(w.shape) for w in ws]
    shapes = [jax.ShapeDtypeStruct(w.shape, F32) for w in ws]
    outs = pl.pallas_call(
        body, name="adamw_small",
        in_specs=specs * 4, out_specs=specs * 3, out_shape=shapes * 3,
    )(*ws, *gs, *ms, *vs)
    return outs[:n], outs[n:2 * n], outs[2 * n:]


def _local_step(x, tgt, pre_w, wg, cw, cb, wa, wx, ba, bx, lam, logits, gw, wo, post_w, tm, tmm):
    u = _rmsnorm_pre(x, pre_w, tmm)
    p = _in_proj(u, wg, tmm)
    h, y2 = _lru_fwd(p, wa, wx, ba, bx, lam, cw, cb, tm)
    y2, o, states = _hgrn_fwd(p, logits, gw, y2, tm)
    dout, dy, sq, d_post = _out_fused(y2, wo, x, tgt, post_w, tmm)
    g_out = _dw_out(y2, dy, tmm)
    dp, d_wa, d_wx, d_ba, d_bx, d_sp, d_cw, d_cb = _lru_bwd(dy, wo, p, h, wa, wx, ba, bx, lam, cw, cb, tm)
    dp, d_gw, d_lb = _hgrn_bwd(dy, wo, p, o, states, logits, gw, dp, tm)
    g_in = _dw_in(u, dp, tmm)
    grad_x, d_pre = _dx(dp, wg, x, dout, pre_w, tmm)
    small = jnp.concatenate([d_pre, d_cw, d_cb, d_ba, d_bx, d_sp, d_lb,
                             jnp.zeros((1, D_MODEL), F32), d_gw, d_post,
                             jnp.zeros((SMALL_ROWS - 13, D_MODEL), F32)], axis=0)
    return sq, grad_x, g_in, g_out, d_wa, d_wx, small


def kernel(x, pre_norm_w, w_in, conv_w, conv_b, lru_w_a, lru_b_a, lru_w_x, lru_b_x, lru_lambda, hgrn_lb_logits, hgrn_gnorm_w, w_out, post_norm_w, loss_target, m_pre_norm_w, m_w_in, m_conv_w, m_conv_b, m_lru_w_a, m_lru_b_a, m_lru_w_x, m_lru_b_x, m_lru_lambda, m_hgrn_lb_logits, m_hgrn_gnorm_w, m_w_out, m_post_norm_w, v_pre_norm_w, v_w_in, v_conv_w, v_conv_b, v_lru_w_a, v_lru_b_a, v_lru_w_x, v_lru_b_x, v_lru_lambda, v_hgrn_lb_logits, v_hgrn_gnorm_w, v_w_out, v_post_norm_w):
    t = x.shape[1]
    tm = min(256, t)
    tmm = min(512, t)
    chip = 2 * lax.axis_index("x") + lax.axis_index("y")

    win_units = w_in[0].astype(_MXU_DTYPE).reshape(D_MODEL, 3, UNIT).transpose(1, 0, 2)
    small_w = jnp.concatenate([conv_w[0], lru_b_a[0], lru_b_x[0]], axis=1)
    wg4, wo4, wa4, wx4, sw4 = _gather_weights(
        [win_units.reshape(2, 3 * 512, UNIT), w_out[0].astype(_MXU_DTYPE).reshape(2, 256, D_MODEL),
         lru_w_a[0].astype(_MXU_DTYPE).reshape(2, 128, LRU_BW), lru_w_x[0].astype(_MXU_DTYPE).reshape(2, 128, LRU_BW),
         small_w.reshape(1, 4, 384)],
        [True, True, True, True, False])
    wg = wg4.reshape(N_UNIT, D_MODEL, UNIT)
    wo = wo4.reshape(2 * D_MODEL, D_MODEL)
    by_block = lambda w4: w4.reshape(N_CHIPS, LRU_BLOCKS, 64, LRU_BW).transpose(1, 0, 2, 3).reshape(
        LRU_BLOCKS, LRU_BW, LRU_BW)
    wa, wx = by_block(wa4), by_block(wx4)
    sw4 = sw4.reshape(N_CHIPS, 4, 384)
    cw = sw4[:, :, 0:256].transpose(1, 0, 2).reshape(4, D_MODEL)
    ba = sw4[:, :, 256:320].transpose(1, 0, 2).reshape(1, D_MODEL)
    bx = sw4[:, :, 320:384].transpose(1, 0, 2).reshape(1, D_MODEL)

    sq, grad_x, g_in, g_out, d_wa, d_wx, small = _local_step(
        x[0], loss_target[0], pre_norm_w, wg, cw, conv_b, wa, wx, ba, bx, lru_lambda, hgrn_lb_logits,
        hgrn_gnorm_w, wo, post_norm_w, tm, tmm)
    loss = lax.psum(jnp.sum(sq) * (0.5 / D_MODEL), ("x", "y", "c"))

    def lru_layout(g):
        g = g.reshape(2, 2, N_CHIPS, 64, LRU_BW).transpose(0, 2, 1, 3, 4)
        return g.reshape(2, N_CHIPS * 128, LRU_BW)

    g_in = g_in.reshape(2, N_UNIT * 512, UNIT)
    g_out = g_out.reshape(2, N_CHIPS * 256, D_MODEL)
    g_wa, g_wx = lru_layout(d_wa), lru_layout(d_wx)
    b_in, b_out, b_wa, b_wx = _core_swap([g_in, g_out, g_wa, g_wx], "reduce_swap")
    p_in = _sum_own_half(g_in, b_in, "presum_in", _WIRE_DTYPE)
    p_out = _sum_own_half(g_out, b_out, "presum_out", _WIRE_DTYPE)
    p_wa = _sum_own_half(g_wa, b_wa, "presum_wa", _WIRE_DTYPE)
    p_wx = _sum_own_half(g_wx, b_wx, "presum_wx", _WIRE_DTYPE)
    r_in, r_out, r_wa, r_wx, r_small = _chip_exchange(
        [p_in.reshape(N_CHIPS, 3 * 512, UNIT), p_out.reshape(N_CHIPS, 256, D_MODEL),
         p_wa.reshape(N_CHIPS, 128, LRU_BW), p_wx.reshape(N_CHIPS, 128, LRU_BW), small],
        [True, True, True, True, False], "reduce_chips")
    s_in = _sum_slots(r_in, "sum_in")
    s_out = _sum_slots(r_out, "sum_out")
    s_wa = _sum_slots(r_wa, "sum_wa")
    s_wx = _sum_slots(r_wx, "sum_wx")
    s_small = _sum_slots(r_small, "sum_small")
    q_in, q_out, q_wa, q_wx, q_small = _core_exchange([s_in, s_out, s_wa, s_wx, s_small], "reduce_cores")

    g_w_in, d_w_in, nm_w_in, nv_w_in = _adamw_big(
        q_in.reshape(2, 3, 512, UNIT), w_in[0], m_w_in[0], v_w_in[0], "adamw_w_in", (3, 2),
        pl.BlockSpec((None, None, 512, UNIT), lambda k, h: (h, k, 0, 0)),
        pl.BlockSpec((512, UNIT), lambda k, h: (h, k)))
    g_w_out, d_w_out, nm_w_out, nv_w_out = _adamw_big(
        q_out, w_out[0], m_w_out[0], v_w_out[0], "adamw_w_out", (2,),
        pl.BlockSpec((None, 256, D_MODEL), lambda h: (h, 0, 0)), pl.BlockSpec((256, D_MODEL), lambda h: (h, 0)))
    sq2 = lambda a: a.reshape(LRU_BW, LRU_BW)
    lru_specs = ((2,), pl.BlockSpec((None, 128, LRU_BW), lambda h: (h, 0, 0)),
                 pl.BlockSpec((128, LRU_BW), lambda h: (h, 0)))
    g_wa, d_wa2, nm_wa, nv_wa = _adamw_big(q_wa, sq2(lru_w_a), sq2(m_lru_w_a), sq2(v_lru_w_a), "adamw_wa", *lru_specs)
    g_wx, d_wx2, nm_wx, nv_wx = _adamw_big(q_wx, sq2(lru_w_x), sq2(m_lru_w_x), sq2(v_lru_w_x), "adamw_wx", *lru_specs)

    gs = _small_finish(q_small, lru_lambda, hgrn_lb_logits)
    g_pre = gs[0:1]
    g_cw = lax.dynamic_slice(gs[1:5], (0, chip * 256), (4, 256))
    g_cb = gs[5:6]
    g_ba = lax.dynamic_slice(gs[6].reshape(LRU_BLOCKS, N_CHIPS, 64), (0, chip, 0), (LRU_BLOCKS, 1, 64)).reshape(4, 64)
    g_bx = lax.dynamic_slice(gs[7].reshape(LRU_BLOCKS, N_CHIPS, 64), (0, chip, 0), (LRU_BLOCKS, 1, 64)).reshape(4, 64)
    g_lam = gs[8:9]
    g_lb = gs[9:11]
    g_gw = gs[11:12]
    g_post = gs[12:13]
    small_g = [g_pre, g_cw, g_cb, g_ba, g_bx, g_lam, g_lb, g_gw, g_post]
    two_d = lambda a: a.reshape(a.shape[-2:])
    small_w_list = [pre_norm_w, conv_w, conv_b, lru_b_a, lru_b_x, lru_lambda, hgrn_lb_logits, hgrn_gnorm_w, post_norm_w]
    small_m_list = [m_pre_norm_w, m_conv_w, m_conv_b, m_lru_b_a, m_lru_b_x, m_lru_lambda, m_hgrn_lb_logits,
                    m_hgrn_gnorm_w, m_post_norm_w]
    small_v_list = [v_pre_norm_w, v_conv_w, v_conv_b, v_lru_b_a, v_lru_b_x, v_lru_lambda, v_hgrn_lb_logits,
                    v_hgrn_gnorm_w, v_post_norm_w]
    sd, sm, sv = _adamw_small([two_d(a) for a in small_w_list], small_g,
                              [two_d(a) for a in small_m_list], [two_d(a) for a in small_v_list])

    def shaped(vals, refs):
        return [val.reshape(ref.shape) for val, ref in zip(vals, refs)]

    s_g = shaped(small_g, small_w_list)
    s_d = shaped(sd, small_w_list)
    s_m = shaped(sm, small_w_list)
    s_v = shaped(sv, small_w_list)

    def ordered(small, big_in, big_wa, big_wx, big_out):
        pre, cw_, cb_, ba_, bx_, lam_, lb_, gw_, post_ = small
        return [pre, big_in.reshape(w_in.shape), cw_, cb_, big_wa.reshape(lru_w_a.shape), ba_,
                big_wx.reshape(lru_w_x.shape), bx_, lam_, lb_, gw_, big_out.reshape(w_out.shape), post_]

    grads = ordered(s_g, g_w_in, g_wa, g_wx, g_w_out)
    deltas = ordered(s_d, d_w_in, d_wa2, d_wx2, d_w_out)
    new_m = ordered(s_m, nm_w_in, nm_wa, nm_wx, nm_w_out)
    new_v = ordered(s_v, nv_w_in, nv_wa, nv_wx, nv_w_out)
    return (loss, grad_x.reshape(x.shape), *grads, *deltas, *new_m, *new_v)
```

```python
import functools

import jax
import jax.numpy as jnp
from jax import lax
from jax.experimental import pallas as pl
from jax.experimental.pallas import tpu as pltpu

F32 = jnp.float32
BF16 = jnp.bfloat16
_MXU_DTYPE = jnp.bfloat16
_WIRE_DTYPE = jnp.bfloat16

D_MODEL = 1024
N_SEG = 6
UNIT = 512
N_UNIT = 12
LRU_BLOCKS = 4
LRU_BW = 256
LRU_C = 8.0
HEADS = 8
HEAD_D = 128
CHUNK = 64
EPS = 1e-6
N_CHIPS = 4
EXP_CLAMP = 80.0

ADAM_LR = 0.001
ADAM_B1 = 0.9
ADAM_B2 = 0.999
ADAM_EPS = 1e-08
ADAM_WD = 0.01
ADAM_STEP = 10

SMALL_ROWS = 16
MESH = pl.DeviceIdType.MESH


def _sigmoid(x):
    return 1.0 / (1.0 + jnp.exp(-x))


def _mx(x):
    return x.astype(_MXU_DTYPE)


def _dot(a, b):
    return jnp.dot(_mx(a), _mx(b), preferred_element_type=F32)


def _dot_nt(a, b):
    return lax.dot_general(_mx(a), _mx(b), (((1,), (1,)), ((), ())), preferred_element_type=F32)


def _dot_tn(a, b):
    return lax.dot_general(_mx(a), _mx(b), (((0,), (0,)), ((), ())), preferred_element_type=F32)


def _rows(shape):
    return lax.broadcasted_iota(jnp.int32, shape, 0)


def _cols(shape):
    return lax.broadcasted_iota(jnp.int32, shape, 1)


def _softplus_neg(lam):
    z = -lam
    e = jnp.exp(-jnp.abs(z))
    series = e * (1.0 - e * (0.5 - e * (1.0 / 3.0 - 0.25 * e)))
    return jnp.maximum(z, 0.0) + jnp.where(e < 1e-2, series, jnp.log(1.0 + e))


def _neg_expm1(y):
    series = -y * (1.0 + y * (0.5 + y * (1.0 / 6.0 + y * (1.0 / 24.0))))
    return jnp.where(y > -0.03, series, 1.0 - jnp.exp(y))


def _split3(x):
    hi = x.astype(BF16)
    r1 = x - hi.astype(F32)
    mid = r1.astype(BF16)
    lo = (r1 - mid.astype(F32)).astype(BF16)
    return hi, mid, lo


def _tri_sum(tri, x):
    hi, mid, lo = _split3(x)
    d = lambda p: jnp.dot(tri, p, preferred_element_type=F32)
    return d(hi) + d(mid) + d(lo)


def _shift_down(x, halo, s):
    if s == 0:
        return x
    r = pltpu.roll(x, s, 0)
    top = jnp.where(_rows(halo.shape) < s, pltpu.roll(halo, s, 0), r[0:8])
    return jnp.concatenate([top, r[8:]], axis=0)


def _shift_up(x, head, s):
    if s == 0:
        return x
    n = x.shape[0]
    r = pltpu.roll(x, n - s, 0)
    bottom = jnp.where(_rows(head.shape) >= 8 - s, pltpu.roll(head, 8 - s, 0), r[n - 8:n])
    return jnp.concatenate([r[:n - 8], bottom], axis=0)


def _lru_gates(lx, halo, cw, cb, wa, wx, ba, bx, sp):
    xs = [_shift_down(lx, halo, 3 - k) for k in range(4)]
    xc = cb + cw[0:1] * xs[0] + cw[1:2] * xs[1] + cw[2:3] * xs[2] + cw[3:4] * xs[3]
    zr, zi = [], []
    for n in range(LRU_BLOCKS):
        xb = _mx(xc[:, n * LRU_BW:(n + 1) * LRU_BW])
        zr.append(jnp.dot(xb, wa[n], preferred_element_type=F32))
        zi.append(jnp.dot(xb, wx[n], preferred_element_type=F32))
    r = _sigmoid(jnp.concatenate(zr, axis=1) + ba)
    ig = _sigmoid(jnp.concatenate(zi, axis=1) + bx)
    la = (-LRU_C * sp) * r
    a = jnp.exp(la)
    mult = jnp.sqrt(_neg_expm1(2.0 * la))
    return xs, xc, r, ig, a, mult


def _hgrn_prep(q, fr, lb, tri):
    sig = _sigmoid(fr)
    f = lb + (1.0 - lb) * sig
    k = 1.0 - f
    sq = _sigmoid(q)
    qs = q * sq
    b = _tri_sum(tri, jnp.log(f))
    bm = b[CHUNK // 2 - 1:CHUNK // 2]
    bc = b[CHUNK - 1:CHUNK]
    eb = jnp.exp(b)
    em1 = jnp.exp(jnp.minimum(b - bm, EXP_CLAMP))
    em2 = jnp.exp(jnp.minimum(bm - b, EXP_CLAMP))
    ee = jnp.exp(bc - b)
    ebc = jnp.exp(bc)
    return sig, f, k, sq, qs, eb, em1, em2, ee, ebc


def _adamw(w, g, m, v):
    m = ADAM_B1 * m + (1.0 - ADAM_B1) * g
    v = ADAM_B2 * v + (1.0 - ADAM_B2) * jnp.square(g)
    m_hat = m / (1.0 - ADAM_B1 ** ADAM_STEP)
    v_hat = v / (1.0 - ADAM_B2 ** ADAM_STEP)
    delta = -ADAM_LR * (m_hat / (jnp.sqrt(v_hat) + ADAM_EPS) + ADAM_WD * w)
    return delta, m, v


def _full(shape):
    nd = len(shape)
    return pl.BlockSpec(shape, lambda *_: (0,) * nd)


def _seq_params():
    return pltpu.CompilerParams(dimension_semantics=("arbitrary",))


def _resident(shape):
    nd = len(shape)
    return pl.BlockSpec(shape, lambda *_: (0,) * nd, pipeline_mode=pl.Buffered(1))


def _in_proj(x, w, wg, tm):
    t = x.shape[0]

    def body(x_ref, w_ref, wg_ref, ut_ref, p_ref):
        xv = x_ref[...]
        rstd = lax.rsqrt(jnp.mean(xv * xv, axis=-1, keepdims=True) + EPS)
        u = xv * rstd * w_ref[...]
        ut_ref[...] = u.T.astype(_MXU_DTYPE)
        ub = _mx(u)
        for m in range(N_UNIT):
            p_ref[m // 2, :, (m % 2) * UNIT:(m % 2 + 1) * UNIT] = jnp.dot(
                ub, wg_ref[m], preferred_element_type=F32)

    return pl.pallas_call(
        body, name="in_proj", grid=(t // tm,),
        in_specs=[pl.BlockSpec((tm, D_MODEL), lambda i: (i, 0)), _full((1, D_MODEL)),
                  _resident((N_UNIT, D_MODEL, UNIT))],
        out_specs=[pl.BlockSpec((D_MODEL, tm), lambda i: (0, i)),
                   pl.BlockSpec((N_SEG, tm, D_MODEL), lambda i: (0, i, 0))],
        out_shape=[jax.ShapeDtypeStruct((D_MODEL, t), _MXU_DTYPE),
                   jax.ShapeDtypeStruct((N_SEG, t, D_MODEL), F32)],
        compiler_params=pltpu.CompilerParams(dimension_semantics=("parallel",)),
    )(x, w, wg)


def _lru_fwd(p, wa, wx, ba, bx, lam, cw, cb, tm):
    t = p.shape[1]
    ng = tm // 8

    def body(lx_ref, halo_ref, lg_ref, wa_ref, wx_ref, ba_ref, bx_ref, lam_ref, cw_ref, cb_ref,
             h_ref, y_ref, a_s, u_s, hc_s):
        i = pl.program_id(0)

        @pl.when(i == 0)
        def _():
            hc_s[...] = jnp.zeros_like(hc_s)

        sp = _softplus_neg(lam_ref[...])
        halo = jnp.where(i == 0, 0.0, halo_ref[...])
        _, xc, _, ig, a, mult = _lru_gates(lx_ref[...], halo, cw_ref[...], cb_ref[...], wa_ref[...],
                                           wx_ref[...], ba_ref[...], bx_ref[...], sp)
        a_s[...] = a
        u_s[...] = mult * (ig * xc)
        row8 = _rows((8, D_MODEL))

        def group(g, hc):
            rows = pl.ds(pl.multiple_of(g * 8, 8), 8)
            av = a_s[rows, :]
            uv = u_s[rows, :]
            for s in (1, 2, 4):
                a_sh = jnp.where(row8 >= s, pltpu.roll(av, s, 0), 1.0)
                u_sh = jnp.where(row8 >= s, pltpu.roll(uv, s, 0), 0.0)
                uv = av * u_sh + uv
                av = av * a_sh
            hh = av * hc + uv
            h_ref[rows, :] = hh
            return jnp.broadcast_to(hh[7:8, :], (8, D_MODEL))

        hc_s[...] = lax.fori_loop(0, ng, group, hc_s[...])
        lg = lg_ref[...]
        y_ref[...] = (h_ref[...] * (lg * _sigmoid(lg))).astype(_MXU_DTYPE)

    seg = lambda s: pl.BlockSpec((None, tm, D_MODEL), lambda i: (s, i, 0))
    return pl.pallas_call(
        body, name="lru_fwd", grid=(t // tm,),
        in_specs=[seg(0),
                  pl.BlockSpec((None, 8, D_MODEL), lambda i: (0, jnp.maximum(i * (tm // 8) - 1, 0), 0)),
                  seg(1),
                  _full((LRU_BLOCKS, LRU_BW, LRU_BW)), _full((LRU_BLOCKS, LRU_BW, LRU_BW)),
                  _full((1, D_MODEL)), _full((1, D_MODEL)), _full((1, D_MODEL)),
                  _full((4, D_MODEL)), _full((1, D_MODEL))],
        out_specs=[pl.BlockSpec((tm, D_MODEL), lambda i: (i, 0)),
                   pl.BlockSpec((None, tm, D_MODEL), lambda i: (0, i, 0))],
        out_shape=[jax.ShapeDtypeStruct((t, D_MODEL), F32),
                   jax.ShapeDtypeStruct((2, t, D_MODEL), _MXU_DTYPE)],
        scratch_shapes=[pltpu.VMEM((tm, D_MODEL), F32), pltpu.VMEM((tm, D_MODEL), F32),
                        pltpu.VMEM((8, D_MODEL), F32)],
        compiler_params=_seq_params(),
    )(p, p, p, wa, wx, ba, bx, lam, cw, cb)


def _hgrn_fwd(p, logits, gw, y2, tm):
    t = p.shape[1]
    nc = tm // CHUNK

    def body(qf_ref, vh_ref, lg_ref, gw_ref, y_in, y_ref, o_ref, st_ref, state):
        del y_in
        i = pl.program_id(0)

        @pl.when(i == 0)
        def _():
            state[...] = jnp.zeros_like(state)

        l0 = lg_ref[0:1, :]
        l1 = lg_ref[1:2, :]
        lb = 1.0 / (1.0 + jnp.exp(l1 - l0))
        gwv = gw_ref[...]
        tri = (_rows((CHUNK, CHUNK)) >= _cols((CHUNK, CHUNK))).astype(BF16)
        causal = _rows((CHUNK, CHUNK)) >= _cols((CHUNK, CHUNK))

        def chunk(c, carry):
            rows = pl.ds(pl.multiple_of(c * CHUNK, CHUNK), CHUNK)
            q = qf_ref[0, rows, :]
            fr = qf_ref[1, rows, :]
            v = vh_ref[0, rows, :]
            hg = vh_ref[1, rows, :]
            _, _, k, _, qs, eb, em1, em2, ee, ebc = _hgrn_prep(q, fr, lb, tri)
            qi, qm, km, ke = qs * eb, qs * em1, k * em2, k * ee
            gate = hg * _sigmoid(hg)
            for h in range(HEADS):
                ln = slice(h * HEAD_D, (h + 1) * HEAD_D)
                s0 = state[h]
                st_ref[c, h] = s0
                att = jnp.where(causal, _dot_nt(qm[:, ln], km[:, ln]), 0.0)
                o = _dot_nt(qi[:, ln], s0) + _dot(att, v[:, ln])
                state[h] = s0 * ebc[:, ln] + _dot_tn(v[:, ln], ke[:, ln])
                rstd = lax.rsqrt(jnp.mean(o * o, axis=-1, keepdims=True) + EPS)
                o_ref[rows, ln] = o
                y_ref[rows, ln] = (o * rstd * gwv[:, ln] * gate[:, ln]).astype(_MXU_DTYPE)
            return carry

        lax.fori_loop(0, nc, chunk, 0)

    pair = lambda s: pl.BlockSpec((2, tm, D_MODEL), lambda i: (s, i, 0))
    return pl.pallas_call(
        body, name="hgrn_fwd", grid=(t // tm,),
        in_specs=[pair(1), pair(2), _full((2, D_MODEL)), _full((1, D_MODEL)),
                  pl.BlockSpec(memory_space=pl.ANY)],
        out_specs=[pl.BlockSpec((None, tm, D_MODEL), lambda i: (1, i, 0)),
                   pl.BlockSpec((tm, D_MODEL), lambda i: (i, 0)),
                   pl.BlockSpec((nc, HEADS, HEAD_D, HEAD_D), lambda i: (i, 0, 0, 0))],
        out_shape=[jax.ShapeDtypeStruct((2, t, D_MODEL), _MXU_DTYPE),
                   jax.ShapeDtypeStruct((t, D_MODEL), F32),
                   jax.ShapeDtypeStruct((t // CHUNK, HEADS, HEAD_D, HEAD_D), F32)],
        scratch_shapes=[pltpu.VMEM((HEADS, HEAD_D, HEAD_D), F32)],
        input_output_aliases={4: 0},
        compiler_params=_seq_params(),
    )(p, p, logits, gw, y2)


def _out_fused(y2, wo, x, tgt, wpost, tm):
    t = x.shape[0]
    inv_d = 1.0 / D_MODEL

    def body(y_ref, wo_ref, x_ref, t_ref, w_ref, dout_ref, dy_ref, sq_ref, dw_ref):
        i = pl.program_id(0)

        @pl.when(i == 0)
        def _():
            sq_ref[...] = jnp.zeros_like(sq_ref)
            dw_ref[...] = jnp.zeros_like(dw_ref)

        y = (jnp.dot(y_ref[0], wo_ref[0:D_MODEL, :], preferred_element_type=F32)
             + jnp.dot(y_ref[1], wo_ref[D_MODEL:2 * D_MODEL, :], preferred_element_type=F32))
        w = w_ref[...]
        rstd = lax.rsqrt(jnp.mean(y * y, axis=-1, keepdims=True) + EPS)
        n = y * rstd
        err = (x_ref[...] + n * w) - t_ref[...]
        sq_ref[...] += jnp.sum(err * err, axis=0, keepdims=True)
        dout = err * inv_d
        dout_ref[...] = dout
        dw_ref[...] += jnp.sum(dout * n, axis=0, keepdims=True)
        dn = dout * w
        dy_ref[...] = (rstd * (dn - n * jnp.mean(dn * n, axis=-1, keepdims=True))).astype(_MXU_DTYPE)

    row = pl.BlockSpec((tm, D_MODEL), lambda i: (i, 0))
    return pl.pallas_call(
        body, name="out_fused", grid=(t // tm,),
        in_specs=[pl.BlockSpec((2, tm, D_MODEL), lambda i: (0, i, 0)), _full((2 * D_MODEL, D_MODEL)),
                  row, row, _full((1, D_MODEL))],
        out_specs=[row, row, _full((1, D_MODEL)), _full((1, D_MODEL))],
        out_shape=[jax.ShapeDtypeStruct((t, D_MODEL), F32), jax.ShapeDtypeStruct((t, D_MODEL), _MXU_DTYPE),
                   jax.ShapeDtypeStruct((1, D_MODEL), F32), jax.ShapeDtypeStruct((1, D_MODEL), F32)],
        compiler_params=_seq_params(),
    )(y2, wo, x, tgt, wpost)


def _dw_out(y2, dy, tt):
    t = dy.shape[0]
    hr = UNIT // 2

    def body(y_ref, dy_ref, g_ref):
        @pl.when(pl.program_id(1) == 0)
        def _():
            g_ref[...] = jnp.zeros_like(g_ref)

        r = _dot_tn(y_ref[...], dy_ref[...])
        g_ref[0] += r[0:hr]
        g_ref[1] += r[hr:UNIT]

    return pl.pallas_call(
        body, name="dw_out", grid=(N_CHIPS, t // tt),
        in_specs=[pl.BlockSpec((None, tt, UNIT), lambda c, k: (c // 2, k, c % 2)),
                  pl.BlockSpec((tt, D_MODEL), lambda c, k: (k, 0))],
        out_specs=pl.BlockSpec((2, None, hr, D_MODEL), lambda c, k: (0, c, 0, 0)),
        out_shape=jax.ShapeDtypeStruct((2, N_CHIPS, hr, D_MODEL), F32),
        compiler_params=pltpu.CompilerParams(dimension_semantics=("arbitrary", "arbitrary")),
    )(y2, dy)


def _lru_bwd(dy, wo, p, h, wa, wx, ba, bx, lam, cw, cb, tm):
    t = dy.shape[0]
    nt = t // tm
    ng = tm // 8

    def body(dy_ref, wo_ref, lx_ref, halo_ref, lg_ref, h_ref, hhalo_ref, wa_ref, wx_ref, ba_ref, bx_ref,
             lam_ref, cw_ref, cb_ref,
             dp_ref, dwa_ref, dwx_ref, dba_ref, dbx_ref, dsp_ref, dcw_ref, dcb_ref,
             c_s, in_s, dh_s, dhc_s, afirst_s, dxc_s):
        i = pl.program_id(0)
        first = i == nt - 1

        @pl.when(i == 0)
        def _():
            for ref in (dwa_ref, dwx_ref, dba_ref, dbx_ref, dsp_ref, dcw_ref, dcb_ref, dhc_s, afirst_s, dxc_s):
                ref[...] = jnp.zeros_like(ref)

        sp = _softplus_neg(lam_ref[...])
        cwv = cw_ref[...]
        wav = wa_ref[...]
        wxv = wx_ref[...]
        halo = jnp.where(first, 0.0, halo_ref[...])
        xs, xc, r, ig, a, mult = _lru_gates(lx_ref[...], halo, cwv, cb_ref[...], wav, wxv,
                                            ba_ref[...], bx_ref[...], sp)
        d_y = _dot_nt(dy_ref[...], wo_ref[...])
        lg = lg_ref[...]
        sl = _sigmoid(lg)
        hv = h_ref[...]
        d_lg = d_y * hv * (sl * (1.0 + lg * (1.0 - sl)))

        c_s[...] = _shift_up(a, afirst_s[...], 1)
        in_s[...] = d_y * (lg * sl)
        row8 = _rows((8, D_MODEL))

        def group(gg, carry):
            rows = pl.ds(pl.multiple_of((ng - 1 - gg) * 8, 8), 8)
            cv = c_s[rows, :]
            uv = in_s[rows, :]
            for s in (1, 2, 4):
                c_sh = jnp.where(row8 < 8 - s, pltpu.roll(cv, 8 - s, 0), 1.0)
                u_sh = jnp.where(row8 < 8 - s, pltpu.roll(uv, 8 - s, 0), 0.0)
                uv = cv * u_sh + uv
                cv = cv * c_sh
            hh = cv * carry + uv
            dh_s[rows, :] = hh
            return jnp.broadcast_to(hh[0:1, :], (8, D_MODEL))

        dhc_s[...] = lax.fori_loop(0, ng, group, dhc_s[...])
        afirst_s[...] = a[0:8]
        dh = dh_s[...]

        h_prev = _shift_down(hv, jnp.where(first, 0.0, hhalo_ref[...]), 1)
        gx = ig * xc
        d_mult = dh * gx
        d_i = dh * (mult * xc)
        d_xc = dh * (mult * ig)
        d_la = (dh * h_prev) * a - d_mult * (a * a) / mult
        dsp_ref[...] += jnp.sum(d_la * r, axis=0, keepdims=True) * (-LRU_C)
        d_zr = (d_la * (-LRU_C * sp)) * (r * (1.0 - r))
        d_zi = d_i * (ig * (1.0 - ig))
        dba_ref[...] += jnp.sum(d_zr, axis=0, keepdims=True)
        dbx_ref[...] += jnp.sum(d_zi, axis=0, keepdims=True)
        back = []
        for n in range(LRU_BLOCKS):
            ln = slice(n * LRU_BW, (n + 1) * LRU_BW)
            xb = _mx(xc[:, ln])
            zr_n = _mx(d_zr[:, ln])
            zi_n = _mx(d_zi[:, ln])
            dwa_ref[n] += _dot_tn(xb, zr_n)
            dwx_ref[n] += _dot_tn(xb, zi_n)
            back.append(_dot_nt(zr_n, wav[n]) + _dot_nt(zi_n, wxv[n]))
        d_xc = d_xc + jnp.concatenate(back, axis=1)
        dcb_ref[...] += jnp.sum(d_xc, axis=0, keepdims=True)
        for k in range(4):
            dcw_ref[k:k + 1, :] += jnp.sum(d_xc * xs[k], axis=0, keepdims=True)
        head = dxc_s[...]
        d_lx = cwv[3:4] * d_xc
        for k in range(3):
            d_lx = d_lx + cwv[k:k + 1] * _shift_up(d_xc, head, 3 - k)
        dxc_s[...] = d_xc[0:8]
        dp_ref[0] = d_lx.astype(_MXU_DTYPE)
        dp_ref[1] = d_lg.astype(_MXU_DTYPE)

    rev = lambda i: nt - 1 - i
    seg = lambda s: pl.BlockSpec((None, tm, D_MODEL), lambda i: (s, rev(i), 0))
    halo_row = lambda i: jnp.maximum(rev(i) * (tm // 8) - 1, 0)
    row = pl.BlockSpec((tm, D_MODEL), lambda i: (rev(i), 0))
    vec = _full((1, D_MODEL))
    wblk = _full((LRU_BLOCKS, LRU_BW, LRU_BW))
    return pl.pallas_call(
        body, name="lru_bwd", grid=(nt,),
        in_specs=[row, pl.BlockSpec((D_MODEL, D_MODEL), lambda i: (0, 0)),
                  seg(0), pl.BlockSpec((None, 8, D_MODEL), lambda i: (0, halo_row(i), 0)), seg(1),
                  row, pl.BlockSpec((8, D_MODEL), lambda i: (halo_row(i), 0)),
                  wblk, wblk, vec, vec, vec, _full((4, D_MODEL)), vec],
        out_specs=[pl.BlockSpec((2, tm, D_MODEL), lambda i: (2, rev(i), 0)),
                   wblk, wblk, vec, vec, vec, _full((4, D_MODEL)), vec],
        out_shape=[jax.ShapeDtypeStruct((N_SEG, t, D_MODEL), _MXU_DTYPE),
                   jax.ShapeDtypeStruct((LRU_BLOCKS, LRU_BW, LRU_BW), F32),
                   jax.ShapeDtypeStruct((LRU_BLOCKS, LRU_BW, LRU_BW), F32),
                   jax.ShapeDtypeStruct((1, D_MODEL), F32), jax.ShapeDtypeStruct((1, D_MODEL), F32),
                   jax.ShapeDtypeStruct((1, D_MODEL), F32), jax.ShapeDtypeStruct((4, D_MODEL), F32),
                   jax.ShapeDtypeStruct((1, D_MODEL), F32)],
        scratch_shapes=[pltpu.VMEM((tm, D_MODEL), F32), pltpu.VMEM((tm, D_MODEL), F32),
                        pltpu.VMEM((tm, D_MODEL), F32), pltpu.VMEM((8, D_MODEL), F32),
                        pltpu.VMEM((8, D_MODEL), F32), pltpu.VMEM((8, D_MODEL), F32)],
        compiler_params=_seq_params(),
    )(dy, wo, p, p, p, h, h, wa, wx, ba, bx, lam, cw, cb)


def _hgrn_bwd(dy, wo, p, o, states, logits, gw, dp, tm):
    t = dy.shape[0]
    nt = t // tm
    nc = tm // CHUNK

    def body(dy_ref, wo_ref, qf_ref, vh_ref, o_ref, st_ref, lg_ref, gw_ref, dp_in,
             dp_ref, dgw_ref, dlb_ref, dyh_s, dstate, snext):
        del dp_in
        i = pl.program_id(0)

        @pl.when(i == 0)
        def _():
            for ref in (dgw_ref, dlb_ref, dstate, snext):
                ref[...] = jnp.zeros_like(ref)

        dyh_s[...] = _dot_nt(dy_ref[...], wo_ref[...])
        l0 = lg_ref[0:1, :]
        l1 = lg_ref[1:2, :]
        lb = 1.0 / (1.0 + jnp.exp(l1 - l0))
        gwv = gw_ref[...]
        rr = _rows((CHUNK, CHUNK))
        cc = _cols((CHUNK, CHUNK))
        causal = rr >= cc
        tri = causal.astype(BF16)
        tri_up = (rr <= cc).astype(BF16)

        def chunk(c2, carry):
            c = nc - 1 - c2
            rows = pl.ds(pl.multiple_of(c * CHUNK, CHUNK), CHUNK)
            q = qf_ref[0, rows, :]
            fr = qf_ref[1, rows, :]
            v = vh_ref[0, rows, :]
            hg = vh_ref[1, rows, :]
            ov = o_ref[rows, :]
            dyh = dyh_s[rows, :]
            sig, f, k, sq, qs, eb, em1, em2, ee, ebc = _hgrn_prep(q, fr, lb, tri)
            qi, qm, km, ke = _mx(qs * eb), _mx(qs * em1), _mx(k * em2), _mx(k * ee)
            sg = _sigmoid(hg)
            gate = hg * sg
            d_on = dyh * gate
            dqs_p, dk_p, dv_p, dhg_p, db_p, ex_p = [], [], [], [], [], []
            dgw_acc = []
            for h in range(HEADS):
                ln = slice(h * HEAD_D, (h + 1) * HEAD_D)
                o_h = ov[:, ln]
                rstd = lax.rsqrt(jnp.mean(o_h * o_h, axis=-1, keepdims=True) + EPS)
                n = o_h * rstd
                dhg_p.append(dyh[:, ln] * (n * gwv[:, ln]) * (sg[:, ln] * (1.0 + hg[:, ln] * (1.0 - sg[:, ln]))))
                dgw_acc.append(jnp.sum(d_on[:, ln] * n, axis=0, keepdims=True))
                dn = d_on[:, ln] * gwv[:, ln]
                d_o = rstd * (dn - n * jnp.mean(dn * n, axis=-1, keepdims=True))
                s0 = st_ref[c, h]
                ds = dstate[h]
                att = jnp.where(causal, _dot_nt(qm[:, ln], km[:, ln]), 0.0)
                d_att = jnp.where(causal, _dot_nt(d_o, v[:, ln]), 0.0)
                dv_p.append(_dot_tn(att, d_o) + _dot_nt(ke[:, ln], ds))
                d_qm = _dot(d_att, km[:, ln])
                d_km = _dot_tn(d_att, qm[:, ln])
                d_qi = _dot(d_o, s0)
                d_ke = _dot(v[:, ln], ds)
                dqs_h = d_qi * eb[:, ln] + d_qm * em1[:, ln]
                dk_h = d_km * em2[:, ln] + d_ke * ee[:, ln]
                dqs_p.append(dqs_h)
                dk_p.append(dk_h)
                db_p.append((qi[:, ln].astype(F32) * d_qi + qm[:, ln].astype(F32) * d_qm)
                            - (km[:, ln].astype(F32) * d_km + ke[:, ln].astype(F32) * d_ke))
                ex_p.append(jnp.sum(ds * snext[h], axis=0, keepdims=True))
                dstate[h] = ds * ebc[:, ln] + _dot_tn(d_o, qi[:, ln])
                snext[h] = s0
            cat = lambda parts: jnp.concatenate(parts, axis=1)
            dqs, dk, db = cat(dqs_p), cat(dk_p), cat(db_p)
            dg = _tri_sum(tri_up, db) + cat(ex_p)
            df = dg / f - dk
            dgw_ref[...] += cat(dgw_acc)
            dlb_ref[...] += jnp.sum(df * (1.0 - sig), axis=0, keepdims=True)
            dp_ref[0, rows, :] = (dqs * (sq * (1.0 + q * (1.0 - sq)))).astype(_MXU_DTYPE)
            dp_ref[1, rows, :] = (df * ((1.0 - lb) * sig * (1.0 - sig))).astype(_MXU_DTYPE)
            dp_ref[2, rows, :] = cat(dv_p).astype(_MXU_DTYPE)
            dp_ref[3, rows, :] = cat(dhg_p).astype(_MXU_DTYPE)
            return carry

        lax.fori_loop(0, nc, chunk, 0)

    rev = lambda i: nt - 1 - i
    pair = lambda s: pl.BlockSpec((2, tm, D_MODEL), lambda i: (s, rev(i), 0))
    row = pl.BlockSpec((tm, D_MODEL), lambda i: (rev(i), 0))
    vec = _full((1, D_MODEL))
    return pl.pallas_call(
        body, name="hgrn_bwd", grid=(nt,),
        in_specs=[row, pl.BlockSpec((D_MODEL, D_MODEL), lambda i: (1, 0)), pair(1), pair(2), row,
                  pl.BlockSpec((nc, HEADS, HEAD_D, HEAD_D), lambda i: (rev(i), 0, 0, 0)),
                  _full((2, D_MODEL)), vec, pl.BlockSpec(memory_space=pl.ANY)],
        out_specs=[pl.BlockSpec((4, tm, D_MODEL), lambda i: (0, rev(i), 0)), vec, vec],
        out_shape=[jax.ShapeDtypeStruct((N_SEG, t, D_MODEL), _MXU_DTYPE),
                   jax.ShapeDtypeStruct((1, D_MODEL), F32), jax.ShapeDtypeStruct((1, D_MODEL), F32)],
        scratch_shapes=[pltpu.VMEM((tm, D_MODEL), F32), pltpu.VMEM((HEADS, HEAD_D, HEAD_D), F32),
                        pltpu.VMEM((HEADS, HEAD_D, HEAD_D), F32)],
        input_output_aliases={8: 0},
        compiler_params=_seq_params(),
    )(dy, wo, p, p, o, states, logits, gw, dp)


def _dp_unit(m):
    return ((m // 2 + 2) % N_SEG) * 2 + m % 2


def _dw_in(ut, dp, tt):
    t = ut.shape[1]
    hr = D_MODEL // 2

    def body(ut_ref, dp_ref, g_ref):
        @pl.when(pl.program_id(1) == 0)
        def _():
            g_ref[...] = jnp.zeros_like(g_ref)

        r = jnp.dot(ut_ref[...], dp_ref[...], preferred_element_type=F32)
        for h in range(2):
            for half in range(2):
                g_ref[h, half] += r[h * hr:(h + 1) * hr, half * UNIT:(half + 1) * UNIT]

    return pl.pallas_call(
        body, name="dw_in", grid=(N_SEG, t // tt),
        in_specs=[pl.BlockSpec((D_MODEL, tt), lambda s, k: (0, k)),
                  pl.BlockSpec((None, tt, D_MODEL), lambda s, k: (s, k, 0))],
        out_specs=pl.BlockSpec((2, 2, hr, UNIT), lambda s, k: (0, (s + 2) % N_SEG, 0, 0)),
        out_shape=jax.ShapeDtypeStruct((2, N_UNIT, hr, UNIT), F32),
        compiler_params=pltpu.CompilerParams(dimension_semantics=("arbitrary", "arbitrary")),
    )(ut, dp)


def _dx(dp, wg, x, dout, wpre, tm):
    t = x.shape[0]

    def body(dp_ref, w_ref, x_ref, dout_ref, wpre_ref, gx_ref, dw_ref):
        @pl.when(pl.program_id(0) == 0)
        def _():
            dw_ref[...] = jnp.zeros_like(dw_ref)

        du = None
        for m in range(N_UNIT):
            part = _dot_nt(dp_ref[m // 2, :, (m % 2) * UNIT:(m % 2 + 1) * UNIT], w_ref[_dp_unit(m)])
            du = part if du is None else du + part
        xv = x_ref[...]
        rstd = lax.rsqrt(jnp.mean(xv * xv, axis=-1, keepdims=True) + EPS)
        n = xv * rstd
        dw_ref[...] += jnp.sum(du * n, axis=0, keepdims=True)
        dn = du * wpre_ref[...]
        gx_ref[...] = dout_ref[...] + rstd * (dn - n * jnp.mean(dn * n, axis=-1, keepdims=True))

    row = pl.BlockSpec((tm, D_MODEL), lambda i: (i, 0))
    return pl.pallas_call(
        body, name="dx", grid=(t // tm,),
        in_specs=[pl.BlockSpec((N_SEG, tm, D_MODEL), lambda i: (0, i, 0)), _resident((N_UNIT, D_MODEL, UNIT)),
                  row, row, _full((1, D_MODEL))],
        out_specs=[row, _full((1, D_MODEL))],
        out_shape=[jax.ShapeDtypeStruct((t, D_MODEL), F32), jax.ShapeDtypeStruct((1, D_MODEL), F32)],
        compiler_params=_seq_params(),
    )(dp, wg, x, dout, wpre)


CHUNK_BYTES = 256 * 1024
MAX_CHUNKS = 16


def _n_chunks(shape, dtype, rows):
    nbytes = jnp.dtype(dtype).itemsize
    for d in shape:
        nbytes *= d
    k = max(1, min(MAX_CHUNKS, rows, nbytes // CHUNK_BYTES))
    while rows % k:
        k -= 1
    return k


def _gather_weights(shards, split):
    n = len(shards)
    ks = [_n_chunks(a.shape[1:], a.dtype, a.shape[1]) for a in shards]
    offs = [sum(ks[:a]) for a in range(n)]
    total = sum(ks)

    def body(*refs):
        ins, outs = refs[:n], refs[n:2 * n]
        send_sems, recv_sems, fsend_sems, frecv_sems, local_sems = refs[2 * n:]
        x, y, c = lax.axis_index("x"), lax.axis_index("y"), lax.axis_index("c")
        me = 2 * x + y
        peers = [(1 - x, y), (x, 1 - y), (1 - x, 1 - y)]
        half = lambda a: c if split[a] else 0

        def rows(a, k):
            step = shards[a].shape[1] // ks[a]
            return pl.ds(k * step, step)

        def fetch(a, j, k, slot):
            px, py = peers[j]
            sem = 3 * (offs[a] + k) + j
            return pltpu.make_async_remote_copy(
                src_ref=ins[a].at[half(a), rows(a, k)], dst_ref=outs[a].at[slot, half(a), rows(a, k)],
                send_sem=send_sems.at[sem], recv_sem=recv_sems.at[sem],
                device_id=(px, py, c), device_id_type=MESH)

        def forward(a, j, k, h):
            px, py = peers[j]
            sem = 3 * (offs[a] + k) + j
            piece = outs[a].at[2 * px + py, h, rows(a, k)]
            return pltpu.make_async_remote_copy(
                src_ref=piece, dst_ref=piece, send_sem=fsend_sems.at[sem], recv_sem=frecv_sems.at[sem],
                device_id=(x, y, 1 - c), device_id_type=MESH)

        def local(a, h, k):
            return pltpu.make_async_copy(ins[a].at[h, rows(a, k)], outs[a].at[me, h, rows(a, k)],
                                         local_sems.at[2 * (offs[a] + k) + h])

        pieces = [(a, k) for a in range(n) for k in range(ks[a])]
        halves = lambda a: range(2 if split[a] else 1)
        for a, k in pieces:
            for j in range(3):
                fetch(a, j, k, me).start()
        for a, k in pieces:
            for h in halves(a):
                local(a, h, k).start()
        for a, k in pieces:
            for j in range(3):
                px, py = peers[j]
                fetch(a, j, k, 2 * px + py).wait_recv()
                if split[a]:
                    forward(a, j, k, c).start()
        for a, k in pieces:
            for j in range(3):
                if split[a]:
                    forward(a, j, k, 1 - c).wait_recv()
        for a, k in pieces:
            for j in range(3):
                fetch(a, j, k, me).wait_send()
                if split[a]:
                    forward(a, j, k, c).wait_send()
            for h in halves(a):
                local(a, h, k).wait()

    any_spec = pl.BlockSpec(memory_space=pl.ANY)
    return pl.pallas_call(
        body, name="gather_weights",
        in_specs=[any_spec] * n, out_specs=[any_spec] * n,
        out_shape=[jax.ShapeDtypeStruct((N_CHIPS,) + a.shape, a.dtype) for a in shards],
        scratch_shapes=[pltpu.SemaphoreType.DMA((3 * total,)), pltpu.SemaphoreType.DMA((3 * total,)),
                        pltpu.SemaphoreType.DMA((3 * total,)), pltpu.SemaphoreType.DMA((3 * total,)),
                        pltpu.SemaphoreType.DMA((2 * total,))],
    )(*shards)


def _chip_exchange(arrays, scatter, name):
    n = len(arrays)
    slab_shapes = [a.shape[1:] if sc else a.shape for a, sc in zip(arrays, scatter)]
    ks = [_n_chunks(s, a.dtype, s[0]) for s, a in zip(slab_shapes, arrays)]
    offs = [sum(ks[:a]) for a in range(n)]

    def body(*refs):
        ins, outs = refs[:n], refs[n:2 * n]
        send_sems, recv_sems, local_sems = refs[2 * n:]
        x, y, c = lax.axis_index("x"), lax.axis_index("y"), lax.axis_index("c")
        me = 2 * x + y
        peers = [(1 - x, y), (x, 1 - y), (1 - x, 1 - y)]

        def part(a, q):
            return ins[a].at[q] if scatter[a] else ins[a]

        def remote(a, j):
            px, py = peers[j]
            return pltpu.make_async_remote_copy(
                src_ref=part(a, 2 * px + py), dst_ref=outs[a].at[me],
                send_sem=send_sems.at[a * 3 + j], recv_sem=recv_sems.at[a * 3 + j],
                device_id=(px, py, c), device_id_type=MESH)

        def arrival(a, j):
            px, py = peers[j]
            return pltpu.make_async_remote_copy(
                src_ref=part(a, me), dst_ref=outs[a].at[2 * px + py],
                send_sem=send_sems.at[a * 3 + j], recv_sem=recv_sems.at[a * 3 + j],
                device_id=(px, py, c), device_id_type=MESH)

        def local(a, k):
            step = slab_shapes[a][0] // ks[a]
            rows = pl.ds(k * step, step)
            return pltpu.make_async_copy(part(a, me).at[rows], outs[a].at[me, rows], local_sems.at[offs[a] + k])

        pieces = [(a, k) for a in range(n) for k in range(ks[a])]
        for a in range(n):
            for j in range(3):
                remote(a, j).start()
        for a, k in pieces:
            local(a, k).start()
        for a in range(n):
            for j in range(3):
                arrival(a, j).wait_recv()
        for a in range(n):
            for j in range(3):
                remote(a, j).wait_send()
        for a, k in pieces:
            local(a, k).wait()

    any_spec = pl.BlockSpec(memory_space=pl.ANY)
    return pl.pallas_call(
        body, name=name,
        in_specs=[any_spec] * n, out_specs=[any_spec] * n,
        out_shape=[jax.ShapeDtypeStruct((N_CHIPS,) + tuple(s), a.dtype) for s, a in zip(slab_shapes, arrays)],
        scratch_shapes=[pltpu.SemaphoreType.DMA((3 * n,)), pltpu.SemaphoreType.DMA((3 * n,)),
                        pltpu.SemaphoreType.DMA((sum(ks),))],
    )(*arrays)


def _core_swap(arrays, name):
    n = len(arrays)
    ks = [_n_chunks(a.shape[1:], a.dtype, a.shape[1]) for a in arrays]
    offs = [sum(ks[:a]) for a in range(n)]

    def body(*refs):
        ins, outs = refs[:n], refs[n:2 * n]
        send_sems, recv_sems = refs[2 * n:]
        x, y, c = lax.axis_index("x"), lax.axis_index("y"), lax.axis_index("c")

        def copy(a, j):
            step = arrays[a].shape[1] // ks[a]
            rows = pl.ds(j * step, step)
            return pltpu.make_async_remote_copy(
                src_ref=ins[a].at[1 - c, rows], dst_ref=outs[a].at[rows],
                send_sem=send_sems.at[offs[a] + j], recv_sem=recv_sems.at[offs[a] + j],
                device_id=(x, y, 1 - c), device_id_type=MESH)

        pieces = [(a, j) for a in range(n) for j in range(ks[a])]
        for a, j in pieces:
            copy(a, j).start()
        for a, j in pieces:
            copy(a, j).wait_recv()
        for a, j in pieces:
            copy(a, j).wait_send()

    any_spec = pl.BlockSpec(memory_space=pl.ANY)
    return pl.pallas_call(
        body, name=name,
        in_specs=[any_spec] * n, out_specs=[any_spec] * n,
        out_shape=[jax.ShapeDtypeStruct(a.shape[1:], a.dtype) for a in arrays],
        scratch_shapes=[pltpu.SemaphoreType.DMA((sum(ks),)), pltpu.SemaphoreType.DMA((sum(ks),))],
    )(*arrays)


def _core_exchange(arrays, name):
    n = len(arrays)
    ks = [_n_chunks(a.shape, a.dtype, a.shape[0]) for a in arrays]
    offs = [sum(ks[:a]) for a in range(n)]

    def body(*refs):
        ins, outs = refs[:n], refs[n:2 * n]
        send_sems, recv_sems, local_sems = refs[2 * n:]
        x, y, c = lax.axis_index("x"), lax.axis_index("y"), lax.axis_index("c")

        def copy(a, j, slot):
            step = arrays[a].shape[0] // ks[a]
            rows = pl.ds(j * step, step)
            return pltpu.make_async_remote_copy(
                src_ref=ins[a].at[rows], dst_ref=outs[a].at[slot, rows],
                send_sem=send_sems.at[offs[a] + j], recv_sem=recv_sems.at[offs[a] + j],
                device_id=(x, y, 1 - c), device_id_type=MESH)

        def local(a, j):
            step = arrays[a].shape[0] // ks[a]
            rows = pl.ds(j * step, step)
            return pltpu.make_async_copy(ins[a].at[rows], outs[a].at[c, rows], local_sems.at[offs[a] + j])

        pieces = [(a, j) for a in range(n) for j in range(ks[a])]
        for a, j in pieces:
            copy(a, j, c).start()
        for a, j in pieces:
            local(a, j).start()
        for a, j in pieces:
            copy(a, j, 1 - c).wait_recv()
        for a, j in pieces:
            copy(a, j, c).wait_send()
        for a, j in pieces:
            local(a, j).wait()

    any_spec = pl.BlockSpec(memory_space=pl.ANY)
    return pl.pallas_call(
        body, name=name,
        in_specs=[any_spec] * n, out_specs=[any_spec] * n,
        out_shape=[jax.ShapeDtypeStruct((2,) + a.shape, a.dtype) for a in arrays],
        scratch_shapes=[pltpu.SemaphoreType.DMA((sum(ks),)), pltpu.SemaphoreType.DMA((sum(ks),)),
                        pltpu.SemaphoreType.DMA((sum(ks),))],
    )(*arrays)


def _sum_own_half(g, b, name, out_dtype, tr=512):
    _, rows, cols = g.shape
    tr = min(tr, rows)
    core = lax.axis_index("c").reshape(1)

    def body(c_ref, g_ref, b_ref, o_ref):
        del c_ref
        o_ref[...] = (g_ref[...] + b_ref[...]).astype(out_dtype)

    return pl.pallas_call(
        body, name=name,
        grid_spec=pltpu.PrefetchScalarGridSpec(
            num_scalar_prefetch=1, grid=(rows // tr,),
            in_specs=[pl.BlockSpec((None, tr, cols), lambda i, c_ref: (c_ref[0], i, 0)),
                      pl.BlockSpec((tr, cols), lambda i, c_ref: (i, 0))],
            out_specs=pl.BlockSpec((tr, cols), lambda i, c_ref: (i, 0))),
        out_shape=jax.ShapeDtypeStruct((rows, cols), out_dtype),
        compiler_params=pltpu.CompilerParams(dimension_semantics=("parallel",)),
    )(core, g, b)


def _sum_slots(r, name, tr=512):
    s, rows, cols = r.shape
    tr = min(tr, rows)

    def body(r_ref, o_ref):
        acc = r_ref[0].astype(F32)
        for j in range(1, s):
            acc = acc + r_ref[j].astype(F32)
        o_ref[...] = acc

    return pl.pallas_call(
        body, name=name, grid=(rows // tr,),
        in_specs=[pl.BlockSpec((s, tr, cols), lambda i: (0, i, 0))],
        out_specs=pl.BlockSpec((tr, cols), lambda i: (i, 0)),
        out_shape=jax.ShapeDtypeStruct((rows, cols), F32),
        compiler_params=pltpu.CompilerParams(dimension_semantics=("parallel",)),
    )(r)


def _adamw_big(g, w, m, v, name, grid, g_spec, w_spec):
    def body(q_ref, w_ref, m_ref, v_ref, g_ref, d_ref, nm_ref, nv_ref):
        gv = q_ref[...]
        g_ref[...] = gv
        d_ref[...], nm_ref[...], nv_ref[...] = _adamw(w_ref[...], gv, m_ref[...], v_ref[...])

    shp = jax.ShapeDtypeStruct(w.shape, F32)
    return pl.pallas_call(
        body, name=name, grid=grid,
        in_specs=[g_spec, w_spec, w_spec, w_spec], out_specs=[w_spec] * 4, out_shape=[shp] * 4,
        compiler_params=pltpu.CompilerParams(dimension_semantics=("parallel",) * len(grid)),
    )(g, w, m, v)


def _small_finish(q, lam, logits):
    def body(q_ref, lam_ref, lg_ref, o_ref):
        g = q_ref[0] + q_ref[1]
        o_ref[...] = g
        o_ref[8:9, :] = g[8:9, :] * (-_sigmoid(-lam_ref[...]))
        lb = 1.0 / (1.0 + jnp.exp(lg_ref[1:2, :] - lg_ref[0:1, :]))
        d0 = g[9:10, :] * (lb * (1.0 - lb))
        o_ref[9:10, :] = d0
        o_ref[10:11, :] = -d0

    return pl.pallas_call(
        body, name="small_finish",
        in_specs=[_full((2, SMALL_ROWS, D_MODEL)), _full((1, D_MODEL)), _full((2, D_MODEL))],
        out_specs=_full((SMALL_ROWS, D_MODEL)),
        out_shape=jax.ShapeDtypeStruct((SMALL_ROWS, D_MODEL), F32),
    )(q, lam, logits)


def _adamw_small(ws, gs, ms, vs):
    n = len(ws)

    def body(*refs):
        w_r, g_r, m_r, v_r = refs[:n], refs[n:2 * n], refs[2 * n:3 * n], refs[3 * n:4 * n]
        d_o, m_o, v_o = refs[4 * n:5 * n], refs[5 * n:6 * n], refs[6 * n:7 * n]
        for j in range(n):
            d_o[j][...], m_o[j][...], v_o[j][...] = _adamw(w_r[j][...], g_r[j][...], m_r[j][...], v_r[j][...])

    specs = [_full(w.shape) for w in ws]
    shapes = [jax.ShapeDtypeStruct(w.shape, F32) for w in ws]
    outs = pl.pallas_call(
        body, name="adamw_small",
        in_specs=specs * 4, out_specs=specs * 3, out_shape=shapes * 3,
    )(*ws, *gs, *ms, *vs)
    return outs[:n], outs[n:2 * n], outs[2 * n:]


def _local_step(x, tgt, pre_w, wg, cw, cb, wa, wx, ba, bx, lam, logits, gw, wo, post_w, tm, tmm):
    ut, p = _in_proj(x, pre_w, wg, tm)
    h, y2 = _lru_fwd(p, wa, wx, ba, bx, lam, cw, cb, tm)
    y2, o, states = _hgrn_fwd(p, logits, gw, y2, tm)
    dout, dy, sq, d_post = _out_fused(y2, wo, x, tgt, post_w, tmm)
    g_out = _dw_out(y2, dy, tmm)
    dp, d_wa, d_wx, d_ba, d_bx, d_sp, d_cw, d_cb = _lru_bwd(dy, wo, p, h, wa, wx, ba, bx, lam, cw, cb, tm)
    dp, d_gw, d_lb = _hgrn_bwd(dy, wo, p, o, states, logits, gw, dp, tm)
    g_in = _dw_in(ut, dp, min(2048, x.shape[0]))
    grad_x, d_pre = _dx(dp, wg, x, dout, pre_w, tm)
    small = jnp.concatenate([d_pre, d_cw, d_cb, d_ba, d_bx, d_sp, d_lb,
                             jnp.zeros((1, D_MODEL), F32), d_gw, d_post,
                             jnp.zeros((SMALL_ROWS - 13, D_MODEL), F32)], axis=0)
    return sq, grad_x, g_in, g_out, d_wa, d_wx, small


def kernel(x, pre_norm_w, w_in, conv_w, conv_b, lru_w_a, lru_b_a, lru_w_x, lru_b_x, lru_lambda, hgrn_lb_logits, hgrn_gnorm_w, w_out, post_norm_w, loss_target, m_pre_norm_w, m_w_in, m_conv_w, m_conv_b, m_lru_w_a, m_lru_b_a, m_lru_w_x, m_lru_b_x, m_lru_lambda, m_hgrn_lb_logits, m_hgrn_gnorm_w, m_w_out, m_post_norm_w, v_pre_norm_w, v_w_in, v_conv_w, v_conv_b, v_lru_w_a, v_lru_b_a, v_lru_w_x, v_lru_b_x, v_lru_lambda, v_hgrn_lb_logits, v_hgrn_gnorm_w, v_w_out, v_post_norm_w):
    t = x.shape[1]
    tm = min(256, t)
    tmm = min(512, t)
    chip = 2 * lax.axis_index("x") + lax.axis_index("y")

    win_units = w_in[0].astype(_MXU_DTYPE).reshape(D_MODEL, 3, UNIT).transpose(1, 0, 2)
    small_w = jnp.concatenate([conv_w[0], lru_b_a[0], lru_b_x[0]], axis=1)
    wg4, wo4, wa4, wx4, sw4 = _gather_weights(
        [win_units.reshape(2, 3 * 512, UNIT), w_out[0].astype(_MXU_DTYPE).reshape(2, 256, D_MODEL),
         lru_w_a[0].astype(_MXU_DTYPE).reshape(2, 128, LRU_BW), lru_w_x[0].astype(_MXU_DTYPE).reshape(2, 128, LRU_BW),
         small_w.reshape(1, 4, 384)],
        [True, True, True, True, False])
    wg = wg4.reshape(N_UNIT, D_MODEL, UNIT)
    wo = wo4.reshape(2 * D_MODEL, D_MODEL)
    by_block = lambda w4: w4.reshape(N_CHIPS, LRU_BLOCKS, 64, LRU_BW).transpose(1, 0, 2, 3).reshape(
        LRU_BLOCKS, LRU_BW, LRU_BW)
    wa, wx = by_block(wa4), by_block(wx4)
    sw4 = sw4.reshape(N_CHIPS, 4, 384)
    cw = sw4[:, :, 0:256].transpose(1, 0, 2).reshape(4, D_MODEL)
    ba = sw4[:, :, 256:320].transpose(1, 0, 2).reshape(1, D_MODEL)
    bx = sw4[:, :, 320:384].transpose(1, 0, 2).reshape(1, D_MODEL)

    sq, grad_x, g_in, g_out, d_wa, d_wx, small = _local_step(
        x[0], loss_target[0], pre_norm_w, wg, cw, conv_b, wa, wx, ba, bx, lru_lambda, hgrn_lb_logits,
        hgrn_gnorm_w, wo, post_norm_w, tm, tmm)
    loss = lax.psum(jnp.sum(sq) * (0.5 / D_MODEL), ("x", "y", "c"))

    def lru_layout(g):
        g = g.reshape(2, 2, N_CHIPS, 64, LRU_BW).transpose(0, 2, 1, 3, 4)
        return g.reshape(2, N_CHIPS * 128, LRU_BW)

    g_in = g_in.reshape(2, N_UNIT * 512, UNIT)
    g_out = g_out.reshape(2, N_CHIPS * 256, D_MODEL)
    g_wa, g_wx = lru_layout(d_wa), lru_layout(d_wx)
    b_in, b_out, b_wa, b_wx = _core_swap([g_in, g_out, g_wa, g_wx], "reduce_swap")
    p_in = _sum_own_half(g_in, b_in, "presum_in", _WIRE_DTYPE)
    p_out = _sum_own_half(g_out, b_out, "presum_out", _WIRE_DTYPE)
    p_wa = _sum_own_half(g_wa, b_wa, "presum_wa", _WIRE_DTYPE)
    p_wx = _sum_own_half(g_wx, b_wx, "presum_wx", _WIRE_DTYPE)
    r_in, r_out, r_wa, r_wx, r_small = _chip_exchange(
        [p_in.reshape(N_CHIPS, 3 * 512, UNIT), p_out.reshape(N_CHIPS, 256, D_MODEL),
         p_wa.reshape(N_CHIPS, 128, LRU_BW), p_wx.reshape(N_CHIPS, 128, LRU_BW), small],
        [True, True, True, True, False], "reduce_chips")
    s_in = _sum_slots(r_in, "sum_in")
    s_out = _sum_slots(r_out, "sum_out")
    s_wa = _sum_slots(r_wa, "sum_wa")
    s_wx = _sum_slots(r_wx, "sum_wx")
    s_small = _sum_slots(r_small, "sum_small")
    q_in, q_out, q_wa, q_wx, q_small = _core_exchange([s_in, s_out, s_wa, s_wx, s_small], "reduce_cores")

    g_w_in, d_w_in, nm_w_in, nv_w_in = _adamw_big(
        q_in.reshape(2, 3, 512, UNIT), w_in[0], m_w_in[0], v_w_in[0], "adamw_w_in", (3, 2),
        pl.BlockSpec((None, None, 512, UNIT), lambda k, h: (h, k, 0, 0)),
        pl.BlockSpec((512, UNIT), lambda k, h: (h, k)))
    g_w_out, d_w_out, nm_w_out, nv_w_out = _adamw_big(
        q_out, w_out[0], m_w_out[0], v_w_out[0], "adamw_w_out", (2,),
        pl.BlockSpec((None, 256, D_MODEL), lambda h: (h, 0, 0)), pl.BlockSpec((256, D_MODEL), lambda h: (h, 0)))
    sq2 = lambda a: a.reshape(LRU_BW, LRU_BW)
    lru_specs = ((2,), pl.BlockSpec((None, 128, LRU_BW), lambda h: (h, 0, 0)),
                 pl.BlockSpec((128, LRU_BW), lambda h: (h, 0)))
    g_wa, d_wa2, nm_wa, nv_wa = _adamw_big(q_wa, sq2(lru_w_a), sq2(m_lru_w_a), sq2(v_lru_w_a), "adamw_wa", *lru_specs)
    g_wx, d_wx2, nm_wx, nv_wx = _adamw_big(q_wx, sq2(lru_w_x), sq2(m_lru_w_x), sq2(v_lru_w_x), "adamw_wx", *lru_specs)

    gs = _small_finish(q_small, lru_lambda, hgrn_lb_logits)
    g_pre = gs[0:1]
    g_cw = lax.dynamic_slice(gs[1:5], (0, chip * 256), (4, 256))
    g_cb = gs[5:6]
    g_ba = lax.dynamic_slice(gs[6].reshape(LRU_BLOCKS, N_CHIPS, 64), (0, chip, 0), (LRU_BLOCKS, 1, 64)).reshape(4, 64)
    g_bx = lax.dynamic_slice(gs[7].reshape(LRU_BLOCKS, N_CHIPS, 64), (0, chip, 0), (LRU_BLOCKS, 1, 64)).reshape(4, 64)
    g_lam = gs[8:9]
    g_lb = gs[9:11]
    g_gw = gs[11:12]
    g_post = gs[12:13]
    small_g = [g_pre, g_cw, g_cb, g_ba, g_bx, g_lam, g_lb, g_gw, g_post]
    two_d = lambda a: a.reshape(a.shape[-2:])
    small_w_list = [pre_norm_w, conv_w, conv_b, lru_b_a, lru_b_x, lru_lambda, hgrn_lb_logits, hgrn_gnorm_w, post_norm_w]
    small_m_list = [m_pre_norm_w, m_conv_w, m_conv_b, m_lru_b_a, m_lru_b_x, m_lru_lambda, m_hgrn_lb_logits,
                    m_hgrn_gnorm_w, m_post_norm_w]
    small_v_list = [v_pre_norm_w, v_conv_w, v_conv_b, v_lru_b_a, v_lru_b_x, v_lru_lambda, v_hgrn_lb_logits,
                    v_hgrn_gnorm_w, v_post_norm_w]
    sd, sm, sv = _adamw_small([two_d(a) for a in small_w_list], small_g,
                              [two_d(a) for a in small_m_list], [two_d(a) for a in small_v_list])

    def shaped(vals, refs):
        return [val.reshape(ref.shape) for val, ref in zip(vals, refs)]

    s_g = shaped(small_g, small_w_list)
    s_d = shaped(sd, small_w_list)
    s_m = shaped(sm, small_w_list)
    s_v = shaped(sv, small_w_list)

    def ordered(small, big_in, big_wa, big_wx, big_out):
        pre, cw_, cb_, ba_, bx_, lam_, lb_, gw_, post_ = small
        return [pre, big_in.reshape(w_in.shape), cw_, cb_, big_wa.reshape(lru_w_a.shape), ba_,
                big_wx.reshape(lru_w_x.shape), bx_, lam_, lb_, gw_, big_out.reshape(w_out.shape), post_]

    grads = ordered(s_g, g_w_in, g_wa, g_wx, g_w_out)
    deltas = ordered(s_d, d_w_in, d_wa2, d_wx2, d_w_out)
    new_m = ordered(s_m, nm_w_in, nm_wa, nm_wx, nm_w_out)
    new_v = ordered(s_v, nv_w_in, nv_wa, nv_wx, nv_w_out)
    return (loss, grad_x.reshape(x.shape), *grads, *deltas, *new_m, *new_v)
```

```python
import functools

import jax
import jax.numpy as jnp
from jax import lax
from jax.experimental import pallas as pl
from jax.experimental.pallas import tpu as pltpu

F32 = jnp.float32
BF16 = jnp.bfloat16
_MXU_DTYPE = jnp.bfloat16
_WIRE_DTYPE = jnp.bfloat16

D_MODEL = 1024
N_SEG = 6
UNIT = 512
N_UNIT = 12
LRU_BLOCKS = 4
LRU_BW = 256
LRU_C = 8.0
HEADS = 8
HEAD_D = 128
CHUNK = 64
EPS = 1e-6
N_CHIPS = 4
EXP_CLAMP = 80.0

ADAM_LR = 0.001
ADAM_B1 = 0.9
ADAM_B2 = 0.999
ADAM_EPS = 1e-08
ADAM_WD = 0.01
ADAM_STEP = 10

SMALL_ROWS = 16
MESH = pl.DeviceIdType.MESH


def _sigmoid(x):
    return 1.0 / (1.0 + jnp.exp(-x))


def _mx(x):
    return x.astype(_MXU_DTYPE)


def _dot(a, b):
    return jnp.dot(_mx(a), _mx(b), preferred_element_type=F32)


def _dot_nt(a, b):
    return lax.dot_general(_mx(a), _mx(b), (((1,), (1,)), ((), ())), preferred_element_type=F32)


def _dot_tn(a, b):
    return lax.dot_general(_mx(a), _mx(b), (((0,), (0,)), ((), ())), preferred_element_type=F32)


def _rows(shape):
    return lax.broadcasted_iota(jnp.int32, shape, 0)


def _cols(shape):
    return lax.broadcasted_iota(jnp.int32, shape, 1)


def _softplus_neg(lam):
    z = -lam
    e = jnp.exp(-jnp.abs(z))
    series = e * (1.0 - e * (0.5 - e * (1.0 / 3.0 - 0.25 * e)))
    return jnp.maximum(z, 0.0) + jnp.where(e < 1e-2, series, jnp.log(1.0 + e))


def _neg_expm1(y):
    series = -y * (1.0 + y * (0.5 + y * (1.0 / 6.0 + y * (1.0 / 24.0))))
    return jnp.where(y > -0.03, series, 1.0 - jnp.exp(y))


def _split3(x):
    hi = x.astype(BF16)
    r1 = x - hi.astype(F32)
    mid = r1.astype(BF16)
    lo = (r1 - mid.astype(F32)).astype(BF16)
    return hi, mid, lo


def _tri_sum(tri, x):
    hi, mid, lo = _split3(x)
    d = lambda p: jnp.dot(tri, p, preferred_element_type=F32)
    return d(hi) + d(mid) + d(lo)


def _shift_down(x, halo, s):
    if s == 0:
        return x
    r = pltpu.roll(x, s, 0)
    top = jnp.where(_rows(halo.shape) < s, pltpu.roll(halo, s, 0), r[0:8])
    return jnp.concatenate([top, r[8:]], axis=0)


def _shift_up(x, head, s):
    if s == 0:
        return x
    n = x.shape[0]
    r = pltpu.roll(x, n - s, 0)
    bottom = jnp.where(_rows(head.shape) >= 8 - s, pltpu.roll(head, 8 - s, 0), r[n - 8:n])
    return jnp.concatenate([r[:n - 8], bottom], axis=0)


def _lru_gates(lx, halo, cw, cb, wa, wx, ba, bx, sp):
    xs = [_shift_down(lx, halo, 3 - k) for k in range(4)]
    xc = cb + cw[0:1] * xs[0] + cw[1:2] * xs[1] + cw[2:3] * xs[2] + cw[3:4] * xs[3]
    zr, zi = [], []
    for n in range(LRU_BLOCKS):
        xb = _mx(xc[:, n * LRU_BW:(n + 1) * LRU_BW])
        zr.append(jnp.dot(xb, wa[n], preferred_element_type=F32))
        zi.append(jnp.dot(xb, wx[n], preferred_element_type=F32))
    r = _sigmoid(jnp.concatenate(zr, axis=1) + ba)
    ig = _sigmoid(jnp.concatenate(zi, axis=1) + bx)
    la = (-LRU_C * sp) * r
    a = jnp.exp(la)
    mult = jnp.sqrt(_neg_expm1(2.0 * la))
    return xs, xc, r, ig, a, mult


def _hgrn_prep(q, fr, lb, tri):
    sig = _sigmoid(fr)
    f = lb + (1.0 - lb) * sig
    k = 1.0 - f
    sq = _sigmoid(q)
    qs = q * sq
    b = _tri_sum(tri, jnp.log(f))
    bm = b[CHUNK // 2 - 1:CHUNK // 2]
    bc = b[CHUNK - 1:CHUNK]
    eb = jnp.exp(b)
    em1 = jnp.exp(jnp.minimum(b - bm, EXP_CLAMP))
    em2 = jnp.exp(jnp.minimum(bm - b, EXP_CLAMP))
    ee = jnp.exp(bc - b)
    ebc = jnp.exp(bc)
    return sig, f, k, sq, qs, eb, em1, em2, ee, ebc


def _adamw(w, g, m, v):
    m = ADAM_B1 * m + (1.0 - ADAM_B1) * g
    v = ADAM_B2 * v + (1.0 - ADAM_B2) * jnp.square(g)
    m_hat = m / (1.0 - ADAM_B1 ** ADAM_STEP)
    v_hat = v / (1.0 - ADAM_B2 ** ADAM_STEP)
    delta = -ADAM_LR * (m_hat / (jnp.sqrt(v_hat) + ADAM_EPS) + ADAM_WD * w)
    return delta, m, v


def _full(shape):
    nd = len(shape)
    return pl.BlockSpec(shape, lambda *_: (0,) * nd)


def _seq_params():
    return pltpu.CompilerParams(dimension_semantics=("arbitrary",))


def _resident(shape):
    nd = len(shape)
    return pl.BlockSpec(shape, lambda *_: (0,) * nd, pipeline_mode=pl.Buffered(1))


def _in_proj(x, w, wg, tm):
    t = x.shape[0]

    def body(x_ref, w_ref, wg_ref, ut_ref, p_ref):
        xv = x_ref[...]
        rstd = lax.rsqrt(jnp.mean(xv * xv, axis=-1, keepdims=True) + EPS)
        u = xv * rstd * w_ref[...]
        ut_ref[...] = u.T.astype(_MXU_DTYPE)
        ub = _mx(u)
        for m in range(N_UNIT):
            p_ref[m // 2, :, (m % 2) * UNIT:(m % 2 + 1) * UNIT] = jnp.dot(
                ub, wg_ref[m], preferred_element_type=F32)

    return pl.pallas_call(
        body, name="in_proj", grid=(t // tm,),
        in_specs=[pl.BlockSpec((tm, D_MODEL), lambda i: (i, 0)), _full((1, D_MODEL)),
                  _resident((N_UNIT, D_MODEL, UNIT))],
        out_specs=[pl.BlockSpec((D_MODEL, tm), lambda i: (0, i)),
                   pl.BlockSpec((N_SEG, tm, D_MODEL), lambda i: (0, i, 0))],
        out_shape=[jax.ShapeDtypeStruct((D_MODEL, t), _MXU_DTYPE),
                   jax.ShapeDtypeStruct((N_SEG, t, D_MODEL), F32)],
        compiler_params=pltpu.CompilerParams(dimension_semantics=("parallel",)),
    )(x, w, wg)


def _lru_fwd(p, wa, wx, ba, bx, lam, cw, cb, tm):
    t = p.shape[1]
    ng = tm // 8

    def body(lx_ref, halo_ref, lg_ref, wa_ref, wx_ref, ba_ref, bx_ref, lam_ref, cw_ref, cb_ref,
             h_ref, y_ref, a_s, u_s, hc_s):
        i = pl.program_id(0)

        @pl.when(i == 0)
        def _():
            hc_s[...] = jnp.zeros_like(hc_s)

        sp = _softplus_neg(lam_ref[...])
        halo = jnp.where(i == 0, 0.0, halo_ref[...])
        _, xc, _, ig, a, mult = _lru_gates(lx_ref[...], halo, cw_ref[...], cb_ref[...], wa_ref[...],
                                           wx_ref[...], ba_ref[...], bx_ref[...], sp)
        a_s[...] = a
        u_s[...] = mult * (ig * xc)
        row8 = _rows((8, D_MODEL))

        def group(g, hc):
            rows = pl.ds(pl.multiple_of(g * 8, 8), 8)
            av = a_s[rows, :]
            uv = u_s[rows, :]
            for s in (1, 2, 4):
                a_sh = jnp.where(row8 >= s, pltpu.roll(av, s, 0), 1.0)
                u_sh = jnp.where(row8 >= s, pltpu.roll(uv, s, 0), 0.0)
                uv = av * u_sh + uv
                av = av * a_sh
            hh = av * hc + uv
            h_ref[rows, :] = hh
            return jnp.broadcast_to(hh[7:8, :], (8, D_MODEL))

        hc_s[...] = lax.fori_loop(0, ng, group, hc_s[...])
        lg = lg_ref[...]
        y_ref[...] = (h_ref[...] * (lg * _sigmoid(lg))).astype(_MXU_DTYPE)

    seg = lambda s: pl.BlockSpec((None, tm, D_MODEL), lambda i: (s, i, 0))
    return pl.pallas_call(
        body, name="lru_fwd", grid=(t // tm,),
        in_specs=[seg(0),
                  pl.BlockSpec((None, 8, D_MODEL), lambda i: (0, jnp.maximum(i * (tm // 8) - 1, 0), 0)),
                  seg(1),
                  _full((LRU_BLOCKS, LRU_BW, LRU_BW)), _full((LRU_BLOCKS, LRU_BW, LRU_BW)),
                  _full((1, D_MODEL)), _full((1, D_MODEL)), _full((1, D_MODEL)),
                  _full((4, D_MODEL)), _full((1, D_MODEL))],
        out_specs=[pl.BlockSpec((tm, D_MODEL), lambda i: (i, 0)),
                   pl.BlockSpec((None, tm, D_MODEL), lambda i: (0, i, 0))],
        out_shape=[jax.ShapeDtypeStruct((t, D_MODEL), F32),
                   jax.ShapeDtypeStruct((2, t, D_MODEL), _MXU_DTYPE)],
        scratch_shapes=[pltpu.VMEM((tm, D_MODEL), F32), pltpu.VMEM((tm, D_MODEL), F32),
                        pltpu.VMEM((8, D_MODEL), F32)],
        compiler_params=_seq_params(),
    )(p, p, p, wa, wx, ba, bx, lam, cw, cb)


def _hgrn_fwd(p, logits, gw, y2, tm):
    t = p.shape[1]
    nc = tm // CHUNK

    def body(qf_ref, vh_ref, lg_ref, gw_ref, y_in, y_ref, o_ref, st_ref, state):
        del y_in
        i = pl.program_id(0)

        @pl.when(i == 0)
        def _():
            state[...] = jnp.zeros_like(state)

        l0 = lg_ref[0:1, :]
        l1 = lg_ref[1:2, :]
        lb = 1.0 / (1.0 + jnp.exp(l1 - l0))
        gwv = gw_ref[...]
        tri = (_rows((CHUNK, CHUNK)) >= _cols((CHUNK, CHUNK))).astype(BF16)
        causal = _rows((CHUNK, CHUNK)) >= _cols((CHUNK, CHUNK))

        def chunk(c, carry):
            rows = pl.ds(c * CHUNK, CHUNK)
            q = qf_ref[0, rows, :]
            fr = qf_ref[1, rows, :]
            v = vh_ref[0, rows, :]
            hg = vh_ref[1, rows, :]
            _, _, k, _, qs, eb, em1, em2, ee, ebc = _hgrn_prep(q, fr, lb, tri)
            qi, qm, km, ke = qs * eb, qs * em1, k * em2, k * ee
            gate = hg * _sigmoid(hg)
            for h in range(HEADS):
                ln = slice(h * HEAD_D, (h + 1) * HEAD_D)
                s0 = state[h]
                st_ref[c, h] = s0
                att = jnp.where(causal, _dot_nt(qm[:, ln], km[:, ln]), 0.0)
                o = _dot_nt(qi[:, ln], s0) + _dot(att, v[:, ln])
                state[h] = s0 * ebc[:, ln] + _dot_tn(v[:, ln], ke[:, ln])
                rstd = lax.rsqrt(jnp.mean(o * o, axis=-1, keepdims=True) + EPS)
                o_ref[rows, ln] = o
                y_ref[rows, ln] = (o * rstd * gwv[:, ln] * gate[:, ln]).astype(_MXU_DTYPE)
            return carry

        for step in range(nc):
            chunk(step, 0)

    pair = lambda s: pl.BlockSpec((2, tm, D_MODEL), lambda i: (s, i, 0))
    return pl.pallas_call(
        body, name="hgrn_fwd", grid=(t // tm,),
        in_specs=[pair(1), pair(2), _full((2, D_MODEL)), _full((1, D_MODEL)),
                  pl.BlockSpec(memory_space=pl.ANY)],
        out_specs=[pl.BlockSpec((None, tm, D_MODEL), lambda i: (1, i, 0)),
                   pl.BlockSpec((tm, D_MODEL), lambda i: (i, 0)),
                   pl.BlockSpec((nc, HEADS, HEAD_D, HEAD_D), lambda i: (i, 0, 0, 0))],
        out_shape=[jax.ShapeDtypeStruct((2, t, D_MODEL), _MXU_DTYPE),
                   jax.ShapeDtypeStruct((t, D_MODEL), F32),
                   jax.ShapeDtypeStruct((t // CHUNK, HEADS, HEAD_D, HEAD_D), F32)],
        scratch_shapes=[pltpu.VMEM((HEADS, HEAD_D, HEAD_D), F32)],
        input_output_aliases={4: 0},
        compiler_params=_seq_params(),
    )(p, p, logits, gw, y2)


def _out_fused(y2, wo, x, tgt, wpost, tm):
    t = x.shape[0]
    inv_d = 1.0 / D_MODEL

    def body(y_ref, wo_ref, x_ref, t_ref, w_ref, dout_ref, dy_ref, sq_ref, dw_ref):
        i = pl.program_id(0)

        @pl.when(i == 0)
        def _():
            sq_ref[...] = jnp.zeros_like(sq_ref)
            dw_ref[...] = jnp.zeros_like(dw_ref)

        y = (jnp.dot(y_ref[0], wo_ref[0:D_MODEL, :], preferred_element_type=F32)
             + jnp.dot(y_ref[1], wo_ref[D_MODEL:2 * D_MODEL, :], preferred_element_type=F32))
        w = w_ref[...]
        rstd = lax.rsqrt(jnp.mean(y * y, axis=-1, keepdims=True) + EPS)
        n = y * rstd
        err = (x_ref[...] + n * w) - t_ref[...]
        sq_ref[...] += jnp.sum(err * err, axis=0, keepdims=True)
        dout = err * inv_d
        dout_ref[...] = dout
        dw_ref[...] += jnp.sum(dout * n, axis=0, keepdims=True)
        dn = dout * w
        dy_ref[...] = (rstd * (dn - n * jnp.mean(dn * n, axis=-1, keepdims=True))).astype(_MXU_DTYPE)

    row = pl.BlockSpec((tm, D_MODEL), lambda i: (i, 0))
    return pl.pallas_call(
        body, name="out_fused", grid=(t // tm,),
        in_specs=[pl.BlockSpec((2, tm, D_MODEL), lambda i: (0, i, 0)), _full((2 * D_MODEL, D_MODEL)),
                  row, row, _full((1, D_MODEL))],
        out_specs=[row, row, _full((1, D_MODEL)), _full((1, D_MODEL))],
        out_shape=[jax.ShapeDtypeStruct((t, D_MODEL), F32), jax.ShapeDtypeStruct((t, D_MODEL), _MXU_DTYPE),
                   jax.ShapeDtypeStruct((1, D_MODEL), F32), jax.ShapeDtypeStruct((1, D_MODEL), F32)],
        compiler_params=_seq_params(),
    )(y2, wo, x, tgt, wpost)


def _dw_out(y2, dy, tt):
    t = dy.shape[0]
    hr = UNIT // 2

    def body(y_ref, dy_ref, g_ref):
        @pl.when(pl.program_id(1) == 0)
        def _():
            g_ref[...] = jnp.zeros_like(g_ref)

        r = _dot_tn(y_ref[...], dy_ref[...])
        g_ref[0] += r[0:hr]
        g_ref[1] += r[hr:UNIT]

    return pl.pallas_call(
        body, name="dw_out", grid=(N_CHIPS, t // tt),
        in_specs=[pl.BlockSpec((None, tt, UNIT), lambda c, k: (c // 2, k, c % 2)),
                  pl.BlockSpec((tt, D_MODEL), lambda c, k: (k, 0))],
        out_specs=pl.BlockSpec((2, None, hr, D_MODEL), lambda c, k: (0, c, 0, 0)),
        out_shape=jax.ShapeDtypeStruct((2, N_CHIPS, hr, D_MODEL), F32),
        compiler_params=pltpu.CompilerParams(dimension_semantics=("arbitrary", "arbitrary")),
    )(y2, dy)


def _lru_bwd(dy, wo, p, h, wa, wx, ba, bx, lam, cw, cb, tm):
    t = dy.shape[0]
    nt = t // tm
    ng = tm // 8

    def body(dy_ref, wo_ref, lx_ref, halo_ref, lg_ref, h_ref, hhalo_ref, wa_ref, wx_ref, ba_ref, bx_ref,
             lam_ref, cw_ref, cb_ref,
             dp_ref, dwa_ref, dwx_ref, dba_ref, dbx_ref, dsp_ref, dcw_ref, dcb_ref,
             c_s, in_s, dh_s, dhc_s, afirst_s, dxc_s):
        i = pl.program_id(0)
        first = i == nt - 1

        @pl.when(i == 0)
        def _():
            for ref in (dwa_ref, dwx_ref, dba_ref, dbx_ref, dsp_ref, dcw_ref, dcb_ref, dhc_s, afirst_s, dxc_s):
                ref[...] = jnp.zeros_like(ref)

        sp = _softplus_neg(lam_ref[...])
        cwv = cw_ref[...]
        wav = wa_ref[...]
        wxv = wx_ref[...]
        halo = jnp.where(first, 0.0, halo_ref[...])
        xs, xc, r, ig, a, mult = _lru_gates(lx_ref[...], halo, cwv, cb_ref[...], wav, wxv,
                                            ba_ref[...], bx_ref[...], sp)
        d_y = _dot_nt(dy_ref[...], wo_ref[...])
        lg = lg_ref[...]
        sl = _sigmoid(lg)
        hv = h_ref[...]
        d_lg = d_y * hv * (sl * (1.0 + lg * (1.0 - sl)))

        c_s[...] = _shift_up(a, afirst_s[...], 1)
        in_s[...] = d_y * (lg * sl)
        row8 = _rows((8, D_MODEL))

        def group(gg, carry):
            rows = pl.ds(pl.multiple_of((ng - 1 - gg) * 8, 8), 8)
            cv = c_s[rows, :]
            uv = in_s[rows, :]
            for s in (1, 2, 4):
                c_sh = jnp.where(row8 < 8 - s, pltpu.roll(cv, 8 - s, 0), 1.0)
                u_sh = jnp.where(row8 < 8 - s, pltpu.roll(uv, 8 - s, 0), 0.0)
                uv = cv * u_sh + uv
                cv = cv * c_sh
            hh = cv * carry + uv
            dh_s[rows, :] = hh
            return jnp.broadcast_to(hh[0:1, :], (8, D_MODEL))

        dhc_s[...] = lax.fori_loop(0, ng, group, dhc_s[...])
        afirst_s[...] = a[0:8]
        dh = dh_s[...]

        h_prev = _shift_down(hv, jnp.where(first, 0.0, hhalo_ref[...]), 1)
        gx = ig * xc
        d_mult = dh * gx
        d_i = dh * (mult * xc)
        d_xc = dh * (mult * ig)
        d_la = (dh * h_prev) * a - d_mult * (a * a) / mult
        dsp_ref[...] += jnp.sum(d_la * r, axis=0, keepdims=True) * (-LRU_C)
        d_zr = (d_la * (-LRU_C * sp)) * (r * (1.0 - r))
        d_zi = d_i * (ig * (1.0 - ig))
        dba_ref[...] += jnp.sum(d_zr, axis=0, keepdims=True)
        dbx_ref[...] += jnp.sum(d_zi, axis=0, keepdims=True)
        back = []
        for n in range(LRU_BLOCKS):
            ln = slice(n * LRU_BW, (n + 1) * LRU_BW)
            xb = _mx(xc[:, ln])
            zr_n = _mx(d_zr[:, ln])
            zi_n = _mx(d_zi[:, ln])
            dwa_ref[n] += _dot_tn(xb, zr_n)
            dwx_ref[n] += _dot_tn(xb, zi_n)
            back.append(_dot_nt(zr_n, wav[n]) + _dot_nt(zi_n, wxv[n]))
        d_xc = d_xc + jnp.concatenate(back, axis=1)
        dcb_ref[...] += jnp.sum(d_xc, axis=0, keepdims=True)
        for k in range(4):
            dcw_ref[k:k + 1, :] += jnp.sum(d_xc * xs[k], axis=0, keepdims=True)
        head = dxc_s[...]
        d_lx = cwv[3:4] * d_xc
        for k in range(3):
            d_lx = d_lx + cwv[k:k + 1] * _shift_up(d_xc, head, 3 - k)
        dxc_s[...] = d_xc[0:8]
        dp_ref[0] = d_lx.astype(_MXU_DTYPE)
        dp_ref[1] = d_lg.astype(_MXU_DTYPE)

    rev = lambda i: nt - 1 - i
    seg = lambda s: pl.BlockSpec((None, tm, D_MODEL), lambda i: (s, rev(i), 0))
    halo_row = lambda i: jnp.maximum(rev(i) * (tm // 8) - 1, 0)
    row = pl.BlockSpec((tm, D_MODEL), lambda i: (rev(i), 0))
    vec = _full((1, D_MODEL))
    wblk = _full((LRU_BLOCKS, LRU_BW, LRU_BW))
    return pl.pallas_call(
        body, name="lru_bwd", grid=(nt,),
        in_specs=[row, pl.BlockSpec((D_MODEL, D_MODEL), lambda i: (0, 0)),
                  seg(0), pl.BlockSpec((None, 8, D_MODEL), lambda i: (0, halo_row(i), 0)), seg(1),
                  row, pl.BlockSpec((8, D_MODEL), lambda i: (halo_row(i), 0)),
                  wblk, wblk, vec, vec, vec, _full((4, D_MODEL)), vec],
        out_specs=[pl.BlockSpec((2, tm, D_MODEL), lambda i: (2, rev(i), 0)),
                   wblk, wblk, vec, vec, vec, _full((4, D_MODEL)), vec],
        out_shape=[jax.ShapeDtypeStruct((N_SEG, t, D_MODEL), _MXU_DTYPE),
                   jax.ShapeDtypeStruct((LRU_BLOCKS, LRU_BW, LRU_BW), F32),
                   jax.ShapeDtypeStruct((LRU_BLOCKS, LRU_BW, LRU_BW), F32),
                   jax.ShapeDtypeStruct((1, D_MODEL), F32), jax.ShapeDtypeStruct((1, D_MODEL), F32),
                   jax.ShapeDtypeStruct((1, D_MODEL), F32), jax.ShapeDtypeStruct((4, D_MODEL), F32),
                   jax.ShapeDtypeStruct((1, D_MODEL), F32)],
        scratch_shapes=[pltpu.VMEM((tm, D_MODEL), F32), pltpu.VMEM((tm, D_MODEL), F32),
                        pltpu.VMEM((tm, D_MODEL), F32), pltpu.VMEM((8, D_MODEL), F32),
                        pltpu.VMEM((8, D_MODEL), F32), pltpu.VMEM((8, D_MODEL), F32)],
        compiler_params=_seq_params(),
    )(dy, wo, p, p, p, h, h, wa, wx, ba, bx, lam, cw, cb)


def _hgrn_bwd(dy, wo, p, o, states, logits, gw, dp, tm):
    t = dy.shape[0]
    nt = t // tm
    nc = tm // CHUNK

    def body(dy_ref, wo_ref, qf_ref, vh_ref, o_ref, st_ref, lg_ref, gw_ref, dp_in,
             dp_ref, dgw_ref, dlb_ref, dyh_s, dstate, snext):
        del dp_in
        i = pl.program_id(0)

        @pl.when(i == 0)
        def _():
            for ref in (dgw_ref, dlb_ref, dstate, snext):
                ref[...] = jnp.zeros_like(ref)

        dyh_s[...] = _dot_nt(dy_ref[...], wo_ref[...])
        l0 = lg_ref[0:1, :]
        l1 = lg_ref[1:2, :]
        lb = 1.0 / (1.0 + jnp.exp(l1 - l0))
        gwv = gw_ref[...]
        rr = _rows((CHUNK, CHUNK))
        cc = _cols((CHUNK, CHUNK))
        causal = rr >= cc
        tri = causal.astype(BF16)
        tri_up = (rr <= cc).astype(BF16)

        def chunk(c2, carry):
            c = nc - 1 - c2
            rows = pl.ds(c * CHUNK, CHUNK)
            q = qf_ref[0, rows, :]
            fr = qf_ref[1, rows, :]
            v = vh_ref[0, rows, :]
            hg = vh_ref[1, rows, :]
            ov = o_ref[rows, :]
            dyh = dyh_s[rows, :]
            sig, f, k, sq, qs, eb, em1, em2, ee, ebc = _hgrn_prep(q, fr, lb, tri)
            qi, qm, km, ke = _mx(qs * eb), _mx(qs * em1), _mx(k * em2), _mx(k * ee)
            sg = _sigmoid(hg)
            gate = hg * sg
            d_on = dyh * gate
            dqs_p, dk_p, dv_p, dhg_p, db_p, ex_p = [], [], [], [], [], []
            dgw_acc = []
            for h in range(HEADS):
                ln = slice(h * HEAD_D, (h + 1) * HEAD_D)
                o_h = ov[:, ln]
                rstd = lax.rsqrt(jnp.mean(o_h * o_h, axis=-1, keepdims=True) + EPS)
                n = o_h * rstd
                dhg_p.append(dyh[:, ln] * (n * gwv[:, ln]) * (sg[:, ln] * (1.0 + hg[:, ln] * (1.0 - sg[:, ln]))))
                dgw_acc.append(jnp.sum(d_on[:, ln] * n, axis=0, keepdims=True))
                dn = d_on[:, ln] * gwv[:, ln]
                d_o = rstd * (dn - n * jnp.mean(dn * n, axis=-1, keepdims=True))
                s0 = st_ref[c, h]
                ds = dstate[h]
                att = jnp.where(causal, _dot_nt(qm[:, ln], km[:, ln]), 0.0)
                d_att = jnp.where(causal, _dot_nt(d_o, v[:, ln]), 0.0)
                dv_p.append(_dot_tn(att, d_o) + _dot_nt(ke[:, ln], ds))
                d_qm = _dot(d_att, km[:, ln])
                d_km = _dot_tn(d_att, qm[:, ln])
                d_qi = _dot(d_o, s0)
                d_ke = _dot(v[:, ln], ds)
                dqs_h = d_qi * eb[:, ln] + d_qm * em1[:, ln]
                dk_h = d_km * em2[:, ln] + d_ke * ee[:, ln]
                dqs_p.append(dqs_h)
                dk_p.append(dk_h)
                db_p.append((qi[:, ln].astype(F32) * d_qi + qm[:, ln].astype(F32) * d_qm)
                            - (km[:, ln].astype(F32) * d_km + ke[:, ln].astype(F32) * d_ke))
                ex_p.append(jnp.sum(ds * snext[h], axis=0, keepdims=True))
                dstate[h] = ds * ebc[:, ln] + _dot_tn(d_o, qi[:, ln])
                snext[h] = s0
            cat = lambda parts: jnp.concatenate(parts, axis=1)
            dqs, dk, db = cat(dqs_p), cat(dk_p), cat(db_p)
            dg = _tri_sum(tri_up, db) + cat(ex_p)
            df = dg / f - dk
            dgw_ref[...] += cat(dgw_acc)
            dlb_ref[...] += jnp.sum(df * (1.0 - sig), axis=0, keepdims=True)
            dp_ref[0, rows, :] = (dqs * (sq * (1.0 + q * (1.0 - sq)))).astype(_MXU_DTYPE)
            dp_ref[1, rows, :] = (df * ((1.0 - lb) * sig * (1.0 - sig))).astype(_MXU_DTYPE)
            dp_ref[2, rows, :] = cat(dv_p).astype(_MXU_DTYPE)
            dp_ref[3, rows, :] = cat(dhg_p).astype(_MXU_DTYPE)
            return carry

        for step in range(nc):
            chunk(step, 0)

    rev = lambda i: nt - 1 - i
    pair = lambda s: pl.BlockSpec((2, tm, D_MODEL), lambda i: (s, rev(i), 0))
    row = pl.BlockSpec((tm, D_MODEL), lambda i: (rev(i), 0))
    vec = _full((1, D_MODEL))
    return pl.pallas_call(
        body, name="hgrn_bwd", grid=(nt,),
        in_specs=[row, pl.BlockSpec((D_MODEL, D_MODEL), lambda i: (1, 0)), pair(1), pair(2), row,
                  pl.BlockSpec((nc, HEADS, HEAD_D, HEAD_D), lambda i: (rev(i), 0, 0, 0)),
                  _full((2, D_MODEL)), vec, pl.BlockSpec(memory_space=pl.ANY)],
        out_specs=[pl.BlockSpec((4, tm, D_MODEL), lambda i: (0, rev(i), 0)), vec, vec],
        out_shape=[jax.ShapeDtypeStruct((N_SEG, t, D_MODEL), _MXU_DTYPE),
                   jax.ShapeDtypeStruct((1, D_MODEL), F32), jax.ShapeDtypeStruct((1, D_MODEL), F32)],
        scratch_shapes=[pltpu.VMEM((tm, D_MODEL), F32), pltpu.VMEM((HEADS, HEAD_D, HEAD_D), F32),
                        pltpu.VMEM((HEADS, HEAD_D, HEAD_D), F32)],
        input_output_aliases={8: 0},
        compiler_params=_seq_params(),
    )(dy, wo, p, p, o, states, logits, gw, dp)


def _dp_unit(m):
    return ((m // 2 + 2) % N_SEG) * 2 + m % 2


def _dw_in(ut, dp, tt):
    t = ut.shape[1]
    hr = D_MODEL // 2

    def body(ut_ref, dp_ref, g_ref):
        @pl.when(pl.program_id(1) == 0)
        def _():
            g_ref[...] = jnp.zeros_like(g_ref)

        r = jnp.dot(ut_ref[...], dp_ref[...], preferred_element_type=F32)
        for h in range(2):
            for half in range(2):
                g_ref[h, half] += r[h * hr:(h + 1) * hr, half * UNIT:(half + 1) * UNIT]

    return pl.pallas_call(
        body, name="dw_in", grid=(N_SEG, t // tt),
        in_specs=[pl.BlockSpec((D_MODEL, tt), lambda s, k: (0, k)),
                  pl.BlockSpec((None, tt, D_MODEL), lambda s, k: (s, k, 0))],
        out_specs=pl.BlockSpec((2, 2, hr, UNIT), lambda s, k: (0, (s + 2) % N_SEG, 0, 0)),
        out_shape=jax.ShapeDtypeStruct((2, N_UNIT, hr, UNIT), F32),
        compiler_params=pltpu.CompilerParams(dimension_semantics=("arbitrary", "arbitrary")),
    )(ut, dp)


def _dx(dp, wg, x, dout, wpre, tm):
    t = x.shape[0]

    def body(dp_ref, w_ref, x_ref, dout_ref, wpre_ref, gx_ref, dw_ref):
        @pl.when(pl.program_id(0) == 0)
        def _():
            dw_ref[...] = jnp.zeros_like(dw_ref)

        du = None
        for m in range(N_UNIT):
            part = _dot_nt(dp_ref[m // 2, :, (m % 2) * UNIT:(m % 2 + 1) * UNIT], w_ref[_dp_unit(m)])
            du = part if du is None else du + part
        xv = x_ref[...]
        rstd = lax.rsqrt(jnp.mean(xv * xv, axis=-1, keepdims=True) + EPS)
        n = xv * rstd
        dw_ref[...] += jnp.sum(du * n, axis=0, keepdims=True)
        dn = du * wpre_ref[...]
        gx_ref[...] = dout_ref[...] + rstd * (dn - n * jnp.mean(dn * n, axis=-1, keepdims=True))

    row = pl.BlockSpec((tm, D_MODEL), lambda i: (i, 0))
    return pl.pallas_call(
        body, name="dx", grid=(t // tm,),
        in_specs=[pl.BlockSpec((N_SEG, tm, D_MODEL), lambda i: (0, i, 0)), _resident((N_UNIT, D_MODEL, UNIT)),
                  row, row, _full((1, D_MODEL))],
        out_specs=[row, _full((1, D_MODEL))],
        out_shape=[jax.ShapeDtypeStruct((t, D_MODEL), F32), jax.ShapeDtypeStruct((1, D_MODEL), F32)],
        compiler_params=_seq_params(),
    )(dp, wg, x, dout, wpre)


CHUNK_BYTES = 256 * 1024
MAX_CHUNKS = 16


def _n_chunks(shape, dtype, rows):
    nbytes = jnp.dtype(dtype).itemsize
    for d in shape:
        nbytes *= d
    k = max(1, min(MAX_CHUNKS, rows, nbytes // CHUNK_BYTES))
    while rows % k:
        k -= 1
    return k


def _gather_weights(shards, split):
    n = len(shards)
    ks = [_n_chunks(a.shape[1:], a.dtype, a.shape[1]) for a in shards]
    offs = [sum(ks[:a]) for a in range(n)]
    total = sum(ks)

    def body(*refs):
        ins, outs = refs[:n], refs[n:2 * n]
        send_sems, recv_sems, fsend_sems, frecv_sems, osend_sems, orecv_sems = refs[2 * n:]
        x, y, c = lax.axis_index("x"), lax.axis_index("y"), lax.axis_index("c")
        me = 2 * x + y
        peers = [(1 - x, y), (x, 1 - y), (1 - x, 1 - y)]
        half = lambda a: c if split[a] else 0

        def rows(a, k):
            step = shards[a].shape[1] // ks[a]
            return pl.ds(k * step, step)

        def fetch(a, j, k, slot):
            px, py = peers[j]
            sem = 3 * (offs[a] + k) + j
            return pltpu.make_async_remote_copy(
                src_ref=ins[a].at[half(a), rows(a, k)], dst_ref=outs[a].at[slot, half(a), rows(a, k)],
                send_sem=send_sems.at[sem], recv_sem=recv_sems.at[sem],
                device_id=(px, py, c), device_id_type=MESH)

        def forward(a, j, k, h):
            px, py = peers[j]
            sem = 3 * (offs[a] + k) + j
            piece = outs[a].at[2 * px + py, h, rows(a, k)]
            return pltpu.make_async_remote_copy(
                src_ref=piece, dst_ref=piece, send_sem=fsend_sems.at[sem], recv_sem=frecv_sems.at[sem],
                device_id=(x, y, 1 - c), device_id_type=MESH)

        def own(a, h, k):
            sem = 2 * (offs[a] + k) + h
            return pltpu.make_async_remote_copy(
                src_ref=ins[a].at[h, rows(a, k)], dst_ref=outs[a].at[me, h, rows(a, k)],
                send_sem=osend_sems.at[sem], recv_sem=orecv_sems.at[sem],
                device_id=(x, y, 1 - c), device_id_type=MESH)

        pieces = [(a, k) for a in range(n) for k in range(ks[a])]
        halves = lambda a: range(2 if split[a] else 1)
        for a, k in pieces:
            for j in range(3):
                fetch(a, j, k, me).start()
        for a, k in pieces:
            for h in halves(a):
                own(a, h, k).start()
        for a, k in pieces:
            for j in range(3):
                px, py = peers[j]
                fetch(a, j, k, 2 * px + py).wait_recv()
                if split[a]:
                    forward(a, j, k, c).start()
        for a, k in pieces:
            for j in range(3):
                if split[a]:
                    forward(a, j, k, 1 - c).wait_recv()
            for h in halves(a):
                own(a, h, k).wait_recv()
        for a, k in pieces:
            for j in range(3):
                fetch(a, j, k, me).wait_send()
                if split[a]:
                    forward(a, j, k, c).wait_send()
            for h in halves(a):
                own(a, h, k).wait_send()

    any_spec = pl.BlockSpec(memory_space=pl.ANY)
    return pl.pallas_call(
        body, name="gather_weights",
        in_specs=[any_spec] * n, out_specs=[any_spec] * n,
        out_shape=[jax.ShapeDtypeStruct((N_CHIPS,) + a.shape, a.dtype) for a in shards],
        scratch_shapes=[pltpu.SemaphoreType.DMA((3 * total,)), pltpu.SemaphoreType.DMA((3 * total,)),
                        pltpu.SemaphoreType.DMA((3 * total,)), pltpu.SemaphoreType.DMA((3 * total,)),
                        pltpu.SemaphoreType.DMA((2 * total,)), pltpu.SemaphoreType.DMA((2 * total,))],
    )(*shards)


def _chip_exchange(arrays, scatter, name):
    n = len(arrays)
    slab_shapes = [a.shape[1:] if sc else a.shape for a, sc in zip(arrays, scatter)]

    def body(*refs):
        ins, outs = refs[:n], refs[n:2 * n]
        send_sems, recv_sems = refs[2 * n:]
        x, y, c = lax.axis_index("x"), lax.axis_index("y"), lax.axis_index("c")
        peers = [(1 - x, y), (x, 1 - y), (1 - x, 1 - y)]

        def copy(a, j):
            px, py = peers[j]
            return pltpu.make_async_remote_copy(
                src_ref=ins[a].at[2 * px + py] if scatter[a] else ins[a], dst_ref=outs[a].at[j],
                send_sem=send_sems.at[a * 3 + j], recv_sem=recv_sems.at[a * 3 + j],
                device_id=(px, py, c), device_id_type=MESH)

        copies = [copy(a, j) for a in range(n) for j in range(3)]
        for cp in copies:
            cp.start()
        for cp in copies:
            cp.wait_recv()
        for cp in copies:
            cp.wait_send()

    any_spec = pl.BlockSpec(memory_space=pl.ANY)
    return pl.pallas_call(
        body, name=name,
        in_specs=[any_spec] * n, out_specs=[any_spec] * n,
        out_shape=[jax.ShapeDtypeStruct((3,) + tuple(s), a.dtype) for s, a in zip(slab_shapes, arrays)],
        scratch_shapes=[pltpu.SemaphoreType.DMA((3 * n,)), pltpu.SemaphoreType.DMA((3 * n,))],
    )(*arrays)


def _core_swap(arrays, name, halved):
    n = len(arrays)
    shapes = [a.shape[1:] if halved else a.shape for a in arrays]
    ks = [_n_chunks(s, a.dtype, s[0]) for s, a in zip(shapes, arrays)]
    offs = [sum(ks[:a]) for a in range(n)]

    def body(*refs):
        ins, outs = refs[:n], refs[n:2 * n]
        send_sems, recv_sems = refs[2 * n:]
        x, y, c = lax.axis_index("x"), lax.axis_index("y"), lax.axis_index("c")

        def copy(a, j):
            step = shapes[a][0] // ks[a]
            rows = pl.ds(j * step, step)
            return pltpu.make_async_remote_copy(
                src_ref=ins[a].at[1 - c, rows] if halved else ins[a].at[rows], dst_ref=outs[a].at[rows],
                send_sem=send_sems.at[offs[a] + j], recv_sem=recv_sems.at[offs[a] + j],
                device_id=(x, y, 1 - c), device_id_type=MESH)

        pieces = [(a, j) for a in range(n) for j in range(ks[a])]
        for a, j in pieces:
            copy(a, j).start()
        for a, j in pieces:
            copy(a, j).wait_recv()
        for a, j in pieces:
            copy(a, j).wait_send()

    any_spec = pl.BlockSpec(memory_space=pl.ANY)
    return pl.pallas_call(
        body, name=name,
        in_specs=[any_spec] * n, out_specs=[any_spec] * n,
        out_shape=[jax.ShapeDtypeStruct(s, a.dtype) for s, a in zip(shapes, arrays)],
        scratch_shapes=[pltpu.SemaphoreType.DMA((sum(ks),)), pltpu.SemaphoreType.DMA((sum(ks),))],
    )(*arrays)


def _sum_own_half(g, b, name, out_dtype, tr=512):
    _, rows, cols = g.shape
    tr = min(tr, rows)
    core = lax.axis_index("c").reshape(1)

    def body(c_ref, g_ref, b_ref, o_ref):
        del c_ref
        o_ref[...] = (g_ref[...] + b_ref[...]).astype(out_dtype)

    return pl.pallas_call(
        body, name=name,
        grid_spec=pltpu.PrefetchScalarGridSpec(
            num_scalar_prefetch=1, grid=(rows // tr,),
            in_specs=[pl.BlockSpec((None, tr, cols), lambda i, c_ref: (c_ref[0], i, 0)),
                      pl.BlockSpec((tr, cols), lambda i, c_ref: (i, 0))],
            out_specs=pl.BlockSpec((tr, cols), lambda i, c_ref: (i, 0))),
        out_shape=jax.ShapeDtypeStruct((rows, cols), out_dtype),
        compiler_params=pltpu.CompilerParams(dimension_semantics=("parallel",)),
    )(core, g, b)


def _sum_chips(own, r, slabbed, name, tr=512):
    _, rows, cols = r.shape
    tr = min(tr, rows)
    chip = (2 * lax.axis_index("x") + lax.axis_index("y")).reshape(1)

    def body(c_ref, own_ref, r_ref, o_ref):
        del c_ref
        f = lambda val: val.astype(F32)
        o_ref[...] = (f(own_ref[...]) + f(r_ref[0])) + (f(r_ref[1]) + f(r_ref[2]))

    if slabbed:
        own_spec = pl.BlockSpec((None, tr, cols), lambda i, c_ref: (c_ref[0], i, 0))
    else:
        own_spec = pl.BlockSpec((tr, cols), lambda i, c_ref: (i, 0))
    return pl.pallas_call(
        body, name=name,
        grid_spec=pltpu.PrefetchScalarGridSpec(
            num_scalar_prefetch=1, grid=(rows // tr,),
            in_specs=[own_spec, pl.BlockSpec((3, tr, cols), lambda i, c_ref: (0, i, 0))],
            out_specs=pl.BlockSpec((tr, cols), lambda i, c_ref: (i, 0))),
        out_shape=jax.ShapeDtypeStruct((rows, cols), F32),
        compiler_params=pltpu.CompilerParams(dimension_semantics=("parallel",)),
    )(chip, own, r)


def _adamw_big(own, other, w, m, v, name, grid, g_block, g_index, w_block, w_index):
    core = lax.axis_index("c").reshape(1)
    half_axis = len(grid) - 1

    def body(c_ref, own_ref, oth_ref, w_ref, m_ref, v_ref, g_ref, d_ref, nm_ref, nv_ref):
        mine = pl.program_id(half_axis) == c_ref[0]
        gv = jnp.where(mine, own_ref[...], oth_ref[...])
        g_ref[...] = gv
        d_ref[...], nm_ref[...], nv_ref[...] = _adamw(w_ref[...], gv, m_ref[...], v_ref[...])

    g_spec = pl.BlockSpec(g_block, lambda *a: g_index(*a[:half_axis]))
    w_spec = pl.BlockSpec(w_block, lambda *a: w_index(*a[:-1]))
    shp = jax.ShapeDtypeStruct(w.shape, F32)
    return pl.pallas_call(
        body, name=name,
        grid_spec=pltpu.PrefetchScalarGridSpec(
            num_scalar_prefetch=1, grid=grid,
            in_specs=[g_spec, g_spec, w_spec, w_spec, w_spec], out_specs=[w_spec] * 4),
        out_shape=[shp] * 4,
        compiler_params=pltpu.CompilerParams(dimension_semantics=("parallel",) * len(grid)),
    )(core, own, other, w, m, v)


def _small_finish(own, other, lam, logits):
    def body(a_ref, b_ref, lam_ref, lg_ref, o_ref):
        g = a_ref[...] + b_ref[...]
        o_ref[...] = g
        o_ref[8:9, :] = g[8:9, :] * (-_sigmoid(-lam_ref[...]))
        lb = 1.0 / (1.0 + jnp.exp(lg_ref[1:2, :] - lg_ref[0:1, :]))
        d0 = g[9:10, :] * (lb * (1.0 - lb))
        o_ref[9:10, :] = d0
        o_ref[10:11, :] = -d0

    return pl.pallas_call(
        body, name="small_finish",
        in_specs=[_full((SMALL_ROWS, D_MODEL)), _full((SMALL_ROWS, D_MODEL)), _full((1, D_MODEL)), _full((2, D_MODEL))],
        out_specs=_full((SMALL_ROWS, D_MODEL)),
        out_shape=jax.ShapeDtypeStruct((SMALL_ROWS, D_MODEL), F32),
    )(own, other, lam, logits)


def _adamw_small(ws, gs, ms, vs):
    n = len(ws)

    def body(*refs):
        w_r, g_r, m_r, v_r = refs[:n], refs[n:2 * n], refs[2 * n:3 * n], refs[3 * n:4 * n]
        d_o, m_o, v_o = refs[4 * n:5 * n], refs[5 * n:6 * n], refs[6 * n:7 * n]
        for j in range(n):
            d_o[j][...], m_o[j][...], v_o[j][...] = _adamw(w_r[j][...], g_r[j][...], m_r[j][...], v_r[j][...])

    specs = [_full(w.shape) for w in ws]
    shapes = [jax.ShapeDtypeStruct(w.shape, F32) for w in ws]
    outs = pl.pallas_call(
        body, name="adamw_small",
        in_specs=specs * 4, out_specs=specs * 3, out_shape=shapes * 3,
    )(*ws, *gs, *ms, *vs)
    return outs[:n], outs[n:2 * n], outs[2 * n:]


def _local_step(x, tgt, pre_w, wg, cw, cb, wa, wx, ba, bx, lam, logits, gw, wo, post_w, tm, tmm):
    ut, p = _in_proj(x, pre_w, wg, tm)
    h, y2 = _lru_fwd(p, wa, wx, ba, bx, lam, cw, cb, tm)
    y2, o, states = _hgrn_fwd(p, logits, gw, y2, tm)
    dout, dy, sq, d_post = _out_fused(y2, wo, x, tgt, post_w, tmm)
    g_out = _dw_out(y2, dy, tmm)
    dp, d_wa, d_wx, d_ba, d_bx, d_sp, d_cw, d_cb = _lru_bwd(dy, wo, p, h, wa, wx, ba, bx, lam, cw, cb, tm)
    dp, d_gw, d_lb = _hgrn_bwd(dy, wo, p, o, states, logits, gw, dp, tm)
    g_in = _dw_in(ut, dp, min(2048, x.shape[0]))
    grad_x, d_pre = _dx(dp, wg, x, dout, pre_w, tm)
    small = jnp.concatenate([d_pre, d_cw, d_cb, d_ba, d_bx, d_sp, d_lb,
                             jnp.zeros((1, D_MODEL), F32), d_gw, d_post,
                             jnp.zeros((SMALL_ROWS - 13, D_MODEL), F32)], axis=0)
    return sq, grad_x, g_in, g_out, d_wa, d_wx, small


def kernel(x, pre_norm_w, w_in, conv_w, conv_b, lru_w_a, lru_b_a, lru_w_x, lru_b_x, lru_lambda, hgrn_lb_logits, hgrn_gnorm_w, w_out, post_norm_w, loss_target, m_pre_norm_w, m_w_in, m_conv_w, m_conv_b, m_lru_w_a, m_lru_b_a, m_lru_w_x, m_lru_b_x, m_lru_lambda, m_hgrn_lb_logits, m_hgrn_gnorm_w, m_w_out, m_post_norm_w, v_pre_norm_w, v_w_in, v_conv_w, v_conv_b, v_lru_w_a, v_lru_b_a, v_lru_w_x, v_lru_b_x, v_lru_lambda, v_hgrn_lb_logits, v_hgrn_gnorm_w, v_w_out, v_post_norm_w):
    t = x.shape[1]
    tm = min(256, t)
    tmm = min(512, t)
    chip = 2 * lax.axis_index("x") + lax.axis_index("y")

    win_units = w_in[0].astype(_MXU_DTYPE).reshape(D_MODEL, 3, UNIT).transpose(1, 0, 2)
    small_w = jnp.concatenate([conv_w[0], lru_b_a[0], lru_b_x[0]], axis=1)
    wg4, wo4, wa4, wx4, sw4 = _gather_weights(
        [win_units.reshape(2, 3 * 512, UNIT), w_out[0].astype(_MXU_DTYPE).reshape(2, 256, D_MODEL),
         lru_w_a[0].astype(_MXU_DTYPE).reshape(2, 128, LRU_BW), lru_w_x[0].astype(_MXU_DTYPE).reshape(2, 128, LRU_BW),
         small_w.reshape(1, 4, 384)],
        [True, True, True, True, False])
    wg = wg4.reshape(N_UNIT, D_MODEL, UNIT)
    wo = wo4.reshape(2 * D_MODEL, D_MODEL)
    by_block = lambda w4: w4.reshape(N_CHIPS, LRU_BLOCKS, 64, LRU_BW).transpose(1, 0, 2, 3).reshape(
        LRU_BLOCKS, LRU_BW, LRU_BW)
    wa, wx = by_block(wa4), by_block(wx4)
    sw4 = sw4.reshape(N_CHIPS, 4, 384)
    cw = sw4[:, :, 0:256].transpose(1, 0, 2).reshape(4, D_MODEL)
    ba = sw4[:, :, 256:320].transpose(1, 0, 2).reshape(1, D_MODEL)
    bx = sw4[:, :, 320:384].transpose(1, 0, 2).reshape(1, D_MODEL)

    sq, grad_x, g_in, g_out, d_wa, d_wx, small = _local_step(
        x[0], loss_target[0], pre_norm_w, wg, cw, conv_b, wa, wx, ba, bx, lru_lambda, hgrn_lb_logits,
        hgrn_gnorm_w, wo, post_norm_w, tm, tmm)
    loss = lax.psum(jnp.sum(sq) * (0.5 / D_MODEL), ("x", "y", "c"))

    def lru_layout(g):
        g = g.reshape(2, 2, N_CHIPS, 64, LRU_BW).transpose(0, 2, 1, 3, 4)
        return g.reshape(2, N_CHIPS * 128, LRU_BW)

    g_in = g_in.reshape(2, N_UNIT * 512, UNIT)
    g_out = g_out.reshape(2, N_CHIPS * 256, D_MODEL)
    g_wa, g_wx = lru_layout(d_wa), lru_layout(d_wx)
    b_in, b_out, b_wa, b_wx = _core_swap([g_in, g_out, g_wa, g_wx], "reduce_swap", halved=True)
    p_in = _sum_own_half(g_in, b_in, "presum_in", _WIRE_DTYPE).reshape(N_CHIPS, 3 * 512, UNIT)
    p_out = _sum_own_half(g_out, b_out, "presum_out", _WIRE_DTYPE).reshape(N_CHIPS, 256, D_MODEL)
    p_wa = _sum_own_half(g_wa, b_wa, "presum_wa", _WIRE_DTYPE).reshape(N_CHIPS, 128, LRU_BW)
    p_wx = _sum_own_half(g_wx, b_wx, "presum_wx", _WIRE_DTYPE).reshape(N_CHIPS, 128, LRU_BW)
    r_in, r_out, r_wa, r_wx, r_small = _chip_exchange(
        [p_in, p_out, p_wa, p_wx, small], [True, True, True, True, False], "reduce_chips")
    s_in = _sum_chips(p_in, r_in, True, "sum_in")
    s_out = _sum_chips(p_out, r_out, True, "sum_out")
    s_wa = _sum_chips(p_wa, r_wa, True, "sum_wa")
    s_wx = _sum_chips(p_wx, r_wx, True, "sum_wx")
    s_small = _sum_chips(small, r_small, False, "sum_small")
    o_in, o_out, o_wa, o_wx, o_small = _core_swap([s_in, s_out, s_wa, s_wx, s_small], "reduce_cores", halved=False)

    g_w_in, d_w_in, nm_w_in, nv_w_in = _adamw_big(
        s_in.reshape(3, 512, UNIT), o_in.reshape(3, 512, UNIT), w_in[0], m_w_in[0], v_w_in[0], "adamw_w_in", (3, 2),
        (None, 512, UNIT), lambda k: (k, 0, 0), (512, UNIT), lambda k, h: (h, k))
    g_w_out, d_w_out, nm_w_out, nv_w_out = _adamw_big(
        s_out, o_out, w_out[0], m_w_out[0], v_w_out[0], "adamw_w_out", (2,),
        (256, D_MODEL), lambda: (0, 0), (256, D_MODEL), lambda h: (h, 0))
    sq2 = lambda a: a.reshape(LRU_BW, LRU_BW)
    lru_specs = ((2,), (128, LRU_BW), lambda: (0, 0), (128, LRU_BW), lambda h: (h, 0))
    g_wa, d_wa2, nm_wa, nv_wa = _adamw_big(s_wa, o_wa, sq2(lru_w_a), sq2(m_lru_w_a), sq2(v_lru_w_a), "adamw_wa",
                                           *lru_specs)
    g_wx, d_wx2, nm_wx, nv_wx = _adamw_big(s_wx, o_wx, sq2(lru_w_x), sq2(m_lru_w_x), sq2(v_lru_w_x), "adamw_wx",
                                           *lru_specs)

    gs = _small_finish(s_small, o_small, lru_lambda, hgrn_lb_logits)
    g_pre = gs[0:1]
    g_cw = lax.dynamic_slice(gs[1:5], (0, chip * 256), (4, 256))
    g_cb = gs[5:6]
    g_ba = lax.dynamic_slice(gs[6].reshape(LRU_BLOCKS, N_CHIPS, 64), (0, chip, 0), (LRU_BLOCKS, 1, 64)).reshape(4, 64)
    g_bx = lax.dynamic_slice(gs[7].reshape(LRU_BLOCKS, N_CHIPS, 64), (0, chip, 0), (LRU_BLOCKS, 1, 64)).reshape(4, 64)
    g_lam = gs[8:9]
    g_lb = gs[9:11]
    g_gw = gs[11:12]
    g_post = gs[12:13]
    small_g = [g_pre, g_cw, g_cb, g_ba, g_bx, g_lam, g_lb, g_gw, g_post]
    two_d = lambda a: a.reshape(a.shape[-2:])
    small_w_list = [pre_norm_w, conv_w, conv_b, lru_b_a, lru_b_x, lru_lambda, hgrn_lb_logits, hgrn_gnorm_w, post_norm_w]
    small_m_list = [m_pre_norm_w, m_conv_w, m_conv_b, m_lru_b_a, m_lru_b_x, m_lru_lambda, m_hgrn_lb_logits,
                    m_hgrn_gnorm_w, m_post_norm_w]
    small_v_list = [v_pre_norm_w, v_conv_w, v_conv_b, v_lru_b_a, v_lru_b_x, v_lru_lambda, v_hgrn_lb_logits,
                    v_hgrn_gnorm_w, v_post_norm_w]
    sd, sm, sv = _adamw_small([two_d(a) for a in small_w_list], small_g,
                              [two_d(a) for a in small_m_list], [two_d(a) for a in small_v_list])

    def shaped(vals, refs):
        return [val.reshape(ref.shape) for val, ref in zip(vals, refs)]

    s_g = shaped(small_g, small_w_list)
    s_d = shaped(sd, small_w_list)
    s_m = shaped(sm, small_w_list)
    s_v = shaped(sv, small_w_list)

    def ordered(small, big_in, big_wa, big_wx, big_out):
        pre, cw_, cb_, ba_, bx_, lam_, lb_, gw_, post_ = small
        return [pre, big_in.reshape(w_in.shape), cw_, cb_, big_wa.reshape(lru_w_a.shape), ba_,
                big_wx.reshape(lru_w_x.shape), bx_, lam_, lb_, gw_, big_out.reshape(w_out.shape), post_]

    grads = ordered(s_g, g_w_in, g_wa, g_wx, g_w_out)
    deltas = ordered(s_d, d_w_in, d_wa2, d_wx2, d_w_out)
    new_m = ordered(s_m, nm_w_in, nm_wa, nm_wx, nm_w_out)
    new_v = ordered(s_v, nv_w_in, nv_wa, nv_wx, nv_w_out)
    return (loss, grad_x.reshape(x.shape), *grads, *deltas, *new_m, *new_v)
```

```python
import functools

import jax
import jax.numpy as jnp
from jax import lax
from jax.experimental import pallas as pl
from jax.experimental.pallas import tpu as pltpu

F32 = jnp.float32
BF16 = jnp.bfloat16
_MXU_DTYPE = jnp.bfloat16
_WIRE_DTYPE = jnp.bfloat16

D_MODEL = 1024
N_SEG = 6
UNIT = 512
N_UNIT = 12
LRU_BLOCKS = 4
LRU_BW = 256
LRU_C = 8.0
HEADS = 8
HEAD_D = 128
CHUNK = 64
EPS = 1e-6
N_CHIPS = 4
EXP_CLAMP = 80.0

ADAM_LR = 0.001
ADAM_B1 = 0.9
ADAM_B2 = 0.999
ADAM_EPS = 1e-08
ADAM_WD = 0.01
ADAM_STEP = 10

SMALL_ROWS = 16
MESH = pl.DeviceIdType.MESH


def _sigmoid(x):
    return 1.0 / (1.0 + jnp.exp(-x))


def _mx(x):
    return x.astype(_MXU_DTYPE)


def _dot(a, b):
    return jnp.dot(_mx(a), _mx(b), preferred_element_type=F32)


def _dot_nt(a, b):
    return lax.dot_general(_mx(a), _mx(b), (((1,), (1,)), ((), ())), preferred_element_type=F32)


def _dot_tn(a, b):
    return lax.dot_general(_mx(a), _mx(b), (((0,), (0,)), ((), ())), preferred_element_type=F32)


def _rows(shape):
    return lax.broadcasted_iota(jnp.int32, shape, 0)


def _cols(shape):
    return lax.broadcasted_iota(jnp.int32, shape, 1)


def _softplus_neg(lam):
    z = -lam
    e = jnp.exp(-jnp.abs(z))
    series = e * (1.0 - e * (0.5 - e * (1.0 / 3.0 - 0.25 * e)))
    return jnp.maximum(z, 0.0) + jnp.where(e < 1e-2, series, jnp.log(1.0 + e))


def _neg_expm1(y):
    series = -y * (1.0 + y * (0.5 + y * (1.0 / 6.0 + y * (1.0 / 24.0))))
    return jnp.where(y > -0.03, series, 1.0 - jnp.exp(y))


def _cumsum_rows(x, reverse=False):
    n = x.shape[0] // 8
    row8 = _rows((8, x.shape[1]))
    out = [None] * n
    carry = None
    for g in (reversed(range(n)) if reverse else range(n)):
        blk = x[8 * g:8 * g + 8]
        for s in (1, 2, 4):
            if reverse:
                blk = blk + jnp.where(row8 < 8 - s, pltpu.roll(blk, 8 - s, 0), 0.0)
            else:
                blk = blk + jnp.where(row8 >= s, pltpu.roll(blk, s, 0), 0.0)
        if carry is not None:
            blk = blk + carry
        carry = blk[0:1] if reverse else blk[7:8]
        out[g] = blk
    return jnp.concatenate(out, axis=0)


def _shift_down(x, halo, s):
    if s == 0:
        return x
    r = pltpu.roll(x, s, 0)
    top = jnp.where(_rows(halo.shape) < s, pltpu.roll(halo, s, 0), r[0:8])
    return jnp.concatenate([top, r[8:]], axis=0)


def _shift_up(x, head, s):
    if s == 0:
        return x
    n = x.shape[0]
    r = pltpu.roll(x, n - s, 0)
    bottom = jnp.where(_rows(head.shape) >= 8 - s, pltpu.roll(head, 8 - s, 0), r[n - 8:n])
    return jnp.concatenate([r[:n - 8], bottom], axis=0)


def _lru_gates(lx, halo, cw, cb, wa, wx, ba, bx, sp):
    xs = [_shift_down(lx, halo, 3 - k) for k in range(4)]
    xc = cb + cw[0:1] * xs[0] + cw[1:2] * xs[1] + cw[2:3] * xs[2] + cw[3:4] * xs[3]
    zr, zi = [], []
    for n in range(LRU_BLOCKS):
        xb = _mx(xc[:, n * LRU_BW:(n + 1) * LRU_BW])
        zr.append(jnp.dot(xb, wa[n], preferred_element_type=F32))
        zi.append(jnp.dot(xb, wx[n], preferred_element_type=F32))
    r = _sigmoid(jnp.concatenate(zr, axis=1) + ba)
    ig = _sigmoid(jnp.concatenate(zi, axis=1) + bx)
    la = (-LRU_C * sp) * r
    a = jnp.exp(la)
    mult = jnp.sqrt(_neg_expm1(2.0 * la))
    return xs, xc, r, ig, a, mult


def _hgrn_prep(q, fr, lb):
    sig = _sigmoid(fr)
    f = lb + (1.0 - lb) * sig
    sq = _sigmoid(q)
    b = _cumsum_rows(jnp.log(f))
    bm = b[CHUNK // 2 - 1:CHUNK // 2]
    bc = b[CHUNK - 1:CHUNK]
    em1 = jnp.exp(jnp.minimum(b - bm, EXP_CLAMP))
    em2 = jnp.exp(jnp.minimum(bm - b, EXP_CLAMP))
    qm = _mx((q * sq) * em1)
    km = _mx((1.0 - f) * em2)
    return sig, f, sq, em1, em2, qm, km, jnp.exp(bm), jnp.exp(bc - bm), jnp.exp(bc)


def _pad_lanes(parts, width):
    have = sum(p.shape[1] for p in parts)
    pad = [jnp.zeros((parts[0].shape[0], width - have), parts[0].dtype)] if width > have else []
    return jnp.concatenate(list(parts) + pad, axis=1)


def _pad_rows(parts, height):
    have = sum(p.shape[0] for p in parts)
    pad = [jnp.zeros((height - have, parts[0].shape[1]), parts[0].dtype)] if height > have else []
    return jnp.concatenate(list(parts) + pad, axis=0)


def _adamw(w, g, m, v):
    m = ADAM_B1 * m + (1.0 - ADAM_B1) * g
    v = ADAM_B2 * v + (1.0 - ADAM_B2) * jnp.square(g)
    m_hat = m / (1.0 - ADAM_B1 ** ADAM_STEP)
    v_hat = v / (1.0 - ADAM_B2 ** ADAM_STEP)
    delta = -ADAM_LR * (m_hat / (jnp.sqrt(v_hat) + ADAM_EPS) + ADAM_WD * w)
    return delta, m, v


def _full(shape):
    nd = len(shape)
    return pl.BlockSpec(shape, lambda *_: (0,) * nd)


def _seq_params():
    return pltpu.CompilerParams(dimension_semantics=("arbitrary",))


def _resident(shape):
    nd = len(shape)
    return pl.BlockSpec(shape, lambda *_: (0,) * nd, pipeline_mode=pl.Buffered(1))


def _in_proj(x, w, wg, tm):
    t = x.shape[0]

    def body(x_ref, w_ref, wg_ref, ut_ref, p_ref):
        xv = x_ref[...]
        rstd = lax.rsqrt(jnp.mean(xv * xv, axis=-1, keepdims=True) + EPS)
        u = xv * rstd * w_ref[...]
        ut_ref[...] = u.T.astype(_MXU_DTYPE)
        ub = _mx(u)
        for m in range(N_UNIT):
            p_ref[m // 2, :, (m % 2) * UNIT:(m % 2 + 1) * UNIT] = jnp.dot(
                ub, wg_ref[m], preferred_element_type=F32)

    return pl.pallas_call(
        body, name="in_proj", grid=(t // tm,),
        in_specs=[pl.BlockSpec((tm, D_MODEL), lambda i: (i, 0)), _full((1, D_MODEL)),
                  _resident((N_UNIT, D_MODEL, UNIT))],
        out_specs=[pl.BlockSpec((D_MODEL, tm), lambda i: (0, i)),
                   pl.BlockSpec((N_SEG, tm, D_MODEL), lambda i: (0, i, 0))],
        out_shape=[jax.ShapeDtypeStruct((D_MODEL, t), _MXU_DTYPE),
                   jax.ShapeDtypeStruct((N_SEG, t, D_MODEL), F32)],
        compiler_params=pltpu.CompilerParams(dimension_semantics=("parallel",)),
    )(x, w, wg)


def _lru_fwd(p, wa, wx, ba, bx, lam, cw, cb, tm):
    t = p.shape[1]
    ng = tm // 8

    def body(lx_ref, halo_ref, lg_ref, wa_ref, wx_ref, ba_ref, bx_ref, lam_ref, cw_ref, cb_ref,
             h_ref, y_ref, a_s, u_s, hc_s):
        i = pl.program_id(0)

        @pl.when(i == 0)
        def _():
            hc_s[...] = jnp.zeros_like(hc_s)

        sp = _softplus_neg(lam_ref[...])
        halo = jnp.where(i == 0, 0.0, halo_ref[...])
        _, xc, _, ig, a, mult = _lru_gates(lx_ref[...], halo, cw_ref[...], cb_ref[...], wa_ref[...],
                                           wx_ref[...], ba_ref[...], bx_ref[...], sp)
        a_s[...] = a
        u_s[...] = mult * (ig * xc)
        row8 = _rows((8, D_MODEL))

        def group(g, hc):
            rows = pl.ds(pl.multiple_of(g * 8, 8), 8)
            av = a_s[rows, :]
            uv = u_s[rows, :]
            for s in (1, 2, 4):
                a_sh = jnp.where(row8 >= s, pltpu.roll(av, s, 0), 1.0)
                u_sh = jnp.where(row8 >= s, pltpu.roll(uv, s, 0), 0.0)
                uv = av * u_sh + uv
                av = av * a_sh
            hh = av * hc + uv
            h_ref[rows, :] = hh
            return jnp.broadcast_to(hh[7:8, :], (8, D_MODEL))

        hc_s[...] = lax.fori_loop(0, ng, group, hc_s[...])
        lg = lg_ref[...]
        y_ref[...] = (h_ref[...] * (lg * _sigmoid(lg))).astype(_MXU_DTYPE)

    seg = lambda s: pl.BlockSpec((None, tm, D_MODEL), lambda i: (s, i, 0))
    return pl.pallas_call(
        body, name="lru_fwd", grid=(t // tm,),
        in_specs=[seg(0),
                  pl.BlockSpec((None, 8, D_MODEL), lambda i: (0, jnp.maximum(i * (tm // 8) - 1, 0), 0)),
                  seg(1),
                  _full((LRU_BLOCKS, LRU_BW, LRU_BW)), _full((LRU_BLOCKS, LRU_BW, LRU_BW)),
                  _full((1, D_MODEL)), _full((1, D_MODEL)), _full((1, D_MODEL)),
                  _full((4, D_MODEL)), _full((1, D_MODEL))],
        out_specs=[pl.BlockSpec((tm, D_MODEL), lambda i: (i, 0)),
                   pl.BlockSpec((None, tm, D_MODEL), lambda i: (0, i, 0))],
        out_shape=[jax.ShapeDtypeStruct((t, D_MODEL), F32),
                   jax.ShapeDtypeStruct((2, t, D_MODEL), _MXU_DTYPE)],
        scratch_shapes=[pltpu.VMEM((tm, D_MODEL), F32), pltpu.VMEM((tm, D_MODEL), F32),
                        pltpu.VMEM((8, D_MODEL), F32)],
        compiler_params=_seq_params(),
    )(p, p, p, wa, wx, ba, bx, lam, cw, cb)


def _hgrn_fwd(p, logits, gw, y2, tm):
    t = p.shape[1]
    nc = tm // CHUNK

    def body(qf_ref, vh_ref, lg_ref, gw_ref, y_in, y_ref, o_ref, st_ref, state):
        del y_in
        i = pl.program_id(0)

        @pl.when(i == 0)
        def _():
            state[...] = jnp.zeros_like(state)

        l0 = lg_ref[0:1, :]
        l1 = lg_ref[1:2, :]
        lb = 1.0 / (1.0 + jnp.exp(l1 - l0))
        gwv = gw_ref[...]
        causal = _rows((CHUNK, CHUNK)) >= _cols((CHUNK, CHUNK))

        for c in range(nc):
            rows = pl.ds(c * CHUNK, CHUNK)
            v = vh_ref[0, rows, :]
            hg = vh_ref[1, rows, :]
            _, _, _, _, _, qm, km, ebm, ecm, ebc = _hgrn_prep(qf_ref[0, rows, :], qf_ref[1, rows, :], lb)
            ke = _mx(km.astype(F32) * ecm)
            gate = gwv * (hg * _sigmoid(hg))
            o_parts, y_parts = [], []
            for h in range(HEADS):
                ln = slice(h * HEAD_D, (h + 1) * HEAD_D)
                s0 = state[h]
                st_ref[c, h] = s0
                vt = _mx(v[:, ln].T)
                att = _mx(jnp.where(causal, _dot_nt(qm[:, ln], km[:, ln]), 0.0))
                o = _dot_nt(_pad_lanes([qm[:, ln], att], 2 * HEAD_D),
                            _pad_lanes([_mx(s0 * ebm[:, ln]), vt], 2 * HEAD_D))
                state[h] = s0 * ebc[:, ln] + jnp.dot(vt, ke[:, ln], preferred_element_type=F32)
                rstd = lax.rsqrt(jnp.mean(o * o, axis=-1, keepdims=True) + EPS)
                o_parts.append(o)
                y_parts.append((o * rstd * gate[:, ln]).astype(_MXU_DTYPE))
            o_ref[rows, :] = jnp.concatenate(o_parts, axis=1)
            y_ref[rows, :] = jnp.concatenate(y_parts, axis=1)

    pair = lambda s: pl.BlockSpec((2, tm, D_MODEL), lambda i: (s, i, 0))
    return pl.pallas_call(
        body, name="hgrn_fwd", grid=(t // tm,),
        in_specs=[pair(1), pair(2), _full((2, D_MODEL)), _full((1, D_MODEL)),
                  pl.BlockSpec(memory_space=pl.ANY)],
        out_specs=[pl.BlockSpec((None, tm, D_MODEL), lambda i: (1, i, 0)),
                   pl.BlockSpec((tm, D_MODEL), lambda i: (i, 0)),
                   pl.BlockSpec((nc, HEADS, HEAD_D, HEAD_D), lambda i: (i, 0, 0, 0))],
        out_shape=[jax.ShapeDtypeStruct((2, t, D_MODEL), _MXU_DTYPE),
                   jax.ShapeDtypeStruct((t, D_MODEL), F32),
                   jax.ShapeDtypeStruct((t // CHUNK, HEADS, HEAD_D, HEAD_D), F32)],
        scratch_shapes=[pltpu.VMEM((HEADS, HEAD_D, HEAD_D), F32)],
        input_output_aliases={4: 0},
        compiler_params=_seq_params(),
    )(p, p, logits, gw, y2)


def _out_fused(y2, wo, x, tgt, wpost, tm):
    t = x.shape[0]
    inv_d = 1.0 / D_MODEL

    def body(y_ref, wo_ref, x_ref, t_ref, w_ref, dout_ref, dy_ref, sq_ref, dw_ref):
        i = pl.program_id(0)

        @pl.when(i == 0)
        def _():
            sq_ref[...] = jnp.zeros_like(sq_ref)
            dw_ref[...] = jnp.zeros_like(dw_ref)

        y = (jnp.dot(y_ref[0], wo_ref[0:D_MODEL, :], preferred_element_type=F32)
             + jnp.dot(y_ref[1], wo_ref[D_MODEL:2 * D_MODEL, :], preferred_element_type=F32))
        w = w_ref[...]
        rstd = lax.rsqrt(jnp.mean(y * y, axis=-1, keepdims=True) + EPS)
        n = y * rstd
        err = (x_ref[...] + n * w) - t_ref[...]
        sq_ref[...] += jnp.sum(err * err, axis=0, keepdims=True)
        dout = err * inv_d
        dout_ref[...] = dout
        dw_ref[...] += jnp.sum(dout * n, axis=0, keepdims=True)
        dn = dout * w
        dy_ref[...] = (rstd * (dn - n * jnp.mean(dn * n, axis=-1, keepdims=True))).astype(_MXU_DTYPE)

    row = pl.BlockSpec((tm, D_MODEL), lambda i: (i, 0))
    return pl.pallas_call(
        body, name="out_fused", grid=(t // tm,),
        in_specs=[pl.BlockSpec((2, tm, D_MODEL), lambda i: (0, i, 0)), _full((2 * D_MODEL, D_MODEL)),
                  row, row, _full((1, D_MODEL))],
        out_specs=[row, row, _full((1, D_MODEL)), _full((1, D_MODEL))],
        out_shape=[jax.ShapeDtypeStruct((t, D_MODEL), F32), jax.ShapeDtypeStruct((t, D_MODEL), _MXU_DTYPE),
                   jax.ShapeDtypeStruct((1, D_MODEL), F32), jax.ShapeDtypeStruct((1, D_MODEL), F32)],
        compiler_params=_seq_params(),
    )(y2, wo, x, tgt, wpost)


def _dw_out(y2, dy, tt):
    t = dy.shape[0]
    hr = UNIT // 2

    def body(y_ref, dy_ref, g_ref):
        @pl.when(pl.program_id(1) == 0)
        def _():
            g_ref[...] = jnp.zeros_like(g_ref)

        r = _dot_tn(y_ref[...], dy_ref[...])
        g_ref[0] += r[0:hr]
        g_ref[1] += r[hr:UNIT]

    return pl.pallas_call(
        body, name="dw_out", grid=(N_CHIPS, t // tt),
        in_specs=[pl.BlockSpec((None, tt, UNIT), lambda c, k: (c // 2, k, c % 2)),
                  pl.BlockSpec((tt, D_MODEL), lambda c, k: (k, 0))],
        out_specs=pl.BlockSpec((2, None, hr, D_MODEL), lambda c, k: (0, c, 0, 0)),
        out_shape=jax.ShapeDtypeStruct((2, N_CHIPS, hr, D_MODEL), F32),
        compiler_params=pltpu.CompilerParams(dimension_semantics=("arbitrary", "arbitrary")),
    )(y2, dy)


def _lru_bwd(dy, wo, p, h, wa, wx, ba, bx, lam, cw, cb, tm):
    t = dy.shape[0]
    nt = t // tm
    ng = tm // 8

    def body(dy_ref, wo_ref, lx_ref, halo_ref, lg_ref, h_ref, hhalo_ref, wa_ref, wx_ref, ba_ref, bx_ref,
             lam_ref, cw_ref, cb_ref,
             dp_ref, dwa_ref, dwx_ref, dba_ref, dbx_ref, dsp_ref, dcw_ref, dcb_ref,
             c_s, in_s, dh_s, dhc_s, afirst_s, dxc_s):
        i = pl.program_id(0)
        first = i == nt - 1

        @pl.when(i == 0)
        def _():
            for ref in (dwa_ref, dwx_ref, dba_ref, dbx_ref, dsp_ref, dcw_ref, dcb_ref, dhc_s, afirst_s, dxc_s):
                ref[...] = jnp.zeros_like(ref)

        sp = _softplus_neg(lam_ref[...])
        cwv = cw_ref[...]
        wav = wa_ref[...]
        wxv = wx_ref[...]
        halo = jnp.where(first, 0.0, halo_ref[...])
        xs, xc, r, ig, a, mult = _lru_gates(lx_ref[...], halo, cwv, cb_ref[...], wav, wxv,
                                            ba_ref[...], bx_ref[...], sp)
        d_y = _dot_nt(dy_ref[...], wo_ref[...])
        lg = lg_ref[...]
        sl = _sigmoid(lg)
        hv = h_ref[...]
        d_lg = d_y * hv * (sl * (1.0 + lg * (1.0 - sl)))

        c_s[...] = _shift_up(a, afirst_s[...], 1)
        in_s[...] = d_y * (lg * sl)
        row8 = _rows((8, D_MODEL))

        def group(gg, carry):
            rows = pl.ds(pl.multiple_of((ng - 1 - gg) * 8, 8), 8)
            cv = c_s[rows, :]
            uv = in_s[rows, :]
            for s in (1, 2, 4):
                c_sh = jnp.where(row8 < 8 - s, pltpu.roll(cv, 8 - s, 0), 1.0)
                u_sh = jnp.where(row8 < 8 - s, pltpu.roll(uv, 8 - s, 0), 0.0)
                uv = cv * u_sh + uv
                cv = cv * c_sh
            hh = cv * carry + uv
            dh_s[rows, :] = hh
            return jnp.broadcast_to(hh[0:1, :], (8, D_MODEL))

        dhc_s[...] = lax.fori_loop(0, ng, group, dhc_s[...])
        afirst_s[...] = a[0:8]
        dh = dh_s[...]

        h_prev = _shift_down(hv, jnp.where(first, 0.0, hhalo_ref[...]), 1)
        gx = ig * xc
        d_mult = dh * gx
        d_i = dh * (mult * xc)
        d_xc = dh * (mult * ig)
        d_la = (dh * h_prev) * a - d_mult * (a * a) / mult
        dsp_ref[...] += jnp.sum(d_la * r, axis=0, keepdims=True) * (-LRU_C)
        d_zr = (d_la * (-LRU_C * sp)) * (r * (1.0 - r))
        d_zi = d_i * (ig * (1.0 - ig))
        dba_ref[...] += jnp.sum(d_zr, axis=0, keepdims=True)
        dbx_ref[...] += jnp.sum(d_zi, axis=0, keepdims=True)
        back = []
        for n in range(LRU_BLOCKS):
            ln = slice(n * LRU_BW, (n + 1) * LRU_BW)
            xb = _mx(xc[:, ln])
            zr_n = _mx(d_zr[:, ln])
            zi_n = _mx(d_zi[:, ln])
            dwa_ref[n] += _dot_tn(xb, zr_n)
            dwx_ref[n] += _dot_tn(xb, zi_n)
            back.append(_dot_nt(zr_n, wav[n]) + _dot_nt(zi_n, wxv[n]))
        d_xc = d_xc + jnp.concatenate(back, axis=1)
        dcb_ref[...] += jnp.sum(d_xc, axis=0, keepdims=True)
        for k in range(4):
            dcw_ref[k:k + 1, :] += jnp.sum(d_xc * xs[k], axis=0, keepdims=True)
        head = dxc_s[...]
        d_lx = cwv[3:4] * d_xc
        for k in range(3):
            d_lx = d_lx + cwv[k:k + 1] * _shift_up(d_xc, head, 3 - k)
        dxc_s[...] = d_xc[0:8]
        dp_ref[0] = d_lx.astype(_MXU_DTYPE)
        dp_ref[1] = d_lg.astype(_MXU_DTYPE)

    rev = lambda i: nt - 1 - i
    seg = lambda s: pl.BlockSpec((None, tm, D_MODEL), lambda i: (s, rev(i), 0))
    halo_row = lambda i: jnp.maximum(rev(i) * (tm // 8) - 1, 0)
    row = pl.BlockSpec((tm, D_MODEL), lambda i: (rev(i), 0))
    vec = _full((1, D_MODEL))
    wblk = _full((LRU_BLOCKS, LRU_BW, LRU_BW))
    return pl.pallas_call(
        body, name="lru_bwd", grid=(nt,),
        in_specs=[row, pl.BlockSpec((D_MODEL, D_MODEL), lambda i: (0, 0)),
                  seg(0), pl.BlockSpec((None, 8, D_MODEL), lambda i: (0, halo_row(i), 0)), seg(1),
                  row, pl.BlockSpec((8, D_MODEL), lambda i: (halo_row(i), 0)),
                  wblk, wblk, vec, vec, vec, _full((4, D_MODEL)), vec],
        out_specs=[pl.BlockSpec((2, tm, D_MODEL), lambda i: (2, rev(i), 0)),
                   wblk, wblk, vec, vec, vec, _full((4, D_MODEL)), vec],
        out_shape=[jax.ShapeDtypeStruct((N_SEG, t, D_MODEL), _MXU_DTYPE),
                   jax.ShapeDtypeStruct((LRU_BLOCKS, LRU_BW, LRU_BW), F32),
                   jax.ShapeDtypeStruct((LRU_BLOCKS, LRU_BW, LRU_BW), F32),
                   jax.ShapeDtypeStruct((1, D_MODEL), F32), jax.ShapeDtypeStruct((1, D_MODEL), F32),
                   jax.ShapeDtypeStruct((1, D_MODEL), F32), jax.ShapeDtypeStruct((4, D_MODEL), F32),
                   jax.ShapeDtypeStruct((1, D_MODEL), F32)],
        scratch_shapes=[pltpu.VMEM((tm, D_MODEL), F32), pltpu.VMEM((tm, D_MODEL), F32),
                        pltpu.VMEM((tm, D_MODEL), F32), pltpu.VMEM((8, D_MODEL), F32),
                        pltpu.VMEM((8, D_MODEL), F32), pltpu.VMEM((8, D_MODEL), F32)],
        compiler_params=_seq_params(),
    )(dy, wo, p, p, p, h, h, wa, wx, ba, bx, lam, cw, cb)


def _hgrn_bwd(dy, wo, p, o, states, logits, gw, dp, tm):
    t = dy.shape[0]
    nt = t // tm
    nc = tm // CHUNK

    def body(dy_ref, wo_ref, qf_ref, vh_ref, o_ref, st_ref, lg_ref, gw_ref, dp_in,
             dp_ref, dgw_ref, dlb_ref, dyh_s, dstate, snext):
        del dp_in
        i = pl.program_id(0)

        @pl.when(i == 0)
        def _():
            for ref in (dgw_ref, dlb_ref, dstate, snext):
                ref[...] = jnp.zeros_like(ref)

        dyh_s[...] = _dot_nt(dy_ref[...], wo_ref[...])
        l0 = lg_ref[0:1, :]
        l1 = lg_ref[1:2, :]
        lb = 1.0 / (1.0 + jnp.exp(l1 - l0))
        gwv = gw_ref[...]
        causal = _rows((CHUNK, CHUNK)) >= _cols((CHUNK, CHUNK))
        wide = 2 * HEAD_D

        for c in reversed(range(nc)):
            rows = pl.ds(c * CHUNK, CHUNK)
            q = qf_ref[0, rows, :]
            hg = vh_ref[1, rows, :]
            ov = o_ref[rows, :]
            dyh = dyh_s[rows, :]
            sig, f, sq, em1, em2, qm, km, ebm, ecm, ebc = _hgrn_prep(q, qf_ref[1, rows, :], lb)
            ke = _mx(km.astype(F32) * ecm)
            vb = _mx(vh_ref[0, rows, :])
            sg = _sigmoid(hg)
            d_on = dyh * (hg * sg)
            x_p, y_p, dv_p, n_p, ex_p = [], [], [], [], []
            for h in range(HEADS):
                ln = slice(h * HEAD_D, (h + 1) * HEAD_D)
                o_h = ov[:, ln]
                rstd = lax.rsqrt(jnp.mean(o_h * o_h, axis=-1, keepdims=True) + EPS)
                n = o_h * rstd
                dn = d_on[:, ln] * gwv[:, ln]
                d_o = rstd * (dn - n * jnp.mean(dn * n, axis=-1, keepdims=True))
                d_ob = _mx(d_o)
                d_ot = _mx(d_o.T)
                s0 = st_ref[c, h]
                ds = dstate[h]
                att = jnp.where(causal, _dot_nt(qm[:, ln], km[:, ln]), 0.0)
                d_att = jnp.where(causal, _dot_nt(d_ob, vb[:, ln]), 0.0)
                x_p.append(jnp.dot(_pad_lanes([d_ob, _mx(d_att)], wide),
                                   _pad_rows([_mx(s0 * ebm[:, ln]), km[:, ln]], wide), preferred_element_type=F32))
                y_p.append(jnp.dot(_pad_lanes([vb[:, ln], _mx(d_att.T)], wide),
                                   _pad_rows([_mx(ds * ecm[:, ln]), qm[:, ln]], wide), preferred_element_type=F32))
                dv_p.append(_dot_nt(_pad_lanes([ke[:, ln], _mx(att.T)], wide), _pad_lanes([_mx(ds), d_ot], wide)))
                ex_p.append(jnp.sum(ds * snext[h], axis=0, keepdims=True))
                dstate[h] = ds * ebc[:, ln] + jnp.dot(d_ot, qm[:, ln], preferred_element_type=F32) * ebm[:, ln]
                snext[h] = s0
                n_p.append(n)
            cat = lambda parts: jnp.concatenate(parts, axis=1)
            x_all, y_all, n_all = cat(x_p), cat(y_p), cat(n_p)
            dg = _cumsum_rows(qm.astype(F32) * x_all - km.astype(F32) * y_all, reverse=True) + cat(ex_p)
            df = dg / f - em2 * y_all
            dgw_ref[...] += jnp.sum(d_on * n_all, axis=0, keepdims=True)
            dlb_ref[...] += jnp.sum(df * (1.0 - sig), axis=0, keepdims=True)
            dp_ref[0, rows, :] = ((em1 * x_all) * (sq * (1.0 + q * (1.0 - sq)))).astype(_MXU_DTYPE)
            dp_ref[1, rows, :] = (df * ((1.0 - lb) * sig * (1.0 - sig))).astype(_MXU_DTYPE)
            dp_ref[2, rows, :] = cat(dv_p).astype(_MXU_DTYPE)
            dp_ref[3, rows, :] = (dyh * (n_all * gwv) * (sg * (1.0 + hg * (1.0 - sg)))).astype(_MXU_DTYPE)

    rev = lambda i: nt - 1 - i
    pair = lambda s: pl.BlockSpec((2, tm, D_MODEL), lambda i: (s, rev(i), 0))
    row = pl.BlockSpec((tm, D_MODEL), lambda i: (rev(i), 0))
    vec = _full((1, D_MODEL))
    return pl.pallas_call(
        body, name="hgrn_bwd", grid=(nt,),
        in_specs=[row, pl.BlockSpec((D_MODEL, D_MODEL), lambda i: (1, 0)), pair(1), pair(2), row,
                  pl.BlockSpec((nc, HEADS, HEAD_D, HEAD_D), lambda i: (rev(i), 0, 0, 0)),
                  _full((2, D_MODEL)), vec, pl.BlockSpec(memory_space=pl.ANY)],
        out_specs=[pl.BlockSpec((4, tm, D_MODEL), lambda i: (0, rev(i), 0)), vec, vec],
        out_shape=[jax.ShapeDtypeStruct((N_SEG, t, D_MODEL), _MXU_DTYPE),
                   jax.ShapeDtypeStruct((1, D_MODEL), F32), jax.ShapeDtypeStruct((1, D_MODEL), F32)],
        scratch_shapes=[pltpu.VMEM((tm, D_MODEL), F32), pltpu.VMEM((HEADS, HEAD_D, HEAD_D), F32),
                        pltpu.VMEM((HEADS, HEAD_D, HEAD_D), F32)],
        input_output_aliases={8: 0},
        compiler_params=_seq_params(),
    )(dy, wo, p, p, o, states, logits, gw, dp)


def _dp_unit(m):
    return ((m // 2 + 2) % N_SEG) * 2 + m % 2


def _dw_in(ut, dp, tt):
    t = ut.shape[1]
    hr = D_MODEL // 2

    def body(ut_ref, dp_ref, g_ref):
        @pl.when(pl.program_id(1) == 0)
        def _():
            g_ref[...] = jnp.zeros_like(g_ref)

        r = jnp.dot(ut_ref[...], dp_ref[...], preferred_element_type=F32)
        for h in range(2):
            for half in range(2):
                g_ref[h, half] += r[h * hr:(h + 1) * hr, half * UNIT:(half + 1) * UNIT]

    return pl.pallas_call(
        body, name="dw_in", grid=(N_SEG, t // tt),
        in_specs=[pl.BlockSpec((D_MODEL, tt), lambda s, k: (0, k)),
                  pl.BlockSpec((None, tt, D_MODEL), lambda s, k: (s, k, 0))],
        out_specs=pl.BlockSpec((2, 2, hr, UNIT), lambda s, k: (0, (s + 2) % N_SEG, 0, 0)),
        out_shape=jax.ShapeDtypeStruct((2, N_UNIT, hr, UNIT), F32),
        compiler_params=pltpu.CompilerParams(dimension_semantics=("arbitrary", "arbitrary")),
    )(ut, dp)


def _dx(dp, wg, x, dout, wpre, tm):
    t = x.shape[0]

    def body(dp_ref, w_ref, x_ref, dout_ref, wpre_ref, gx_ref, dw_ref):
        @pl.when(pl.program_id(0) == 0)
        def _():
            dw_ref[...] = jnp.zeros_like(dw_ref)

        du = None
        for m in range(N_UNIT):
            part = _dot_nt(dp_ref[m // 2, :, (m % 2) * UNIT:(m % 2 + 1) * UNIT], w_ref[_dp_unit(m)])
            du = part if du is None else du + part
        xv = x_ref[...]
        rstd = lax.rsqrt(jnp.mean(xv * xv, axis=-1, keepdims=True) + EPS)
        n = xv * rstd
        dw_ref[...] += jnp.sum(du * n, axis=0, keepdims=True)
        dn = du * wpre_ref[...]
        gx_ref[...] = dout_ref[...] + rstd * (dn - n * jnp.mean(dn * n, axis=-1, keepdims=True))

    row = pl.BlockSpec((tm, D_MODEL), lambda i: (i, 0))
    return pl.pallas_call(
        body, name="dx", grid=(t // tm,),
        in_specs=[pl.BlockSpec((N_SEG, tm, D_MODEL), lambda i: (0, i, 0)), _resident((N_UNIT, D_MODEL, UNIT)),
                  row, row, _full((1, D_MODEL))],
        out_specs=[row, _full((1, D_MODEL))],
        out_shape=[jax.ShapeDtypeStruct((t, D_MODEL), F32), jax.ShapeDtypeStruct((1, D_MODEL), F32)],
        compiler_params=_seq_params(),
    )(dp, wg, x, dout, wpre)


CHUNK_BYTES = 256 * 1024
MAX_CHUNKS = 16


def _n_chunks(shape, dtype, rows):
    nbytes = jnp.dtype(dtype).itemsize
    for d in shape:
        nbytes *= d
    k = max(1, min(MAX_CHUNKS, rows, nbytes // CHUNK_BYTES))
    while rows % k:
        k -= 1
    return k


def _gather_weights(shards, split):
    n = len(shards)
    ks = [_n_chunks(a.shape[1:], a.dtype, a.shape[1]) for a in shards]
    offs = [sum(ks[:a]) for a in range(n)]
    total = sum(ks)

    def body(*refs):
        ins, outs = refs[:n], refs[n:2 * n]
        send_sems, recv_sems, fsend_sems, frecv_sems, osend_sems, orecv_sems = refs[2 * n:]
        x, y, c = lax.axis_index("x"), lax.axis_index("y"), lax.axis_index("c")
        me = 2 * x + y
        peers = [(1 - x, y), (x, 1 - y), (1 - x, 1 - y)]
        half = lambda a: c if split[a] else 0

        def rows(a, k):
            step = shards[a].shape[1] // ks[a]
            return pl.ds(k * step, step)

        def fetch(a, j, k, slot):
            px, py = peers[j]
            sem = 3 * (offs[a] + k) + j
            return pltpu.make_async_remote_copy(
                src_ref=ins[a].at[half(a), rows(a, k)], dst_ref=outs[a].at[slot, half(a), rows(a, k)],
                send_sem=send_sems.at[sem], recv_sem=recv_sems.at[sem],
                device_id=(px, py, c), device_id_type=MESH)

        def forward(a, j, k, h):
            px, py = peers[j]
            sem = 3 * (offs[a] + k) + j
            piece = outs[a].at[2 * px + py, h, rows(a, k)]
            return pltpu.make_async_remote_copy(
                src_ref=piece, dst_ref=piece, send_sem=fsend_sems.at[sem], recv_sem=frecv_sems.at[sem],
                device_id=(x, y, 1 - c), device_id_type=MESH)

        def own(a, h, k):
            sem = 2 * (offs[a] + k) + h
            return pltpu.make_async_remote_copy(
                src_ref=ins[a].at[h, rows(a, k)], dst_ref=outs[a].at[me, h, rows(a, k)],
                send_sem=osend_sems.at[sem], recv_sem=orecv_sems.at[sem],
                device_id=(x, y, 1 - c), device_id_type=MESH)

        pieces = [(a, k) for a in range(n) for k in range(ks[a])]
        halves = lambda a: range(2 if split[a] else 1)
        for a, k in pieces:
            for j in range(3):
                fetch(a, j, k, me).start()
        for a, k in pieces:
            for h in halves(a):
                own(a, h, k).start()
        for a, k in pieces:
            for j in range(3):
                px, py = peers[j]
                fetch(a, j, k, 2 * px + py).wait_recv()
                if split[a]:
                    forward(a, j, k, c).start()
        for a, k in pieces:
            for j in range(3):
                if split[a]:
                    forward(a, j, k, 1 - c).wait_recv()
            for h in halves(a):
                own(a, h, k).wait_recv()
        for a, k in pieces:
            for j in range(3):
                fetch(a, j, k, me).wait_send()
                if split[a]:
                    forward(a, j, k, c).wait_send()
            for h in halves(a):
                own(a, h, k).wait_send()

    any_spec = pl.BlockSpec(memory_space=pl.ANY)
    return pl.pallas_call(
        body, name="gather_weights",
        in_specs=[any_spec] * n, out_specs=[any_spec] * n,
        out_shape=[jax.ShapeDtypeStruct((N_CHIPS,) + a.shape, a.dtype) for a in shards],
        scratch_shapes=[pltpu.SemaphoreType.DMA((3 * total,)), pltpu.SemaphoreType.DMA((3 * total,)),
                        pltpu.SemaphoreType.DMA((3 * total,)), pltpu.SemaphoreType.DMA((3 * total,)),
                        pltpu.SemaphoreType.DMA((2 * total,)), pltpu.SemaphoreType.DMA((2 * total,))],
    )(*shards)


def _chip_exchange(arrays, scatter, name):
    n = len(arrays)
    slab_shapes = [a.shape[1:] if sc else a.shape for a, sc in zip(arrays, scatter)]

    def body(*refs):
        ins, outs = refs[:n], refs[n:2 * n]
        send_sems, recv_sems = refs[2 * n:]
        x, y, c = lax.axis_index("x"), lax.axis_index("y"), lax.axis_index("c")
        peers = [(1 - x, y), (x, 1 - y), (1 - x, 1 - y)]

        def copy(a, j):
            px, py = peers[j]
            return pltpu.make_async_remote_copy(
                src_ref=ins[a].at[2 * px + py] if scatter[a] else ins[a], dst_ref=outs[a].at[j],
                send_sem=send_sems.at[a * 3 + j], recv_sem=recv_sems.at[a * 3 + j],
                device_id=(px, py, c), device_id_type=MESH)

        copies = [copy(a, j) for a in range(n) for j in range(3)]
        for cp in copies:
            cp.start()
        for cp in copies:
            cp.wait_recv()
        for cp in copies:
            cp.wait_send()

    any_spec = pl.BlockSpec(memory_space=pl.ANY)
    return pl.pallas_call(
        body, name=name,
        in_specs=[any_spec] * n, out_specs=[any_spec] * n,
        out_shape=[jax.ShapeDtypeStruct((3,) + tuple(s), a.dtype) for s, a in zip(slab_shapes, arrays)],
        scratch_shapes=[pltpu.SemaphoreType.DMA((3 * n,)), pltpu.SemaphoreType.DMA((3 * n,))],
    )(*arrays)


def _core_swap(arrays, name, halved):
    n = len(arrays)
    shapes = [a.shape[1:] if halved else a.shape for a in arrays]
    ks = [_n_chunks(s, a.dtype, s[0]) for s, a in zip(shapes, arrays)]
    offs = [sum(ks[:a]) for a in range(n)]

    def body(*refs):
        ins, outs = refs[:n], refs[n:2 * n]
        send_sems, recv_sems = refs[2 * n:]
        x, y, c = lax.axis_index("x"), lax.axis_index("y"), lax.axis_index("c")

        def copy(a, j):
            step = shapes[a][0] // ks[a]
            rows = pl.ds(j * step, step)
            return pltpu.make_async_remote_copy(
                src_ref=ins[a].at[1 - c, rows] if halved else ins[a].at[rows], dst_ref=outs[a].at[rows],
                send_sem=send_sems.at[offs[a] + j], recv_sem=recv_sems.at[offs[a] + j],
                device_id=(x, y, 1 - c), device_id_type=MESH)

        pieces = [(a, j) for a in range(n) for j in range(ks[a])]
        for a, j in pieces:
            copy(a, j).start()
        for a, j in pieces:
            copy(a, j).wait_recv()
        for a, j in pieces:
            copy(a, j).wait_send()

    any_spec = pl.BlockSpec(memory_space=pl.ANY)
    return pl.pallas_call(
        body, name=name,
        in_specs=[any_spec] * n, out_specs=[any_spec] * n,
        out_shape=[jax.ShapeDtypeStruct(s, a.dtype) for s, a in zip(shapes, arrays)],
        scratch_shapes=[pltpu.SemaphoreType.DMA((sum(ks),)), pltpu.SemaphoreType.DMA((sum(ks),))],
    )(*arrays)


def _sum_own_half(g, b, name, out_dtype, tr=512):
    _, rows, cols = g.shape
    tr = min(tr, rows)
    core = lax.axis_index("c").reshape(1)

    def body(c_ref, g_ref, b_ref, o_ref):
        del c_ref
        o_ref[...] = (g_ref[...] + b_ref[...]).astype(out_dtype)

    return pl.pallas_call(
        body, name=name,
        grid_spec=pltpu.PrefetchScalarGridSpec(
            num_scalar_prefetch=1, grid=(rows // tr,),
            in_specs=[pl.BlockSpec((None, tr, cols), lambda i, c_ref: (c_ref[0], i, 0)),
                      pl.BlockSpec((tr, cols), lambda i, c_ref: (i, 0))],
            out_specs=pl.BlockSpec((tr, cols), lambda i, c_ref: (i, 0))),
        out_shape=jax.ShapeDtypeStruct((rows, cols), out_dtype),
        compiler_params=pltpu.CompilerParams(dimension_semantics=("parallel",)),
    )(core, g, b)


def _sum_chips(own, r, slabbed, name, tr=512):
    _, rows, cols = r.shape
    tr = min(tr, rows)
    chip = (2 * lax.axis_index("x") + lax.axis_index("y")).reshape(1)

    def body(c_ref, own_ref, r_ref, o_ref):
        del c_ref
        f = lambda val: val.astype(F32)
        o_ref[...] = (f(own_ref[...]) + f(r_ref[0])) + (f(r_ref[1]) + f(r_ref[2]))

    if slabbed:
        own_spec = pl.BlockSpec((None, tr, cols), lambda i, c_ref: (c_ref[0], i, 0))
    else:
        own_spec = pl.BlockSpec((tr, cols), lambda i, c_ref: (i, 0))
    return pl.pallas_call(
        body, name=name,
        grid_spec=pltpu.PrefetchScalarGridSpec(
            num_scalar_prefetch=1, grid=(rows // tr,),
            in_specs=[own_spec, pl.BlockSpec((3, tr, cols), lambda i, c_ref: (0, i, 0))],
            out_specs=pl.BlockSpec((tr, cols), lambda i, c_ref: (i, 0))),
        out_shape=jax.ShapeDtypeStruct((rows, cols), F32),
        compiler_params=pltpu.CompilerParams(dimension_semantics=("parallel",)),
    )(chip, own, r)


def _adamw_big(own, other, w, m, v, name, grid, g_block, g_index, w_block, w_index):
    core = lax.axis_index("c").reshape(1)
    half_axis = len(grid) - 1

    def body(c_ref, own_ref, oth_ref, w_ref, m_ref, v_ref, g_ref, d_ref, nm_ref, nv_ref):
        mine = pl.program_id(half_axis) == c_ref[0]
        gv = jnp.where(mine, own_ref[...], oth_ref[...])
        g_ref[...] = gv
        d_ref[...], nm_ref[...], nv_ref[...] = _adamw(w_ref[...], gv, m_ref[...], v_ref[...])

    g_spec = pl.BlockSpec(g_block, lambda *a: g_index(*a[:half_axis]))
    w_spec = pl.BlockSpec(w_block, lambda *a: w_index(*a[:-1]))
    shp = jax.ShapeDtypeStruct(w.shape, F32)
    return pl.pallas_call(
        body, name=name,
        grid_spec=pltpu.PrefetchScalarGridSpec(
            num_scalar_prefetch=1, grid=grid,
            in_specs=[g_spec, g_spec, w_spec, w_spec, w_spec], out_specs=[w_spec] * 4),
        out_shape=[shp] * 4,
        compiler_params=pltpu.CompilerParams(dimension_semantics=("parallel",) * len(grid)),
    )(core, own, other, w, m, v)


def _small_finish(own, other, lam, logits):
    def body(a_ref, b_ref, lam_ref, lg_ref, o_ref):
        g = a_ref[...] + b_ref[...]
        o_ref[...] = g
        o_ref[8:9, :] = g[8:9, :] * (-_sigmoid(-lam_ref[...]))
        lb = 1.0 / (1.0 + jnp.exp(lg_ref[1:2, :] - lg_ref[0:1, :]))
        d0 = g[9:10, :] * (lb * (1.0 - lb))
        o_ref[9:10, :] = d0
        o_ref[10:11, :] = -d0

    return pl.pallas_call(
        body, name="small_finish",
        in_specs=[_full((SMALL_ROWS, D_MODEL)), _full((SMALL_ROWS, D_MODEL)), _full((1, D_MODEL)), _full((2, D_MODEL))],
        out_specs=_full((SMALL_ROWS, D_MODEL)),
        out_shape=jax.ShapeDtypeStruct((SMALL_ROWS, D_MODEL), F32),
    )(own, other, lam, logits)


def _adamw_small(ws, gs, ms, vs):
    n = len(ws)

    def body(*refs):
        w_r, g_r, m_r, v_r = refs[:n], refs[n:2 * n], refs[2 * n:3 * n], refs[3 * n:4 * n]
        d_o, m_o, v_o = refs[4 * n:5 * n], refs[5 * n:6 * n], refs[6 * n:7 * n]
        for j in range(n):
            d_o[j][...], m_o[j][...], v_o[j][...] = _adamw(w_r[j][...], g_r[j][...], m_r[j][...], v_r[j][...])

    specs = [_full(w.shape) for w in ws]
    shapes = [jax.ShapeDtypeStruct(w.shape, F32) for w in ws]
    outs = pl.pallas_call(
        body, name="adamw_small",
        in_specs=specs * 4, out_specs=specs * 3, out_shape=shapes * 3,
    )(*ws, *gs, *ms, *vs)
    return outs[:n], outs[n:2 * n], outs[2 * n:]


def _local_step(x, tgt, pre_w, wg, cw, cb, wa, wx, ba, bx, lam, logits, gw, wo, post_w, tm, tmm):
    ut, p = _in_proj(x, pre_w, wg, tm)
    h, y2 = _lru_fwd(p, wa, wx, ba, bx, lam, cw, cb, tm)
    y2, o, states = _hgrn_fwd(p, logits, gw, y2, tm)
    dout, dy, sq, d_post = _out_fused(y2, wo, x, tgt, post_w, tmm)
    g_out = _dw_out(y2, dy, tmm)
    dp, d_wa, d_wx, d_ba, d_bx, d_sp, d_cw, d_cb = _lru_bwd(dy, wo, p, h, wa, wx, ba, bx, lam, cw, cb, tm)
    dp, d_gw, d_lb = _hgrn_bwd(dy, wo, p, o, states, logits, gw, dp, tm)
    g_in = _dw_in(ut, dp, min(2048, x.shape[0]))
    grad_x, d_pre = _dx(dp, wg, x, dout, pre_w, tm)
    small = jnp.concatenate([d_pre, d_cw, d_cb, d_ba, d_bx, d_sp, d_lb,
                             jnp.zeros((1, D_MODEL), F32), d_gw, d_post,
                             jnp.zeros((SMALL_ROWS - 13, D_MODEL), F32)], axis=0)
    return sq, grad_x, g_in, g_out, d_wa, d_wx, small


def kernel(x, pre_norm_w, w_in, conv_w, conv_b, lru_w_a, lru_b_a, lru_w_x, lru_b_x, lru_lambda, hgrn_lb_logits, hgrn_gnorm_w, w_out, post_norm_w, loss_target, m_pre_norm_w, m_w_in, m_conv_w, m_conv_b, m_lru_w_a, m_lru_b_a, m_lru_w_x, m_lru_b_x, m_lru_lambda, m_hgrn_lb_logits, m_hgrn_gnorm_w, m_w_out, m_post_norm_w, v_pre_norm_w, v_w_in, v_conv_w, v_conv_b, v_lru_w_a, v_lru_b_a, v_lru_w_x, v_lru_b_x, v_lru_lambda, v_hgrn_lb_logits, v_hgrn_gnorm_w, v_w_out, v_post_norm_w):
    t = x.shape[1]
    tm = min(256, t)
    tmm = min(512, t)
    chip = 2 * lax.axis_index("x") + lax.axis_index("y")

    win_units = w_in[0].astype(_MXU_DTYPE).reshape(D_MODEL, 3, UNIT).transpose(1, 0, 2)
    small_w = jnp.concatenate([conv_w[0], lru_b_a[0], lru_b_x[0]], axis=1)
    wg4, wo4, wa4, wx4, sw4 = _gather_weights(
        [win_units.reshape(2, 3 * 512, UNIT), w_out[0].astype(_MXU_DTYPE).reshape(2, 256, D_MODEL),
         lru_w_a[0].astype(_MXU_DTYPE).reshape(2, 128, LRU_BW), lru_w_x[0].astype(_MXU_DTYPE).reshape(2, 128, LRU_BW),
         small_w.reshape(1, 4, 384)],
        [True, True, True, True, False])
    wg = wg4.reshape(N_UNIT, D_MODEL, UNIT)
    wo = wo4.reshape(2 * D_MODEL, D_MODEL)
    by_block = lambda w4: w4.reshape(N_CHIPS, LRU_BLOCKS, 64, LRU_BW).transpose(1, 0, 2, 3).reshape(
        LRU_BLOCKS, LRU_BW, LRU_BW)
    wa, wx = by_block(wa4), by_block(wx4)
    sw4 = sw4.reshape(N_CHIPS, 4, 384)
    cw = sw4[:, :, 0:256].transpose(1, 0, 2).reshape(4, D_MODEL)
    ba = sw4[:, :, 256:320].transpose(1, 0, 2).reshape(1, D_MODEL)
    bx = sw4[:, :, 320:384].transpose(1, 0, 2).reshape(1, D_MODEL)

    sq, grad_x, g_in, g_out, d_wa, d_wx, small = _local_step(
        x[0], loss_target[0], pre_norm_w, wg, cw, conv_b, wa, wx, ba, bx, lru_lambda, hgrn_lb_logits,
        hgrn_gnorm_w, wo, post_norm_w, tm, tmm)
    loss = lax.psum(jnp.sum(sq) * (0.5 / D_MODEL), ("x", "y", "c"))

    def lru_layout(g):
        g = g.reshape(2, 2, N_CHIPS, 64, LRU_BW).transpose(0, 2, 1, 3, 4)
        return g.reshape(2, N_CHIPS * 128, LRU_BW)

    g_in = g_in.reshape(2, N_UNIT * 512, UNIT)
    g_out = g_out.reshape(2, N_CHIPS * 256, D_MODEL)
    g_wa, g_wx = lru_layout(d_wa), lru_layout(d_wx)
    b_in, b_out, b_wa, b_wx = _core_swap([g_in, g_out, g_wa, g_wx], "reduce_swap", halved=True)
    p_in = _sum_own_half(g_in, b_in, "presum_in", _WIRE_DTYPE).reshape(N_CHIPS, 3 * 512, UNIT)
    p_out = _sum_own_half(g_out, b_out, "presum_out", _WIRE_DTYPE).reshape(N_CHIPS, 256, D_MODEL)
    p_wa = _sum_own_half(g_wa, b_wa, "presum_wa", _WIRE_DTYPE).reshape(N_CHIPS, 128, LRU_BW)
    p_wx = _sum_own_half(g_wx, b_wx, "presum_wx", _WIRE_DTYPE).reshape(N_CHIPS, 128, LRU_BW)
    r_in, r_out, r_wa, r_wx, r_small = _chip_exchange(
        [p_in, p_out, p_wa, p_wx, small], [True, True, True, True, False], "reduce_chips")
    s_in = _sum_chips(p_in, r_in, True, "sum_in")
    s_out = _sum_chips(p_out, r_out, True, "sum_out")
    s_wa = _sum_chips(p_wa, r_wa, True, "sum_wa")
    s_wx = _sum_chips(p_wx, r_wx, True, "sum_wx")
    s_small = _sum_chips(small, r_small, False, "sum_small")
    o_in, o_out, o_wa, o_wx, o_small = _core_swap([s_in, s_out, s_wa, s_wx, s_small], "reduce_cores", halved=False)

    g_w_in, d_w_in, nm_w_in, nv_w_in = _adamw_big(
        s_in.reshape(3, 512, UNIT), o_in.reshape(3, 512, UNIT), w_in[0], m_w_in[0], v_w_in[0], "adamw_w_in", (3, 2),
        (None, 512, UNIT), lambda k: (k, 0, 0), (512, UNIT), lambda k, h: (h, k))
    g_w_out, d_w_out, nm_w_out, nv_w_out = _adamw_big(
        s_out, o_out, w_out[0], m_w_out[0], v_w_out[0], "adamw_w_out", (2,),
        (256, D_MODEL), lambda: (0, 0), (256, D_MODEL), lambda h: (h, 0))
    sq2 = lambda a: a.reshape(LRU_BW, LRU_BW)
    lru_specs = ((2,), (128, LRU_BW), lambda: (0, 0), (128, LRU_BW), lambda h: (h, 0))
    g_wa, d_wa2, nm_wa, nv_wa = _adamw_big(s_wa, o_wa, sq2(lru_w_a), sq2(m_lru_w_a), sq2(v_lru_w_a), "adamw_wa",
                                           *lru_specs)
    g_wx, d_wx2, nm_wx, nv_wx = _adamw_big(s_wx, o_wx, sq2(lru_w_x), sq2(m_lru_w_x), sq2(v_lru_w_x), "adamw_wx",
                                           *lru_specs)

    gs = _small_finish(s_small, o_small, lru_lambda, hgrn_lb_logits)
    g_pre = gs[0:1]
    g_cw = lax.dynamic_slice(gs[1:5], (0, chip * 256), (4, 256))
    g_cb = gs[5:6]
    g_ba = lax.dynamic_slice(gs[6].reshape(LRU_BLOCKS, N_CHIPS, 64), (0, chip, 0), (LRU_BLOCKS, 1, 64)).reshape(4, 64)
    g_bx = lax.dynamic_slice(gs[7].reshape(LRU_BLOCKS, N_CHIPS, 64), (0, chip, 0), (LRU_BLOCKS, 1, 64)).reshape(4, 64)
    g_lam = gs[8:9]
    g_lb = gs[9:11]
    g_gw = gs[11:12]
    g_post = gs[12:13]
    small_g = [g_pre, g_cw, g_cb, g_ba, g_bx, g_lam, g_lb, g_gw, g_post]
    two_d = lambda a: a.reshape(a.shape[-2:])
    small_w_list = [pre_norm_w, conv_w, conv_b, lru_b_a, lru_b_x, lru_lambda, hgrn_lb_logits, hgrn_gnorm_w, post_norm_w]
    small_m_list = [m_pre_norm_w, m_conv_w, m_conv_b, m_lru_b_a, m_lru_b_x, m_lru_lambda, m_hgrn_lb_logits,
                    m_hgrn_gnorm_w, m_post_norm_w]
    small_v_list = [v_pre_norm_w, v_conv_w, v_conv_b, v_lru_b_a, v_lru_b_x, v_lru_lambda, v_hgrn_lb_logits,
                    v_hgrn_gnorm_w, v_post_norm_w]
    sd, sm, sv = _adamw_small([two_d(a) for a in small_w_list], small_g,
                              [two_d(a) for a in small_m_list], [two_d(a) for a in small_v_list])

    def shaped(vals, refs):
        return [val.reshape(ref.shape) for val, ref in zip(vals, refs)]

    s_g = shaped(small_g, small_w_list)
    s_d = shaped(sd, small_w_list)
    s_m = shaped(sm, small_w_list)
    s_v = shaped(sv, small_w_list)

    def ordered(small, big_in, big_wa, big_wx, big_out):
        pre, cw_, cb_, ba_, bx_, lam_, lb_, gw_, post_ = small
        return [pre, big_in.reshape(w_in.shape), cw_, cb_, big_wa.reshape(lru_w_a.shape), ba_,
                big_wx.reshape(lru_w_x.shape), bx_, lam_, lb_, gw_, big_out.reshape(w_out.shape), post_]

    grads = ordered(s_g, g_w_in, g_wa, g_wx, g_w_out)
    deltas = ordered(s_d, d_w_in, d_wa2, d_wx2, d_w_out)
    new_m = ordered(s_m, nm_w_in, nm_wa, nm_wx, nm_w_out)
    new_v = ordered(s_v, nv_w_in, nv_wa, nv_wx, nv_w_out)
    return (loss, grad_x.reshape(x.shape), *grads, *deltas, *new_m, *new_v)
```

```python
import functools

import jax
import jax.numpy as jnp
from jax import lax
from jax.experimental import pallas as pl
from jax.experimental.pallas import tpu as pltpu

F32 = jnp.float32
BF16 = jnp.bfloat16
_MXU_DTYPE = jnp.bfloat16
_WIRE_DTYPE = jnp.bfloat16

D_MODEL = 1024
N_SEG = 6
UNIT = 512
N_UNIT = 12
LRU_BLOCKS = 4
LRU_BW = 256
LRU_C = 8.0
HEADS = 8
HEAD_D = 128
CHUNK = 64
EPS = 1e-6
N_CHIPS = 4
EXP_CLAMP = 80.0

ADAM_LR = 0.001
ADAM_B1 = 0.9
ADAM_B2 = 0.999
ADAM_EPS = 1e-08
ADAM_WD = 0.01
ADAM_STEP = 10

SMALL_ROWS = 16
MESH = pl.DeviceIdType.MESH


def _sigmoid(x):
    return 1.0 / (1.0 + jnp.exp(-x))


def _mx(x):
    return x.astype(_MXU_DTYPE)


def _dot(a, b):
    return jnp.dot(_mx(a), _mx(b), preferred_element_type=F32)


def _dot_nt(a, b):
    return lax.dot_general(_mx(a), _mx(b), (((1,), (1,)), ((), ())), preferred_element_type=F32)


def _dot_tn(a, b):
    return lax.dot_general(_mx(a), _mx(b), (((0,), (0,)), ((), ())), preferred_element_type=F32)


def _rows(shape):
    return lax.broadcasted_iota(jnp.int32, shape, 0)


def _cols(shape):
    return lax.broadcasted_iota(jnp.int32, shape, 1)


def _softplus_neg(lam):
    z = -lam
    e = jnp.exp(-jnp.abs(z))
    series = e * (1.0 - e * (0.5 - e * (1.0 / 3.0 - 0.25 * e)))
    return jnp.maximum(z, 0.0) + jnp.where(e < 1e-2, series, jnp.log(1.0 + e))


def _neg_expm1(y):
    series = -y * (1.0 + y * (0.5 + y * (1.0 / 6.0 + y * (1.0 / 24.0))))
    return jnp.where(y > -0.03, series, 1.0 - jnp.exp(y))


def _cumsum_rows(x, reverse=False):
    n = x.shape[0] // 8
    row8 = _rows((8, x.shape[1]))
    out = [None] * n
    carry = None
    for g in (reversed(range(n)) if reverse else range(n)):
        blk = x[8 * g:8 * g + 8]
        for s in (1, 2, 4):
            if reverse:
                blk = blk + jnp.where(row8 < 8 - s, pltpu.roll(blk, 8 - s, 0), 0.0)
            else:
                blk = blk + jnp.where(row8 >= s, pltpu.roll(blk, s, 0), 0.0)
        if carry is not None:
            blk = blk + carry
        carry = blk[0:1] if reverse else blk[7:8]
        out[g] = blk
    return jnp.concatenate(out, axis=0)


def _shift_down(x, halo, s):
    if s == 0:
        return x
    r = pltpu.roll(x, s, 0)
    top = jnp.where(_rows(halo.shape) < s, pltpu.roll(halo, s, 0), r[0:8])
    return jnp.concatenate([top, r[8:]], axis=0)


def _shift_up(x, head, s):
    if s == 0:
        return x
    n = x.shape[0]
    r = pltpu.roll(x, n - s, 0)
    bottom = jnp.where(_rows(head.shape) >= 8 - s, pltpu.roll(head, 8 - s, 0), r[n - 8:n])
    return jnp.concatenate([r[:n - 8], bottom], axis=0)


def _lru_gates(lx, halo, cw, cb, wa, wx, ba, bx, sp):
    xs = [_shift_down(lx, halo, 3 - k) for k in range(4)]
    xc = cb + cw[0:1] * xs[0] + cw[1:2] * xs[1] + cw[2:3] * xs[2] + cw[3:4] * xs[3]
    zr, zi = [], []
    for n in range(LRU_BLOCKS):
        xb = _mx(xc[:, n * LRU_BW:(n + 1) * LRU_BW])
        zr.append(jnp.dot(xb, wa[n], preferred_element_type=F32))
        zi.append(jnp.dot(xb, wx[n], preferred_element_type=F32))
    r = _sigmoid(jnp.concatenate(zr, axis=1) + ba)
    ig = _sigmoid(jnp.concatenate(zi, axis=1) + bx)
    la = (-LRU_C * sp) * r
    a = jnp.exp(la)
    mult = jnp.sqrt(_neg_expm1(2.0 * la))
    return xs, xc, r, ig, a, mult


def _hgrn_prep(q, fr, lb):
    sig = _sigmoid(fr)
    f = lb + (1.0 - lb) * sig
    sq = _sigmoid(q)
    b = _cumsum_rows(jnp.log(f))
    bm = b[CHUNK // 2 - 1:CHUNK // 2]
    bc = b[CHUNK - 1:CHUNK]
    em1 = jnp.exp(jnp.minimum(b - bm, EXP_CLAMP))
    em2 = jnp.exp(jnp.minimum(bm - b, EXP_CLAMP))
    qm = _mx((q * sq) * em1)
    km = _mx((1.0 - f) * em2)
    return sig, f, sq, em1, em2, qm, km, jnp.exp(bm), jnp.exp(bc - bm), jnp.exp(bc)


def _pad_lanes(parts, width):
    have = sum(p.shape[1] for p in parts)
    pad = [jnp.zeros((parts[0].shape[0], width - have), parts[0].dtype)] if width > have else []
    return jnp.concatenate(list(parts) + pad, axis=1)


def _pad_rows(parts, height):
    have = sum(p.shape[0] for p in parts)
    pad = [jnp.zeros((height - have, parts[0].shape[1]), parts[0].dtype)] if height > have else []
    return jnp.concatenate(list(parts) + pad, axis=0)


def _adamw(w, g, m, v):
    m = ADAM_B1 * m + (1.0 - ADAM_B1) * g
    v = ADAM_B2 * v + (1.0 - ADAM_B2) * jnp.square(g)
    m_hat = m / (1.0 - ADAM_B1 ** ADAM_STEP)
    v_hat = v / (1.0 - ADAM_B2 ** ADAM_STEP)
    delta = -ADAM_LR * (m_hat / (jnp.sqrt(v_hat) + ADAM_EPS) + ADAM_WD * w)
    return delta, m, v


def _full(shape):
    nd = len(shape)
    return pl.BlockSpec(shape, lambda *_: (0,) * nd)


def _seq_params():
    return pltpu.CompilerParams(dimension_semantics=("arbitrary",))


def _resident(shape):
    nd = len(shape)
    return pl.BlockSpec(shape, lambda *_: (0,) * nd, pipeline_mode=pl.Buffered(1))


def _in_proj(x, w, wg, tm):
    t = x.shape[0]

    def body(x_ref, w_ref, wg_ref, ut_ref, p_ref):
        xv = x_ref[...]
        rstd = lax.rsqrt(jnp.mean(xv * xv, axis=-1, keepdims=True) + EPS)
        u = xv * rstd * w_ref[...]
        ut_ref[...] = u.T.astype(_MXU_DTYPE)
        ub = _mx(u)
        for m in range(N_UNIT):
            p_ref[m // 2, :, (m % 2) * UNIT:(m % 2 + 1) * UNIT] = jnp.dot(
                ub, wg_ref[m], preferred_element_type=F32)

    return pl.pallas_call(
        body, name="in_proj", grid=(t // tm,),
        in_specs=[pl.BlockSpec((tm, D_MODEL), lambda i: (i, 0)), _full((1, D_MODEL)),
                  _resident((N_UNIT, D_MODEL, UNIT))],
        out_specs=[pl.BlockSpec((D_MODEL, tm), lambda i: (0, i)),
                   pl.BlockSpec((N_SEG, tm, D_MODEL), lambda i: (0, i, 0))],
        out_shape=[jax.ShapeDtypeStruct((D_MODEL, t), _MXU_DTYPE),
                   jax.ShapeDtypeStruct((N_SEG, t, D_MODEL), F32)],
        compiler_params=pltpu.CompilerParams(dimension_semantics=("parallel",)),
    )(x, w, wg)


def _lru_fwd(p, wa, wx, ba, bx, lam, cw, cb, tm):
    t = p.shape[1]
    ng = tm // 8

    def body(lx_ref, halo_ref, lg_ref, wa_ref, wx_ref, ba_ref, bx_ref, lam_ref, cw_ref, cb_ref,
             h_ref, y_ref, a_s, u_s, hc_s):
        i = pl.program_id(0)

        @pl.when(i == 0)
        def _():
            hc_s[...] = jnp.zeros_like(hc_s)

        sp = _softplus_neg(lam_ref[...])
        halo = jnp.where(i == 0, 0.0, halo_ref[...])
        _, xc, _, ig, a, mult = _lru_gates(lx_ref[...], halo, cw_ref[...], cb_ref[...], wa_ref[...],
                                           wx_ref[...], ba_ref[...], bx_ref[...], sp)
        a_s[...] = a
        u_s[...] = mult * (ig * xc)
        row8 = _rows((8, D_MODEL))

        def group(g, hc):
            rows = pl.ds(pl.multiple_of(g * 8, 8), 8)
            av = a_s[rows, :]
            uv = u_s[rows, :]
            for s in (1, 2, 4):
                a_sh = jnp.where(row8 >= s, pltpu.roll(av, s, 0), 1.0)
                u_sh = jnp.where(row8 >= s, pltpu.roll(uv, s, 0), 0.0)
                uv = av * u_sh + uv
                av = av * a_sh
            hh = av * hc + uv
            h_ref[rows, :] = hh
            return jnp.broadcast_to(hh[7:8, :], (8, D_MODEL))

        hc_s[...] = lax.fori_loop(0, ng, group, hc_s[...])
        lg = lg_ref[...]
        y_ref[...] = (h_ref[...] * (lg * _sigmoid(lg))).astype(_MXU_DTYPE)

    seg = lambda s: pl.BlockSpec((None, tm, D_MODEL), lambda i: (s, i, 0))
    return pl.pallas_call(
        body, name="lru_fwd", grid=(t // tm,),
        in_specs=[seg(0),
                  pl.BlockSpec((None, 8, D_MODEL), lambda i: (0, jnp.maximum(i * (tm // 8) - 1, 0), 0)),
                  seg(1),
                  _full((LRU_BLOCKS, LRU_BW, LRU_BW)), _full((LRU_BLOCKS, LRU_BW, LRU_BW)),
                  _full((1, D_MODEL)), _full((1, D_MODEL)), _full((1, D_MODEL)),
                  _full((4, D_MODEL)), _full((1, D_MODEL))],
        out_specs=[pl.BlockSpec((tm, D_MODEL), lambda i: (i, 0)),
                   pl.BlockSpec((None, tm, D_MODEL), lambda i: (0, i, 0))],
        out_shape=[jax.ShapeDtypeStruct((t, D_MODEL), F32),
                   jax.ShapeDtypeStruct((2, t, D_MODEL), _MXU_DTYPE)],
        scratch_shapes=[pltpu.VMEM((tm, D_MODEL), F32), pltpu.VMEM((tm, D_MODEL), F32),
                        pltpu.VMEM((8, D_MODEL), F32)],
        compiler_params=_seq_params(),
    )(p, p, p, wa, wx, ba, bx, lam, cw, cb)


def _hgrn_fwd(p, logits, gw, y2, tm):
    t = p.shape[1]
    nc = tm // CHUNK

    def body(qf_ref, vh_ref, lg_ref, gw_ref, y_in, y_ref, o_ref, st_ref, state):
        del y_in
        i = pl.program_id(0)

        @pl.when(i == 0)
        def _():
            state[...] = jnp.zeros_like(state)

        l0 = lg_ref[0:1, :]
        l1 = lg_ref[1:2, :]
        lb = 1.0 / (1.0 + jnp.exp(l1 - l0))
        gwv = gw_ref[...]
        causal = _rows((CHUNK, CHUNK)) >= _cols((CHUNK, CHUNK))

        for c in range(nc):
            rows = pl.ds(c * CHUNK, CHUNK)
            v = vh_ref[0, rows, :]
            hg = vh_ref[1, rows, :]
            _, _, _, _, _, qm, km, ebm, ecm, ebc = _hgrn_prep(qf_ref[0, rows, :], qf_ref[1, rows, :], lb)
            ke = _mx(km.astype(F32) * ecm)
            gate = gwv * (hg * _sigmoid(hg))
            o_parts, y_parts = [], []
            for h in range(HEADS):
                ln = slice(h * HEAD_D, (h + 1) * HEAD_D)
                s0 = state[h]
                st_ref[c, h] = s0
                vt = _mx(v[:, ln].T)
                att = _mx(jnp.where(causal, _dot_nt(qm[:, ln], km[:, ln]), 0.0))
                o = _dot_nt(_pad_lanes([qm[:, ln], att], 2 * HEAD_D),
                            _pad_lanes([_mx(s0 * ebm[:, ln]), vt], 2 * HEAD_D))
                state[h] = s0 * ebc[:, ln] + jnp.dot(vt, ke[:, ln], preferred_element_type=F32)
                rstd = lax.rsqrt(jnp.mean(o * o, axis=-1, keepdims=True) + EPS)
                o_parts.append(o)
                y_parts.append((o * rstd * gate[:, ln]).astype(_MXU_DTYPE))
            o_ref[rows, :] = jnp.concatenate(o_parts, axis=1)
            y_ref[rows, :] = jnp.concatenate(y_parts, axis=1)

    pair = lambda s: pl.BlockSpec((2, tm, D_MODEL), lambda i: (s, i, 0))
    return pl.pallas_call(
        body, name="hgrn_fwd", grid=(t // tm,),
        in_specs=[pair(1), pair(2), _full((2, D_MODEL)), _full((1, D_MODEL)),
                  pl.BlockSpec(memory_space=pl.ANY)],
        out_specs=[pl.BlockSpec((None, tm, D_MODEL), lambda i: (1, i, 0)),
                   pl.BlockSpec((tm, D_MODEL), lambda i: (i, 0)),
                   pl.BlockSpec((nc, HEADS, HEAD_D, HEAD_D), lambda i: (i, 0, 0, 0))],
        out_shape=[jax.ShapeDtypeStruct((2, t, D_MODEL), _MXU_DTYPE),
                   jax.ShapeDtypeStruct((t, D_MODEL), F32),
                   jax.ShapeDtypeStruct((t // CHUNK, HEADS, HEAD_D, HEAD_D), F32)],
        scratch_shapes=[pltpu.VMEM((HEADS, HEAD_D, HEAD_D), F32)],
        input_output_aliases={4: 0},
        compiler_params=_seq_params(),
    )(p, p, logits, gw, y2)


def _out_fused(y2, wo, x, tgt, wpost, tm):
    t = x.shape[0]
    inv_d = 1.0 / D_MODEL

    def body(y_ref, wo_ref, x_ref, t_ref, w_ref, dout_ref, dy_ref, sq_ref, dw_ref):
        i = pl.program_id(0)

        @pl.when(i == 0)
        def _():
            sq_ref[...] = jnp.zeros_like(sq_ref)
            dw_ref[...] = jnp.zeros_like(dw_ref)

        y = (jnp.dot(y_ref[0], wo_ref[0:D_MODEL, :], preferred_element_type=F32)
             + jnp.dot(y_ref[1], wo_ref[D_MODEL:2 * D_MODEL, :], preferred_element_type=F32))
        w = w_ref[...]
        rstd = lax.rsqrt(jnp.mean(y * y, axis=-1, keepdims=True) + EPS)
        n = y * rstd
        err = (x_ref[...] + n * w) - t_ref[...]
        sq_ref[...] += jnp.sum(err * err, axis=0, keepdims=True)
        dout = err * inv_d
        dout_ref[...] = dout
        dw_ref[...] += jnp.sum(dout * n, axis=0, keepdims=True)
        dn = dout * w
        dy_ref[...] = (rstd * (dn - n * jnp.mean(dn * n, axis=-1, keepdims=True))).astype(_MXU_DTYPE)

    row = pl.BlockSpec((tm, D_MODEL), lambda i: (i, 0))
    return pl.pallas_call(
        body, name="out_fused", grid=(t // tm,),
        in_specs=[pl.BlockSpec((2, tm, D_MODEL), lambda i: (0, i, 0)), _full((2 * D_MODEL, D_MODEL)),
                  row, row, _full((1, D_MODEL))],
        out_specs=[row, row, _full((1, D_MODEL)), _full((1, D_MODEL))],
        out_shape=[jax.ShapeDtypeStruct((t, D_MODEL), F32), jax.ShapeDtypeStruct((t, D_MODEL), _MXU_DTYPE),
                   jax.ShapeDtypeStruct((1, D_MODEL), F32), jax.ShapeDtypeStruct((1, D_MODEL), F32)],
        compiler_params=_seq_params(),
    )(y2, wo, x, tgt, wpost)


def _dw_out(y2, dy, tt):
    t = dy.shape[0]
    hr = UNIT // 2

    def body(y_ref, dy_ref, g_ref):
        @pl.when(pl.program_id(1) == 0)
        def _():
            g_ref[...] = jnp.zeros_like(g_ref)

        r = _dot_tn(y_ref[...], dy_ref[...])
        g_ref[0] += r[0:hr]
        g_ref[1] += r[hr:UNIT]

    return pl.pallas_call(
        body, name="dw_out", grid=(N_CHIPS, t // tt),
        in_specs=[pl.BlockSpec((None, tt, UNIT), lambda c, k: (c // 2, k, c % 2)),
                  pl.BlockSpec((tt, D_MODEL), lambda c, k: (k, 0))],
        out_specs=pl.BlockSpec((2, None, hr, D_MODEL), lambda c, k: (0, c, 0, 0)),
        out_shape=jax.ShapeDtypeStruct((2, N_CHIPS, hr, D_MODEL), F32),
        compiler_params=pltpu.CompilerParams(dimension_semantics=("arbitrary", "arbitrary")),
    )(y2, dy)


def _lru_bwd(dy, wo, p, h, wa, wx, ba, bx, lam, cw, cb, tm):
    t = dy.shape[0]
    nt = t // tm
    ng = tm // 8

    def body(dy_ref, wo_ref, lx_ref, halo_ref, lg_ref, h_ref, hhalo_ref, wa_ref, wx_ref, ba_ref, bx_ref,
             lam_ref, cw_ref, cb_ref,
             dp_ref, dwa_ref, dwx_ref, dba_ref, dbx_ref, dsp_ref, dcw_ref, dcb_ref,
             c_s, in_s, dh_s, dhc_s, afirst_s, dxc_s):
        i = pl.program_id(0)
        first = i == nt - 1

        @pl.when(i == 0)
        def _():
            for ref in (dwa_ref, dwx_ref, dba_ref, dbx_ref, dsp_ref, dcw_ref, dcb_ref, dhc_s, afirst_s, dxc_s):
                ref[...] = jnp.zeros_like(ref)

        sp = _softplus_neg(lam_ref[...])
        cwv = cw_ref[...]
        wav = wa_ref[...]
        wxv = wx_ref[...]
        halo = jnp.where(first, 0.0, halo_ref[...])
        xs, xc, r, ig, a, mult = _lru_gates(lx_ref[...], halo, cwv, cb_ref[...], wav, wxv,
                                            ba_ref[...], bx_ref[...], sp)
        d_y = _dot_nt(dy_ref[...], wo_ref[...])
        lg = lg_ref[...]
        sl = _sigmoid(lg)
        hv = h_ref[...]
        d_lg = d_y * hv * (sl * (1.0 + lg * (1.0 - sl)))

        c_s[...] = _shift_up(a, afirst_s[...], 1)
        in_s[...] = d_y * (lg * sl)
        row8 = _rows((8, D_MODEL))

        def group(gg, carry):
            rows = pl.ds(pl.multiple_of((ng - 1 - gg) * 8, 8), 8)
            cv = c_s[rows, :]
            uv = in_s[rows, :]
            for s in (1, 2, 4):
                c_sh = jnp.where(row8 < 8 - s, pltpu.roll(cv, 8 - s, 0), 1.0)
                u_sh = jnp.where(row8 < 8 - s, pltpu.roll(uv, 8 - s, 0), 0.0)
                uv = cv * u_sh + uv
                cv = cv * c_sh
            hh = cv * carry + uv
            dh_s[rows, :] = hh
            return jnp.broadcast_to(hh[0:1, :], (8, D_MODEL))

        dhc_s[...] = lax.fori_loop(0, ng, group, dhc_s[...])
        afirst_s[...] = a[0:8]
        dh = dh_s[...]

        h_prev = _shift_down(hv, jnp.where(first, 0.0, hhalo_ref[...]), 1)
        gx = ig * xc
        d_mult = dh * gx
        d_i = dh * (mult * xc)
        d_xc = dh * (mult * ig)
        d_la = (dh * h_prev) * a - d_mult * (a * a) / mult
        dsp_ref[...] += jnp.sum(d_la * r, axis=0, keepdims=True) * (-LRU_C)
        d_zr = (d_la * (-LRU_C * sp)) * (r * (1.0 - r))
        d_zi = d_i * (ig * (1.0 - ig))
        dba_ref[...] += jnp.sum(d_zr, axis=0, keepdims=True)
        dbx_ref[...] += jnp.sum(d_zi, axis=0, keepdims=True)
        back = []
        for n in range(LRU_BLOCKS):
            ln = slice(n * LRU_BW, (n + 1) * LRU_BW)
            xb = _mx(xc[:, ln])
            zr_n = _mx(d_zr[:, ln])
            zi_n = _mx(d_zi[:, ln])
            dwa_ref[n] += _dot_tn(xb, zr_n)
            dwx_ref[n] += _dot_tn(xb, zi_n)
            back.append(_dot_nt(zr_n, wav[n]) + _dot_nt(zi_n, wxv[n]))
        d_xc = d_xc + jnp.concatenate(back, axis=1)
        dcb_ref[...] += jnp.sum(d_xc, axis=0, keepdims=True)
        for k in range(4):
            dcw_ref[k:k + 1, :] += jnp.sum(d_xc * xs[k], axis=0, keepdims=True)
        head = dxc_s[...]
        d_lx = cwv[3:4] * d_xc
        for k in range(3):
            d_lx = d_lx + cwv[k:k + 1] * _shift_up(d_xc, head, 3 - k)
        dxc_s[...] = d_xc[0:8]
        dp_ref[0] = d_lx.astype(_MXU_DTYPE)
        dp_ref[1] = d_lg.astype(_MXU_DTYPE)

    rev = lambda i: nt - 1 - i
    seg = lambda s: pl.BlockSpec((None, tm, D_MODEL), lambda i: (s, rev(i), 0))
    halo_row = lambda i: jnp.maximum(rev(i) * (tm // 8) - 1, 0)
    row = pl.BlockSpec((tm, D_MODEL), lambda i: (rev(i), 0))
    vec = _full((1, D_MODEL))
    wblk = _full((LRU_BLOCKS, LRU_BW, LRU_BW))
    return pl.pallas_call(
        body, name="lru_bwd", grid=(nt,),
        in_specs=[row, pl.BlockSpec((D_MODEL, D_MODEL), lambda i: (0, 0)),
                  seg(0), pl.BlockSpec((None, 8, D_MODEL), lambda i: (0, halo_row(i), 0)), seg(1),
                  row, pl.BlockSpec((8, D_MODEL), lambda i: (halo_row(i), 0)),
                  wblk, wblk, vec, vec, vec, _full((4, D_MODEL)), vec],
        out_specs=[pl.BlockSpec((2, tm, D_MODEL), lambda i: (2, rev(i), 0)),
                   wblk, wblk, vec, vec, vec, _full((4, D_MODEL)), vec],
        out_shape=[jax.ShapeDtypeStruct((N_SEG, t, D_MODEL), _MXU_DTYPE),
                   jax.ShapeDtypeStruct((LRU_BLOCKS, LRU_BW, LRU_BW), F32),
                   jax.ShapeDtypeStruct((LRU_BLOCKS, LRU_BW, LRU_BW), F32),
                   jax.ShapeDtypeStruct((1, D_MODEL), F32), jax.ShapeDtypeStruct((1, D_MODEL), F32),
                   jax.ShapeDtypeStruct((1, D_MODEL), F32), jax.ShapeDtypeStruct((4, D_MODEL), F32),
                   jax.ShapeDtypeStruct((1, D_MODEL), F32)],
        scratch_shapes=[pltpu.VMEM((tm, D_MODEL), F32), pltpu.VMEM((tm, D_MODEL), F32),
                        pltpu.VMEM((tm, D_MODEL), F32), pltpu.VMEM((8, D_MODEL), F32),
                        pltpu.VMEM((8, D_MODEL), F32), pltpu.VMEM((8, D_MODEL), F32)],
        compiler_params=_seq_params(),
    )(dy, wo, p, p, p, h, h, wa, wx, ba, bx, lam, cw, cb)


def _hgrn_bwd(dy, wo, p, o, states, logits, gw, dp, tm):
    t = dy.shape[0]
    nt = t // tm
    nc = tm // CHUNK

    def body(dy_ref, wo_ref, qf_ref, vh_ref, o_ref, st_ref, lg_ref, gw_ref, dp_in,
             dp_ref, dgw_ref, dlb_ref, dyh_s, dstate, snext):
        del dp_in
        i = pl.program_id(0)

        @pl.when(i == 0)
        def _():
            for ref in (dgw_ref, dlb_ref, dstate, snext):
                ref[...] = jnp.zeros_like(ref)

        dyh_s[...] = _dot_nt(dy_ref[...], wo_ref[...])
        l0 = lg_ref[0:1, :]
        l1 = lg_ref[1:2, :]
        lb = 1.0 / (1.0 + jnp.exp(l1 - l0))
        gwv = gw_ref[...]
        causal = _rows((CHUNK, CHUNK)) >= _cols((CHUNK, CHUNK))
        wide = 2 * HEAD_D

        for c in reversed(range(nc)):
            rows = pl.ds(c * CHUNK, CHUNK)
            q = qf_ref[0, rows, :]
            hg = vh_ref[1, rows, :]
            ov = o_ref[rows, :]
            dyh = dyh_s[rows, :]
            sig, f, sq, em1, em2, qm, km, ebm, ecm, ebc = _hgrn_prep(q, qf_ref[1, rows, :], lb)
            ke = _mx(km.astype(F32) * ecm)
            vb = _mx(vh_ref[0, rows, :])
            sg = _sigmoid(hg)
            d_on = dyh * (hg * sg)
            x_p, y_p, dv_p, n_p, ex_p = [], [], [], [], []
            for h in range(HEADS):
                ln = slice(h * HEAD_D, (h + 1) * HEAD_D)
                o_h = ov[:, ln]
                rstd = lax.rsqrt(jnp.mean(o_h * o_h, axis=-1, keepdims=True) + EPS)
                n = o_h * rstd
                dn = d_on[:, ln] * gwv[:, ln]
                d_o = rstd * (dn - n * jnp.mean(dn * n, axis=-1, keepdims=True))
                d_ob = _mx(d_o)
                d_ot = _mx(d_o.T)
                s0 = st_ref[c, h]
                ds = dstate[h]
                att = jnp.where(causal, _dot_nt(qm[:, ln], km[:, ln]), 0.0)
                d_att = jnp.where(causal, _dot_nt(d_ob, vb[:, ln]), 0.0)
                x_p.append(jnp.dot(_pad_lanes([d_ob, _mx(d_att)], wide),
                                   _pad_rows([_mx(s0 * ebm[:, ln]), km[:, ln]], wide), preferred_element_type=F32))
                y_p.append(jnp.dot(_pad_lanes([vb[:, ln], _mx(d_att.T)], wide),
                                   _pad_rows([_mx(ds * ecm[:, ln]), qm[:, ln]], wide), preferred_element_type=F32))
                dv_p.append(_dot_nt(_pad_lanes([ke[:, ln], _mx(att.T)], wide), _pad_lanes([_mx(ds), d_ot], wide)))
                ex_p.append(jnp.sum(ds * snext[h], axis=0, keepdims=True))
                dstate[h] = ds * ebc[:, ln] + jnp.dot(d_ot, qm[:, ln], preferred_element_type=F32) * ebm[:, ln]
                snext[h] = s0
                n_p.append(n)
            cat = lambda parts: jnp.concatenate(parts, axis=1)
            x_all, y_all, n_all = cat(x_p), cat(y_p), cat(n_p)
            dg = _cumsum_rows(qm.astype(F32) * x_all - km.astype(F32) * y_all, reverse=True) + cat(ex_p)
            df = dg / f - em2 * y_all
            dgw_ref[...] += jnp.sum(d_on * n_all, axis=0, keepdims=True)
            dlb_ref[...] += jnp.sum(df * (1.0 - sig), axis=0, keepdims=True)
            dp_ref[0, rows, :] = ((em1 * x_all) * (sq * (1.0 + q * (1.0 - sq)))).astype(_MXU_DTYPE)
            dp_ref[1, rows, :] = (df * ((1.0 - lb) * sig * (1.0 - sig))).astype(_MXU_DTYPE)
            dp_ref[2, rows, :] = cat(dv_p).astype(_MXU_DTYPE)
            dp_ref[3, rows, :] = (dyh * (n_all * gwv) * (sg * (1.0 + hg * (1.0 - sg)))).astype(_MXU_DTYPE)

    rev = lambda i: nt - 1 - i
    pair = lambda s: pl.BlockSpec((2, tm, D_MODEL), lambda i: (s, rev(i), 0))
    row = pl.BlockSpec((tm, D_MODEL), lambda i: (rev(i), 0))
    vec = _full((1, D_MODEL))
    return pl.pallas_call(
        body, name="hgrn_bwd", grid=(nt,),
        in_specs=[row, pl.BlockSpec((D_MODEL, D_MODEL), lambda i: (1, 0)), pair(1), pair(2), row,
                  pl.BlockSpec((nc, HEADS, HEAD_D, HEAD_D), lambda i: (rev(i), 0, 0, 0)),
                  _full((2, D_MODEL)), vec, pl.BlockSpec(memory_space=pl.ANY)],
        out_specs=[pl.BlockSpec((4, tm, D_MODEL), lambda i: (0, rev(i), 0)), vec, vec],
        out_shape=[jax.ShapeDtypeStruct((N_SEG, t, D_MODEL), _MXU_DTYPE),
                   jax.ShapeDtypeStruct((1, D_MODEL), F32), jax.ShapeDtypeStruct((1, D_MODEL), F32)],
        scratch_shapes=[pltpu.VMEM((tm, D_MODEL), F32), pltpu.VMEM((HEADS, HEAD_D, HEAD_D), F32),
                        pltpu.VMEM((HEADS, HEAD_D, HEAD_D), F32)],
        input_output_aliases={8: 0},
        compiler_params=_seq_params(),
    )(dy, wo, p, p, o, states, logits, gw, dp)


def _dp_unit(m):
    return ((m // 2 + 2) % N_SEG) * 2 + m % 2


def _dw_in(ut, dp, tt):
    t = ut.shape[1]
    hr = D_MODEL // 2

    def body(ut_ref, dp_ref, g_ref):
        @pl.when(pl.program_id(1) == 0)
        def _():
            g_ref[...] = jnp.zeros_like(g_ref)

        r = jnp.dot(ut_ref[...], dp_ref[...], preferred_element_type=F32)
        for h in range(2):
            for half in range(2):
                g_ref[h, half] += r[h * hr:(h + 1) * hr, half * UNIT:(half + 1) * UNIT]

    return pl.pallas_call(
        body, name="dw_in", grid=(N_SEG, t // tt),
        in_specs=[pl.BlockSpec((D_MODEL, tt), lambda s, k: (0, k)),
                  pl.BlockSpec((None, tt, D_MODEL), lambda s, k: (s, k, 0))],
        out_specs=pl.BlockSpec((2, 2, hr, UNIT), lambda s, k: (0, (s + 2) % N_SEG, 0, 0)),
        out_shape=jax.ShapeDtypeStruct((2, N_UNIT, hr, UNIT), F32),
        compiler_params=pltpu.CompilerParams(dimension_semantics=("arbitrary", "arbitrary")),
    )(ut, dp)


def _dx(dp, wg, x, dout, wpre, tm, parts, scatter):
    t = x.shape[0]
    n = len(parts)
    nt = t // tm

    def body(*refs):
        dp_ref, w_ref, x_ref, dout_ref, wpre_ref = refs[:5]
        ins = refs[5:5 + n]
        gx_ref, dw_ref = refs[5 + n:7 + n]
        outs = refs[7 + n:7 + 2 * n]
        send_sems, recv_sems = refs[7 + 2 * n:]
        i = pl.program_id(0)

        @pl.when(i == 0)
        def _():
            dw_ref[...] = jnp.zeros_like(dw_ref)
            for cp in _chip_copies(ins, outs, send_sems, recv_sems, scatter):
                cp.start()

        du = None
        for m in range(N_UNIT):
            part = _dot_nt(dp_ref[m // 2, :, (m % 2) * UNIT:(m % 2 + 1) * UNIT], w_ref[_dp_unit(m)])
            du = part if du is None else du + part
        xv = x_ref[...]
        rstd = lax.rsqrt(jnp.mean(xv * xv, axis=-1, keepdims=True) + EPS)
        xn = xv * rstd
        dw_ref[...] += jnp.sum(du * xn, axis=0, keepdims=True)
        dn = du * wpre_ref[...]
        gx_ref[...] = dout_ref[...] + rstd * (dn - xn * jnp.mean(dn * xn, axis=-1, keepdims=True))

        @pl.when(i == nt - 1)
        def _():
            copies = _chip_copies(ins, outs, send_sems, recv_sems, scatter)
            for cp in copies:
                cp.wait_recv()
            for cp in copies:
                cp.wait_send()

    row = pl.BlockSpec((tm, D_MODEL), lambda i: (i, 0))
    any_spec = pl.BlockSpec(memory_space=pl.ANY)
    res = pl.pallas_call(
        body, name="dx", grid=(nt,),
        in_specs=[pl.BlockSpec((N_SEG, tm, D_MODEL), lambda i: (0, i, 0)), _resident((N_UNIT, D_MODEL, UNIT)),
                  row, row, _full((1, D_MODEL))] + [any_spec] * n,
        out_specs=[row, _full((1, D_MODEL))] + [any_spec] * n,
        out_shape=[jax.ShapeDtypeStruct((t, D_MODEL), F32), jax.ShapeDtypeStruct((1, D_MODEL), F32)]
        + _chip_exchange_shapes(parts, scatter),
        scratch_shapes=[pltpu.SemaphoreType.DMA((max(3 * n, 1),)), pltpu.SemaphoreType.DMA((max(3 * n, 1),))],
        compiler_params=_seq_params(),
    )(dp, wg, x, dout, wpre, *parts)
    return res[0], res[1], res[2:]


CHUNK_BYTES = 256 * 1024
MAX_CHUNKS = 16


def _n_chunks(shape, dtype, rows):
    nbytes = jnp.dtype(dtype).itemsize
    for d in shape:
        nbytes *= d
    k = max(1, min(MAX_CHUNKS, rows, nbytes // CHUNK_BYTES))
    while rows % k:
        k -= 1
    return k


def _gather_weights(shards, split):
    n = len(shards)
    ks = [_n_chunks(a.shape[1:], a.dtype, a.shape[1]) for a in shards]
    offs = [sum(ks[:a]) for a in range(n)]
    total = sum(ks)

    def body(*refs):
        ins, outs = refs[:n], refs[n:2 * n]
        send_sems, recv_sems, fsend_sems, frecv_sems, osend_sems, orecv_sems = refs[2 * n:]
        x, y, c = lax.axis_index("x"), lax.axis_index("y"), lax.axis_index("c")
        me = 2 * x + y
        peers = [(1 - x, y), (x, 1 - y), (1 - x, 1 - y)]
        half = lambda a: c if split[a] else 0

        def rows(a, k):
            step = shards[a].shape[1] // ks[a]
            return pl.ds(k * step, step)

        def fetch(a, j, k, slot):
            px, py = peers[j]
            sem = 3 * (offs[a] + k) + j
            return pltpu.make_async_remote_copy(
                src_ref=ins[a].at[half(a), rows(a, k)], dst_ref=outs[a].at[slot, half(a), rows(a, k)],
                send_sem=send_sems.at[sem], recv_sem=recv_sems.at[sem],
                device_id=(px, py, c), device_id_type=MESH)

        def forward(a, j, k, h):
            px, py = peers[j]
            sem = 3 * (offs[a] + k) + j
            piece = outs[a].at[2 * px + py, h, rows(a, k)]
            return pltpu.make_async_remote_copy(
                src_ref=piece, dst_ref=piece, send_sem=fsend_sems.at[sem], recv_sem=frecv_sems.at[sem],
                device_id=(x, y, 1 - c), device_id_type=MESH)

        def own(a, h, k):
            sem = 2 * (offs[a] + k) + h
            return pltpu.make_async_remote_copy(
                src_ref=ins[a].at[h, rows(a, k)], dst_ref=outs[a].at[me, h, rows(a, k)],
                send_sem=osend_sems.at[sem], recv_sem=orecv_sems.at[sem],
                device_id=(x, y, 1 - c), device_id_type=MESH)

        pieces = [(a, k) for a in range(n) for k in range(ks[a])]
        halves = lambda a: range(2 if split[a] else 1)
        for a, k in pieces:
            for j in range(3):
                fetch(a, j, k, me).start()
        for a, k in pieces:
            for h in halves(a):
                own(a, h, k).start()
        for a, k in pieces:
            for j in range(3):
                px, py = peers[j]
                fetch(a, j, k, 2 * px + py).wait_recv()
                if split[a]:
                    forward(a, j, k, c).start()
        for a, k in pieces:
            for j in range(3):
                if split[a]:
                    forward(a, j, k, 1 - c).wait_recv()
            for h in halves(a):
                own(a, h, k).wait_recv()
        for a, k in pieces:
            for j in range(3):
                fetch(a, j, k, me).wait_send()
                if split[a]:
                    forward(a, j, k, c).wait_send()
            for h in halves(a):
                own(a, h, k).wait_send()

    any_spec = pl.BlockSpec(memory_space=pl.ANY)
    return pl.pallas_call(
        body, name="gather_weights",
        in_specs=[any_spec] * n, out_specs=[any_spec] * n,
        out_shape=[jax.ShapeDtypeStruct((N_CHIPS,) + a.shape, a.dtype) for a in shards],
        scratch_shapes=[pltpu.SemaphoreType.DMA((3 * total,)), pltpu.SemaphoreType.DMA((3 * total,)),
                        pltpu.SemaphoreType.DMA((3 * total,)), pltpu.SemaphoreType.DMA((3 * total,)),
                        pltpu.SemaphoreType.DMA((2 * total,)), pltpu.SemaphoreType.DMA((2 * total,))],
    )(*shards)


def _chip_exchange_shapes(arrays, scatter):
    return [jax.ShapeDtypeStruct((3,) + tuple(a.shape[1:] if sc else a.shape), a.dtype)
            for a, sc in zip(arrays, scatter)]


def _chip_copies(ins, outs, send_sems, recv_sems, scatter):
    if not ins:
        return []
    x, y, c = lax.axis_index("x"), lax.axis_index("y"), lax.axis_index("c")
    peers = [(1 - x, y), (x, 1 - y), (1 - x, 1 - y)]
    copies = []
    for a in range(len(ins)):
        for j, (px, py) in enumerate(peers):
            copies.append(pltpu.make_async_remote_copy(
                src_ref=ins[a].at[2 * px + py] if scatter[a] else ins[a], dst_ref=outs[a].at[j],
                send_sem=send_sems.at[a * 3 + j], recv_sem=recv_sems.at[a * 3 + j],
                device_id=(px, py, c), device_id_type=MESH))
    return copies


def _core_swap(arrays, name, halved):
    n = len(arrays)
    shapes = [a.shape[1:] if halved else a.shape for a in arrays]
    ks = [_n_chunks(s, a.dtype, s[0]) for s, a in zip(shapes, arrays)]
    offs = [sum(ks[:a]) for a in range(n)]

    def body(*refs):
        ins, outs = refs[:n], refs[n:2 * n]
        send_sems, recv_sems = refs[2 * n:]
        x, y, c = lax.axis_index("x"), lax.axis_index("y"), lax.axis_index("c")

        def copy(a, j):
            step = shapes[a][0] // ks[a]
            rows = pl.ds(j * step, step)
            return pltpu.make_async_remote_copy(
                src_ref=ins[a].at[1 - c, rows] if halved else ins[a].at[rows], dst_ref=outs[a].at[rows],
                send_sem=send_sems.at[offs[a] + j], recv_sem=recv_sems.at[offs[a] + j],
                device_id=(x, y, 1 - c), device_id_type=MESH)

        pieces = [(a, j) for a in range(n) for j in range(ks[a])]
        for a, j in pieces:
            copy(a, j).start()
        for a, j in pieces:
            copy(a, j).wait_recv()
        for a, j in pieces:
            copy(a, j).wait_send()

    any_spec = pl.BlockSpec(memory_space=pl.ANY)
    return pl.pallas_call(
        body, name=name,
        in_specs=[any_spec] * n, out_specs=[any_spec] * n,
        out_shape=[jax.ShapeDtypeStruct(s, a.dtype) for s, a in zip(shapes, arrays)],
        scratch_shapes=[pltpu.SemaphoreType.DMA((sum(ks),)), pltpu.SemaphoreType.DMA((sum(ks),))],
    )(*arrays)


def _finish_exchange(arrays, small):
    n = len(arrays)
    ks = [_n_chunks(a.shape, a.dtype, a.shape[0]) for a in arrays]
    offs = [sum(ks[:a]) for a in range(n)]
    total = sum(ks)

    def body(*refs):
        ins, small_ref = refs[:n], refs[n]
        outs, gathered = refs[n + 1:2 * n + 1], refs[2 * n + 1]
        send_sems, recv_sems = refs[2 * n + 2:]
        x, y, c = lax.axis_index("x"), lax.axis_index("y"), lax.axis_index("c")
        copies = []
        for a in range(n):
            step = arrays[a].shape[0] // ks[a]
            for j in range(ks[a]):
                rows = pl.ds(j * step, step)
                copies.append(pltpu.make_async_remote_copy(
                    src_ref=ins[a].at[rows], dst_ref=outs[a].at[rows],
                    send_sem=send_sems.at[offs[a] + j], recv_sem=recv_sems.at[offs[a] + j],
                    device_id=(x, y, 1 - c), device_id_type=MESH))
        for slot in range(7):
            dx, dy, dc = (slot + 1) % 2, ((slot + 1) // 2) % 2, (slot + 1) // 4
            copies.append(pltpu.make_async_remote_copy(
                src_ref=small_ref, dst_ref=gathered.at[slot],
                send_sem=send_sems.at[total + slot], recv_sem=recv_sems.at[total + slot],
                device_id=(x + dx - 2 * x * dx, y + dy - 2 * y * dy, c + dc - 2 * c * dc), device_id_type=MESH))
        for cp in copies:
            cp.start()
        for cp in copies:
            cp.wait_recv()
        for cp in copies:
            cp.wait_send()

    any_spec = pl.BlockSpec(memory_space=pl.ANY)
    res = pl.pallas_call(
        body, name="reduce_cores",
        in_specs=[any_spec] * (n + 1), out_specs=[any_spec] * (n + 1),
        out_shape=[jax.ShapeDtypeStruct(a.shape, a.dtype) for a in arrays]
        + [jax.ShapeDtypeStruct((7,) + small.shape, small.dtype)],
        scratch_shapes=[pltpu.SemaphoreType.DMA((total + 7,)), pltpu.SemaphoreType.DMA((total + 7,))],
    )(*arrays, small)
    return res[:n], res[n]


def _sum_own_half(g, b, name, out_dtype, tr=512):
    _, rows, cols = g.shape
    tr = min(tr, rows)
    core = lax.axis_index("c").reshape(1)

    def body(c_ref, g_ref, b_ref, o_ref):
        del c_ref
        o_ref[...] = (g_ref[...] + b_ref[...]).astype(out_dtype)

    return pl.pallas_call(
        body, name=name,
        grid_spec=pltpu.PrefetchScalarGridSpec(
            num_scalar_prefetch=1, grid=(rows // tr,),
            in_specs=[pl.BlockSpec((None, tr, cols), lambda i, c_ref: (c_ref[0], i, 0)),
                      pl.BlockSpec((tr, cols), lambda i, c_ref: (i, 0))],
            out_specs=pl.BlockSpec((tr, cols), lambda i, c_ref: (i, 0))),
        out_shape=jax.ShapeDtypeStruct((rows, cols), out_dtype),
        compiler_params=pltpu.CompilerParams(dimension_semantics=("parallel",)),
    )(core, g, b)


def _sum_chips(own, r, slabbed, name, tr=512):
    _, rows, cols = r.shape
    tr = min(tr, rows)
    chip = (2 * lax.axis_index("x") + lax.axis_index("y")).reshape(1)

    def body(c_ref, own_ref, r_ref, o_ref):
        del c_ref
        f = lambda val: val.astype(F32)
        o_ref[...] = (f(own_ref[...]) + f(r_ref[0])) + (f(r_ref[1]) + f(r_ref[2]))

    if slabbed:
        own_spec = pl.BlockSpec((None, tr, cols), lambda i, c_ref: (c_ref[0], i, 0))
    else:
        own_spec = pl.BlockSpec((tr, cols), lambda i, c_ref: (i, 0))
    return pl.pallas_call(
        body, name=name,
        grid_spec=pltpu.PrefetchScalarGridSpec(
            num_scalar_prefetch=1, grid=(rows // tr,),
            in_specs=[own_spec, pl.BlockSpec((3, tr, cols), lambda i, c_ref: (0, i, 0))],
            out_specs=pl.BlockSpec((tr, cols), lambda i, c_ref: (i, 0))),
        out_shape=jax.ShapeDtypeStruct((rows, cols), F32),
        compiler_params=pltpu.CompilerParams(dimension_semantics=("parallel",)),
    )(chip, own, r)


def _adamw_big(own, other, w, m, v, name, grid, g_block, g_index, w_block, w_index):
    core = lax.axis_index("c").reshape(1)
    half_axis = len(grid) - 1

    def body(c_ref, own_ref, oth_ref, w_ref, m_ref, v_ref, g_ref, d_ref, nm_ref, nv_ref):
        mine = pl.program_id(half_axis) == c_ref[0]
        gv = jnp.where(mine, own_ref[...], oth_ref[...])
        g_ref[...] = gv
        d_ref[...], nm_ref[...], nv_ref[...] = _adamw(w_ref[...], gv, m_ref[...], v_ref[...])

    g_spec = pl.BlockSpec(g_block, lambda *a: g_index(*a[:half_axis]))
    w_spec = pl.BlockSpec(w_block, lambda *a: w_index(*a[:-1]))
    shp = jax.ShapeDtypeStruct(w.shape, F32)
    return pl.pallas_call(
        body, name=name,
        grid_spec=pltpu.PrefetchScalarGridSpec(
            num_scalar_prefetch=1, grid=grid,
            in_specs=[g_spec, g_spec, w_spec, w_spec, w_spec], out_specs=[w_spec] * 4),
        out_shape=[shp] * 4,
        compiler_params=pltpu.CompilerParams(dimension_semantics=("parallel",) * len(grid)),
    )(core, own, other, w, m, v)


def _small_finish(own, others, lam, logits):
    def body(a_ref, b_ref, lam_ref, lg_ref, o_ref):
        g = (((a_ref[...] + b_ref[0]) + (b_ref[1] + b_ref[2]))
             + ((b_ref[3] + b_ref[4]) + (b_ref[5] + b_ref[6])))
        o_ref[...] = g
        o_ref[8:9, :] = g[8:9, :] * (-_sigmoid(-lam_ref[...]))
        lb = 1.0 / (1.0 + jnp.exp(lg_ref[1:2, :] - lg_ref[0:1, :]))
        d0 = g[9:10, :] * (lb * (1.0 - lb))
        o_ref[9:10, :] = d0
        o_ref[10:11, :] = -d0

    return pl.pallas_call(
        body, name="small_finish",
        in_specs=[_full((SMALL_ROWS, D_MODEL)), _full((7, SMALL_ROWS, D_MODEL)), _full((1, D_MODEL)),
                  _full((2, D_MODEL))],
        out_specs=_full((SMALL_ROWS, D_MODEL)),
        out_shape=jax.ShapeDtypeStruct((SMALL_ROWS, D_MODEL), F32),
    )(own, others, lam, logits)


def _adamw_small(ws, gs, ms, vs):
    n = len(ws)

    def body(*refs):
        w_r, g_r, m_r, v_r = refs[:n], refs[n:2 * n], refs[2 * n:3 * n], refs[3 * n:4 * n]
        d_o, m_o, v_o = refs[4 * n:5 * n], refs[5 * n:6 * n], refs[6 * n:7 * n]
        for j in range(n):
            d_o[j][...], m_o[j][...], v_o[j][...] = _adamw(w_r[j][...], g_r[j][...], m_r[j][...], v_r[j][...])

    specs = [_full(w.shape) for w in ws]
    shapes = [jax.ShapeDtypeStruct(w.shape, F32) for w in ws]
    outs = pl.pallas_call(
        body, name="adamw_small",
        in_specs=specs * 4, out_specs=specs * 3, out_shape=shapes * 3,
    )(*ws, *gs, *ms, *vs)
    return outs[:n], outs[n:2 * n], outs[2 * n:]


def _local_step(x, tgt, pre_w, wg, cw, cb, wa, wx, ba, bx, lam, logits, gw, wo, post_w, tm, tmm, prepare=None):
    ut, p = _in_proj(x, pre_w, wg, tm)
    h, y2 = _lru_fwd(p, wa, wx, ba, bx, lam, cw, cb, tm)
    y2, o, states = _hgrn_fwd(p, logits, gw, y2, tm)
    dout, dy, sq, d_post = _out_fused(y2, wo, x, tgt, post_w, tmm)
    g_out = _dw_out(y2, dy, tmm)
    dp, d_wa, d_wx, d_ba, d_bx, d_sp, d_cw, d_cb = _lru_bwd(dy, wo, p, h, wa, wx, ba, bx, lam, cw, cb, tm)
    dp, d_gw, d_lb = _hgrn_bwd(dy, wo, p, o, states, logits, gw, dp, tm)
    g_in = _dw_in(ut, dp, min(2048, x.shape[0]))
    parts, scatter = prepare(g_in, g_out, d_wa, d_wx) if prepare is not None else ([], [])
    grad_x, d_pre, received = _dx(dp, wg, x, dout, pre_w, tm, parts, scatter)
    small = jnp.concatenate([d_pre, d_cw, d_cb, d_ba, d_bx, d_sp, d_lb,
                             jnp.zeros((1, D_MODEL), F32), d_gw, d_post, sq,
                             jnp.zeros((SMALL_ROWS - 14, D_MODEL), F32)], axis=0)
    return grad_x, g_in, g_out, d_wa, d_wx, small, parts, received


def kernel(x, pre_norm_w, w_in, conv_w, conv_b, lru_w_a, lru_b_a, lru_w_x, lru_b_x, lru_lambda, hgrn_lb_logits, hgrn_gnorm_w, w_out, post_norm_w, loss_target, m_pre_norm_w, m_w_in, m_conv_w, m_conv_b, m_lru_w_a, m_lru_b_a, m_lru_w_x, m_lru_b_x, m_lru_lambda, m_hgrn_lb_logits, m_hgrn_gnorm_w, m_w_out, m_post_norm_w, v_pre_norm_w, v_w_in, v_conv_w, v_conv_b, v_lru_w_a, v_lru_b_a, v_lru_w_x, v_lru_b_x, v_lru_lambda, v_hgrn_lb_logits, v_hgrn_gnorm_w, v_w_out, v_post_norm_w):
    t = x.shape[1]
    tm = min(256, t)
    tmm = min(512, t)
    chip = 2 * lax.axis_index("x") + lax.axis_index("y")

    win_units = w_in[0].astype(_MXU_DTYPE).reshape(D_MODEL, 3, UNIT).transpose(1, 0, 2)
    small_w = jnp.concatenate([conv_w[0], lru_b_a[0], lru_b_x[0]], axis=1)
    wg4, wo4, wa4, wx4, sw4 = _gather_weights(
        [win_units.reshape(2, 3 * 512, UNIT), w_out[0].astype(_MXU_DTYPE).reshape(2, 256, D_MODEL),
         lru_w_a[0].astype(_MXU_DTYPE).reshape(2, 128, LRU_BW), lru_w_x[0].astype(_MXU_DTYPE).reshape(2, 128, LRU_BW),
         small_w.reshape(1, 4, 384)],
        [True, True, True, True, False])
    wg = wg4.reshape(N_UNIT, D_MODEL, UNIT)
    wo = wo4.reshape(2 * D_MODEL, D_MODEL)
    by_block = lambda w4: w4.reshape(N_CHIPS, LRU_BLOCKS, 64, LRU_BW).transpose(1, 0, 2, 3).reshape(
        LRU_BLOCKS, LRU_BW, LRU_BW)
    wa, wx = by_block(wa4), by_block(wx4)
    sw4 = sw4.reshape(N_CHIPS, 4, 384)
    cw = sw4[:, :, 0:256].transpose(1, 0, 2).reshape(4, D_MODEL)
    ba = sw4[:, :, 256:320].transpose(1, 0, 2).reshape(1, D_MODEL)
    bx = sw4[:, :, 320:384].transpose(1, 0, 2).reshape(1, D_MODEL)

    def lru_layout(g):
        g = g.reshape(2, 2, N_CHIPS, 64, LRU_BW).transpose(0, 2, 1, 3, 4)
        return g.reshape(2, N_CHIPS * 128, LRU_BW)

    def prepare(g_in, g_out, d_wa, d_wx):
        g_in = g_in.reshape(2, N_UNIT * 512, UNIT)
        g_out = g_out.reshape(2, N_CHIPS * 256, D_MODEL)
        g_wa, g_wx = lru_layout(d_wa), lru_layout(d_wx)
        b_in, b_out, b_wa, b_wx = _core_swap([g_in, g_out, g_wa, g_wx], "reduce_swap", halved=True)
        parts = [_sum_own_half(g_in, b_in, "presum_in", _WIRE_DTYPE).reshape(N_CHIPS, 3 * 512, UNIT),
                 _sum_own_half(g_out, b_out, "presum_out", _WIRE_DTYPE).reshape(N_CHIPS, 256, D_MODEL),
                 _sum_own_half(g_wa, b_wa, "presum_wa", _WIRE_DTYPE).reshape(N_CHIPS, 128, LRU_BW),
                 _sum_own_half(g_wx, b_wx, "presum_wx", _WIRE_DTYPE).reshape(N_CHIPS, 128, LRU_BW)]
        return parts, [True] * 4

    grad_x, _, _, _, _, small, (p_in, p_out, p_wa, p_wx), (r_in, r_out, r_wa, r_wx) = _local_step(
        x[0], loss_target[0], pre_norm_w, wg, cw, conv_b, wa, wx, ba, bx, lru_lambda, hgrn_lb_logits,
        hgrn_gnorm_w, wo, post_norm_w, tm, tmm, prepare)
    s_in = _sum_chips(p_in, r_in, True, "sum_in")
    s_out = _sum_chips(p_out, r_out, True, "sum_out")
    s_wa = _sum_chips(p_wa, r_wa, True, "sum_wa")
    s_wx = _sum_chips(p_wx, r_wx, True, "sum_wx")
    (o_in, o_out, o_wa, o_wx), small_others = _finish_exchange([s_in, s_out, s_wa, s_wx], small)

    g_w_in, d_w_in, nm_w_in, nv_w_in = _adamw_big(
        s_in.reshape(3, 512, UNIT), o_in.reshape(3, 512, UNIT), w_in[0], m_w_in[0], v_w_in[0], "adamw_w_in", (3, 2),
        (None, 512, UNIT), lambda k: (k, 0, 0), (512, UNIT), lambda k, h: (h, k))
    g_w_out, d_w_out, nm_w_out, nv_w_out = _adamw_big(
        s_out, o_out, w_out[0], m_w_out[0], v_w_out[0], "adamw_w_out", (2,),
        (256, D_MODEL), lambda: (0, 0), (256, D_MODEL), lambda h: (h, 0))
    sq2 = lambda a: a.reshape(LRU_BW, LRU_BW)
    lru_specs = ((2,), (128, LRU_BW), lambda: (0, 0), (128, LRU_BW), lambda h: (h, 0))
    g_wa, d_wa2, nm_wa, nv_wa = _adamw_big(s_wa, o_wa, sq2(lru_w_a), sq2(m_lru_w_a), sq2(v_lru_w_a), "adamw_wa",
                                           *lru_specs)
    g_wx, d_wx2, nm_wx, nv_wx = _adamw_big(s_wx, o_wx, sq2(lru_w_x), sq2(m_lru_w_x), sq2(v_lru_w_x), "adamw_wx",
                                           *lru_specs)

    gs = _small_finish(small, small_others, lru_lambda, hgrn_lb_logits)
    loss = jnp.sum(gs[13]) * (0.5 / D_MODEL)
    g_pre = gs[0:1]
    g_cw = lax.dynamic_slice(gs[1:5], (0, chip * 256), (4, 256))
    g_cb = gs[5:6]
    g_ba = lax.dynamic_slice(gs[6].reshape(LRU_BLOCKS, N_CHIPS, 64), (0, chip, 0), (LRU_BLOCKS, 1, 64)).reshape(4, 64)
    g_bx = lax.dynamic_slice(gs[7].reshape(LRU_BLOCKS, N_CHIPS, 64), (0, chip, 0), (LRU_BLOCKS, 1, 64)).reshape(4, 64)
    g_lam = gs[8:9]
    g_lb = gs[9:11]
    g_gw = gs[11:12]
    g_post = gs[12:13]
    small_g = [g_pre, g_cw, g_cb, g_ba, g_bx, g_lam, g_lb, g_gw, g_post]
    two_d = lambda a: a.reshape(a.shape[-2:])
    small_w_list = [pre_norm_w, conv_w, conv_b, lru_b_a, lru_b_x, lru_lambda, hgrn_lb_logits, hgrn_gnorm_w, post_norm_w]
    small_m_list = [m_pre_norm_w, m_conv_w, m_conv_b, m_lru_b_a, m_lru_b_x, m_lru_lambda, m_hgrn_lb_logits,
                    m_hgrn_gnorm_w, m_post_norm_w]
    small_v_list = [v_pre_norm_w, v_conv_w, v_conv_b, v_lru_b_a, v_lru_b_x, v_lru_lambda, v_hgrn_lb_logits,
                    v_hgrn_gnorm_w, v_post_norm_w]
    sd, sm, sv = _adamw_small([two_d(a) for a in small_w_list], small_g,
                              [two_d(a) for a in small_m_list], [two_d(a) for a in small_v_list])

    def shaped(vals, refs):
        return [val.reshape(ref.shape) for val, ref in zip(vals, refs)]

    s_g = shaped(small_g, small_w_list)
    s_d = shaped(sd, small_w_list)
    s_m = shaped(sm, small_w_list)
    s_v = shaped(sv, small_w_list)

    def ordered(small, big_in, big_wa, big_wx, big_out):
        pre, cw_, cb_, ba_, bx_, lam_, lb_, gw_, post_ = small
        return [pre, big_in.reshape(w_in.shape), cw_, cb_, big_wa.reshape(lru_w_a.shape), ba_,
                big_wx.reshape(lru_w_x.shape), bx_, lam_, lb_, gw_, big_out.reshape(w_out.shape), post_]

    grads = ordered(s_g, g_w_in, g_wa, g_wx, g_w_out)
    deltas = ordered(s_d, d_w_in, d_wa2, d_wx2, d_w_out)
    new_m = ordered(s_m, nm_w_in, nm_wa, nm_wx, nm_w_out)
    new_v = ordered(s_v, nv_w_in, nv_wa, nv_wx, nv_w_out)
    return (loss, grad_x.reshape(x.shape), *grads, *deltas, *new_m, *new_v)
```

```python
import functools

import jax
import jax.numpy as jnp
from jax import lax
from jax.experimental import pallas as pl
from jax.experimental.pallas import tpu as pltpu

F32 = jnp.float32
BF16 = jnp.bfloat16
_MXU_DTYPE = jnp.bfloat16
_WIRE_DTYPE = jnp.bfloat16

D_MODEL = 1024
N_SEG = 6
UNIT = 512
N_UNIT = 12
LRU_BLOCKS = 4
LRU_BW = 256
LRU_C = 8.0
HEADS = 8
HEAD_D = 128
CHUNK = 64
EPS = 1e-6
N_CHIPS = 4
EXP_CLAMP = 80.0

ADAM_LR = 0.001
ADAM_B1 = 0.9
ADAM_B2 = 0.999
ADAM_EPS = 1e-08
ADAM_WD = 0.01
ADAM_STEP = 10

SMALL_ROWS = 16
MESH = pl.DeviceIdType.MESH


def _sigmoid(x):
    return 0.5 * jnp.tanh(0.5 * x) + 0.5


def _mx(x):
    return x.astype(_MXU_DTYPE)


def _dot(a, b):
    return jnp.dot(_mx(a), _mx(b), preferred_element_type=F32)


def _dot_nt(a, b):
    return lax.dot_general(_mx(a), _mx(b), (((1,), (1,)), ((), ())), preferred_element_type=F32)


def _dot_tn(a, b):
    return lax.dot_general(_mx(a), _mx(b), (((0,), (0,)), ((), ())), preferred_element_type=F32)


def _rows(shape):
    return lax.broadcasted_iota(jnp.int32, shape, 0)


def _cols(shape):
    return lax.broadcasted_iota(jnp.int32, shape, 1)


def _softplus_neg(lam):
    z = -lam
    e = jnp.exp(-jnp.abs(z))
    series = e * (1.0 - e * (0.5 - e * (1.0 / 3.0 - 0.25 * e)))
    return jnp.maximum(z, 0.0) + jnp.where(e < 1e-2, series, jnp.log(1.0 + e))


def _one_minus_sq(a, y):
    series = -y * (1.0 + y * (0.5 + y * (1.0 / 6.0)))
    return jnp.where(y > -0.01, series, 1.0 - a * a)


def _cumsum_rows(x, reverse=False):
    n = x.shape[0] // 8
    row8 = _rows((8, x.shape[1]))
    out = [None] * n
    carry = None
    for g in (reversed(range(n)) if reverse else range(n)):
        blk = x[8 * g:8 * g + 8]
        for s in (1, 2, 4):
            if reverse:
                blk = blk + jnp.where(row8 < 8 - s, pltpu.roll(blk, 8 - s, 0), 0.0)
            else:
                blk = blk + jnp.where(row8 >= s, pltpu.roll(blk, s, 0), 0.0)
        if carry is not None:
            blk = blk + carry
        carry = blk[0:1] if reverse else blk[7:8]
        out[g] = blk
    return jnp.concatenate(out, axis=0)


def _shift_down(x, halo, s):
    if s == 0:
        return x
    r = pltpu.roll(x, s, 0)
    top = jnp.where(_rows(halo.shape) < s, pltpu.roll(halo, s, 0), r[0:8])
    return jnp.concatenate([top, r[8:]], axis=0)


def _shift_up(x, head, s):
    if s == 0:
        return x
    n = x.shape[0]
    r = pltpu.roll(x, n - s, 0)
    bottom = jnp.where(_rows(head.shape) >= 8 - s, pltpu.roll(head, 8 - s, 0), r[n - 8:n])
    return jnp.concatenate([r[:n - 8], bottom], axis=0)


def _lru_gates(lx, halo, cw, cb, wa, wx, ba, bx, sp):
    xs = [_shift_down(lx, halo, 3 - k) for k in range(4)]
    xc = cb + cw[0:1] * xs[0] + cw[1:2] * xs[1] + cw[2:3] * xs[2] + cw[3:4] * xs[3]
    zr, zi = [], []
    for n in range(LRU_BLOCKS):
        xb = _mx(xc[:, n * LRU_BW:(n + 1) * LRU_BW])
        zr.append(jnp.dot(xb, wa[n], preferred_element_type=F32))
        zi.append(jnp.dot(xb, wx[n], preferred_element_type=F32))
    r = _sigmoid(jnp.concatenate(zr, axis=1) + ba)
    ig = _sigmoid(jnp.concatenate(zi, axis=1) + bx)
    la = (-LRU_C * sp) * r
    a = jnp.exp(la)
    om = _one_minus_sq(a, 2.0 * la)
    return xs, xc, r, ig, a, om


def _hgrn_prep(q, fr, lb):
    sig = _sigmoid(fr)
    f = lb + (1.0 - lb) * sig
    sq = _sigmoid(q)
    b = _cumsum_rows(jnp.log(f))
    bm = b[CHUNK // 2 - 1:CHUNK // 2]
    bc = b[CHUNK - 1:CHUNK]
    em1 = jnp.exp(jnp.minimum(b - bm, EXP_CLAMP))
    em2 = jnp.exp(jnp.minimum(bm - b, EXP_CLAMP))
    qm = _mx((q * sq) * em1)
    km = _mx((1.0 - f) * em2)
    return sig, f, sq, em1, em2, qm, km, jnp.exp(bm), jnp.exp(bc - bm), jnp.exp(bc)


def _pad_lanes(parts, width):
    have = sum(p.shape[1] for p in parts)
    pad = [jnp.zeros((parts[0].shape[0], width - have), parts[0].dtype)] if width > have else []
    return jnp.concatenate(list(parts) + pad, axis=1)


def _pad_rows(parts, height):
    have = sum(p.shape[0] for p in parts)
    pad = [jnp.zeros((height - have, parts[0].shape[1]), parts[0].dtype)] if height > have else []
    return jnp.concatenate(list(parts) + pad, axis=0)


def _adamw(w, g, m, v):
    m = ADAM_B1 * m + (1.0 - ADAM_B1) * g
    v = ADAM_B2 * v + (1.0 - ADAM_B2) * jnp.square(g)
    m_hat = m / (1.0 - ADAM_B1 ** ADAM_STEP)
    v_hat = v / (1.0 - ADAM_B2 ** ADAM_STEP)
    delta = -ADAM_LR * (m_hat / (jnp.sqrt(v_hat) + ADAM_EPS) + ADAM_WD * w)
    return delta, m, v


def _full(shape):
    nd = len(shape)
    return pl.BlockSpec(shape, lambda *_: (0,) * nd)


def _seq_params():
    return pltpu.CompilerParams(dimension_semantics=("arbitrary",))


def _resident(shape):
    nd = len(shape)
    return pl.BlockSpec(shape, lambda *_: (0,) * nd, pipeline_mode=pl.Buffered(1))


def _in_proj(x, w, wg, tm):
    t = x.shape[0]

    def body(x_ref, w_ref, wg_ref, ut_ref, p_ref):
        xv = x_ref[...]
        rstd = lax.rsqrt(jnp.mean(xv * xv, axis=-1, keepdims=True) + EPS)
        u = xv * rstd * w_ref[...]
        ut_ref[...] = u.T.astype(_MXU_DTYPE)
        ub = _mx(u)
        for m in range(N_UNIT):
            p_ref[m // 2, :, (m % 2) * UNIT:(m % 2 + 1) * UNIT] = jnp.dot(
                ub, wg_ref[m], preferred_element_type=F32)

    return pl.pallas_call(
        body, name="in_proj", grid=(t // tm,),
        in_specs=[pl.BlockSpec((tm, D_MODEL), lambda i: (i, 0)), _full((1, D_MODEL)),
                  _resident((N_UNIT, D_MODEL, UNIT))],
        out_specs=[pl.BlockSpec((D_MODEL, tm), lambda i: (0, i)),
                   pl.BlockSpec((N_SEG, tm, D_MODEL), lambda i: (0, i, 0))],
        out_shape=[jax.ShapeDtypeStruct((D_MODEL, t), _MXU_DTYPE),
                   jax.ShapeDtypeStruct((N_SEG, t, D_MODEL), F32)],
        compiler_params=pltpu.CompilerParams(dimension_semantics=("parallel",)),
    )(x, w, wg)


def _lru_fwd(p, wa, wx, ba, bx, lam, cw, cb, tm):
    t = p.shape[1]
    ng = tm // 8

    def body(lx_ref, halo_ref, lg_ref, wa_ref, wx_ref, ba_ref, bx_ref, lam_ref, cw_ref, cb_ref,
             h_ref, y_ref, a_s, u_s, hc_s):
        i = pl.program_id(0)

        @pl.when(i == 0)
        def _():
            hc_s[...] = jnp.zeros_like(hc_s)

        sp = _softplus_neg(lam_ref[...])
        halo = jnp.where(i == 0, 0.0, halo_ref[...])
        _, xc, _, ig, a, om = _lru_gates(lx_ref[...], halo, cw_ref[...], cb_ref[...], wa_ref[...],
                                         wx_ref[...], ba_ref[...], bx_ref[...], sp)
        a_s[...] = a
        u_s[...] = jnp.sqrt(om) * (ig * xc)
        row8 = _rows((8, D_MODEL))

        def group(g, hc):
            rows = pl.ds(pl.multiple_of(g * 8, 8), 8)
            av = a_s[rows, :]
            uv = u_s[rows, :]
            for s in (1, 2, 4):
                a_sh = jnp.where(row8 >= s, pltpu.roll(av, s, 0), 1.0)
                u_sh = jnp.where(row8 >= s, pltpu.roll(uv, s, 0), 0.0)
                uv = av * u_sh + uv
                av = av * a_sh
            hh = av * hc + uv
            h_ref[rows, :] = hh
            return jnp.broadcast_to(hh[7:8, :], (8, D_MODEL))

        hc_s[...] = lax.fori_loop(0, ng, group, hc_s[...])
        lg = lg_ref[...]
        y_ref[...] = (h_ref[...] * (lg * _sigmoid(lg))).astype(_MXU_DTYPE)

    seg = lambda s: pl.BlockSpec((None, tm, D_MODEL), lambda i: (s, i, 0))
    return pl.pallas_call(
        body, name="lru_fwd", grid=(t // tm,),
        in_specs=[seg(0),
                  pl.BlockSpec((None, 8, D_MODEL), lambda i: (0, jnp.maximum(i * (tm // 8) - 1, 0), 0)),
                  seg(1),
                  _full((LRU_BLOCKS, LRU_BW, LRU_BW)), _full((LRU_BLOCKS, LRU_BW, LRU_BW)),
                  _full((1, D_MODEL)), _full((1, D_MODEL)), _full((1, D_MODEL)),
                  _full((4, D_MODEL)), _full((1, D_MODEL))],
        out_specs=[pl.BlockSpec((tm, D_MODEL), lambda i: (i, 0)),
                   pl.BlockSpec((None, tm, D_MODEL), lambda i: (0, i, 0))],
        out_shape=[jax.ShapeDtypeStruct((t, D_MODEL), F32),
                   jax.ShapeDtypeStruct((2, t, D_MODEL), _MXU_DTYPE)],
        scratch_shapes=[pltpu.VMEM((tm, D_MODEL), F32), pltpu.VMEM((tm, D_MODEL), F32),
                        pltpu.VMEM((8, D_MODEL), F32)],
        compiler_params=_seq_params(),
    )(p, p, p, wa, wx, ba, bx, lam, cw, cb)


def _hgrn_fwd(p, logits, gw, y2, tm):
    t = p.shape[1]
    nc = tm // CHUNK

    def body(qf_ref, vh_ref, lg_ref, gw_ref, y_in, y_ref, o_ref, st_ref, state):
        del y_in
        i = pl.program_id(0)

        @pl.when(i == 0)
        def _():
            state[...] = jnp.zeros_like(state)

        l0 = lg_ref[0:1, :]
        l1 = lg_ref[1:2, :]
        lb = 1.0 / (1.0 + jnp.exp(l1 - l0))
        gwv = gw_ref[...]
        causal = _rows((CHUNK, CHUNK)) >= _cols((CHUNK, CHUNK))

        for c in range(nc):
            rows = pl.ds(c * CHUNK, CHUNK)
            v = vh_ref[0, rows, :]
            hg = vh_ref[1, rows, :]
            _, _, _, _, _, qm, km, ebm, ecm, ebc = _hgrn_prep(qf_ref[0, rows, :], qf_ref[1, rows, :], lb)
            ke = _mx(km.astype(F32) * ecm)
            gate = gwv * (hg * _sigmoid(hg))
            o_parts, y_parts = [], []
            for h in range(HEADS):
                ln = slice(h * HEAD_D, (h + 1) * HEAD_D)
                s0 = state[h]
                st_ref[c, h] = s0
                vt = _mx(v[:, ln].T)
                att = _mx(jnp.where(causal, _dot_nt(qm[:, ln], km[:, ln]), 0.0))
                o = _dot_nt(_pad_lanes([qm[:, ln], att], 2 * HEAD_D),
                            _pad_lanes([_mx(s0 * ebm[:, ln]), vt], 2 * HEAD_D))
                state[h] = s0 * ebc[:, ln] + jnp.dot(vt, ke[:, ln], preferred_element_type=F32)
                rstd = lax.rsqrt(jnp.mean(o * o, axis=-1, keepdims=True) + EPS)
                o_parts.append(o)
                y_parts.append((o * rstd * gate[:, ln]).astype(_MXU_DTYPE))
            o_ref[rows, :] = jnp.concatenate(o_parts, axis=1)
            y_ref[rows, :] = jnp.concatenate(y_parts, axis=1)

    pair = lambda s: pl.BlockSpec((2, tm, D_MODEL), lambda i: (s, i, 0))
    return pl.pallas_call(
        body, name="hgrn_fwd", grid=(t // tm,),
        in_specs=[pair(1), pair(2), _full((2, D_MODEL)), _full((1, D_MODEL)),
                  pl.BlockSpec(memory_space=pl.ANY)],
        out_specs=[pl.BlockSpec((None, tm, D_MODEL), lambda i: (1, i, 0)),
                   pl.BlockSpec((tm, D_MODEL), lambda i: (i, 0)),
                   pl.BlockSpec((nc, HEADS, HEAD_D, HEAD_D), lambda i: (i, 0, 0, 0))],
        out_shape=[jax.ShapeDtypeStruct((2, t, D_MODEL), _MXU_DTYPE),
                   jax.ShapeDtypeStruct((t, D_MODEL), F32),
                   jax.ShapeDtypeStruct((t // CHUNK, HEADS, HEAD_D, HEAD_D), F32)],
        scratch_shapes=[pltpu.VMEM((HEADS, HEAD_D, HEAD_D), F32)],
        input_output_aliases={4: 0},
        compiler_params=_seq_params(),
    )(p, p, logits, gw, y2)


def _out_fused(y2, wo, x, tgt, wpost, tm):
    t = x.shape[0]
    inv_d = 1.0 / D_MODEL

    def body(y_ref, wo_ref, x_ref, t_ref, w_ref, dout_ref, dy_ref, sq_ref, dw_ref):
        i = pl.program_id(0)

        @pl.when(i == 0)
        def _():
            sq_ref[...] = jnp.zeros_like(sq_ref)
            dw_ref[...] = jnp.zeros_like(dw_ref)

        y = (jnp.dot(y_ref[0], wo_ref[0:D_MODEL, :], preferred_element_type=F32)
             + jnp.dot(y_ref[1], wo_ref[D_MODEL:2 * D_MODEL, :], preferred_element_type=F32))
        w = w_ref[...]
        rstd = lax.rsqrt(jnp.mean(y * y, axis=-1, keepdims=True) + EPS)
        n = y * rstd
        err = (x_ref[...] + n * w) - t_ref[...]
        sq_ref[...] += jnp.sum(err * err, axis=0, keepdims=True)
        dout = err * inv_d
        dout_ref[...] = dout
        dw_ref[...] += jnp.sum(dout * n, axis=0, keepdims=True)
        dn = dout * w
        dy_ref[...] = (rstd * (dn - n * jnp.mean(dn * n, axis=-1, keepdims=True))).astype(_MXU_DTYPE)

    row = pl.BlockSpec((tm, D_MODEL), lambda i: (i, 0))
    return pl.pallas_call(
        body, name="out_fused", grid=(t // tm,),
        in_specs=[pl.BlockSpec((2, tm, D_MODEL), lambda i: (0, i, 0)), _full((2 * D_MODEL, D_MODEL)),
                  row, row, _full((1, D_MODEL))],
        out_specs=[row, row, _full((1, D_MODEL)), _full((1, D_MODEL))],
        out_shape=[jax.ShapeDtypeStruct((t, D_MODEL), F32), jax.ShapeDtypeStruct((t, D_MODEL), _MXU_DTYPE),
                   jax.ShapeDtypeStruct((1, D_MODEL), F32), jax.ShapeDtypeStruct((1, D_MODEL), F32)],
        compiler_params=_seq_params(),
    )(y2, wo, x, tgt, wpost)


def _dw_out(y2, dy, tt):
    t = dy.shape[0]
    hr = UNIT // 2

    def body(y_ref, dy_ref, g_ref):
        @pl.when(pl.program_id(1) == 0)
        def _():
            g_ref[...] = jnp.zeros_like(g_ref)

        r = _dot_tn(y_ref[...], dy_ref[...])
        g_ref[0] += r[0:hr]
        g_ref[1] += r[hr:UNIT]

    return pl.pallas_call(
        body, name="dw_out", grid=(N_CHIPS, t // tt),
        in_specs=[pl.BlockSpec((None, tt, UNIT), lambda c, k: (c // 2, k, c % 2)),
                  pl.BlockSpec((tt, D_MODEL), lambda c, k: (k, 0))],
        out_specs=pl.BlockSpec((2, None, hr, D_MODEL), lambda c, k: (0, c, 0, 0)),
        out_shape=jax.ShapeDtypeStruct((2, N_CHIPS, hr, D_MODEL), F32),
        compiler_params=pltpu.CompilerParams(dimension_semantics=("arbitrary", "arbitrary")),
    )(y2, dy)


def _lru_bwd(dy, wo, p, h, wa, wx, ba, bx, lam, cw, cb, tm):
    t = dy.shape[0]
    nt = t // tm
    ng = tm // 8

    def body(dy_ref, wo_ref, lx_ref, halo_ref, lg_ref, h_ref, hhalo_ref, wa_ref, wx_ref, ba_ref, bx_ref,
             lam_ref, cw_ref, cb_ref,
             dp_ref, dwa_ref, dwx_ref, dba_ref, dbx_ref, dsp_ref, dcw_ref, dcb_ref,
             c_s, in_s, dh_s, dhc_s, afirst_s, dxc_s):
        i = pl.program_id(0)
        first = i == nt - 1

        @pl.when(i == 0)
        def _():
            for ref in (dwa_ref, dwx_ref, dba_ref, dbx_ref, dsp_ref, dcw_ref, dcb_ref, dhc_s, afirst_s, dxc_s):
                ref[...] = jnp.zeros_like(ref)

        sp = _softplus_neg(lam_ref[...])
        cwv = cw_ref[...]
        wav = wa_ref[...]
        wxv = wx_ref[...]
        halo = jnp.where(first, 0.0, halo_ref[...])
        xs, xc, r, ig, a, om = _lru_gates(lx_ref[...], halo, cwv, cb_ref[...], wav, wxv,
                                          ba_ref[...], bx_ref[...], sp)
        inv_mult = lax.rsqrt(om)
        mult = om * inv_mult
        d_y = _dot_nt(dy_ref[...], wo_ref[...])
        lg = lg_ref[...]
        sl = _sigmoid(lg)
        hv = h_ref[...]
        d_lg = d_y * hv * (sl * (1.0 + lg * (1.0 - sl)))

        c_s[...] = _shift_up(a, afirst_s[...], 1)
        in_s[...] = d_y * (lg * sl)
        row8 = _rows((8, D_MODEL))

        def group(gg, carry):
            rows = pl.ds(pl.multiple_of((ng - 1 - gg) * 8, 8), 8)
            cv = c_s[rows, :]
            uv = in_s[rows, :]
            for s in (1, 2, 4):
                c_sh = jnp.where(row8 < 8 - s, pltpu.roll(cv, 8 - s, 0), 1.0)
                u_sh = jnp.where(row8 < 8 - s, pltpu.roll(uv, 8 - s, 0), 0.0)
                uv = cv * u_sh + uv
                cv = cv * c_sh
            hh = cv * carry + uv
            dh_s[rows, :] = hh
            return jnp.broadcast_to(hh[0:1, :], (8, D_MODEL))

        dhc_s[...] = lax.fori_loop(0, ng, group, dhc_s[...])
        afirst_s[...] = a[0:8]
        dh = dh_s[...]

        h_prev = _shift_down(hv, jnp.where(first, 0.0, hhalo_ref[...]), 1)
        gx = ig * xc
        d_mult = dh * gx
        d_i = dh * (mult * xc)
        d_xc = dh * (mult * ig)
        d_la = (dh * h_prev) * a - d_mult * ((a * a) * inv_mult)
        dsp_ref[...] += jnp.sum(d_la * r, axis=0, keepdims=True) * (-LRU_C)
        d_zr = (d_la * (-LRU_C * sp)) * (r * (1.0 - r))
        d_zi = d_i * (ig * (1.0 - ig))
        dba_ref[...] += jnp.sum(d_zr, axis=0, keepdims=True)
        dbx_ref[...] += jnp.sum(d_zi, axis=0, keepdims=True)
        back = []
        for n in range(LRU_BLOCKS):
            ln = slice(n * LRU_BW, (n + 1) * LRU_BW)
            xb = _mx(xc[:, ln])
            zr_n = _mx(d_zr[:, ln])
            zi_n = _mx(d_zi[:, ln])
            dwa_ref[n] += _dot_tn(xb, zr_n)
            dwx_ref[n] += _dot_tn(xb, zi_n)
            back.append(_dot_nt(zr_n, wav[n]) + _dot_nt(zi_n, wxv[n]))
        d_xc = d_xc + jnp.concatenate(back, axis=1)
        dcb_ref[...] += jnp.sum(d_xc, axis=0, keepdims=True)
        for k in range(4):
            dcw_ref[k:k + 1, :] += jnp.sum(d_xc * xs[k], axis=0, keepdims=True)
        head = dxc_s[...]
        d_lx = cwv[3:4] * d_xc
        for k in range(3):
            d_lx = d_lx + cwv[k:k + 1] * _shift_up(d_xc, head, 3 - k)
        dxc_s[...] = d_xc[0:8]
        dp_ref[0] = d_lx.astype(_MXU_DTYPE)
        dp_ref[1] = d_lg.astype(_MXU_DTYPE)

    rev = lambda i: nt - 1 - i
    seg = lambda s: pl.BlockSpec((None, tm, D_MODEL), lambda i: (s, rev(i), 0))
    halo_row = lambda i: jnp.maximum(rev(i) * (tm // 8) - 1, 0)
    row = pl.BlockSpec((tm, D_MODEL), lambda i: (rev(i), 0))
    vec = _full((1, D_MODEL))
    wblk = _full((LRU_BLOCKS, LRU_BW, LRU_BW))
    return pl.pallas_call(
        body, name="lru_bwd", grid=(nt,),
        in_specs=[row, pl.BlockSpec((D_MODEL, D_MODEL), lambda i: (0, 0)),
                  seg(0), pl.BlockSpec((None, 8, D_MODEL), lambda i: (0, halo_row(i), 0)), seg(1),
                  row, pl.BlockSpec((8, D_MODEL), lambda i: (halo_row(i), 0)),
                  wblk, wblk, vec, vec, vec, _full((4, D_MODEL)), vec],
        out_specs=[pl.BlockSpec((2, tm, D_MODEL), lambda i: (2, rev(i), 0)),
                   wblk, wblk, vec, vec, vec, _full((4, D_MODEL)), vec],
        out_shape=[jax.ShapeDtypeStruct((N_SEG, t, D_MODEL), _MXU_DTYPE),
                   jax.ShapeDtypeStruct((LRU_BLOCKS, LRU_BW, LRU_BW), F32),
                   jax.ShapeDtypeStruct((LRU_BLOCKS, LRU_BW, LRU_BW), F32),
                   jax.ShapeDtypeStruct((1, D_MODEL), F32), jax.ShapeDtypeStruct((1, D_MODEL), F32),
                   jax.ShapeDtypeStruct((1, D_MODEL), F32), jax.ShapeDtypeStruct((4, D_MODEL), F32),
                   jax.ShapeDtypeStruct((1, D_MODEL), F32)],
        scratch_shapes=[pltpu.VMEM((tm, D_MODEL), F32), pltpu.VMEM((tm, D_MODEL), F32),
                        pltpu.VMEM((tm, D_MODEL), F32), pltpu.VMEM((8, D_MODEL), F32),
                        pltpu.VMEM((8, D_MODEL), F32), pltpu.VMEM((8, D_MODEL), F32)],
        compiler_params=_seq_params(),
    )(dy, wo, p, p, p, h, h, wa, wx, ba, bx, lam, cw, cb)


def _hgrn_bwd(dy, wo, p, o, states, logits, gw, dp, tm):
    t = dy.shape[0]
    nt = t // tm
    nc = tm // CHUNK

    def body(dy_ref, wo_ref, qf_ref, vh_ref, o_ref, st_ref, lg_ref, gw_ref, dp_in,
             dp_ref, dgw_ref, dlb_ref, dyh_s, dstate, snext):
        del dp_in
        i = pl.program_id(0)

        @pl.when(i == 0)
        def _():
            for ref in (dgw_ref, dlb_ref, dstate, snext):
                ref[...] = jnp.zeros_like(ref)

        dyh_s[...] = _dot_nt(dy_ref[...], wo_ref[...])
        l0 = lg_ref[0:1, :]
        l1 = lg_ref[1:2, :]
        lb = 1.0 / (1.0 + jnp.exp(l1 - l0))
        gwv = gw_ref[...]
        causal = _rows((CHUNK, CHUNK)) >= _cols((CHUNK, CHUNK))
        wide = 2 * HEAD_D

        for c in reversed(range(nc)):
            rows = pl.ds(c * CHUNK, CHUNK)
            q = qf_ref[0, rows, :]
            hg = vh_ref[1, rows, :]
            ov = o_ref[rows, :]
            dyh = dyh_s[rows, :]
            sig, f, sq, em1, em2, qm, km, ebm, ecm, ebc = _hgrn_prep(q, qf_ref[1, rows, :], lb)
            ke = _mx(km.astype(F32) * ecm)
            vb = _mx(vh_ref[0, rows, :])
            sg = _sigmoid(hg)
            d_on = dyh * (hg * sg)
            x_p, y_p, dv_p, n_p, ex_p = [], [], [], [], []
            for h in range(HEADS):
                ln = slice(h * HEAD_D, (h + 1) * HEAD_D)
                o_h = ov[:, ln]
                rstd = lax.rsqrt(jnp.mean(o_h * o_h, axis=-1, keepdims=True) + EPS)
                n = o_h * rstd
                dn = d_on[:, ln] * gwv[:, ln]
                d_o = rstd * (dn - n * jnp.mean(dn * n, axis=-1, keepdims=True))
                d_ob = _mx(d_o)
                d_ot = _mx(d_o.T)
                s0 = st_ref[c, h]
                ds = dstate[h]
                att = jnp.where(causal, _dot_nt(qm[:, ln], km[:, ln]), 0.0)
                d_att = jnp.where(causal, _dot_nt(d_ob, vb[:, ln]), 0.0)
                x_p.append(jnp.dot(_pad_lanes([d_ob, _mx(d_att)], wide),
                                   _pad_rows([_mx(s0 * ebm[:, ln]), km[:, ln]], wide), preferred_element_type=F32))
                y_p.append(jnp.dot(_pad_lanes([vb[:, ln], _mx(d_att.T)], wide),
                                   _pad_rows([_mx(ds * ecm[:, ln]), qm[:, ln]], wide), preferred_element_type=F32))
                dv_p.append(_dot_nt(_pad_lanes([ke[:, ln], _mx(att.T)], wide), _pad_lanes([_mx(ds), d_ot], wide)))
                ex_p.append(jnp.sum(ds * snext[h], axis=0, keepdims=True))
                dstate[h] = ds * ebc[:, ln] + jnp.dot(d_ot, qm[:, ln], preferred_element_type=F32) * ebm[:, ln]
                snext[h] = s0
                n_p.append(n)
            cat = lambda parts: jnp.concatenate(parts, axis=1)
            x_all, y_all, n_all = cat(x_p), cat(y_p), cat(n_p)
            dg = _cumsum_rows(qm.astype(F32) * x_all - km.astype(F32) * y_all, reverse=True) + cat(ex_p)
            df = dg / f - em2 * y_all
            dgw_ref[...] += jnp.sum(d_on * n_all, axis=0, keepdims=True)
            dlb_ref[...] += jnp.sum(df * (1.0 - sig), axis=0, keepdims=True)
            dp_ref[0, rows, :] = ((em1 * x_all) * (sq * (1.0 + q * (1.0 - sq)))).astype(_MXU_DTYPE)
            dp_ref[1, rows, :] = (df * ((1.0 - lb) * sig * (1.0 - sig))).astype(_MXU_DTYPE)
            dp_ref[2, rows, :] = cat(dv_p).astype(_MXU_DTYPE)
            dp_ref[3, rows, :] = (dyh * (n_all * gwv) * (sg * (1.0 + hg * (1.0 - sg)))).astype(_MXU_DTYPE)

    rev = lambda i: nt - 1 - i
    pair = lambda s: pl.BlockSpec((2, tm, D_MODEL), lambda i: (s, rev(i), 0))
    row = pl.BlockSpec((tm, D_MODEL), lambda i: (rev(i), 0))
    vec = _full((1, D_MODEL))
    return pl.pallas_call(
        body, name="hgrn_bwd", grid=(nt,),
        in_specs=[row, pl.BlockSpec((D_MODEL, D_MODEL), lambda i: (1, 0)), pair(1), pair(2), row,
                  pl.BlockSpec((nc, HEADS, HEAD_D, HEAD_D), lambda i: (rev(i), 0, 0, 0)),
                  _full((2, D_MODEL)), vec, pl.BlockSpec(memory_space=pl.ANY)],
        out_specs=[pl.BlockSpec((4, tm, D_MODEL), lambda i: (0, rev(i), 0)), vec, vec],
        out_shape=[jax.ShapeDtypeStruct((N_SEG, t, D_MODEL), _MXU_DTYPE),
                   jax.ShapeDtypeStruct((1, D_MODEL), F32), jax.ShapeDtypeStruct((1, D_MODEL), F32)],
        scratch_shapes=[pltpu.VMEM((tm, D_MODEL), F32), pltpu.VMEM((HEADS, HEAD_D, HEAD_D), F32),
                        pltpu.VMEM((HEADS, HEAD_D, HEAD_D), F32)],
        input_output_aliases={8: 0},
        compiler_params=_seq_params(),
    )(dy, wo, p, p, o, states, logits, gw, dp)


def _dp_unit(m):
    return ((m // 2 + 2) % N_SEG) * 2 + m % 2


def _dw_in(ut, dp, tt):
    t = ut.shape[1]
    hr = D_MODEL // 2

    def body(ut_ref, dp_ref, g_ref):
        @pl.when(pl.program_id(1) == 0)
        def _():
            g_ref[...] = jnp.zeros_like(g_ref)

        r = jnp.dot(ut_ref[...], dp_ref[...], preferred_element_type=F32)
        for h in range(2):
            for half in range(2):
                g_ref[h, half] += r[h * hr:(h + 1) * hr, half * UNIT:(half + 1) * UNIT]

    return pl.pallas_call(
        body, name="dw_in", grid=(N_SEG, t // tt),
        in_specs=[pl.BlockSpec((D_MODEL, tt), lambda s, k: (0, k)),
                  pl.BlockSpec((None, tt, D_MODEL), lambda s, k: (s, k, 0))],
        out_specs=pl.BlockSpec((2, 2, hr, UNIT), lambda s, k: (0, (s + 2) % N_SEG, 0, 0)),
        out_shape=jax.ShapeDtypeStruct((2, N_UNIT, hr, UNIT), F32),
        compiler_params=pltpu.CompilerParams(dimension_semantics=("arbitrary", "arbitrary")),
    )(ut, dp)


def _dx(dp, wg, x, dout, wpre, tm, parts, scatter):
    t = x.shape[0]
    n = len(parts)
    nt = t // tm

    def body(*refs):
        dp_ref, w_ref, x_ref, dout_ref, wpre_ref = refs[:5]
        ins = refs[5:5 + n]
        gx_ref, dw_ref = refs[5 + n:7 + n]
        outs = refs[7 + n:7 + 2 * n]
        send_sems, recv_sems = refs[7 + 2 * n:]
        i = pl.program_id(0)

        @pl.when(i == 0)
        def _():
            dw_ref[...] = jnp.zeros_like(dw_ref)
            for cp in _chip_copies(ins, outs, send_sems, recv_sems, scatter):
                cp.start()

        du = None
        for m in range(N_UNIT):
            part = _dot_nt(dp_ref[m // 2, :, (m % 2) * UNIT:(m % 2 + 1) * UNIT], w_ref[_dp_unit(m)])
            du = part if du is None else du + part
        xv = x_ref[...]
        rstd = lax.rsqrt(jnp.mean(xv * xv, axis=-1, keepdims=True) + EPS)
        xn = xv * rstd
        dw_ref[...] += jnp.sum(du * xn, axis=0, keepdims=True)
        dn = du * wpre_ref[...]
        gx_ref[...] = dout_ref[...] + rstd * (dn - xn * jnp.mean(dn * xn, axis=-1, keepdims=True))

        @pl.when(i == nt - 1)
        def _():
            copies = _chip_copies(ins, outs, send_sems, recv_sems, scatter)
            for cp in copies:
                cp.wait_recv()
            for cp in copies:
                cp.wait_send()

    row = pl.BlockSpec((tm, D_MODEL), lambda i: (i, 0))
    any_spec = pl.BlockSpec(memory_space=pl.ANY)
    res = pl.pallas_call(
        body, name="dx", grid=(nt,),
        in_specs=[pl.BlockSpec((N_SEG, tm, D_MODEL), lambda i: (0, i, 0)), _resident((N_UNIT, D_MODEL, UNIT)),
                  row, row, _full((1, D_MODEL))] + [any_spec] * n,
        out_specs=[row, _full((1, D_MODEL))] + [any_spec] * n,
        out_shape=[jax.ShapeDtypeStruct((t, D_MODEL), F32), jax.ShapeDtypeStruct((1, D_MODEL), F32)]
        + _chip_exchange_shapes(parts, scatter),
        scratch_shapes=[pltpu.SemaphoreType.DMA((max(3 * n, 1),)), pltpu.SemaphoreType.DMA((max(3 * n, 1),))],
        compiler_params=_seq_params(),
    )(dp, wg, x, dout, wpre, *parts)
    return res[0], res[1], res[2:]


CHUNK_BYTES = 256 * 1024
MAX_CHUNKS = 16


def _n_chunks(shape, dtype, rows):
    nbytes = jnp.dtype(dtype).itemsize
    for d in shape:
        nbytes *= d
    k = max(1, min(MAX_CHUNKS, rows, nbytes // CHUNK_BYTES))
    while rows % k:
        k -= 1
    return k


def _gather_weights(shards, split):
    n = len(shards)
    ks = [_n_chunks(a.shape[1:], a.dtype, a.shape[1]) for a in shards]
    offs = [sum(ks[:a]) for a in range(n)]
    total = sum(ks)

    def body(*refs):
        ins, outs = refs[:n], refs[n:2 * n]
        send_sems, recv_sems, fsend_sems, frecv_sems, osend_sems, orecv_sems = refs[2 * n:]
        x, y, c = lax.axis_index("x"), lax.axis_index("y"), lax.axis_index("c")
        me = 2 * x + y
        peers = [(1 - x, y), (x, 1 - y), (1 - x, 1 - y)]
        half = lambda a: c if split[a] else 0

        def rows(a, k):
            step = shards[a].shape[1] // ks[a]
            return pl.ds(k * step, step)

        def fetch(a, j, k, slot):
            px, py = peers[j]
            sem = 3 * (offs[a] + k) + j
            return pltpu.make_async_remote_copy(
                src_ref=ins[a].at[half(a), rows(a, k)], dst_ref=outs[a].at[slot, half(a), rows(a, k)],
                send_sem=send_sems.at[sem], recv_sem=recv_sems.at[sem],
                device_id=(px, py, c), device_id_type=MESH)

        def forward(a, j, k, h):
            px, py = peers[j]
            sem = 3 * (offs[a] + k) + j
            piece = outs[a].at[2 * px + py, h, rows(a, k)]
            return pltpu.make_async_remote_copy(
                src_ref=piece, dst_ref=piece, send_sem=fsend_sems.at[sem], recv_sem=frecv_sems.at[sem],
                device_id=(x, y, 1 - c), device_id_type=MESH)

        def own(a, h, k):
            sem = 2 * (offs[a] + k) + h
            return pltpu.make_async_remote_copy(
                src_ref=ins[a].at[h, rows(a, k)], dst_ref=outs[a].at[me, h, rows(a, k)],
                send_sem=osend_sems.at[sem], recv_sem=orecv_sems.at[sem],
                device_id=(x, y, 1 - c), device_id_type=MESH)

        pieces = [(a, k) for a in range(n) for k in range(ks[a])]
        halves = lambda a: range(2 if split[a] else 1)
        for a, k in pieces:
            for j in range(3):
                fetch(a, j, k, me).start()
        for a, k in pieces:
            for h in halves(a):
                own(a, h, k).start()
        for a, k in pieces:
            for j in range(3):
                px, py = peers[j]
                fetch(a, j, k, 2 * px + py).wait_recv()
                if split[a]:
                    forward(a, j, k, c).start()
        for a, k in pieces:
            for j in range(3):
                if split[a]:
                    forward(a, j, k, 1 - c).wait_recv()
            for h in halves(a):
                own(a, h, k).wait_recv()
        for a, k in pieces:
            for j in range(3):
                fetch(a, j, k, me).wait_send()
                if split[a]:
                    forward(a, j, k, c).wait_send()
            for h in halves(a):
                own(a, h, k).wait_send()

    any_spec = pl.BlockSpec(memory_space=pl.ANY)
    return pl.pallas_call(
        body, name="gather_weights",
        in_specs=[any_spec] * n, out_specs=[any_spec] * n,
        out_shape=[jax.ShapeDtypeStruct((N_CHIPS,) + a.shape, a.dtype) for a in shards],
        scratch_shapes=[pltpu.SemaphoreType.DMA((3 * total,)), pltpu.SemaphoreType.DMA((3 * total,)),
                        pltpu.SemaphoreType.DMA((3 * total,)), pltpu.SemaphoreType.DMA((3 * total,)),
                        pltpu.SemaphoreType.DMA((2 * total,)), pltpu.SemaphoreType.DMA((2 * total,))],
    )(*shards)


def _chip_exchange_shapes(arrays, scatter):
    return [jax.ShapeDtypeStruct((3,) + tuple(a.shape[1:] if sc else a.shape), a.dtype)
            for a, sc in zip(arrays, scatter)]


def _chip_copies(ins, outs, send_sems, recv_sems, scatter):
    if not ins:
        return []
    x, y, c = lax.axis_index("x"), lax.axis_index("y"), lax.axis_index("c")
    peers = [(1 - x, y), (x, 1 - y), (1 - x, 1 - y)]
    copies = []
    for a in range(len(ins)):
        for j, (px, py) in enumerate(peers):
            copies.append(pltpu.make_async_remote_copy(
                src_ref=ins[a].at[2 * px + py] if scatter[a] else ins[a], dst_ref=outs[a].at[j],
                send_sem=send_sems.at[a * 3 + j], recv_sem=recv_sems.at[a * 3 + j],
                device_id=(px, py, c), device_id_type=MESH))
    return copies


def _core_swap(arrays, name, halved):
    n = len(arrays)
    shapes = [a.shape[1:] if halved else a.shape for a in arrays]
    ks = [_n_chunks(s, a.dtype, s[0]) for s, a in zip(shapes, arrays)]
    offs = [sum(ks[:a]) for a in range(n)]

    def body(*refs):
        ins, outs = refs[:n], refs[n:2 * n]
        send_sems, recv_sems = refs[2 * n:]
        x, y, c = lax.axis_index("x"), lax.axis_index("y"), lax.axis_index("c")

        def copy(a, j):
            step = shapes[a][0] // ks[a]
            rows = pl.ds(j * step, step)
            return pltpu.make_async_remote_copy(
                src_ref=ins[a].at[1 - c, rows] if halved else ins[a].at[rows], dst_ref=outs[a].at[rows],
                send_sem=send_sems.at[offs[a] + j], recv_sem=recv_sems.at[offs[a] + j],
                device_id=(x, y, 1 - c), device_id_type=MESH)

        pieces = [(a, j) for a in range(n) for j in range(ks[a])]
        for a, j in pieces:
            copy(a, j).start()
        for a, j in pieces:
            copy(a, j).wait_recv()
        for a, j in pieces:
            copy(a, j).wait_send()

    any_spec = pl.BlockSpec(memory_space=pl.ANY)
    return pl.pallas_call(
        body, name=name,
        in_specs=[any_spec] * n, out_specs=[any_spec] * n,
        out_shape=[jax.ShapeDtypeStruct(s, a.dtype) for s, a in zip(shapes, arrays)],
        scratch_shapes=[pltpu.SemaphoreType.DMA((sum(ks),)), pltpu.SemaphoreType.DMA((sum(ks),))],
    )(*arrays)


def _finish_exchange(arrays, small):
    n = len(arrays)
    ks = [_n_chunks(a.shape, a.dtype, a.shape[0]) for a in arrays]
    offs = [sum(ks[:a]) for a in range(n)]
    total = sum(ks)

    def body(*refs):
        ins, small_ref = refs[:n], refs[n]
        outs, gathered = refs[n + 1:2 * n + 1], refs[2 * n + 1]
        send_sems, recv_sems = refs[2 * n + 2:]
        x, y, c = lax.axis_index("x"), lax.axis_index("y"), lax.axis_index("c")
        copies = []
        for a in range(n):
            step = arrays[a].shape[0] // ks[a]
            for j in range(ks[a]):
                rows = pl.ds(j * step, step)
                copies.append(pltpu.make_async_remote_copy(
                    src_ref=ins[a].at[rows], dst_ref=outs[a].at[rows],
                    send_sem=send_sems.at[offs[a] + j], recv_sem=recv_sems.at[offs[a] + j],
                    device_id=(x, y, 1 - c), device_id_type=MESH))
        for slot in range(7):
            dx, dy, dc = (slot + 1) % 2, ((slot + 1) // 2) % 2, (slot + 1) // 4
            copies.append(pltpu.make_async_remote_copy(
                src_ref=small_ref, dst_ref=gathered.at[slot],
                send_sem=send_sems.at[total + slot], recv_sem=recv_sems.at[total + slot],
                device_id=(x + dx - 2 * x * dx, y + dy - 2 * y * dy, c + dc - 2 * c * dc), device_id_type=MESH))
        for cp in copies:
            cp.start()
        for cp in copies:
            cp.wait_recv()
        for cp in copies:
            cp.wait_send()

    any_spec = pl.BlockSpec(memory_space=pl.ANY)
    res = pl.pallas_call(
        body, name="reduce_cores",
        in_specs=[any_spec] * (n + 1), out_specs=[any_spec] * (n + 1),
        out_shape=[jax.ShapeDtypeStruct(a.shape, a.dtype) for a in arrays]
        + [jax.ShapeDtypeStruct((7,) + small.shape, small.dtype)],
        scratch_shapes=[pltpu.SemaphoreType.DMA((total + 7,)), pltpu.SemaphoreType.DMA((total + 7,))],
    )(*arrays, small)
    return res[:n], res[n]


def _sum_own_half(g, b, name, out_dtype, tr=512):
    _, rows, cols = g.shape
    tr = min(tr, rows)
    core = lax.axis_index("c").reshape(1)

    def body(c_ref, g_ref, b_ref, o_ref):
        del c_ref
        o_ref[...] = (g_ref[...] + b_ref[...]).astype(out_dtype)

    return pl.pallas_call(
        body, name=name,
        grid_spec=pltpu.PrefetchScalarGridSpec(
            num_scalar_prefetch=1, grid=(rows // tr,),
            in_specs=[pl.BlockSpec((None, tr, cols), lambda i, c_ref: (c_ref[0], i, 0)),
                      pl.BlockSpec((tr, cols), lambda i, c_ref: (i, 0))],
            out_specs=pl.BlockSpec((tr, cols), lambda i, c_ref: (i, 0))),
        out_shape=jax.ShapeDtypeStruct((rows, cols), out_dtype),
        compiler_params=pltpu.CompilerParams(dimension_semantics=("parallel",)),
    )(core, g, b)


def _sum_chips(own, r, slabbed, name, tr=512):
    _, rows, cols = r.shape
    tr = min(tr, rows)
    chip = (2 * lax.axis_index("x") + lax.axis_index("y")).reshape(1)

    def body(c_ref, own_ref, r_ref, o_ref):
        del c_ref
        f = lambda val: val.astype(F32)
        o_ref[...] = (f(own_ref[...]) + f(r_ref[0])) + (f(r_ref[1]) + f(r_ref[2]))

    if slabbed:
        own_spec = pl.BlockSpec((None, tr, cols), lambda i, c_ref: (c_ref[0], i, 0))
    else:
        own_spec = pl.BlockSpec((tr, cols), lambda i, c_ref: (i, 0))
    return pl.pallas_call(
        body, name=name,
        grid_spec=pltpu.PrefetchScalarGridSpec(
            num_scalar_prefetch=1, grid=(rows // tr,),
            in_specs=[own_spec, pl.BlockSpec((3, tr, cols), lambda i, c_ref: (0, i, 0))],
            out_specs=pl.BlockSpec((tr, cols), lambda i, c_ref: (i, 0))),
        out_shape=jax.ShapeDtypeStruct((rows, cols), F32),
        compiler_params=pltpu.CompilerParams(dimension_semantics=("parallel",)),
    )(chip, own, r)


def _adamw_big(own, other, w, m, v, name, grid, g_block, g_index, w_block, w_index):
    core = lax.axis_index("c").reshape(1)
    half_axis = len(grid) - 1

    def body(c_ref, own_ref, oth_ref, w_ref, m_ref, v_ref, g_ref, d_ref, nm_ref, nv_ref):
        mine = pl.program_id(half_axis) == c_ref[0]
        gv = jnp.where(mine, own_ref[...], oth_ref[...])
        g_ref[...] = gv
        d_ref[...], nm_ref[...], nv_ref[...] = _adamw(w_ref[...], gv, m_ref[...], v_ref[...])

    g_spec = pl.BlockSpec(g_block, lambda *a: g_index(*a[:half_axis]))
    w_spec = pl.BlockSpec(w_block, lambda *a: w_index(*a[:-1]))
    shp = jax.ShapeDtypeStruct(w.shape, F32)
    return pl.pallas_call(
        body, name=name,
        grid_spec=pltpu.PrefetchScalarGridSpec(
            num_scalar_prefetch=1, grid=grid,
            in_specs=[g_spec, g_spec, w_spec, w_spec, w_spec], out_specs=[w_spec] * 4),
        out_shape=[shp] * 4,
        compiler_params=pltpu.CompilerParams(dimension_semantics=("parallel",) * len(grid)),
    )(core, own, other, w, m, v)


def _small_finish(own, others, lam, logits):
    def body(a_ref, b_ref, lam_ref, lg_ref, o_ref):
        g = (((a_ref[...] + b_ref[0]) + (b_ref[1] + b_ref[2]))
             + ((b_ref[3] + b_ref[4]) + (b_ref[5] + b_ref[6])))
        o_ref[...] = g
        o_ref[8:9, :] = g[8:9, :] * (-_sigmoid(-lam_ref[...]))
        lb = 1.0 / (1.0 + jnp.exp(lg_ref[1:2, :] - lg_ref[0:1, :]))
        d0 = g[9:10, :] * (lb * (1.0 - lb))
        o_ref[9:10, :] = d0
        o_ref[10:11, :] = -d0

    return pl.pallas_call(
        body, name="small_finish",
        in_specs=[_full((SMALL_ROWS, D_MODEL)), _full((7, SMALL_ROWS, D_MODEL)), _full((1, D_MODEL)),
                  _full((2, D_MODEL))],
        out_specs=_full((SMALL_ROWS, D_MODEL)),
        out_shape=jax.ShapeDtypeStruct((SMALL_ROWS, D_MODEL), F32),
    )(own, others, lam, logits)


def _adamw_small(ws, gs, ms, vs):
    n = len(ws)

    def body(*refs):
        w_r, g_r, m_r, v_r = refs[:n], refs[n:2 * n], refs[2 * n:3 * n], refs[3 * n:4 * n]
        d_o, m_o, v_o = refs[4 * n:5 * n], refs[5 * n:6 * n], refs[6 * n:7 * n]
        for j in range(n):
            d_o[j][...], m_o[j][...], v_o[j][...] = _adamw(w_r[j][...], g_r[j][...], m_r[j][...], v_r[j][...])

    specs = [_full(w.shape) for w in ws]
    shapes = [jax.ShapeDtypeStruct(w.shape, F32) for w in ws]
    outs = pl.pallas_call(
        body, name="adamw_small",
        in_specs=specs * 4, out_specs=specs * 3, out_shape=shapes * 3,
    )(*ws, *gs, *ms, *vs)
    return outs[:n], outs[n:2 * n], outs[2 * n:]


def _local_step(x, tgt, pre_w, wg, cw, cb, wa, wx, ba, bx, lam, logits, gw, wo, post_w, tm, tmm, prepare=None):
    ut, p = _in_proj(x, pre_w, wg, tm)
    h, y2 = _lru_fwd(p, wa, wx, ba, bx, lam, cw, cb, tm)
    y2, o, states = _hgrn_fwd(p, logits, gw, y2, tm)
    dout, dy, sq, d_post = _out_fused(y2, wo, x, tgt, post_w, tmm)
    g_out = _dw_out(y2, dy, min(2048, x.shape[0]))
    dp, d_wa, d_wx, d_ba, d_bx, d_sp, d_cw, d_cb = _lru_bwd(dy, wo, p, h, wa, wx, ba, bx, lam, cw, cb, tm)
    dp, d_gw, d_lb = _hgrn_bwd(dy, wo, p, o, states, logits, gw, dp, tm)
    g_in = _dw_in(ut, dp, min(2048, x.shape[0]))
    parts, scatter = prepare(g_in, g_out, d_wa, d_wx) if prepare is not None else ([], [])
    grad_x, d_pre, received = _dx(dp, wg, x, dout, pre_w, tm, parts, scatter)
    small = jnp.concatenate([d_pre, d_cw, d_cb, d_ba, d_bx, d_sp, d_lb,
                             jnp.zeros((1, D_MODEL), F32), d_gw, d_post, sq,
                             jnp.zeros((SMALL_ROWS - 14, D_MODEL), F32)], axis=0)
    return grad_x, g_in, g_out, d_wa, d_wx, small, parts, received


def kernel(x, pre_norm_w, w_in, conv_w, conv_b, lru_w_a, lru_b_a, lru_w_x, lru_b_x, lru_lambda, hgrn_lb_logits, hgrn_gnorm_w, w_out, post_norm_w, loss_target, m_pre_norm_w, m_w_in, m_conv_w, m_conv_b, m_lru_w_a, m_lru_b_a, m_lru_w_x, m_lru_b_x, m_lru_lambda, m_hgrn_lb_logits, m_hgrn_gnorm_w, m_w_out, m_post_norm_w, v_pre_norm_w, v_w_in, v_conv_w, v_conv_b, v_lru_w_a, v_lru_b_a, v_lru_w_x, v_lru_b_x, v_lru_lambda, v_hgrn_lb_logits, v_hgrn_gnorm_w, v_w_out, v_post_norm_w):
    t = x.shape[1]
    tm = min(256, t)
    tmm = min(512, t)
    chip = 2 * lax.axis_index("x") + lax.axis_index("y")

    win_units = w_in[0].astype(_MXU_DTYPE).reshape(D_MODEL, 3, UNIT).transpose(1, 0, 2)
    small_w = jnp.concatenate([conv_w[0], lru_b_a[0], lru_b_x[0]], axis=1)
    wg4, wo4, wa4, wx4, sw4 = _gather_weights(
        [win_units.reshape(2, 3 * 512, UNIT), w_out[0].astype(_MXU_DTYPE).reshape(2, 256, D_MODEL),
         lru_w_a[0].astype(_MXU_DTYPE).reshape(2, 128, LRU_BW), lru_w_x[0].astype(_MXU_DTYPE).reshape(2, 128, LRU_BW),
         small_w.reshape(1, 4, 384)],
        [True, True, True, True, False])
    wg = wg4.reshape(N_UNIT, D_MODEL, UNIT)
    wo = wo4.reshape(2 * D_MODEL, D_MODEL)
    by_block = lambda w4: w4.reshape(N_CHIPS, LRU_BLOCKS, 64, LRU_BW).transpose(1, 0, 2, 3).reshape(
        LRU_BLOCKS, LRU_BW, LRU_BW)
    wa, wx = by_block(wa4), by_block(wx4)
    sw4 = sw4.reshape(N_CHIPS, 4, 384)
    cw = sw4[:, :, 0:256].transpose(1, 0, 2).reshape(4, D_MODEL)
    ba = sw4[:, :, 256:320].transpose(1, 0, 2).reshape(1, D_MODEL)
    bx = sw4[:, :, 320:384].transpose(1, 0, 2).reshape(1, D_MODEL)

    def lru_layout(g):
        g = g.reshape(2, 2, N_CHIPS, 64, LRU_BW).transpose(0, 2, 1, 3, 4)
        return g.reshape(2, N_CHIPS * 128, LRU_BW)

    def prepare(g_in, g_out, d_wa, d_wx):
        g_in = g_in.reshape(2, N_UNIT * 512, UNIT)
        g_out = g_out.reshape(2, N_CHIPS * 256, D_MODEL)
        g_wa, g_wx = lru_layout(d_wa), lru_layout(d_wx)
        b_in, b_out, b_wa, b_wx = _core_swap([g_in, g_out, g_wa, g_wx], "reduce_swap", halved=True)
        parts = [_sum_own_half(g_in, b_in, "presum_in", _WIRE_DTYPE).reshape(N_CHIPS, 3 * 512, UNIT),
                 _sum_own_half(g_out, b_out, "presum_out", _WIRE_DTYPE).reshape(N_CHIPS, 256, D_MODEL),
                 _sum_own_half(g_wa, b_wa, "presum_wa", _WIRE_DTYPE).reshape(N_CHIPS, 128, LRU_BW),
                 _sum_own_half(g_wx, b_wx, "presum_wx", _WIRE_DTYPE).reshape(N_CHIPS, 128, LRU_BW)]
        return parts, [True] * 4

    grad_x, _, _, _, _, small, (p_in, p_out, p_wa, p_wx), (r_in, r_out, r_wa, r_wx) = _local_step(
        x[0], loss_target[0], pre_norm_w, wg, cw, conv_b, wa, wx, ba, bx, lru_lambda, hgrn_lb_logits,
        hgrn_gnorm_w, wo, post_norm_w, tm, tmm, prepare)
    s_in = _sum_chips(p_in, r_in, True, "sum_in")
    s_out = _sum_chips(p_out, r_out, True, "sum_out")
    s_wa = _sum_chips(p_wa, r_wa, True, "sum_wa")
    s_wx = _sum_chips(p_wx, r_wx, True, "sum_wx")
    (o_in, o_out, o_wa, o_wx), small_others = _finish_exchange([s_in, s_out, s_wa, s_wx], small)

    g_w_in, d_w_in, nm_w_in, nv_w_in = _adamw_big(
        s_in.reshape(3, 512, UNIT), o_in.reshape(3, 512, UNIT), w_in[0], m_w_in[0], v_w_in[0], "adamw_w_in", (3, 2),
        (None, 512, UNIT), lambda k: (k, 0, 0), (512, UNIT), lambda k, h: (h, k))
    g_w_out, d_w_out, nm_w_out, nv_w_out = _adamw_big(
        s_out, o_out, w_out[0], m_w_out[0], v_w_out[0], "adamw_w_out", (2,),
        (256, D_MODEL), lambda: (0, 0), (256, D_MODEL), lambda h: (h, 0))
    sq2 = lambda a: a.reshape(LRU_BW, LRU_BW)
    lru_specs = ((2,), (128, LRU_BW), lambda: (0, 0), (128, LRU_BW), lambda h: (h, 0))
    g_wa, d_wa2, nm_wa, nv_wa = _adamw_big(s_wa, o_wa, sq2(lru_w_a), sq2(m_lru_w_a), sq2(v_lru_w_a), "adamw_wa",
                                           *lru_specs)
    g_wx, d_wx2, nm_wx, nv_wx = _adamw_big(s_wx, o_wx, sq2(lru_w_x), sq2(m_lru_w_x), sq2(v_lru_w_x), "adamw_wx",
                                           *lru_specs)

    gs = _small_finish(small, small_others, lru_lambda, hgrn_lb_logits)
    loss = jnp.sum(gs[13]) * (0.5 / D_MODEL)
    g_pre = gs[0:1]
    g_cw = lax.dynamic_slice(gs[1:5], (0, chip * 256), (4, 256))
    g_cb = gs[5:6]
    g_ba = lax.dynamic_slice(gs[6].reshape(LRU_BLOCKS, N_CHIPS, 64), (0, chip, 0), (LRU_BLOCKS, 1, 64)).reshape(4, 64)
    g_bx = lax.dynamic_slice(gs[7].reshape(LRU_BLOCKS, N_CHIPS, 64), (0, chip, 0), (LRU_BLOCKS, 1, 64)).reshape(4, 64)
    g_lam = gs[8:9]
    g_lb = gs[9:11]
    g_gw = gs[11:12]
    g_post = gs[12:13]
    small_g = [g_pre, g_cw, g_cb, g_ba, g_bx, g_lam, g_lb, g_gw, g_post]
    two_d = lambda a: a.reshape(a.shape[-2:])
    small_w_list = [pre_norm_w, conv_w, conv_b, lru_b_a, lru_b_x, lru_lambda, hgrn_lb_logits, hgrn_gnorm_w, post_norm_w]
    small_m_list = [m_pre_norm_w, m_conv_w, m_conv_b, m_lru_b_a, m_lru_b_x, m_lru_lambda, m_hgrn_lb_logits,
                    m_hgrn_gnorm_w, m_post_norm_w]
    small_v_list = [v_pre_norm_w, v_conv_w, v_conv_b, v_lru_b_a, v_lru_b_x, v_lru_lambda, v_hgrn_lb_logits,
                    v_hgrn_gnorm_w, v_post_norm_w]
    sd, sm, sv = _adamw_small([two_d(a) for a in small_w_list], small_g,
                              [two_d(a) for a in small_m_list], [two_d(a) for a in small_v_list])

    def shaped(vals, refs):
        return [val.reshape(ref.shape) for val, ref in zip(vals, refs)]

    s_g = shaped(small_g, small_w_list)
    s_d = shaped(sd, small_w_list)
    s_m = shaped(sm, small_w_list)
    s_v = shaped(sv, small_w_list)

    def ordered(small, big_in, big_wa, big_wx, big_out):
        pre, cw_, cb_, ba_, bx_, lam_, lb_, gw_, post_ = small
        return [pre, big_in.reshape(w_in.shape), cw_, cb_, big_wa.reshape(lru_w_a.shape), ba_,
                big_wx.reshape(lru_w_x.shape), bx_, lam_, lb_, gw_, big_out.reshape(w_out.shape), post_]

    grads = ordered(s_g, g_w_in, g_wa, g_wx, g_w_out)
    deltas = ordered(s_d, d_w_in, d_wa2, d_wx2, d_w_out)
    new_m = ordered(s_m, nm_w_in, nm_wa, nm_wx, nm_w_out)
    new_v = ordered(s_v, nv_w_in, nv_wa, nv_wx, nv_w_out)
    return (loss, grad_x.reshape(x.shape), *grads, *deltas, *new_m, *new_v)
```

```python
import functools

import jax
import jax.numpy as jnp
from jax import lax
from jax.experimental import pallas as pl
from jax.experimental.pallas import tpu as pltpu

F32 = jnp.float32
BF16 = jnp.bfloat16
_MXU_DTYPE = jnp.bfloat16
_WIRE_DTYPE = jnp.bfloat16

D_MODEL = 1024
N_SEG = 6
UNIT = 512
N_UNIT = 12
LRU_BLOCKS = 4
LRU_BW = 256
LRU_C = 8.0
HEADS = 8
HEAD_D = 128
CHUNK = 64
EPS = 1e-6
N_CHIPS = 4
EXP_CLAMP = 80.0

ADAM_LR = 0.001
ADAM_B1 = 0.9
ADAM_B2 = 0.999
ADAM_EPS = 1e-08
ADAM_WD = 0.01
ADAM_STEP = 10

SMALL_ROWS = 16
FWD_PIECES = 4
MESH = pl.DeviceIdType.MESH


def _sigmoid(x):
    return 0.5 * jnp.tanh(0.5 * x) + 0.5


def _mx(x):
    return x.astype(_MXU_DTYPE)


def _dot(a, b):
    return jnp.dot(_mx(a), _mx(b), preferred_element_type=F32)


def _dot_nt(a, b):
    return lax.dot_general(_mx(a), _mx(b), (((1,), (1,)), ((), ())), preferred_element_type=F32)


def _dot_tn(a, b):
    return lax.dot_general(_mx(a), _mx(b), (((0,), (0,)), ((), ())), preferred_element_type=F32)


def _rows(shape):
    return lax.broadcasted_iota(jnp.int32, shape, 0)


def _cols(shape):
    return lax.broadcasted_iota(jnp.int32, shape, 1)


def _softplus_neg(lam):
    z = -lam
    e = jnp.exp(-jnp.abs(z))
    series = e * (1.0 - e * (0.5 - e * (1.0 / 3.0 - 0.25 * e)))
    return jnp.maximum(z, 0.0) + jnp.where(e < 1e-2, series, jnp.log(1.0 + e))


def _one_minus_sq(a, y):
    series = -y * (1.0 + y * (0.5 + y * (1.0 / 6.0)))
    return jnp.where(y > -0.01, series, 1.0 - a * a)


def _cumsum_rows(x, reverse=False):
    n = x.shape[0] // 8
    row8 = _rows((8, x.shape[1]))
    out = [None] * n
    carry = None
    for g in (reversed(range(n)) if reverse else range(n)):
        blk = x[8 * g:8 * g + 8]
        for s in (1, 2, 4):
            if reverse:
                blk = blk + jnp.where(row8 < 8 - s, pltpu.roll(blk, 8 - s, 0), 0.0)
            else:
                blk = blk + jnp.where(row8 >= s, pltpu.roll(blk, s, 0), 0.0)
        if carry is not None:
            blk = blk + carry
        carry = blk[0:1] if reverse else blk[7:8]
        out[g] = blk
    return jnp.concatenate(out, axis=0)


def _shift_down(x, halo, s):
    if s == 0:
        return x
    r = pltpu.roll(x, s, 0)
    top = jnp.where(_rows(halo.shape) < s, pltpu.roll(halo, s, 0), r[0:8])
    return jnp.concatenate([top, r[8:]], axis=0)


def _shift_up(x, head, s):
    if s == 0:
        return x
    n = x.shape[0]
    r = pltpu.roll(x, n - s, 0)
    bottom = jnp.where(_rows(head.shape) >= 8 - s, pltpu.roll(head, 8 - s, 0), r[n - 8:n])
    return jnp.concatenate([r[:n - 8], bottom], axis=0)


def _lru_gates(lx, halo, cw, cb, wa, wx, ba, bx, sp):
    xs = [_shift_down(lx, halo, 3 - k) for k in range(4)]
    xc = cb + cw[0:1] * xs[0] + cw[1:2] * xs[1] + cw[2:3] * xs[2] + cw[3:4] * xs[3]
    zr, zi = [], []
    for n in range(LRU_BLOCKS):
        xb = _mx(xc[:, n * LRU_BW:(n + 1) * LRU_BW])
        zr.append(jnp.dot(xb, wa[n], preferred_element_type=F32))
        zi.append(jnp.dot(xb, wx[n], preferred_element_type=F32))
    r = _sigmoid(jnp.concatenate(zr, axis=1) + ba)
    ig = _sigmoid(jnp.concatenate(zi, axis=1) + bx)
    la = (-LRU_C * sp) * r
    a = jnp.exp(la)
    om = _one_minus_sq(a, 2.0 * la)
    return xs, xc, r, ig, a, om


def _hgrn_prep(q, fr, lb):
    sig = _sigmoid(fr)
    f = lb + (1.0 - lb) * sig
    sq = _sigmoid(q)
    b = _cumsum_rows(jnp.log(f))
    bm = b[CHUNK // 2 - 1:CHUNK // 2]
    bc = b[CHUNK - 1:CHUNK]
    em1 = jnp.exp(jnp.minimum(b - bm, EXP_CLAMP))
    em2 = jnp.exp(jnp.minimum(bm - b, EXP_CLAMP))
    qm = _mx((q * sq) * em1)
    km = _mx((1.0 - f) * em2)
    return sig, f, sq, em1, em2, qm, km, jnp.exp(bm), jnp.exp(bc - bm), jnp.exp(bc)


def _pad_lanes(parts, width):
    have = sum(p.shape[1] for p in parts)
    pad = [jnp.zeros((parts[0].shape[0], width - have), parts[0].dtype)] if width > have else []
    return jnp.concatenate(list(parts) + pad, axis=1)


def _pad_rows(parts, height):
    have = sum(p.shape[0] for p in parts)
    pad = [jnp.zeros((height - have, parts[0].shape[1]), parts[0].dtype)] if height > have else []
    return jnp.concatenate(list(parts) + pad, axis=0)


def _adamw(w, g, m, v):
    m = ADAM_B1 * m + (1.0 - ADAM_B1) * g
    v = ADAM_B2 * v + (1.0 - ADAM_B2) * jnp.square(g)
    m_hat = m / (1.0 - ADAM_B1 ** ADAM_STEP)
    v_hat = v / (1.0 - ADAM_B2 ** ADAM_STEP)
    delta = -ADAM_LR * (m_hat / (jnp.sqrt(v_hat) + ADAM_EPS) + ADAM_WD * w)
    return delta, m, v


def _full(shape):
    nd = len(shape)
    return pl.BlockSpec(shape, lambda *_: (0,) * nd)


def _seq_params():
    return pltpu.CompilerParams(dimension_semantics=("arbitrary",))


def _resident(shape):
    nd = len(shape)
    return pl.BlockSpec(shape, lambda *_: (0,) * nd, pipeline_mode=pl.Buffered(1))


def _in_proj_gather(x, w, win_units, halved, small_w, tm):
    t = x.shape[0]
    nt = t // tm
    n_h = len(halved)
    n_split = 3 + n_h
    n_items = n_split + 1
    hu = D_MODEL // 2
    chip = 2 * lax.axis_index("x") + lax.axis_index("y")
    flips = (2, 1, 3)
    order = jnp.stack([3 * chip + u for u in range(3)]
                      + [3 * jnp.bitwise_xor(chip, flips[jj]) + u for u in range(3) for jj in range(3)])

    def body(order_ref, x_ref, w_ref, win_ref, *rest):
        del order_ref
        h_ins, small_ref = rest[:n_h], rest[n_h]
        ut_ref, p_ref, wg_ref = rest[n_h + 1:n_h + 4]
        h_outs, sw_ref = rest[n_h + 4:2 * n_h + 4], rest[2 * n_h + 4]
        u_s, wbuf, send, recv, fsend, frecv, osend, orecv, wsem = rest[2 * n_h + 5:]
        j, i = pl.program_id(0), pl.program_id(1)
        mx, my, c = lax.axis_index("x"), lax.axis_index("y"), lax.axis_index("c")
        me = 2 * mx + my
        peers = [(1 - mx, my), (mx, 1 - my), (1 - mx, 1 - my)]
        sibling = (mx, my, 1 - c)

        def remote(src, dst, ssem, rsem, dev):
            return pltpu.make_async_remote_copy(src_ref=src, dst_ref=dst, send_sem=ssem, recv_sem=rsem,
                                                device_id=dev, device_id_type=MESH)

        def mine(item):
            if item < 3:
                return win_ref.at[item, pl.ds(c * hu, hu)]
            return h_ins[item - 3].at[c] if item < n_split else small_ref

        def piece(item, q, h):
            if item < 3:
                return wg_ref.at[q, item, pl.ds(h * hu, hu)]
            return h_outs[item - 3].at[q, h] if item < n_split else sw_ref.at[q]

        def fetch(item, jj, q):
            px, py = peers[jj]
            return remote(mine(item), piece(item, q, c), send.at[item * 3 + jj], recv.at[item * 3 + jj], (px, py, c))

        def forward(item, jj, q, h):
            whole = piece(item, q, h)
            step = whole.shape[0] // FWD_PIECES
            copies = []
            for k in range(FWD_PIECES):
                part = whole.at[pl.ds(k * step, step)]
                sem = (item * 3 + jj) * FWD_PIECES + k
                copies.append(remote(part, part, fsend.at[sem], frecv.at[sem], sibling))
            return copies

        def own(item):
            if item < 3:
                src, dst = win_ref.at[item], wg_ref.at[me, item]
            elif item < n_split:
                src, dst = h_ins[item - 3], h_outs[item - 3].at[me]
            else:
                src, dst = small_ref, sw_ref.at[me]
            return remote(src, dst, osend.at[item], orecv.at[item], sibling)

        def unit_of(jn):
            u, jj = (jn - 3) // 3, (jn - 3) % 3
            q = jnp.bitwise_xor(me, jnp.where(jj == 0, flips[0], jnp.where(jj == 1, flips[1], flips[2])))
            return u, jj, q

        def unit_fetch(u, jj, q):
            half = wg_ref.at[q, u, pl.ds(c * hu, hu)]
            return remote(half, half, send.at[u * 3 + jj], recv.at[u * 3 + jj], sibling)

        def unit_forward(u, jj, q, h):
            step = hu // FWD_PIECES
            copies = []
            for k in range(FWD_PIECES):
                part = wg_ref.at[q, u, pl.ds(h * hu + k * step, step)]
                sem = (u * 3 + jj) * FWD_PIECES + k
                copies.append(remote(part, part, fsend.at[sem], frecv.at[sem], sibling))
            return copies

        @pl.when(jnp.logical_and(j == 0, i == 0))
        def _():
            for item in range(n_items):
                for jj in range(3):
                    fetch(item, jj, me).start()
            for item in range(n_items):
                own(item).start()

        @pl.when(j == 0)
        def _():
            xv = x_ref[...]
            rstd = lax.rsqrt(jnp.mean(xv * xv, axis=-1, keepdims=True) + EPS)
            u = xv * rstd * w_ref[...]
            ut_ref[...] = u.T.astype(_MXU_DTYPE)
            u_s[pl.ds(pl.multiple_of(i * tm, tm), tm), :] = _mx(u)

        @pl.when(jnp.logical_and(i == 0, j < 3))
        def _():
            cp = pltpu.make_async_copy(win_ref.at[j], wbuf, wsem.at[0])
            cp.start()
            cp.wait()

        @pl.when(jnp.logical_and(i == 0, j >= 3))
        def _():
            u, jj, q = unit_of(j)
            for cp in unit_forward(u, jj, q, 1 - c):
                cp.wait_recv()
            cp = pltpu.make_async_copy(wg_ref.at[q, u], wbuf, wsem.at[0])
            cp.start()
            cp.wait()

        p_ref[...] = jnp.dot(u_s[pl.ds(pl.multiple_of(i * tm, tm), tm), :], wbuf[...], preferred_element_type=F32)

        @pl.when(jnp.logical_and(i == nt - 1, jnp.logical_and(j >= 1, j < N_UNIT - 2)))
        def _():
            u, jj, q = unit_of(j + 2)
            unit_fetch(u, jj, q).wait_recv()
            for cp in unit_forward(u, jj, q, c):
                cp.start()

        @pl.when(jnp.logical_and(i == nt - 1, j == N_UNIT - 1))
        def _():
            qs = [2 * px + py for px, py in peers]
            for item in range(3, n_items):
                for jj in range(3):
                    fetch(item, jj, qs[jj]).wait_recv()
                    if item < n_split:
                        for cp in forward(item, jj, qs[jj], c):
                            cp.start()
            for item in range(3, n_split):
                for jj in range(3):
                    for cp in forward(item, jj, qs[jj], 1 - c):
                        cp.wait_recv()
            for item in range(n_items):
                own(item).wait_recv()
            for item in range(n_items):
                for jj in range(3):
                    fetch(item, jj, me).wait_send()
                    if item < n_split:
                        for cp in forward(item, jj, qs[jj], c):
                            cp.wait_send()
                own(item).wait_send()

    any_spec = pl.BlockSpec(memory_space=pl.ANY)
    first_pass = lambda j, i: jnp.where(j == 0, i, nt - 1)
    res = pl.pallas_call(
        body, name="in_proj",
        grid_spec=pltpu.PrefetchScalarGridSpec(
            num_scalar_prefetch=1, grid=(N_UNIT, nt),
            in_specs=[pl.BlockSpec((tm, D_MODEL), lambda j, i, o: (first_pass(j, i), 0)),
                      pl.BlockSpec((1, D_MODEL), lambda j, i, o: (0, 0))] + [any_spec] * (n_h + 2),
            out_specs=[pl.BlockSpec((D_MODEL, tm), lambda j, i, o: (0, first_pass(j, i))),
                       pl.BlockSpec((None, tm, UNIT), lambda j, i, o: (o[j] // 2, i, o[j] % 2))]
            + [any_spec] * (n_h + 2),
            scratch_shapes=[pltpu.VMEM((t, D_MODEL), _MXU_DTYPE), pltpu.VMEM((D_MODEL, UNIT), _MXU_DTYPE),
                            pltpu.SemaphoreType.DMA((3 * n_items,)), pltpu.SemaphoreType.DMA((3 * n_items,)),
                            pltpu.SemaphoreType.DMA((3 * n_split * FWD_PIECES,)),
                            pltpu.SemaphoreType.DMA((3 * n_split * FWD_PIECES,)),
                            pltpu.SemaphoreType.DMA((n_items,)), pltpu.SemaphoreType.DMA((n_items,)),
                            pltpu.SemaphoreType.DMA((1,))]),
        out_shape=[jax.ShapeDtypeStruct((D_MODEL, t), _MXU_DTYPE), jax.ShapeDtypeStruct((N_SEG, t, D_MODEL), F32),
                   jax.ShapeDtypeStruct((N_CHIPS,) + win_units.shape, win_units.dtype)]
        + [jax.ShapeDtypeStruct((N_CHIPS,) + a.shape, a.dtype) for a in halved]
        + [jax.ShapeDtypeStruct((N_CHIPS,) + small_w.shape, small_w.dtype)],
        compiler_params=pltpu.CompilerParams(dimension_semantics=("arbitrary", "arbitrary")),
    )(order, x, w, win_units, *halved, small_w)
    return res[0], res[1], res[2], res[3:3 + n_h], res[3 + n_h]


def _lru_fwd(p, wa, wx, ba, bx, lam, cw, cb, tm):
    t = p.shape[1]
    ng = tm // 8

    def body(lx_ref, halo_ref, lg_ref, wa_ref, wx_ref, ba_ref, bx_ref, lam_ref, cw_ref, cb_ref,
             h_ref, y_ref, a_s, u_s, hc_s):
        i = pl.program_id(0)

        @pl.when(i == 0)
        def _():
            hc_s[...] = jnp.zeros_like(hc_s)

        sp = _softplus_neg(lam_ref[...])
        halo = jnp.where(i == 0, 0.0, halo_ref[...])
        _, xc, _, ig, a, om = _lru_gates(lx_ref[...], halo, cw_ref[...], cb_ref[...], wa_ref[...],
                                         wx_ref[...], ba_ref[...], bx_ref[...], sp)
        a_s[...] = a
        u_s[...] = jnp.sqrt(om) * (ig * xc)
        row8 = _rows((8, D_MODEL))

        def group(g, hc):
            rows = pl.ds(pl.multiple_of(g * 8, 8), 8)
            av = a_s[rows, :]
            uv = u_s[rows, :]
            for s in (1, 2, 4):
                a_sh = jnp.where(row8 >= s, pltpu.roll(av, s, 0), 1.0)
                u_sh = jnp.where(row8 >= s, pltpu.roll(uv, s, 0), 0.0)
                uv = av * u_sh + uv
                av = av * a_sh
            hh = av * hc + uv
            h_ref[rows, :] = hh
            return jnp.broadcast_to(hh[7:8, :], (8, D_MODEL))

        hc_s[...] = lax.fori_loop(0, ng, group, hc_s[...])
        lg = lg_ref[...]
        y_ref[...] = (h_ref[...] * (lg * _sigmoid(lg))).astype(_MXU_DTYPE)

    seg = lambda s: pl.BlockSpec((None, tm, D_MODEL), lambda i: (s, i, 0))
    return pl.pallas_call(
        body, name="lru_fwd", grid=(t // tm,),
        in_specs=[seg(0),
                  pl.BlockSpec((None, 8, D_MODEL), lambda i: (0, jnp.maximum(i * (tm // 8) - 1, 0), 0)),
                  seg(1),
                  _full((LRU_BLOCKS, LRU_BW, LRU_BW)), _full((LRU_BLOCKS, LRU_BW, LRU_BW)),
                  _full((1, D_MODEL)), _full((1, D_MODEL)), _full((1, D_MODEL)),
                  _full((4, D_MODEL)), _full((1, D_MODEL))],
        out_specs=[pl.BlockSpec((tm, D_MODEL), lambda i: (i, 0)),
                   pl.BlockSpec((None, tm, D_MODEL), lambda i: (0, i, 0))],
        out_shape=[jax.ShapeDtypeStruct((t, D_MODEL), F32),
                   jax.ShapeDtypeStruct((2, t, D_MODEL), _MXU_DTYPE)],
        scratch_shapes=[pltpu.VMEM((tm, D_MODEL), F32), pltpu.VMEM((tm, D_MODEL), F32),
                        pltpu.VMEM((8, D_MODEL), F32)],
        compiler_params=_seq_params(),
    )(p, p, p, wa, wx, ba, bx, lam, cw, cb)


def _hgrn_fwd(p, logits, gw, y2, tm):
    t = p.shape[1]
    nc = tm // CHUNK

    def body(qf_ref, vh_ref, lg_ref, gw_ref, y_in, y_ref, o_ref, st_ref, state):
        del y_in
        i = pl.program_id(0)

        @pl.when(i == 0)
        def _():
            state[...] = jnp.zeros_like(state)

        l0 = lg_ref[0:1, :]
        l1 = lg_ref[1:2, :]
        lb = 1.0 / (1.0 + jnp.exp(l1 - l0))
        gwv = gw_ref[...]
        causal = _rows((CHUNK, CHUNK)) >= _cols((CHUNK, CHUNK))

        for c in range(nc):
            rows = pl.ds(c * CHUNK, CHUNK)
            v = vh_ref[0, rows, :]
            hg = vh_ref[1, rows, :]
            _, _, _, _, _, qm, km, ebm, ecm, ebc = _hgrn_prep(qf_ref[0, rows, :], qf_ref[1, rows, :], lb)
            ke = _mx(km.astype(F32) * ecm)
            gate = gwv * (hg * _sigmoid(hg))
            o_parts, y_parts = [], []
            for h in range(HEADS):
                ln = slice(h * HEAD_D, (h + 1) * HEAD_D)
                s0 = state[h]
                st_ref[c, h] = s0
                vt = _mx(v[:, ln].T)
                att = _mx(jnp.where(causal, _dot_nt(qm[:, ln], km[:, ln]), 0.0))
                o = _dot_nt(_pad_lanes([qm[:, ln], att], 2 * HEAD_D),
                            _pad_lanes([_mx(s0 * ebm[:, ln]), vt], 2 * HEAD_D))
                state[h] = s0 * ebc[:, ln] + jnp.dot(vt, ke[:, ln], preferred_element_type=F32)
                rstd = lax.rsqrt(jnp.mean(o * o, axis=-1, keepdims=True) + EPS)
                o_parts.append(o)
                y_parts.append((o * rstd * gate[:, ln]).astype(_MXU_DTYPE))
            o_ref[rows, :] = jnp.concatenate(o_parts, axis=1)
            y_ref[rows, :] = jnp.concatenate(y_parts, axis=1)

    pair = lambda s: pl.BlockSpec((2, tm, D_MODEL), lambda i: (s, i, 0))
    return pl.pallas_call(
        body, name="hgrn_fwd", grid=(t // tm,),
        in_specs=[pair(1), pair(2), _full((2, D_MODEL)), _full((1, D_MODEL)),
                  pl.BlockSpec(memory_space=pl.ANY)],
        out_specs=[pl.BlockSpec((None, tm, D_MODEL), lambda i: (1, i, 0)),
                   pl.BlockSpec((tm, D_MODEL), lambda i: (i, 0)),
                   pl.BlockSpec((nc, HEADS, HEAD_D, HEAD_D), lambda i: (i, 0, 0, 0))],
        out_shape=[jax.ShapeDtypeStruct((2, t, D_MODEL), _MXU_DTYPE),
                   jax.ShapeDtypeStruct((t, D_MODEL), F32),
                   jax.ShapeDtypeStruct((t // CHUNK, HEADS, HEAD_D, HEAD_D), F32)],
        scratch_shapes=[pltpu.VMEM((HEADS, HEAD_D, HEAD_D), F32)],
        input_output_aliases={4: 0},
        compiler_params=_seq_params(),
    )(p, p, logits, gw, y2)


def _out_fused(y2, wo, x, tgt, wpost, tm):
    t = x.shape[0]
    inv_d = 1.0 / D_MODEL

    def body(y_ref, wo_ref, x_ref, t_ref, w_ref, dout_ref, dy_ref, sq_ref, dw_ref):
        i = pl.program_id(0)

        @pl.when(i == 0)
        def _():
            sq_ref[...] = jnp.zeros_like(sq_ref)
            dw_ref[...] = jnp.zeros_like(dw_ref)

        y = (jnp.dot(y_ref[0], wo_ref[0:D_MODEL, :], preferred_element_type=F32)
             + jnp.dot(y_ref[1], wo_ref[D_MODEL:2 * D_MODEL, :], preferred_element_type=F32))
        w = w_ref[...]
        rstd = lax.rsqrt(jnp.mean(y * y, axis=-1, keepdims=True) + EPS)
        n = y * rstd
        err = (x_ref[...] + n * w) - t_ref[...]
        sq_ref[...] += jnp.sum(err * err, axis=0, keepdims=True)
        dout = err * inv_d
        dout_ref[...] = dout
        dw_ref[...] += jnp.sum(dout * n, axis=0, keepdims=True)
        dn = dout * w
        dy_ref[...] = (rstd * (dn - n * jnp.mean(dn * n, axis=-1, keepdims=True))).astype(_MXU_DTYPE)

    row = pl.BlockSpec((tm, D_MODEL), lambda i: (i, 0))
    return pl.pallas_call(
        body, name="out_fused", grid=(t // tm,),
        in_specs=[pl.BlockSpec((2, tm, D_MODEL), lambda i: (0, i, 0)), _full((2 * D_MODEL, D_MODEL)),
                  row, row, _full((1, D_MODEL))],
        out_specs=[row, row, _full((1, D_MODEL)), _full((1, D_MODEL))],
        out_shape=[jax.ShapeDtypeStruct((t, D_MODEL), F32), jax.ShapeDtypeStruct((t, D_MODEL), _MXU_DTYPE),
                   jax.ShapeDtypeStruct((1, D_MODEL), F32), jax.ShapeDtypeStruct((1, D_MODEL), F32)],
        compiler_params=_seq_params(),
    )(y2, wo, x, tgt, wpost)


def _dw_out(y2, dy, tt):
    t = dy.shape[0]
    hr = UNIT // 2

    def body(y_ref, dy_ref, g_ref):
        @pl.when(pl.program_id(1) == 0)
        def _():
            g_ref[...] = jnp.zeros_like(g_ref)

        r = _dot_tn(y_ref[...], dy_ref[...])
        g_ref[0] += r[0:hr]
        g_ref[1] += r[hr:UNIT]

    return pl.pallas_call(
        body, name="dw_out", grid=(N_CHIPS, t // tt),
        in_specs=[pl.BlockSpec((None, tt, UNIT), lambda c, k: (c // 2, k, c % 2)),
                  pl.BlockSpec((tt, D_MODEL), lambda c, k: (k, 0))],
        out_specs=pl.BlockSpec((2, None, hr, D_MODEL), lambda c, k: (0, c, 0, 0)),
        out_shape=jax.ShapeDtypeStruct((2, N_CHIPS, hr, D_MODEL), F32),
        compiler_params=pltpu.CompilerParams(dimension_semantics=("arbitrary", "arbitrary")),
    )(y2, dy)


def _lru_bwd(dy, wo, p, h, wa, wx, ba, bx, lam, cw, cb, tm):
    t = dy.shape[0]
    nt = t // tm
    ng = tm // 8

    def body(dy_ref, wo_ref, lx_ref, halo_ref, lg_ref, h_ref, hhalo_ref, wa_ref, wx_ref, ba_ref, bx_ref,
             lam_ref, cw_ref, cb_ref,
             dp_ref, dwa_ref, dwx_ref, dba_ref, dbx_ref, dsp_ref, dcw_ref, dcb_ref,
             c_s, in_s, dh_s, dhc_s, afirst_s, dxc_s):
        i = pl.program_id(0)
        first = i == nt - 1

        @pl.when(i == 0)
        def _():
            for ref in (dwa_ref, dwx_ref, dba_ref, dbx_ref, dsp_ref, dcw_ref, dcb_ref, dhc_s, afirst_s, dxc_s):
                ref[...] = jnp.zeros_like(ref)

        sp = _softplus_neg(lam_ref[...])
        cwv = cw_ref[...]
        wav = wa_ref[...]
        wxv = wx_ref[...]
        halo = jnp.where(first, 0.0, halo_ref[...])
        xs, xc, r, ig, a, om = _lru_gates(lx_ref[...], halo, cwv, cb_ref[...], wav, wxv,
                                          ba_ref[...], bx_ref[...], sp)
        inv_mult = lax.rsqrt(om)
        mult = om * inv_mult
        d_y = _dot_nt(dy_ref[...], wo_ref[...])
        lg = lg_ref[...]
        sl = _sigmoid(lg)
        hv = h_ref[...]
        d_lg = d_y * hv * (sl * (1.0 + lg * (1.0 - sl)))

        c_s[...] = _shift_up(a, afirst_s[...], 1)
        in_s[...] = d_y * (lg * sl)
        row8 = _rows((8, D_MODEL))

        def group(gg, carry):
            rows = pl.ds(pl.multiple_of((ng - 1 - gg) * 8, 8), 8)
            cv = c_s[rows, :]
            uv = in_s[rows, :]
            for s in (1, 2, 4):
                c_sh = jnp.where(row8 < 8 - s, pltpu.roll(cv, 8 - s, 0), 1.0)
                u_sh = jnp.where(row8 < 8 - s, pltpu.roll(uv, 8 - s, 0), 0.0)
                uv = cv * u_sh + uv
                cv = cv * c_sh
            hh = cv * carry + uv
            dh_s[rows, :] = hh
            return jnp.broadcast_to(hh[0:1, :], (8, D_MODEL))

        dhc_s[...] = lax.fori_loop(0, ng, group, dhc_s[...])
        afirst_s[...] = a[0:8]
        dh = dh_s[...]

        h_prev = _shift_down(hv, jnp.where(first, 0.0, hhalo_ref[...]), 1)
        gx = ig * xc
        d_mult = dh * gx
        d_i = dh * (mult * xc)
        d_xc = dh * (mult * ig)
        d_la = (dh * h_prev) * a - d_mult * ((a * a) * inv_mult)
        dsp_ref[...] += jnp.sum(d_la * r, axis=0, keepdims=True) * (-LRU_C)
        d_zr = (d_la * (-LRU_C * sp)) * (r * (1.0 - r))
        d_zi = d_i * (ig * (1.0 - ig))
        dba_ref[...] += jnp.sum(d_zr, axis=0, keepdims=True)
        dbx_ref[...] += jnp.sum(d_zi, axis=0, keepdims=True)
        back = []
        for n in range(LRU_BLOCKS):
            ln = slice(n * LRU_BW, (n + 1) * LRU_BW)
            xb = _mx(xc[:, ln])
            zr_n = _mx(d_zr[:, ln])
            zi_n = _mx(d_zi[:, ln])
            dwa_ref[n] += _dot_tn(xb, zr_n)
            dwx_ref[n] += _dot_tn(xb, zi_n)
            back.append(_dot_nt(zr_n, wav[n]) + _dot_nt(zi_n, wxv[n]))
        d_xc = d_xc + jnp.concatenate(back, axis=1)
        dcb_ref[...] += jnp.sum(d_xc, axis=0, keepdims=True)
        for k in range(4):
            dcw_ref[k:k + 1, :] += jnp.sum(d_xc * xs[k], axis=0, keepdims=True)
        head = dxc_s[...]
        d_lx = cwv[3:4] * d_xc
        for k in range(3):
            d_lx = d_lx + cwv[k:k + 1] * _shift_up(d_xc, head, 3 - k)
        dxc_s[...] = d_xc[0:8]
        dp_ref[0] = d_lx.astype(_MXU_DTYPE)
        dp_ref[1] = d_lg.astype(_MXU_DTYPE)

    rev = lambda i: nt - 1 - i
    seg = lambda s: pl.BlockSpec((None, tm, D_MODEL), lambda i: (s, rev(i), 0))
    halo_row = lambda i: jnp.maximum(rev(i) * (tm // 8) - 1, 0)
    row = pl.BlockSpec((tm, D_MODEL), lambda i: (rev(i), 0))
    vec = _full((1, D_MODEL))
    wblk = _full((LRU_BLOCKS, LRU_BW, LRU_BW))
    return pl.pallas_call(
        body, name="lru_bwd", grid=(nt,),
        in_specs=[row, pl.BlockSpec((D_MODEL, D_MODEL), lambda i: (0, 0)),
                  seg(0), pl.BlockSpec((None, 8, D_MODEL), lambda i: (0, halo_row(i), 0)), seg(1),
                  row, pl.BlockSpec((8, D_MODEL), lambda i: (halo_row(i), 0)),
                  wblk, wblk, vec, vec, vec, _full((4, D_MODEL)), vec],
        out_specs=[pl.BlockSpec((2, tm, D_MODEL), lambda i: (2, rev(i), 0)),
                   wblk, wblk, vec, vec, vec, _full((4, D_MODEL)), vec],
        out_shape=[jax.ShapeDtypeStruct((N_SEG, t, D_MODEL), _MXU_DTYPE),
                   jax.ShapeDtypeStruct((LRU_BLOCKS, LRU_BW, LRU_BW), F32),
                   jax.ShapeDtypeStruct((LRU_BLOCKS, LRU_BW, LRU_BW), F32),
                   jax.ShapeDtypeStruct((1, D_MODEL), F32), jax.ShapeDtypeStruct((1, D_MODEL), F32),
                   jax.ShapeDtypeStruct((1, D_MODEL), F32), jax.ShapeDtypeStruct((4, D_MODEL), F32),
                   jax.ShapeDtypeStruct((1, D_MODEL), F32)],
        scratch_shapes=[pltpu.VMEM((tm, D_MODEL), F32), pltpu.VMEM((tm, D_MODEL), F32),
                        pltpu.VMEM((tm, D_MODEL), F32), pltpu.VMEM((8, D_MODEL), F32),
                        pltpu.VMEM((8, D_MODEL), F32), pltpu.VMEM((8, D_MODEL), F32)],
        compiler_params=_seq_params(),
    )(dy, wo, p, p, p, h, h, wa, wx, ba, bx, lam, cw, cb)


def _hgrn_bwd(dy, wo, p, o, states, logits, gw, dp, tm):
    t = dy.shape[0]
    nt = t // tm
    nc = tm // CHUNK

    def body(dy_ref, wo_ref, qf_ref, vh_ref, o_ref, st_ref, lg_ref, gw_ref, dp_in,
             dp_ref, dgw_ref, dlb_ref, dyh_s, dstate, snext):
        del dp_in
        i = pl.program_id(0)

        @pl.when(i == 0)
        def _():
            for ref in (dgw_ref, dlb_ref, dstate, snext):
                ref[...] = jnp.zeros_like(ref)

        dyh_s[...] = _dot_nt(dy_ref[...], wo_ref[...])
        l0 = lg_ref[0:1, :]
        l1 = lg_ref[1:2, :]
        lb = 1.0 / (1.0 + jnp.exp(l1 - l0))
        gwv = gw_ref[...]
        causal = _rows((CHUNK, CHUNK)) >= _cols((CHUNK, CHUNK))
        wide = 2 * HEAD_D

        for c in reversed(range(nc)):
            rows = pl.ds(c * CHUNK, CHUNK)
            q = qf_ref[0, rows, :]
            hg = vh_ref[1, rows, :]
            ov = o_ref[rows, :]
            dyh = dyh_s[rows, :]
            sig, f, sq, em1, em2, qm, km, ebm, ecm, ebc = _hgrn_prep(q, qf_ref[1, rows, :], lb)
            ke = _mx(km.astype(F32) * ecm)
            vb = _mx(vh_ref[0, rows, :])
            sg = _sigmoid(hg)
            d_on = dyh * (hg * sg)
            x_p, y_p, dv_p, n_p, ex_p = [], [], [], [], []
            for h in range(HEADS):
                ln = slice(h * HEAD_D, (h + 1) * HEAD_D)
                o_h = ov[:, ln]
                rstd = lax.rsqrt(jnp.mean(o_h * o_h, axis=-1, keepdims=True) + EPS)
                n = o_h * rstd
                dn = d_on[:, ln] * gwv[:, ln]
                d_o = rstd * (dn - n * jnp.mean(dn * n, axis=-1, keepdims=True))
                d_ob = _mx(d_o)
                d_ot = _mx(d_o.T)
                s0 = st_ref[c, h]
                ds = dstate[h]
                att = jnp.where(causal, _dot_nt(qm[:, ln], km[:, ln]), 0.0)
                d_att = jnp.where(causal, _dot_nt(d_ob, vb[:, ln]), 0.0)
                x_p.append(jnp.dot(_pad_lanes([d_ob, _mx(d_att)], wide),
                                   _pad_rows([_mx(s0 * ebm[:, ln]), km[:, ln]], wide), preferred_element_type=F32))
                y_p.append(jnp.dot(_pad_lanes([vb[:, ln], _mx(d_att.T)], wide),
                                   _pad_rows([_mx(ds * ecm[:, ln]), qm[:, ln]], wide), preferred_element_type=F32))
                dv_p.append(_dot_nt(_pad_lanes([ke[:, ln], _mx(att.T)], wide), _pad_lanes([_mx(ds), d_ot], wide)))
                ex_p.append(jnp.sum(ds * snext[h], axis=0, keepdims=True))
                dstate[h] = ds * ebc[:, ln] + jnp.dot(d_ot, qm[:, ln], preferred_element_type=F32) * ebm[:, ln]
                snext[h] = s0
                n_p.append(n)
            cat = lambda parts: jnp.concatenate(parts, axis=1)
            x_all, y_all, n_all = cat(x_p), cat(y_p), cat(n_p)
            dg = _cumsum_rows(qm.astype(F32) * x_all - km.astype(F32) * y_all, reverse=True) + cat(ex_p)
            df = dg / f - em2 * y_all
            dgw_ref[...] += jnp.sum(d_on * n_all, axis=0, keepdims=True)
            dlb_ref[...] += jnp.sum(df * (1.0 - sig), axis=0, keepdims=True)
            dp_ref[0, rows, :] = ((em1 * x_all) * (sq * (1.0 + q * (1.0 - sq)))).astype(_MXU_DTYPE)
            dp_ref[1, rows, :] = (df * ((1.0 - lb) * sig * (1.0 - sig))).astype(_MXU_DTYPE)
            dp_ref[2, rows, :] = cat(dv_p).astype(_MXU_DTYPE)
            dp_ref[3, rows, :] = (dyh * (n_all * gwv) * (sg * (1.0 + hg * (1.0 - sg)))).astype(_MXU_DTYPE)

    rev = lambda i: nt - 1 - i
    pair = lambda s: pl.BlockSpec((2, tm, D_MODEL), lambda i: (s, rev(i), 0))
    row = pl.BlockSpec((tm, D_MODEL), lambda i: (rev(i), 0))
    vec = _full((1, D_MODEL))
    return pl.pallas_call(
        body, name="hgrn_bwd", grid=(nt,),
        in_specs=[row, pl.BlockSpec((D_MODEL, D_MODEL), lambda i: (1, 0)), pair(1), pair(2), row,
                  pl.BlockSpec((nc, HEADS, HEAD_D, HEAD_D), lambda i: (rev(i), 0, 0, 0)),
                  _full((2, D_MODEL)), vec, pl.BlockSpec(memory_space=pl.ANY)],
        out_specs=[pl.BlockSpec((4, tm, D_MODEL), lambda i: (0, rev(i), 0)), vec, vec],
        out_shape=[jax.ShapeDtypeStruct((N_SEG, t, D_MODEL), _MXU_DTYPE),
                   jax.ShapeDtypeStruct((1, D_MODEL), F32), jax.ShapeDtypeStruct((1, D_MODEL), F32)],
        scratch_shapes=[pltpu.VMEM((tm, D_MODEL), F32), pltpu.VMEM((HEADS, HEAD_D, HEAD_D), F32),
                        pltpu.VMEM((HEADS, HEAD_D, HEAD_D), F32)],
        input_output_aliases={8: 0},
        compiler_params=_seq_params(),
    )(dy, wo, p, p, o, states, logits, gw, dp)


def _dp_unit(m):
    return ((m // 2 + 2) % N_SEG) * 2 + m % 2


def _dw_in(ut, dp, tt):
    t = ut.shape[1]
    hr = D_MODEL // 2

    def body(ut_ref, dp_ref, g_ref):
        @pl.when(pl.program_id(1) == 0)
        def _():
            g_ref[...] = jnp.zeros_like(g_ref)

        r = jnp.dot(ut_ref[...], dp_ref[...], preferred_element_type=F32)
        for h in range(2):
            for half in range(2):
                g_ref[h, half] += r[h * hr:(h + 1) * hr, half * UNIT:(half + 1) * UNIT]

    return pl.pallas_call(
        body, name="dw_in", grid=(N_SEG, t // tt),
        in_specs=[pl.BlockSpec((D_MODEL, tt), lambda s, k: (0, k)),
                  pl.BlockSpec((None, tt, D_MODEL), lambda s, k: (s, k, 0))],
        out_specs=pl.BlockSpec((2, 2, hr, UNIT), lambda s, k: (0, (s + 2) % N_SEG, 0, 0)),
        out_shape=jax.ShapeDtypeStruct((2, N_UNIT, hr, UNIT), F32),
        compiler_params=pltpu.CompilerParams(dimension_semantics=("arbitrary", "arbitrary")),
    )(ut, dp)


def _dx(dp, wg, x, dout, wpre, tm, parts, scatter):
    t = x.shape[0]
    n = len(parts)
    nt = t // tm

    def body(*refs):
        dp_ref, w_ref, x_ref, dout_ref, wpre_ref = refs[:5]
        ins = refs[5:5 + n]
        gx_ref, dw_ref = refs[5 + n:7 + n]
        outs = refs[7 + n:7 + 2 * n]
        send_sems, recv_sems = refs[7 + 2 * n:]
        i = pl.program_id(0)

        @pl.when(i == 0)
        def _():
            dw_ref[...] = jnp.zeros_like(dw_ref)
            for cp in _chip_copies(ins, outs, send_sems, recv_sems, scatter):
                cp.start()

        du = None
        for m in range(N_UNIT):
            part = _dot_nt(dp_ref[m // 2, :, (m % 2) * UNIT:(m % 2 + 1) * UNIT], w_ref[_dp_unit(m)])
            du = part if du is None else du + part
        xv = x_ref[...]
        rstd = lax.rsqrt(jnp.mean(xv * xv, axis=-1, keepdims=True) + EPS)
        xn = xv * rstd
        dw_ref[...] += jnp.sum(du * xn, axis=0, keepdims=True)
        dn = du * wpre_ref[...]
        gx_ref[...] = dout_ref[...] + rstd * (dn - xn * jnp.mean(dn * xn, axis=-1, keepdims=True))

        @pl.when(i == nt - 1)
        def _():
            copies = _chip_copies(ins, outs, send_sems, recv_sems, scatter)
            for cp in copies:
                cp.wait_recv()
            for cp in copies:
                cp.wait_send()

    row = pl.BlockSpec((tm, D_MODEL), lambda i: (i, 0))
    any_spec = pl.BlockSpec(memory_space=pl.ANY)
    res = pl.pallas_call(
        body, name="dx", grid=(nt,),
        in_specs=[pl.BlockSpec((N_SEG, tm, D_MODEL), lambda i: (0, i, 0)), _resident((N_UNIT, D_MODEL, UNIT)),
                  row, row, _full((1, D_MODEL))] + [any_spec] * n,
        out_specs=[row, _full((1, D_MODEL))] + [any_spec] * n,
        out_shape=[jax.ShapeDtypeStruct((t, D_MODEL), F32), jax.ShapeDtypeStruct((1, D_MODEL), F32)]
        + _chip_exchange_shapes(parts, scatter),
        scratch_shapes=[pltpu.SemaphoreType.DMA((max(3 * n, 1),)), pltpu.SemaphoreType.DMA((max(3 * n, 1),))],
        compiler_params=_seq_params(),
    )(dp, wg, x, dout, wpre, *parts)
    return res[0], res[1], res[2:]


CHUNK_BYTES = 256 * 1024
MAX_CHUNKS = 16


def _n_chunks(shape, dtype, rows):
    nbytes = jnp.dtype(dtype).itemsize
    for d in shape:
        nbytes *= d
    k = max(1, min(MAX_CHUNKS, rows, nbytes // CHUNK_BYTES))
    while rows % k:
        k -= 1
    return k


def _chip_exchange_shapes(arrays, scatter):
    return [jax.ShapeDtypeStruct((3,) + tuple(a.shape[1:] if sc else a.shape), a.dtype)
            for a, sc in zip(arrays, scatter)]


def _chip_copies(ins, outs, send_sems, recv_sems, scatter):
    if not ins:
        return []
    x, y, c = lax.axis_index("x"), lax.axis_index("y"), lax.axis_index("c")
    peers = [(1 - x, y), (x, 1 - y), (1 - x, 1 - y)]
    copies = []
    for a in range(len(ins)):
        for j, (px, py) in enumerate(peers):
            copies.append(pltpu.make_async_remote_copy(
                src_ref=ins[a].at[2 * px + py] if scatter[a] else ins[a], dst_ref=outs[a].at[j],
                send_sem=send_sems.at[a * 3 + j], recv_sem=recv_sems.at[a * 3 + j],
                device_id=(px, py, c), device_id_type=MESH))
    return copies


def _core_swap(arrays, name, halved):
    n = len(arrays)
    shapes = [a.shape[1:] if halved else a.shape for a in arrays]
    ks = [_n_chunks(s, a.dtype, s[0]) for s, a in zip(shapes, arrays)]
    offs = [sum(ks[:a]) for a in range(n)]

    def body(*refs):
        ins, outs = refs[:n], refs[n:2 * n]
        send_sems, recv_sems = refs[2 * n:]
        x, y, c = lax.axis_index("x"), lax.axis_index("y"), lax.axis_index("c")

        def copy(a, j):
            step = shapes[a][0] // ks[a]
            rows = pl.ds(j * step, step)
            return pltpu.make_async_remote_copy(
                src_ref=ins[a].at[1 - c, rows] if halved else ins[a].at[rows], dst_ref=outs[a].at[rows],
                send_sem=send_sems.at[offs[a] + j], recv_sem=recv_sems.at[offs[a] + j],
                device_id=(x, y, 1 - c), device_id_type=MESH)

        pieces = [(a, j) for a in range(n) for j in range(ks[a])]
        for a, j in pieces:
            copy(a, j).start()
        for a, j in pieces:
            copy(a, j).wait_recv()
        for a, j in pieces:
            copy(a, j).wait_send()

    any_spec = pl.BlockSpec(memory_space=pl.ANY)
    return pl.pallas_call(
        body, name=name,
        in_specs=[any_spec] * n, out_specs=[any_spec] * n,
        out_shape=[jax.ShapeDtypeStruct(s, a.dtype) for s, a in zip(shapes, arrays)],
        scratch_shapes=[pltpu.SemaphoreType.DMA((sum(ks),)), pltpu.SemaphoreType.DMA((sum(ks),))],
    )(*arrays)


def _finish_exchange(arrays, small):
    n = len(arrays)
    ks = [_n_chunks(a.shape, a.dtype, a.shape[0]) for a in arrays]
    offs = [sum(ks[:a]) for a in range(n)]
    total = sum(ks)

    def body(*refs):
        ins, small_ref = refs[:n], refs[n]
        outs, gathered = refs[n + 1:2 * n + 1], refs[2 * n + 1]
        send_sems, recv_sems = refs[2 * n + 2:]
        x, y, c = lax.axis_index("x"), lax.axis_index("y"), lax.axis_index("c")
        copies = []
        for a in range(n):
            step = arrays[a].shape[0] // ks[a]
            for j in range(ks[a]):
                rows = pl.ds(j * step, step)
                copies.append(pltpu.make_async_remote_copy(
                    src_ref=ins[a].at[rows], dst_ref=outs[a].at[rows],
                    send_sem=send_sems.at[offs[a] + j], recv_sem=recv_sems.at[offs[a] + j],
                    device_id=(x, y, 1 - c), device_id_type=MESH))
        for slot in range(7):
            dx, dy, dc = (slot + 1) % 2, ((slot + 1) // 2) % 2, (slot + 1) // 4
            copies.append(pltpu.make_async_remote_copy(
                src_ref=small_ref, dst_ref=gathered.at[slot],
                send_sem=send_sems.at[total + slot], recv_sem=recv_sems.at[total + slot],
                device_id=(x + dx - 2 * x * dx, y + dy - 2 * y * dy, c + dc - 2 * c * dc), device_id_type=MESH))
        for cp in copies:
            cp.start()
        for cp in copies:
            cp.wait_recv()
        for cp in copies:
            cp.wait_send()

    any_spec = pl.BlockSpec(memory_space=pl.ANY)
    res = pl.pallas_call(
        body, name="reduce_cores",
        in_specs=[any_spec] * (n + 1), out_specs=[any_spec] * (n + 1),
        out_shape=[jax.ShapeDtypeStruct(a.shape, a.dtype) for a in arrays]
        + [jax.ShapeDtypeStruct((7,) + small.shape, small.dtype)],
        scratch_shapes=[pltpu.SemaphoreType.DMA((total + 7,)), pltpu.SemaphoreType.DMA((total + 7,))],
    )(*arrays, small)
    return res[:n], res[n]


def _sum_own_half(g, b, name, out_dtype, tr=512):
    _, rows, cols = g.shape
    tr = min(tr, rows)
    core = lax.axis_index("c").reshape(1)

    def body(c_ref, g_ref, b_ref, o_ref):
        del c_ref
        o_ref[...] = (g_ref[...] + b_ref[...]).astype(out_dtype)

    return pl.pallas_call(
        body, name=name,
        grid_spec=pltpu.PrefetchScalarGridSpec(
            num_scalar_prefetch=1, grid=(rows // tr,),
            in_specs=[pl.BlockSpec((None, tr, cols), lambda i, c_ref: (c_ref[0], i, 0)),
                      pl.BlockSpec((tr, cols), lambda i, c_ref: (i, 0))],
            out_specs=pl.BlockSpec((tr, cols), lambda i, c_ref: (i, 0))),
        out_shape=jax.ShapeDtypeStruct((rows, cols), out_dtype),
        compiler_params=pltpu.CompilerParams(dimension_semantics=("parallel",)),
    )(core, g, b)


def _sum_chips(own, r, slabbed, name, tr=512):
    _, rows, cols = r.shape
    tr = min(tr, rows)
    chip = (2 * lax.axis_index("x") + lax.axis_index("y")).reshape(1)

    def body(c_ref, own_ref, r_ref, o_ref):
        del c_ref
        f = lambda val: val.astype(F32)
        o_ref[...] = (f(own_ref[...]) + f(r_ref[0])) + (f(r_ref[1]) + f(r_ref[2]))

    if slabbed:
        own_spec = pl.BlockSpec((None, tr, cols), lambda i, c_ref: (c_ref[0], i, 0))
    else:
        own_spec = pl.BlockSpec((tr, cols), lambda i, c_ref: (i, 0))
    return pl.pallas_call(
        body, name=name,
        grid_spec=pltpu.PrefetchScalarGridSpec(
            num_scalar_prefetch=1, grid=(rows // tr,),
            in_specs=[own_spec, pl.BlockSpec((3, tr, cols), lambda i, c_ref: (0, i, 0))],
            out_specs=pl.BlockSpec((tr, cols), lambda i, c_ref: (i, 0))),
        out_shape=jax.ShapeDtypeStruct((rows, cols), F32),
        compiler_params=pltpu.CompilerParams(dimension_semantics=("parallel",)),
    )(chip, own, r)


def _adamw_big(own, other, w, m, v, name, grid, g_block, g_index, w_block, w_index):
    core = lax.axis_index("c").reshape(1)
    half_axis = len(grid) - 1

    def body(c_ref, own_ref, oth_ref, w_ref, m_ref, v_ref, g_ref, d_ref, nm_ref, nv_ref):
        mine = pl.program_id(half_axis) == c_ref[0]
        gv = jnp.where(mine, own_ref[...], oth_ref[...])
        g_ref[...] = gv
        d_ref[...], nm_ref[...], nv_ref[...] = _adamw(w_ref[...], gv, m_ref[...], v_ref[...])

    g_spec = pl.BlockSpec(g_block, lambda *a: g_index(*a[:half_axis]))
    w_spec = pl.BlockSpec(w_block, lambda *a: w_index(*a[:-1]))
    shp = jax.ShapeDtypeStruct(w.shape, F32)
    return pl.pallas_call(
        body, name=name,
        grid_spec=pltpu.PrefetchScalarGridSpec(
            num_scalar_prefetch=1, grid=grid,
            in_specs=[g_spec, g_spec, w_spec, w_spec, w_spec], out_specs=[w_spec] * 4),
        out_shape=[shp] * 4,
        compiler_params=pltpu.CompilerParams(dimension_semantics=("parallel",) * len(grid)),
    )(core, own, other, w, m, v)


def _small_finish(own, others, lam, logits):
    def body(a_ref, b_ref, lam_ref, lg_ref, o_ref):
        g = (((a_ref[...] + b_ref[0]) + (b_ref[1] + b_ref[2]))
             + ((b_ref[3] + b_ref[4]) + (b_ref[5] + b_ref[6])))
        o_ref[...] = g
        o_ref[8:9, :] = g[8:9, :] * (-_sigmoid(-lam_ref[...]))
        lb = 1.0 / (1.0 + jnp.exp(lg_ref[1:2, :] - lg_ref[0:1, :]))
        d0 = g[9:10, :] * (lb * (1.0 - lb))
        o_ref[9:10, :] = d0
        o_ref[10:11, :] = -d0

    return pl.pallas_call(
        body, name="small_finish",
        in_specs=[_full((SMALL_ROWS, D_MODEL)), _full((7, SMALL_ROWS, D_MODEL)), _full((1, D_MODEL)),
                  _full((2, D_MODEL))],
        out_specs=_full((SMALL_ROWS, D_MODEL)),
        out_shape=jax.ShapeDtypeStruct((SMALL_ROWS, D_MODEL), F32),
    )(own, others, lam, logits)


def _adamw_small(ws, gs, ms, vs):
    n = len(ws)

    def body(*refs):
        w_r, g_r, m_r, v_r = refs[:n], refs[n:2 * n], refs[2 * n:3 * n], refs[3 * n:4 * n]
        d_o, m_o, v_o = refs[4 * n:5 * n], refs[5 * n:6 * n], refs[6 * n:7 * n]
        for j in range(n):
            d_o[j][...], m_o[j][...], v_o[j][...] = _adamw(w_r[j][...], g_r[j][...], m_r[j][...], v_r[j][...])

    specs = [_full(w.shape) for w in ws]
    shapes = [jax.ShapeDtypeStruct(w.shape, F32) for w in ws]
    outs = pl.pallas_call(
        body, name="adamw_small",
        in_specs=specs * 4, out_specs=specs * 3, out_shape=shapes * 3,
    )(*ws, *gs, *ms, *vs)
    return outs[:n], outs[n:2 * n], outs[2 * n:]


def _local_step(x, tgt, ut, p, pre_w, wg, cw, cb, wa, wx, ba, bx, lam, logits, gw, wo, post_w, tm, tmm,
                prepare=None):
    h, y2 = _lru_fwd(p, wa, wx, ba, bx, lam, cw, cb, tm)
    y2, o, states = _hgrn_fwd(p, logits, gw, y2, tm)
    dout, dy, sq, d_post = _out_fused(y2, wo, x, tgt, post_w, tmm)
    g_out = _dw_out(y2, dy, min(2048, x.shape[0]))
    dp, d_wa, d_wx, d_ba, d_bx, d_sp, d_cw, d_cb = _lru_bwd(dy, wo, p, h, wa, wx, ba, bx, lam, cw, cb, tm)
    dp, d_gw, d_lb = _hgrn_bwd(dy, wo, p, o, states, logits, gw, dp, tm)
    g_in = _dw_in(ut, dp, min(2048, x.shape[0]))
    parts, scatter = prepare(g_in, g_out, d_wa, d_wx) if prepare is not None else ([], [])
    grad_x, d_pre, received = _dx(dp, wg, x, dout, pre_w, tm, parts, scatter)
    small = jnp.concatenate([d_pre, d_cw, d_cb, d_ba, d_bx, d_sp, d_lb,
                             jnp.zeros((1, D_MODEL), F32), d_gw, d_post, sq,
                             jnp.zeros((SMALL_ROWS - 14, D_MODEL), F32)], axis=0)
    return grad_x, g_in, g_out, d_wa, d_wx, small, parts, received


def kernel(x, pre_norm_w, w_in, conv_w, conv_b, lru_w_a, lru_b_a, lru_w_x, lru_b_x, lru_lambda, hgrn_lb_logits, hgrn_gnorm_w, w_out, post_norm_w, loss_target, m_pre_norm_w, m_w_in, m_conv_w, m_conv_b, m_lru_w_a, m_lru_b_a, m_lru_w_x, m_lru_b_x, m_lru_lambda, m_hgrn_lb_logits, m_hgrn_gnorm_w, m_w_out, m_post_norm_w, v_pre_norm_w, v_w_in, v_conv_w, v_conv_b, v_lru_w_a, v_lru_b_a, v_lru_w_x, v_lru_b_x, v_lru_lambda, v_hgrn_lb_logits, v_hgrn_gnorm_w, v_w_out, v_post_norm_w):
    t = x.shape[1]
    tm = min(256, t)
    tmm = min(512, t)
    chip = 2 * lax.axis_index("x") + lax.axis_index("y")

    win_units = w_in[0].astype(_MXU_DTYPE).reshape(D_MODEL, 3, UNIT).transpose(1, 0, 2)
    small_w = jnp.concatenate([conv_w[0], lru_b_a[0], lru_b_x[0]], axis=1)
    ut, p, wg4, (wo4, wa4, wx4), sw4 = _in_proj_gather(
        x[0], pre_norm_w, win_units,
        [w_out[0].astype(_MXU_DTYPE).reshape(2, 256, D_MODEL), lru_w_a[0].astype(_MXU_DTYPE).reshape(2, 128, LRU_BW),
         lru_w_x[0].astype(_MXU_DTYPE).reshape(2, 128, LRU_BW)],
        small_w, min(1024, t))
    wg = wg4.reshape(N_UNIT, D_MODEL, UNIT)
    wo = wo4.reshape(2 * D_MODEL, D_MODEL)
    by_block = lambda w4: w4.reshape(N_CHIPS, LRU_BLOCKS, 64, LRU_BW).transpose(1, 0, 2, 3).reshape(
        LRU_BLOCKS, LRU_BW, LRU_BW)
    wa, wx = by_block(wa4), by_block(wx4)
    sw4 = sw4.reshape(N_CHIPS, 4, 384)
    cw = sw4[:, :, 0:256].transpose(1, 0, 2).reshape(4, D_MODEL)
    ba = sw4[:, :, 256:320].transpose(1, 0, 2).reshape(1, D_MODEL)
    bx = sw4[:, :, 320:384].transpose(1, 0, 2).reshape(1, D_MODEL)

    def lru_layout(g):
        g = g.reshape(2, 2, N_CHIPS, 64, LRU_BW).transpose(0, 2, 1, 3, 4)
        return g.reshape(2, N_CHIPS * 128, LRU_BW)

    def prepare(g_in, g_out, d_wa, d_wx):
        g_in = g_in.reshape(2, N_UNIT * 512, UNIT)
        g_out = g_out.reshape(2, N_CHIPS * 256, D_MODEL)
        g_wa, g_wx = lru_layout(d_wa), lru_layout(d_wx)
        b_in, b_out, b_wa, b_wx = _core_swap([g_in, g_out, g_wa, g_wx], "reduce_swap", halved=True)
        parts = [_sum_own_half(g_in, b_in, "presum_in", _WIRE_DTYPE).reshape(N_CHIPS, 3 * 512, UNIT),
                 _sum_own_half(g_out, b_out, "presum_out", _WIRE_DTYPE).reshape(N_CHIPS, 256, D_MODEL),
                 _sum_own_half(g_wa, b_wa, "presum_wa", _WIRE_DTYPE).reshape(N_CHIPS, 128, LRU_BW),
                 _sum_own_half(g_wx, b_wx, "presum_wx", _WIRE_DTYPE).reshape(N_CHIPS, 128, LRU_BW)]
        return parts, [True] * 4

    grad_x, _, _, _, _, small, (p_in, p_out, p_wa, p_wx), (r_in, r_out, r_wa, r_wx) = _local_step(
        x[0], loss_target[0], ut, p, pre_norm_w, wg, cw, conv_b, wa, wx, ba, bx, lru_lambda, hgrn_lb_logits,
        hgrn_gnorm_w, wo, post_norm_w, tm, tmm, prepare)
    s_in = _sum_chips(p_in, r_in, True, "sum_in")
    s_out = _sum_chips(p_out, r_out, True, "sum_out")
    s_wa = _sum_chips(p_wa, r_wa, True, "sum_wa")
    s_wx = _sum_chips(p_wx, r_wx, True, "sum_wx")
    (o_in, o_out, o_wa, o_wx), small_others = _finish_exchange([s_in, s_out, s_wa, s_wx], small)

    g_w_in, d_w_in, nm_w_in, nv_w_in = _adamw_big(
        s_in.reshape(3, 512, UNIT), o_in.reshape(3, 512, UNIT), w_in[0], m_w_in[0], v_w_in[0], "adamw_w_in", (3, 2),
        (None, 512, UNIT), lambda k: (k, 0, 0), (512, UNIT), lambda k, h: (h, k))
    g_w_out, d_w_out, nm_w_out, nv_w_out = _adamw_big(
        s_out, o_out, w_out[0], m_w_out[0], v_w_out[0], "adamw_w_out", (2,),
        (256, D_MODEL), lambda: (0, 0), (256, D_MODEL), lambda h: (h, 0))
    sq2 = lambda a: a.reshape(LRU_BW, LRU_BW)
    lru_specs = ((2,), (128, LRU_BW), lambda: (0, 0), (128, LRU_BW), lambda h: (h, 0))
    g_wa, d_wa2, nm_wa, nv_wa = _adamw_big(s_wa, o_wa, sq2(lru_w_a), sq2(m_lru_w_a), sq2(v_lru_w_a), "adamw_wa",
                                           *lru_specs)
    g_wx, d_wx2, nm_wx, nv_wx = _adamw_big(s_wx, o_wx, sq2(lru_w_x), sq2(m_lru_w_x), sq2(v_lru_w_x), "adamw_wx",
                                           *lru_specs)

    gs = _small_finish(small, small_others, lru_lambda, hgrn_lb_logits)
    loss = jnp.sum(gs[13]) * (0.5 / D_MODEL)
    g_pre = gs[0:1]
    g_cw = lax.dynamic_slice(gs[1:5], (0, chip * 256), (4, 256))
    g_cb = gs[5:6]
    g_ba = lax.dynamic_slice(gs[6].reshape(LRU_BLOCKS, N_CHIPS, 64), (0, chip, 0), (LRU_BLOCKS, 1, 64)).reshape(4, 64)
    g_bx = lax.dynamic_slice(gs[7].reshape(LRU_BLOCKS, N_CHIPS, 64), (0, chip, 0), (LRU_BLOCKS, 1, 64)).reshape(4, 64)
    g_lam = gs[8:9]
    g_lb = gs[9:11]
    g_gw = gs[11:12]
    g_post = gs[12:13]
    small_g = [g_pre, g_cw, g_cb, g_ba, g_bx, g_lam, g_lb, g_gw, g_post]
    two_d = lambda a: a.reshape(a.shape[-2:])
    small_w_list = [pre_norm_w, conv_w, conv_b, lru_b_a, lru_b_x, lru_lambda, hgrn_lb_logits, hgrn_gnorm_w, post_norm_w]
    small_m_list = [m_pre_norm_w, m_conv_w, m_conv_b, m_lru_b_a, m_lru_b_x, m_lru_lambda, m_hgrn_lb_logits,
                    m_hgrn_gnorm_w, m_post_norm_w]
    small_v_list = [v_pre_norm_w, v_conv_w, v_conv_b, v_lru_b_a, v_lru_b_x, v_lru_lambda, v_hgrn_lb_logits,
                    v_hgrn_gnorm_w, v_post_norm_w]
    sd, sm, sv = _adamw_small([two_d(a) for a in small_w_list], small_g,
                              [two_d(a) for a in small_m_list], [two_d(a) for a in small_v_list])

    def shaped(vals, refs):
        return [val.reshape(ref.shape) for val, ref in zip(vals, refs)]

    s_g = shaped(small_g, small_w_list)
    s_d = shaped(sd, small_w_list)
    s_m = shaped(sm, small_w_list)
    s_v = shaped(sv, small_w_list)

    def ordered(small, big_in, big_wa, big_wx, big_out):
        pre, cw_, cb_, ba_, bx_, lam_, lb_, gw_, post_ = small
        return [pre, big_in.reshape(w_in.shape), cw_, cb_, big_wa.reshape(lru_w_a.shape), ba_,
                big_wx.reshape(lru_w_x.shape), bx_, lam_, lb_, gw_, big_out.reshape(w_out.shape), post_]

    grads = ordered(s_g, g_w_in, g_wa, g_wx, g_w_out)
    deltas = ordered(s_d, d_w_in, d_wa2, d_wx2, d_w_out)
    new_m = ordered(s_m, nm_w_in, nm_wa, nm_wx, nm_w_out)
    new_v = ordered(s_v, nv_w_in, nv_wa, nv_wx, nv_w_out)
    return (loss, grad_x.reshape(x.shape), *grads, *deltas, *new_m, *new_v)
```

```python
import functools

import jax
import jax.numpy as jnp
from jax import lax
from jax.experimental import pallas as pl
from jax.experimental.pallas import tpu as pltpu

F32 = jnp.float32
BF16 = jnp.bfloat16
_MXU_DTYPE = jnp.bfloat16
_WIRE_DTYPE = jnp.bfloat16

D_MODEL = 1024
N_SEG = 6
UNIT = 512
N_UNIT = 12
LRU_BLOCKS = 4
LRU_BW = 256
LRU_C = 8.0
HEADS = 8
HEAD_D = 128
CHUNK = 64
EPS = 1e-6
N_CHIPS = 4
EXP_CLAMP = 80.0

ADAM_LR = 0.001
ADAM_B1 = 0.9
ADAM_B2 = 0.999
ADAM_EPS = 1e-08
ADAM_WD = 0.01
ADAM_STEP = 10

SMALL_ROWS = 16
FWD_PIECES = 4
MESH = pl.DeviceIdType.MESH


def _sigmoid(x):
    return 0.5 * jnp.tanh(0.5 * x) + 0.5


def _mx(x):
    return x.astype(_MXU_DTYPE)


def _dot(a, b):
    return jnp.dot(_mx(a), _mx(b), preferred_element_type=F32)


def _dot_nt(a, b):
    return lax.dot_general(_mx(a), _mx(b), (((1,), (1,)), ((), ())), preferred_element_type=F32)


def _dot_tn(a, b):
    return lax.dot_general(_mx(a), _mx(b), (((0,), (0,)), ((), ())), preferred_element_type=F32)


def _rows(shape):
    return lax.broadcasted_iota(jnp.int32, shape, 0)


def _cols(shape):
    return lax.broadcasted_iota(jnp.int32, shape, 1)


def _softplus_neg(lam):
    z = -lam
    e = jnp.exp(-jnp.abs(z))
    series = e * (1.0 - e * (0.5 - e * (1.0 / 3.0 - 0.25 * e)))
    return jnp.maximum(z, 0.0) + jnp.where(e < 1e-2, series, jnp.log(1.0 + e))


def _one_minus_sq(a, y):
    series = -y * (1.0 + y * (0.5 + y * (1.0 / 6.0)))
    return jnp.where(y > -0.01, series, 1.0 - a * a)


def _cumsum_rows(x, reverse=False):
    n = x.shape[0] // 8
    row8 = _rows((8, x.shape[1]))
    out = [None] * n
    carry = None
    for g in (reversed(range(n)) if reverse else range(n)):
        blk = x[8 * g:8 * g + 8]
        for s in (1, 2, 4):
            if reverse:
                blk = blk + jnp.where(row8 < 8 - s, pltpu.roll(blk, 8 - s, 0), 0.0)
            else:
                blk = blk + jnp.where(row8 >= s, pltpu.roll(blk, s, 0), 0.0)
        if carry is not None:
            blk = blk + carry
        carry = blk[0:1] if reverse else blk[7:8]
        out[g] = blk
    return jnp.concatenate(out, axis=0)


def _shift_down(x, halo, s):
    if s == 0:
        return x
    r = pltpu.roll(x, s, 0)
    top = jnp.where(_rows(halo.shape) < s, pltpu.roll(halo, s, 0), r[0:8])
    return jnp.concatenate([top, r[8:]], axis=0)


def _shift_up(x, head, s):
    if s == 0:
        return x
    n = x.shape[0]
    r = pltpu.roll(x, n - s, 0)
    bottom = jnp.where(_rows(head.shape) >= 8 - s, pltpu.roll(head, 8 - s, 0), r[n - 8:n])
    return jnp.concatenate([r[:n - 8], bottom], axis=0)


def _lru_gates(lx, halo, cw, cb, wa, wx, ba, bx, sp):
    xs = [_shift_down(lx, halo, 3 - k) for k in range(4)]
    xc = cb + cw[0:1] * xs[0] + cw[1:2] * xs[1] + cw[2:3] * xs[2] + cw[3:4] * xs[3]
    zr, zi = [], []
    for n in range(LRU_BLOCKS):
        xb = _mx(xc[:, n * LRU_BW:(n + 1) * LRU_BW])
        zr.append(jnp.dot(xb, wa[n], preferred_element_type=F32))
        zi.append(jnp.dot(xb, wx[n], preferred_element_type=F32))
    r = _sigmoid(jnp.concatenate(zr, axis=1) + ba)
    ig = _sigmoid(jnp.concatenate(zi, axis=1) + bx)
    la = (-LRU_C * sp) * r
    a = jnp.exp(la)
    om = _one_minus_sq(a, 2.0 * la)
    return xs, xc, r, ig, a, om


def _hgrn_prep(q, fr, lb):
    sig = _sigmoid(fr)
    f = lb + (1.0 - lb) * sig
    sq = _sigmoid(q)
    b = _cumsum_rows(jnp.log(f))
    bm = b[CHUNK // 2 - 1:CHUNK // 2]
    bc = b[CHUNK - 1:CHUNK]
    em1 = jnp.exp(jnp.minimum(b - bm, EXP_CLAMP))
    em2 = jnp.exp(jnp.minimum(bm - b, EXP_CLAMP))
    qm = _mx((q * sq) * em1)
    km = _mx((1.0 - f) * em2)
    return sig, f, sq, em1, em2, qm, km, jnp.exp(bm), jnp.exp(bc - bm), jnp.exp(bc)


def _pad_lanes(parts, width):
    have = sum(p.shape[1] for p in parts)
    pad = [jnp.zeros((parts[0].shape[0], width - have), parts[0].dtype)] if width > have else []
    return jnp.concatenate(list(parts) + pad, axis=1)


def _pad_rows(parts, height):
    have = sum(p.shape[0] for p in parts)
    pad = [jnp.zeros((height - have, parts[0].shape[1]), parts[0].dtype)] if height > have else []
    return jnp.concatenate(list(parts) + pad, axis=0)


def _adamw(w, g, m, v):
    m = ADAM_B1 * m + (1.0 - ADAM_B1) * g
    v = ADAM_B2 * v + (1.0 - ADAM_B2) * jnp.square(g)
    m_hat = m / (1.0 - ADAM_B1 ** ADAM_STEP)
    v_hat = v / (1.0 - ADAM_B2 ** ADAM_STEP)
    delta = -ADAM_LR * (m_hat / (jnp.sqrt(v_hat) + ADAM_EPS) + ADAM_WD * w)
    return delta, m, v


def _full(shape):
    nd = len(shape)
    return pl.BlockSpec(shape, lambda *_: (0,) * nd)


def _seq_params():
    return pltpu.CompilerParams(dimension_semantics=("arbitrary",))


def _resident(shape):
    nd = len(shape)
    return pl.BlockSpec(shape, lambda *_: (0,) * nd, pipeline_mode=pl.Buffered(1))


def _in_proj_gather(x, w, win_units, halved, small_w, tm):
    t = x.shape[0]
    nt = t // tm
    n_h = len(halved)
    n_split = 3 + n_h
    n_items = n_split + 1
    hu = D_MODEL // 2
    chip = 2 * lax.axis_index("x") + lax.axis_index("y")
    flips = (2, 1, 3)
    order = jnp.stack([3 * chip + u for u in range(3)]
                      + [3 * jnp.bitwise_xor(chip, flips[jj]) + u for u in range(3) for jj in range(3)])

    def body(order_ref, x_ref, w_ref, win_ref, *rest):
        del order_ref
        h_ins, small_ref = rest[:n_h], rest[n_h]
        ut_ref, p_ref, wg_ref = rest[n_h + 1:n_h + 4]
        h_outs, sw_ref = rest[n_h + 4:2 * n_h + 4], rest[2 * n_h + 4]
        u_s, wbuf, send, recv, fsend, frecv, osend, orecv, wsem = rest[2 * n_h + 5:]
        j, i = pl.program_id(0), pl.program_id(1)
        mx, my, c = lax.axis_index("x"), lax.axis_index("y"), lax.axis_index("c")
        me = 2 * mx + my
        peers = [(1 - mx, my), (mx, 1 - my), (1 - mx, 1 - my)]
        sibling = (mx, my, 1 - c)

        def remote(src, dst, ssem, rsem, dev):
            return pltpu.make_async_remote_copy(src_ref=src, dst_ref=dst, send_sem=ssem, recv_sem=rsem,
                                                device_id=dev, device_id_type=MESH)

        def mine(item):
            if item < 3:
                return win_ref.at[item, pl.ds(c * hu, hu)]
            return h_ins[item - 3].at[c] if item < n_split else small_ref

        def piece(item, q, h):
            if item < 3:
                return wg_ref.at[q, item, pl.ds(h * hu, hu)]
            return h_outs[item - 3].at[q, h] if item < n_split else sw_ref.at[q]

        def fetch(item, jj, q):
            px, py = peers[jj]
            return remote(mine(item), piece(item, q, c), send.at[item * 3 + jj], recv.at[item * 3 + jj], (px, py, c))

        def forward(item, jj, q, h):
            whole = piece(item, q, h)
            step = whole.shape[0] // FWD_PIECES
            copies = []
            for k in range(FWD_PIECES):
                part = whole.at[pl.ds(k * step, step)]
                sem = (item * 3 + jj) * FWD_PIECES + k
                copies.append(remote(part, part, fsend.at[sem], frecv.at[sem], sibling))
            return copies

        def own(item):
            if item < 3:
                src, dst = win_ref.at[item], wg_ref.at[me, item]
            elif item < n_split:
                src, dst = h_ins[item - 3], h_outs[item - 3].at[me]
            else:
                src, dst = small_ref, sw_ref.at[me]
            return remote(src, dst, osend.at[item], orecv.at[item], sibling)

        def unit_of(jn):
            u, jj = (jn - 3) // 3, (jn - 3) % 3
            q = jnp.bitwise_xor(me, jnp.where(jj == 0, flips[0], jnp.where(jj == 1, flips[1], flips[2])))
            return u, jj, q

        def unit_fetch(u, jj, q):
            half = wg_ref.at[q, u, pl.ds(c * hu, hu)]
            return remote(half, half, send.at[u * 3 + jj], recv.at[u * 3 + jj], sibling)

        def unit_forward(u, jj, q, h):
            step = hu // FWD_PIECES
            copies = []
            for k in range(FWD_PIECES):
                part = wg_ref.at[q, u, pl.ds(h * hu + k * step, step)]
                sem = (u * 3 + jj) * FWD_PIECES + k
                copies.append(remote(part, part, fsend.at[sem], frecv.at[sem], sibling))
            return copies

        for item in range(n_items):
            @pl.when(jnp.logical_and(j == min(2 * item, 6), i == 0))
            def _(item=item):
                for jj in range(3):
                    fetch(item, jj, me).start()

        def load_unit(src, slot):
            return pltpu.make_async_copy(src, wbuf.at[slot], wsem.at[slot])

        @pl.when(jnp.logical_and(j == 0, i == 0))
        def _():
            for item in range(n_items):
                own(item).start()
            load_unit(win_ref.at[0], 0).start()

        @pl.when(j == 0)
        def _():
            xv = x_ref[...]
            rstd = lax.rsqrt(jnp.mean(xv * xv, axis=-1, keepdims=True) + EPS)
            u = xv * rstd * w_ref[...]
            ut_ref[...] = u.T.astype(_MXU_DTYPE)
            u_s[pl.ds(pl.multiple_of(i * tm, tm), tm), :] = _mx(u)

        slot = j % 2

        @pl.when(i == 0)
        def _():
            load_unit(win_ref.at[0], slot).wait()

        p_ref[...] = jnp.dot(u_s[pl.ds(pl.multiple_of(i * tm, tm), tm), :], wbuf[slot], preferred_element_type=F32)

        @pl.when(jnp.logical_and(i == nt - 1, jnp.logical_and(j >= 1, j < N_UNIT - 2)))
        def _():
            u, jj, q = unit_of(j + 2)
            unit_fetch(u, jj, q).wait_recv()
            for cp in unit_forward(u, jj, q, c):
                cp.start()

        @pl.when(jnp.logical_and(i == nt - 1, j < 2))
        def _():
            load_unit(win_ref.at[j + 1], 1 - slot).start()

        @pl.when(jnp.logical_and(i == nt - 1, jnp.logical_and(j >= 2, j < N_UNIT - 1)))
        def _():
            u, jj, q = unit_of(j + 1)
            for cp in unit_forward(u, jj, q, 1 - c):
                cp.wait_recv()
            load_unit(wg_ref.at[q, u], 1 - slot).start()

        @pl.when(jnp.logical_and(i == nt - 1, j == N_UNIT - 1))
        def _():
            qs = [2 * px + py for px, py in peers]
            for item in range(3, n_items):
                for jj in range(3):
                    fetch(item, jj, qs[jj]).wait_recv()
                    if item < n_split:
                        for cp in forward(item, jj, qs[jj], c):
                            cp.start()
            for item in range(3, n_split):
                for jj in range(3):
                    for cp in forward(item, jj, qs[jj], 1 - c):
                        cp.wait_recv()
            for item in range(n_items):
                own(item).wait_recv()
            for item in range(n_items):
                for jj in range(3):
                    fetch(item, jj, me).wait_send()
                    if item < n_split:
                        for cp in forward(item, jj, qs[jj], c):
                            cp.wait_send()
                own(item).wait_send()

    any_spec = pl.BlockSpec(memory_space=pl.ANY)
    first_pass = lambda j, i: jnp.where(j == 0, i, nt - 1)
    res = pl.pallas_call(
        body, name="in_proj",
        grid_spec=pltpu.PrefetchScalarGridSpec(
            num_scalar_prefetch=1, grid=(N_UNIT, nt),
            in_specs=[pl.BlockSpec((tm, D_MODEL), lambda j, i, o: (first_pass(j, i), 0)),
                      pl.BlockSpec((1, D_MODEL), lambda j, i, o: (0, 0))] + [any_spec] * (n_h + 2),
            out_specs=[pl.BlockSpec((D_MODEL, tm), lambda j, i, o: (0, first_pass(j, i))),
                       pl.BlockSpec((None, tm, UNIT), lambda j, i, o: (o[j] // 2, i, o[j] % 2))]
            + [any_spec] * (n_h + 2),
            scratch_shapes=[pltpu.VMEM((t, D_MODEL), _MXU_DTYPE), pltpu.VMEM((2, D_MODEL, UNIT), _MXU_DTYPE),
                            pltpu.SemaphoreType.DMA((3 * n_items,)), pltpu.SemaphoreType.DMA((3 * n_items,)),
                            pltpu.SemaphoreType.DMA((3 * n_split * FWD_PIECES,)),
                            pltpu.SemaphoreType.DMA((3 * n_split * FWD_PIECES,)),
                            pltpu.SemaphoreType.DMA((n_items,)), pltpu.SemaphoreType.DMA((n_items,)),
                            pltpu.SemaphoreType.DMA((2,))]),
        out_shape=[jax.ShapeDtypeStruct((D_MODEL, t), _MXU_DTYPE), jax.ShapeDtypeStruct((N_SEG, t, D_MODEL), F32),
                   jax.ShapeDtypeStruct((N_CHIPS,) + win_units.shape, win_units.dtype)]
        + [jax.ShapeDtypeStruct((N_CHIPS,) + a.shape, a.dtype) for a in halved]
        + [jax.ShapeDtypeStruct((N_CHIPS,) + small_w.shape, small_w.dtype)],
        compiler_params=pltpu.CompilerParams(dimension_semantics=("arbitrary", "arbitrary")),
    )(order, x, w, win_units, *halved, small_w)
    return res[0], res[1], res[2], res[3:3 + n_h], res[3 + n_h]


def _lru_fwd(p, wa, wx, ba, bx, lam, cw, cb, tm):
    t = p.shape[1]
    ng = tm // 8

    def body(lx_ref, halo_ref, lg_ref, wa_ref, wx_ref, ba_ref, bx_ref, lam_ref, cw_ref, cb_ref,
             h_ref, y_ref, a_s, u_s, hc_s):
        i = pl.program_id(0)

        @pl.when(i == 0)
        def _():
            hc_s[...] = jnp.zeros_like(hc_s)

        sp = _softplus_neg(lam_ref[...])
        halo = jnp.where(i == 0, 0.0, halo_ref[...])
        _, xc, _, ig, a, om = _lru_gates(lx_ref[...], halo, cw_ref[...], cb_ref[...], wa_ref[...],
                                         wx_ref[...], ba_ref[...], bx_ref[...], sp)
        a_s[...] = a
        u_s[...] = jnp.sqrt(om) * (ig * xc)
        row8 = _rows((8, D_MODEL))

        def group(g, hc):
            rows = pl.ds(pl.multiple_of(g * 8, 8), 8)
            av = a_s[rows, :]
            uv = u_s[rows, :]
            for s in (1, 2, 4):
                a_sh = jnp.where(row8 >= s, pltpu.roll(av, s, 0), 1.0)
                u_sh = jnp.where(row8 >= s, pltpu.roll(uv, s, 0), 0.0)
                uv = av * u_sh + uv
                av = av * a_sh
            hh = av * hc + uv
            h_ref[rows, :] = hh
            return jnp.broadcast_to(hh[7:8, :], (8, D_MODEL))

        hc_s[...] = lax.fori_loop(0, ng, group, hc_s[...])
        lg = lg_ref[...]
        y_ref[...] = (h_ref[...] * (lg * _sigmoid(lg))).astype(_MXU_DTYPE)

    seg = lambda s: pl.BlockSpec((None, tm, D_MODEL), lambda i: (s, i, 0))
    return pl.pallas_call(
        body, name="lru_fwd", grid=(t // tm,),
        in_specs=[seg(0),
                  pl.BlockSpec((None, 8, D_MODEL), lambda i: (0, jnp.maximum(i * (tm // 8) - 1, 0), 0)),
                  seg(1),
                  _full((LRU_BLOCKS, LRU_BW, LRU_BW)), _full((LRU_BLOCKS, LRU_BW, LRU_BW)),
                  _full((1, D_MODEL)), _full((1, D_MODEL)), _full((1, D_MODEL)),
                  _full((4, D_MODEL)), _full((1, D_MODEL))],
        out_specs=[pl.BlockSpec((tm, D_MODEL), lambda i: (i, 0)),
                   pl.BlockSpec((None, tm, D_MODEL), lambda i: (0, i, 0))],
        out_shape=[jax.ShapeDtypeStruct((t, D_MODEL), F32),
                   jax.ShapeDtypeStruct((2, t, D_MODEL), _MXU_DTYPE)],
        scratch_shapes=[pltpu.VMEM((tm, D_MODEL), F32), pltpu.VMEM((tm, D_MODEL), F32),
                        pltpu.VMEM((8, D_MODEL), F32)],
        compiler_params=_seq_params(),
    )(p, p, p, wa, wx, ba, bx, lam, cw, cb)


def _hgrn_fwd(p, logits, gw, y2, tm):
    t = p.shape[1]
    nc = tm // CHUNK

    def body(qf_ref, vh_ref, lg_ref, gw_ref, y_in, y_ref, o_ref, st_ref, state):
        del y_in
        i = pl.program_id(0)

        @pl.when(i == 0)
        def _():
            state[...] = jnp.zeros_like(state)

        l0 = lg_ref[0:1, :]
        l1 = lg_ref[1:2, :]
        lb = 1.0 / (1.0 + jnp.exp(l1 - l0))
        gwv = gw_ref[...]
        causal = _rows((CHUNK, CHUNK)) >= _cols((CHUNK, CHUNK))

        for c in range(nc):
            rows = pl.ds(c * CHUNK, CHUNK)
            v = vh_ref[0, rows, :]
            hg = vh_ref[1, rows, :]
            _, _, _, _, _, qm, km, ebm, ecm, ebc = _hgrn_prep(qf_ref[0, rows, :], qf_ref[1, rows, :], lb)
            ke = _mx(km.astype(F32) * ecm)
            gate = gwv * (hg * _sigmoid(hg))
            o_parts, y_parts = [], []
            for h in range(HEADS):
                ln = slice(h * HEAD_D, (h + 1) * HEAD_D)
                s0 = state[h]
                st_ref[c, h] = s0
                vt = _mx(v[:, ln].T)
                att = _mx(jnp.where(causal, _dot_nt(qm[:, ln], km[:, ln]), 0.0))
                o = _dot_nt(_pad_lanes([qm[:, ln], att], 2 * HEAD_D),
                            _pad_lanes([_mx(s0 * ebm[:, ln]), vt], 2 * HEAD_D))
                state[h] = s0 * ebc[:, ln] + jnp.dot(vt, ke[:, ln], preferred_element_type=F32)
                rstd = lax.rsqrt(jnp.mean(o * o, axis=-1, keepdims=True) + EPS)
                o_parts.append(o)
                y_parts.append((o * rstd * gate[:, ln]).astype(_MXU_DTYPE))
            o_ref[rows, :] = jnp.concatenate(o_parts, axis=1)
            y_ref[rows, :] = jnp.concatenate(y_parts, axis=1)

    pair = lambda s: pl.BlockSpec((2, tm, D_MODEL), lambda i: (s, i, 0))
    return pl.pallas_call(
        body, name="hgrn_fwd", grid=(t // tm,),
        in_specs=[pair(1), pair(2), _full((2, D_MODEL)), _full((1, D_MODEL)),
                  pl.BlockSpec(memory_space=pl.ANY)],
        out_specs=[pl.BlockSpec((None, tm, D_MODEL), lambda i: (1, i, 0)),
                   pl.BlockSpec((tm, D_MODEL), lambda i: (i, 0)),
                   pl.BlockSpec((nc, HEADS, HEAD_D, HEAD_D), lambda i: (i, 0, 0, 0))],
        out_shape=[jax.ShapeDtypeStruct((2, t, D_MODEL), _MXU_DTYPE),
                   jax.ShapeDtypeStruct((t, D_MODEL), F32),
                   jax.ShapeDtypeStruct((t // CHUNK, HEADS, HEAD_D, HEAD_D), F32)],
        scratch_shapes=[pltpu.VMEM((HEADS, HEAD_D, HEAD_D), F32)],
        input_output_aliases={4: 0},
        compiler_params=_seq_params(),
    )(p, p, logits, gw, y2)


def _out_fused(y2, wo, x, tgt, wpost, tm):
    t = x.shape[0]
    inv_d = 1.0 / D_MODEL

    def body(y_ref, wo_ref, x_ref, t_ref, w_ref, dout_ref, dy_ref, sq_ref, dw_ref):
        i = pl.program_id(0)

        @pl.when(i == 0)
        def _():
            sq_ref[...] = jnp.zeros_like(sq_ref)
            dw_ref[...] = jnp.zeros_like(dw_ref)

        y = (jnp.dot(y_ref[0], wo_ref[0:D_MODEL, :], preferred_element_type=F32)
             + jnp.dot(y_ref[1], wo_ref[D_MODEL:2 * D_MODEL, :], preferred_element_type=F32))
        w = w_ref[...]
        rstd = lax.rsqrt(jnp.mean(y * y, axis=-1, keepdims=True) + EPS)
        n = y * rstd
        err = (x_ref[...] + n * w) - t_ref[...]
        sq_ref[...] += jnp.sum(err * err, axis=0, keepdims=True)
        dout = err * inv_d
        dout_ref[...] = dout
        dw_ref[...] += jnp.sum(dout * n, axis=0, keepdims=True)
        dn = dout * w
        dy_ref[...] = (rstd * (dn - n * jnp.mean(dn * n, axis=-1, keepdims=True))).astype(_MXU_DTYPE)

    row = pl.BlockSpec((tm, D_MODEL), lambda i: (i, 0))
    return pl.pallas_call(
        body, name="out_fused", grid=(t // tm,),
        in_specs=[pl.BlockSpec((2, tm, D_MODEL), lambda i: (0, i, 0)), _full((2 * D_MODEL, D_MODEL)),
                  row, row, _full((1, D_MODEL))],
        out_specs=[row, row, _full((1, D_MODEL)), _full((1, D_MODEL))],
        out_shape=[jax.ShapeDtypeStruct((t, D_MODEL), F32), jax.ShapeDtypeStruct((t, D_MODEL), _MXU_DTYPE),
                   jax.ShapeDtypeStruct((1, D_MODEL), F32), jax.ShapeDtypeStruct((1, D_MODEL), F32)],
        compiler_params=_seq_params(),
    )(y2, wo, x, tgt, wpost)


def _dw_out(y2, dy, tt):
    t = dy.shape[0]
    hr = UNIT // 2

    def body(y_ref, dy_ref, g_ref):
        @pl.when(pl.program_id(1) == 0)
        def _():
            g_ref[...] = jnp.zeros_like(g_ref)

        r = _dot_tn(y_ref[...], dy_ref[...])
        g_ref[0] += r[0:hr]
        g_ref[1] += r[hr:UNIT]

    return pl.pallas_call(
        body, name="dw_out", grid=(N_CHIPS, t // tt),
        in_specs=[pl.BlockSpec((None, tt, UNIT), lambda c, k: (c // 2, k, c % 2)),
                  pl.BlockSpec((tt, D_MODEL), lambda c, k: (k, 0))],
        out_specs=pl.BlockSpec((2, None, hr, D_MODEL), lambda c, k: (0, c, 0, 0)),
        out_shape=jax.ShapeDtypeStruct((2, N_CHIPS, hr, D_MODEL), F32),
        compiler_params=pltpu.CompilerParams(dimension_semantics=("arbitrary", "arbitrary")),
    )(y2, dy)


def _lru_bwd(dy, wo, p, h, wa, wx, ba, bx, lam, cw, cb, tm):
    t = dy.shape[0]
    nt = t // tm
    ng = tm // 8

    def body(dy_ref, wo_ref, lx_ref, halo_ref, lg_ref, h_ref, hhalo_ref, wa_ref, wx_ref, ba_ref, bx_ref,
             lam_ref, cw_ref, cb_ref,
             dp_ref, dwa_ref, dwx_ref, dba_ref, dbx_ref, dsp_ref, dcw_ref, dcb_ref,
             c_s, in_s, dh_s, dhc_s, afirst_s, dxc_s):
        i = pl.program_id(0)
        first = i == nt - 1

        @pl.when(i == 0)
        def _():
            for ref in (dwa_ref, dwx_ref, dba_ref, dbx_ref, dsp_ref, dcw_ref, dcb_ref, dhc_s, afirst_s, dxc_s):
                ref[...] = jnp.zeros_like(ref)

        sp = _softplus_neg(lam_ref[...])
        cwv = cw_ref[...]
        wav = wa_ref[...]
        wxv = wx_ref[...]
        halo = jnp.where(first, 0.0, halo_ref[...])
        xs, xc, r, ig, a, om = _lru_gates(lx_ref[...], halo, cwv, cb_ref[...], wav, wxv,
                                          ba_ref[...], bx_ref[...], sp)
        inv_mult = lax.rsqrt(om)
        mult = om * inv_mult
        d_y = _dot_nt(dy_ref[...], wo_ref[...])
        lg = lg_ref[...]
        sl = _sigmoid(lg)
        hv = h_ref[...]
        d_lg = d_y * hv * (sl * (1.0 + lg * (1.0 - sl)))

        c_s[...] = _shift_up(a, afirst_s[...], 1)
        in_s[...] = d_y * (lg * sl)
        row8 = _rows((8, D_MODEL))

        def group(gg, carry):
            rows = pl.ds(pl.multiple_of((ng - 1 - gg) * 8, 8), 8)
            cv = c_s[rows, :]
            uv = in_s[rows, :]
            for s in (1, 2, 4):
                c_sh = jnp.where(row8 < 8 - s, pltpu.roll(cv, 8 - s, 0), 1.0)
                u_sh = jnp.where(row8 < 8 - s, pltpu.roll(uv, 8 - s, 0), 0.0)
                uv = cv * u_sh + uv
                cv = cv * c_sh
            hh = cv * carry + uv
            dh_s[rows, :] = hh
            return jnp.broadcast_to(hh[0:1, :], (8, D_MODEL))

        dhc_s[...] = lax.fori_loop(0, ng, group, dhc_s[...])
        afirst_s[...] = a[0:8]
        dh = dh_s[...]

        h_prev = _shift_down(hv, jnp.where(first, 0.0, hhalo_ref[...]), 1)
        gx = ig * xc
        d_mult = dh * gx
        d_i = dh * (mult * xc)
        d_xc = dh * (mult * ig)
        d_la = (dh * h_prev) * a - d_mult * ((a * a) * inv_mult)
        dsp_ref[...] += jnp.sum(d_la * r, axis=0, keepdims=True) * (-LRU_C)
        d_zr = (d_la * (-LRU_C * sp)) * (r * (1.0 - r))
        d_zi = d_i * (ig * (1.0 - ig))
        dba_ref[...] += jnp.sum(d_zr, axis=0, keepdims=True)
        dbx_ref[...] += jnp.sum(d_zi, axis=0, keepdims=True)
        back = []
        for n in range(LRU_BLOCKS):
            ln = slice(n * LRU_BW, (n + 1) * LRU_BW)
            xb = _mx(xc[:, ln])
            zr_n = _mx(d_zr[:, ln])
            zi_n = _mx(d_zi[:, ln])
            dwa_ref[n] += _dot_tn(xb, zr_n)
            dwx_ref[n] += _dot_tn(xb, zi_n)
            back.append(_dot_nt(zr_n, wav[n]) + _dot_nt(zi_n, wxv[n]))
        d_xc = d_xc + jnp.concatenate(back, axis=1)
        dcb_ref[...] += jnp.sum(d_xc, axis=0, keepdims=True)
        for k in range(4):
            dcw_ref[k:k + 1, :] += jnp.sum(d_xc * xs[k], axis=0, keepdims=True)
        head = dxc_s[...]
        d_lx = cwv[3:4] * d_xc
        for k in range(3):
            d_lx = d_lx + cwv[k:k + 1] * _shift_up(d_xc, head, 3 - k)
        dxc_s[...] = d_xc[0:8]
        dp_ref[0] = d_lx.astype(_MXU_DTYPE)
        dp_ref[1] = d_lg.astype(_MXU_DTYPE)

    rev = lambda i: nt - 1 - i
    seg = lambda s: pl.BlockSpec((None, tm, D_MODEL), lambda i: (s, rev(i), 0))
    halo_row = lambda i: jnp.maximum(rev(i) * (tm // 8) - 1, 0)
    row = pl.BlockSpec((tm, D_MODEL), lambda i: (rev(i), 0))
    vec = _full((1, D_MODEL))
    wblk = _full((LRU_BLOCKS, LRU_BW, LRU_BW))
    return pl.pallas_call(
        body, name="lru_bwd", grid=(nt,),
        in_specs=[row, pl.BlockSpec((D_MODEL, D_MODEL), lambda i: (0, 0)),
                  seg(0), pl.BlockSpec((None, 8, D_MODEL), lambda i: (0, halo_row(i), 0)), seg(1),
                  row, pl.BlockSpec((8, D_MODEL), lambda i: (halo_row(i), 0)),
                  wblk, wblk, vec, vec, vec, _full((4, D_MODEL)), vec],
        out_specs=[pl.BlockSpec((2, tm, D_MODEL), lambda i: (2, rev(i), 0)),
                   wblk, wblk, vec, vec, vec, _full((4, D_MODEL)), vec],
        out_shape=[jax.ShapeDtypeStruct((N_SEG, t, D_MODEL), _MXU_DTYPE),
                   jax.ShapeDtypeStruct((LRU_BLOCKS, LRU_BW, LRU_BW), F32),
                   jax.ShapeDtypeStruct((LRU_BLOCKS, LRU_BW, LRU_BW), F32),
                   jax.ShapeDtypeStruct((1, D_MODEL), F32), jax.ShapeDtypeStruct((1, D_MODEL), F32),
                   jax.ShapeDtypeStruct((1, D_MODEL), F32), jax.ShapeDtypeStruct((4, D_MODEL), F32),
                   jax.ShapeDtypeStruct((1, D_MODEL), F32)],
        scratch_shapes=[pltpu.VMEM((tm, D_MODEL), F32), pltpu.VMEM((tm, D_MODEL), F32),
                        pltpu.VMEM((tm, D_MODEL), F32), pltpu.VMEM((8, D_MODEL), F32),
                        pltpu.VMEM((8, D_MODEL), F32), pltpu.VMEM((8, D_MODEL), F32)],
        compiler_params=_seq_params(),
    )(dy, wo, p, p, p, h, h, wa, wx, ba, bx, lam, cw, cb)


def _hgrn_bwd(dy, wo, p, o, states, logits, gw, dp, tm):
    t = dy.shape[0]
    nt = t // tm
    nc = tm // CHUNK

    def body(dy_ref, wo_ref, qf_ref, vh_ref, o_ref, st_ref, lg_ref, gw_ref, dp_in,
             dp_ref, dgw_ref, dlb_ref, dyh_s, dstate, snext):
        del dp_in
        i = pl.program_id(0)

        @pl.when(i == 0)
        def _():
            for ref in (dgw_ref, dlb_ref, dstate, snext):
                ref[...] = jnp.zeros_like(ref)

        dyh_s[...] = _dot_nt(dy_ref[...], wo_ref[...])
        l0 = lg_ref[0:1, :]
        l1 = lg_ref[1:2, :]
        lb = 1.0 / (1.0 + jnp.exp(l1 - l0))
        gwv = gw_ref[...]
        causal = _rows((CHUNK, CHUNK)) >= _cols((CHUNK, CHUNK))
        wide = 2 * HEAD_D

        for c in reversed(range(nc)):
            rows = pl.ds(c * CHUNK, CHUNK)
            q = qf_ref[0, rows, :]
            hg = vh_ref[1, rows, :]
            ov = o_ref[rows, :]
            dyh = dyh_s[rows, :]
            sig, f, sq, em1, em2, qm, km, ebm, ecm, ebc = _hgrn_prep(q, qf_ref[1, rows, :], lb)
            ke = _mx(km.astype(F32) * ecm)
            vb = _mx(vh_ref[0, rows, :])
            sg = _sigmoid(hg)
            d_on = dyh * (hg * sg)
            x_p, y_p, dv_p, n_p, ex_p = [], [], [], [], []
            for h in range(HEADS):
                ln = slice(h * HEAD_D, (h + 1) * HEAD_D)
                o_h = ov[:, ln]
                rstd = lax.rsqrt(jnp.mean(o_h * o_h, axis=-1, keepdims=True) + EPS)
                n = o_h * rstd
                dn = d_on[:, ln] * gwv[:, ln]
                d_o = rstd * (dn - n * jnp.mean(dn * n, axis=-1, keepdims=True))
                d_ob = _mx(d_o)
                d_ot = _mx(d_o.T)
                s0 = st_ref[c, h]
                ds = dstate[h]
                att = jnp.where(causal, _dot_nt(qm[:, ln], km[:, ln]), 0.0)
                d_att = jnp.where(causal, _dot_nt(d_ob, vb[:, ln]), 0.0)
                x_p.append(jnp.dot(_pad_lanes([d_ob, _mx(d_att)], wide),
                                   _pad_rows([_mx(s0 * ebm[:, ln]), km[:, ln]], wide), preferred_element_type=F32))
                y_p.append(jnp.dot(_pad_lanes([vb[:, ln], _mx(d_att.T)], wide),
                                   _pad_rows([_mx(ds * ecm[:, ln]), qm[:, ln]], wide), preferred_element_type=F32))
                dv_p.append(_dot_nt(_pad_lanes([ke[:, ln], _mx(att.T)], wide), _pad_lanes([_mx(ds), d_ot], wide)))
                ex_p.append(jnp.sum(ds * snext[h], axis=0, keepdims=True))
                dstate[h] = ds * ebc[:, ln] + jnp.dot(d_ot, qm[:, ln], preferred_element_type=F32) * ebm[:, ln]
                snext[h] = s0
                n_p.append(n)
            cat = lambda parts: jnp.concatenate(parts, axis=1)
            x_all, y_all, n_all = cat(x_p), cat(y_p), cat(n_p)
            dg = _cumsum_rows(qm.astype(F32) * x_all - km.astype(F32) * y_all, reverse=True) + cat(ex_p)
            df = dg / f - em2 * y_all
            dgw_ref[...] += jnp.sum(d_on * n_all, axis=0, keepdims=True)
            dlb_ref[...] += jnp.sum(df * (1.0 - sig), axis=0, keepdims=True)
            dp_ref[0, rows, :] = ((em1 * x_all) * (sq * (1.0 + q * (1.0 - sq)))).astype(_MXU_DTYPE)
            dp_ref[1, rows, :] = (df * ((1.0 - lb) * sig * (1.0 - sig))).astype(_MXU_DTYPE)
            dp_ref[2, rows, :] = cat(dv_p).astype(_MXU_DTYPE)
            dp_ref[3, rows, :] = (dyh * (n_all * gwv) * (sg * (1.0 + hg * (1.0 - sg)))).astype(_MXU_DTYPE)

    rev = lambda i: nt - 1 - i
    pair = lambda s: pl.BlockSpec((2, tm, D_MODEL), lambda i: (s, rev(i), 0))
    row = pl.BlockSpec((tm, D_MODEL), lambda i: (rev(i), 0))
    vec = _full((1, D_MODEL))
    return pl.pallas_call(
        body, name="hgrn_bwd", grid=(nt,),
        in_specs=[row, pl.BlockSpec((D_MODEL, D_MODEL), lambda i: (1, 0)), pair(1), pair(2), row,
                  pl.BlockSpec((nc, HEADS, HEAD_D, HEAD_D), lambda i: (rev(i), 0, 0, 0)),
                  _full((2, D_MODEL)), vec, pl.BlockSpec(memory_space=pl.ANY)],
        out_specs=[pl.BlockSpec((4, tm, D_MODEL), lambda i: (0, rev(i), 0)), vec, vec],
        out_shape=[jax.ShapeDtypeStruct((N_SEG, t, D_MODEL), _MXU_DTYPE),
                   jax.ShapeDtypeStruct((1, D_MODEL), F32), jax.ShapeDtypeStruct((1, D_MODEL), F32)],
        scratch_shapes=[pltpu.VMEM((tm, D_MODEL), F32), pltpu.VMEM((HEADS, HEAD_D, HEAD_D), F32),
                        pltpu.VMEM((HEADS, HEAD_D, HEAD_D), F32)],
        input_output_aliases={8: 0},
        compiler_params=_seq_params(),
    )(dy, wo, p, p, o, states, logits, gw, dp)


def _dp_unit(m):
    return ((m // 2 + 2) % N_SEG) * 2 + m % 2


def _dw_in(ut, dp, tt):
    t = ut.shape[1]
    hr = D_MODEL // 2

    def body(ut_ref, dp_ref, g_ref):
        @pl.when(pl.program_id(1) == 0)
        def _():
            g_ref[...] = jnp.zeros_like(g_ref)

        r = jnp.dot(ut_ref[...], dp_ref[...], preferred_element_type=F32)
        for h in range(2):
            for half in range(2):
                g_ref[h, half] += r[h * hr:(h + 1) * hr, half * UNIT:(half + 1) * UNIT]

    return pl.pallas_call(
        body, name="dw_in", grid=(N_SEG, t // tt),
        in_specs=[pl.BlockSpec((D_MODEL, tt), lambda s, k: (0, k)),
                  pl.BlockSpec((None, tt, D_MODEL), lambda s, k: (s, k, 0))],
        out_specs=pl.BlockSpec((2, 2, hr, UNIT), lambda s, k: (0, (s + 2) % N_SEG, 0, 0)),
        out_shape=jax.ShapeDtypeStruct((2, N_UNIT, hr, UNIT), F32),
        compiler_params=pltpu.CompilerParams(dimension_semantics=("arbitrary", "arbitrary")),
    )(ut, dp)


def _dx(dp, wg, x, dout, wpre, tm, parts, scatter):
    t = x.shape[0]
    n = len(parts)
    nt = t // tm

    def body(*refs):
        dp_ref, w_ref, x_ref, dout_ref, wpre_ref = refs[:5]
        ins = refs[5:5 + n]
        gx_ref, dw_ref = refs[5 + n:7 + n]
        outs = refs[7 + n:7 + 2 * n]
        send_sems, recv_sems = refs[7 + 2 * n:]
        i = pl.program_id(0)

        @pl.when(i == 0)
        def _():
            dw_ref[...] = jnp.zeros_like(dw_ref)
            for cp in _chip_copies(ins, outs, send_sems, recv_sems, scatter):
                cp.start()

        du = None
        for m in range(N_UNIT):
            part = _dot_nt(dp_ref[m // 2, :, (m % 2) * UNIT:(m % 2 + 1) * UNIT], w_ref[_dp_unit(m)])
            du = part if du is None else du + part
        xv = x_ref[...]
        rstd = lax.rsqrt(jnp.mean(xv * xv, axis=-1, keepdims=True) + EPS)
        xn = xv * rstd
        dw_ref[...] += jnp.sum(du * xn, axis=0, keepdims=True)
        dn = du * wpre_ref[...]
        gx_ref[...] = dout_ref[...] + rstd * (dn - xn * jnp.mean(dn * xn, axis=-1, keepdims=True))

        @pl.when(i == nt - 1)
        def _():
            copies = _chip_copies(ins, outs, send_sems, recv_sems, scatter)
            for cp in copies:
                cp.wait_recv()
            for cp in copies:
                cp.wait_send()

    row = pl.BlockSpec((tm, D_MODEL), lambda i: (i, 0))
    any_spec = pl.BlockSpec(memory_space=pl.ANY)
    res = pl.pallas_call(
        body, name="dx", grid=(nt,),
        in_specs=[pl.BlockSpec((N_SEG, tm, D_MODEL), lambda i: (0, i, 0)), _resident((N_UNIT, D_MODEL, UNIT)),
                  row, row, _full((1, D_MODEL))] + [any_spec] * n,
        out_specs=[row, _full((1, D_MODEL))] + [any_spec] * n,
        out_shape=[jax.ShapeDtypeStruct((t, D_MODEL), F32), jax.ShapeDtypeStruct((1, D_MODEL), F32)]
        + _chip_exchange_shapes(parts, scatter),
        scratch_shapes=[pltpu.SemaphoreType.DMA((max(3 * n, 1),)), pltpu.SemaphoreType.DMA((max(3 * n, 1),))],
        compiler_params=_seq_params(),
    )(dp, wg, x, dout, wpre, *parts)
    return res[0], res[1], res[2:]


CHUNK_BYTES = 256 * 1024
MAX_CHUNKS = 16


def _n_chunks(shape, dtype, rows):
    nbytes = jnp.dtype(dtype).itemsize
    for d in shape:
        nbytes *= d
    k = max(1, min(MAX_CHUNKS, rows, nbytes // CHUNK_BYTES))
    while rows % k:
        k -= 1
    return k


def _chip_exchange_shapes(arrays, scatter):
    return [jax.ShapeDtypeStruct((3,) + tuple(a.shape[1:] if sc else a.shape), a.dtype)
            for a, sc in zip(arrays, scatter)]


def _chip_copies(ins, outs, send_sems, recv_sems, scatter):
    if not ins:
        return []
    x, y, c = lax.axis_index("x"), lax.axis_index("y"), lax.axis_index("c")
    peers = [(1 - x, y), (x, 1 - y), (1 - x, 1 - y)]
    copies = []
    for a in range(len(ins)):
        for j, (px, py) in enumerate(peers):
            copies.append(pltpu.make_async_remote_copy(
                src_ref=ins[a].at[2 * px + py] if scatter[a] else ins[a], dst_ref=outs[a].at[j],
                send_sem=send_sems.at[a * 3 + j], recv_sem=recv_sems.at[a * 3 + j],
                device_id=(px, py, c), device_id_type=MESH))
    return copies


def _core_swap(arrays, name, halved):
    n = len(arrays)
    shapes = [a.shape[1:] if halved else a.shape for a in arrays]
    ks = [_n_chunks(s, a.dtype, s[0]) for s, a in zip(shapes, arrays)]
    offs = [sum(ks[:a]) for a in range(n)]

    def body(*refs):
        ins, outs = refs[:n], refs[n:2 * n]
        send_sems, recv_sems = refs[2 * n:]
        x, y, c = lax.axis_index("x"), lax.axis_index("y"), lax.axis_index("c")

        def copy(a, j):
            step = shapes[a][0] // ks[a]
            rows = pl.ds(j * step, step)
            return pltpu.make_async_remote_copy(
                src_ref=ins[a].at[1 - c, rows] if halved else ins[a].at[rows], dst_ref=outs[a].at[rows],
                send_sem=send_sems.at[offs[a] + j], recv_sem=recv_sems.at[offs[a] + j],
                device_id=(x, y, 1 - c), device_id_type=MESH)

        pieces = [(a, j) for a in range(n) for j in range(ks[a])]
        for a, j in pieces:
            copy(a, j).start()
        for a, j in pieces:
            copy(a, j).wait_recv()
        for a, j in pieces:
            copy(a, j).wait_send()

    any_spec = pl.BlockSpec(memory_space=pl.ANY)
    return pl.pallas_call(
        body, name=name,
        in_specs=[any_spec] * n, out_specs=[any_spec] * n,
        out_shape=[jax.ShapeDtypeStruct(s, a.dtype) for s, a in zip(shapes, arrays)],
        scratch_shapes=[pltpu.SemaphoreType.DMA((sum(ks),)), pltpu.SemaphoreType.DMA((sum(ks),))],
    )(*arrays)


def _finish_exchange(arrays, small):
    n = len(arrays)
    ks = [_n_chunks(a.shape, a.dtype, a.shape[0]) for a in arrays]
    offs = [sum(ks[:a]) for a in range(n)]
    total = sum(ks)

    def body(*refs):
        ins, small_ref = refs[:n], refs[n]
        outs, gathered = refs[n + 1:2 * n + 1], refs[2 * n + 1]
        send_sems, recv_sems = refs[2 * n + 2:]
        x, y, c = lax.axis_index("x"), lax.axis_index("y"), lax.axis_index("c")
        copies = []
        for a in range(n):
            step = arrays[a].shape[0] // ks[a]
            for j in range(ks[a]):
                rows = pl.ds(j * step, step)
                copies.append(pltpu.make_async_remote_copy(
                    src_ref=ins[a].at[rows], dst_ref=outs[a].at[rows],
                    send_sem=send_sems.at[offs[a] + j], recv_sem=recv_sems.at[offs[a] + j],
                    device_id=(x, y, 1 - c), device_id_type=MESH))
        for slot in range(7):
            dx, dy, dc = (slot + 1) % 2, ((slot + 1) // 2) % 2, (slot + 1) // 4
            copies.append(pltpu.make_async_remote_copy(
                src_ref=small_ref, dst_ref=gathered.at[slot],
                send_sem=send_sems.at[total + slot], recv_sem=recv_sems.at[total + slot],
                device_id=(x + dx - 2 * x * dx, y + dy - 2 * y * dy, c + dc - 2 * c * dc), device_id_type=MESH))
        for cp in copies:
            cp.start()
        for cp in copies:
            cp.wait_recv()
        for cp in copies:
            cp.wait_send()

    any_spec = pl.BlockSpec(memory_space=pl.ANY)
    res = pl.pallas_call(
        body, name="reduce_cores",
        in_specs=[any_spec] * (n + 1), out_specs=[any_spec] * (n + 1),
        out_shape=[jax.ShapeDtypeStruct(a.shape, a.dtype) for a in arrays]
        + [jax.ShapeDtypeStruct((7,) + small.shape, small.dtype)],
        scratch_shapes=[pltpu.SemaphoreType.DMA((total + 7,)), pltpu.SemaphoreType.DMA((total + 7,))],
    )(*arrays, small)
    return res[:n], res[n]


def _sum_own_half(g, b, name, out_dtype, tr=512):
    _, rows, cols = g.shape
    tr = min(tr, rows)
    core = lax.axis_index("c").reshape(1)

    def body(c_ref, g_ref, b_ref, o_ref):
        del c_ref
        o_ref[...] = (g_ref[...] + b_ref[...]).astype(out_dtype)

    return pl.pallas_call(
        body, name=name,
        grid_spec=pltpu.PrefetchScalarGridSpec(
            num_scalar_prefetch=1, grid=(rows // tr,),
            in_specs=[pl.BlockSpec((None, tr, cols), lambda i, c_ref: (c_ref[0], i, 0)),
                      pl.BlockSpec((tr, cols), lambda i, c_ref: (i, 0))],
            out_specs=pl.BlockSpec((tr, cols), lambda i, c_ref: (i, 0))),
        out_shape=jax.ShapeDtypeStruct((rows, cols), out_dtype),
        compiler_params=pltpu.CompilerParams(dimension_semantics=("parallel",)),
    )(core, g, b)


def _sum_chips(own, r, slabbed, name, tr=512):
    _, rows, cols = r.shape
    tr = min(tr, rows)
    chip = (2 * lax.axis_index("x") + lax.axis_index("y")).reshape(1)

    def body(c_ref, own_ref, r_ref, o_ref):
        del c_ref
        f = lambda val: val.astype(F32)
        o_ref[...] = (f(own_ref[...]) + f(r_ref[0])) + (f(r_ref[1]) + f(r_ref[2]))

    if slabbed:
        own_spec = pl.BlockSpec((None, tr, cols), lambda i, c_ref: (c_ref[0], i, 0))
    else:
        own_spec = pl.BlockSpec((tr, cols), lambda i, c_ref: (i, 0))
    return pl.pallas_call(
        body, name=name,
        grid_spec=pltpu.PrefetchScalarGridSpec(
            num_scalar_prefetch=1, grid=(rows // tr,),
            in_specs=[own_spec, pl.BlockSpec((3, tr, cols), lambda i, c_ref: (0, i, 0))],
            out_specs=pl.BlockSpec((tr, cols), lambda i, c_ref: (i, 0))),
        out_shape=jax.ShapeDtypeStruct((rows, cols), F32),
        compiler_params=pltpu.CompilerParams(dimension_semantics=("parallel",)),
    )(chip, own, r)


def _adamw_big(own, other, w, m, v, name, grid, g_block, g_index, w_block, w_index):
    core = lax.axis_index("c").reshape(1)
    half_axis = len(grid) - 1

    def body(c_ref, own_ref, oth_ref, w_ref, m_ref, v_ref, g_ref, d_ref, nm_ref, nv_ref):
        mine = pl.program_id(half_axis) == c_ref[0]
        gv = jnp.where(mine, own_ref[...], oth_ref[...])
        g_ref[...] = gv
        d_ref[...], nm_ref[...], nv_ref[...] = _adamw(w_ref[...], gv, m_ref[...], v_ref[...])

    g_spec = pl.BlockSpec(g_block, lambda *a: g_index(*a[:half_axis]))
    w_spec = pl.BlockSpec(w_block, lambda *a: w_index(*a[:-1]))
    shp = jax.ShapeDtypeStruct(w.shape, F32)
    return pl.pallas_call(
        body, name=name,
        grid_spec=pltpu.PrefetchScalarGridSpec(
            num_scalar_prefetch=1, grid=grid,
            in_specs=[g_spec, g_spec, w_spec, w_spec, w_spec], out_specs=[w_spec] * 4),
        out_shape=[shp] * 4,
        compiler_params=pltpu.CompilerParams(dimension_semantics=("parallel",) * len(grid)),
    )(core, own, other, w, m, v)


def _small_finish(own, others, lam, logits):
    def body(a_ref, b_ref, lam_ref, lg_ref, o_ref):
        g = (((a_ref[...] + b_ref[0]) + (b_ref[1] + b_ref[2]))
             + ((b_ref[3] + b_ref[4]) + (b_ref[5] + b_ref[6])))
        o_ref[...] = g
        o_ref[8:9, :] = g[8:9, :] * (-_sigmoid(-lam_ref[...]))
        lb = 1.0 / (1.0 + jnp.exp(lg_ref[1:2, :] - lg_ref[0:1, :]))
        d0 = g[9:10, :] * (lb * (1.0 - lb))
        o_ref[9:10, :] = d0
        o_ref[10:11, :] = -d0

    return pl.pallas_call(
        body, name="small_finish",
        in_specs=[_full((SMALL_ROWS, D_MODEL)), _full((7, SMALL_ROWS, D_MODEL)), _full((1, D_MODEL)),
                  _full((2, D_MODEL))],
        out_specs=_full((SMALL_ROWS, D_MODEL)),
        out_shape=jax.ShapeDtypeStruct((SMALL_ROWS, D_MODEL), F32),
    )(own, others, lam, logits)


def _adamw_small(ws, gs, ms, vs):
    n = len(ws)

    def body(*refs):
        w_r, g_r, m_r, v_r = refs[:n], refs[n:2 * n], refs[2 * n:3 * n], refs[3 * n:4 * n]
        d_o, m_o, v_o = refs[4 * n:5 * n], refs[5 * n:6 * n], refs[6 * n:7 * n]
        for j in range(n):
            d_o[j][...], m_o[j][...], v_o[j][...] = _adamw(w_r[j][...], g_r[j][...], m_r[j][...], v_r[j][...])

    specs = [_full(w.shape) for w in ws]
    shapes = [jax.ShapeDtypeStruct(w.shape, F32) for w in ws]
    outs = pl.pallas_call(
        body, name="adamw_small",
        in_specs=specs * 4, out_specs=specs * 3, out_shape=shapes * 3,
    )(*ws, *gs, *ms, *vs)
    return outs[:n], outs[n:2 * n], outs[2 * n:]


def _local_step(x, tgt, ut, p, pre_w, wg, cw, cb, wa, wx, ba, bx, lam, logits, gw, wo, post_w, tm, tmm,
                prepare=None):
    h, y2 = _lru_fwd(p, wa, wx, ba, bx, lam, cw, cb, tm)
    y2, o, states = _hgrn_fwd(p, logits, gw, y2, tm)
    dout, dy, sq, d_post = _out_fused(y2, wo, x, tgt, post_w, tmm)
    g_out = _dw_out(y2, dy, min(2048, x.shape[0]))
    dp, d_wa, d_wx, d_ba, d_bx, d_sp, d_cw, d_cb = _lru_bwd(dy, wo, p, h, wa, wx, ba, bx, lam, cw, cb, tm)
    dp, d_gw, d_lb = _hgrn_bwd(dy, wo, p, o, states, logits, gw, dp, tm)
    g_in = _dw_in(ut, dp, min(2048, x.shape[0]))
    parts, scatter = prepare(g_in, g_out, d_wa, d_wx) if prepare is not None else ([], [])
    grad_x, d_pre, received = _dx(dp, wg, x, dout, pre_w, tm, parts, scatter)
    small = jnp.concatenate([d_pre, d_cw, d_cb, d_ba, d_bx, d_sp, d_lb,
                             jnp.zeros((1, D_MODEL), F32), d_gw, d_post, sq,
                             jnp.zeros((SMALL_ROWS - 14, D_MODEL), F32)], axis=0)
    return grad_x, g_in, g_out, d_wa, d_wx, small, parts, received


def kernel(x, pre_norm_w, w_in, conv_w, conv_b, lru_w_a, lru_b_a, lru_w_x, lru_b_x, lru_lambda, hgrn_lb_logits, hgrn_gnorm_w, w_out, post_norm_w, loss_target, m_pre_norm_w, m_w_in, m_conv_w, m_conv_b, m_lru_w_a, m_lru_b_a, m_lru_w_x, m_lru_b_x, m_lru_lambda, m_hgrn_lb_logits, m_hgrn_gnorm_w, m_w_out, m_post_norm_w, v_pre_norm_w, v_w_in, v_conv_w, v_conv_b, v_lru_w_a, v_lru_b_a, v_lru_w_x, v_lru_b_x, v_lru_lambda, v_hgrn_lb_logits, v_hgrn_gnorm_w, v_w_out, v_post_norm_w):
    t = x.shape[1]
    tm = min(256, t)
    tmm = min(512, t)
    chip = 2 * lax.axis_index("x") + lax.axis_index("y")

    win_units = w_in[0].astype(_MXU_DTYPE).reshape(D_MODEL, 3, UNIT).transpose(1, 0, 2)
    small_w = jnp.concatenate([conv_w[0], lru_b_a[0], lru_b_x[0]], axis=1)
    ut, p, wg4, (wo4, wa4, wx4), sw4 = _in_proj_gather(
        x[0], pre_norm_w, win_units,
        [w_out[0].astype(_MXU_DTYPE).reshape(2, 256, D_MODEL), lru_w_a[0].astype(_MXU_DTYPE).reshape(2, 128, LRU_BW),
         lru_w_x[0].astype(_MXU_DTYPE).reshape(2, 128, LRU_BW)],
        small_w, min(1024, t))
    wg = wg4.reshape(N_UNIT, D_MODEL, UNIT)
    wo = wo4.reshape(2 * D_MODEL, D_MODEL)
    by_block = lambda w4: w4.reshape(N_CHIPS, LRU_BLOCKS, 64, LRU_BW).transpose(1, 0, 2, 3).reshape(
        LRU_BLOCKS, LRU_BW, LRU_BW)
    wa, wx = by_block(wa4), by_block(wx4)
    sw4 = sw4.reshape(N_CHIPS, 4, 384)
    cw = sw4[:, :, 0:256].transpose(1, 0, 2).reshape(4, D_MODEL)
    ba = sw4[:, :, 256:320].transpose(1, 0, 2).reshape(1, D_MODEL)
    bx = sw4[:, :, 320:384].transpose(1, 0, 2).reshape(1, D_MODEL)

    def lru_layout(g):
        g = g.reshape(2, 2, N_CHIPS, 64, LRU_BW).transpose(0, 2, 1, 3, 4)
        return g.reshape(2, N_CHIPS * 128, LRU_BW)

    def prepare(g_in, g_out, d_wa, d_wx):
        g_in = g_in.reshape(2, N_UNIT * 512, UNIT)
        g_out = g_out.reshape(2, N_CHIPS * 256, D_MODEL)
        g_wa, g_wx = lru_layout(d_wa), lru_layout(d_wx)
        b_in, b_out, b_wa, b_wx = _core_swap([g_in, g_out, g_wa, g_wx], "reduce_swap", halved=True)
        parts = [_sum_own_half(g_in, b_in, "presum_in", _WIRE_DTYPE).reshape(N_CHIPS, 3 * 512, UNIT),
                 _sum_own_half(g_out, b_out, "presum_out", _WIRE_DTYPE).reshape(N_CHIPS, 256, D_MODEL),
                 _sum_own_half(g_wa, b_wa, "presum_wa", _WIRE_DTYPE).reshape(N_CHIPS, 128, LRU_BW),
                 _sum_own_half(g_wx, b_wx, "presum_wx", _WIRE_DTYPE).reshape(N_CHIPS, 128, LRU_BW)]
        return parts, [True] * 4

    grad_x, _, _, _, _, small, (p_in, p_out, p_wa, p_wx), (r_in, r_out, r_wa, r_wx) = _local_step(
        x[0], loss_target[0], ut, p, pre_norm_w, wg, cw, conv_b, wa, wx, ba, bx, lru_lambda, hgrn_lb_logits,
        hgrn_gnorm_w, wo, post_norm_w, tm, tmm, prepare)
    s_in = _sum_chips(p_in, r_in, True, "sum_in")
    s_out = _sum_chips(p_out, r_out, True, "sum_out")
    s_wa = _sum_chips(p_wa, r_wa, True, "sum_wa")
    s_wx = _sum_chips(p_wx, r_wx, True, "sum_wx")
    (o_in, o_out, o_wa, o_wx), small_others = _finish_exchange([s_in, s_out, s_wa, s_wx], small)

    g_w_in, d_w_in, nm_w_in, nv_w_in = _adamw_big(
        s_in.reshape(3, 512, UNIT), o_in.reshape(3, 512, UNIT), w_in[0], m_w_in[0], v_w_in[0], "adamw_w_in", (3, 2),
        (None, 512, UNIT), lambda k: (k, 0, 0), (512, UNIT), lambda k, h: (h, k))
    g_w_out, d_w_out, nm_w_out, nv_w_out = _adamw_big(
        s_out, o_out, w_out[0], m_w_out[0], v_w_out[0], "adamw_w_out", (2,),
        (256, D_MODEL), lambda: (0, 0), (256, D_MODEL), lambda h: (h, 0))
    sq2 = lambda a: a.reshape(LRU_BW, LRU_BW)
    lru_specs = ((2,), (128, LRU_BW), lambda: (0, 0), (128, LRU_BW), lambda h: (h, 0))
    g_wa, d_wa2, nm_wa, nv_wa = _adamw_big(s_wa, o_wa, sq2(lru_w_a), sq2(m_lru_w_a), sq2(v_lru_w_a), "adamw_wa",
                                           *lru_specs)
    g_wx, d_wx2, nm_wx, nv_wx = _adamw_big(s_wx, o_wx, sq2(lru_w_x), sq2(m_lru_w_x), sq2(v_lru_w_x), "adamw_wx",
                                           *lru_specs)

    gs = _small_finish(small, small_others, lru_lambda, hgrn_lb_logits)
    loss = jnp.sum(gs[13]) * (0.5 / D_MODEL)
    g_pre = gs[0:1]
    g_cw = lax.dynamic_slice(gs[1:5], (0, chip * 256), (4, 256))
    g_cb = gs[5:6]
    g_ba = lax.dynamic_slice(gs[6].reshape(LRU_BLOCKS, N_CHIPS, 64), (0, chip, 0), (LRU_BLOCKS, 1, 64)).reshape(4, 64)
    g_bx = lax.dynamic_slice(gs[7].reshape(LRU_BLOCKS, N_CHIPS, 64), (0, chip, 0), (LRU_BLOCKS, 1, 64)).reshape(4, 64)
    g_lam = gs[8:9]
    g_lb = gs[9:11]
    g_gw = gs[11:12]
    g_post = gs[12:13]
    small_g = [g_pre, g_cw, g_cb, g_ba, g_bx, g_lam, g_lb, g_gw, g_post]
    two_d = lambda a: a.reshape(a.shape[-2:])
    small_w_list = [pre_norm_w, conv_w, conv_b, lru_b_a, lru_b_x, lru_lambda, hgrn_lb_logits, hgrn_gnorm_w, post_norm_w]
    small_m_list = [m_pre_norm_w, m_conv_w, m_conv_b, m_lru_b_a, m_lru_b_x, m_lru_lambda, m_hgrn_lb_logits,
                    m_hgrn_gnorm_w, m_post_norm_w]
    small_v_list = [v_pre_norm_w, v_conv_w, v_conv_b, v_lru_b_a, v_lru_b_x, v_lru_lambda, v_hgrn_lb_logits,
                    v_hgrn_gnorm_w, v_post_norm_w]
    sd, sm, sv = _adamw_small([two_d(a) for a in small_w_list], small_g,
                              [two_d(a) for a in small_m_list], [two_d(a) for a in small_v_list])

    def shaped(vals, refs):
        return [val.reshape(ref.shape) for val, ref in zip(vals, refs)]

    s_g = shaped(small_g, small_w_list)
    s_d = shaped(sd, small_w_list)
    s_m = shaped(sm, small_w_list)
    s_v = shaped(sv, small_w_list)

    def ordered(small, big_in, big_wa, big_wx, big_out):
        pre, cw_, cb_, ba_, bx_, lam_, lb_, gw_, post_ = small
        return [pre, big_in.reshape(w_in.shape), cw_, cb_, big_wa.reshape(lru_w_a.shape), ba_,
                big_wx.reshape(lru_w_x.shape), bx_, lam_, lb_, gw_, big_out.reshape(w_out.shape), post_]

    grads = ordered(s_g, g_w_in, g_wa, g_wx, g_w_out)
    deltas = ordered(s_d, d_w_in, d_wa2, d_wx2, d_w_out)
    new_m = ordered(s_m, nm_w_in, nm_wa, nm_wx, nm_w_out)
    new_v = ordered(s_v, nv_w_in, nv_wa, nv_wx, nv_w_out)
    return (loss, grad_x.reshape(x.shape), *grads, *deltas, *new_m, *new_v)
```

```python
import functools

import jax
import jax.numpy as jnp
from jax import lax
from jax.experimental import pallas as pl
from jax.experimental.pallas import tpu as pltpu

F32 = jnp.float32
BF16 = jnp.bfloat16
_MXU_DTYPE = jnp.bfloat16
_WIRE_DTYPE = jnp.bfloat16

D_MODEL = 1024
N_SEG = 6
UNIT = 512
N_UNIT = 12
LRU_BLOCKS = 4
LRU_BW = 256
LRU_C = 8.0
HEADS = 8
HEAD_D = 128
CHUNK = 64
EPS = 1e-6
N_CHIPS = 4
EXP_CLAMP = 80.0

ADAM_LR = 0.001
ADAM_B1 = 0.9
ADAM_B2 = 0.999
ADAM_EPS = 1e-08
ADAM_WD = 0.01
ADAM_STEP = 10

SMALL_ROWS = 16
FWD_PIECES = 4
MESH = pl.DeviceIdType.MESH


def _sigmoid(x):
    return 0.5 * jnp.tanh(0.5 * x) + 0.5


def _mx(x):
    return x.astype(_MXU_DTYPE)


def _dot(a, b):
    return jnp.dot(_mx(a), _mx(b), preferred_element_type=F32)


def _dot_nt(a, b):
    return lax.dot_general(_mx(a), _mx(b), (((1,), (1,)), ((), ())), preferred_element_type=F32)


def _dot_tn(a, b):
    return lax.dot_general(_mx(a), _mx(b), (((0,), (0,)), ((), ())), preferred_element_type=F32)


def _rows(shape):
    return lax.broadcasted_iota(jnp.int32, shape, 0)


def _cols(shape):
    return lax.broadcasted_iota(jnp.int32, shape, 1)


def _softplus_neg(lam):
    z = -lam
    e = jnp.exp(-jnp.abs(z))
    series = e * (1.0 - e * (0.5 - e * (1.0 / 3.0 - 0.25 * e)))
    return jnp.maximum(z, 0.0) + jnp.where(e < 1e-2, series, jnp.log(1.0 + e))


def _one_minus_sq(a, y):
    series = -y * (1.0 + y * (0.5 + y * (1.0 / 6.0)))
    return jnp.where(y > -0.01, series, 1.0 - a * a)


def _cumsum_rows(x, reverse=False):
    n = x.shape[0] // 8
    row8 = _rows((8, x.shape[1]))
    out = [None] * n
    carry = None
    for g in (reversed(range(n)) if reverse else range(n)):
        blk = x[8 * g:8 * g + 8]
        for s in (1, 2, 4):
            if reverse:
                blk = blk + jnp.where(row8 < 8 - s, pltpu.roll(blk, 8 - s, 0), 0.0)
            else:
                blk = blk + jnp.where(row8 >= s, pltpu.roll(blk, s, 0), 0.0)
        if carry is not None:
            blk = blk + carry
        carry = blk[0:1] if reverse else blk[7:8]
        out[g] = blk
    return jnp.concatenate(out, axis=0)


def _shift_down(x, halo, s):
    if s == 0:
        return x
    r = pltpu.roll(x, s, 0)
    top = jnp.where(_rows(halo.shape) < s, pltpu.roll(halo, s, 0), r[0:8])
    return jnp.concatenate([top, r[8:]], axis=0)


def _shift_up(x, head, s):
    if s == 0:
        return x
    n = x.shape[0]
    r = pltpu.roll(x, n - s, 0)
    bottom = jnp.where(_rows(head.shape) >= 8 - s, pltpu.roll(head, 8 - s, 0), r[n - 8:n])
    return jnp.concatenate([r[:n - 8], bottom], axis=0)


def _lru_gates(lx, halo, cw, cb, wa, wx, ba, bx, sp):
    xs = [_shift_down(lx, halo, 3 - k) for k in range(4)]
    xc = cb + cw[0:1] * xs[0] + cw[1:2] * xs[1] + cw[2:3] * xs[2] + cw[3:4] * xs[3]
    zr, zi = [], []
    for n in range(LRU_BLOCKS):
        xb = _mx(xc[:, n * LRU_BW:(n + 1) * LRU_BW])
        zr.append(jnp.dot(xb, wa[n], preferred_element_type=F32))
        zi.append(jnp.dot(xb, wx[n], preferred_element_type=F32))
    r = _sigmoid(jnp.concatenate(zr, axis=1) + ba)
    ig = _sigmoid(jnp.concatenate(zi, axis=1) + bx)
    la = (-LRU_C * sp) * r
    a = jnp.exp(la)
    om = _one_minus_sq(a, 2.0 * la)
    return xs, xc, r, ig, a, om


def _hgrn_prep(q, fr, lb):
    sig = _sigmoid(fr)
    f = lb + (1.0 - lb) * sig
    sq = _sigmoid(q)
    b = _cumsum_rows(jnp.log(f))
    bm = b[CHUNK // 2 - 1:CHUNK // 2]
    bc = b[CHUNK - 1:CHUNK]
    em1 = jnp.exp(jnp.minimum(b - bm, EXP_CLAMP))
    em2 = jnp.exp(jnp.minimum(bm - b, EXP_CLAMP))
    qm = _mx((q * sq) * em1)
    km = _mx((1.0 - f) * em2)
    return sig, f, sq, em1, em2, qm, km, jnp.exp(bm), jnp.exp(bc - bm), jnp.exp(bc)


def _pad_lanes(parts, width):
    have = sum(p.shape[1] for p in parts)
    pad = [jnp.zeros((parts[0].shape[0], width - have), parts[0].dtype)] if width > have else []
    return jnp.concatenate(list(parts) + pad, axis=1)


def _pad_rows(parts, height):
    have = sum(p.shape[0] for p in parts)
    pad = [jnp.zeros((height - have, parts[0].shape[1]), parts[0].dtype)] if height > have else []
    return jnp.concatenate(list(parts) + pad, axis=0)


def _adamw(w, g, m, v):
    m = ADAM_B1 * m + (1.0 - ADAM_B1) * g
    v = ADAM_B2 * v + (1.0 - ADAM_B2) * jnp.square(g)
    m_hat = m / (1.0 - ADAM_B1 ** ADAM_STEP)
    v_hat = v / (1.0 - ADAM_B2 ** ADAM_STEP)
    delta = -ADAM_LR * (m_hat / (jnp.sqrt(v_hat) + ADAM_EPS) + ADAM_WD * w)
    return delta, m, v


def _full(shape):
    nd = len(shape)
    return pl.BlockSpec(shape, lambda *_: (0,) * nd)


def _seq_params():
    return pltpu.CompilerParams(dimension_semantics=("arbitrary",))


def _resident(shape):
    nd = len(shape)
    return pl.BlockSpec(shape, lambda *_: (0,) * nd, pipeline_mode=pl.Buffered(1))


def _in_proj_gather(x, w, win_units, halved, small_w, tm):
    t = x.shape[0]
    nt = t // tm
    n_h = len(halved)
    n_split = 3 + n_h
    n_items = n_split + 1
    hu = D_MODEL // 2
    chip = 2 * lax.axis_index("x") + lax.axis_index("y")
    flips = (2, 1, 3)
    order = jnp.stack([3 * chip + u for u in range(3)]
                      + [3 * jnp.bitwise_xor(chip, flips[jj]) + u for u in range(3) for jj in range(3)])

    def body(order_ref, x_ref, w_ref, win_ref, *rest):
        del order_ref
        h_ins, small_ref = rest[:n_h], rest[n_h]
        ut_ref, p_ref, wg_ref = rest[n_h + 1:n_h + 4]
        h_outs, sw_ref = rest[n_h + 4:2 * n_h + 4], rest[2 * n_h + 4]
        u_s, wbuf, send, recv, fsend, frecv, osend, orecv, wsem = rest[2 * n_h + 5:]
        j, i = pl.program_id(0), pl.program_id(1)
        mx, my, c = lax.axis_index("x"), lax.axis_index("y"), lax.axis_index("c")
        me = 2 * mx + my
        peers = [(1 - mx, my), (mx, 1 - my), (1 - mx, 1 - my)]
        sibling = (mx, my, 1 - c)

        def remote(src, dst, ssem, rsem, dev):
            return pltpu.make_async_remote_copy(src_ref=src, dst_ref=dst, send_sem=ssem, recv_sem=rsem,
                                                device_id=dev, device_id_type=MESH)

        def mine(item):
            if item < 3:
                return win_ref.at[item, pl.ds(c * hu, hu)]
            return h_ins[item - 3].at[c] if item < n_split else small_ref

        def piece(item, q, h):
            if item < 3:
                return wg_ref.at[q, item, pl.ds(h * hu, hu)]
            return h_outs[item - 3].at[q, h] if item < n_split else sw_ref.at[q]

        def fetch(item, jj, q):
            px, py = peers[jj]
            return remote(mine(item), piece(item, q, c), send.at[item * 3 + jj], recv.at[item * 3 + jj], (px, py, c))

        def forward(item, jj, q, h):
            whole = piece(item, q, h)
            step = whole.shape[0] // FWD_PIECES
            copies = []
            for k in range(FWD_PIECES):
                part = whole.at[pl.ds(k * step, step)]
                sem = (item * 3 + jj) * FWD_PIECES + k
                copies.append(remote(part, part, fsend.at[sem], frecv.at[sem], sibling))
            return copies

        def own(item):
            if item < 3:
                src, dst = win_ref.at[item], wg_ref.at[me, item]
            elif item < n_split:
                src, dst = h_ins[item - 3], h_outs[item - 3].at[me]
            else:
                src, dst = small_ref, sw_ref.at[me]
            return remote(src, dst, osend.at[item], orecv.at[item], sibling)

        def unit_of(jn):
            u, jj = (jn - 3) // 3, (jn - 3) % 3
            q = jnp.bitwise_xor(me, jnp.where(jj == 0, flips[0], jnp.where(jj == 1, flips[1], flips[2])))
            return u, jj, q

        def unit_fetch(u, jj, q):
            half = wg_ref.at[q, u, pl.ds(c * hu, hu)]
            return remote(half, half, send.at[u * 3 + jj], recv.at[u * 3 + jj], sibling)

        def unit_forward(u, jj, q, h):
            step = hu // FWD_PIECES
            copies = []
            for k in range(FWD_PIECES):
                part = wg_ref.at[q, u, pl.ds(h * hu + k * step, step)]
                sem = (u * 3 + jj) * FWD_PIECES + k
                copies.append(remote(part, part, fsend.at[sem], frecv.at[sem], sibling))
            return copies

        for item in range(n_items):
            @pl.when(jnp.logical_and(j == min(2 * item, 6), i == 0))
            def _(item=item):
                for jj in range(3):
                    fetch(item, jj, me).start()

        def load_unit(src, slot):
            return pltpu.make_async_copy(src, wbuf.at[slot], wsem.at[slot])

        @pl.when(jnp.logical_and(j == 0, i == 0))
        def _():
            for item in range(n_items):
                own(item).start()
            load_unit(win_ref.at[0], 0).start()

        @pl.when(j == 0)
        def _():
            xv = x_ref[...]
            rstd = lax.rsqrt(jnp.mean(xv * xv, axis=-1, keepdims=True) + EPS)
            u = xv * rstd * w_ref[...]
            ut_ref[...] = u.T.astype(_MXU_DTYPE)
            u_s[pl.ds(pl.multiple_of(i * tm, tm), tm), :] = _mx(u)

        slot = j % 2

        @pl.when(i == 0)
        def _():
            load_unit(win_ref.at[0], slot).wait()

        p_ref[...] = jnp.dot(u_s[pl.ds(pl.multiple_of(i * tm, tm), tm), :], wbuf[slot], preferred_element_type=F32)

        @pl.when(jnp.logical_and(i == nt - 1, jnp.logical_and(j >= 1, j < N_UNIT - 2)))
        def _():
            u, jj, q = unit_of(j + 2)
            unit_fetch(u, jj, q).wait_recv()
            for cp in unit_forward(u, jj, q, c):
                cp.start()

        @pl.when(jnp.logical_and(i == nt - 1, j < 2))
        def _():
            load_unit(win_ref.at[j + 1], 1 - slot).start()

        @pl.when(jnp.logical_and(i == nt - 1, jnp.logical_and(j >= 2, j < N_UNIT - 1)))
        def _():
            u, jj, q = unit_of(j + 1)
            for cp in unit_forward(u, jj, q, 1 - c):
                cp.wait_recv()
            load_unit(wg_ref.at[q, u], 1 - slot).start()

        @pl.when(jnp.logical_and(i == nt - 1, j == N_UNIT - 1))
        def _():
            qs = [2 * px + py for px, py in peers]
            for item in range(3, n_items):
                for jj in range(3):
                    fetch(item, jj, qs[jj]).wait_recv()
                    if item < n_split:
                        for cp in forward(item, jj, qs[jj], c):
                            cp.start()
            for item in range(3, n_split):
                for jj in range(3):
                    for cp in forward(item, jj, qs[jj], 1 - c):
                        cp.wait_recv()
            for item in range(n_items):
                own(item).wait_recv()
            for item in range(n_items):
                for jj in range(3):
                    fetch(item, jj, me).wait_send()
                    if item < n_split:
                        for cp in forward(item, jj, qs[jj], c):
                            cp.wait_send()
                own(item).wait_send()

    any_spec = pl.BlockSpec(memory_space=pl.ANY)
    first_pass = lambda j, i: jnp.where(j == 0, i, nt - 1)
    res = pl.pallas_call(
        body, name="in_proj",
        grid_spec=pltpu.PrefetchScalarGridSpec(
            num_scalar_prefetch=1, grid=(N_UNIT, nt),
            in_specs=[pl.BlockSpec((tm, D_MODEL), lambda j, i, o: (first_pass(j, i), 0)),
                      pl.BlockSpec((1, D_MODEL), lambda j, i, o: (0, 0))] + [any_spec] * (n_h + 2),
            out_specs=[pl.BlockSpec((D_MODEL, tm), lambda j, i, o: (0, first_pass(j, i))),
                       pl.BlockSpec((None, tm, UNIT), lambda j, i, o: (o[j] // 2, i, o[j] % 2))]
            + [any_spec] * (n_h + 2),
            scratch_shapes=[pltpu.VMEM((t, D_MODEL), _MXU_DTYPE), pltpu.VMEM((2, D_MODEL, UNIT), _MXU_DTYPE),
                            pltpu.SemaphoreType.DMA((3 * n_items,)), pltpu.SemaphoreType.DMA((3 * n_items,)),
                            pltpu.SemaphoreType.DMA((3 * n_split * FWD_PIECES,)),
                            pltpu.SemaphoreType.DMA((3 * n_split * FWD_PIECES,)),
                            pltpu.SemaphoreType.DMA((n_items,)), pltpu.SemaphoreType.DMA((n_items,)),
                            pltpu.SemaphoreType.DMA((2,))]),
        out_shape=[jax.ShapeDtypeStruct((D_MODEL, t), _MXU_DTYPE), jax.ShapeDtypeStruct((N_SEG, t, D_MODEL), F32),
                   jax.ShapeDtypeStruct((N_CHIPS,) + win_units.shape, win_units.dtype)]
        + [jax.ShapeDtypeStruct((N_CHIPS,) + a.shape, a.dtype) for a in halved]
        + [jax.ShapeDtypeStruct((N_CHIPS,) + small_w.shape, small_w.dtype)],
        compiler_params=pltpu.CompilerParams(dimension_semantics=("arbitrary", "arbitrary")),
    )(order, x, w, win_units, *halved, small_w)
    return res[0], res[1], res[2], res[3:3 + n_h], res[3 + n_h]


def _lru_fwd(p, wa, wx, ba, bx, lam, cw, cb, tm):
    t = p.shape[1]
    ng = tm // 8

    def body(lx_ref, halo_ref, lg_ref, wa_ref, wx_ref, ba_ref, bx_ref, lam_ref, cw_ref, cb_ref,
             h_ref, y_ref, a_s, u_s, hc_s):
        i = pl.program_id(0)

        @pl.when(i == 0)
        def _():
            hc_s[...] = jnp.zeros_like(hc_s)

        sp = _softplus_neg(lam_ref[...])
        halo = jnp.where(i == 0, 0.0, halo_ref[...])
        _, xc, _, ig, a, om = _lru_gates(lx_ref[...], halo, cw_ref[...], cb_ref[...], wa_ref[...],
                                         wx_ref[...], ba_ref[...], bx_ref[...], sp)
        a_s[...] = a
        u_s[...] = jnp.sqrt(om) * (ig * xc)
        row8 = _rows((8, D_MODEL))

        def group(g, hc):
            rows = pl.ds(pl.multiple_of(g * 8, 8), 8)
            av = a_s[rows, :]
            uv = u_s[rows, :]
            for s in (1, 2, 4):
                a_sh = jnp.where(row8 >= s, pltpu.roll(av, s, 0), 1.0)
                u_sh = jnp.where(row8 >= s, pltpu.roll(uv, s, 0), 0.0)
                uv = av * u_sh + uv
                av = av * a_sh
            hh = av * hc + uv
            h_ref[rows, :] = hh
            return jnp.broadcast_to(hh[7:8, :], (8, D_MODEL))

        hc_s[...] = lax.fori_loop(0, ng, group, hc_s[...])
        lg = lg_ref[...]
        y_ref[...] = (h_ref[...] * (lg * _sigmoid(lg))).astype(_MXU_DTYPE)

    seg = lambda s: pl.BlockSpec((None, tm, D_MODEL), lambda i: (s, i, 0))
    return pl.pallas_call(
        body, name="lru_fwd", grid=(t // tm,),
        in_specs=[seg(0),
                  pl.BlockSpec((None, 8, D_MODEL), lambda i: (0, jnp.maximum(i * (tm // 8) - 1, 0), 0)),
                  seg(1),
                  _full((LRU_BLOCKS, LRU_BW, LRU_BW)), _full((LRU_BLOCKS, LRU_BW, LRU_BW)),
                  _full((1, D_MODEL)), _full((1, D_MODEL)), _full((1, D_MODEL)),
                  _full((4, D_MODEL)), _full((1, D_MODEL))],
        out_specs=[pl.BlockSpec((tm, D_MODEL), lambda i: (i, 0)),
                   pl.BlockSpec((None, tm, D_MODEL), lambda i: (0, i, 0))],
        out_shape=[jax.ShapeDtypeStruct((t, D_MODEL), F32),
                   jax.ShapeDtypeStruct((2, t, D_MODEL), _MXU_DTYPE)],
        scratch_shapes=[pltpu.VMEM((tm, D_MODEL), F32), pltpu.VMEM((tm, D_MODEL), F32),
                        pltpu.VMEM((8, D_MODEL), F32)],
        compiler_params=_seq_params(),
    )(p, p, p, wa, wx, ba, bx, lam, cw, cb)


def _hgrn_fwd(p, logits, gw, y2, tm):
    t = p.shape[1]
    nc = tm // CHUNK

    def body(qf_ref, vh_ref, lg_ref, gw_ref, y_in, y_ref, o_ref, st_ref, state):
        del y_in
        i = pl.program_id(0)

        @pl.when(i == 0)
        def _():
            state[...] = jnp.zeros_like(state)

        l0 = lg_ref[0:1, :]
        l1 = lg_ref[1:2, :]
        lb = 1.0 / (1.0 + jnp.exp(l1 - l0))
        gwv = gw_ref[...]
        causal = _rows((CHUNK, CHUNK)) >= _cols((CHUNK, CHUNK))

        for c in range(nc):
            rows = pl.ds(c * CHUNK, CHUNK)
            v = vh_ref[0, rows, :]
            hg = vh_ref[1, rows, :]
            _, _, _, _, _, qm, km, ebm, ecm, ebc = _hgrn_prep(qf_ref[0, rows, :], qf_ref[1, rows, :], lb)
            ke = _mx(km.astype(F32) * ecm)
            gate = gwv * (hg * _sigmoid(hg))
            o_parts, y_parts = [], []
            for h in range(HEADS):
                ln = slice(h * HEAD_D, (h + 1) * HEAD_D)
                s0 = state[h]
                st_ref[c, h] = s0
                vt = _mx(v[:, ln].T)
                att = _mx(jnp.where(causal, _dot_nt(qm[:, ln], km[:, ln]), 0.0))
                o = _dot_nt(_pad_lanes([qm[:, ln], att], 2 * HEAD_D),
                            _pad_lanes([_mx(s0 * ebm[:, ln]), vt], 2 * HEAD_D))
                state[h] = s0 * ebc[:, ln] + jnp.dot(vt, ke[:, ln], preferred_element_type=F32)
                rstd = lax.rsqrt(jnp.mean(o * o, axis=-1, keepdims=True) + EPS)
                o_parts.append(o)
                y_parts.append((o * rstd * gate[:, ln]).astype(_MXU_DTYPE))
            o_ref[rows, :] = jnp.concatenate(o_parts, axis=1)
            y_ref[rows, :] = jnp.concatenate(y_parts, axis=1)

    pair = lambda s: pl.BlockSpec((2, tm, D_MODEL), lambda i: (s, i, 0))
    return pl.pallas_call(
        body, name="hgrn_fwd", grid=(t // tm,),
        in_specs=[pair(1), pair(2), _full((2, D_MODEL)), _full((1, D_MODEL)),
                  pl.BlockSpec(memory_space=pl.ANY)],
        out_specs=[pl.BlockSpec((None, tm, D_MODEL), lambda i: (1, i, 0)),
                   pl.BlockSpec((tm, D_MODEL), lambda i: (i, 0)),
                   pl.BlockSpec((nc, HEADS, HEAD_D, HEAD_D), lambda i: (i, 0, 0, 0))],
        out_shape=[jax.ShapeDtypeStruct((2, t, D_MODEL), _MXU_DTYPE),
                   jax.ShapeDtypeStruct((t, D_MODEL), F32),
                   jax.ShapeDtypeStruct((t // CHUNK, HEADS, HEAD_D, HEAD_D), F32)],
        scratch_shapes=[pltpu.VMEM((HEADS, HEAD_D, HEAD_D), F32)],
        input_output_aliases={4: 0},
        compiler_params=_seq_params(),
    )(p, p, logits, gw, y2)


def _out_fused(y2, wo, x, tgt, wpost, tm):
    t = x.shape[0]
    inv_d = 1.0 / D_MODEL

    def body(y_ref, wo_ref, x_ref, t_ref, w_ref, dout_ref, dy_ref, sq_ref, dw_ref):
        i = pl.program_id(0)

        @pl.when(i == 0)
        def _():
            sq_ref[...] = jnp.zeros_like(sq_ref)
            dw_ref[...] = jnp.zeros_like(dw_ref)

        y = (jnp.dot(y_ref[0], wo_ref[0:D_MODEL, :], preferred_element_type=F32)
             + jnp.dot(y_ref[1], wo_ref[D_MODEL:2 * D_MODEL, :], preferred_element_type=F32))
        w = w_ref[...]
        rstd = lax.rsqrt(jnp.mean(y * y, axis=-1, keepdims=True) + EPS)
        n = y * rstd
        err = (x_ref[...] + n * w) - t_ref[...]
        sq_ref[...] += jnp.sum(err * err, axis=0, keepdims=True)
        dout = err * inv_d
        dout_ref[...] = dout
        dw_ref[...] += jnp.sum(dout * n, axis=0, keepdims=True)
        dn = dout * w
        dy_ref[...] = (rstd * (dn - n * jnp.mean(dn * n, axis=-1, keepdims=True))).astype(_MXU_DTYPE)

    row = pl.BlockSpec((tm, D_MODEL), lambda i: (i, 0))
    return pl.pallas_call(
        body, name="out_fused", grid=(t // tm,),
        in_specs=[pl.BlockSpec((2, tm, D_MODEL), lambda i: (0, i, 0)), _full((2 * D_MODEL, D_MODEL)),
                  row, row, _full((1, D_MODEL))],
        out_specs=[row, row, _full((1, D_MODEL)), _full((1, D_MODEL))],
        out_shape=[jax.ShapeDtypeStruct((t, D_MODEL), F32), jax.ShapeDtypeStruct((t, D_MODEL), _MXU_DTYPE),
                   jax.ShapeDtypeStruct((1, D_MODEL), F32), jax.ShapeDtypeStruct((1, D_MODEL), F32)],
        compiler_params=_seq_params(),
    )(y2, wo, x, tgt, wpost)


def _dw_out(y2, dy, tt):
    t = dy.shape[0]
    hr = UNIT // 2

    def body(y_ref, dy_ref, g_ref):
        @pl.when(pl.program_id(1) == 0)
        def _():
            g_ref[...] = jnp.zeros_like(g_ref)

        r = _dot_tn(y_ref[...], dy_ref[...])
        g_ref[0] += r[0:hr]
        g_ref[1] += r[hr:UNIT]

    return pl.pallas_call(
        body, name="dw_out", grid=(N_CHIPS, t // tt),
        in_specs=[pl.BlockSpec((None, tt, UNIT), lambda c, k: (c // 2, k, c % 2)),
                  pl.BlockSpec((tt, D_MODEL), lambda c, k: (k, 0))],
        out_specs=pl.BlockSpec((2, None, hr, D_MODEL), lambda c, k: (0, c, 0, 0)),
        out_shape=jax.ShapeDtypeStruct((2, N_CHIPS, hr, D_MODEL), F32),
        compiler_params=pltpu.CompilerParams(dimension_semantics=("arbitrary", "arbitrary")),
    )(y2, dy)


def _lru_bwd(dy, wo, p, h, wa, wx, ba, bx, lam, cw, cb, tm):
    t = dy.shape[0]
    nt = t // tm
    ng = tm // 8

    def body(dy_ref, wo_ref, lx_ref, halo_ref, lg_ref, h_ref, hhalo_ref, wa_ref, wx_ref, ba_ref, bx_ref,
             lam_ref, cw_ref, cb_ref,
             dp_ref, dwa_ref, dwx_ref, dba_ref, dbx_ref, dsp_ref, dcw_ref, dcb_ref,
             c_s, in_s, dh_s, dhc_s, afirst_s, dxc_s):
        i = pl.program_id(0)
        first = i == nt - 1

        @pl.when(i == 0)
        def _():
            for ref in (dwa_ref, dwx_ref, dba_ref, dbx_ref, dsp_ref, dcw_ref, dcb_ref, dhc_s, afirst_s, dxc_s):
                ref[...] = jnp.zeros_like(ref)

        sp = _softplus_neg(lam_ref[...])
        cwv = cw_ref[...]
        wav = wa_ref[...]
        wxv = wx_ref[...]
        halo = jnp.where(first, 0.0, halo_ref[...])
        xs, xc, r, ig, a, om = _lru_gates(lx_ref[...], halo, cwv, cb_ref[...], wav, wxv,
                                          ba_ref[...], bx_ref[...], sp)
        inv_mult = lax.rsqrt(om)
        mult = om * inv_mult
        d_y = _dot_nt(dy_ref[...], wo_ref[...])
        lg = lg_ref[...]
        sl = _sigmoid(lg)
        hv = h_ref[...]
        d_lg = d_y * hv * (sl * (1.0 + lg * (1.0 - sl)))

        c_s[...] = _shift_up(a, afirst_s[...], 1)
        in_s[...] = d_y * (lg * sl)
        row8 = _rows((8, D_MODEL))

        def group(gg, carry):
            rows = pl.ds(pl.multiple_of((ng - 1 - gg) * 8, 8), 8)
            cv = c_s[rows, :]
            uv = in_s[rows, :]
            for s in (1, 2, 4):
                c_sh = jnp.where(row8 < 8 - s, pltpu.roll(cv, 8 - s, 0), 1.0)
                u_sh = jnp.where(row8 < 8 - s, pltpu.roll(uv, 8 - s, 0), 0.0)
                uv = cv * u_sh + uv
                cv = cv * c_sh
            hh = cv * carry + uv
            dh_s[rows, :] = hh
            return jnp.broadcast_to(hh[0:1, :], (8, D_MODEL))

        dhc_s[...] = lax.fori_loop(0, ng, group, dhc_s[...])
        afirst_s[...] = a[0:8]
        dh = dh_s[...]

        h_prev = _shift_down(hv, jnp.where(first, 0.0, hhalo_ref[...]), 1)
        gx = ig * xc
        d_mult = dh * gx
        d_i = dh * (mult * xc)
        d_xc = dh * (mult * ig)
        d_la = (dh * h_prev) * a - d_mult * ((a * a) * inv_mult)
        dsp_ref[...] += jnp.sum(d_la * r, axis=0, keepdims=True) * (-LRU_C)
        d_zr = (d_la * (-LRU_C * sp)) * (r * (1.0 - r))
        d_zi = d_i * (ig * (1.0 - ig))
        dba_ref[...] += jnp.sum(d_zr, axis=0, keepdims=True)
        dbx_ref[...] += jnp.sum(d_zi, axis=0, keepdims=True)
        back = []
        for n in range(LRU_BLOCKS):
            ln = slice(n * LRU_BW, (n + 1) * LRU_BW)
            xb = _mx(xc[:, ln])
            zr_n = _mx(d_zr[:, ln])
            zi_n = _mx(d_zi[:, ln])
            dwa_ref[n] += _dot_tn(xb, zr_n)
            dwx_ref[n] += _dot_tn(xb, zi_n)
            back.append(_dot_nt(zr_n, wav[n]) + _dot_nt(zi_n, wxv[n]))
        d_xc = d_xc + jnp.concatenate(back, axis=1)
        dcb_ref[...] += jnp.sum(d_xc, axis=0, keepdims=True)
        for k in range(4):
            dcw_ref[k:k + 1, :] += jnp.sum(d_xc * xs[k], axis=0, keepdims=True)
        head = dxc_s[...]
        d_lx = cwv[3:4] * d_xc
        for k in range(3):
            d_lx = d_lx + cwv[k:k + 1] * _shift_up(d_xc, head, 3 - k)
        dxc_s[...] = d_xc[0:8]
        dp_ref[0] = d_lx.astype(_MXU_DTYPE)
        dp_ref[1] = d_lg.astype(_MXU_DTYPE)

    rev = lambda i: nt - 1 - i
    seg = lambda s: pl.BlockSpec((None, tm, D_MODEL), lambda i: (s, rev(i), 0))
    halo_row = lambda i: jnp.maximum(rev(i) * (tm // 8) - 1, 0)
    row = pl.BlockSpec((tm, D_MODEL), lambda i: (rev(i), 0))
    vec = _full((1, D_MODEL))
    wblk = _full((LRU_BLOCKS, LRU_BW, LRU_BW))
    return pl.pallas_call(
        body, name="lru_bwd", grid=(nt,),
        in_specs=[row, pl.BlockSpec((D_MODEL, D_MODEL), lambda i: (0, 0)),
                  seg(0), pl.BlockSpec((None, 8, D_MODEL), lambda i: (0, halo_row(i), 0)), seg(1),
                  row, pl.BlockSpec((8, D_MODEL), lambda i: (halo_row(i), 0)),
                  wblk, wblk, vec, vec, vec, _full((4, D_MODEL)), vec],
        out_specs=[pl.BlockSpec((2, tm, D_MODEL), lambda i: (2, rev(i), 0)),
                   wblk, wblk, vec, vec, vec, _full((4, D_MODEL)), vec],
        out_shape=[jax.ShapeDtypeStruct((N_SEG, t, D_MODEL), _MXU_DTYPE),
                   jax.ShapeDtypeStruct((LRU_BLOCKS, LRU_BW, LRU_BW), F32),
                   jax.ShapeDtypeStruct((LRU_BLOCKS, LRU_BW, LRU_BW), F32),
                   jax.ShapeDtypeStruct((1, D_MODEL), F32), jax.ShapeDtypeStruct((1, D_MODEL), F32),
                   jax.ShapeDtypeStruct((1, D_MODEL), F32), jax.ShapeDtypeStruct((4, D_MODEL), F32),
                   jax.ShapeDtypeStruct((1, D_MODEL), F32)],
        scratch_shapes=[pltpu.VMEM((tm, D_MODEL), F32), pltpu.VMEM((tm, D_MODEL), F32),
                        pltpu.VMEM((tm, D_MODEL), F32), pltpu.VMEM((8, D_MODEL), F32),
                        pltpu.VMEM((8, D_MODEL), F32), pltpu.VMEM((8, D_MODEL), F32)],
        compiler_params=_seq_params(),
    )(dy, wo, p, p, p, h, h, wa, wx, ba, bx, lam, cw, cb)


def _hgrn_bwd(dy, wo, p, o, states, logits, gw, dp, tm):
    t = dy.shape[0]
    nt = t // tm
    nc = tm // CHUNK

    def body(dy_ref, wo_ref, qf_ref, vh_ref, o_ref, st_ref, lg_ref, gw_ref, dp_in,
             dp_ref, dgw_ref, dlb_ref, dyh_s, dstate, snext):
        del dp_in
        i = pl.program_id(0)

        @pl.when(i == 0)
        def _():
            for ref in (dgw_ref, dlb_ref, dstate, snext):
                ref[...] = jnp.zeros_like(ref)

        dyh_s[...] = _dot_nt(dy_ref[...], wo_ref[...])
        l0 = lg_ref[0:1, :]
        l1 = lg_ref[1:2, :]
        lb = 1.0 / (1.0 + jnp.exp(l1 - l0))
        gwv = gw_ref[...]
        causal = _rows((CHUNK, CHUNK)) >= _cols((CHUNK, CHUNK))
        wide = 2 * HEAD_D
        zeros_h = jnp.zeros((CHUNK, HEAD_D), _MXU_DTYPE)

        for c in reversed(range(nc)):
            rows = pl.ds(c * CHUNK, CHUNK)
            q = qf_ref[0, rows, :]
            hg = vh_ref[1, rows, :]
            ov = o_ref[rows, :]
            dyh = dyh_s[rows, :]
            sig, f, sq, em1, em2, qm, km, ebm, ecm, ebc = _hgrn_prep(q, qf_ref[1, rows, :], lb)
            ke = _mx(km.astype(F32) * ecm)
            vb = _mx(vh_ref[0, rows, :])
            sg = _sigmoid(hg)
            d_on = dyh * (hg * sg)
            x_p, y_p, dv_p, n_p, ex_p = [], [], [], [], []
            for h in range(HEADS):
                ln = slice(h * HEAD_D, (h + 1) * HEAD_D)
                o_h = ov[:, ln]
                rstd = lax.rsqrt(jnp.mean(o_h * o_h, axis=-1, keepdims=True) + EPS)
                n = o_h * rstd
                dn = d_on[:, ln] * gwv[:, ln]
                d_o = rstd * (dn - n * jnp.mean(dn * n, axis=-1, keepdims=True))
                d_ob = _mx(d_o)
                d_ot = _mx(d_o.T)
                s0 = st_ref[c, h]
                ds = dstate[h]
                both = _dot_nt(jnp.concatenate([_pad_lanes([qm[:, ln]], wide), _pad_lanes([zeros_h, d_ob], wide)], axis=0),
                               jnp.concatenate([km[:, ln], vb[:, ln]], axis=1))
                att = jnp.where(causal, both[0:CHUNK], 0.0)
                d_att = jnp.where(causal, both[CHUNK:2 * CHUNK], 0.0)
                x_p.append(jnp.dot(_pad_lanes([d_ob, _mx(d_att)], wide),
                                   _pad_rows([_mx(s0 * ebm[:, ln]), km[:, ln]], wide), preferred_element_type=F32))
                y_p.append(jnp.dot(_pad_lanes([vb[:, ln], _mx(d_att.T)], wide),
                                   _pad_rows([_mx(ds * ecm[:, ln]), qm[:, ln]], wide), preferred_element_type=F32))
                dv_p.append(_dot_nt(_pad_lanes([ke[:, ln], _mx(att.T)], wide), _pad_lanes([_mx(ds), d_ot], wide)))
                ex_p.append(jnp.sum(ds * snext[h], axis=0, keepdims=True))
                dstate[h] = ds * ebc[:, ln] + jnp.dot(d_ot, qm[:, ln], preferred_element_type=F32) * ebm[:, ln]
                snext[h] = s0
                n_p.append(n)
            cat = lambda parts: jnp.concatenate(parts, axis=1)
            x_all, y_all, n_all = cat(x_p), cat(y_p), cat(n_p)
            dg = _cumsum_rows(qm.astype(F32) * x_all - km.astype(F32) * y_all, reverse=True) + cat(ex_p)
            df = dg / f - em2 * y_all
            dgw_ref[...] += jnp.sum(d_on * n_all, axis=0, keepdims=True)
            dlb_ref[...] += jnp.sum(df * (1.0 - sig), axis=0, keepdims=True)
            dp_ref[0, rows, :] = ((em1 * x_all) * (sq * (1.0 + q * (1.0 - sq)))).astype(_MXU_DTYPE)
            dp_ref[1, rows, :] = (df * ((1.0 - lb) * sig * (1.0 - sig))).astype(_MXU_DTYPE)
            dp_ref[2, rows, :] = cat(dv_p).astype(_MXU_DTYPE)
            dp_ref[3, rows, :] = (dyh * (n_all * gwv) * (sg * (1.0 + hg * (1.0 - sg)))).astype(_MXU_DTYPE)

    rev = lambda i: nt - 1 - i
    pair = lambda s: pl.BlockSpec((2, tm, D_MODEL), lambda i: (s, rev(i), 0))
    row = pl.BlockSpec((tm, D_MODEL), lambda i: (rev(i), 0))
    vec = _full((1, D_MODEL))
    return pl.pallas_call(
        body, name="hgrn_bwd", grid=(nt,),
        in_specs=[row, pl.BlockSpec((D_MODEL, D_MODEL), lambda i: (1, 0)), pair(1), pair(2), row,
                  pl.BlockSpec((nc, HEADS, HEAD_D, HEAD_D), lambda i: (rev(i), 0, 0, 0)),
                  _full((2, D_MODEL)), vec, pl.BlockSpec(memory_space=pl.ANY)],
        out_specs=[pl.BlockSpec((4, tm, D_MODEL), lambda i: (0, rev(i), 0)), vec, vec],
        out_shape=[jax.ShapeDtypeStruct((N_SEG, t, D_MODEL), _MXU_DTYPE),
                   jax.ShapeDtypeStruct((1, D_MODEL), F32), jax.ShapeDtypeStruct((1, D_MODEL), F32)],
        scratch_shapes=[pltpu.VMEM((tm, D_MODEL), F32), pltpu.VMEM((HEADS, HEAD_D, HEAD_D), F32),
                        pltpu.VMEM((HEADS, HEAD_D, HEAD_D), F32)],
        input_output_aliases={8: 0},
        compiler_params=_seq_params(),
    )(dy, wo, p, p, o, states, logits, gw, dp)


def _dp_unit(m):
    return ((m // 2 + 2) % N_SEG) * 2 + m % 2


def _dw_in(ut, dp, tt):
    t = ut.shape[1]
    hr = D_MODEL // 2

    def body(ut_ref, dp_ref, g_ref):
        @pl.when(pl.program_id(1) == 0)
        def _():
            g_ref[...] = jnp.zeros_like(g_ref)

        r = jnp.dot(ut_ref[...], dp_ref[...], preferred_element_type=F32)
        for h in range(2):
            for half in range(2):
                g_ref[h, half] += r[h * hr:(h + 1) * hr, half * UNIT:(half + 1) * UNIT]

    return pl.pallas_call(
        body, name="dw_in", grid=(N_SEG, t // tt),
        in_specs=[pl.BlockSpec((D_MODEL, tt), lambda s, k: (0, k)),
                  pl.BlockSpec((None, tt, D_MODEL), lambda s, k: (s, k, 0))],
        out_specs=pl.BlockSpec((2, 2, hr, UNIT), lambda s, k: (0, (s + 2) % N_SEG, 0, 0)),
        out_shape=jax.ShapeDtypeStruct((2, N_UNIT, hr, UNIT), F32),
        compiler_params=pltpu.CompilerParams(dimension_semantics=("arbitrary", "arbitrary")),
    )(ut, dp)


def _dx(dp, wg, x, dout, wpre, tm, parts, scatter):
    t = x.shape[0]
    n = len(parts)
    nt = t // tm

    def body(*refs):
        dp_ref, w_ref, x_ref, dout_ref, wpre_ref = refs[:5]
        ins = refs[5:5 + n]
        gx_ref, dw_ref = refs[5 + n:7 + n]
        outs = refs[7 + n:7 + 2 * n]
        send_sems, recv_sems = refs[7 + 2 * n:]
        i = pl.program_id(0)

        @pl.when(i == 0)
        def _():
            dw_ref[...] = jnp.zeros_like(dw_ref)
            for cp in _chip_copies(ins, outs, send_sems, recv_sems, scatter):
                cp.start()

        du = None
        for m in range(N_UNIT):
            part = _dot_nt(dp_ref[m // 2, :, (m % 2) * UNIT:(m % 2 + 1) * UNIT], w_ref[_dp_unit(m)])
            du = part if du is None else du + part
        xv = x_ref[...]
        rstd = lax.rsqrt(jnp.mean(xv * xv, axis=-1, keepdims=True) + EPS)
        xn = xv * rstd
        dw_ref[...] += jnp.sum(du * xn, axis=0, keepdims=True)
        dn = du * wpre_ref[...]
        gx_ref[...] = dout_ref[...] + rstd * (dn - xn * jnp.mean(dn * xn, axis=-1, keepdims=True))

        @pl.when(i == nt - 1)
        def _():
            copies = _chip_copies(ins, outs, send_sems, recv_sems, scatter)
            for cp in copies:
                cp.wait_recv()
            for cp in copies:
                cp.wait_send()

    row = pl.BlockSpec((tm, D_MODEL), lambda i: (i, 0))
    any_spec = pl.BlockSpec(memory_space=pl.ANY)
    res = pl.pallas_call(
        body, name="dx", grid=(nt,),
        in_specs=[pl.BlockSpec((N_SEG, tm, D_MODEL), lambda i: (0, i, 0)), _resident((N_UNIT, D_MODEL, UNIT)),
                  row, row, _full((1, D_MODEL))] + [any_spec] * n,
        out_specs=[row, _full((1, D_MODEL))] + [any_spec] * n,
        out_shape=[jax.ShapeDtypeStruct((t, D_MODEL), F32), jax.ShapeDtypeStruct((1, D_MODEL), F32)]
        + _chip_exchange_shapes(parts, scatter),
        scratch_shapes=[pltpu.SemaphoreType.DMA((max(3 * n, 1),)), pltpu.SemaphoreType.DMA((max(3 * n, 1),))],
        compiler_params=_seq_params(),
    )(dp, wg, x, dout, wpre, *parts)
    return res[0], res[1], res[2:]


CHUNK_BYTES = 256 * 1024
MAX_CHUNKS = 16


def _n_chunks(shape, dtype, rows):
    nbytes = jnp.dtype(dtype).itemsize
    for d in shape:
        nbytes *= d
    k = max(1, min(MAX_CHUNKS, rows, nbytes // CHUNK_BYTES))
    while rows % k:
        k -= 1
    return k


def _chip_exchange_shapes(arrays, scatter):
    return [jax.ShapeDtypeStruct((3,) + tuple(a.shape[1:] if sc else a.shape), a.dtype)
            for a, sc in zip(arrays, scatter)]


def _chip_copies(ins, outs, send_sems, recv_sems, scatter):
    if not ins:
        return []
    x, y, c = lax.axis_index("x"), lax.axis_index("y"), lax.axis_index("c")
    peers = [(1 - x, y), (x, 1 - y), (1 - x, 1 - y)]
    copies = []
    for a in range(len(ins)):
        for j, (px, py) in enumerate(peers):
            copies.append(pltpu.make_async_remote_copy(
                src_ref=ins[a].at[2 * px + py] if scatter[a] else ins[a], dst_ref=outs[a].at[j],
                send_sem=send_sems.at[a * 3 + j], recv_sem=recv_sems.at[a * 3 + j],
                device_id=(px, py, c), device_id_type=MESH))
    return copies


def _core_swap(arrays, name, halved):
    n = len(arrays)
    shapes = [a.shape[1:] if halved else a.shape for a in arrays]
    ks = [_n_chunks(s, a.dtype, s[0]) for s, a in zip(shapes, arrays)]
    offs = [sum(ks[:a]) for a in range(n)]

    def body(*refs):
        ins, outs = refs[:n], refs[n:2 * n]
        send_sems, recv_sems = refs[2 * n:]
        x, y, c = lax.axis_index("x"), lax.axis_index("y"), lax.axis_index("c")

        def copy(a, j):
            step = shapes[a][0] // ks[a]
            rows = pl.ds(j * step, step)
            return pltpu.make_async_remote_copy(
                src_ref=ins[a].at[1 - c, rows] if halved else ins[a].at[rows], dst_ref=outs[a].at[rows],
                send_sem=send_sems.at[offs[a] + j], recv_sem=recv_sems.at[offs[a] + j],
                device_id=(x, y, 1 - c), device_id_type=MESH)

        pieces = [(a, j) for a in range(n) for j in range(ks[a])]
        for a, j in pieces:
            copy(a, j).start()
        for a, j in pieces:
            copy(a, j).wait_recv()
        for a, j in pieces:
            copy(a, j).wait_send()

    any_spec = pl.BlockSpec(memory_space=pl.ANY)
    return pl.pallas_call(
        body, name=name,
        in_specs=[any_spec] * n, out_specs=[any_spec] * n,
        out_shape=[jax.ShapeDtypeStruct(s, a.dtype) for s, a in zip(shapes, arrays)],
        scratch_shapes=[pltpu.SemaphoreType.DMA((sum(ks),)), pltpu.SemaphoreType.DMA((sum(ks),))],
    )(*arrays)


def _finish_exchange(arrays, small):
    n = len(arrays)
    ks = [_n_chunks(a.shape, a.dtype, a.shape[0]) for a in arrays]
    offs = [sum(ks[:a]) for a in range(n)]
    total = sum(ks)

    def body(*refs):
        ins, small_ref = refs[:n], refs[n]
        outs, gathered = refs[n + 1:2 * n + 1], refs[2 * n + 1]
        send_sems, recv_sems = refs[2 * n + 2:]
        x, y, c = lax.axis_index("x"), lax.axis_index("y"), lax.axis_index("c")
        copies = []
        for a in range(n):
            step = arrays[a].shape[0] // ks[a]
            for j in range(ks[a]):
                rows = pl.ds(j * step, step)
                copies.append(pltpu.make_async_remote_copy(
                    src_ref=ins[a].at[rows], dst_ref=outs[a].at[rows],
                    send_sem=send_sems.at[offs[a] + j], recv_sem=recv_sems.at[offs[a] + j],
                    device_id=(x, y, 1 - c), device_id_type=MESH))
        for slot in range(7):
            dx, dy, dc = (slot + 1) % 2, ((slot + 1) // 2) % 2, (slot + 1) // 4
            copies.append(pltpu.make_async_remote_copy(
                src_ref=small_ref, dst_ref=gathered.at[slot],
                send_sem=send_sems.at[total + slot], recv_sem=recv_sems.at[total + slot],
                device_id=(x + dx - 2 * x * dx, y + dy - 2 * y * dy, c + dc - 2 * c * dc), device_id_type=MESH))
        for cp in copies:
            cp.start()
        for cp in copies:
            cp.wait_recv()
        for cp in copies:
            cp.wait_send()

    any_spec = pl.BlockSpec(memory_space=pl.ANY)
    res = pl.pallas_call(
        body, name="reduce_cores",
        in_specs=[any_spec] * (n + 1), out_specs=[any_spec] * (n + 1),
        out_shape=[jax.ShapeDtypeStruct(a.shape, a.dtype) for a in arrays]
        + [jax.ShapeDtypeStruct((7,) + small.shape, small.dtype)],
        scratch_shapes=[pltpu.SemaphoreType.DMA((total + 7,)), pltpu.SemaphoreType.DMA((total + 7,))],
    )(*arrays, small)
    return res[:n], res[n]


def _sum_own_half(g, b, name, out_dtype, tr=512):
    _, rows, cols = g.shape
    tr = min(tr, rows)
    core = lax.axis_index("c").reshape(1)

    def body(c_ref, g_ref, b_ref, o_ref):
        del c_ref
        o_ref[...] = (g_ref[...] + b_ref[...]).astype(out_dtype)

    return pl.pallas_call(
        body, name=name,
        grid_spec=pltpu.PrefetchScalarGridSpec(
            num_scalar_prefetch=1, grid=(rows // tr,),
            in_specs=[pl.BlockSpec((None, tr, cols), lambda i, c_ref: (c_ref[0], i, 0)),
                      pl.BlockSpec((tr, cols), lambda i, c_ref: (i, 0))],
            out_specs=pl.BlockSpec((tr, cols), lambda i, c_ref: (i, 0))),
        out_shape=jax.ShapeDtypeStruct((rows, cols), out_dtype),
        compiler_params=pltpu.CompilerParams(dimension_semantics=("parallel",)),
    )(core, g, b)


def _sum_chips(own, r, slabbed, name, tr=512):
    _, rows, cols = r.shape
    tr = min(tr, rows)
    chip = (2 * lax.axis_index("x") + lax.axis_index("y")).reshape(1)

    def body(c_ref, own_ref, r_ref, o_ref):
        del c_ref
        f = lambda val: val.astype(F32)
        o_ref[...] = (f(own_ref[...]) + f(r_ref[0])) + (f(r_ref[1]) + f(r_ref[2]))

    if slabbed:
        own_spec = pl.BlockSpec((None, tr, cols), lambda i, c_ref: (c_ref[0], i, 0))
    else:
        own_spec = pl.BlockSpec((tr, cols), lambda i, c_ref: (i, 0))
    return pl.pallas_call(
        body, name=name,
        grid_spec=pltpu.PrefetchScalarGridSpec(
            num_scalar_prefetch=1, grid=(rows // tr,),
            in_specs=[own_spec, pl.BlockSpec((3, tr, cols), lambda i, c_ref: (0, i, 0))],
            out_specs=pl.BlockSpec((tr, cols), lambda i, c_ref: (i, 0))),
        out_shape=jax.ShapeDtypeStruct((rows, cols), F32),
        compiler_params=pltpu.CompilerParams(dimension_semantics=("parallel",)),
    )(chip, own, r)


def _adamw_big(own, other, w, m, v, name, grid, g_block, g_index, w_block, w_index):
    core = lax.axis_index("c").reshape(1)
    half_axis = len(grid) - 1

    def body(c_ref, own_ref, oth_ref, w_ref, m_ref, v_ref, g_ref, d_ref, nm_ref, nv_ref):
        mine = pl.program_id(half_axis) == c_ref[0]
        gv = jnp.where(mine, own_ref[...], oth_ref[...])
        g_ref[...] = gv
        d_ref[...], nm_ref[...], nv_ref[...] = _adamw(w_ref[...], gv, m_ref[...], v_ref[...])

    g_spec = pl.BlockSpec(g_block, lambda *a: g_index(*a[:half_axis]))
    w_spec = pl.BlockSpec(w_block, lambda *a: w_index(*a[:-1]))
    shp = jax.ShapeDtypeStruct(w.shape, F32)
    return pl.pallas_call(
        body, name=name,
        grid_spec=pltpu.PrefetchScalarGridSpec(
            num_scalar_prefetch=1, grid=grid,
            in_specs=[g_spec, g_spec, w_spec, w_spec, w_spec], out_specs=[w_spec] * 4),
        out_shape=[shp] * 4,
        compiler_params=pltpu.CompilerParams(dimension_semantics=("parallel",) * len(grid)),
    )(core, own, other, w, m, v)


def _hand_over(a):
    def body(a_ref, o_ref):
        del a_ref, o_ref

    spec = pl.BlockSpec(memory_space=pl.ANY)
    return pl.pallas_call(body, name="hand_over", in_specs=[spec], out_specs=spec,
                          out_shape=jax.ShapeDtypeStruct(a.shape, a.dtype), input_output_aliases={0: 0})(a)


def _small_finish(own, others, lam, logits):
    def body(a_ref, b_ref, lam_ref, lg_ref, o_ref):
        g = (((a_ref[...] + b_ref[0]) + (b_ref[1] + b_ref[2]))
             + ((b_ref[3] + b_ref[4]) + (b_ref[5] + b_ref[6])))
        o_ref[...] = g
        o_ref[8:9, :] = g[8:9, :] * (-_sigmoid(-lam_ref[...]))
        lb = 1.0 / (1.0 + jnp.exp(lg_ref[1:2, :] - lg_ref[0:1, :]))
        d0 = g[9:10, :] * (lb * (1.0 - lb))
        o_ref[9:10, :] = d0
        o_ref[10:11, :] = -d0

    return pl.pallas_call(
        body, name="small_finish",
        in_specs=[_full((SMALL_ROWS, D_MODEL)), _full((7, SMALL_ROWS, D_MODEL)), _full((1, D_MODEL)),
                  _full((2, D_MODEL))],
        out_specs=_full((SMALL_ROWS, D_MODEL)),
        out_shape=jax.ShapeDtypeStruct((SMALL_ROWS, D_MODEL), F32),
    )(own, others, lam, logits)


def _adamw_small(ws, gs, ms, vs):
    n = len(ws)

    def body(*refs):
        w_r, g_r, m_r, v_r = refs[:n], refs[n:2 * n], refs[2 * n:3 * n], refs[3 * n:4 * n]
        d_o, m_o, v_o = refs[4 * n:5 * n], refs[5 * n:6 * n], refs[6 * n:7 * n]
        for j in range(n):
            d_o[j][...], m_o[j][...], v_o[j][...] = _adamw(w_r[j][...], g_r[j][...], m_r[j][...], v_r[j][...])

    specs = [_full(w.shape) for w in ws]
    shapes = [jax.ShapeDtypeStruct(w.shape, F32) for w in ws]
    outs = pl.pallas_call(
        body, name="adamw_small",
        in_specs=specs * 4, out_specs=specs * 3, out_shape=shapes * 3,
    )(*ws, *gs, *ms, *vs)
    return outs[:n], outs[n:2 * n], outs[2 * n:]


def _local_step(x, tgt, ut, p, pre_w, wg, cw, cb, wa, wx, ba, bx, lam, logits, gw, wo, post_w, tm, tmm,
                prepare=None):
    h, y2 = _lru_fwd(p, wa, wx, ba, bx, lam, cw, cb, tm)
    y2, o, states = _hgrn_fwd(p, logits, gw, y2, tm)
    dout, dy, sq, d_post = _out_fused(y2, wo, x, tgt, post_w, tmm)
    g_out = _dw_out(y2, dy, min(2048, x.shape[0]))
    dp, d_wa, d_wx, d_ba, d_bx, d_sp, d_cw, d_cb = _lru_bwd(dy, wo, p, h, wa, wx, ba, bx, lam, cw, cb, tm)
    dp, d_gw, d_lb = _hgrn_bwd(dy, wo, p, o, states, logits, gw, dp, tm)
    g_in = _dw_in(ut, dp, min(2048, x.shape[0]))
    parts, scatter = prepare(g_in, g_out, d_wa, d_wx) if prepare is not None else ([], [])
    grad_x, d_pre, received = _dx(dp, wg, x, dout, pre_w, tm, parts, scatter)
    small = jnp.concatenate([d_pre, d_cw, d_cb, d_ba, d_bx, d_sp, d_lb,
                             jnp.zeros((1, D_MODEL), F32), d_gw, d_post, sq,
                             jnp.zeros((SMALL_ROWS - 14, D_MODEL), F32)], axis=0)
    return grad_x, g_in, g_out, d_wa, d_wx, small, parts, received


def kernel(x, pre_norm_w, w_in, conv_w, conv_b, lru_w_a, lru_b_a, lru_w_x, lru_b_x, lru_lambda, hgrn_lb_logits, hgrn_gnorm_w, w_out, post_norm_w, loss_target, m_pre_norm_w, m_w_in, m_conv_w, m_conv_b, m_lru_w_a, m_lru_b_a, m_lru_w_x, m_lru_b_x, m_lru_lambda, m_hgrn_lb_logits, m_hgrn_gnorm_w, m_w_out, m_post_norm_w, v_pre_norm_w, v_w_in, v_conv_w, v_conv_b, v_lru_w_a, v_lru_b_a, v_lru_w_x, v_lru_b_x, v_lru_lambda, v_hgrn_lb_logits, v_hgrn_gnorm_w, v_w_out, v_post_norm_w):
    t = x.shape[1]
    tm = min(256, t)
    tmm = min(512, t)
    chip = 2 * lax.axis_index("x") + lax.axis_index("y")

    win_units = w_in[0].astype(_MXU_DTYPE).reshape(D_MODEL, 3, UNIT).transpose(1, 0, 2)
    small_w = jnp.concatenate([conv_w[0], lru_b_a[0], lru_b_x[0]], axis=1)
    ut, p, wg4, (wo4, wa4, wx4), sw4 = _in_proj_gather(
        x[0], pre_norm_w, win_units,
        [w_out[0].astype(_MXU_DTYPE).reshape(2, 256, D_MODEL), lru_w_a[0].astype(_MXU_DTYPE).reshape(2, 128, LRU_BW),
         lru_w_x[0].astype(_MXU_DTYPE).reshape(2, 128, LRU_BW)],
        small_w, min(1024, t))
    wg = wg4.reshape(N_UNIT, D_MODEL, UNIT)
    wo = wo4.reshape(2 * D_MODEL, D_MODEL)
    by_block = lambda w4: w4.reshape(N_CHIPS, LRU_BLOCKS, 64, LRU_BW).transpose(1, 0, 2, 3).reshape(
        LRU_BLOCKS, LRU_BW, LRU_BW)
    wa, wx = by_block(wa4), by_block(wx4)
    sw4 = sw4.reshape(N_CHIPS, 4, 384)
    cw = sw4[:, :, 0:256].transpose(1, 0, 2).reshape(4, D_MODEL)
    ba = sw4[:, :, 256:320].transpose(1, 0, 2).reshape(1, D_MODEL)
    bx = sw4[:, :, 320:384].transpose(1, 0, 2).reshape(1, D_MODEL)

    def lru_layout(g):
        g = g.reshape(2, 2, N_CHIPS, 64, LRU_BW).transpose(0, 2, 1, 3, 4)
        return g.reshape(2, N_CHIPS * 128, LRU_BW)

    def prepare(g_in, g_out, d_wa, d_wx):
        g_in = g_in.reshape(2, N_UNIT * 512, UNIT)
        g_out = g_out.reshape(2, N_CHIPS * 256, D_MODEL)
        g_wa, g_wx = lru_layout(d_wa), lru_layout(d_wx)
        b_in, b_out, b_wa, b_wx = _core_swap([g_in, g_out, g_wa, g_wx], "reduce_swap", halved=True)
        parts = [_sum_own_half(g_in, b_in, "presum_in", _WIRE_DTYPE).reshape(N_CHIPS, 3 * 512, UNIT),
                 _sum_own_half(g_out, b_out, "presum_out", _WIRE_DTYPE).reshape(N_CHIPS, 256, D_MODEL),
                 _sum_own_half(g_wa, b_wa, "presum_wa", _WIRE_DTYPE).reshape(N_CHIPS, 128, LRU_BW),
                 _sum_own_half(g_wx, b_wx, "presum_wx", _WIRE_DTYPE).reshape(N_CHIPS, 128, LRU_BW)]
        return parts, [True] * 4

    grad_x, _, _, _, _, small, (p_in, p_out, p_wa, p_wx), (r_in, r_out, r_wa, r_wx) = _local_step(
        x[0], loss_target[0], ut, p, pre_norm_w, wg, cw, conv_b, wa, wx, ba, bx, lru_lambda, hgrn_lb_logits,
        hgrn_gnorm_w, wo, post_norm_w, tm, tmm, prepare)
    s_in = _sum_chips(p_in, r_in, True, "sum_in")
    s_out = _sum_chips(p_out, r_out, True, "sum_out")
    s_wa = _sum_chips(p_wa, r_wa, True, "sum_wa")
    s_wx = _sum_chips(p_wx, r_wx, True, "sum_wx")
    (o_in, o_out, o_wa, o_wx), small_others = _finish_exchange([s_in, s_out, s_wa, s_wx], small)

    g_w_in, d_w_in, nm_w_in, nv_w_in = _adamw_big(
        s_in.reshape(3, 512, UNIT), o_in.reshape(3, 512, UNIT), w_in[0], m_w_in[0], v_w_in[0], "adamw_w_in", (3, 2),
        (None, 512, UNIT), lambda k: (k, 0, 0), (512, UNIT), lambda k, h: (h, k))
    g_w_out, d_w_out, nm_w_out, nv_w_out = _adamw_big(
        s_out, o_out, w_out[0], m_w_out[0], v_w_out[0], "adamw_w_out", (2,),
        (256, D_MODEL), lambda: (0, 0), (256, D_MODEL), lambda h: (h, 0))
    sq2 = lambda a: a.reshape(LRU_BW, LRU_BW)
    lru_specs = ((2,), (128, LRU_BW), lambda: (0, 0), (128, LRU_BW), lambda h: (h, 0))
    g_wa, d_wa2, nm_wa, nv_wa = _adamw_big(s_wa, o_wa, sq2(lru_w_a), sq2(m_lru_w_a), sq2(v_lru_w_a), "adamw_wa",
                                           *lru_specs)
    g_wx, d_wx2, nm_wx, nv_wx = _adamw_big(s_wx, o_wx, sq2(lru_w_x), sq2(m_lru_w_x), sq2(v_lru_w_x), "adamw_wx",
                                           *lru_specs)

    gs = _small_finish(small, small_others, lru_lambda, hgrn_lb_logits)
    loss = jnp.sum(gs[13]) * (0.5 / D_MODEL)
    g_pre = gs[0:1]
    g_cw = lax.dynamic_slice(gs[1:5], (0, chip * 256), (4, 256))
    g_cb = gs[5:6]
    g_ba = lax.dynamic_slice(gs[6].reshape(LRU_BLOCKS, N_CHIPS, 64), (0, chip, 0), (LRU_BLOCKS, 1, 64)).reshape(4, 64)
    g_bx = lax.dynamic_slice(gs[7].reshape(LRU_BLOCKS, N_CHIPS, 64), (0, chip, 0), (LRU_BLOCKS, 1, 64)).reshape(4, 64)
    g_lam = gs[8:9]
    g_lb = gs[9:11]
    g_gw = gs[11:12]
    g_post = gs[12:13]
    small_g = [g_pre, g_cw, g_cb, g_ba, g_bx, g_lam, g_lb, g_gw, g_post]
    two_d = lambda a: a.reshape(a.shape[-2:])
    small_w_list = [pre_norm_w, conv_w, conv_b, lru_b_a, lru_b_x, lru_lambda, hgrn_lb_logits, hgrn_gnorm_w, post_norm_w]
    small_m_list = [m_pre_norm_w, m_conv_w, m_conv_b, m_lru_b_a, m_lru_b_x, m_lru_lambda, m_hgrn_lb_logits,
                    m_hgrn_gnorm_w, m_post_norm_w]
    small_v_list = [v_pre_norm_w, v_conv_w, v_conv_b, v_lru_b_a, v_lru_b_x, v_lru_lambda, v_hgrn_lb_logits,
                    v_hgrn_gnorm_w, v_post_norm_w]
    sd, sm, sv = _adamw_small([two_d(a) for a in small_w_list], small_g,
                              [two_d(a) for a in small_m_list], [two_d(a) for a in small_v_list])

    def shaped(vals, refs):
        return [val.reshape(ref.shape) for val, ref in zip(vals, refs)]

    s_g = shaped(small_g, small_w_list)
    s_d = shaped(sd, small_w_list)
    s_m = shaped(sm, small_w_list)
    s_v = shaped(sv, small_w_list)

    def ordered(small, big_in, big_wa, big_wx, big_out):
        pre, cw_, cb_, ba_, bx_, lam_, lb_, gw_, post_ = small
        return [pre, big_in.reshape(w_in.shape), cw_, cb_, big_wa.reshape(lru_w_a.shape), ba_,
                big_wx.reshape(lru_w_x.shape), bx_, lam_, lb_, gw_, big_out.reshape(w_out.shape), post_]

    grads = ordered(s_g, g_w_in, g_wa, g_wx, g_w_out)
    deltas = ordered(s_d, d_w_in, d_wa2, d_wx2, d_w_out)
    new_m = ordered(s_m, nm_w_in, nm_wa, nm_wx, nm_w_out)
    new_v = ordered(s_v, nv_w_in, nv_wa, nv_wx, nv_w_out)
    return (loss, _hand_over(grad_x).reshape(x.shape), *grads, *deltas, *new_m, *new_v)
```

```python
import functools

import jax
import jax.numpy as jnp
from jax import lax
from jax.experimental import pallas as pl
from jax.experimental.pallas import tpu as pltpu

F32 = jnp.float32
BF16 = jnp.bfloat16
_MXU_DTYPE = jnp.bfloat16
_WIRE_DTYPE = jnp.bfloat16

D_MODEL = 1024
N_SEG = 6
UNIT = 512
N_UNIT = 12
LRU_BLOCKS = 4
LRU_BW = 256
LRU_C = 8.0
HEADS = 8
HEAD_D = 128
CHUNK = 64
EPS = 1e-6
N_CHIPS = 4
EXP_CLAMP = 80.0

ADAM_LR = 0.001
ADAM_B1 = 0.9
ADAM_B2 = 0.999
ADAM_EPS = 1e-08
ADAM_WD = 0.01
ADAM_STEP = 10

SMALL_ROWS = 16
FWD_PIECES = 4
MESH = pl.DeviceIdType.MESH


def _sigmoid(x):
    return 0.5 * jnp.tanh(0.5 * x) + 0.5


def _mx(x):
    return x.astype(_MXU_DTYPE)


def _dot(a, b):
    return jnp.dot(_mx(a), _mx(b), preferred_element_type=F32)


def _dot_nt(a, b):
    return lax.dot_general(_mx(a), _mx(b), (((1,), (1,)), ((), ())), preferred_element_type=F32)


def _dot_tn(a, b):
    return lax.dot_general(_mx(a), _mx(b), (((0,), (0,)), ((), ())), preferred_element_type=F32)


def _rows(shape):
    return lax.broadcasted_iota(jnp.int32, shape, 0)


def _cols(shape):
    return lax.broadcasted_iota(jnp.int32, shape, 1)


def _softplus_neg(lam):
    z = -lam
    e = jnp.exp(-jnp.abs(z))
    series = e * (1.0 - e * (0.5 - e * (1.0 / 3.0 - 0.25 * e)))
    return jnp.maximum(z, 0.0) + jnp.where(e < 1e-2, series, jnp.log(1.0 + e))


def _one_minus_sq(a, y):
    series = -y * (1.0 + y * (0.5 + y * (1.0 / 6.0)))
    return jnp.where(y > -0.01, series, 1.0 - a * a)


def _cumsum_rows(x, reverse=False):
    n = x.shape[0] // 8
    row8 = _rows((8, x.shape[1]))
    out = [None] * n
    carry = None
    for g in (reversed(range(n)) if reverse else range(n)):
        blk = x[8 * g:8 * g + 8]
        for s in (1, 2, 4):
            if reverse:
                blk = blk + jnp.where(row8 < 8 - s, pltpu.roll(blk, 8 - s, 0), 0.0)
            else:
                blk = blk + jnp.where(row8 >= s, pltpu.roll(blk, s, 0), 0.0)
        if carry is not None:
            blk = blk + carry
        carry = blk[0:1] if reverse else blk[7:8]
        out[g] = blk
    return jnp.concatenate(out, axis=0)


def _shift_down(x, halo, s):
    if s == 0:
        return x
    r = pltpu.roll(x, s, 0)
    top = jnp.where(_rows(halo.shape) < s, pltpu.roll(halo, s, 0), r[0:8])
    return jnp.concatenate([top, r[8:]], axis=0)


def _shift_up(x, head, s):
    if s == 0:
        return x
    n = x.shape[0]
    r = pltpu.roll(x, n - s, 0)
    bottom = jnp.where(_rows(head.shape) >= 8 - s, pltpu.roll(head, 8 - s, 0), r[n - 8:n])
    return jnp.concatenate([r[:n - 8], bottom], axis=0)


def _lru_gates(lx, halo, cw, cb, wa, wx, ba, bx, sp):
    xs = [_shift_down(lx, halo, 3 - k) for k in range(4)]
    xc = cb + cw[0:1] * xs[0] + cw[1:2] * xs[1] + cw[2:3] * xs[2] + cw[3:4] * xs[3]
    zr, zi = [], []
    for n in range(LRU_BLOCKS):
        xb = _mx(xc[:, n * LRU_BW:(n + 1) * LRU_BW])
        zr.append(jnp.dot(xb, wa[n], preferred_element_type=F32))
        zi.append(jnp.dot(xb, wx[n], preferred_element_type=F32))
    r = _sigmoid(jnp.concatenate(zr, axis=1) + ba)
    ig = _sigmoid(jnp.concatenate(zi, axis=1) + bx)
    la = (-LRU_C * sp) * r
    a = jnp.exp(la)
    om = _one_minus_sq(a, 2.0 * la)
    return xs, xc, r, ig, a, om


def _hgrn_prep(q, fr, lb):
    sig = _sigmoid(fr)
    f = lb + (1.0 - lb) * sig
    sq = _sigmoid(q)
    b = _cumsum_rows(jnp.log(f))
    bm = b[CHUNK // 2 - 1:CHUNK // 2]
    bc = b[CHUNK - 1:CHUNK]
    em1 = jnp.exp(jnp.minimum(b - bm, EXP_CLAMP))
    em2 = jnp.exp(jnp.minimum(bm - b, EXP_CLAMP))
    qm = _mx((q * sq) * em1)
    km = _mx((1.0 - f) * em2)
    return sig, f, sq, em1, em2, qm, km, jnp.exp(bm), jnp.exp(bc - bm), jnp.exp(bc)


def _seg(ref, s, rows=slice(None)):
    return jnp.concatenate([ref[2 * s, rows, :], ref[2 * s + 1, rows, :]], axis=1)


def _pad_lanes(parts, width):
    have = sum(p.shape[1] for p in parts)
    pad = [jnp.zeros((parts[0].shape[0], width - have), parts[0].dtype)] if width > have else []
    return jnp.concatenate(list(parts) + pad, axis=1)


def _pad_rows(parts, height):
    have = sum(p.shape[0] for p in parts)
    pad = [jnp.zeros((height - have, parts[0].shape[1]), parts[0].dtype)] if height > have else []
    return jnp.concatenate(list(parts) + pad, axis=0)


def _adamw(w, g, m, v):
    m = ADAM_B1 * m + (1.0 - ADAM_B1) * g
    v = ADAM_B2 * v + (1.0 - ADAM_B2) * jnp.square(g)
    m_hat = m / (1.0 - ADAM_B1 ** ADAM_STEP)
    v_hat = v / (1.0 - ADAM_B2 ** ADAM_STEP)
    delta = -ADAM_LR * (m_hat / (jnp.sqrt(v_hat) + ADAM_EPS) + ADAM_WD * w)
    return delta, m, v


def _full(shape):
    nd = len(shape)
    return pl.BlockSpec(shape, lambda *_: (0,) * nd)


def _seq_params():
    return pltpu.CompilerParams(dimension_semantics=("arbitrary",))


def _resident(shape):
    nd = len(shape)
    return pl.BlockSpec(shape, lambda *_: (0,) * nd, pipeline_mode=pl.Buffered(1))


def _in_proj_gather(x, w, win_units, halved, small_w, tm):
    t = x.shape[0]
    nt = t // tm
    n_h = len(halved)
    n_split = 3 + n_h
    n_items = n_split + 1
    hu = D_MODEL // 2
    sub = min(256, tm)
    chip = 2 * lax.axis_index("x") + lax.axis_index("y")
    order = jnp.stack([chip] + [jnp.bitwise_xor(chip, flip) for flip in (2, 1, 3)])

    def body(order_ref, x_ref, w_ref, win_ref, *rest):
        del order_ref
        h_ins, small_ref = rest[:n_h], rest[n_h]
        ut_ref, p_ref, wg_ref = rest[n_h + 1:n_h + 4]
        h_outs, sw_ref = rest[n_h + 4:2 * n_h + 4], rest[2 * n_h + 4]
        u_s, wbuf, send, recv, fsend, frecv, osend, orecv, wsem = rest[2 * n_h + 5:]
        g, i = pl.program_id(0), pl.program_id(1)
        mx, my, c = lax.axis_index("x"), lax.axis_index("y"), lax.axis_index("c")
        me = 2 * mx + my
        peers = [(1 - mx, my), (mx, 1 - my), (1 - mx, 1 - my)]
        qs = [2 * px + py for px, py in peers]
        sibling = (mx, my, 1 - c)

        def remote(src, dst, ssem, rsem, dev):
            return pltpu.make_async_remote_copy(src_ref=src, dst_ref=dst, send_sem=ssem, recv_sem=rsem,
                                                device_id=dev, device_id_type=MESH)

        def mine(item):
            if item < 3:
                return win_ref.at[item, pl.ds(c * hu, hu)]
            return h_ins[item - 3].at[c] if item < n_split else small_ref

        def piece(item, q, h):
            if item < 3:
                return wg_ref.at[q, item, pl.ds(h * hu, hu)]
            return h_outs[item - 3].at[q, h] if item < n_split else sw_ref.at[q]

        def fetch(item, jj, q):
            px, py = peers[jj]
            return remote(mine(item), piece(item, q, c), send.at[item * 3 + jj], recv.at[item * 3 + jj], (px, py, c))

        def forward(item, jj, h):
            whole = piece(item, qs[jj], h)
            step = whole.shape[0] // FWD_PIECES
            copies = []
            for k in range(FWD_PIECES):
                part = whole.at[pl.ds(k * step, step)]
                sem = (item * 3 + jj) * FWD_PIECES + k
                copies.append(remote(part, part, fsend.at[sem], frecv.at[sem], sibling))
            return copies

        def own(item):
            if item < 3:
                src, dst = win_ref.at[item], wg_ref.at[me, item]
            elif item < n_split:
                src, dst = h_ins[item - 3], h_outs[item - 3].at[me]
            else:
                src, dst = small_ref, sw_ref.at[me]
            return remote(src, dst, osend.at[item], orecv.at[item], sibling)

        def load_units(gg):
            src = lambda k: win_ref.at[k] if gg == 0 else wg_ref.at[qs[gg - 1], k]
            return [pltpu.make_async_copy(src(k), wbuf.at[gg % 2, k], wsem.at[(gg % 2) * 3 + k]) for k in range(3)]

        def at(gg, ii):
            return pl.when(jnp.logical_and(g == gg, i == ii))

        @at(0, 0)
        def _():
            for item in range(3):
                for jj in range(2):
                    fetch(item, jj, me).start()
            for item in range(n_items):
                own(item).start()
            for cp in load_units(0):
                cp.start()

        @at(1, 0)
        def _():
            for item in range(3):
                fetch(item, 2, me).start()

        @at(2, 0)
        def _():
            for item in range(3, n_items):
                for jj in range(3):
                    fetch(item, jj, me).start()

        @pl.when(g == 0)
        def _():
            xv = x_ref[...]
            rstd = lax.rsqrt(jnp.mean(xv * xv, axis=-1, keepdims=True) + EPS)
            u = xv * rstd * w_ref[...]
            ut_ref[...] = u.T.astype(_MXU_DTYPE)
            u_s[pl.ds(pl.multiple_of(i * tm, tm), tm), :] = _mx(u)

        for gg in range(N_CHIPS):
            @at(gg, 0)
            def _(gg=gg):
                for cp in load_units(gg):
                    cp.wait()

        slot = g % 2
        for r in range(tm // sub):
            ur = u_s[pl.ds(pl.multiple_of(i * tm + r * sub, sub), sub), :]
            for k in range(3):
                p_ref[k, r * sub:(r + 1) * sub, :] = jnp.dot(ur, wbuf[slot, k], preferred_element_type=F32)

        for gg in range(3):
            @at(gg, nt - 1 if gg == 0 else nt // 2)
            def _(gg=gg):
                for item in range(3):
                    fetch(item, gg, qs[gg]).wait_recv()
                    for cp in forward(item, gg, c):
                        cp.start()

            @at(gg, nt - 1)
            def _(gg=gg):
                for item in range(3):
                    for cp in forward(item, gg, 1 - c):
                        cp.wait_recv()
                for cp in load_units(gg + 1):
                    cp.start()

        @at(N_CHIPS - 1, nt - 1)
        def _():
            for item in range(3, n_items):
                for jj in range(3):
                    fetch(item, jj, qs[jj]).wait_recv()
                    if item < n_split:
                        for cp in forward(item, jj, c):
                            cp.start()
            for item in range(3, n_split):
                for jj in range(3):
                    for cp in forward(item, jj, 1 - c):
                        cp.wait_recv()
            for item in range(n_items):
                own(item).wait_recv()
            for item in range(n_items):
                for jj in range(3):
                    fetch(item, jj, me).wait_send()
                    if item < n_split:
                        for cp in forward(item, jj, c):
                            cp.wait_send()
                own(item).wait_send()

    any_spec = pl.BlockSpec(memory_space=pl.ANY)
    first_pass = lambda g, i: jnp.where(g == 0, i, nt - 1)
    res = pl.pallas_call(
        body, name="in_proj",
        grid_spec=pltpu.PrefetchScalarGridSpec(
            num_scalar_prefetch=1, grid=(N_CHIPS, nt),
            in_specs=[pl.BlockSpec((tm, D_MODEL), lambda g, i, o: (first_pass(g, i), 0)),
                      pl.BlockSpec((1, D_MODEL), lambda g, i, o: (0, 0))] + [any_spec] * (n_h + 2),
            out_specs=[pl.BlockSpec((D_MODEL, tm), lambda g, i, o: (0, first_pass(g, i))),
                       pl.BlockSpec((3, tm, UNIT), lambda g, i, o: (o[g], i, 0))]
            + [any_spec] * (n_h + 2),
            scratch_shapes=[pltpu.VMEM((t, D_MODEL), _MXU_DTYPE), pltpu.VMEM((2, 3, D_MODEL, UNIT), _MXU_DTYPE),
                            pltpu.SemaphoreType.DMA((3 * n_items,)), pltpu.SemaphoreType.DMA((3 * n_items,)),
                            pltpu.SemaphoreType.DMA((3 * n_split * FWD_PIECES,)),
                            pltpu.SemaphoreType.DMA((3 * n_split * FWD_PIECES,)),
                            pltpu.SemaphoreType.DMA((n_items,)), pltpu.SemaphoreType.DMA((n_items,)),
                            pltpu.SemaphoreType.DMA((6,))]),
        out_shape=[jax.ShapeDtypeStruct((D_MODEL, t), _MXU_DTYPE), jax.ShapeDtypeStruct((N_UNIT, t, UNIT), F32),
                   jax.ShapeDtypeStruct((N_CHIPS,) + win_units.shape, win_units.dtype)]
        + [jax.ShapeDtypeStruct((N_CHIPS,) + a.shape, a.dtype) for a in halved]
        + [jax.ShapeDtypeStruct((N_CHIPS,) + small_w.shape, small_w.dtype)],
        compiler_params=pltpu.CompilerParams(dimension_semantics=("arbitrary", "arbitrary")),
    )(order, x, w, win_units, *halved, small_w)
    return res[0], res[1], res[2], res[3:3 + n_h], res[3 + n_h]


def _lru_fwd(p, wa, wx, ba, bx, lam, cw, cb, tm):
    t = p.shape[1]
    ng = tm // 8

    def body(lx_ref, halo_ref, lg_ref, wa_ref, wx_ref, ba_ref, bx_ref, lam_ref, cw_ref, cb_ref,
             h_ref, y_ref, a_s, u_s, hc_s):
        i = pl.program_id(0)

        @pl.when(i == 0)
        def _():
            hc_s[...] = jnp.zeros_like(hc_s)

        sp = _softplus_neg(lam_ref[...])
        halo = jnp.where(i == 0, 0.0, _seg(halo_ref, 0))
        _, xc, _, ig, a, om = _lru_gates(_seg(lx_ref, 0), halo, cw_ref[...], cb_ref[...], wa_ref[...],
                                         wx_ref[...], ba_ref[...], bx_ref[...], sp)
        a_s[...] = a
        u_s[...] = jnp.sqrt(om) * (ig * xc)
        row8 = _rows((8, D_MODEL))

        def group(g, hc):
            rows = pl.ds(pl.multiple_of(g * 8, 8), 8)
            av = a_s[rows, :]
            uv = u_s[rows, :]
            for s in (1, 2, 4):
                a_sh = jnp.where(row8 >= s, pltpu.roll(av, s, 0), 1.0)
                u_sh = jnp.where(row8 >= s, pltpu.roll(uv, s, 0), 0.0)
                uv = av * u_sh + uv
                av = av * a_sh
            hh = av * hc + uv
            h_ref[rows, :] = hh
            return jnp.broadcast_to(hh[7:8, :], (8, D_MODEL))

        hc_s[...] = lax.fori_loop(0, ng, group, hc_s[...])
        lg = _seg(lg_ref, 0)
        y_ref[...] = (h_ref[...] * (lg * _sigmoid(lg))).astype(_MXU_DTYPE)

    seg = lambda s: pl.BlockSpec((2, tm, UNIT), lambda i: (s, i, 0))
    return pl.pallas_call(
        body, name="lru_fwd", grid=(t // tm,),
        in_specs=[seg(0),
                  pl.BlockSpec((2, 8, UNIT), lambda i: (0, jnp.maximum(i * (tm // 8) - 1, 0), 0)),
                  seg(1),
                  _full((LRU_BLOCKS, LRU_BW, LRU_BW)), _full((LRU_BLOCKS, LRU_BW, LRU_BW)),
                  _full((1, D_MODEL)), _full((1, D_MODEL)), _full((1, D_MODEL)),
                  _full((4, D_MODEL)), _full((1, D_MODEL))],
        out_specs=[pl.BlockSpec((tm, D_MODEL), lambda i: (i, 0)),
                   pl.BlockSpec((None, tm, D_MODEL), lambda i: (0, i, 0))],
        out_shape=[jax.ShapeDtypeStruct((t, D_MODEL), F32),
                   jax.ShapeDtypeStruct((2, t, D_MODEL), _MXU_DTYPE)],
        scratch_shapes=[pltpu.VMEM((tm, D_MODEL), F32), pltpu.VMEM((tm, D_MODEL), F32),
                        pltpu.VMEM((8, D_MODEL), F32)],
        compiler_params=_seq_params(),
    )(p, p, p, wa, wx, ba, bx, lam, cw, cb)


def _hgrn_fwd(p, logits, gw, y2, tm):
    t = p.shape[1]
    nc = tm // CHUNK

    def body(qf_ref, vh_ref, lg_ref, gw_ref, y_in, y_ref, o_ref, st_ref, state):
        del y_in
        i = pl.program_id(0)

        @pl.when(i == 0)
        def _():
            state[...] = jnp.zeros_like(state)

        l0 = lg_ref[0:1, :]
        l1 = lg_ref[1:2, :]
        lb = 1.0 / (1.0 + jnp.exp(l1 - l0))
        gwv = gw_ref[...]
        causal = _rows((CHUNK, CHUNK)) >= _cols((CHUNK, CHUNK))

        for c in range(nc):
            rows = pl.ds(c * CHUNK, CHUNK)
            v = _seg(vh_ref, 0, rows)
            hg = _seg(vh_ref, 1, rows)
            _, _, _, _, _, qm, km, ebm, ecm, ebc = _hgrn_prep(_seg(qf_ref, 0, rows), _seg(qf_ref, 1, rows), lb)
            ke = _mx(km.astype(F32) * ecm)
            gate = gwv * (hg * _sigmoid(hg))
            o_parts, y_parts = [], []
            for h in range(HEADS):
                ln = slice(h * HEAD_D, (h + 1) * HEAD_D)
                s0 = state[h]
                st_ref[c, h] = s0
                vt = _mx(v[:, ln].T)
                att = _mx(jnp.where(causal, _dot_nt(qm[:, ln], km[:, ln]), 0.0))
                o = _dot_nt(_pad_lanes([qm[:, ln], att], 2 * HEAD_D),
                            _pad_lanes([_mx(s0 * ebm[:, ln]), vt], 2 * HEAD_D))
                state[h] = s0 * ebc[:, ln] + jnp.dot(vt, ke[:, ln], preferred_element_type=F32)
                rstd = lax.rsqrt(jnp.mean(o * o, axis=-1, keepdims=True) + EPS)
                o_parts.append(o)
                y_parts.append((o * rstd * gate[:, ln]).astype(_MXU_DTYPE))
            o_ref[rows, :] = jnp.concatenate(o_parts, axis=1)
            y_ref[rows, :] = jnp.concatenate(y_parts, axis=1)

    pair = lambda s: pl.BlockSpec((4, tm, UNIT), lambda i: (s, i, 0))
    return pl.pallas_call(
        body, name="hgrn_fwd", grid=(t // tm,),
        in_specs=[pair(1), pair(2), _full((2, D_MODEL)), _full((1, D_MODEL)),
                  pl.BlockSpec(memory_space=pl.ANY)],
        out_specs=[pl.BlockSpec((None, tm, D_MODEL), lambda i: (1, i, 0)),
                   pl.BlockSpec((tm, D_MODEL), lambda i: (i, 0)),
                   pl.BlockSpec((nc, HEADS, HEAD_D, HEAD_D), lambda i: (i, 0, 0, 0))],
        out_shape=[jax.ShapeDtypeStruct((2, t, D_MODEL), _MXU_DTYPE),
                   jax.ShapeDtypeStruct((t, D_MODEL), F32),
                   jax.ShapeDtypeStruct((t // CHUNK, HEADS, HEAD_D, HEAD_D), F32)],
        scratch_shapes=[pltpu.VMEM((HEADS, HEAD_D, HEAD_D), F32)],
        input_output_aliases={4: 0},
        compiler_params=_seq_params(),
    )(p, p, logits, gw, y2)


def _out_fused(y2, wo, x, tgt, wpost, tm):
    t = x.shape[0]
    inv_d = 1.0 / D_MODEL

    def body(y_ref, wo_ref, x_ref, t_ref, w_ref, dout_ref, dy_ref, sq_ref, dw_ref):
        i = pl.program_id(0)

        @pl.when(i == 0)
        def _():
            sq_ref[...] = jnp.zeros_like(sq_ref)
            dw_ref[...] = jnp.zeros_like(dw_ref)

        y = (jnp.dot(y_ref[0], wo_ref[0:D_MODEL, :], preferred_element_type=F32)
             + jnp.dot(y_ref[1], wo_ref[D_MODEL:2 * D_MODEL, :], preferred_element_type=F32))
        w = w_ref[...]
        rstd = lax.rsqrt(jnp.mean(y * y, axis=-1, keepdims=True) + EPS)
        n = y * rstd
        err = (x_ref[...] + n * w) - t_ref[...]
        sq_ref[...] += jnp.sum(err * err, axis=0, keepdims=True)
        dout = err * inv_d
        dout_ref[...] = dout
        dw_ref[...] += jnp.sum(dout * n, axis=0, keepdims=True)
        dn = dout * w
        dy_ref[...] = (rstd * (dn - n * jnp.mean(dn * n, axis=-1, keepdims=True))).astype(_MXU_DTYPE)

    row = pl.BlockSpec((tm, D_MODEL), lambda i: (i, 0))
    return pl.pallas_call(
        body, name="out_fused", grid=(t // tm,),
        in_specs=[pl.BlockSpec((2, tm, D_MODEL), lambda i: (0, i, 0)), _full((2 * D_MODEL, D_MODEL)),
                  row, row, _full((1, D_MODEL))],
        out_specs=[row, row, _full((1, D_MODEL)), _full((1, D_MODEL))],
        out_shape=[jax.ShapeDtypeStruct((t, D_MODEL), F32), jax.ShapeDtypeStruct((t, D_MODEL), _MXU_DTYPE),
                   jax.ShapeDtypeStruct((1, D_MODEL), F32), jax.ShapeDtypeStruct((1, D_MODEL), F32)],
        compiler_params=_seq_params(),
    )(y2, wo, x, tgt, wpost)


def _dw_out(y2, dy, tt):
    t = dy.shape[0]
    hr = UNIT // 2

    def body(y_ref, dy_ref, g_ref):
        @pl.when(pl.program_id(1) == 0)
        def _():
            g_ref[...] = jnp.zeros_like(g_ref)

        r = _dot_tn(y_ref[...], dy_ref[...])
        g_ref[0] += r[0:hr]
        g_ref[1] += r[hr:UNIT]

    return pl.pallas_call(
        body, name="dw_out", grid=(N_CHIPS, t // tt),
        in_specs=[pl.BlockSpec((None, tt, UNIT), lambda c, k: (c // 2, k, c % 2)),
                  pl.BlockSpec((tt, D_MODEL), lambda c, k: (k, 0))],
        out_specs=pl.BlockSpec((2, None, hr, D_MODEL), lambda c, k: (0, c, 0, 0)),
        out_shape=jax.ShapeDtypeStruct((2, N_CHIPS, hr, D_MODEL), F32),
        compiler_params=pltpu.CompilerParams(dimension_semantics=("arbitrary", "arbitrary")),
    )(y2, dy)


def _lru_bwd(dy, wo, p, h, wa, wx, ba, bx, lam, cw, cb, tm):
    t = dy.shape[0]
    nt = t // tm
    ng = tm // 8

    def body(dy_ref, wo_ref, lx_ref, halo_ref, lg_ref, h_ref, hhalo_ref, wa_ref, wx_ref, ba_ref, bx_ref,
             lam_ref, cw_ref, cb_ref,
             dp_ref, dwa_ref, dwx_ref, dba_ref, dbx_ref, dsp_ref, dcw_ref, dcb_ref,
             c_s, in_s, dh_s, dhc_s, afirst_s, dxc_s):
        i = pl.program_id(0)
        first = i == nt - 1

        @pl.when(i == 0)
        def _():
            for ref in (dwa_ref, dwx_ref, dba_ref, dbx_ref, dsp_ref, dcw_ref, dcb_ref, dhc_s, afirst_s, dxc_s):
                ref[...] = jnp.zeros_like(ref)

        sp = _softplus_neg(lam_ref[...])
        cwv = cw_ref[...]
        wav = wa_ref[...]
        wxv = wx_ref[...]
        halo = jnp.where(first, 0.0, _seg(halo_ref, 0))
        xs, xc, r, ig, a, om = _lru_gates(_seg(lx_ref, 0), halo, cwv, cb_ref[...], wav, wxv,
                                          ba_ref[...], bx_ref[...], sp)
        inv_mult = lax.rsqrt(om)
        mult = om * inv_mult
        d_y = _dot_nt(dy_ref[...], wo_ref[...])
        lg = _seg(lg_ref, 0)
        sl = _sigmoid(lg)
        hv = h_ref[...]
        d_lg = d_y * hv * (sl * (1.0 + lg * (1.0 - sl)))

        c_s[...] = _shift_up(a, afirst_s[...], 1)
        in_s[...] = d_y * (lg * sl)
        row8 = _rows((8, D_MODEL))

        def group(gg, carry):
            rows = pl.ds(pl.multiple_of((ng - 1 - gg) * 8, 8), 8)
            cv = c_s[rows, :]
            uv = in_s[rows, :]
            for s in (1, 2, 4):
                c_sh = jnp.where(row8 < 8 - s, pltpu.roll(cv, 8 - s, 0), 1.0)
                u_sh = jnp.where(row8 < 8 - s, pltpu.roll(uv, 8 - s, 0), 0.0)
                uv = cv * u_sh + uv
                cv = cv * c_sh
            hh = cv * carry + uv
            dh_s[rows, :] = hh
            return jnp.broadcast_to(hh[0:1, :], (8, D_MODEL))

        dhc_s[...] = lax.fori_loop(0, ng, group, dhc_s[...])
        afirst_s[...] = a[0:8]
        dh = dh_s[...]

        h_prev = _shift_down(hv, jnp.where(first, 0.0, hhalo_ref[...]), 1)
        gx = ig * xc
        d_mult = dh * gx
        d_i = dh * (mult * xc)
        d_xc = dh * (mult * ig)
        d_la = (dh * h_prev) * a - d_mult * ((a * a) * inv_mult)
        dsp_ref[...] += jnp.sum(d_la * r, axis=0, keepdims=True) * (-LRU_C)
        d_zr = (d_la * (-LRU_C * sp)) * (r * (1.0 - r))
        d_zi = d_i * (ig * (1.0 - ig))
        dba_ref[...] += jnp.sum(d_zr, axis=0, keepdims=True)
        dbx_ref[...] += jnp.sum(d_zi, axis=0, keepdims=True)
        back = []
        for n in range(LRU_BLOCKS):
            ln = slice(n * LRU_BW, (n + 1) * LRU_BW)
            xb = _mx(xc[:, ln])
            zr_n = _mx(d_zr[:, ln])
            zi_n = _mx(d_zi[:, ln])
            dwa_ref[n] += _dot_tn(xb, zr_n)
            dwx_ref[n] += _dot_tn(xb, zi_n)
            back.append(_dot_nt(zr_n, wav[n]) + _dot_nt(zi_n, wxv[n]))
        d_xc = d_xc + jnp.concatenate(back, axis=1)
        dcb_ref[...] += jnp.sum(d_xc, axis=0, keepdims=True)
        for k in range(4):
            dcw_ref[k:k + 1, :] += jnp.sum(d_xc * xs[k], axis=0, keepdims=True)
        head = dxc_s[...]
        d_lx = cwv[3:4] * d_xc
        for k in range(3):
            d_lx = d_lx + cwv[k:k + 1] * _shift_up(d_xc, head, 3 - k)
        dxc_s[...] = d_xc[0:8]
        dp_ref[0] = d_lx.astype(_MXU_DTYPE)
        dp_ref[1] = d_lg.astype(_MXU_DTYPE)

    rev = lambda i: nt - 1 - i
    seg = lambda s: pl.BlockSpec((2, tm, UNIT), lambda i: (s, rev(i), 0))
    halo_row = lambda i: jnp.maximum(rev(i) * (tm // 8) - 1, 0)
    row = pl.BlockSpec((tm, D_MODEL), lambda i: (rev(i), 0))
    vec = _full((1, D_MODEL))
    wblk = _full((LRU_BLOCKS, LRU_BW, LRU_BW))
    return pl.pallas_call(
        body, name="lru_bwd", grid=(nt,),
        in_specs=[row, pl.BlockSpec((D_MODEL, D_MODEL), lambda i: (0, 0)),
                  seg(0), pl.BlockSpec((2, 8, UNIT), lambda i: (0, halo_row(i), 0)), seg(1),
                  row, pl.BlockSpec((8, D_MODEL), lambda i: (halo_row(i), 0)),
                  wblk, wblk, vec, vec, vec, _full((4, D_MODEL)), vec],
        out_specs=[pl.BlockSpec((2, tm, D_MODEL), lambda i: (2, rev(i), 0)),
                   wblk, wblk, vec, vec, vec, _full((4, D_MODEL)), vec],
        out_shape=[jax.ShapeDtypeStruct((N_SEG, t, D_MODEL), _MXU_DTYPE),
                   jax.ShapeDtypeStruct((LRU_BLOCKS, LRU_BW, LRU_BW), F32),
                   jax.ShapeDtypeStruct((LRU_BLOCKS, LRU_BW, LRU_BW), F32),
                   jax.ShapeDtypeStruct((1, D_MODEL), F32), jax.ShapeDtypeStruct((1, D_MODEL), F32),
                   jax.ShapeDtypeStruct((1, D_MODEL), F32), jax.ShapeDtypeStruct((4, D_MODEL), F32),
                   jax.ShapeDtypeStruct((1, D_MODEL), F32)],
        scratch_shapes=[pltpu.VMEM((tm, D_MODEL), F32), pltpu.VMEM((tm, D_MODEL), F32),
                        pltpu.VMEM((tm, D_MODEL), F32), pltpu.VMEM((8, D_MODEL), F32),
                        pltpu.VMEM((8, D_MODEL), F32), pltpu.VMEM((8, D_MODEL), F32)],
        compiler_params=_seq_params(),
    )(dy, wo, p, p, p, h, h, wa, wx, ba, bx, lam, cw, cb)


def _hgrn_bwd(dy, wo, p, o, states, logits, gw, dp, tm):
    t = dy.shape[0]
    nt = t // tm
    nc = tm // CHUNK

    def body(dy_ref, wo_ref, qf_ref, vh_ref, o_ref, st_ref, lg_ref, gw_ref, dp_in,
             dp_ref, dgw_ref, dlb_ref, dyh_s, dstate, snext):
        del dp_in
        i = pl.program_id(0)

        @pl.when(i == 0)
        def _():
            for ref in (dgw_ref, dlb_ref, dstate, snext):
                ref[...] = jnp.zeros_like(ref)

        dyh_s[...] = _dot_nt(dy_ref[...], wo_ref[...])
        l0 = lg_ref[0:1, :]
        l1 = lg_ref[1:2, :]
        lb = 1.0 / (1.0 + jnp.exp(l1 - l0))
        gwv = gw_ref[...]
        causal = _rows((CHUNK, CHUNK)) >= _cols((CHUNK, CHUNK))
        wide = 2 * HEAD_D
        zeros_h = jnp.zeros((CHUNK, HEAD_D), _MXU_DTYPE)

        for c in reversed(range(nc)):
            rows = pl.ds(c * CHUNK, CHUNK)
            q = _seg(qf_ref, 0, rows)
            hg = _seg(vh_ref, 1, rows)
            ov = o_ref[rows, :]
            dyh = dyh_s[rows, :]
            sig, f, sq, em1, em2, qm, km, ebm, ecm, ebc = _hgrn_prep(q, _seg(qf_ref, 1, rows), lb)
            ke = _mx(km.astype(F32) * ecm)
            vb = _mx(_seg(vh_ref, 0, rows))
            sg = _sigmoid(hg)
            d_on = dyh * (hg * sg)
            x_p, y_p, dv_p, n_p, ex_p = [], [], [], [], []
            for h in range(HEADS):
                ln = slice(h * HEAD_D, (h + 1) * HEAD_D)
                o_h = ov[:, ln]
                rstd = lax.rsqrt(jnp.mean(o_h * o_h, axis=-1, keepdims=True) + EPS)
                n = o_h * rstd
                dn = d_on[:, ln] * gwv[:, ln]
                d_o = rstd * (dn - n * jnp.mean(dn * n, axis=-1, keepdims=True))
                d_ob = _mx(d_o)
                d_ot = _mx(d_o.T)
                s0 = st_ref[c, h]
                ds = dstate[h]
                both = _dot_nt(jnp.concatenate([_pad_lanes([qm[:, ln]], wide), _pad_lanes([zeros_h, d_ob], wide)], axis=0),
                               jnp.concatenate([km[:, ln], vb[:, ln]], axis=1))
                att = jnp.where(causal, both[0:CHUNK], 0.0)
                d_att = jnp.where(causal, both[CHUNK:2 * CHUNK], 0.0)
                x_p.append(jnp.dot(_pad_lanes([d_ob, _mx(d_att)], wide),
                                   _pad_rows([_mx(s0 * ebm[:, ln]), km[:, ln]], wide), preferred_element_type=F32))
                y_p.append(jnp.dot(_pad_lanes([vb[:, ln], _mx(d_att.T)], wide),
                                   _pad_rows([_mx(ds * ecm[:, ln]), qm[:, ln]], wide), preferred_element_type=F32))
                dv_p.append(_dot_nt(_pad_lanes([ke[:, ln], _mx(att.T)], wide), _pad_lanes([_mx(ds), d_ot], wide)))
                ex_p.append(jnp.sum(ds * snext[h], axis=0, keepdims=True))
                dstate[h] = ds * ebc[:, ln] + jnp.dot(d_ot, qm[:, ln], preferred_element_type=F32) * ebm[:, ln]
                snext[h] = s0
                n_p.append(n)
            cat = lambda parts: jnp.concatenate(parts, axis=1)
            x_all, y_all, n_all = cat(x_p), cat(y_p), cat(n_p)
            dg = _cumsum_rows(qm.astype(F32) * x_all - km.astype(F32) * y_all, reverse=True) + cat(ex_p)
            df = dg / f - em2 * y_all
            dgw_ref[...] += jnp.sum(d_on * n_all, axis=0, keepdims=True)
            dlb_ref[...] += jnp.sum(df * (1.0 - sig), axis=0, keepdims=True)
            dp_ref[0, rows, :] = ((em1 * x_all) * (sq * (1.0 + q * (1.0 - sq)))).astype(_MXU_DTYPE)
            dp_ref[1, rows, :] = (df * ((1.0 - lb) * sig * (1.0 - sig))).astype(_MXU_DTYPE)
            dp_ref[2, rows, :] = cat(dv_p).astype(_MXU_DTYPE)
            dp_ref[3, rows, :] = (dyh * (n_all * gwv) * (sg * (1.0 + hg * (1.0 - sg)))).astype(_MXU_DTYPE)

    rev = lambda i: nt - 1 - i
    pair = lambda s: pl.BlockSpec((4, tm, UNIT), lambda i: (s, rev(i), 0))
    row = pl.BlockSpec((tm, D_MODEL), lambda i: (rev(i), 0))
    vec = _full((1, D_MODEL))
    return pl.pallas_call(
        body, name="hgrn_bwd", grid=(nt,),
        in_specs=[row, pl.BlockSpec((D_MODEL, D_MODEL), lambda i: (1, 0)), pair(1), pair(2), row,
                  pl.BlockSpec((nc, HEADS, HEAD_D, HEAD_D), lambda i: (rev(i), 0, 0, 0)),
                  _full((2, D_MODEL)), vec, pl.BlockSpec(memory_space=pl.ANY)],
        out_specs=[pl.BlockSpec((4, tm, D_MODEL), lambda i: (0, rev(i), 0)), vec, vec],
        out_shape=[jax.ShapeDtypeStruct((N_SEG, t, D_MODEL), _MXU_DTYPE),
                   jax.ShapeDtypeStruct((1, D_MODEL), F32), jax.ShapeDtypeStruct((1, D_MODEL), F32)],
        scratch_shapes=[pltpu.VMEM((tm, D_MODEL), F32), pltpu.VMEM((HEADS, HEAD_D, HEAD_D), F32),
                        pltpu.VMEM((HEADS, HEAD_D, HEAD_D), F32)],
        input_output_aliases={8: 0},
        compiler_params=_seq_params(),
    )(dy, wo, p, p, o, states, logits, gw, dp)


def _dp_unit(m):
    return ((m // 2 + 2) % N_SEG) * 2 + m % 2


def _dw_in(ut, dp, tt):
    t = ut.shape[1]
    hr = D_MODEL // 2

    def body(ut_ref, dp_ref, g_ref):
        @pl.when(pl.program_id(1) == 0)
        def _():
            g_ref[...] = jnp.zeros_like(g_ref)

        r = jnp.dot(ut_ref[...], dp_ref[...], preferred_element_type=F32)
        for h in range(2):
            for half in range(2):
                g_ref[h, half] += r[h * hr:(h + 1) * hr, half * UNIT:(half + 1) * UNIT]

    return pl.pallas_call(
        body, name="dw_in", grid=(N_SEG, t // tt),
        in_specs=[pl.BlockSpec((D_MODEL, tt), lambda s, k: (0, k)),
                  pl.BlockSpec((None, tt, D_MODEL), lambda s, k: (s, k, 0))],
        out_specs=pl.BlockSpec((2, 2, hr, UNIT), lambda s, k: (0, (s + 2) % N_SEG, 0, 0)),
        out_shape=jax.ShapeDtypeStruct((2, N_UNIT, hr, UNIT), F32),
        compiler_params=pltpu.CompilerParams(dimension_semantics=("arbitrary", "arbitrary")),
    )(ut, dp)


def _dx(dp, wg, x, dout, wpre, tm, parts, scatter):
    t = x.shape[0]
    n = len(parts)
    nt = t // tm

    def body(*refs):
        dp_ref, w_ref, x_ref, dout_ref, wpre_ref = refs[:5]
        ins = refs[5:5 + n]
        gx_ref, dw_ref = refs[5 + n:7 + n]
        outs = refs[7 + n:7 + 2 * n]
        send_sems, recv_sems = refs[7 + 2 * n:]
        i = pl.program_id(0)

        @pl.when(i == 0)
        def _():
            dw_ref[...] = jnp.zeros_like(dw_ref)
            for cp in _chip_copies(ins, outs, send_sems, recv_sems, scatter):
                cp.start()

        du = None
        for m in range(N_UNIT):
            part = _dot_nt(dp_ref[m // 2, :, (m % 2) * UNIT:(m % 2 + 1) * UNIT], w_ref[_dp_unit(m)])
            du = part if du is None else du + part
        xv = x_ref[...]
        rstd = lax.rsqrt(jnp.mean(xv * xv, axis=-1, keepdims=True) + EPS)
        xn = xv * rstd
        dw_ref[...] += jnp.sum(du * xn, axis=0, keepdims=True)
        dn = du * wpre_ref[...]
        gx_ref[...] = dout_ref[...] + rstd * (dn - xn * jnp.mean(dn * xn, axis=-1, keepdims=True))

        @pl.when(i == nt - 1)
        def _():
            copies = _chip_copies(ins, outs, send_sems, recv_sems, scatter)
            for cp in copies:
                cp.wait_recv()
            for cp in copies:
                cp.wait_send()

    row = pl.BlockSpec((tm, D_MODEL), lambda i: (i, 0))
    any_spec = pl.BlockSpec(memory_space=pl.ANY)
    res = pl.pallas_call(
        body, name="dx", grid=(nt,),
        in_specs=[pl.BlockSpec((N_SEG, tm, D_MODEL), lambda i: (0, i, 0)), _resident((N_UNIT, D_MODEL, UNIT)),
                  row, row, _full((1, D_MODEL))] + [any_spec] * n,
        out_specs=[row, _full((1, D_MODEL))] + [any_spec] * n,
        out_shape=[jax.ShapeDtypeStruct((t, D_MODEL), F32), jax.ShapeDtypeStruct((1, D_MODEL), F32)]
        + _chip_exchange_shapes(parts, scatter),
        scratch_shapes=[pltpu.SemaphoreType.DMA((max(3 * n, 1),)), pltpu.SemaphoreType.DMA((max(3 * n, 1),))],
        compiler_params=_seq_params(),
    )(dp, wg, x, dout, wpre, *parts)
    return res[0], res[1], res[2:]


CHUNK_BYTES = 256 * 1024
MAX_CHUNKS = 16


def _n_chunks(shape, dtype, rows):
    nbytes = jnp.dtype(dtype).itemsize
    for d in shape:
        nbytes *= d
    k = max(1, min(MAX_CHUNKS, rows, nbytes // CHUNK_BYTES))
    while rows % k:
        k -= 1
    return k


def _chip_exchange_shapes(arrays, scatter):
    return [jax.ShapeDtypeStruct((3,) + tuple(a.shape[1:] if sc else a.shape), a.dtype)
            for a, sc in zip(arrays, scatter)]


def _chip_copies(ins, outs, send_sems, recv_sems, scatter):
    if not ins:
        return []
    x, y, c = lax.axis_index("x"), lax.axis_index("y"), lax.axis_index("c")
    peers = [(1 - x, y), (x, 1 - y), (1 - x, 1 - y)]
    copies = []
    for a in range(len(ins)):
        for j, (px, py) in enumerate(peers):
            copies.append(pltpu.make_async_remote_copy(
                src_ref=ins[a].at[2 * px + py] if scatter[a] else ins[a], dst_ref=outs[a].at[j],
                send_sem=send_sems.at[a * 3 + j], recv_sem=recv_sems.at[a * 3 + j],
                device_id=(px, py, c), device_id_type=MESH))
    return copies


def _core_swap(arrays, name, halved):
    n = len(arrays)
    shapes = [a.shape[1:] if halved else a.shape for a in arrays]
    ks = [_n_chunks(s, a.dtype, s[0]) for s, a in zip(shapes, arrays)]
    offs = [sum(ks[:a]) for a in range(n)]

    def body(*refs):
        ins, outs = refs[:n], refs[n:2 * n]
        send_sems, recv_sems = refs[2 * n:]
        x, y, c = lax.axis_index("x"), lax.axis_index("y"), lax.axis_index("c")

        def copy(a, j):
            step = shapes[a][0] // ks[a]
            rows = pl.ds(j * step, step)
            return pltpu.make_async_remote_copy(
                src_ref=ins[a].at[1 - c, rows] if halved else ins[a].at[rows], dst_ref=outs[a].at[rows],
                send_sem=send_sems.at[offs[a] + j], recv_sem=recv_sems.at[offs[a] + j],
                device_id=(x, y, 1 - c), device_id_type=MESH)

        pieces = [(a, j) for a in range(n) for j in range(ks[a])]
        for a, j in pieces:
            copy(a, j).start()
        for a, j in pieces:
            copy(a, j).wait_recv()
        for a, j in pieces:
            copy(a, j).wait_send()

    any_spec = pl.BlockSpec(memory_space=pl.ANY)
    return pl.pallas_call(
        body, name=name,
        in_specs=[any_spec] * n, out_specs=[any_spec] * n,
        out_shape=[jax.ShapeDtypeStruct(s, a.dtype) for s, a in zip(shapes, arrays)],
        scratch_shapes=[pltpu.SemaphoreType.DMA((sum(ks),)), pltpu.SemaphoreType.DMA((sum(ks),))],
    )(*arrays)


def _finish_exchange(arrays, small):
    n = len(arrays)
    ks = [_n_chunks(a.shape, a.dtype, a.shape[0]) for a in arrays]
    offs = [sum(ks[:a]) for a in range(n)]
    total = sum(ks)

    def body(*refs):
        ins, small_ref = refs[:n], refs[n]
        outs, gathered = refs[n + 1:2 * n + 1], refs[2 * n + 1]
        send_sems, recv_sems = refs[2 * n + 2:]
        x, y, c = lax.axis_index("x"), lax.axis_index("y"), lax.axis_index("c")
        copies = []
        for a in range(n):
            step = arrays[a].shape[0] // ks[a]
            for j in range(ks[a]):
                rows = pl.ds(j * step, step)
                copies.append(pltpu.make_async_remote_copy(
                    src_ref=ins[a].at[rows], dst_ref=outs[a].at[rows],
                    send_sem=send_sems.at[offs[a] + j], recv_sem=recv_sems.at[offs[a] + j],
                    device_id=(x, y, 1 - c), device_id_type=MESH))
        for slot in range(7):
            dx, dy, dc = (slot + 1) % 2, ((slot + 1) // 2) % 2, (slot + 1) // 4
            copies.append(pltpu.make_async_remote_copy(
                src_ref=small_ref, dst_ref=gathered.at[slot],
                send_sem=send_sems.at[total + slot], recv_sem=recv_sems.at[total + slot],
                device_id=(x + dx - 2 * x * dx, y + dy - 2 * y * dy, c + dc - 2 * c * dc), device_id_type=MESH))
        for cp in copies:
            cp.start()
        for cp in copies:
            cp.wait_recv()
        for cp in copies:
            cp.wait_send()

    any_spec = pl.BlockSpec(memory_space=pl.ANY)
    res = pl.pallas_call(
        body, name="reduce_cores",
        in_specs=[any_spec] * (n + 1), out_specs=[any_spec] * (n + 1),
        out_shape=[jax.ShapeDtypeStruct(a.shape, a.dtype) for a in arrays]
        + [jax.ShapeDtypeStruct((7,) + small.shape, small.dtype)],
        scratch_shapes=[pltpu.SemaphoreType.DMA((total + 7,)), pltpu.SemaphoreType.DMA((total + 7,))],
    )(*arrays, small)
    return res[:n], res[n]


def _sum_own_half(g, b, name, out_dtype, tr=512):
    _, rows, cols = g.shape
    tr = min(tr, rows)
    core = lax.axis_index("c").reshape(1)

    def body(c_ref, g_ref, b_ref, o_ref):
        del c_ref
        o_ref[...] = (g_ref[...] + b_ref[...]).astype(out_dtype)

    return pl.pallas_call(
        body, name=name,
        grid_spec=pltpu.PrefetchScalarGridSpec(
            num_scalar_prefetch=1, grid=(rows // tr,),
            in_specs=[pl.BlockSpec((None, tr, cols), lambda i, c_ref: (c_ref[0], i, 0)),
                      pl.BlockSpec((tr, cols), lambda i, c_ref: (i, 0))],
            out_specs=pl.BlockSpec((tr, cols), lambda i, c_ref: (i, 0))),
        out_shape=jax.ShapeDtypeStruct((rows, cols), out_dtype),
        compiler_params=pltpu.CompilerParams(dimension_semantics=("parallel",)),
    )(core, g, b)


def _sum_chips(own, r, slabbed, name, tr=512):
    _, rows, cols = r.shape
    tr = min(tr, rows)
    chip = (2 * lax.axis_index("x") + lax.axis_index("y")).reshape(1)

    def body(c_ref, own_ref, r_ref, o_ref):
        del c_ref
        f = lambda val: val.astype(F32)
        o_ref[...] = (f(own_ref[...]) + f(r_ref[0])) + (f(r_ref[1]) + f(r_ref[2]))

    if slabbed:
        own_spec = pl.BlockSpec((None, tr, cols), lambda i, c_ref: (c_ref[0], i, 0))
    else:
        own_spec = pl.BlockSpec((tr, cols), lambda i, c_ref: (i, 0))
    return pl.pallas_call(
        body, name=name,
        grid_spec=pltpu.PrefetchScalarGridSpec(
            num_scalar_prefetch=1, grid=(rows // tr,),
            in_specs=[own_spec, pl.BlockSpec((3, tr, cols), lambda i, c_ref: (0, i, 0))],
            out_specs=pl.BlockSpec((tr, cols), lambda i, c_ref: (i, 0))),
        out_shape=jax.ShapeDtypeStruct((rows, cols), F32),
        compiler_params=pltpu.CompilerParams(dimension_semantics=("parallel",)),
    )(chip, own, r)


def _adamw_big(own, other, w, m, v, name, grid, g_block, g_index, w_block, w_index):
    core = lax.axis_index("c").reshape(1)
    half_axis = len(grid) - 1

    def body(c_ref, own_ref, oth_ref, w_ref, m_ref, v_ref, g_ref, d_ref, nm_ref, nv_ref):
        mine = pl.program_id(half_axis) == c_ref[0]
        gv = jnp.where(mine, own_ref[...], oth_ref[...])
        g_ref[...] = gv
        d_ref[...], nm_ref[...], nv_ref[...] = _adamw(w_ref[...], gv, m_ref[...], v_ref[...])

    g_spec = pl.BlockSpec(g_block, lambda *a: g_index(*a[:half_axis]))
    w_spec = pl.BlockSpec(w_block, lambda *a: w_index(*a[:-1]))
    shp = jax.ShapeDtypeStruct(w.shape, F32)
    return pl.pallas_call(
        body, name=name,
        grid_spec=pltpu.PrefetchScalarGridSpec(
            num_scalar_prefetch=1, grid=grid,
            in_specs=[g_spec, g_spec, w_spec, w_spec, w_spec], out_specs=[w_spec] * 4),
        out_shape=[shp] * 4,
        compiler_params=pltpu.CompilerParams(dimension_semantics=("parallel",) * len(grid)),
    )(core, own, other, w, m, v)


def _small_finish(own, others, lam, logits):
    def body(a_ref, b_ref, lam_ref, lg_ref, o_ref):
        g = (((a_ref[...] + b_ref[0]) + (b_ref[1] + b_ref[2]))
             + ((b_ref[3] + b_ref[4]) + (b_ref[5] + b_ref[6])))
        o_ref[...] = g
        o_ref[8:9, :] = g[8:9, :] * (-_sigmoid(-lam_ref[...]))
        lb = 1.0 / (1.0 + jnp.exp(lg_ref[1:2, :] - lg_ref[0:1, :]))
        d0 = g[9:10, :] * (lb * (1.0 - lb))
        o_ref[9:10, :] = d0
        o_ref[10:11, :] = -d0

    return pl.pallas_call(
        body, name="small_finish",
        in_specs=[_full((SMALL_ROWS, D_MODEL)), _full((7, SMALL_ROWS, D_MODEL)), _full((1, D_MODEL)),
                  _full((2, D_MODEL))],
        out_specs=_full((SMALL_ROWS, D_MODEL)),
        out_shape=jax.ShapeDtypeStruct((SMALL_ROWS, D_MODEL), F32),
    )(own, others, lam, logits)


def _adamw_small(ws, gs, ms, vs):
    n = len(ws)

    def body(*refs):
        w_r, g_r, m_r, v_r = refs[:n], refs[n:2 * n], refs[2 * n:3 * n], refs[3 * n:4 * n]
        d_o, m_o, v_o = refs[4 * n:5 * n], refs[5 * n:6 * n], refs[6 * n:7 * n]
        for j in range(n):
            d_o[j][...], m_o[j][...], v_o[j][...] = _adamw(w_r[j][...], g_r[j][...], m_r[j][...], v_r[j][...])

    specs = [_full(w.shape) for w in ws]
    shapes = [jax.ShapeDtypeStruct(w.shape, F32) for w in ws]
    outs = pl.pallas_call(
        body, name="adamw_small",
        in_specs=specs * 4, out_specs=specs * 3, out_shape=shapes * 3,
    )(*ws, *gs, *ms, *vs)
    return outs[:n], outs[n:2 * n], outs[2 * n:]


def _local_step(x, tgt, ut, p, pre_w, wg, cw, cb, wa, wx, ba, bx, lam, logits, gw, wo, post_w, tm, tmm,
                prepare=None):
    h, y2 = _lru_fwd(p, wa, wx, ba, bx, lam, cw, cb, tm)
    y2, o, states = _hgrn_fwd(p, logits, gw, y2, tm)
    dout, dy, sq, d_post = _out_fused(y2, wo, x, tgt, post_w, tmm)
    g_out = _dw_out(y2, dy, min(2048, x.shape[0]))
    dp, d_wa, d_wx, d_ba, d_bx, d_sp, d_cw, d_cb = _lru_bwd(dy, wo, p, h, wa, wx, ba, bx, lam, cw, cb, tm)
    dp, d_gw, d_lb = _hgrn_bwd(dy, wo, p, o, states, logits, gw, dp, tm)
    g_in = _dw_in(ut, dp, min(2048, x.shape[0]))
    parts, scatter = prepare(g_in, g_out, d_wa, d_wx) if prepare is not None else ([], [])
    grad_x, d_pre, received = _dx(dp, wg, x, dout, pre_w, tm, parts, scatter)
    small = jnp.concatenate([d_pre, d_cw, d_cb, d_ba, d_bx, d_sp, d_lb,
                             jnp.zeros((1, D_MODEL), F32), d_gw, d_post, sq,
                             jnp.zeros((SMALL_ROWS - 14, D_MODEL), F32)], axis=0)
    return grad_x, g_in, g_out, d_wa, d_wx, small, parts, received


def kernel(x, pre_norm_w, w_in, conv_w, conv_b, lru_w_a, lru_b_a, lru_w_x, lru_b_x, lru_lambda, hgrn_lb_logits, hgrn_gnorm_w, w_out, post_norm_w, loss_target, m_pre_norm_w, m_w_in, m_conv_w, m_conv_b, m_lru_w_a, m_lru_b_a, m_lru_w_x, m_lru_b_x, m_lru_lambda, m_hgrn_lb_logits, m_hgrn_gnorm_w, m_w_out, m_post_norm_w, v_pre_norm_w, v_w_in, v_conv_w, v_conv_b, v_lru_w_a, v_lru_b_a, v_lru_w_x, v_lru_b_x, v_lru_lambda, v_hgrn_lb_logits, v_hgrn_gnorm_w, v_w_out, v_post_norm_w):
    t = x.shape[1]
    tm = min(256, t)
    tmm = min(512, t)
    chip = 2 * lax.axis_index("x") + lax.axis_index("y")

    win_units = w_in[0].astype(_MXU_DTYPE).reshape(D_MODEL, 3, UNIT).transpose(1, 0, 2)
    small_w = jnp.concatenate([conv_w[0], lru_b_a[0], lru_b_x[0]], axis=1)
    ut, p, wg4, (wo4, wa4, wx4), sw4 = _in_proj_gather(
        x[0], pre_norm_w, win_units,
        [w_out[0].astype(_MXU_DTYPE).reshape(2, 256, D_MODEL), lru_w_a[0].astype(_MXU_DTYPE).reshape(2, 128, LRU_BW),
         lru_w_x[0].astype(_MXU_DTYPE).reshape(2, 128, LRU_BW)],
        small_w, min(512, t))
    wg = wg4.reshape(N_UNIT, D_MODEL, UNIT)
    wo = wo4.reshape(2 * D_MODEL, D_MODEL)
    by_block = lambda w4: w4.reshape(N_CHIPS, LRU_BLOCKS, 64, LRU_BW).transpose(1, 0, 2, 3).reshape(
        LRU_BLOCKS, LRU_BW, LRU_BW)
    wa, wx = by_block(wa4), by_block(wx4)
    sw4 = sw4.reshape(N_CHIPS, 4, 384)
    cw = sw4[:, :, 0:256].transpose(1, 0, 2).reshape(4, D_MODEL)
    ba = sw4[:, :, 256:320].transpose(1, 0, 2).reshape(1, D_MODEL)
    bx = sw4[:, :, 320:384].transpose(1, 0, 2).reshape(1, D_MODEL)

    def lru_layout(g):
        g = g.reshape(2, 2, N_CHIPS, 64, LRU_BW).transpose(0, 2, 1, 3, 4)
        return g.reshape(2, N_CHIPS * 128, LRU_BW)

    def prepare(g_in, g_out, d_wa, d_wx):
        g_in = g_in.reshape(2, N_UNIT * 512, UNIT)
        g_out = g_out.reshape(2, N_CHIPS * 256, D_MODEL)
        g_wa, g_wx = lru_layout(d_wa), lru_layout(d_wx)
        b_in, b_out, b_wa, b_wx = _core_swap([g_in, g_out, g_wa, g_wx], "reduce_swap", halved=True)
        parts = [_sum_own_half(g_in, b_in, "presum_in", _WIRE_DTYPE).reshape(N_CHIPS, 3 * 512, UNIT),
                 _sum_own_half(g_out, b_out, "presum_out", _WIRE_DTYPE).reshape(N_CHIPS, 256, D_MODEL),
                 _sum_own_half(g_wa, b_wa, "presum_wa", _WIRE_DTYPE).reshape(N_CHIPS, 128, LRU_BW),
                 _sum_own_half(g_wx, b_wx, "presum_wx", _WIRE_DTYPE).reshape(N_CHIPS, 128, LRU_BW)]
        return parts, [True] * 4

    grad_x, _, _, _, _, small, (p_in, p_out, p_wa, p_wx), (r_in, r_out, r_wa, r_wx) = _local_step(
        x[0], loss_target[0], ut, p, pre_norm_w, wg, cw, conv_b, wa, wx, ba, bx, lru_lambda, hgrn_lb_logits,
        hgrn_gnorm_w, wo, post_norm_w, tm, tmm, prepare)
    s_in = _sum_chips(p_in, r_in, True, "sum_in")
    s_out = _sum_chips(p_out, r_out, True, "sum_out")
    s_wa = _sum_chips(p_wa, r_wa, True, "sum_wa")
    s_wx = _sum_chips(p_wx, r_wx, True, "sum_wx")
    (o_in, o_out, o_wa, o_wx), small_others = _finish_exchange([s_in, s_out, s_wa, s_wx], small)

    g_w_in, d_w_in, nm_w_in, nv_w_in = _adamw_big(
        s_in.reshape(3, 512, UNIT), o_in.reshape(3, 512, UNIT), w_in[0], m_w_in[0], v_w_in[0], "adamw_w_in", (3, 2),
        (None, 512, UNIT), lambda k: (k, 0, 0), (512, UNIT), lambda k, h: (h, k))
    g_w_out, d_w_out, nm_w_out, nv_w_out = _adamw_big(
        s_out, o_out, w_out[0], m_w_out[0], v_w_out[0], "adamw_w_out", (2,),
        (256, D_MODEL), lambda: (0, 0), (256, D_MODEL), lambda h: (h, 0))
    sq2 = lambda a: a.reshape(LRU_BW, LRU_BW)
    lru_specs = ((2,), (128, LRU_BW), lambda: (0, 0), (128, LRU_BW), lambda h: (h, 0))
    g_wa, d_wa2, nm_wa, nv_wa = _adamw_big(s_wa, o_wa, sq2(lru_w_a), sq2(m_lru_w_a), sq2(v_lru_w_a), "adamw_wa",
                                           *lru_specs)
    g_wx, d_wx2, nm_wx, nv_wx = _adamw_big(s_wx, o_wx, sq2(lru_w_x), sq2(m_lru_w_x), sq2(v_lru_w_x), "adamw_wx",
                                           *lru_specs)

    gs = _small_finish(small, small_others, lru_lambda, hgrn_lb_logits)
    loss = jnp.sum(gs[13]) * (0.5 / D_MODEL)
    g_pre = gs[0:1]
    g_cw = lax.dynamic_slice(gs[1:5], (0, chip * 256), (4, 256))
    g_cb = gs[5:6]
    g_ba = lax.dynamic_slice(gs[6].reshape(LRU_BLOCKS, N_CHIPS, 64), (0, chip, 0), (LRU_BLOCKS, 1, 64)).reshape(4, 64)
    g_bx = lax.dynamic_slice(gs[7].reshape(LRU_BLOCKS, N_CHIPS, 64), (0, chip, 0), (LRU_BLOCKS, 1, 64)).reshape(4, 64)
    g_lam = gs[8:9]
    g_lb = gs[9:11]
    g_gw = gs[11:12]
    g_post = gs[12:13]
    small_g = [g_pre, g_cw, g_cb, g_ba, g_bx, g_lam, g_lb, g_gw, g_post]
    two_d = lambda a: a.reshape(a.shape[-2:])
    small_w_list = [pre_norm_w, conv_w, conv_b, lru_b_a, lru_b_x, lru_lambda, hgrn_lb_logits, hgrn_gnorm_w, post_norm_w]
    small_m_list = [m_pre_norm_w, m_conv_w, m_conv_b, m_lru_b_a, m_lru_b_x, m_lru_lambda, m_hgrn_lb_logits,
                    m_hgrn_gnorm_w, m_post_norm_w]
    small_v_list = [v_pre_norm_w, v_conv_w, v_conv_b, v_lru_b_a, v_lru_b_x, v_lru_lambda, v_hgrn_lb_logits,
                    v_hgrn_gnorm_w, v_post_norm_w]
    sd, sm, sv = _adamw_small([two_d(a) for a in small_w_list], small_g,
                              [two_d(a) for a in small_m_list], [two_d(a) for a in small_v_list])

    def shaped(vals, refs):
        return [val.reshape(ref.shape) for val, ref in zip(vals, refs)]

    s_g = shaped(small_g, small_w_list)
    s_d = shaped(sd, small_w_list)
    s_m = shaped(sm, small_w_list)
    s_v = shaped(sv, small_w_list)

    def ordered(small, big_in, big_wa, big_wx, big_out):
        pre, cw_, cb_, ba_, bx_, lam_, lb_, gw_, post_ = small
        return [pre, big_in.reshape(w_in.shape), cw_, cb_, big_wa.reshape(lru_w_a.shape), ba_,
                big_wx.reshape(lru_w_x.shape), bx_, lam_, lb_, gw_, big_out.reshape(w_out.shape), post_]

    grads = ordered(s_g, g_w_in, g_wa, g_wx, g_w_out)
    deltas = ordered(s_d, d_w_in, d_wa2, d_wx2, d_w_out)
    new_m = ordered(s_m, nm_w_in, nm_wa, nm_wx, nm_w_out)
    new_v = ordered(s_v, nv_w_in, nv_wa, nv_wx, nv_w_out)
    return (loss, grad_x.reshape(x.shape), *grads, *deltas, *new_m, *new_v)
```

```python
import functools

import jax
import jax.numpy as jnp
from jax import lax
from jax.experimental import pallas as pl
from jax.experimental.pallas import tpu as pltpu

F32 = jnp.float32
BF16 = jnp.bfloat16
_MXU_DTYPE = jnp.bfloat16
_WIRE_DTYPE = jnp.bfloat16

D_MODEL = 1024
N_SEG = 6
UNIT = 512
N_UNIT = 12
LRU_BLOCKS = 4
LRU_BW = 256
LRU_C = 8.0
HEADS = 8
HEAD_D = 128
CHUNK = 64
EPS = 1e-6
N_CHIPS = 4
EXP_CLAMP = 80.0

ADAM_LR = 0.001
ADAM_B1 = 0.9
ADAM_B2 = 0.999
ADAM_EPS = 1e-08
ADAM_WD = 0.01
ADAM_STEP = 10

SMALL_ROWS = 16
FWD_PIECES = 4
MESH = pl.DeviceIdType.MESH


def _sigmoid(x):
    return 0.5 * jnp.tanh(0.5 * x) + 0.5


def _mx(x):
    return x.astype(_MXU_DTYPE)


def _dot(a, b):
    return jnp.dot(_mx(a), _mx(b), preferred_element_type=F32)


def _dot_nt(a, b):
    return lax.dot_general(_mx(a), _mx(b), (((1,), (1,)), ((), ())), preferred_element_type=F32)


def _dot_tn(a, b):
    return lax.dot_general(_mx(a), _mx(b), (((0,), (0,)), ((), ())), preferred_element_type=F32)


def _rows(shape):
    return lax.broadcasted_iota(jnp.int32, shape, 0)


def _cols(shape):
    return lax.broadcasted_iota(jnp.int32, shape, 1)


def _softplus_neg(lam):
    z = -lam
    e = jnp.exp(-jnp.abs(z))
    series = e * (1.0 - e * (0.5 - e * (1.0 / 3.0 - 0.25 * e)))
    return jnp.maximum(z, 0.0) + jnp.where(e < 1e-2, series, jnp.log(1.0 + e))


def _one_minus_sq(a, y):
    series = -y * (1.0 + y * (0.5 + y * (1.0 / 6.0)))
    return jnp.where(y > -0.01, series, 1.0 - a * a)


def _cumsum_rows(x, reverse=False):
    n = x.shape[0] // 8
    row8 = _rows((8, x.shape[1]))
    out = [None] * n
    carry = None
    for g in (reversed(range(n)) if reverse else range(n)):
        blk = x[8 * g:8 * g + 8]
        for s in (1, 2, 4):
            if reverse:
                blk = blk + jnp.where(row8 < 8 - s, pltpu.roll(blk, 8 - s, 0), 0.0)
            else:
                blk = blk + jnp.where(row8 >= s, pltpu.roll(blk, s, 0), 0.0)
        if carry is not None:
            blk = blk + carry
        carry = blk[0:1] if reverse else blk[7:8]
        out[g] = blk
    return jnp.concatenate(out, axis=0)


def _shift_down(x, halo, s):
    if s == 0:
        return x
    r = pltpu.roll(x, s, 0)
    top = jnp.where(_rows(halo.shape) < s, pltpu.roll(halo, s, 0), r[0:8])
    return jnp.concatenate([top, r[8:]], axis=0)


def _shift_up(x, head, s):
    if s == 0:
        return x
    n = x.shape[0]
    r = pltpu.roll(x, n - s, 0)
    bottom = jnp.where(_rows(head.shape) >= 8 - s, pltpu.roll(head, 8 - s, 0), r[n - 8:n])
    return jnp.concatenate([r[:n - 8], bottom], axis=0)


def _lru_gates(lx, halo, cw, cb, wa, wx, ba, bx, sp):
    xs = [_shift_down(lx, halo, 3 - k) for k in range(4)]
    xc = cb + cw[0:1] * xs[0] + cw[1:2] * xs[1] + cw[2:3] * xs[2] + cw[3:4] * xs[3]
    zr, zi = [], []
    for n in range(LRU_BLOCKS):
        xb = _mx(xc[:, n * LRU_BW:(n + 1) * LRU_BW])
        zr.append(jnp.dot(xb, wa[n], preferred_element_type=F32))
        zi.append(jnp.dot(xb, wx[n], preferred_element_type=F32))
    r = _sigmoid(jnp.concatenate(zr, axis=1) + ba)
    ig = _sigmoid(jnp.concatenate(zi, axis=1) + bx)
    la = (-LRU_C * sp) * r
    a = jnp.exp(la)
    om = _one_minus_sq(a, 2.0 * la)
    return xs, xc, r, ig, a, om


def _hgrn_prep(q, fr, lb):
    sig = _sigmoid(fr)
    f = lb + (1.0 - lb) * sig
    sq = _sigmoid(q)
    b = _cumsum_rows(jnp.log(f))
    bm = b[CHUNK // 2 - 1:CHUNK // 2]
    bc = b[CHUNK - 1:CHUNK]
    em1 = jnp.exp(jnp.minimum(b - bm, EXP_CLAMP))
    em2 = jnp.exp(jnp.minimum(bm - b, EXP_CLAMP))
    qm = _mx((q * sq) * em1)
    km = _mx((1.0 - f) * em2)
    return sig, f, sq, em1, em2, qm, km, jnp.exp(bm), jnp.exp(bc - bm), jnp.exp(bc)


def _seg(ref, s, rows=slice(None)):
    return jnp.concatenate([ref[2 * s, rows, :], ref[2 * s + 1, rows, :]], axis=1)


def _pad_lanes(parts, width):
    have = sum(p.shape[1] for p in parts)
    pad = [jnp.zeros((parts[0].shape[0], width - have), parts[0].dtype)] if width > have else []
    return jnp.concatenate(list(parts) + pad, axis=1)


def _pad_rows(parts, height):
    have = sum(p.shape[0] for p in parts)
    pad = [jnp.zeros((height - have, parts[0].shape[1]), parts[0].dtype)] if height > have else []
    return jnp.concatenate(list(parts) + pad, axis=0)


def _adamw(w, g, m, v):
    m = ADAM_B1 * m + (1.0 - ADAM_B1) * g
    v = ADAM_B2 * v + (1.0 - ADAM_B2) * jnp.square(g)
    m_hat = m / (1.0 - ADAM_B1 ** ADAM_STEP)
    v_hat = v / (1.0 - ADAM_B2 ** ADAM_STEP)
    delta = -ADAM_LR * (m_hat / (jnp.sqrt(v_hat) + ADAM_EPS) + ADAM_WD * w)
    return delta, m, v


def _full(shape):
    nd = len(shape)
    return pl.BlockSpec(shape, lambda *_: (0,) * nd)


def _seq_params():
    return pltpu.CompilerParams(dimension_semantics=("arbitrary",))


def _resident(shape):
    nd = len(shape)
    return pl.BlockSpec(shape, lambda *_: (0,) * nd, pipeline_mode=pl.Buffered(1))


def _in_proj_gather(x, w, win_units, halved, small_w, tm):
    t = x.shape[0]
    nt = t // tm
    n_h = len(halved)
    n_split = 3 + n_h
    n_items = n_split + 1
    hu = D_MODEL // 2
    sub = min(256, tm)
    chip = 2 * lax.axis_index("x") + lax.axis_index("y")
    order = jnp.stack([chip] + [jnp.bitwise_xor(chip, flip) for flip in (2, 1, 3)])

    def body(order_ref, x_ref, w_ref, win_ref, *rest):
        del order_ref
        h_ins, small_ref = rest[:n_h], rest[n_h]
        ut_ref, p_ref, wg_ref = rest[n_h + 1:n_h + 4]
        h_outs, sw_ref = rest[n_h + 4:2 * n_h + 4], rest[2 * n_h + 4]
        u_s, wbuf, send, recv, fsend, frecv, osend, orecv, wsem = rest[2 * n_h + 5:]
        g, i = pl.program_id(0), pl.program_id(1)
        mx, my, c = lax.axis_index("x"), lax.axis_index("y"), lax.axis_index("c")
        me = 2 * mx + my
        peers = [(1 - mx, my), (mx, 1 - my), (1 - mx, 1 - my)]
        qs = [2 * px + py for px, py in peers]
        sibling = (mx, my, 1 - c)

        def remote(src, dst, ssem, rsem, dev):
            return pltpu.make_async_remote_copy(src_ref=src, dst_ref=dst, send_sem=ssem, recv_sem=rsem,
                                                device_id=dev, device_id_type=MESH)

        def mine(item):
            if item < 3:
                return win_ref.at[item, pl.ds(c * hu, hu)]
            return h_ins[item - 3].at[c] if item < n_split else small_ref

        def piece(item, q, h):
            if item < 3:
                return wg_ref.at[q, item, pl.ds(h * hu, hu)]
            return h_outs[item - 3].at[q, h] if item < n_split else sw_ref.at[q]

        def fetch(item, jj, q):
            px, py = peers[jj]
            return remote(mine(item), piece(item, q, c), send.at[item * 3 + jj], recv.at[item * 3 + jj], (px, py, c))

        def forward(item, jj, h):
            whole = piece(item, qs[jj], h)
            step = whole.shape[0] // FWD_PIECES
            copies = []
            for k in range(FWD_PIECES):
                part = whole.at[pl.ds(k * step, step)]
                sem = (item * 3 + jj) * FWD_PIECES + k
                copies.append(remote(part, part, fsend.at[sem], frecv.at[sem], sibling))
            return copies

        def own(item):
            if item < 3:
                src, dst = win_ref.at[item], wg_ref.at[me, item]
            elif item < n_split:
                src, dst = h_ins[item - 3], h_outs[item - 3].at[me]
            else:
                src, dst = small_ref, sw_ref.at[me]
            return remote(src, dst, osend.at[item], orecv.at[item], sibling)

        def load_units(gg):
            src = lambda k: win_ref.at[k] if gg == 0 else wg_ref.at[qs[gg - 1], k]
            return [pltpu.make_async_copy(src(k), wbuf.at[gg % 2, k], wsem.at[(gg % 2) * 3 + k]) for k in range(3)]

        def in_pass(gg):
            return pl.when(g == gg)

        def pass_on(gg):
            for item in range(3):
                fetch(item, gg, qs[gg]).wait_recv()
                for cp in forward(item, gg, c):
                    cp.start()

        @pl.when(i == 0)
        def _():
            @in_pass(0)
            def _():
                for item in range(3):
                    for jj in range(2):
                        fetch(item, jj, me).start()
                for item in range(n_items):
                    own(item).start()
                for cp in load_units(0):
                    cp.start()

            @in_pass(1)
            def _():
                for item in range(3):
                    fetch(item, 2, me).start()

            @in_pass(2)
            def _():
                for item in range(3, n_items):
                    for jj in range(3):
                        fetch(item, jj, me).start()

            for gg in range(N_CHIPS):
                @in_pass(gg)
                def _(gg=gg):
                    for cp in load_units(gg):
                        cp.wait()

        @pl.when(g == 0)
        def _():
            xv = x_ref[...]
            rstd = lax.rsqrt(jnp.mean(xv * xv, axis=-1, keepdims=True) + EPS)
            u = xv * rstd * w_ref[...]
            ut_ref[...] = u.T.astype(_MXU_DTYPE)
            u_s[pl.ds(pl.multiple_of(i * tm, tm), tm), :] = _mx(u)

        slot = g % 2
        for r in range(tm // sub):
            ur = u_s[pl.ds(pl.multiple_of(i * tm + r * sub, sub), sub), :]
            for k in range(3):
                p_ref[k, r * sub:(r + 1) * sub, :] = jnp.dot(ur, wbuf[slot, k], preferred_element_type=F32)

        @pl.when(i == nt // 2)
        def _():
            for gg in (1, 2):
                in_pass(gg)(functools.partial(pass_on, gg))

        @pl.when(i == nt - 1)
        def _():
            in_pass(0)(functools.partial(pass_on, 0))
            for gg in range(3):
                @in_pass(gg)
                def _(gg=gg):
                    for item in range(3):
                        for cp in forward(item, gg, 1 - c):
                            cp.wait_recv()
                    for cp in load_units(gg + 1):
                        cp.start()

            @in_pass(N_CHIPS - 1)
            def _():
                for item in range(3, n_items):
                    for jj in range(3):
                        fetch(item, jj, qs[jj]).wait_recv()
                        if item < n_split:
                            for cp in forward(item, jj, c):
                                cp.start()
                for item in range(3, n_split):
                    for jj in range(3):
                        for cp in forward(item, jj, 1 - c):
                            cp.wait_recv()
                for item in range(n_items):
                    own(item).wait_recv()
                for item in range(n_items):
                    for jj in range(3):
                        fetch(item, jj, me).wait_send()
                        if item < n_split:
                            for cp in forward(item, jj, c):
                                cp.wait_send()
                    own(item).wait_send()

    any_spec = pl.BlockSpec(memory_space=pl.ANY)
    first_pass = lambda g, i: jnp.where(g == 0, i, nt - 1)
    res = pl.pallas_call(
        body, name="in_proj",
        grid_spec=pltpu.PrefetchScalarGridSpec(
            num_scalar_prefetch=1, grid=(N_CHIPS, nt),
            in_specs=[pl.BlockSpec((tm, D_MODEL), lambda g, i, o: (first_pass(g, i), 0)),
                      pl.BlockSpec((1, D_MODEL), lambda g, i, o: (0, 0))] + [any_spec] * (n_h + 2),
            out_specs=[pl.BlockSpec((D_MODEL, tm), lambda g, i, o: (0, first_pass(g, i))),
                       pl.BlockSpec((3, tm, UNIT), lambda g, i, o: (o[g], i, 0))]
            + [any_spec] * (n_h + 2),
            scratch_shapes=[pltpu.VMEM((t, D_MODEL), _MXU_DTYPE), pltpu.VMEM((2, 3, D_MODEL, UNIT), _MXU_DTYPE),
                            pltpu.SemaphoreType.DMA((3 * n_items,)), pltpu.SemaphoreType.DMA((3 * n_items,)),
                            pltpu.SemaphoreType.DMA((3 * n_split * FWD_PIECES,)),
                            pltpu.SemaphoreType.DMA((3 * n_split * FWD_PIECES,)),
                            pltpu.SemaphoreType.DMA((n_items,)), pltpu.SemaphoreType.DMA((n_items,)),
                            pltpu.SemaphoreType.DMA((6,))]),
        out_shape=[jax.ShapeDtypeStruct((D_MODEL, t), _MXU_DTYPE), jax.ShapeDtypeStruct((N_UNIT, t, UNIT), F32),
                   jax.ShapeDtypeStruct((N_CHIPS,) + win_units.shape, win_units.dtype)]
        + [jax.ShapeDtypeStruct((N_CHIPS,) + a.shape, a.dtype) for a in halved]
        + [jax.ShapeDtypeStruct((N_CHIPS,) + small_w.shape, small_w.dtype)],
        compiler_params=pltpu.CompilerParams(dimension_semantics=("arbitrary", "arbitrary")),
    )(order, x, w, win_units, *halved, small_w)
    return res[0], res[1], res[2], res[3:3 + n_h], res[3 + n_h]


def _lru_fwd(p, wa, wx, ba, bx, lam, cw, cb, tm):
    t = p.shape[1]
    ng = tm // 8

    def body(lx_ref, halo_ref, lg_ref, wa_ref, wx_ref, ba_ref, bx_ref, lam_ref, cw_ref, cb_ref,
             h_ref, y_ref, a_s, u_s, hc_s):
        i = pl.program_id(0)

        @pl.when(i == 0)
        def _():
            hc_s[...] = jnp.zeros_like(hc_s)

        sp = _softplus_neg(lam_ref[...])
        halo = jnp.where(i == 0, 0.0, _seg(halo_ref, 0))
        _, xc, _, ig, a, om = _lru_gates(_seg(lx_ref, 0), halo, cw_ref[...], cb_ref[...], wa_ref[...],
                                         wx_ref[...], ba_ref[...], bx_ref[...], sp)
        a_s[...] = a
        u_s[...] = jnp.sqrt(om) * (ig * xc)
        row8 = _rows((8, D_MODEL))

        def group(g, hc):
            rows = pl.ds(pl.multiple_of(g * 8, 8), 8)
            av = a_s[rows, :]
            uv = u_s[rows, :]
            for s in (1, 2, 4):
                a_sh = jnp.where(row8 >= s, pltpu.roll(av, s, 0), 1.0)
                u_sh = jnp.where(row8 >= s, pltpu.roll(uv, s, 0), 0.0)
                uv = av * u_sh + uv
                av = av * a_sh
            hh = av * hc + uv
            h_ref[rows, :] = hh
            return jnp.broadcast_to(hh[7:8, :], (8, D_MODEL))

        hc_s[...] = lax.fori_loop(0, ng, group, hc_s[...])
        lg = _seg(lg_ref, 0)
        y_ref[...] = (h_ref[...] * (lg * _sigmoid(lg))).astype(_MXU_DTYPE)

    seg = lambda s: pl.BlockSpec((2, tm, UNIT), lambda i: (s, i, 0))
    return pl.pallas_call(
        body, name="lru_fwd", grid=(t // tm,),
        in_specs=[seg(0),
                  pl.BlockSpec((2, 8, UNIT), lambda i: (0, jnp.maximum(i * (tm // 8) - 1, 0), 0)),
                  seg(1),
                  _full((LRU_BLOCKS, LRU_BW, LRU_BW)), _full((LRU_BLOCKS, LRU_BW, LRU_BW)),
                  _full((1, D_MODEL)), _full((1, D_MODEL)), _full((1, D_MODEL)),
                  _full((4, D_MODEL)), _full((1, D_MODEL))],
        out_specs=[pl.BlockSpec((tm, D_MODEL), lambda i: (i, 0)),
                   pl.BlockSpec((None, tm, D_MODEL), lambda i: (0, i, 0))],
        out_shape=[jax.ShapeDtypeStruct((t, D_MODEL), F32),
                   jax.ShapeDtypeStruct((2, t, D_MODEL), _MXU_DTYPE)],
        scratch_shapes=[pltpu.VMEM((tm, D_MODEL), F32), pltpu.VMEM((tm, D_MODEL), F32),
                        pltpu.VMEM((8, D_MODEL), F32)],
        compiler_params=_seq_params(),
    )(p, p, p, wa, wx, ba, bx, lam, cw, cb)


def _hgrn_fwd(p, logits, gw, y2, tm):
    t = p.shape[1]
    nc = tm // CHUNK

    def body(qf_ref, vh_ref, lg_ref, gw_ref, y_in, y_ref, o_ref, st_ref, state):
        del y_in
        i = pl.program_id(0)

        @pl.when(i == 0)
        def _():
            state[...] = jnp.zeros_like(state)

        l0 = lg_ref[0:1, :]
        l1 = lg_ref[1:2, :]
        lb = 1.0 / (1.0 + jnp.exp(l1 - l0))
        gwv = gw_ref[...]
        causal = _rows((CHUNK, CHUNK)) >= _cols((CHUNK, CHUNK))

        for c in range(nc):
            rows = pl.ds(c * CHUNK, CHUNK)
            v = _seg(vh_ref, 0, rows)
            hg = _seg(vh_ref, 1, rows)
            _, _, _, _, _, qm, km, ebm, ecm, ebc = _hgrn_prep(_seg(qf_ref, 0, rows), _seg(qf_ref, 1, rows), lb)
            ke = _mx(km.astype(F32) * ecm)
            gate = gwv * (hg * _sigmoid(hg))
            o_parts, y_parts = [], []
            for h in range(HEADS):
                ln = slice(h * HEAD_D, (h + 1) * HEAD_D)
                s0 = state[h]
                st_ref[c, h] = s0
                vt = _mx(v[:, ln].T)
                att = _mx(jnp.where(causal, _dot_nt(qm[:, ln], km[:, ln]), 0.0))
                o = _dot_nt(_pad_lanes([qm[:, ln], att], 2 * HEAD_D),
                            _pad_lanes([_mx(s0 * ebm[:, ln]), vt], 2 * HEAD_D))
                state[h] = s0 * ebc[:, ln] + jnp.dot(vt, ke[:, ln], preferred_element_type=F32)
                rstd = lax.rsqrt(jnp.mean(o * o, axis=-1, keepdims=True) + EPS)
                o_parts.append(o)
                y_parts.append((o * rstd * gate[:, ln]).astype(_MXU_DTYPE))
            o_ref[rows, :] = jnp.concatenate(o_parts, axis=1)
            y_ref[rows, :] = jnp.concatenate(y_parts, axis=1)

    pair = lambda s: pl.BlockSpec((4, tm, UNIT), lambda i: (s, i, 0))
    return pl.pallas_call(
        body, name="hgrn_fwd", grid=(t // tm,),
        in_specs=[pair(1), pair(2), _full((2, D_MODEL)), _full((1, D_MODEL)),
                  pl.BlockSpec(memory_space=pl.ANY)],
        out_specs=[pl.BlockSpec((None, tm, D_MODEL), lambda i: (1, i, 0)),
                   pl.BlockSpec((tm, D_MODEL), lambda i: (i, 0)),
                   pl.BlockSpec((nc, HEADS, HEAD_D, HEAD_D), lambda i: (i, 0, 0, 0))],
        out_shape=[jax.ShapeDtypeStruct((2, t, D_MODEL), _MXU_DTYPE),
                   jax.ShapeDtypeStruct((t, D_MODEL), F32),
                   jax.ShapeDtypeStruct((t // CHUNK, HEADS, HEAD_D, HEAD_D), F32)],
        scratch_shapes=[pltpu.VMEM((HEADS, HEAD_D, HEAD_D), F32)],
        input_output_aliases={4: 0},
        compiler_params=_seq_params(),
    )(p, p, logits, gw, y2)


def _out_fused(y2, wo, x, tgt, wpost, tm):
    t = x.shape[0]
    inv_d = 1.0 / D_MODEL

    def body(y_ref, wo_ref, x_ref, t_ref, w_ref, dout_ref, dy_ref, sq_ref, dw_ref):
        i = pl.program_id(0)

        @pl.when(i == 0)
        def _():
            sq_ref[...] = jnp.zeros_like(sq_ref)
            dw_ref[...] = jnp.zeros_like(dw_ref)

        y = (jnp.dot(y_ref[0], wo_ref[0:D_MODEL, :], preferred_element_type=F32)
             + jnp.dot(y_ref[1], wo_ref[D_MODEL:2 * D_MODEL, :], preferred_element_type=F32))
        w = w_ref[...]
        rstd = lax.rsqrt(jnp.mean(y * y, axis=-1, keepdims=True) + EPS)
        n = y * rstd
        err = (x_ref[...] + n * w) - t_ref[...]
        sq_ref[...] += jnp.sum(err * err, axis=0, keepdims=True)
        dout = err * inv_d
        dout_ref[...] = dout
        dw_ref[...] += jnp.sum(dout * n, axis=0, keepdims=True)
        dn = dout * w
        dy_ref[...] = (rstd * (dn - n * jnp.mean(dn * n, axis=-1, keepdims=True))).astype(_MXU_DTYPE)

    row = pl.BlockSpec((tm, D_MODEL), lambda i: (i, 0))
    return pl.pallas_call(
        body, name="out_fused", grid=(t // tm,),
        in_specs=[pl.BlockSpec((2, tm, D_MODEL), lambda i: (0, i, 0)), _full((2 * D_MODEL, D_MODEL)),
                  row, row, _full((1, D_MODEL))],
        out_specs=[row, row, _full((1, D_MODEL)), _full((1, D_MODEL))],
        out_shape=[jax.ShapeDtypeStruct((t, D_MODEL), F32), jax.ShapeDtypeStruct((t, D_MODEL), _MXU_DTYPE),
                   jax.ShapeDtypeStruct((1, D_MODEL), F32), jax.ShapeDtypeStruct((1, D_MODEL), F32)],
        compiler_params=_seq_params(),
    )(y2, wo, x, tgt, wpost)


def _dw_out(y2, dy, tt):
    t = dy.shape[0]
    hr = UNIT // 2

    def body(y_ref, dy_ref, g_ref):
        @pl.when(pl.program_id(1) == 0)
        def _():
            g_ref[...] = jnp.zeros_like(g_ref)

        r = _dot_tn(y_ref[...], dy_ref[...])
        g_ref[0] += r[0:hr]
        g_ref[1] += r[hr:UNIT]

    return pl.pallas_call(
        body, name="dw_out", grid=(N_CHIPS, t // tt),
        in_specs=[pl.BlockSpec((None, tt, UNIT), lambda c, k: (c // 2, k, c % 2)),
                  pl.BlockSpec((tt, D_MODEL), lambda c, k: (k, 0))],
        out_specs=pl.BlockSpec((2, None, hr, D_MODEL), lambda c, k: (0, c, 0, 0)),
        out_shape=jax.ShapeDtypeStruct((2, N_CHIPS, hr, D_MODEL), F32),
        compiler_params=pltpu.CompilerParams(dimension_semantics=("arbitrary", "arbitrary")),
    )(y2, dy)


def _lru_bwd(dy, wo, p, h, wa, wx, ba, bx, lam, cw, cb, tm):
    t = dy.shape[0]
    nt = t // tm
    ng = tm // 8

    def body(dy_ref, wo_ref, lx_ref, halo_ref, lg_ref, h_ref, hhalo_ref, wa_ref, wx_ref, ba_ref, bx_ref,
             lam_ref, cw_ref, cb_ref,
             dp_ref, dwa_ref, dwx_ref, dba_ref, dbx_ref, dsp_ref, dcw_ref, dcb_ref,
             c_s, in_s, dh_s, dhc_s, afirst_s, dxc_s):
        i = pl.program_id(0)
        first = i == nt - 1

        @pl.when(i == 0)
        def _():
            for ref in (dwa_ref, dwx_ref, dba_ref, dbx_ref, dsp_ref, dcw_ref, dcb_ref, dhc_s, afirst_s, dxc_s):
                ref[...] = jnp.zeros_like(ref)

        sp = _softplus_neg(lam_ref[...])
        cwv = cw_ref[...]
        wav = wa_ref[...]
        wxv = wx_ref[...]
        halo = jnp.where(first, 0.0, _seg(halo_ref, 0))
        xs, xc, r, ig, a, om = _lru_gates(_seg(lx_ref, 0), halo, cwv, cb_ref[...], wav, wxv,
                                          ba_ref[...], bx_ref[...], sp)
        inv_mult = lax.rsqrt(om)
        mult = om * inv_mult
        d_y = _dot_nt(dy_ref[...], wo_ref[...])
        lg = _seg(lg_ref, 0)
        sl = _sigmoid(lg)
        hv = h_ref[...]
        d_lg = d_y * hv * (sl * (1.0 + lg * (1.0 - sl)))

        c_s[...] = _shift_up(a, afirst_s[...], 1)
        in_s[...] = d_y * (lg * sl)
        row8 = _rows((8, D_MODEL))

        def group(gg, carry):
            rows = pl.ds(pl.multiple_of((ng - 1 - gg) * 8, 8), 8)
            cv = c_s[rows, :]
            uv = in_s[rows, :]
            for s in (1, 2, 4):
                c_sh = jnp.where(row8 < 8 - s, pltpu.roll(cv, 8 - s, 0), 1.0)
                u_sh = jnp.where(row8 < 8 - s, pltpu.roll(uv, 8 - s, 0), 0.0)
                uv = cv * u_sh + uv
                cv = cv * c_sh
            hh = cv * carry + uv
            dh_s[rows, :] = hh
            return jnp.broadcast_to(hh[0:1, :], (8, D_MODEL))

        dhc_s[...] = lax.fori_loop(0, ng, group, dhc_s[...])
        afirst_s[...] = a[0:8]
        dh = dh_s[...]

        h_prev = _shift_down(hv, jnp.where(first, 0.0, hhalo_ref[...]), 1)
        gx = ig * xc
        d_mult = dh * gx
        d_i = dh * (mult * xc)
        d_xc = dh * (mult * ig)
        d_la = (dh * h_prev) * a - d_mult * ((a * a) * inv_mult)
        dsp_ref[...] += jnp.sum(d_la * r, axis=0, keepdims=True) * (-LRU_C)
        d_zr = (d_la * (-LRU_C * sp)) * (r * (1.0 - r))
        d_zi = d_i * (ig * (1.0 - ig))
        dba_ref[...] += jnp.sum(d_zr, axis=0, keepdims=True)
        dbx_ref[...] += jnp.sum(d_zi, axis=0, keepdims=True)
        back = []
        for n in range(LRU_BLOCKS):
            ln = slice(n * LRU_BW, (n + 1) * LRU_BW)
            xb = _mx(xc[:, ln])
            zr_n = _mx(d_zr[:, ln])
            zi_n = _mx(d_zi[:, ln])
            dwa_ref[n] += _dot_tn(xb, zr_n)
            dwx_ref[n] += _dot_tn(xb, zi_n)
            back.append(_dot_nt(zr_n, wav[n]) + _dot_nt(zi_n, wxv[n]))
        d_xc = d_xc + jnp.concatenate(back, axis=1)
        dcb_ref[...] += jnp.sum(d_xc, axis=0, keepdims=True)
        for k in range(4):
            dcw_ref[k:k + 1, :] += jnp.sum(d_xc * xs[k], axis=0, keepdims=True)
        head = dxc_s[...]
        d_lx = cwv[3:4] * d_xc
        for k in range(3):
            d_lx = d_lx + cwv[k:k + 1] * _shift_up(d_xc, head, 3 - k)
        dxc_s[...] = d_xc[0:8]
        dp_ref[0] = d_lx.astype(_MXU_DTYPE)
        dp_ref[1] = d_lg.astype(_MXU_DTYPE)

    rev = lambda i: nt - 1 - i
    seg = lambda s: pl.BlockSpec((2, tm, UNIT), lambda i: (s, rev(i), 0))
    halo_row = lambda i: jnp.maximum(rev(i) * (tm // 8) - 1, 0)
    row = pl.BlockSpec((tm, D_MODEL), lambda i: (rev(i), 0))
    vec = _full((1, D_MODEL))
    wblk = _full((LRU_BLOCKS, LRU_BW, LRU_BW))
    return pl.pallas_call(
        body, name="lru_bwd", grid=(nt,),
        in_specs=[row, pl.BlockSpec((D_MODEL, D_MODEL), lambda i: (0, 0)),
                  seg(0), pl.BlockSpec((2, 8, UNIT), lambda i: (0, halo_row(i), 0)), seg(1),
                  row, pl.BlockSpec((8, D_MODEL), lambda i: (halo_row(i), 0)),
                  wblk, wblk, vec, vec, vec, _full((4, D_MODEL)), vec],
        out_specs=[pl.BlockSpec((2, tm, D_MODEL), lambda i: (2, rev(i), 0)),
                   wblk, wblk, vec, vec, vec, _full((4, D_MODEL)), vec],
        out_shape=[jax.ShapeDtypeStruct((N_SEG, t, D_MODEL), _MXU_DTYPE),
                   jax.ShapeDtypeStruct((LRU_BLOCKS, LRU_BW, LRU_BW), F32),
                   jax.ShapeDtypeStruct((LRU_BLOCKS, LRU_BW, LRU_BW), F32),
                   jax.ShapeDtypeStruct((1, D_MODEL), F32), jax.ShapeDtypeStruct((1, D_MODEL), F32),
                   jax.ShapeDtypeStruct((1, D_MODEL), F32), jax.ShapeDtypeStruct((4, D_MODEL), F32),
                   jax.ShapeDtypeStruct((1, D_MODEL), F32)],
        scratch_shapes=[pltpu.VMEM((tm, D_MODEL), F32), pltpu.VMEM((tm, D_MODEL), F32),
                        pltpu.VMEM((tm, D_MODEL), F32), pltpu.VMEM((8, D_MODEL), F32),
                        pltpu.VMEM((8, D_MODEL), F32), pltpu.VMEM((8, D_MODEL), F32)],
        compiler_params=_seq_params(),
    )(dy, wo, p, p, p, h, h, wa, wx, ba, bx, lam, cw, cb)


def _hgrn_bwd(dy, wo, p, o, states, logits, gw, dp, tm):
    t = dy.shape[0]
    nt = t // tm
    nc = tm // CHUNK

    def body(dy_ref, wo_ref, qf_ref, vh_ref, o_ref, st_ref, lg_ref, gw_ref, dp_in,
             dp_ref, dgw_ref, dlb_ref, dyh_s, dstate, snext):
        del dp_in
        i = pl.program_id(0)

        @pl.when(i == 0)
        def _():
            for ref in (dgw_ref, dlb_ref, dstate, snext):
                ref[...] = jnp.zeros_like(ref)

        dyh_s[...] = _dot_nt(dy_ref[...], wo_ref[...])
        l0 = lg_ref[0:1, :]
        l1 = lg_ref[1:2, :]
        lb = 1.0 / (1.0 + jnp.exp(l1 - l0))
        gwv = gw_ref[...]
        causal = _rows((CHUNK, CHUNK)) >= _cols((CHUNK, CHUNK))
        wide = 2 * HEAD_D
        zeros_h = jnp.zeros((CHUNK, HEAD_D), _MXU_DTYPE)

        for c in reversed(range(nc)):
            rows = pl.ds(c * CHUNK, CHUNK)
            q = _seg(qf_ref, 0, rows)
            hg = _seg(vh_ref, 1, rows)
            ov = o_ref[rows, :]
            dyh = dyh_s[rows, :]
            sig, f, sq, em1, em2, qm, km, ebm, ecm, ebc = _hgrn_prep(q, _seg(qf_ref, 1, rows), lb)
            ke = _mx(km.astype(F32) * ecm)
            vb = _mx(_seg(vh_ref, 0, rows))
            sg = _sigmoid(hg)
            d_on = dyh * (hg * sg)
            x_p, y_p, dv_p, n_p, ex_p = [], [], [], [], []
            for h in range(HEADS):
                ln = slice(h * HEAD_D, (h + 1) * HEAD_D)
                o_h = ov[:, ln]
                rstd = lax.rsqrt(jnp.mean(o_h * o_h, axis=-1, keepdims=True) + EPS)
                n = o_h * rstd
                dn = d_on[:, ln] * gwv[:, ln]
                d_o = rstd * (dn - n * jnp.mean(dn * n, axis=-1, keepdims=True))
                d_ob = _mx(d_o)
                d_ot = _mx(d_o.T)
                s0 = st_ref[c, h]
                ds = dstate[h]
                both = _dot_nt(jnp.concatenate([_pad_lanes([qm[:, ln]], wide), _pad_lanes([zeros_h, d_ob], wide)], axis=0),
                               jnp.concatenate([km[:, ln], vb[:, ln]], axis=1))
                att = jnp.where(causal, both[0:CHUNK], 0.0)
                d_att = jnp.where(causal, both[CHUNK:2 * CHUNK], 0.0)
                x_p.append(jnp.dot(_pad_lanes([d_ob, _mx(d_att)], wide),
                                   _pad_rows([_mx(s0 * ebm[:, ln]), km[:, ln]], wide), preferred_element_type=F32))
                y_p.append(jnp.dot(_pad_lanes([vb[:, ln], _mx(d_att.T)], wide),
                                   _pad_rows([_mx(ds * ecm[:, ln]), qm[:, ln]], wide), preferred_element_type=F32))
                dv_p.append(_dot_nt(_pad_lanes([ke[:, ln], _mx(att.T)], wide), _pad_lanes([_mx(ds), d_ot], wide)))
                ex_p.append(jnp.sum(ds * snext[h], axis=0, keepdims=True))
                dstate[h] = ds * ebc[:, ln] + jnp.dot(d_ot, qm[:, ln], preferred_element_type=F32) * ebm[:, ln]
                snext[h] = s0
                n_p.append(n)
            cat = lambda parts: jnp.concatenate(parts, axis=1)
            x_all, y_all, n_all = cat(x_p), cat(y_p), cat(n_p)
            dg = _cumsum_rows(qm.astype(F32) * x_all - km.astype(F32) * y_all, reverse=True) + cat(ex_p)
            df = dg / f - em2 * y_all
            dgw_ref[...] += jnp.sum(d_on * n_all, axis=0, keepdims=True)
            dlb_ref[...] += jnp.sum(df * (1.0 - sig), axis=0, keepdims=True)
            dp_ref[0, rows, :] = ((em1 * x_all) * (sq * (1.0 + q * (1.0 - sq)))).astype(_MXU_DTYPE)
            dp_ref[1, rows, :] = (df * ((1.0 - lb) * sig * (1.0 - sig))).astype(_MXU_DTYPE)
            dp_ref[2, rows, :] = cat(dv_p).astype(_MXU_DTYPE)
            dp_ref[3, rows, :] = (dyh * (n_all * gwv) * (sg * (1.0 + hg * (1.0 - sg)))).astype(_MXU_DTYPE)

    rev = lambda i: nt - 1 - i
    pair = lambda s: pl.BlockSpec((4, tm, UNIT), lambda i: (s, rev(i), 0))
    row = pl.BlockSpec((tm, D_MODEL), lambda i: (rev(i), 0))
    vec = _full((1, D_MODEL))
    return pl.pallas_call(
        body, name="hgrn_bwd", grid=(nt,),
        in_specs=[row, pl.BlockSpec((D_MODEL, D_MODEL), lambda i: (1, 0)), pair(1), pair(2), row,
                  pl.BlockSpec((nc, HEADS, HEAD_D, HEAD_D), lambda i: (rev(i), 0, 0, 0)),
                  _full((2, D_MODEL)), vec, pl.BlockSpec(memory_space=pl.ANY)],
        out_specs=[pl.BlockSpec((4, tm, D_MODEL), lambda i: (0, rev(i), 0)), vec, vec],
        out_shape=[jax.ShapeDtypeStruct((N_SEG, t, D_MODEL), _MXU_DTYPE),
                   jax.ShapeDtypeStruct((1, D_MODEL), F32), jax.ShapeDtypeStruct((1, D_MODEL), F32)],
        scratch_shapes=[pltpu.VMEM((tm, D_MODEL), F32), pltpu.VMEM((HEADS, HEAD_D, HEAD_D), F32),
                        pltpu.VMEM((HEADS, HEAD_D, HEAD_D), F32)],
        input_output_aliases={8: 0},
        compiler_params=_seq_params(),
    )(dy, wo, p, p, o, states, logits, gw, dp)


def _dp_unit(m):
    return ((m // 2 + 2) % N_SEG) * 2 + m % 2


def _dw_in(ut, dp, tt):
    t = ut.shape[1]
    hr = D_MODEL // 2

    def body(ut_ref, dp_ref, g_ref):
        @pl.when(pl.program_id(1) == 0)
        def _():
            g_ref[...] = jnp.zeros_like(g_ref)

        r = jnp.dot(ut_ref[...], dp_ref[...], preferred_element_type=F32)
        for h in range(2):
            for half in range(2):
                g_ref[h, half] += r[h * hr:(h + 1) * hr, half * UNIT:(half + 1) * UNIT]

    return pl.pallas_call(
        body, name="dw_in", grid=(N_SEG, t // tt),
        in_specs=[pl.BlockSpec((D_MODEL, tt), lambda s, k: (0, k)),
                  pl.BlockSpec((None, tt, D_MODEL), lambda s, k: (s, k, 0))],
        out_specs=pl.BlockSpec((2, 2, hr, UNIT), lambda s, k: (0, (s + 2) % N_SEG, 0, 0)),
        out_shape=jax.ShapeDtypeStruct((2, N_UNIT, hr, UNIT), F32),
        compiler_params=pltpu.CompilerParams(dimension_semantics=("arbitrary", "arbitrary")),
    )(ut, dp)


def _dx(dp, wg, x, dout, wpre, tm, parts, scatter):
    t = x.shape[0]
    n = len(parts)
    nt = t // tm

    def body(*refs):
        dp_ref, w_ref, x_ref, dout_ref, wpre_ref = refs[:5]
        ins = refs[5:5 + n]
        gx_ref, dw_ref = refs[5 + n:7 + n]
        outs = refs[7 + n:7 + 2 * n]
        send_sems, recv_sems = refs[7 + 2 * n:]
        i = pl.program_id(0)

        @pl.when(i == 0)
        def _():
            dw_ref[...] = jnp.zeros_like(dw_ref)
            for cp in _chip_copies(ins, outs, send_sems, recv_sems, scatter):
                cp.start()

        du = None
        for m in range(N_UNIT):
            part = _dot_nt(dp_ref[m // 2, :, (m % 2) * UNIT:(m % 2 + 1) * UNIT], w_ref[_dp_unit(m)])
            du = part if du is None else du + part
        xv = x_ref[...]
        rstd = lax.rsqrt(jnp.mean(xv * xv, axis=-1, keepdims=True) + EPS)
        xn = xv * rstd
        dw_ref[...] += jnp.sum(du * xn, axis=0, keepdims=True)
        dn = du * wpre_ref[...]
        gx_ref[...] = dout_ref[...] + rstd * (dn - xn * jnp.mean(dn * xn, axis=-1, keepdims=True))

        @pl.when(i == nt - 1)
        def _():
            copies = _chip_copies(ins, outs, send_sems, recv_sems, scatter)
            for cp in copies:
                cp.wait_recv()
            for cp in copies:
                cp.wait_send()

    row = pl.BlockSpec((tm, D_MODEL), lambda i: (i, 0))
    any_spec = pl.BlockSpec(memory_space=pl.ANY)
    res = pl.pallas_call(
        body, name="dx", grid=(nt,),
        in_specs=[pl.BlockSpec((N_SEG, tm, D_MODEL), lambda i: (0, i, 0)), _resident((N_UNIT, D_MODEL, UNIT)),
                  row, row, _full((1, D_MODEL))] + [any_spec] * n,
        out_specs=[row, _full((1, D_MODEL))] + [any_spec] * n,
        out_shape=[jax.ShapeDtypeStruct((t, D_MODEL), F32), jax.ShapeDtypeStruct((1, D_MODEL), F32)]
        + _chip_exchange_shapes(parts, scatter),
        scratch_shapes=[pltpu.SemaphoreType.DMA((max(3 * n, 1),)), pltpu.SemaphoreType.DMA((max(3 * n, 1),))],
        compiler_params=_seq_params(),
    )(dp, wg, x, dout, wpre, *parts)
    return res[0], res[1], res[2:]


CHUNK_BYTES = 256 * 1024
MAX_CHUNKS = 16


def _n_chunks(shape, dtype, rows):
    nbytes = jnp.dtype(dtype).itemsize
    for d in shape:
        nbytes *= d
    k = max(1, min(MAX_CHUNKS, rows, nbytes // CHUNK_BYTES))
    while rows % k:
        k -= 1
    return k


def _chip_exchange_shapes(arrays, scatter):
    return [jax.ShapeDtypeStruct((3,) + tuple(a.shape[1:] if sc else a.shape), a.dtype)
            for a, sc in zip(arrays, scatter)]


def _chip_copies(ins, outs, send_sems, recv_sems, scatter):
    if not ins:
        return []
    x, y, c = lax.axis_index("x"), lax.axis_index("y"), lax.axis_index("c")
    peers = [(1 - x, y), (x, 1 - y), (1 - x, 1 - y)]
    copies = []
    for a in range(len(ins)):
        for j, (px, py) in enumerate(peers):
            copies.append(pltpu.make_async_remote_copy(
                src_ref=ins[a].at[2 * px + py] if scatter[a] else ins[a], dst_ref=outs[a].at[j],
                send_sem=send_sems.at[a * 3 + j], recv_sem=recv_sems.at[a * 3 + j],
                device_id=(px, py, c), device_id_type=MESH))
    return copies


def _core_swap(arrays, name, halved):
    n = len(arrays)
    shapes = [a.shape[1:] if halved else a.shape for a in arrays]
    ks = [_n_chunks(s, a.dtype, s[0]) for s, a in zip(shapes, arrays)]
    offs = [sum(ks[:a]) for a in range(n)]

    def body(*refs):
        ins, outs = refs[:n], refs[n:2 * n]
        send_sems, recv_sems = refs[2 * n:]
        x, y, c = lax.axis_index("x"), lax.axis_index("y"), lax.axis_index("c")

        def copy(a, j):
            step = shapes[a][0] // ks[a]
            rows = pl.ds(j * step, step)
            return pltpu.make_async_remote_copy(
                src_ref=ins[a].at[1 - c, rows] if halved else ins[a].at[rows], dst_ref=outs[a].at[rows],
                send_sem=send_sems.at[offs[a] + j], recv_sem=recv_sems.at[offs[a] + j],
                device_id=(x, y, 1 - c), device_id_type=MESH)

        pieces = [(a, j) for a in range(n) for j in range(ks[a])]
        for a, j in pieces:
            copy(a, j).start()
        for a, j in pieces:
            copy(a, j).wait_recv()
        for a, j in pieces:
            copy(a, j).wait_send()

    any_spec = pl.BlockSpec(memory_space=pl.ANY)
    return pl.pallas_call(
        body, name=name,
        in_specs=[any_spec] * n, out_specs=[any_spec] * n,
        out_shape=[jax.ShapeDtypeStruct(s, a.dtype) for s, a in zip(shapes, arrays)],
        scratch_shapes=[pltpu.SemaphoreType.DMA((sum(ks),)), pltpu.SemaphoreType.DMA((sum(ks),))],
    )(*arrays)


def _finish_exchange(arrays, small):
    n = len(arrays)
    ks = [_n_chunks(a.shape, a.dtype, a.shape[0]) for a in arrays]
    offs = [sum(ks[:a]) for a in range(n)]
    total = sum(ks)

    def body(*refs):
        ins, small_ref = refs[:n], refs[n]
        outs, gathered = refs[n + 1:2 * n + 1], refs[2 * n + 1]
        send_sems, recv_sems = refs[2 * n + 2:]
        x, y, c = lax.axis_index("x"), lax.axis_index("y"), lax.axis_index("c")
        copies = []
        for a in range(n):
            step = arrays[a].shape[0] // ks[a]
            for j in range(ks[a]):
                rows = pl.ds(j * step, step)
                copies.append(pltpu.make_async_remote_copy(
                    src_ref=ins[a].at[rows], dst_ref=outs[a].at[rows],
                    send_sem=send_sems.at[offs[a] + j], recv_sem=recv_sems.at[offs[a] + j],
                    device_id=(x, y, 1 - c), device_id_type=MESH))
        for slot in range(7):
            dx, dy, dc = (slot + 1) % 2, ((slot + 1) // 2) % 2, (slot + 1) // 4
            copies.append(pltpu.make_async_remote_copy(
                src_ref=small_ref, dst_ref=gathered.at[slot],
                send_sem=send_sems.at[total + slot], recv_sem=recv_sems.at[total + slot],
                device_id=(x + dx - 2 * x * dx, y + dy - 2 * y * dy, c + dc - 2 * c * dc), device_id_type=MESH))
        for cp in copies:
            cp.start()
        for cp in copies:
            cp.wait_recv()
        for cp in copies:
            cp.wait_send()

    any_spec = pl.BlockSpec(memory_space=pl.ANY)
    res = pl.pallas_call(
        body, name="reduce_cores",
        in_specs=[any_spec] * (n + 1), out_specs=[any_spec] * (n + 1),
        out_shape=[jax.ShapeDtypeStruct(a.shape, a.dtype) for a in arrays]
        + [jax.ShapeDtypeStruct((7,) + small.shape, small.dtype)],
        scratch_shapes=[pltpu.SemaphoreType.DMA((total + 7,)), pltpu.SemaphoreType.DMA((total + 7,))],
    )(*arrays, small)
    return res[:n], res[n]


def _sum_own_half(g, b, name, out_dtype, tr=512):
    _, rows, cols = g.shape
    tr = min(tr, rows)
    core = lax.axis_index("c").reshape(1)

    def body(c_ref, g_ref, b_ref, o_ref):
        del c_ref
        o_ref[...] = (g_ref[...] + b_ref[...]).astype(out_dtype)

    return pl.pallas_call(
        body, name=name,
        grid_spec=pltpu.PrefetchScalarGridSpec(
            num_scalar_prefetch=1, grid=(rows // tr,),
            in_specs=[pl.BlockSpec((None, tr, cols), lambda i, c_ref: (c_ref[0], i, 0)),
                      pl.BlockSpec((tr, cols), lambda i, c_ref: (i, 0))],
            out_specs=pl.BlockSpec((tr, cols), lambda i, c_ref: (i, 0))),
        out_shape=jax.ShapeDtypeStruct((rows, cols), out_dtype),
        compiler_params=pltpu.CompilerParams(dimension_semantics=("parallel",)),
    )(core, g, b)


def _sum_chips(own, r, slabbed, name, tr=512):
    _, rows, cols = r.shape
    tr = min(tr, rows)
    chip = (2 * lax.axis_index("x") + lax.axis_index("y")).reshape(1)

    def body(c_ref, own_ref, r_ref, o_ref):
        del c_ref
        f = lambda val: val.astype(F32)
        o_ref[...] = (f(own_ref[...]) + f(r_ref[0])) + (f(r_ref[1]) + f(r_ref[2]))

    if slabbed:
        own_spec = pl.BlockSpec((None, tr, cols), lambda i, c_ref: (c_ref[0], i, 0))
    else:
        own_spec = pl.BlockSpec((tr, cols), lambda i, c_ref: (i, 0))
    return pl.pallas_call(
        body, name=name,
        grid_spec=pltpu.PrefetchScalarGridSpec(
            num_scalar_prefetch=1, grid=(rows // tr,),
            in_specs=[own_spec, pl.BlockSpec((3, tr, cols), lambda i, c_ref: (0, i, 0))],
            out_specs=pl.BlockSpec((tr, cols), lambda i, c_ref: (i, 0))),
        out_shape=jax.ShapeDtypeStruct((rows, cols), F32),
        compiler_params=pltpu.CompilerParams(dimension_semantics=("parallel",)),
    )(chip, own, r)


def _adamw_big(own, other, w, m, v, name, grid, g_block, g_index, w_block, w_index):
    core = lax.axis_index("c").reshape(1)
    half_axis = len(grid) - 1

    def body(c_ref, own_ref, oth_ref, w_ref, m_ref, v_ref, g_ref, d_ref, nm_ref, nv_ref):
        mine = pl.program_id(half_axis) == c_ref[0]
        gv = jnp.where(mine, own_ref[...], oth_ref[...])
        g_ref[...] = gv
        d_ref[...], nm_ref[...], nv_ref[...] = _adamw(w_ref[...], gv, m_ref[...], v_ref[...])

    g_spec = pl.BlockSpec(g_block, lambda *a: g_index(*a[:half_axis]))
    w_spec = pl.BlockSpec(w_block, lambda *a: w_index(*a[:-1]))
    shp = jax.ShapeDtypeStruct(w.shape, F32)
    return pl.pallas_call(
        body, name=name,
        grid_spec=pltpu.PrefetchScalarGridSpec(
            num_scalar_prefetch=1, grid=grid,
            in_specs=[g_spec, g_spec, w_spec, w_spec, w_spec], out_specs=[w_spec] * 4),
        out_shape=[shp] * 4,
        compiler_params=pltpu.CompilerParams(dimension_semantics=("parallel",) * len(grid)),
    )(core, own, other, w, m, v)


def _small_finish(own, others, lam, logits):
    def body(a_ref, b_ref, lam_ref, lg_ref, o_ref):
        g = (((a_ref[...] + b_ref[0]) + (b_ref[1] + b_ref[2]))
             + ((b_ref[3] + b_ref[4]) + (b_ref[5] + b_ref[6])))
        o_ref[...] = g
        o_ref[8:9, :] = g[8:9, :] * (-_sigmoid(-lam_ref[...]))
        lb = 1.0 / (1.0 + jnp.exp(lg_ref[1:2, :] - lg_ref[0:1, :]))
        d0 = g[9:10, :] * (lb * (1.0 - lb))
        o_ref[9:10, :] = d0
        o_ref[10:11, :] = -d0

    return pl.pallas_call(
        body, name="small_finish",
        in_specs=[_full((SMALL_ROWS, D_MODEL)), _full((7, SMALL_ROWS, D_MODEL)), _full((1, D_MODEL)),
                  _full((2, D_MODEL))],
        out_specs=_full((SMALL_ROWS, D_MODEL)),
        out_shape=jax.ShapeDtypeStruct((SMALL_ROWS, D_MODEL), F32),
    )(own, others, lam, logits)


def _adamw_small(ws, gs, ms, vs):
    n = len(ws)

    def body(*refs):
        w_r, g_r, m_r, v_r = refs[:n], refs[n:2 * n], refs[2 * n:3 * n], refs[3 * n:4 * n]
        d_o, m_o, v_o = refs[4 * n:5 * n], refs[5 * n:6 * n], refs[6 * n:7 * n]
        for j in range(n):
            d_o[j][...], m_o[j][...], v_o[j][...] = _adamw(w_r[j][...], g_r[j][...], m_r[j][...], v_r[j][...])

    specs = [_full(w.shape) for w in ws]
    shapes = [jax.ShapeDtypeStruct(w.shape, F32) for w in ws]
    outs = pl.pallas_call(
        body, name="adamw_small",
        in_specs=specs * 4, out_specs=specs * 3, out_shape=shapes * 3,
    )(*ws, *gs, *ms, *vs)
    return outs[:n], outs[n:2 * n], outs[2 * n:]


def _local_step(x, tgt, ut, p, pre_w, wg, cw, cb, wa, wx, ba, bx, lam, logits, gw, wo, post_w, tm, tmm,
                prepare=None):
    h, y2 = _lru_fwd(p, wa, wx, ba, bx, lam, cw, cb, tm)
    y2, o, states = _hgrn_fwd(p, logits, gw, y2, tm)
    dout, dy, sq, d_post = _out_fused(y2, wo, x, tgt, post_w, tmm)
    g_out = _dw_out(y2, dy, min(2048, x.shape[0]))
    dp, d_wa, d_wx, d_ba, d_bx, d_sp, d_cw, d_cb = _lru_bwd(dy, wo, p, h, wa, wx, ba, bx, lam, cw, cb, tm)
    dp, d_gw, d_lb = _hgrn_bwd(dy, wo, p, o, states, logits, gw, dp, tm)
    g_in = _dw_in(ut, dp, min(2048, x.shape[0]))
    parts, scatter = prepare(g_in, g_out, d_wa, d_wx) if prepare is not None else ([], [])
    grad_x, d_pre, received = _dx(dp, wg, x, dout, pre_w, tm, parts, scatter)
    small = jnp.concatenate([d_pre, d_cw, d_cb, d_ba, d_bx, d_sp, d_lb,
                             jnp.zeros((1, D_MODEL), F32), d_gw, d_post, sq,
                             jnp.zeros((SMALL_ROWS - 14, D_MODEL), F32)], axis=0)
    return grad_x, g_in, g_out, d_wa, d_wx, small, parts, received


def kernel(x, pre_norm_w, w_in, conv_w, conv_b, lru_w_a, lru_b_a, lru_w_x, lru_b_x, lru_lambda, hgrn_lb_logits, hgrn_gnorm_w, w_out, post_norm_w, loss_target, m_pre_norm_w, m_w_in, m_conv_w, m_conv_b, m_lru_w_a, m_lru_b_a, m_lru_w_x, m_lru_b_x, m_lru_lambda, m_hgrn_lb_logits, m_hgrn_gnorm_w, m_w_out, m_post_norm_w, v_pre_norm_w, v_w_in, v_conv_w, v_conv_b, v_lru_w_a, v_lru_b_a, v_lru_w_x, v_lru_b_x, v_lru_lambda, v_hgrn_lb_logits, v_hgrn_gnorm_w, v_w_out, v_post_norm_w):
    t = x.shape[1]
    tm = min(256, t)
    tmm = min(512, t)
    chip = 2 * lax.axis_index("x") + lax.axis_index("y")

    win_units = w_in[0].astype(_MXU_DTYPE).reshape(D_MODEL, 3, UNIT).transpose(1, 0, 2)
    small_w = jnp.concatenate([conv_w[0], lru_b_a[0], lru_b_x[0]], axis=1)
    ut, p, wg4, (wo4, wa4, wx4), sw4 = _in_proj_gather(
        x[0], pre_norm_w, win_units,
        [w_out[0].astype(_MXU_DTYPE).reshape(2, 256, D_MODEL), lru_w_a[0].astype(_MXU_DTYPE).reshape(2, 128, LRU_BW),
         lru_w_x[0].astype(_MXU_DTYPE).reshape(2, 128, LRU_BW)],
        small_w, min(512, t))
    wg = wg4.reshape(N_UNIT, D_MODEL, UNIT)
    wo = wo4.reshape(2 * D_MODEL, D_MODEL)
    by_block = lambda w4: w4.reshape(N_CHIPS, LRU_BLOCKS, 64, LRU_BW).transpose(1, 0, 2, 3).reshape(
        LRU_BLOCKS, LRU_BW, LRU_BW)
    wa, wx = by_block(wa4), by_block(wx4)
    sw4 = sw4.reshape(N_CHIPS, 4, 384)
    cw = sw4[:, :, 0:256].transpose(1, 0, 2).reshape(4, D_MODEL)
    ba = sw4[:, :, 256:320].transpose(1, 0, 2).reshape(1, D_MODEL)
    bx = sw4[:, :, 320:384].transpose(1, 0, 2).reshape(1, D_MODEL)

    def lru_layout(g):
        g = g.reshape(2, 2, N_CHIPS, 64, LRU_BW).transpose(0, 2, 1, 3, 4)
        return g.reshape(2, N_CHIPS * 128, LRU_BW)

    def prepare(g_in, g_out, d_wa, d_wx):
        g_in = g_in.reshape(2, N_UNIT * 512, UNIT)
        g_out = g_out.reshape(2, N_CHIPS * 256, D_MODEL)
        g_wa, g_wx = lru_layout(d_wa), lru_layout(d_wx)
        b_in, b_out, b_wa, b_wx = _core_swap([g_in, g_out, g_wa, g_wx], "reduce_swap", halved=True)
        parts = [_sum_own_half(g_in, b_in, "presum_in", _WIRE_DTYPE).reshape(N_CHIPS, 3 * 512, UNIT),
                 _sum_own_half(g_out, b_out, "presum_out", _WIRE_DTYPE).reshape(N_CHIPS, 256, D_MODEL),
                 _sum_own_half(g_wa, b_wa, "presum_wa", _WIRE_DTYPE).reshape(N_CHIPS, 128, LRU_BW),
                 _sum_own_half(g_wx, b_wx, "presum_wx", _WIRE_DTYPE).reshape(N_CHIPS, 128, LRU_BW)]
        return parts, [True] * 4

    grad_x, _, _, _, _, small, (p_in, p_out, p_wa, p_wx), (r_in, r_out, r_wa, r_wx) = _local_step(
        x[0], loss_target[0], ut, p, pre_norm_w, wg, cw, conv_b, wa, wx, ba, bx, lru_lambda, hgrn_lb_logits,
        hgrn_gnorm_w, wo, post_norm_w, tm, tmm, prepare)
    s_in = _sum_chips(p_in, r_in, True, "sum_in")
    s_out = _sum_chips(p_out, r_out, True, "sum_out")
    s_wa = _sum_chips(p_wa, r_wa, True, "sum_wa")
    s_wx = _sum_chips(p_wx, r_wx, True, "sum_wx")
    (o_in, o_out, o_wa, o_wx), small_others = _finish_exchange([s_in, s_out, s_wa, s_wx], small)

    g_w_in, d_w_in, nm_w_in, nv_w_in = _adamw_big(
        s_in.reshape(3, 512, UNIT), o_in.reshape(3, 512, UNIT), w_in[0], m_w_in[0], v_w_in[0], "adamw_w_in", (3, 2),
        (None, 512, UNIT), lambda k: (k, 0, 0), (512, UNIT), lambda k, h: (h, k))
    g_w_out, d_w_out, nm_w_out, nv_w_out = _adamw_big(
        s_out, o_out, w_out[0], m_w_out[0], v_w_out[0], "adamw_w_out", (2,),
        (256, D_MODEL), lambda: (0, 0), (256, D_MODEL), lambda h: (h, 0))
    sq2 = lambda a: a.reshape(LRU_BW, LRU_BW)
    lru_specs = ((2,), (128, LRU_BW), lambda: (0, 0), (128, LRU_BW), lambda h: (h, 0))
    g_wa, d_wa2, nm_wa, nv_wa = _adamw_big(s_wa, o_wa, sq2(lru_w_a), sq2(m_lru_w_a), sq2(v_lru_w_a), "adamw_wa",
                                           *lru_specs)
    g_wx, d_wx2, nm_wx, nv_wx = _adamw_big(s_wx, o_wx, sq2(lru_w_x), sq2(m_lru_w_x), sq2(v_lru_w_x), "adamw_wx",
                                           *lru_specs)

    gs = _small_finish(small, small_others, lru_lambda, hgrn_lb_logits)
    loss = jnp.sum(gs[13]) * (0.5 / D_MODEL)
    g_pre = gs[0:1]
    g_cw = lax.dynamic_slice(gs[1:5], (0, chip * 256), (4, 256))
    g_cb = gs[5:6]
    g_ba = lax.dynamic_slice(gs[6].reshape(LRU_BLOCKS, N_CHIPS, 64), (0, chip, 0), (LRU_BLOCKS, 1, 64)).reshape(4, 64)
    g_bx = lax.dynamic_slice(gs[7].reshape(LRU_BLOCKS, N_CHIPS, 64), (0, chip, 0), (LRU_BLOCKS, 1, 64)).reshape(4, 64)
    g_lam = gs[8:9]
    g_lb = gs[9:11]
    g_gw = gs[11:12]
    g_post = gs[12:13]
    small_g = [g_pre, g_cw, g_cb, g_ba, g_bx, g_lam, g_lb, g_gw, g_post]
    two_d = lambda a: a.reshape(a.shape[-2:])
    small_w_list = [pre_norm_w, conv_w, conv_b, lru_b_a, lru_b_x, lru_lambda, hgrn_lb_logits, hgrn_gnorm_w, post_norm_w]
    small_m_list = [m_pre_norm_w, m_conv_w, m_conv_b, m_lru_b_a, m_lru_b_x, m_lru_lambda, m_hgrn_lb_logits,
                    m_hgrn_gnorm_w, m_post_norm_w]
    small_v_list = [v_pre_norm_w, v_conv_w, v_conv_b, v_lru_b_a, v_lru_b_x, v_lru_lambda, v_hgrn_lb_logits,
                    v_hgrn_gnorm_w, v_post_norm_w]
    sd, sm, sv = _adamw_small([two_d(a) for a in small_w_list], small_g,
                              [two_d(a) for a in small_m_list], [two_d(a) for a in small_v_list])

    def shaped(vals, refs):
        return [val.reshape(ref.shape) for val, ref in zip(vals, refs)]

    s_g = shaped(small_g, small_w_list)
    s_d = shaped(sd, small_w_list)
    s_m = shaped(sm, small_w_list)
    s_v = shaped(sv, small_w_list)

    def ordered(small, big_in, big_wa, big_wx, big_out):
        pre, cw_, cb_, ba_, bx_, lam_, lb_, gw_, post_ = small
        return [pre, big_in.reshape(w_in.shape), cw_, cb_, big_wa.reshape(lru_w_a.shape), ba_,
                big_wx.reshape(lru_w_x.shape), bx_, lam_, lb_, gw_, big_out.reshape(w_out.shape), post_]

    grads = ordered(s_g, g_w_in, g_wa, g_wx, g_w_out)
    deltas = ordered(s_d, d_w_in, d_wa2, d_wx2, d_w_out)
    new_m = ordered(s_m, nm_w_in, nm_wa, nm_wx, nm_w_out)
    new_v = ordered(s_v, nv_w_in, nv_wa, nv_wx, nv_w_out)
    return (loss, grad_x.reshape(x.shape), *grads, *deltas, *new_m, *new_v)
```

```python
import functools

import jax
import jax.numpy as jnp
from jax import lax
from jax.experimental import pallas as pl
from jax.experimental.pallas import tpu as pltpu

F32 = jnp.float32
BF16 = jnp.bfloat16
_MXU_DTYPE = jnp.bfloat16
_WIRE_DTYPE = jnp.bfloat16

D_MODEL = 1024
N_SEG = 6
UNIT = 512
N_UNIT = 12
LRU_BLOCKS = 4
LRU_BW = 256
LRU_C = 8.0
HEADS = 8
HEAD_D = 128
CHUNK = 64
EPS = 1e-6
N_CHIPS = 4
EXP_CLAMP = 80.0

ADAM_LR = 0.001
ADAM_B1 = 0.9
ADAM_B2 = 0.999
ADAM_EPS = 1e-08
ADAM_WD = 0.01
ADAM_STEP = 10

SMALL_ROWS = 16
FWD_PIECES = 4
MESH = pl.DeviceIdType.MESH


def _sigmoid(x):
    return 0.5 * jnp.tanh(0.5 * x) + 0.5


def _mx(x):
    return x.astype(_MXU_DTYPE)


def _dot(a, b):
    return jnp.dot(_mx(a), _mx(b), preferred_element_type=F32)


def _dot_nt(a, b):
    return lax.dot_general(_mx(a), _mx(b), (((1,), (1,)), ((), ())), preferred_element_type=F32)


def _dot_tn(a, b):
    return lax.dot_general(_mx(a), _mx(b), (((0,), (0,)), ((), ())), preferred_element_type=F32)


def _rows(shape):
    return lax.broadcasted_iota(jnp.int32, shape, 0)


def _cols(shape):
    return lax.broadcasted_iota(jnp.int32, shape, 1)


def _softplus_neg(lam):
    z = -lam
    e = jnp.exp(-jnp.abs(z))
    series = e * (1.0 - e * (0.5 - e * (1.0 / 3.0 - 0.25 * e)))
    return jnp.maximum(z, 0.0) + jnp.where(e < 1e-2, series, jnp.log(1.0 + e))


def _one_minus_sq(a, y):
    series = -y * (1.0 + y * (0.5 + y * (1.0 / 6.0)))
    return jnp.where(y > -0.01, series, 1.0 - a * a)


def _cumsum_rows(x, reverse=False):
    n = x.shape[0] // 8
    row8 = _rows((8, x.shape[1]))
    out = [None] * n
    carry = None
    for g in (reversed(range(n)) if reverse else range(n)):
        blk = x[8 * g:8 * g + 8]
        for s in (1, 2, 4):
            if reverse:
                blk = blk + jnp.where(row8 < 8 - s, pltpu.roll(blk, 8 - s, 0), 0.0)
            else:
                blk = blk + jnp.where(row8 >= s, pltpu.roll(blk, s, 0), 0.0)
        if carry is not None:
            blk = blk + carry
        carry = blk[0:1] if reverse else blk[7:8]
        out[g] = blk
    return jnp.concatenate(out, axis=0)


def _shift_down(x, halo, s):
    if s == 0:
        return x
    r = pltpu.roll(x, s, 0)
    top = jnp.where(_rows(halo.shape) < s, pltpu.roll(halo, s, 0), r[0:8])
    return jnp.concatenate([top, r[8:]], axis=0)


def _shift_up(x, head, s):
    if s == 0:
        return x
    n = x.shape[0]
    r = pltpu.roll(x, n - s, 0)
    bottom = jnp.where(_rows(head.shape) >= 8 - s, pltpu.roll(head, 8 - s, 0), r[n - 8:n])
    return jnp.concatenate([r[:n - 8], bottom], axis=0)


def _lru_gates(lx, halo, cw, cb, wa, wx, ba, bx, sp):
    xs = [_shift_down(lx, halo, 3 - k) for k in range(4)]
    xc = cb + cw[0:1] * xs[0] + cw[1:2] * xs[1] + cw[2:3] * xs[2] + cw[3:4] * xs[3]
    zr, zi = [], []
    for n in range(LRU_BLOCKS):
        xb = _mx(xc[:, n * LRU_BW:(n + 1) * LRU_BW])
        zr.append(jnp.dot(xb, wa[n], preferred_element_type=F32))
        zi.append(jnp.dot(xb, wx[n], preferred_element_type=F32))
    r = _sigmoid(jnp.concatenate(zr, axis=1) + ba)
    ig = _sigmoid(jnp.concatenate(zi, axis=1) + bx)
    la = (-LRU_C * sp) * r
    a = jnp.exp(la)
    om = _one_minus_sq(a, 2.0 * la)
    return xs, xc, r, ig, a, om


def _hgrn_prep(q, fr, lb):
    sig = _sigmoid(fr)
    f = lb + (1.0 - lb) * sig
    sq = _sigmoid(q)
    b = _cumsum_rows(jnp.log(f))
    bm = b[CHUNK // 2 - 1:CHUNK // 2]
    bc = b[CHUNK - 1:CHUNK]
    em1 = jnp.exp(jnp.minimum(b - bm, EXP_CLAMP))
    em2 = jnp.exp(jnp.minimum(bm - b, EXP_CLAMP))
    qm = _mx((q * sq) * em1)
    km = _mx((1.0 - f) * em2)
    return sig, f, sq, em1, em2, qm, km, jnp.exp(bm), jnp.exp(bc - bm), jnp.exp(bc)


def _seg(ref, s, rows=slice(None)):
    return jnp.concatenate([ref[2 * s, rows, :], ref[2 * s + 1, rows, :]], axis=1)


def _pad_lanes(parts, width):
    have = sum(p.shape[1] for p in parts)
    pad = [jnp.zeros((parts[0].shape[0], width - have), parts[0].dtype)] if width > have else []
    return jnp.concatenate(list(parts) + pad, axis=1)


def _pad_rows(parts, height):
    have = sum(p.shape[0] for p in parts)
    pad = [jnp.zeros((height - have, parts[0].shape[1]), parts[0].dtype)] if height > have else []
    return jnp.concatenate(list(parts) + pad, axis=0)


def _adamw(w, g, m, v):
    m = ADAM_B1 * m + (1.0 - ADAM_B1) * g
    v = ADAM_B2 * v + (1.0 - ADAM_B2) * jnp.square(g)
    m_hat = m / (1.0 - ADAM_B1 ** ADAM_STEP)
    v_hat = v / (1.0 - ADAM_B2 ** ADAM_STEP)
    delta = -ADAM_LR * (m_hat / (jnp.sqrt(v_hat) + ADAM_EPS) + ADAM_WD * w)
    return delta, m, v


def _full(shape):
    nd = len(shape)
    return pl.BlockSpec(shape, lambda *_: (0,) * nd)


def _seq_params():
    return pltpu.CompilerParams(dimension_semantics=("arbitrary",))


def _resident(shape):
    nd = len(shape)
    return pl.BlockSpec(shape, lambda *_: (0,) * nd, pipeline_mode=pl.Buffered(1))


def _in_proj_gather(x, w, win_units, halved, small_w, tm):
    t = x.shape[0]
    nt = t // tm
    n_h = len(halved)
    n_split = 3 + n_h
    n_items = n_split + 1
    hu = D_MODEL // 2
    sub = min(256, tm)
    chip = 2 * lax.axis_index("x") + lax.axis_index("y")
    order = jnp.stack([chip] + [jnp.bitwise_xor(chip, flip) for flip in (2, 1, 3)])

    def body(order_ref, x_ref, w_ref, win_ref, *rest):
        del order_ref
        h_ins, small_ref = rest[:n_h], rest[n_h]
        ut_ref, p_ref, wg_ref = rest[n_h + 1:n_h + 4]
        h_outs, sw_ref = rest[n_h + 4:2 * n_h + 4], rest[2 * n_h + 4]
        u_s, wbuf, send, recv, fsend, frecv, osend, orecv, wsem = rest[2 * n_h + 5:]
        g, i = pl.program_id(0), pl.program_id(1)
        mx, my, c = lax.axis_index("x"), lax.axis_index("y"), lax.axis_index("c")
        me = 2 * mx + my
        peers = [(1 - mx, my), (mx, 1 - my), (1 - mx, 1 - my)]
        qs = [2 * px + py for px, py in peers]
        sibling = (mx, my, 1 - c)

        def remote(src, dst, ssem, rsem, dev):
            return pltpu.make_async_remote_copy(src_ref=src, dst_ref=dst, send_sem=ssem, recv_sem=rsem,
                                                device_id=dev, device_id_type=MESH)

        def mine(item):
            if item < 3:
                return win_ref.at[item, pl.ds(c * hu, hu)]
            return h_ins[item - 3].at[c] if item < n_split else small_ref

        def piece(item, q, h):
            if item < 3:
                return wg_ref.at[q, item, pl.ds(h * hu, hu)]
            return h_outs[item - 3].at[q, h] if item < n_split else sw_ref.at[q]

        def fetch(item, jj, q):
            px, py = peers[jj]
            return remote(mine(item), piece(item, q, c), send.at[item * 3 + jj], recv.at[item * 3 + jj], (px, py, c))

        def forward(item, jj, h):
            whole = piece(item, qs[jj], h)
            step = whole.shape[0] // FWD_PIECES
            copies = []
            for k in range(FWD_PIECES):
                part = whole.at[pl.ds(k * step, step)]
                sem = (item * 3 + jj) * FWD_PIECES + k
                copies.append(remote(part, part, fsend.at[sem], frecv.at[sem], sibling))
            return copies

        def own(item):
            if item < 3:
                src, dst = win_ref.at[item], wg_ref.at[me, item]
            elif item < n_split:
                src, dst = h_ins[item - 3], h_outs[item - 3].at[me]
            else:
                src, dst = small_ref, sw_ref.at[me]
            return remote(src, dst, osend.at[item], orecv.at[item], sibling)

        def load_units(gg):
            src = lambda k: win_ref.at[k] if gg == 0 else wg_ref.at[qs[gg - 1], k]
            return [pltpu.make_async_copy(src(k), wbuf.at[gg % 2, k], wsem.at[(gg % 2) * 3 + k]) for k in range(3)]

        def in_pass(gg):
            return pl.when(g == gg)

        def pass_on(gg):
            for item in range(3):
                fetch(item, gg, qs[gg]).wait_recv()
                for cp in forward(item, gg, c):
                    cp.start()

        @pl.when(i == 0)
        def _():
            @in_pass(0)
            def _():
                for item in range(3):
                    for jj in range(2):
                        fetch(item, jj, me).start()
                for item in range(n_items):
                    own(item).start()
                for cp in load_units(0):
                    cp.start()

            @in_pass(1)
            def _():
                for item in range(3):
                    fetch(item, 2, me).start()

            @in_pass(2)
            def _():
                for item in range(3, n_items):
                    for jj in range(3):
                        fetch(item, jj, me).start()

            for gg in range(N_CHIPS):
                @in_pass(gg)
                def _(gg=gg):
                    for cp in load_units(gg):
                        cp.wait()

        @pl.when(g == 0)
        def _():
            xv = x_ref[...]
            rstd = lax.rsqrt(jnp.mean(xv * xv, axis=-1, keepdims=True) + EPS)
            u = xv * rstd * w_ref[...]
            ut_ref[...] = u.T.astype(_MXU_DTYPE)
            u_s[pl.ds(pl.multiple_of(i * tm, tm), tm), :] = _mx(u)

        slot = g % 2
        for r in range(tm // sub):
            ur = u_s[pl.ds(pl.multiple_of(i * tm + r * sub, sub), sub), :]
            for k in range(3):
                p_ref[k, r * sub:(r + 1) * sub, :] = jnp.dot(ur, wbuf[slot, k], preferred_element_type=F32)

        @pl.when(i == nt // 2)
        def _():
            for gg in (1, 2):
                in_pass(gg)(functools.partial(pass_on, gg))

        @pl.when(i == nt - 1)
        def _():
            in_pass(0)(functools.partial(pass_on, 0))
            for gg in range(3):
                @in_pass(gg)
                def _(gg=gg):
                    for item in range(3):
                        for cp in forward(item, gg, 1 - c):
                            cp.wait_recv()
                    for cp in load_units(gg + 1):
                        cp.start()

            @in_pass(N_CHIPS - 1)
            def _():
                for item in range(3, n_items):
                    for jj in range(3):
                        fetch(item, jj, qs[jj]).wait_recv()
                        if item < n_split:
                            for cp in forward(item, jj, c):
                                cp.start()
                for item in range(3, n_split):
                    for jj in range(3):
                        for cp in forward(item, jj, 1 - c):
                            cp.wait_recv()
                for item in range(n_items):
                    own(item).wait_recv()
                for item in range(n_items):
                    for jj in range(3):
                        fetch(item, jj, me).wait_send()
                        if item < n_split:
                            for cp in forward(item, jj, c):
                                cp.wait_send()
                    own(item).wait_send()

    any_spec = pl.BlockSpec(memory_space=pl.ANY)
    first_pass = lambda g, i: jnp.where(g == 0, i, nt - 1)
    res = pl.pallas_call(
        body, name="in_proj",
        grid_spec=pltpu.PrefetchScalarGridSpec(
            num_scalar_prefetch=1, grid=(N_CHIPS, nt),
            in_specs=[pl.BlockSpec((tm, D_MODEL), lambda g, i, o: (first_pass(g, i), 0)),
                      pl.BlockSpec((1, D_MODEL), lambda g, i, o: (0, 0))] + [any_spec] * (n_h + 2),
            out_specs=[pl.BlockSpec((D_MODEL, tm), lambda g, i, o: (0, first_pass(g, i))),
                       pl.BlockSpec((3, tm, UNIT), lambda g, i, o: (o[g], i, 0))]
            + [any_spec] * (n_h + 2),
            scratch_shapes=[pltpu.VMEM((t, D_MODEL), _MXU_DTYPE), pltpu.VMEM((2, 3, D_MODEL, UNIT), _MXU_DTYPE),
                            pltpu.SemaphoreType.DMA((3 * n_items,)), pltpu.SemaphoreType.DMA((3 * n_items,)),
                            pltpu.SemaphoreType.DMA((3 * n_split * FWD_PIECES,)),
                            pltpu.SemaphoreType.DMA((3 * n_split * FWD_PIECES,)),
                            pltpu.SemaphoreType.DMA((n_items,)), pltpu.SemaphoreType.DMA((n_items,)),
                            pltpu.SemaphoreType.DMA((6,))]),
        out_shape=[jax.ShapeDtypeStruct((D_MODEL, t), _MXU_DTYPE), jax.ShapeDtypeStruct((N_UNIT, t, UNIT), F32),
                   jax.ShapeDtypeStruct((N_CHIPS,) + win_units.shape, win_units.dtype)]
        + [jax.ShapeDtypeStruct((N_CHIPS,) + a.shape, a.dtype) for a in halved]
        + [jax.ShapeDtypeStruct((N_CHIPS,) + small_w.shape, small_w.dtype)],
        compiler_params=pltpu.CompilerParams(dimension_semantics=("arbitrary", "arbitrary")),
    )(order, x, w, win_units, *halved, small_w)
    return res[0], res[1], res[2], res[3:3 + n_h], res[3 + n_h]


def _lru_fwd(p, wa, wx, ba, bx, lam, cw, cb, tm):
    t = p.shape[1]
    ng = tm // 8

    def body(lx_ref, halo_ref, lg_ref, wa_ref, wx_ref, ba_ref, bx_ref, lam_ref, cw_ref, cb_ref,
             h_ref, y_ref, a_s, u_s, hc_s):
        i = pl.program_id(0)

        @pl.when(i == 0)
        def _():
            hc_s[...] = jnp.zeros_like(hc_s)

        sp = _softplus_neg(lam_ref[...])
        halo = jnp.where(i == 0, 0.0, _seg(halo_ref, 0))
        _, xc, _, ig, a, om = _lru_gates(_seg(lx_ref, 0), halo, cw_ref[...], cb_ref[...], wa_ref[...],
                                         wx_ref[...], ba_ref[...], bx_ref[...], sp)
        a_s[...] = a
        u_s[...] = jnp.sqrt(om) * (ig * xc)
        row8 = _rows((8, D_MODEL))

        def group(g, hc):
            rows = pl.ds(pl.multiple_of(g * 8, 8), 8)
            av = a_s[rows, :]
            uv = u_s[rows, :]
            for s in (1, 2, 4):
                a_sh = jnp.where(row8 >= s, pltpu.roll(av, s, 0), 1.0)
                u_sh = jnp.where(row8 >= s, pltpu.roll(uv, s, 0), 0.0)
                uv = av * u_sh + uv
                av = av * a_sh
            hh = av * hc + uv
            h_ref[rows, :] = hh
            return jnp.broadcast_to(hh[7:8, :], (8, D_MODEL))

        hc_s[...] = lax.fori_loop(0, ng, group, hc_s[...])
        lg = _seg(lg_ref, 0)
        y_ref[...] = (h_ref[...] * (lg * _sigmoid(lg))).astype(_MXU_DTYPE)

    seg = lambda s: pl.BlockSpec((2, tm, UNIT), lambda i: (s, i, 0))
    return pl.pallas_call(
        body, name="lru_fwd", grid=(t // tm,),
        in_specs=[seg(0),
                  pl.BlockSpec((2, 8, UNIT), lambda i: (0, jnp.maximum(i * (tm // 8) - 1, 0), 0)),
                  seg(1),
                  _full((LRU_BLOCKS, LRU_BW, LRU_BW)), _full((LRU_BLOCKS, LRU_BW, LRU_BW)),
                  _full((1, D_MODEL)), _full((1, D_MODEL)), _full((1, D_MODEL)),
                  _full((4, D_MODEL)), _full((1, D_MODEL))],
        out_specs=[pl.BlockSpec((tm, D_MODEL), lambda i: (i, 0)),
                   pl.BlockSpec((None, tm, D_MODEL), lambda i: (0, i, 0))],
        out_shape=[jax.ShapeDtypeStruct((t, D_MODEL), F32),
                   jax.ShapeDtypeStruct((2, t, D_MODEL), _MXU_DTYPE)],
        scratch_shapes=[pltpu.VMEM((tm, D_MODEL), F32), pltpu.VMEM((tm, D_MODEL), F32),
                        pltpu.VMEM((8, D_MODEL), F32)],
        compiler_params=_seq_params(),
    )(p, p, p, wa, wx, ba, bx, lam, cw, cb)


def _hgrn_fwd(p, logits, gw, y2, tm):
    t = p.shape[1]
    nc = tm // CHUNK

    def body(qf_ref, vh_ref, lg_ref, gw_ref, y_in, y_ref, o_ref, st_ref, state):
        del y_in
        i = pl.program_id(0)

        @pl.when(i == 0)
        def _():
            state[...] = jnp.zeros_like(state)

        l0 = lg_ref[0:1, :]
        l1 = lg_ref[1:2, :]
        lb = 1.0 / (1.0 + jnp.exp(l1 - l0))
        gwv = gw_ref[...]
        causal = _rows((CHUNK, CHUNK)) >= _cols((CHUNK, CHUNK))

        for c in range(nc):
            rows = pl.ds(c * CHUNK, CHUNK)
            v = _seg(vh_ref, 0, rows)
            hg = _seg(vh_ref, 1, rows)
            _, _, _, _, _, qm, km, ebm, ecm, ebc = _hgrn_prep(_seg(qf_ref, 0, rows), _seg(qf_ref, 1, rows), lb)
            ke = _mx(km.astype(F32) * ecm)
            gate = gwv * (hg * _sigmoid(hg))
            o_parts, y_parts = [], []
            for h in range(HEADS):
                ln = slice(h * HEAD_D, (h + 1) * HEAD_D)
                s0 = state[h]
                st_ref[c, h] = s0
                vt = _mx(v[:, ln].T)
                att = _mx(jnp.where(causal, _dot_nt(qm[:, ln], km[:, ln]), 0.0))
                o = _dot_nt(_pad_lanes([qm[:, ln], att], 2 * HEAD_D),
                            _pad_lanes([_mx(s0 * ebm[:, ln]), vt], 2 * HEAD_D))
                state[h] = s0 * ebc[:, ln] + jnp.dot(vt, ke[:, ln], preferred_element_type=F32)
                rstd = lax.rsqrt(jnp.mean(o * o, axis=-1, keepdims=True) + EPS)
                o_parts.append(o)
                y_parts.append((o * rstd * gate[:, ln]).astype(_MXU_DTYPE))
            o_ref[rows, :] = jnp.concatenate(o_parts, axis=1)
            y_ref[rows, :] = jnp.concatenate(y_parts, axis=1)

    pair = lambda s: pl.BlockSpec((4, tm, UNIT), lambda i: (s, i, 0))
    return pl.pallas_call(
        body, name="hgrn_fwd", grid=(t // tm,),
        in_specs=[pair(1), pair(2), _full((2, D_MODEL)), _full((1, D_MODEL)),
                  pl.BlockSpec(memory_space=pl.ANY)],
        out_specs=[pl.BlockSpec((None, tm, D_MODEL), lambda i: (1, i, 0)),
                   pl.BlockSpec((tm, D_MODEL), lambda i: (i, 0)),
                   pl.BlockSpec((nc, HEADS, HEAD_D, HEAD_D), lambda i: (i, 0, 0, 0))],
        out_shape=[jax.ShapeDtypeStruct((2, t, D_MODEL), _MXU_DTYPE),
                   jax.ShapeDtypeStruct((t, D_MODEL), F32),
                   jax.ShapeDtypeStruct((t // CHUNK, HEADS, HEAD_D, HEAD_D), F32)],
        scratch_shapes=[pltpu.VMEM((HEADS, HEAD_D, HEAD_D), F32)],
        input_output_aliases={4: 0},
        compiler_params=_seq_params(),
    )(p, p, logits, gw, y2)


def _out_fused(y2, wo, x, tgt, wpost, tm):
    t = x.shape[0]
    inv_d = 1.0 / D_MODEL

    def body(y_ref, wo_ref, x_ref, t_ref, w_ref, dout_ref, dy_ref, sq_ref, dw_ref):
        i = pl.program_id(0)

        @pl.when(i == 0)
        def _():
            sq_ref[...] = jnp.zeros_like(sq_ref)
            dw_ref[...] = jnp.zeros_like(dw_ref)

        y = (jnp.dot(y_ref[0], wo_ref[0:D_MODEL, :], preferred_element_type=F32)
             + jnp.dot(y_ref[1], wo_ref[D_MODEL:2 * D_MODEL, :], preferred_element_type=F32))
        w = w_ref[...]
        rstd = lax.rsqrt(jnp.mean(y * y, axis=-1, keepdims=True) + EPS)
        n = y * rstd
        err = (x_ref[...] + n * w) - t_ref[...]
        sq_ref[...] += jnp.sum(err * err, axis=0, keepdims=True)
        dout = err * inv_d
        dout_ref[...] = dout
        dw_ref[...] += jnp.sum(dout * n, axis=0, keepdims=True)
        dn = dout * w
        dy_ref[...] = (rstd * (dn - n * jnp.mean(dn * n, axis=-1, keepdims=True))).astype(_MXU_DTYPE)

    row = pl.BlockSpec((tm, D_MODEL), lambda i: (i, 0))
    return pl.pallas_call(
        body, name="out_fused", grid=(t // tm,),
        in_specs=[pl.BlockSpec((2, tm, D_MODEL), lambda i: (0, i, 0)), _full((2 * D_MODEL, D_MODEL)),
                  row, row, _full((1, D_MODEL))],
        out_specs=[row, row, _full((1, D_MODEL)), _full((1, D_MODEL))],
        out_shape=[jax.ShapeDtypeStruct((t, D_MODEL), F32), jax.ShapeDtypeStruct((t, D_MODEL), _MXU_DTYPE),
                   jax.ShapeDtypeStruct((1, D_MODEL), F32), jax.ShapeDtypeStruct((1, D_MODEL), F32)],
        compiler_params=_seq_params(),
    )(y2, wo, x, tgt, wpost)


def _dw_out(y2, dy, tt):
    t = dy.shape[0]
    hr = UNIT // 2

    def body(y_ref, dy_ref, g_ref):
        @pl.when(pl.program_id(1) == 0)
        def _():
            g_ref[...] = jnp.zeros_like(g_ref)

        r = _dot_tn(y_ref[...], dy_ref[...])
        g_ref[0] += r[0:hr]
        g_ref[1] += r[hr:UNIT]

    return pl.pallas_call(
        body, name="dw_out", grid=(N_CHIPS, t // tt),
        in_specs=[pl.BlockSpec((None, tt, UNIT), lambda c, k: (c // 2, k, c % 2)),
                  pl.BlockSpec((tt, D_MODEL), lambda c, k: (k, 0))],
        out_specs=pl.BlockSpec((2, None, hr, D_MODEL), lambda c, k: (0, c, 0, 0)),
        out_shape=jax.ShapeDtypeStruct((2, N_CHIPS, hr, D_MODEL), F32),
        compiler_params=pltpu.CompilerParams(dimension_semantics=("arbitrary", "arbitrary")),
    )(y2, dy)


def _lru_bwd(dy, wo, p, h, wa, wx, ba, bx, lam, cw, cb, tm):
    t = dy.shape[0]
    nt = t // tm
    ng = tm // 8

    def body(dy_ref, wo_ref, lx_ref, halo_ref, lg_ref, h_ref, hhalo_ref, wa_ref, wx_ref, ba_ref, bx_ref,
             lam_ref, cw_ref, cb_ref,
             dp_ref, dwa_ref, dwx_ref, dba_ref, dbx_ref, dsp_ref, dcw_ref, dcb_ref,
             c_s, in_s, dh_s, dhc_s, afirst_s, dxc_s):
        i = pl.program_id(0)
        first = i == nt - 1

        @pl.when(i == 0)
        def _():
            for ref in (dwa_ref, dwx_ref, dba_ref, dbx_ref, dsp_ref, dcw_ref, dcb_ref, dhc_s, afirst_s, dxc_s):
                ref[...] = jnp.zeros_like(ref)

        sp = _softplus_neg(lam_ref[...])
        cwv = cw_ref[...]
        wav = wa_ref[...]
        wxv = wx_ref[...]
        halo = jnp.where(first, 0.0, _seg(halo_ref, 0))
        xs, xc, r, ig, a, om = _lru_gates(_seg(lx_ref, 0), halo, cwv, cb_ref[...], wav, wxv,
                                          ba_ref[...], bx_ref[...], sp)
        inv_mult = lax.rsqrt(om)
        mult = om * inv_mult
        d_y = _dot_nt(dy_ref[...], wo_ref[...])
        lg = _seg(lg_ref, 0)
        sl = _sigmoid(lg)
        hv = h_ref[...]
        d_lg = d_y * hv * (sl * (1.0 + lg * (1.0 - sl)))

        c_s[...] = _shift_up(a, afirst_s[...], 1)
        in_s[...] = d_y * (lg * sl)
        row8 = _rows((8, D_MODEL))

        def group(gg, carry):
            rows = pl.ds(pl.multiple_of((ng - 1 - gg) * 8, 8), 8)
            cv = c_s[rows, :]
            uv = in_s[rows, :]
            for s in (1, 2, 4):
                c_sh = jnp.where(row8 < 8 - s, pltpu.roll(cv, 8 - s, 0), 1.0)
                u_sh = jnp.where(row8 < 8 - s, pltpu.roll(uv, 8 - s, 0), 0.0)
                uv = cv * u_sh + uv
                cv = cv * c_sh
            hh = cv * carry + uv
            dh_s[rows, :] = hh
            return jnp.broadcast_to(hh[0:1, :], (8, D_MODEL))

        dhc_s[...] = lax.fori_loop(0, ng, group, dhc_s[...])
        afirst_s[...] = a[0:8]
        dh = dh_s[...]

        h_prev = _shift_down(hv, jnp.where(first, 0.0, hhalo_ref[...]), 1)
        gx = ig * xc
        d_mult = dh * gx
        d_i = dh * (mult * xc)
        d_xc = dh * (mult * ig)
        d_la = (dh * h_prev) * a - d_mult * ((a * a) * inv_mult)
        dsp_ref[...] += jnp.sum(d_la * r, axis=0, keepdims=True) * (-LRU_C)
        d_zr = (d_la * (-LRU_C * sp)) * (r * (1.0 - r))
        d_zi = d_i * (ig * (1.0 - ig))
        dba_ref[...] += jnp.sum(d_zr, axis=0, keepdims=True)
        dbx_ref[...] += jnp.sum(d_zi, axis=0, keepdims=True)
        back = []
        for n in range(LRU_BLOCKS):
            ln = slice(n * LRU_BW, (n + 1) * LRU_BW)
            xb = _mx(xc[:, ln])
            zr_n = _mx(d_zr[:, ln])
            zi_n = _mx(d_zi[:, ln])
            dwa_ref[n] += _dot_tn(xb, zr_n)
            dwx_ref[n] += _dot_tn(xb, zi_n)
            back.append(_dot_nt(zr_n, wav[n]) + _dot_nt(zi_n, wxv[n]))
        d_xc = d_xc + jnp.concatenate(back, axis=1)
        dcb_ref[...] += jnp.sum(d_xc, axis=0, keepdims=True)
        for k in range(4):
            dcw_ref[k:k + 1, :] += jnp.sum(d_xc * xs[k], axis=0, keepdims=True)
        head = dxc_s[...]
        d_lx = cwv[3:4] * d_xc
        for k in range(3):
            d_lx = d_lx + cwv[k:k + 1] * _shift_up(d_xc, head, 3 - k)
        dxc_s[...] = d_xc[0:8]
        dp_ref[0] = d_lx.astype(_MXU_DTYPE)
        dp_ref[1] = d_lg.astype(_MXU_DTYPE)

    rev = lambda i: nt - 1 - i
    seg = lambda s: pl.BlockSpec((2, tm, UNIT), lambda i: (s, rev(i), 0))
    halo_row = lambda i: jnp.maximum(rev(i) * (tm // 8) - 1, 0)
    row = pl.BlockSpec((tm, D_MODEL), lambda i: (rev(i), 0))
    vec = _full((1, D_MODEL))
    wblk = _full((LRU_BLOCKS, LRU_BW, LRU_BW))
    return pl.pallas_call(
        body, name="lru_bwd", grid=(nt,),
        in_specs=[row, pl.BlockSpec((D_MODEL, D_MODEL), lambda i: (0, 0)),
                  seg(0), pl.BlockSpec((2, 8, UNIT), lambda i: (0, halo_row(i), 0)), seg(1),
                  row, pl.BlockSpec((8, D_MODEL), lambda i: (halo_row(i), 0)),
                  wblk, wblk, vec, vec, vec, _full((4, D_MODEL)), vec],
        out_specs=[pl.BlockSpec((2, tm, D_MODEL), lambda i: (2, rev(i), 0)),
                   wblk, wblk, vec, vec, vec, _full((4, D_MODEL)), vec],
        out_shape=[jax.ShapeDtypeStruct((N_SEG, t, D_MODEL), _MXU_DTYPE),
                   jax.ShapeDtypeStruct((LRU_BLOCKS, LRU_BW, LRU_BW), F32),
                   jax.ShapeDtypeStruct((LRU_BLOCKS, LRU_BW, LRU_BW), F32),
                   jax.ShapeDtypeStruct((1, D_MODEL), F32), jax.ShapeDtypeStruct((1, D_MODEL), F32),
                   jax.ShapeDtypeStruct((1, D_MODEL), F32), jax.ShapeDtypeStruct((4, D_MODEL), F32),
                   jax.ShapeDtypeStruct((1, D_MODEL), F32)],
        scratch_shapes=[pltpu.VMEM((tm, D_MODEL), F32), pltpu.VMEM((tm, D_MODEL), F32),
                        pltpu.VMEM((tm, D_MODEL), F32), pltpu.VMEM((8, D_MODEL), F32),
                        pltpu.VMEM((8, D_MODEL), F32), pltpu.VMEM((8, D_MODEL), F32)],
        compiler_params=_seq_params(),
    )(dy, wo, p, p, p, h, h, wa, wx, ba, bx, lam, cw, cb)


def _hgrn_bwd(dy, wo, p, o, states, logits, gw, dp, tm):
    t = dy.shape[0]
    nt = t // tm
    nc = tm // CHUNK

    def body(dy_ref, wo_ref, qf_ref, vh_ref, o_ref, st_ref, lg_ref, gw_ref, dp_in,
             dp_ref, dgw_ref, dlb_ref, dyh_s, dstate, snext):
        del dp_in
        i = pl.program_id(0)

        @pl.when(i == 0)
        def _():
            for ref in (dgw_ref, dlb_ref, dstate, snext):
                ref[...] = jnp.zeros_like(ref)

        dyh_s[...] = _dot_nt(dy_ref[...], wo_ref[...])
        l0 = lg_ref[0:1, :]
        l1 = lg_ref[1:2, :]
        lb = 1.0 / (1.0 + jnp.exp(l1 - l0))
        gwv = gw_ref[...]
        causal = _rows((CHUNK, CHUNK)) >= _cols((CHUNK, CHUNK))
        wide = 2 * HEAD_D
        zeros_h = jnp.zeros((CHUNK, HEAD_D), _MXU_DTYPE)

        for c in reversed(range(nc)):
            rows = pl.ds(c * CHUNK, CHUNK)
            q = _seg(qf_ref, 0, rows)
            hg = _seg(vh_ref, 1, rows)
            ov = o_ref[rows, :]
            dyh = dyh_s[rows, :]
            sig, f, sq, em1, em2, qm, km, ebm, ecm, ebc = _hgrn_prep(q, _seg(qf_ref, 1, rows), lb)
            ke = _mx(km.astype(F32) * ecm)
            vb = _mx(_seg(vh_ref, 0, rows))
            sg = _sigmoid(hg)
            d_on = dyh * (hg * sg)
            x_p, y_p, dv_p, n_p, ex_p = [], [], [], [], []
            for h in range(HEADS):
                ln = slice(h * HEAD_D, (h + 1) * HEAD_D)
                o_h = ov[:, ln]
                rstd = lax.rsqrt(jnp.mean(o_h * o_h, axis=-1, keepdims=True) + EPS)
                n = o_h * rstd
                dn = d_on[:, ln] * gwv[:, ln]
                d_o = rstd * (dn - n * jnp.mean(dn * n, axis=-1, keepdims=True))
                d_ob = _mx(d_o)
                d_ot = _mx(d_o.T)
                s0 = st_ref[c, h]
                ds = dstate[h]
                both = _dot_nt(jnp.concatenate([_pad_lanes([qm[:, ln]], wide), _pad_lanes([zeros_h, d_ob], wide)], axis=0),
                               jnp.concatenate([km[:, ln], vb[:, ln]], axis=1))
                att = jnp.where(causal, both[0:CHUNK], 0.0)
                d_att = jnp.where(causal, both[CHUNK:2 * CHUNK], 0.0)
                x_p.append(jnp.dot(_pad_lanes([d_ob, _mx(d_att)], wide),
                                   _pad_rows([_mx(s0 * ebm[:, ln]), km[:, ln]], wide), preferred_element_type=F32))
                y_p.append(jnp.dot(_pad_lanes([vb[:, ln], _mx(d_att.T)], wide),
                                   _pad_rows([_mx(ds * ecm[:, ln]), qm[:, ln]], wide), preferred_element_type=F32))
                dv_p.append(_dot_nt(_pad_lanes([ke[:, ln], _mx(att.T)], wide), _pad_lanes([_mx(ds), d_ot], wide)))
                ex_p.append(jnp.sum(ds * snext[h], axis=0, keepdims=True))
                dstate[h] = ds * ebc[:, ln] + jnp.dot(d_ot, qm[:, ln], preferred_element_type=F32) * ebm[:, ln]
                snext[h] = s0
                n_p.append(n)
            cat = lambda parts: jnp.concatenate(parts, axis=1)
            x_all, y_all, n_all = cat(x_p), cat(y_p), cat(n_p)
            dg = _cumsum_rows(qm.astype(F32) * x_all - km.astype(F32) * y_all, reverse=True) + cat(ex_p)
            df = dg / f - em2 * y_all
            dgw_ref[...] += jnp.sum(d_on * n_all, axis=0, keepdims=True)
            dlb_ref[...] += jnp.sum(df * (1.0 - sig), axis=0, keepdims=True)
            dp_ref[0, rows, :] = ((em1 * x_all) * (sq * (1.0 + q * (1.0 - sq)))).astype(_MXU_DTYPE)
            dp_ref[1, rows, :] = (df * ((1.0 - lb) * sig * (1.0 - sig))).astype(_MXU_DTYPE)
            dp_ref[2, rows, :] = cat(dv_p).astype(_MXU_DTYPE)
            dp_ref[3, rows, :] = (dyh * (n_all * gwv) * (sg * (1.0 + hg * (1.0 - sg)))).astype(_MXU_DTYPE)

    rev = lambda i: nt - 1 - i
    pair = lambda s: pl.BlockSpec((4, tm, UNIT), lambda i: (s, rev(i), 0))
    row = pl.BlockSpec((tm, D_MODEL), lambda i: (rev(i), 0))
    vec = _full((1, D_MODEL))
    return pl.pallas_call(
        body, name="hgrn_bwd", grid=(nt,),
        in_specs=[row, pl.BlockSpec((D_MODEL, D_MODEL), lambda i: (1, 0)), pair(1), pair(2), row,
                  pl.BlockSpec((nc, HEADS, HEAD_D, HEAD_D), lambda i: (rev(i), 0, 0, 0)),
                  _full((2, D_MODEL)), vec, pl.BlockSpec(memory_space=pl.ANY)],
        out_specs=[pl.BlockSpec((4, tm, D_MODEL), lambda i: (0, rev(i), 0)), vec, vec],
        out_shape=[jax.ShapeDtypeStruct((N_SEG, t, D_MODEL), _MXU_DTYPE),
                   jax.ShapeDtypeStruct((1, D_MODEL), F32), jax.ShapeDtypeStruct((1, D_MODEL), F32)],
        scratch_shapes=[pltpu.VMEM((tm, D_MODEL), F32), pltpu.VMEM((HEADS, HEAD_D, HEAD_D), F32),
                        pltpu.VMEM((HEADS, HEAD_D, HEAD_D), F32)],
        input_output_aliases={8: 0},
        compiler_params=_seq_params(),
    )(dy, wo, p, p, o, states, logits, gw, dp)


def _dp_unit(m):
    return ((m // 2 + 2) % N_SEG) * 2 + m % 2


def _dw_in(ut, dp, tt, early):
    t = ut.shape[1]
    hr = D_MODEL // 2
    n = len(early)
    nk = t // tt
    swap_shapes, n_sems = _core_swap_plan(early, True)

    def body(ut_ref, dp_ref, *rest):
        ins, g_ref, outs = rest[:n], rest[n], rest[n + 1:2 * n + 1]
        send_sems, recv_sems = rest[2 * n + 1:]
        s, k = pl.program_id(0), pl.program_id(1)

        @pl.when(jnp.logical_and(s == 0, k == 0))
        def _():
            for cp in _core_swap_copies(early, True, ins, outs, send_sems, recv_sems):
                cp.start()

        @pl.when(k == 0)
        def _():
            g_ref[...] = jnp.zeros_like(g_ref)

        r = jnp.dot(ut_ref[...], dp_ref[...], preferred_element_type=F32)
        for h in range(2):
            for half in range(2):
                g_ref[h, half] += r[h * hr:(h + 1) * hr, half * UNIT:(half + 1) * UNIT]

        @pl.when(jnp.logical_and(s == N_SEG - 1, k == nk - 1))
        def _():
            copies = _core_swap_copies(early, True, ins, outs, send_sems, recv_sems)
            for cp in copies:
                cp.wait_recv()
            for cp in copies:
                cp.wait_send()

    any_spec = pl.BlockSpec(memory_space=pl.ANY)
    res = pl.pallas_call(
        body, name="dw_in", grid=(N_SEG, nk),
        in_specs=[pl.BlockSpec((D_MODEL, tt), lambda s, k: (0, k)),
                  pl.BlockSpec((None, tt, D_MODEL), lambda s, k: (s, k, 0))] + [any_spec] * n,
        out_specs=[pl.BlockSpec((2, 2, hr, UNIT), lambda s, k: (0, (s + 2) % N_SEG, 0, 0))] + [any_spec] * n,
        out_shape=[jax.ShapeDtypeStruct((2, N_UNIT, hr, UNIT), F32)] + swap_shapes,
        scratch_shapes=[pltpu.SemaphoreType.DMA((n_sems,)), pltpu.SemaphoreType.DMA((n_sems,))],
        compiler_params=pltpu.CompilerParams(dimension_semantics=("arbitrary", "arbitrary")),
    )(ut, dp, *early)
    return res[0], res[1:]


def _dx(dp, wg, x, dout, wpre, tm, parts, scatter):
    t = x.shape[0]
    n = len(parts)
    nt = t // tm

    def body(*refs):
        dp_ref, w_ref, x_ref, dout_ref, wpre_ref = refs[:5]
        ins = refs[5:5 + n]
        gx_ref, dw_ref = refs[5 + n:7 + n]
        outs = refs[7 + n:7 + 2 * n]
        send_sems, recv_sems = refs[7 + 2 * n:]
        i = pl.program_id(0)

        @pl.when(i == 0)
        def _():
            dw_ref[...] = jnp.zeros_like(dw_ref)
            for cp in _chip_copies(ins, outs, send_sems, recv_sems, scatter):
                cp.start()

        du = None
        for m in range(N_UNIT):
            part = _dot_nt(dp_ref[m // 2, :, (m % 2) * UNIT:(m % 2 + 1) * UNIT], w_ref[_dp_unit(m)])
            du = part if du is None else du + part
        xv = x_ref[...]
        rstd = lax.rsqrt(jnp.mean(xv * xv, axis=-1, keepdims=True) + EPS)
        xn = xv * rstd
        dw_ref[...] += jnp.sum(du * xn, axis=0, keepdims=True)
        dn = du * wpre_ref[...]
        gx_ref[...] = dout_ref[...] + rstd * (dn - xn * jnp.mean(dn * xn, axis=-1, keepdims=True))

        @pl.when(i == nt - 1)
        def _():
            copies = _chip_copies(ins, outs, send_sems, recv_sems, scatter)
            for cp in copies:
                cp.wait_recv()
            for cp in copies:
                cp.wait_send()

    row = pl.BlockSpec((tm, D_MODEL), lambda i: (i, 0))
    any_spec = pl.BlockSpec(memory_space=pl.ANY)
    res = pl.pallas_call(
        body, name="dx", grid=(nt,),
        in_specs=[pl.BlockSpec((N_SEG, tm, D_MODEL), lambda i: (0, i, 0)), _resident((N_UNIT, D_MODEL, UNIT)),
                  row, row, _full((1, D_MODEL))] + [any_spec] * n,
        out_specs=[row, _full((1, D_MODEL))] + [any_spec] * n,
        out_shape=[jax.ShapeDtypeStruct((t, D_MODEL), F32), jax.ShapeDtypeStruct((1, D_MODEL), F32)]
        + _chip_exchange_shapes(parts, scatter),
        scratch_shapes=[pltpu.SemaphoreType.DMA((max(3 * n, 1),)), pltpu.SemaphoreType.DMA((max(3 * n, 1),))],
        compiler_params=_seq_params(),
    )(dp, wg, x, dout, wpre, *parts)
    return res[0], res[1], res[2:]


CHUNK_BYTES = 256 * 1024
MAX_CHUNKS = 32


def _n_chunks(shape, dtype, rows):
    nbytes = jnp.dtype(dtype).itemsize
    for d in shape:
        nbytes *= d
    k = max(1, min(MAX_CHUNKS, rows, nbytes // CHUNK_BYTES))
    while rows % k:
        k -= 1
    return k


def _chip_exchange_shapes(arrays, scatter):
    return [jax.ShapeDtypeStruct((3,) + tuple(a.shape[1:] if sc else a.shape), a.dtype)
            for a, sc in zip(arrays, scatter)]


def _chip_copies(ins, outs, send_sems, recv_sems, scatter):
    if not ins:
        return []
    x, y, c = lax.axis_index("x"), lax.axis_index("y"), lax.axis_index("c")
    peers = [(1 - x, y), (x, 1 - y), (1 - x, 1 - y)]
    copies = []
    for a in range(len(ins)):
        for j, (px, py) in enumerate(peers):
            copies.append(pltpu.make_async_remote_copy(
                src_ref=ins[a].at[2 * px + py] if scatter[a] else ins[a], dst_ref=outs[a].at[j],
                send_sem=send_sems.at[a * 3 + j], recv_sem=recv_sems.at[a * 3 + j],
                device_id=(px, py, c), device_id_type=MESH))
    return copies


def _core_swap(arrays, name, halved):
    n = len(arrays)
    shapes, n_sems = _core_swap_plan(arrays, halved)

    def body(*refs):
        copies = _core_swap_copies(arrays, halved, refs[:n], refs[n:2 * n], refs[2 * n], refs[2 * n + 1])
        for cp in copies:
            cp.start()
        for cp in copies:
            cp.wait_recv()
        for cp in copies:
            cp.wait_send()

    any_spec = pl.BlockSpec(memory_space=pl.ANY)
    return pl.pallas_call(
        body, name=name,
        in_specs=[any_spec] * n, out_specs=[any_spec] * n,
        out_shape=shapes,
        scratch_shapes=[pltpu.SemaphoreType.DMA((n_sems,)), pltpu.SemaphoreType.DMA((n_sems,))],
    )(*arrays)


def _core_swap_plan(arrays, halved):
    shapes = [a.shape[1:] if halved else a.shape for a in arrays]
    n_sems = sum(_n_chunks(s, a.dtype, s[0]) for s, a in zip(shapes, arrays))
    return [jax.ShapeDtypeStruct(s, a.dtype) for s, a in zip(shapes, arrays)], max(n_sems, 1)


def _core_swap_copies(arrays, halved, ins, outs, send_sems, recv_sems):
    if not arrays:
        return []
    x, y, c = lax.axis_index("x"), lax.axis_index("y"), lax.axis_index("c")
    copies, sem = [], 0
    for a, arr in enumerate(arrays):
        shape = arr.shape[1:] if halved else arr.shape
        k = _n_chunks(shape, arr.dtype, shape[0])
        step = shape[0] // k
        for j in range(k):
            rows = pl.ds(j * step, step)
            copies.append(pltpu.make_async_remote_copy(
                src_ref=ins[a].at[1 - c, rows] if halved else ins[a].at[rows], dst_ref=outs[a].at[rows],
                send_sem=send_sems.at[sem], recv_sem=recv_sems.at[sem],
                device_id=(x, y, 1 - c), device_id_type=MESH))
            sem += 1
    return copies


def _finish_exchange(arrays, small):
    n = len(arrays)
    ks = [_n_chunks(a.shape, a.dtype, a.shape[0]) for a in arrays]
    offs = [sum(ks[:a]) for a in range(n)]
    total = sum(ks)

    def body(*refs):
        ins, small_ref = refs[:n], refs[n]
        outs, gathered = refs[n + 1:2 * n + 1], refs[2 * n + 1]
        send_sems, recv_sems = refs[2 * n + 2:]
        x, y, c = lax.axis_index("x"), lax.axis_index("y"), lax.axis_index("c")
        copies = []
        for a in range(n):
            step = arrays[a].shape[0] // ks[a]
            for j in range(ks[a]):
                rows = pl.ds(j * step, step)
                copies.append(pltpu.make_async_remote_copy(
                    src_ref=ins[a].at[rows], dst_ref=outs[a].at[rows],
                    send_sem=send_sems.at[offs[a] + j], recv_sem=recv_sems.at[offs[a] + j],
                    device_id=(x, y, 1 - c), device_id_type=MESH))
        for slot in range(7):
            dx, dy, dc = (slot + 1) % 2, ((slot + 1) // 2) % 2, (slot + 1) // 4
            copies.append(pltpu.make_async_remote_copy(
                src_ref=small_ref, dst_ref=gathered.at[slot],
                send_sem=send_sems.at[total + slot], recv_sem=recv_sems.at[total + slot],
                device_id=(x + dx - 2 * x * dx, y + dy - 2 * y * dy, c + dc - 2 * c * dc), device_id_type=MESH))
        for cp in copies:
            cp.start()
        for cp in copies:
            cp.wait_recv()
        for cp in copies:
            cp.wait_send()

    any_spec = pl.BlockSpec(memory_space=pl.ANY)
    res = pl.pallas_call(
        body, name="reduce_cores",
        in_specs=[any_spec] * (n + 1), out_specs=[any_spec] * (n + 1),
        out_shape=[jax.ShapeDtypeStruct(a.shape, a.dtype) for a in arrays]
        + [jax.ShapeDtypeStruct((7,) + small.shape, small.dtype)],
        scratch_shapes=[pltpu.SemaphoreType.DMA((total + 7,)), pltpu.SemaphoreType.DMA((total + 7,))],
    )(*arrays, small)
    return res[:n], res[n]


def _sum_own_half(g, b, name, out_dtype, tr=512):
    _, rows, cols = g.shape
    tr = min(tr, rows)
    core = lax.axis_index("c").reshape(1)

    def body(c_ref, g_ref, b_ref, o_ref):
        del c_ref
        o_ref[...] = (g_ref[...] + b_ref[...]).astype(out_dtype)

    return pl.pallas_call(
        body, name=name,
        grid_spec=pltpu.PrefetchScalarGridSpec(
            num_scalar_prefetch=1, grid=(rows // tr,),
            in_specs=[pl.BlockSpec((None, tr, cols), lambda i, c_ref: (c_ref[0], i, 0)),
                      pl.BlockSpec((tr, cols), lambda i, c_ref: (i, 0))],
            out_specs=pl.BlockSpec((tr, cols), lambda i, c_ref: (i, 0))),
        out_shape=jax.ShapeDtypeStruct((rows, cols), out_dtype),
        compiler_params=pltpu.CompilerParams(dimension_semantics=("parallel",)),
    )(core, g, b)


def _sum_chips(own, r, slabbed, name, tr=512):
    _, rows, cols = r.shape
    tr = min(tr, rows)
    chip = (2 * lax.axis_index("x") + lax.axis_index("y")).reshape(1)

    def body(c_ref, own_ref, r_ref, o_ref):
        del c_ref
        f = lambda val: val.astype(F32)
        o_ref[...] = (f(own_ref[...]) + f(r_ref[0])) + (f(r_ref[1]) + f(r_ref[2]))

    if slabbed:
        own_spec = pl.BlockSpec((None, tr, cols), lambda i, c_ref: (c_ref[0], i, 0))
    else:
        own_spec = pl.BlockSpec((tr, cols), lambda i, c_ref: (i, 0))
    return pl.pallas_call(
        body, name=name,
        grid_spec=pltpu.PrefetchScalarGridSpec(
            num_scalar_prefetch=1, grid=(rows // tr,),
            in_specs=[own_spec, pl.BlockSpec((3, tr, cols), lambda i, c_ref: (0, i, 0))],
            out_specs=pl.BlockSpec((tr, cols), lambda i, c_ref: (i, 0))),
        out_shape=jax.ShapeDtypeStruct((rows, cols), F32),
        compiler_params=pltpu.CompilerParams(dimension_semantics=("parallel",)),
    )(chip, own, r)


def _adamw_big(own, other, w, m, v, name, grid, g_block, g_index, w_block, w_index):
    core = lax.axis_index("c").reshape(1)
    half_axis = len(grid) - 1

    def body(c_ref, own_ref, oth_ref, w_ref, m_ref, v_ref, g_ref, d_ref, nm_ref, nv_ref):
        mine = pl.program_id(half_axis) == c_ref[0]
        gv = jnp.where(mine, own_ref[...], oth_ref[...])
        g_ref[...] = gv
        d_ref[...], nm_ref[...], nv_ref[...] = _adamw(w_ref[...], gv, m_ref[...], v_ref[...])

    g_spec = pl.BlockSpec(g_block, lambda *a: g_index(*a[:half_axis]))
    w_spec = pl.BlockSpec(w_block, lambda *a: w_index(*a[:-1]))
    shp = jax.ShapeDtypeStruct(w.shape, F32)
    return pl.pallas_call(
        body, name=name,
        grid_spec=pltpu.PrefetchScalarGridSpec(
            num_scalar_prefetch=1, grid=grid,
            in_specs=[g_spec, g_spec, w_spec, w_spec, w_spec], out_specs=[w_spec] * 4),
        out_shape=[shp] * 4,
        compiler_params=pltpu.CompilerParams(dimension_semantics=("parallel",) * len(grid)),
    )(core, own, other, w, m, v)


def _small_finish(own, others, lam, logits):
    def body(a_ref, b_ref, lam_ref, lg_ref, o_ref):
        g = (((a_ref[...] + b_ref[0]) + (b_ref[1] + b_ref[2]))
             + ((b_ref[3] + b_ref[4]) + (b_ref[5] + b_ref[6])))
        o_ref[...] = g
        o_ref[8:9, :] = g[8:9, :] * (-_sigmoid(-lam_ref[...]))
        lb = 1.0 / (1.0 + jnp.exp(lg_ref[1:2, :] - lg_ref[0:1, :]))
        d0 = g[9:10, :] * (lb * (1.0 - lb))
        o_ref[9:10, :] = d0
        o_ref[10:11, :] = -d0

    return pl.pallas_call(
        body, name="small_finish",
        in_specs=[_full((SMALL_ROWS, D_MODEL)), _full((7, SMALL_ROWS, D_MODEL)), _full((1, D_MODEL)),
                  _full((2, D_MODEL))],
        out_specs=_full((SMALL_ROWS, D_MODEL)),
        out_shape=jax.ShapeDtypeStruct((SMALL_ROWS, D_MODEL), F32),
    )(own, others, lam, logits)


def _adamw_small(ws, gs, ms, vs):
    n = len(ws)

    def body(*refs):
        w_r, g_r, m_r, v_r = refs[:n], refs[n:2 * n], refs[2 * n:3 * n], refs[3 * n:4 * n]
        d_o, m_o, v_o = refs[4 * n:5 * n], refs[5 * n:6 * n], refs[6 * n:7 * n]
        for j in range(n):
            d_o[j][...], m_o[j][...], v_o[j][...] = _adamw(w_r[j][...], g_r[j][...], m_r[j][...], v_r[j][...])

    specs = [_full(w.shape) for w in ws]
    shapes = [jax.ShapeDtypeStruct(w.shape, F32) for w in ws]
    outs = pl.pallas_call(
        body, name="adamw_small",
        in_specs=specs * 4, out_specs=specs * 3, out_shape=shapes * 3,
    )(*ws, *gs, *ms, *vs)
    return outs[:n], outs[n:2 * n], outs[2 * n:]


def _local_step(x, tgt, ut, p, pre_w, wg, cw, cb, wa, wx, ba, bx, lam, logits, gw, wo, post_w, tm, tmm,
                prepare=None):
    h, y2 = _lru_fwd(p, wa, wx, ba, bx, lam, cw, cb, tm)
    y2, o, states = _hgrn_fwd(p, logits, gw, y2, tm)
    dout, dy, sq, d_post = _out_fused(y2, wo, x, tgt, post_w, tmm)
    g_out = _dw_out(y2, dy, min(2048, x.shape[0]))
    dp, d_wa, d_wx, d_ba, d_bx, d_sp, d_cw, d_cb = _lru_bwd(dy, wo, p, h, wa, wx, ba, bx, lam, cw, cb, tm)
    dp, d_gw, d_lb = _hgrn_bwd(dy, wo, p, o, states, logits, gw, dp, tm)
    early = prepare[0](g_out, d_wa, d_wx) if prepare is not None else []
    g_in, swapped = _dw_in(ut, dp, min(2048, x.shape[0]), early)
    parts, scatter = prepare[1](g_in, early, swapped) if prepare is not None else ([], [])
    grad_x, d_pre, received = _dx(dp, wg, x, dout, pre_w, tm, parts, scatter)
    small = jnp.concatenate([d_pre, d_cw, d_cb, d_ba, d_bx, d_sp, d_lb,
                             jnp.zeros((1, D_MODEL), F32), d_gw, d_post, sq,
                             jnp.zeros((SMALL_ROWS - 14, D_MODEL), F32)], axis=0)
    return grad_x, g_in, g_out, d_wa, d_wx, small, parts, received


def kernel(x, pre_norm_w, w_in, conv_w, conv_b, lru_w_a, lru_b_a, lru_w_x, lru_b_x, lru_lambda, hgrn_lb_logits, hgrn_gnorm_w, w_out, post_norm_w, loss_target, m_pre_norm_w, m_w_in, m_conv_w, m_conv_b, m_lru_w_a, m_lru_b_a, m_lru_w_x, m_lru_b_x, m_lru_lambda, m_hgrn_lb_logits, m_hgrn_gnorm_w, m_w_out, m_post_norm_w, v_pre_norm_w, v_w_in, v_conv_w, v_conv_b, v_lru_w_a, v_lru_b_a, v_lru_w_x, v_lru_b_x, v_lru_lambda, v_hgrn_lb_logits, v_hgrn_gnorm_w, v_w_out, v_post_norm_w):
    t = x.shape[1]
    tm = min(256, t)
    tmm = min(512, t)
    chip = 2 * lax.axis_index("x") + lax.axis_index("y")

    win_units = w_in[0].astype(_MXU_DTYPE).reshape(D_MODEL, 3, UNIT).transpose(1, 0, 2)
    small_w = jnp.concatenate([conv_w[0], lru_b_a[0], lru_b_x[0]], axis=1)
    ut, p, wg4, (wo4, wa4, wx4), sw4 = _in_proj_gather(
        x[0], pre_norm_w, win_units,
        [w_out[0].astype(_MXU_DTYPE).reshape(2, 256, D_MODEL), lru_w_a[0].astype(_MXU_DTYPE).reshape(2, 128, LRU_BW),
         lru_w_x[0].astype(_MXU_DTYPE).reshape(2, 128, LRU_BW)],
        small_w, min(512, t))
    wg = wg4.reshape(N_UNIT, D_MODEL, UNIT)
    wo = wo4.reshape(2 * D_MODEL, D_MODEL)
    by_block = lambda w4: w4.reshape(N_CHIPS, LRU_BLOCKS, 64, LRU_BW).transpose(1, 0, 2, 3).reshape(
        LRU_BLOCKS, LRU_BW, LRU_BW)
    wa, wx = by_block(wa4), by_block(wx4)
    sw4 = sw4.reshape(N_CHIPS, 4, 384)
    cw = sw4[:, :, 0:256].transpose(1, 0, 2).reshape(4, D_MODEL)
    ba = sw4[:, :, 256:320].transpose(1, 0, 2).reshape(1, D_MODEL)
    bx = sw4[:, :, 320:384].transpose(1, 0, 2).reshape(1, D_MODEL)

    def lru_layout(g):
        g = g.reshape(2, 2, N_CHIPS, 64, LRU_BW).transpose(0, 2, 1, 3, 4)
        return g.reshape(2, N_CHIPS * 128, LRU_BW)

    def early(g_out, d_wa, d_wx):
        return [g_out.reshape(2, N_CHIPS * 256, D_MODEL), lru_layout(d_wa), lru_layout(d_wx)]

    def prepare(g_in, early_parts, swapped):
        g_in = g_in.reshape(2, N_UNIT * 512, UNIT)
        g_out, g_wa, g_wx = early_parts
        b_out, b_wa, b_wx = swapped
        (b_in,) = _core_swap([g_in], "reduce_swap", halved=True)
        parts = [_sum_own_half(g_in, b_in, "presum_in", _WIRE_DTYPE).reshape(N_CHIPS, 3 * 512, UNIT),
                 _sum_own_half(g_out, b_out, "presum_out", _WIRE_DTYPE).reshape(N_CHIPS, 256, D_MODEL),
                 _sum_own_half(g_wa, b_wa, "presum_wa", _WIRE_DTYPE).reshape(N_CHIPS, 128, LRU_BW),
                 _sum_own_half(g_wx, b_wx, "presum_wx", _WIRE_DTYPE).reshape(N_CHIPS, 128, LRU_BW)]
        return parts, [True] * 4

    grad_x, _, _, _, _, small, (p_in, p_out, p_wa, p_wx), (r_in, r_out, r_wa, r_wx) = _local_step(
        x[0], loss_target[0], ut, p, pre_norm_w, wg, cw, conv_b, wa, wx, ba, bx, lru_lambda, hgrn_lb_logits,
        hgrn_gnorm_w, wo, post_norm_w, tm, tmm, (early, prepare))
    s_in = _sum_chips(p_in, r_in, True, "sum_in")
    s_out = _sum_chips(p_out, r_out, True, "sum_out")
    s_wa = _sum_chips(p_wa, r_wa, True, "sum_wa")
    s_wx = _sum_chips(p_wx, r_wx, True, "sum_wx")
    (o_in, o_out, o_wa, o_wx), small_others = _finish_exchange([s_in, s_out, s_wa, s_wx], small)

    g_w_in, d_w_in, nm_w_in, nv_w_in = _adamw_big(
        s_in.reshape(3, 512, UNIT), o_in.reshape(3, 512, UNIT), w_in[0], m_w_in[0], v_w_in[0], "adamw_w_in", (3, 2),
        (None, 512, UNIT), lambda k: (k, 0, 0), (512, UNIT), lambda k, h: (h, k))
    g_w_out, d_w_out, nm_w_out, nv_w_out = _adamw_big(
        s_out, o_out, w_out[0], m_w_out[0], v_w_out[0], "adamw_w_out", (2,),
        (256, D_MODEL), lambda: (0, 0), (256, D_MODEL), lambda h: (h, 0))
    sq2 = lambda a: a.reshape(LRU_BW, LRU_BW)
    lru_specs = ((2,), (128, LRU_BW), lambda: (0, 0), (128, LRU_BW), lambda h: (h, 0))
    g_wa, d_wa2, nm_wa, nv_wa = _adamw_big(s_wa, o_wa, sq2(lru_w_a), sq2(m_lru_w_a), sq2(v_lru_w_a), "adamw_wa",
                                           *lru_specs)
    g_wx, d_wx2, nm_wx, nv_wx = _adamw_big(s_wx, o_wx, sq2(lru_w_x), sq2(m_lru_w_x), sq2(v_lru_w_x), "adamw_wx",
                                           *lru_specs)

    gs = _small_finish(small, small_others, lru_lambda, hgrn_lb_logits)
    loss = jnp.sum(gs[13]) * (0.5 / D_MODEL)
    g_pre = gs[0:1]
    g_cw = lax.dynamic_slice(gs[1:5], (0, chip * 256), (4, 256))
    g_cb = gs[5:6]
    g_ba = lax.dynamic_slice(gs[6].reshape(LRU_BLOCKS, N_CHIPS, 64), (0, chip, 0), (LRU_BLOCKS, 1, 64)).reshape(4, 64)
    g_bx = lax.dynamic_slice(gs[7].reshape(LRU_BLOCKS, N_CHIPS, 64), (0, chip, 0), (LRU_BLOCKS, 1, 64)).reshape(4, 64)
    g_lam = gs[8:9]
    g_lb = gs[9:11]
    g_gw = gs[11:12]
    g_post = gs[12:13]
    small_g = [g_pre, g_cw, g_cb, g_ba, g_bx, g_lam, g_lb, g_gw, g_post]
    two_d = lambda a: a.reshape(a.shape[-2:])
    small_w_list = [pre_norm_w, conv_w, conv_b, lru_b_a, lru_b_x, lru_lambda, hgrn_lb_logits, hgrn_gnorm_w, post_norm_w]
    small_m_list = [m_pre_norm_w, m_conv_w, m_conv_b, m_lru_b_a, m_lru_b_x, m_lru_lambda, m_hgrn_lb_logits,
                    m_hgrn_gnorm_w, m_post_norm_w]
    small_v_list = [v_pre_norm_w, v_conv_w, v_conv_b, v_lru_b_a, v_lru_b_x, v_lru_lambda, v_hgrn_lb_logits,
                    v_hgrn_gnorm_w, v_post_norm_w]
    sd, sm, sv = _adamw_small([two_d(a) for a in small_w_list], small_g,
                              [two_d(a) for a in small_m_list], [two_d(a) for a in small_v_list])

    def shaped(vals, refs):
        return [val.reshape(ref.shape) for val, ref in zip(vals, refs)]

    s_g = shaped(small_g, small_w_list)
    s_d = shaped(sd, small_w_list)
    s_m = shaped(sm, small_w_list)
    s_v = shaped(sv, small_w_list)

    def ordered(small, big_in, big_wa, big_wx, big_out):
        pre, cw_, cb_, ba_, bx_, lam_, lb_, gw_, post_ = small
        return [pre, big_in.reshape(w_in.shape), cw_, cb_, big_wa.reshape(lru_w_a.shape), ba_,
                big_wx.reshape(lru_w_x.shape), bx_, lam_, lb_, gw_, big_out.reshape(w_out.shape), post_]

    grads = ordered(s_g, g_w_in, g_wa, g_wx, g_w_out)
    deltas = ordered(s_d, d_w_in, d_wa2, d_wx2, d_w_out)
    new_m = ordered(s_m, nm_w_in, nm_wa, nm_wx, nm_w_out)
    new_v = ordered(s_v, nv_w_in, nv_wa, nv_wx, nv_w_out)
    return (loss, grad_x.reshape(x.shape), *grads, *deltas, *new_m, *new_v)
```

```python
import functools

import jax
import jax.numpy as jnp
from jax import lax
from jax.experimental import pallas as pl
from jax.experimental.pallas import tpu as pltpu

F32 = jnp.float32
BF16 = jnp.bfloat16
_MXU_DTYPE = jnp.bfloat16
_WIRE_DTYPE = jnp.bfloat16

D_MODEL = 1024
N_SEG = 6
UNIT = 512
N_UNIT = 12
LRU_BLOCKS = 4
LRU_BW = 256
LRU_C = 8.0
HEADS = 8
HEAD_D = 128
CHUNK = 64
EPS = 1e-6
N_CHIPS = 4
EXP_CLAMP = 80.0

ADAM_LR = 0.001
ADAM_B1 = 0.9
ADAM_B2 = 0.999
ADAM_EPS = 1e-08
ADAM_WD = 0.01
ADAM_STEP = 10

SMALL_ROWS = 16
FWD_PIECES = 4
MESH = pl.DeviceIdType.MESH


def _sigmoid(x):
    return 0.5 * jnp.tanh(0.5 * x) + 0.5


def _mx(x):
    return x.astype(_MXU_DTYPE)


def _dot(a, b):
    return jnp.dot(_mx(a), _mx(b), preferred_element_type=F32)


def _dot_nt(a, b):
    return lax.dot_general(_mx(a), _mx(b), (((1,), (1,)), ((), ())), preferred_element_type=F32)


def _dot_tn(a, b):
    return lax.dot_general(_mx(a), _mx(b), (((0,), (0,)), ((), ())), preferred_element_type=F32)


def _rows(shape):
    return lax.broadcasted_iota(jnp.int32, shape, 0)


def _cols(shape):
    return lax.broadcasted_iota(jnp.int32, shape, 1)


def _softplus_neg(lam):
    z = -lam
    e = jnp.exp(-jnp.abs(z))
    series = e * (1.0 - e * (0.5 - e * (1.0 / 3.0 - 0.25 * e)))
    return jnp.maximum(z, 0.0) + jnp.where(e < 1e-2, series, jnp.log(1.0 + e))


def _one_minus_sq(a, y):
    series = -y * (1.0 + y * (0.5 + y * (1.0 / 6.0)))
    return jnp.where(y > -0.01, series, 1.0 - a * a)


def _cumsum_rows(x, reverse=False):
    n = x.shape[0] // 8
    row8 = _rows((8, x.shape[1]))
    out = [None] * n
    carry = None
    for g in (reversed(range(n)) if reverse else range(n)):
        blk = x[8 * g:8 * g + 8]
        for s in (1, 2, 4):
            if reverse:
                blk = blk + jnp.where(row8 < 8 - s, pltpu.roll(blk, 8 - s, 0), 0.0)
            else:
                blk = blk + jnp.where(row8 >= s, pltpu.roll(blk, s, 0), 0.0)
        if carry is not None:
            blk = blk + carry
        carry = blk[0:1] if reverse else blk[7:8]
        out[g] = blk
    return jnp.concatenate(out, axis=0)


def _shift_down(x, halo, s):
    if s == 0:
        return x
    r = pltpu.roll(x, s, 0)
    top = jnp.where(_rows(halo.shape) < s, pltpu.roll(halo, s, 0), r[0:8])
    return jnp.concatenate([top, r[8:]], axis=0)


def _shift_up(x, head, s):
    if s == 0:
        return x
    n = x.shape[0]
    r = pltpu.roll(x, n - s, 0)
    bottom = jnp.where(_rows(head.shape) >= 8 - s, pltpu.roll(head, 8 - s, 0), r[n - 8:n])
    return jnp.concatenate([r[:n - 8], bottom], axis=0)


def _lru_gates(lx, halo, cw, cb, wa, wx, ba, bx, sp):
    xs = [_shift_down(lx, halo, 3 - k) for k in range(4)]
    xc = cb + cw[0:1] * xs[0] + cw[1:2] * xs[1] + cw[2:3] * xs[2] + cw[3:4] * xs[3]
    zr, zi = [], []
    for n in range(LRU_BLOCKS):
        xb = _mx(xc[:, n * LRU_BW:(n + 1) * LRU_BW])
        zr.append(jnp.dot(xb, wa[n], preferred_element_type=F32))
        zi.append(jnp.dot(xb, wx[n], preferred_element_type=F32))
    r = _sigmoid(jnp.concatenate(zr, axis=1) + ba)
    ig = _sigmoid(jnp.concatenate(zi, axis=1) + bx)
    la = (-LRU_C * sp) * r
    a = jnp.exp(la)
    om = _one_minus_sq(a, 2.0 * la)
    return xs, xc, r, ig, a, om


def _hgrn_prep(q, fr, lb):
    sig = _sigmoid(fr)
    f = lb + (1.0 - lb) * sig
    sq = _sigmoid(q)
    b = _cumsum_rows(jnp.log(f))
    bm = b[CHUNK // 2 - 1:CHUNK // 2]
    bc = b[CHUNK - 1:CHUNK]
    em1 = jnp.exp(jnp.minimum(b - bm, EXP_CLAMP))
    em2 = jnp.exp(jnp.minimum(bm - b, EXP_CLAMP))
    qm = _mx((q * sq) * em1)
    km = _mx((1.0 - f) * em2)
    return sig, f, sq, em1, em2, qm, km, jnp.exp(bm), jnp.exp(bc - bm), jnp.exp(bc)


def _seg(ref, s, rows=slice(None)):
    return jnp.concatenate([ref[2 * s, rows, :], ref[2 * s + 1, rows, :]], axis=1)


def _pad_lanes(parts, width):
    have = sum(p.shape[1] for p in parts)
    pad = [jnp.zeros((parts[0].shape[0], width - have), parts[0].dtype)] if width > have else []
    return jnp.concatenate(list(parts) + pad, axis=1)


def _pad_rows(parts, height):
    have = sum(p.shape[0] for p in parts)
    pad = [jnp.zeros((height - have, parts[0].shape[1]), parts[0].dtype)] if height > have else []
    return jnp.concatenate(list(parts) + pad, axis=0)


def _adamw(w, g, m, v):
    m = ADAM_B1 * m + (1.0 - ADAM_B1) * g
    v = ADAM_B2 * v + (1.0 - ADAM_B2) * jnp.square(g)
    m_hat = m / (1.0 - ADAM_B1 ** ADAM_STEP)
    v_hat = v / (1.0 - ADAM_B2 ** ADAM_STEP)
    delta = -ADAM_LR * (m_hat / (jnp.sqrt(v_hat) + ADAM_EPS) + ADAM_WD * w)
    return delta, m, v


def _full(shape):
    nd = len(shape)
    return pl.BlockSpec(shape, lambda *_: (0,) * nd)


def _seq_params():
    return pltpu.CompilerParams(dimension_semantics=("arbitrary",))


def _resident(shape):
    nd = len(shape)
    return pl.BlockSpec(shape, lambda *_: (0,) * nd, pipeline_mode=pl.Buffered(1))


def _in_proj_gather(x, w, win_units, halved, small_w, tm):
    t = x.shape[0]
    nt = t // tm
    n_h = len(halved)
    n_split = 3 + n_h
    n_items = n_split + 1
    hu = D_MODEL // 2
    sub = min(256, tm)
    chip = 2 * lax.axis_index("x") + lax.axis_index("y")
    order = jnp.stack([chip] + [jnp.bitwise_xor(chip, flip) for flip in (2, 1, 3)])

    def body(order_ref, x_ref, w_ref, win_ref, *rest):
        del order_ref
        h_ins, small_ref = rest[:n_h], rest[n_h]
        ut_ref, p_ref, wg_ref = rest[n_h + 1:n_h + 4]
        h_outs, sw_ref = rest[n_h + 4:2 * n_h + 4], rest[2 * n_h + 4]
        u_s, wbuf, send, recv, fsend, frecv, osend, orecv, wsem = rest[2 * n_h + 5:]
        g, i = pl.program_id(0), pl.program_id(1)
        mx, my, c = lax.axis_index("x"), lax.axis_index("y"), lax.axis_index("c")
        me = 2 * mx + my
        peers = [(1 - mx, my), (mx, 1 - my), (1 - mx, 1 - my)]
        qs = [2 * px + py for px, py in peers]
        sibling = (mx, my, 1 - c)

        def remote(src, dst, ssem, rsem, dev):
            return pltpu.make_async_remote_copy(src_ref=src, dst_ref=dst, send_sem=ssem, recv_sem=rsem,
                                                device_id=dev, device_id_type=MESH)

        def mine(item):
            if item < 3:
                return win_ref.at[item, pl.ds(c * hu, hu)]
            return h_ins[item - 3].at[c] if item < n_split else small_ref

        def piece(item, q, h):
            if item < 3:
                return wg_ref.at[q, item, pl.ds(h * hu, hu)]
            return h_outs[item - 3].at[q, h] if item < n_split else sw_ref.at[q]

        def fetch(item, jj, q):
            px, py = peers[jj]
            return remote(mine(item), piece(item, q, c), send.at[item * 3 + jj], recv.at[item * 3 + jj], (px, py, c))

        def forward(item, jj, h):
            whole = piece(item, qs[jj], h)
            step = whole.shape[0] // FWD_PIECES
            copies = []
            for k in range(FWD_PIECES):
                part = whole.at[pl.ds(k * step, step)]
                sem = (item * 3 + jj) * FWD_PIECES + k
                copies.append(remote(part, part, fsend.at[sem], frecv.at[sem], sibling))
            return copies

        def own(item):
            if item < 3:
                src, dst = win_ref.at[item], wg_ref.at[me, item]
            elif item < n_split:
                src, dst = h_ins[item - 3], h_outs[item - 3].at[me]
            else:
                src, dst = small_ref, sw_ref.at[me]
            return remote(src, dst, osend.at[item], orecv.at[item], sibling)

        def load_units(gg):
            src = lambda k: win_ref.at[k] if gg == 0 else wg_ref.at[qs[gg - 1], k]
            return [pltpu.make_async_copy(src(k), wbuf.at[gg % 2, k], wsem.at[(gg % 2) * 3 + k]) for k in range(3)]

        def in_pass(gg):
            return pl.when(g == gg)

        def pass_on(gg):
            for item in range(3):
                fetch(item, gg, qs[gg]).wait_recv()
                for cp in forward(item, gg, c):
                    cp.start()

        @pl.when(i == 0)
        def _():
            @in_pass(0)
            def _():
                for item in range(3):
                    for jj in range(2):
                        fetch(item, jj, me).start()
                for item in range(n_items):
                    own(item).start()
                for cp in load_units(0):
                    cp.start()

            @in_pass(1)
            def _():
                for item in range(3):
                    fetch(item, 2, me).start()

            @in_pass(2)
            def _():
                for item in range(3, n_items):
                    for jj in range(3):
                        fetch(item, jj, me).start()

            for gg in range(N_CHIPS):
                @in_pass(gg)
                def _(gg=gg):
                    for cp in load_units(gg):
                        cp.wait()

        @pl.when(g == 0)
        def _():
            xv = x_ref[...]
            rstd = lax.rsqrt(jnp.mean(xv * xv, axis=-1, keepdims=True) + EPS)
            u = xv * rstd * w_ref[...]
            ut_ref[...] = u.T.astype(_MXU_DTYPE)
            u_s[pl.ds(pl.multiple_of(i * tm, tm), tm), :] = _mx(u)

        slot = g % 2
        for r in range(tm // sub):
            ur = u_s[pl.ds(pl.multiple_of(i * tm + r * sub, sub), sub), :]
            for k in range(3):
                p_ref[k, r * sub:(r + 1) * sub, :] = jnp.dot(ur, wbuf[slot, k], preferred_element_type=F32)

        @pl.when(i == nt // 2)
        def _():
            for gg in (1, 2):
                in_pass(gg)(functools.partial(pass_on, gg))

        @pl.when(i == nt - 1)
        def _():
            in_pass(0)(functools.partial(pass_on, 0))
            for gg in range(3):
                @in_pass(gg)
                def _(gg=gg):
                    for item in range(3):
                        for cp in forward(item, gg, 1 - c):
                            cp.wait_recv()
                    for cp in load_units(gg + 1):
                        cp.start()

            @in_pass(N_CHIPS - 1)
            def _():
                for item in range(3, n_items):
                    for jj in range(3):
                        fetch(item, jj, qs[jj]).wait_recv()
                        if item < n_split:
                            for cp in forward(item, jj, c):
                                cp.start()
                for item in range(3, n_split):
                    for jj in range(3):
                        for cp in forward(item, jj, 1 - c):
                            cp.wait_recv()
                for item in range(n_items):
                    own(item).wait_recv()
                for item in range(n_items):
                    for jj in range(3):
                        fetch(item, jj, me).wait_send()
                        if item < n_split:
                            for cp in forward(item, jj, c):
                                cp.wait_send()
                    own(item).wait_send()

    any_spec = pl.BlockSpec(memory_space=pl.ANY)
    first_pass = lambda g, i: jnp.where(g == 0, i, nt - 1)
    res = pl.pallas_call(
        body, name="in_proj",
        grid_spec=pltpu.PrefetchScalarGridSpec(
            num_scalar_prefetch=1, grid=(N_CHIPS, nt),
            in_specs=[pl.BlockSpec((tm, D_MODEL), lambda g, i, o: (first_pass(g, i), 0)),
                      pl.BlockSpec((1, D_MODEL), lambda g, i, o: (0, 0))] + [any_spec] * (n_h + 2),
            out_specs=[pl.BlockSpec((D_MODEL, tm), lambda g, i, o: (0, first_pass(g, i))),
                       pl.BlockSpec((3, tm, UNIT), lambda g, i, o: (o[g], i, 0))]
            + [any_spec] * (n_h + 2),
            scratch_shapes=[pltpu.VMEM((t, D_MODEL), _MXU_DTYPE), pltpu.VMEM((2, 3, D_MODEL, UNIT), _MXU_DTYPE),
                            pltpu.SemaphoreType.DMA((3 * n_items,)), pltpu.SemaphoreType.DMA((3 * n_items,)),
                            pltpu.SemaphoreType.DMA((3 * n_split * FWD_PIECES,)),
                            pltpu.SemaphoreType.DMA((3 * n_split * FWD_PIECES,)),
                            pltpu.SemaphoreType.DMA((n_items,)), pltpu.SemaphoreType.DMA((n_items,)),
                            pltpu.SemaphoreType.DMA((6,))]),
        out_shape=[jax.ShapeDtypeStruct((D_MODEL, t), _MXU_DTYPE), jax.ShapeDtypeStruct((N_UNIT, t, UNIT), F32),
                   jax.ShapeDtypeStruct((N_CHIPS,) + win_units.shape, win_units.dtype)]
        + [jax.ShapeDtypeStruct((N_CHIPS,) + a.shape, a.dtype) for a in halved]
        + [jax.ShapeDtypeStruct((N_CHIPS,) + small_w.shape, small_w.dtype)],
        compiler_params=pltpu.CompilerParams(dimension_semantics=("arbitrary", "arbitrary")),
    )(order, x, w, win_units, *halved, small_w)
    return res[0], res[1], res[2], res[3:3 + n_h], res[3 + n_h]


def _lru_fwd(p, wa, wx, ba, bx, lam, cw, cb, tm):
    t = p.shape[1]
    ng = tm // 8

    def body(lx_ref, halo_ref, lg_ref, wa_ref, wx_ref, ba_ref, bx_ref, lam_ref, cw_ref, cb_ref,
             h_ref, y_ref, a_s, u_s, hc_s):
        i = pl.program_id(0)

        @pl.when(i == 0)
        def _():
            hc_s[...] = jnp.zeros_like(hc_s)

        sp = _softplus_neg(lam_ref[...])
        halo = jnp.where(i == 0, 0.0, _seg(halo_ref, 0))
        _, xc, _, ig, a, om = _lru_gates(_seg(lx_ref, 0), halo, cw_ref[...], cb_ref[...], wa_ref[...],
                                         wx_ref[...], ba_ref[...], bx_ref[...], sp)
        a_s[...] = a
        u_s[...] = jnp.sqrt(om) * (ig * xc)
        row8 = _rows((8, D_MODEL))

        def group(g, hc):
            rows = pl.ds(pl.multiple_of(g * 8, 8), 8)
            av = a_s[rows, :]
            uv = u_s[rows, :]
            for s in (1, 2, 4):
                a_sh = jnp.where(row8 >= s, pltpu.roll(av, s, 0), 1.0)
                u_sh = jnp.where(row8 >= s, pltpu.roll(uv, s, 0), 0.0)
                uv = av * u_sh + uv
                av = av * a_sh
            hh = av * hc + uv
            h_ref[rows, :] = hh
            return jnp.broadcast_to(hh[7:8, :], (8, D_MODEL))

        hc_s[...] = lax.fori_loop(0, ng, group, hc_s[...])
        lg = _seg(lg_ref, 0)
        y_ref[...] = (h_ref[...] * (lg * _sigmoid(lg))).astype(_MXU_DTYPE)

    seg = lambda s: pl.BlockSpec((2, tm, UNIT), lambda i: (s, i, 0))
    return pl.pallas_call(
        body, name="lru_fwd", grid=(t // tm,),
        in_specs=[seg(0),
                  pl.BlockSpec((2, 8, UNIT), lambda i: (0, jnp.maximum(i * (tm // 8) - 1, 0), 0)),
                  seg(1),
                  _full((LRU_BLOCKS, LRU_BW, LRU_BW)), _full((LRU_BLOCKS, LRU_BW, LRU_BW)),
                  _full((1, D_MODEL)), _full((1, D_MODEL)), _full((1, D_MODEL)),
                  _full((4, D_MODEL)), _full((1, D_MODEL))],
        out_specs=[pl.BlockSpec((tm, D_MODEL), lambda i: (i, 0)),
                   pl.BlockSpec((None, tm, D_MODEL), lambda i: (0, i, 0))],
        out_shape=[jax.ShapeDtypeStruct((t, D_MODEL), F32),
                   jax.ShapeDtypeStruct((2, t, D_MODEL), _MXU_DTYPE)],
        scratch_shapes=[pltpu.VMEM((tm, D_MODEL), F32), pltpu.VMEM((tm, D_MODEL), F32),
                        pltpu.VMEM((8, D_MODEL), F32)],
        compiler_params=_seq_params(),
    )(p, p, p, wa, wx, ba, bx, lam, cw, cb)


def _hgrn_fwd(p, logits, gw, y2, tm):
    t = p.shape[1]
    nc = tm // CHUNK

    def body(qf_ref, vh_ref, lg_ref, gw_ref, y_in, y_ref, o_ref, st_ref, state):
        del y_in
        i = pl.program_id(0)

        @pl.when(i == 0)
        def _():
            state[...] = jnp.zeros_like(state)

        l0 = lg_ref[0:1, :]
        l1 = lg_ref[1:2, :]
        lb = 1.0 / (1.0 + jnp.exp(l1 - l0))
        gwv = gw_ref[...]
        causal = _rows((CHUNK, CHUNK)) >= _cols((CHUNK, CHUNK))

        for c in range(nc):
            rows = pl.ds(c * CHUNK, CHUNK)
            v = _seg(vh_ref, 0, rows)
            hg = _seg(vh_ref, 1, rows)
            _, _, _, _, _, qm, km, ebm, ecm, ebc = _hgrn_prep(_seg(qf_ref, 0, rows), _seg(qf_ref, 1, rows), lb)
            ke = _mx(km.astype(F32) * ecm)
            gate = gwv * (hg * _sigmoid(hg))
            o_parts, y_parts = [], []
            for h in range(HEADS):
                ln = slice(h * HEAD_D, (h + 1) * HEAD_D)
                s0 = state[h]
                st_ref[c, h] = s0
                vt = _mx(v[:, ln].T)
                att = _mx(jnp.where(causal, _dot_nt(qm[:, ln], km[:, ln]), 0.0))
                o = _dot_nt(_pad_lanes([qm[:, ln], att], 2 * HEAD_D),
                            _pad_lanes([_mx(s0 * ebm[:, ln]), vt], 2 * HEAD_D))
                state[h] = s0 * ebc[:, ln] + jnp.dot(vt, ke[:, ln], preferred_element_type=F32)
                rstd = lax.rsqrt(jnp.mean(o * o, axis=-1, keepdims=True) + EPS)
                o_parts.append(o)
                y_parts.append((o * rstd * gate[:, ln]).astype(_MXU_DTYPE))
            o_ref[rows, :] = jnp.concatenate(o_parts, axis=1)
            y_ref[rows, :] = jnp.concatenate(y_parts, axis=1)

    pair = lambda s: pl.BlockSpec((4, tm, UNIT), lambda i: (s, i, 0))
    return pl.pallas_call(
        body, name="hgrn_fwd", grid=(t // tm,),
        in_specs=[pair(1), pair(2), _full((2, D_MODEL)), _full((1, D_MODEL)),
                  pl.BlockSpec(memory_space=pl.ANY)],
        out_specs=[pl.BlockSpec((None, tm, D_MODEL), lambda i: (1, i, 0)),
                   pl.BlockSpec((tm, D_MODEL), lambda i: (i, 0)),
                   pl.BlockSpec((nc, HEADS, HEAD_D, HEAD_D), lambda i: (i, 0, 0, 0))],
        out_shape=[jax.ShapeDtypeStruct((2, t, D_MODEL), _MXU_DTYPE),
                   jax.ShapeDtypeStruct((t, D_MODEL), F32),
                   jax.ShapeDtypeStruct((t // CHUNK, HEADS, HEAD_D, HEAD_D), F32)],
        scratch_shapes=[pltpu.VMEM((HEADS, HEAD_D, HEAD_D), F32)],
        input_output_aliases={4: 0},
        compiler_params=_seq_params(),
    )(p, p, logits, gw, y2)


def _out_fused(y2, wo, x, tgt, wpost, tm):
    t = x.shape[0]
    inv_d = 1.0 / D_MODEL

    def body(y_ref, wo_ref, x_ref, t_ref, w_ref, dout_ref, dy_ref, sq_ref, dw_ref):
        i = pl.program_id(0)

        @pl.when(i == 0)
        def _():
            sq_ref[...] = jnp.zeros_like(sq_ref)
            dw_ref[...] = jnp.zeros_like(dw_ref)

        y = (jnp.dot(y_ref[0], wo_ref[0:D_MODEL, :], preferred_element_type=F32)
             + jnp.dot(y_ref[1], wo_ref[D_MODEL:2 * D_MODEL, :], preferred_element_type=F32))
        w = w_ref[...]
        rstd = lax.rsqrt(jnp.mean(y * y, axis=-1, keepdims=True) + EPS)
        n = y * rstd
        err = (x_ref[...] + n * w) - t_ref[...]
        sq_ref[...] += jnp.sum(err * err, axis=0, keepdims=True)
        dout = err * inv_d
        dout_ref[...] = dout
        dw_ref[...] += jnp.sum(dout * n, axis=0, keepdims=True)
        dn = dout * w
        dy_ref[...] = (rstd * (dn - n * jnp.mean(dn * n, axis=-1, keepdims=True))).astype(_MXU_DTYPE)

    row = pl.BlockSpec((tm, D_MODEL), lambda i: (i, 0))
    return pl.pallas_call(
        body, name="out_fused", grid=(t // tm,),
        in_specs=[pl.BlockSpec((2, tm, D_MODEL), lambda i: (0, i, 0)), _full((2 * D_MODEL, D_MODEL)),
                  row, row, _full((1, D_MODEL))],
        out_specs=[row, row, _full((1, D_MODEL)), _full((1, D_MODEL))],
        out_shape=[jax.ShapeDtypeStruct((t, D_MODEL), F32), jax.ShapeDtypeStruct((t, D_MODEL), _MXU_DTYPE),
                   jax.ShapeDtypeStruct((1, D_MODEL), F32), jax.ShapeDtypeStruct((1, D_MODEL), F32)],
        compiler_params=_seq_params(),
    )(y2, wo, x, tgt, wpost)


def _dw_out(y2, dy, tt):
    t = dy.shape[0]
    hr = UNIT // 2

    def body(y_ref, dy_ref, g_ref):
        @pl.when(pl.program_id(1) == 0)
        def _():
            g_ref[...] = jnp.zeros_like(g_ref)

        r = _dot_tn(y_ref[...], dy_ref[...])
        g_ref[0] += r[0:hr]
        g_ref[1] += r[hr:UNIT]

    return pl.pallas_call(
        body, name="dw_out", grid=(N_CHIPS, t // tt),
        in_specs=[pl.BlockSpec((None, tt, UNIT), lambda c, k: (c // 2, k, c % 2)),
                  pl.BlockSpec((tt, D_MODEL), lambda c, k: (k, 0))],
        out_specs=pl.BlockSpec((2, None, hr, D_MODEL), lambda c, k: (0, c, 0, 0)),
        out_shape=jax.ShapeDtypeStruct((2, N_CHIPS, hr, D_MODEL), F32),
        compiler_params=pltpu.CompilerParams(dimension_semantics=("arbitrary", "arbitrary")),
    )(y2, dy)


def _lru_bwd(dy, wo, p, h, wa, wx, ba, bx, lam, cw, cb, tm):
    t = dy.shape[0]
    nt = t // tm
    ng = tm // 8

    def body(dy_ref, wo_ref, lx_ref, halo_ref, lg_ref, h_ref, hhalo_ref, wa_ref, wx_ref, ba_ref, bx_ref,
             lam_ref, cw_ref, cb_ref,
             dp_ref, dwa_ref, dwx_ref, dba_ref, dbx_ref, dsp_ref, dcw_ref, dcb_ref,
             c_s, in_s, dh_s, dhc_s, afirst_s, dxc_s):
        i = pl.program_id(0)
        first = i == nt - 1

        @pl.when(i == 0)
        def _():
            for ref in (dwa_ref, dwx_ref, dba_ref, dbx_ref, dsp_ref, dcw_ref, dcb_ref, dhc_s, afirst_s, dxc_s):
                ref[...] = jnp.zeros_like(ref)

        sp = _softplus_neg(lam_ref[...])
        cwv = cw_ref[...]
        wav = wa_ref[...]
        wxv = wx_ref[...]
        halo = jnp.where(first, 0.0, _seg(halo_ref, 0))
        xs, xc, r, ig, a, om = _lru_gates(_seg(lx_ref, 0), halo, cwv, cb_ref[...], wav, wxv,
                                          ba_ref[...], bx_ref[...], sp)
        inv_mult = lax.rsqrt(om)
        mult = om * inv_mult
        d_y = _dot_nt(dy_ref[...], wo_ref[...])
        lg = _seg(lg_ref, 0)
        sl = _sigmoid(lg)
        hv = h_ref[...]
        d_lg = d_y * hv * (sl * (1.0 + lg * (1.0 - sl)))

        c_s[...] = _shift_up(a, afirst_s[...], 1)
        in_s[...] = d_y * (lg * sl)
        row8 = _rows((8, D_MODEL))

        def group(gg, carry):
            rows = pl.ds(pl.multiple_of((ng - 1 - gg) * 8, 8), 8)
            cv = c_s[rows, :]
            uv = in_s[rows, :]
            for s in (1, 2, 4):
                c_sh = jnp.where(row8 < 8 - s, pltpu.roll(cv, 8 - s, 0), 1.0)
                u_sh = jnp.where(row8 < 8 - s, pltpu.roll(uv, 8 - s, 0), 0.0)
                uv = cv * u_sh + uv
                cv = cv * c_sh
            hh = cv * carry + uv
            dh_s[rows, :] = hh
            return jnp.broadcast_to(hh[0:1, :], (8, D_MODEL))

        dhc_s[...] = lax.fori_loop(0, ng, group, dhc_s[...])
        afirst_s[...] = a[0:8]
        dh = dh_s[...]

        h_prev = _shift_down(hv, jnp.where(first, 0.0, hhalo_ref[...]), 1)
        gx = ig * xc
        d_mult = dh * gx
        d_i = dh * (mult * xc)
        d_xc = dh * (mult * ig)
        d_la = (dh * h_prev) * a - d_mult * ((a * a) * inv_mult)
        dsp_ref[...] += jnp.sum(d_la * r, axis=0, keepdims=True) * (-LRU_C)
        d_zr = (d_la * (-LRU_C * sp)) * (r * (1.0 - r))
        d_zi = d_i * (ig * (1.0 - ig))
        dba_ref[...] += jnp.sum(d_zr, axis=0, keepdims=True)
        dbx_ref[...] += jnp.sum(d_zi, axis=0, keepdims=True)
        back = []
        for n in range(LRU_BLOCKS):
            ln = slice(n * LRU_BW, (n + 1) * LRU_BW)
            xb = _mx(xc[:, ln])
            zr_n = _mx(d_zr[:, ln])
            zi_n = _mx(d_zi[:, ln])
            dwa_ref[n] += _dot_tn(xb, zr_n)
            dwx_ref[n] += _dot_tn(xb, zi_n)
            back.append(_dot_nt(zr_n, wav[n]) + _dot_nt(zi_n, wxv[n]))
        d_xc = d_xc + jnp.concatenate(back, axis=1)
        dcb_ref[...] += jnp.sum(d_xc, axis=0, keepdims=True)
        for k in range(4):
            dcw_ref[k:k + 1, :] += jnp.sum(d_xc * xs[k], axis=0, keepdims=True)
        head = dxc_s[...]
        d_lx = cwv[3:4] * d_xc
        for k in range(3):
            d_lx = d_lx + cwv[k:k + 1] * _shift_up(d_xc, head, 3 - k)
        dxc_s[...] = d_xc[0:8]
        dp_ref[0] = d_lx.astype(_MXU_DTYPE)
        dp_ref[1] = d_lg.astype(_MXU_DTYPE)

    rev = lambda i: nt - 1 - i
    seg = lambda s: pl.BlockSpec((2, tm, UNIT), lambda i: (s, rev(i), 0))
    halo_row = lambda i: jnp.maximum(rev(i) * (tm // 8) - 1, 0)
    row = pl.BlockSpec((tm, D_MODEL), lambda i: (rev(i), 0))
    vec = _full((1, D_MODEL))
    wblk = _full((LRU_BLOCKS, LRU_BW, LRU_BW))
    return pl.pallas_call(
        body, name="lru_bwd", grid=(nt,),
        in_specs=[row, pl.BlockSpec((D_MODEL, D_MODEL), lambda i: (0, 0)),
                  seg(0), pl.BlockSpec((2, 8, UNIT), lambda i: (0, halo_row(i), 0)), seg(1),
                  row, pl.BlockSpec((8, D_MODEL), lambda i: (halo_row(i), 0)),
                  wblk, wblk, vec, vec, vec, _full((4, D_MODEL)), vec],
        out_specs=[pl.BlockSpec((2, tm, D_MODEL), lambda i: (2, rev(i), 0)),
                   wblk, wblk, vec, vec, vec, _full((4, D_MODEL)), vec],
        out_shape=[jax.ShapeDtypeStruct((N_SEG, t, D_MODEL), _MXU_DTYPE),
                   jax.ShapeDtypeStruct((LRU_BLOCKS, LRU_BW, LRU_BW), F32),
                   jax.ShapeDtypeStruct((LRU_BLOCKS, LRU_BW, LRU_BW), F32),
                   jax.ShapeDtypeStruct((1, D_MODEL), F32), jax.ShapeDtypeStruct((1, D_MODEL), F32),
                   jax.ShapeDtypeStruct((1, D_MODEL), F32), jax.ShapeDtypeStruct((4, D_MODEL), F32),
                   jax.ShapeDtypeStruct((1, D_MODEL), F32)],
        scratch_shapes=[pltpu.VMEM((tm, D_MODEL), F32), pltpu.VMEM((tm, D_MODEL), F32),
                        pltpu.VMEM((tm, D_MODEL), F32), pltpu.VMEM((8, D_MODEL), F32),
                        pltpu.VMEM((8, D_MODEL), F32), pltpu.VMEM((8, D_MODEL), F32)],
        compiler_params=_seq_params(),
    )(dy, wo, p, p, p, h, h, wa, wx, ba, bx, lam, cw, cb)


def _hgrn_bwd(dy, wo, p, o, states, logits, gw, dp, tm):
    t = dy.shape[0]
    nt = t // tm
    nc = tm // CHUNK

    def body(dy_ref, wo_ref, qf_ref, vh_ref, o_ref, st_ref, lg_ref, gw_ref, dp_in,
             dp_ref, dgw_ref, dlb_ref, dyh_s, dstate, snext):
        del dp_in
        i = pl.program_id(0)

        @pl.when(i == 0)
        def _():
            for ref in (dgw_ref, dlb_ref, dstate, snext):
                ref[...] = jnp.zeros_like(ref)

        dyh_s[...] = _dot_nt(dy_ref[...], wo_ref[...])
        l0 = lg_ref[0:1, :]
        l1 = lg_ref[1:2, :]
        lb = 1.0 / (1.0 + jnp.exp(l1 - l0))
        gwv = gw_ref[...]
        causal = _rows((CHUNK, CHUNK)) >= _cols((CHUNK, CHUNK))
        wide = 2 * HEAD_D
        zeros_h = jnp.zeros((CHUNK, HEAD_D), _MXU_DTYPE)

        for c in reversed(range(nc)):
            rows = pl.ds(c * CHUNK, CHUNK)
            q = _seg(qf_ref, 0, rows)
            hg = _seg(vh_ref, 1, rows)
            ov = o_ref[rows, :]
            dyh = dyh_s[rows, :]
            sig, f, sq, em1, em2, qm, km, ebm, ecm, ebc = _hgrn_prep(q, _seg(qf_ref, 1, rows), lb)
            ke = _mx(km.astype(F32) * ecm)
            vb = _mx(_seg(vh_ref, 0, rows))
            sg = _sigmoid(hg)
            d_on = dyh * (hg * sg)
            x_p, y_p, dv_p, n_p, ex_p = [], [], [], [], []
            for h in range(HEADS):
                ln = slice(h * HEAD_D, (h + 1) * HEAD_D)
                o_h = ov[:, ln]
                rstd = lax.rsqrt(jnp.mean(o_h * o_h, axis=-1, keepdims=True) + EPS)
                n = o_h * rstd
                dn = d_on[:, ln] * gwv[:, ln]
                d_o = rstd * (dn - n * jnp.mean(dn * n, axis=-1, keepdims=True))
                d_ob = _mx(d_o)
                d_ot = _mx(d_o.T)
                s0 = st_ref[c, h]
                ds = dstate[h]
                both = _dot_nt(jnp.concatenate([_pad_lanes([qm[:, ln]], wide), _pad_lanes([zeros_h, d_ob], wide)], axis=0),
                               jnp.concatenate([km[:, ln], vb[:, ln]], axis=1))
                att = jnp.where(causal, both[0:CHUNK], 0.0)
                d_att = jnp.where(causal, both[CHUNK:2 * CHUNK], 0.0)
                x_p.append(jnp.dot(_pad_lanes([d_ob, _mx(d_att)], wide),
                                   _pad_rows([_mx(s0 * ebm[:, ln]), km[:, ln]], wide), preferred_element_type=F32))
                y_p.append(jnp.dot(_pad_lanes([vb[:, ln], _mx(d_att.T)], wide),
                                   _pad_rows([_mx(ds * ecm[:, ln]), qm[:, ln]], wide), preferred_element_type=F32))
                dv_p.append(_dot_nt(_pad_lanes([ke[:, ln], _mx(att.T)], wide), _pad_lanes([_mx(ds), d_ot], wide)))
                ex_p.append(jnp.sum(ds * snext[h], axis=0, keepdims=True))
                dstate[h] = ds * ebc[:, ln] + jnp.dot(d_ot, qm[:, ln], preferred_element_type=F32) * ebm[:, ln]
                snext[h] = s0
                n_p.append(n)
            cat = lambda parts: jnp.concatenate(parts, axis=1)
            x_all, y_all, n_all = cat(x_p), cat(y_p), cat(n_p)
            dg = _cumsum_rows(qm.astype(F32) * x_all - km.astype(F32) * y_all, reverse=True) + cat(ex_p)
            df = dg / f - em2 * y_all
            dgw_ref[...] += jnp.sum(d_on * n_all, axis=0, keepdims=True)
            dlb_ref[...] += jnp.sum(df * (1.0 - sig), axis=0, keepdims=True)
            dp_ref[0, rows, :] = ((em1 * x_all) * (sq * (1.0 + q * (1.0 - sq)))).astype(_MXU_DTYPE)
            dp_ref[1, rows, :] = (df * ((1.0 - lb) * sig * (1.0 - sig))).astype(_MXU_DTYPE)
            dp_ref[2, rows, :] = cat(dv_p).astype(_MXU_DTYPE)
            dp_ref[3, rows, :] = (dyh * (n_all * gwv) * (sg * (1.0 + hg * (1.0 - sg)))).astype(_MXU_DTYPE)

    rev = lambda i: nt - 1 - i
    pair = lambda s: pl.BlockSpec((4, tm, UNIT), lambda i: (s, rev(i), 0))
    row = pl.BlockSpec((tm, D_MODEL), lambda i: (rev(i), 0))
    vec = _full((1, D_MODEL))
    return pl.pallas_call(
        body, name="hgrn_bwd", grid=(nt,),
        in_specs=[row, pl.BlockSpec((D_MODEL, D_MODEL), lambda i: (1, 0)), pair(1), pair(2), row,
                  pl.BlockSpec((nc, HEADS, HEAD_D, HEAD_D), lambda i: (rev(i), 0, 0, 0)),
                  _full((2, D_MODEL)), vec, pl.BlockSpec(memory_space=pl.ANY)],
        out_specs=[pl.BlockSpec((4, tm, D_MODEL), lambda i: (0, rev(i), 0)), vec, vec],
        out_shape=[jax.ShapeDtypeStruct((N_SEG, t, D_MODEL), _MXU_DTYPE),
                   jax.ShapeDtypeStruct((1, D_MODEL), F32), jax.ShapeDtypeStruct((1, D_MODEL), F32)],
        scratch_shapes=[pltpu.VMEM((tm, D_MODEL), F32), pltpu.VMEM((HEADS, HEAD_D, HEAD_D), F32),
                        pltpu.VMEM((HEADS, HEAD_D, HEAD_D), F32)],
        input_output_aliases={8: 0},
        compiler_params=_seq_params(),
    )(dy, wo, p, p, o, states, logits, gw, dp)


def _dp_unit(m):
    return ((m // 2 + 2) % N_SEG) * 2 + m % 2


def _dw_in(ut, dp, tt):
    t = ut.shape[1]
    hr = D_MODEL // 2

    def body(ut_ref, dp_ref, g_ref):
        @pl.when(pl.program_id(1) == 0)
        def _():
            g_ref[...] = jnp.zeros_like(g_ref)

        r = jnp.dot(ut_ref[...], dp_ref[...], preferred_element_type=F32)
        for h in range(2):
            for half in range(2):
                g_ref[h, half] += r[h * hr:(h + 1) * hr, half * UNIT:(half + 1) * UNIT]

    return pl.pallas_call(
        body, name="dw_in", grid=(N_SEG, t // tt),
        in_specs=[pl.BlockSpec((D_MODEL, tt), lambda s, k: (0, k)),
                  pl.BlockSpec((None, tt, D_MODEL), lambda s, k: (s, k, 0))],
        out_specs=pl.BlockSpec((2, 2, hr, UNIT), lambda s, k: (0, (s + 2) % N_SEG, 0, 0)),
        out_shape=jax.ShapeDtypeStruct((2, N_UNIT, hr, UNIT), F32),
        compiler_params=pltpu.CompilerParams(dimension_semantics=("arbitrary", "arbitrary")),
    )(ut, dp)


def _dx(dp, wg, x, dout, wpre, tm, parts, scatter):
    t = x.shape[0]
    n = len(parts)
    nt = t // tm

    def body(*refs):
        dp_ref, w_ref, x_ref, dout_ref, wpre_ref = refs[:5]
        ins = refs[5:5 + n]
        gx_ref, dw_ref = refs[5 + n:7 + n]
        outs = refs[7 + n:7 + 2 * n]
        send_sems, recv_sems = refs[7 + 2 * n:]
        i = pl.program_id(0)

        @pl.when(i == 0)
        def _():
            dw_ref[...] = jnp.zeros_like(dw_ref)
            for cp in _chip_copies(ins, outs, send_sems, recv_sems, scatter):
                cp.start()

        du = None
        for m in range(N_UNIT):
            part = _dot_nt(dp_ref[m // 2, :, (m % 2) * UNIT:(m % 2 + 1) * UNIT], w_ref[_dp_unit(m)])
            du = part if du is None else du + part
        xv = x_ref[...]
        rstd = lax.rsqrt(jnp.mean(xv * xv, axis=-1, keepdims=True) + EPS)
        xn = xv * rstd
        dw_ref[...] += jnp.sum(du * xn, axis=0, keepdims=True)
        dn = du * wpre_ref[...]
        gx_ref[...] = dout_ref[...] + rstd * (dn - xn * jnp.mean(dn * xn, axis=-1, keepdims=True))

        @pl.when(i == nt - 1)
        def _():
            copies = _chip_copies(ins, outs, send_sems, recv_sems, scatter)
            for cp in copies:
                cp.wait_recv()
            for cp in copies:
                cp.wait_send()

    row = pl.BlockSpec((tm, D_MODEL), lambda i: (i, 0))
    any_spec = pl.BlockSpec(memory_space=pl.ANY)
    res = pl.pallas_call(
        body, name="dx", grid=(nt,),
        in_specs=[pl.BlockSpec((N_SEG, tm, D_MODEL), lambda i: (0, i, 0)), _resident((N_UNIT, D_MODEL, UNIT)),
                  row, row, _full((1, D_MODEL))] + [any_spec] * n,
        out_specs=[row, _full((1, D_MODEL))] + [any_spec] * n,
        out_shape=[jax.ShapeDtypeStruct((t, D_MODEL), F32), jax.ShapeDtypeStruct((1, D_MODEL), F32)]
        + _chip_exchange_shapes(parts, scatter),
        scratch_shapes=[pltpu.SemaphoreType.DMA((max(3 * n, 1),)), pltpu.SemaphoreType.DMA((max(3 * n, 1),))],
        compiler_params=_seq_params(),
    )(dp, wg, x, dout, wpre, *parts)
    return res[0], res[1], res[2:]


CHUNK_BYTES = 256 * 1024
MAX_CHUNKS = 16


def _n_chunks(shape, dtype, rows):
    nbytes = jnp.dtype(dtype).itemsize
    for d in shape:
        nbytes *= d
    k = max(1, min(MAX_CHUNKS, rows, nbytes // CHUNK_BYTES))
    while rows % k:
        k -= 1
    return k


def _chip_exchange_shapes(arrays, scatter):
    return [jax.ShapeDtypeStruct((3,) + tuple(a.shape[1:] if sc else a.shape), a.dtype)
            for a, sc in zip(arrays, scatter)]


def _chip_copies(ins, outs, send_sems, recv_sems, scatter):
    if not ins:
        return []
    x, y, c = lax.axis_index("x"), lax.axis_index("y"), lax.axis_index("c")
    peers = [(1 - x, y), (x, 1 - y), (1 - x, 1 - y)]
    copies = []
    for a in range(len(ins)):
        for j, (px, py) in enumerate(peers):
            copies.append(pltpu.make_async_remote_copy(
                src_ref=ins[a].at[2 * px + py] if scatter[a] else ins[a], dst_ref=outs[a].at[j],
                send_sem=send_sems.at[a * 3 + j], recv_sem=recv_sems.at[a * 3 + j],
                device_id=(px, py, c), device_id_type=MESH))
    return copies


def _core_swap(arrays, name, halved):
    n = len(arrays)
    shapes, n_sems = _core_swap_plan(arrays, halved)

    def body(*refs):
        copies = _core_swap_copies(arrays, halved, refs[:n], refs[n:2 * n], refs[2 * n], refs[2 * n + 1])
        for cp in copies:
            cp.start()
        for cp in copies:
            cp.wait_recv()
        for cp in copies:
            cp.wait_send()

    any_spec = pl.BlockSpec(memory_space=pl.ANY)
    return pl.pallas_call(
        body, name=name,
        in_specs=[any_spec] * n, out_specs=[any_spec] * n,
        out_shape=shapes,
        scratch_shapes=[pltpu.SemaphoreType.DMA((n_sems,)), pltpu.SemaphoreType.DMA((n_sems,))],
    )(*arrays)


def _core_swap_plan(arrays, halved):
    shapes = [a.shape[1:] if halved else a.shape for a in arrays]
    n_sems = sum(_n_chunks(s, a.dtype, s[0]) for s, a in zip(shapes, arrays))
    return [jax.ShapeDtypeStruct(s, a.dtype) for s, a in zip(shapes, arrays)], max(n_sems, 1)


def _core_swap_copies(arrays, halved, ins, outs, send_sems, recv_sems):
    if not arrays:
        return []
    x, y, c = lax.axis_index("x"), lax.axis_index("y"), lax.axis_index("c")
    copies, sem = [], 0
    for a, arr in enumerate(arrays):
        shape = arr.shape[1:] if halved else arr.shape
        k = _n_chunks(shape, arr.dtype, shape[0])
        step = shape[0] // k
        for j in range(k):
            rows = pl.ds(j * step, step)
            copies.append(pltpu.make_async_remote_copy(
                src_ref=ins[a].at[1 - c, rows] if halved else ins[a].at[rows], dst_ref=outs[a].at[rows],
                send_sem=send_sems.at[sem], recv_sem=recv_sems.at[sem],
                device_id=(x, y, 1 - c), device_id_type=MESH))
            sem += 1
    return copies


def _finish_exchange(arrays, small):
    n = len(arrays)
    ks = [_n_chunks(a.shape, a.dtype, a.shape[0]) for a in arrays]
    offs = [sum(ks[:a]) for a in range(n)]
    total = sum(ks)

    def body(*refs):
        ins, small_ref = refs[:n], refs[n]
        outs, gathered = refs[n + 1:2 * n + 1], refs[2 * n + 1]
        send_sems, recv_sems = refs[2 * n + 2:]
        x, y, c = lax.axis_index("x"), lax.axis_index("y"), lax.axis_index("c")
        copies = []
        for a in range(n):
            step = arrays[a].shape[0] // ks[a]
            for j in range(ks[a]):
                rows = pl.ds(j * step, step)
                copies.append(pltpu.make_async_remote_copy(
                    src_ref=ins[a].at[rows], dst_ref=outs[a].at[rows],
                    send_sem=send_sems.at[offs[a] + j], recv_sem=recv_sems.at[offs[a] + j],
                    device_id=(x, y, 1 - c), device_id_type=MESH))
        for slot in range(7):
            dx, dy, dc = (slot + 1) % 2, ((slot + 1) // 2) % 2, (slot + 1) // 4
            copies.append(pltpu.make_async_remote_copy(
                src_ref=small_ref, dst_ref=gathered.at[slot],
                send_sem=send_sems.at[total + slot], recv_sem=recv_sems.at[total + slot],
                device_id=(x + dx - 2 * x * dx, y + dy - 2 * y * dy, c + dc - 2 * c * dc), device_id_type=MESH))
        for cp in copies:
            cp.start()
        for cp in copies:
            cp.wait_recv()
        for cp in copies:
            cp.wait_send()

    any_spec = pl.BlockSpec(memory_space=pl.ANY)
    res = pl.pallas_call(
        body, name="reduce_cores",
        in_specs=[any_spec] * (n + 1), out_specs=[any_spec] * (n + 1),
        out_shape=[jax.ShapeDtypeStruct(a.shape, a.dtype) for a in arrays]
        + [jax.ShapeDtypeStruct((7,) + small.shape, small.dtype)],
        scratch_shapes=[pltpu.SemaphoreType.DMA((total + 7,)), pltpu.SemaphoreType.DMA((total + 7,))],
    )(*arrays, small)
    return res[:n], res[n]


def _sum_own_half(g, b, name, out_dtype, tr=512):
    _, rows, cols = g.shape
    tr = min(tr, rows)
    core = lax.axis_index("c").reshape(1)

    def body(c_ref, g_ref, b_ref, o_ref):
        del c_ref
        o_ref[...] = (g_ref[...] + b_ref[...]).astype(out_dtype)

    return pl.pallas_call(
        body, name=name,
        grid_spec=pltpu.PrefetchScalarGridSpec(
            num_scalar_prefetch=1, grid=(rows // tr,),
            in_specs=[pl.BlockSpec((None, tr, cols), lambda i, c_ref: (c_ref[0], i, 0)),
                      pl.BlockSpec((tr, cols), lambda i, c_ref: (i, 0))],
            out_specs=pl.BlockSpec((tr, cols), lambda i, c_ref: (i, 0))),
        out_shape=jax.ShapeDtypeStruct((rows, cols), out_dtype),
        compiler_params=pltpu.CompilerParams(dimension_semantics=("parallel",)),
    )(core, g, b)


def _sum_chips(own, r, slabbed, name, tr=512):
    _, rows, cols = r.shape
    tr = min(tr, rows)
    chip = (2 * lax.axis_index("x") + lax.axis_index("y")).reshape(1)

    def body(c_ref, own_ref, r_ref, o_ref):
        del c_ref
        f = lambda val: val.astype(F32)
        o_ref[...] = (f(own_ref[...]) + f(r_ref[0])) + (f(r_ref[1]) + f(r_ref[2]))

    if slabbed:
        own_spec = pl.BlockSpec((None, tr, cols), lambda i, c_ref: (c_ref[0], i, 0))
    else:
        own_spec = pl.BlockSpec((tr, cols), lambda i, c_ref: (i, 0))
    return pl.pallas_call(
        body, name=name,
        grid_spec=pltpu.PrefetchScalarGridSpec(
            num_scalar_prefetch=1, grid=(rows // tr,),
            in_specs=[own_spec, pl.BlockSpec((3, tr, cols), lambda i, c_ref: (0, i, 0))],
            out_specs=pl.BlockSpec((tr, cols), lambda i, c_ref: (i, 0))),
        out_shape=jax.ShapeDtypeStruct((rows, cols), F32),
        compiler_params=pltpu.CompilerParams(dimension_semantics=("parallel",)),
    )(chip, own, r)


def _adamw_big(own, other, w, m, v, name, grid, g_block, g_index, w_block, w_index):
    core = lax.axis_index("c").reshape(1)
    half_axis = len(grid) - 1

    def body(c_ref, own_ref, oth_ref, w_ref, m_ref, v_ref, g_ref, d_ref, nm_ref, nv_ref):
        mine = pl.program_id(half_axis) == c_ref[0]
        gv = jnp.where(mine, own_ref[...], oth_ref[...])
        g_ref[...] = gv
        d_ref[...], nm_ref[...], nv_ref[...] = _adamw(w_ref[...], gv, m_ref[...], v_ref[...])

    g_spec = pl.BlockSpec(g_block, lambda *a: g_index(*a[:half_axis]))
    w_spec = pl.BlockSpec(w_block, lambda *a: w_index(*a[:-1]))
    shp = jax.ShapeDtypeStruct(w.shape, F32)
    return pl.pallas_call(
        body, name=name,
        grid_spec=pltpu.PrefetchScalarGridSpec(
            num_scalar_prefetch=1, grid=grid,
            in_specs=[g_spec, g_spec, w_spec, w_spec, w_spec], out_specs=[w_spec] * 4),
        out_shape=[shp] * 4,
        compiler_params=pltpu.CompilerParams(dimension_semantics=("parallel",) * len(grid)),
    )(core, own, other, w, m, v)


def _small_finish(own, others, lam, logits):
    def body(a_ref, b_ref, lam_ref, lg_ref, o_ref):
        g = (((a_ref[...] + b_ref[0]) + (b_ref[1] + b_ref[2]))
             + ((b_ref[3] + b_ref[4]) + (b_ref[5] + b_ref[6])))
        o_ref[...] = g
        o_ref[8:9, :] = g[8:9, :] * (-_sigmoid(-lam_ref[...]))
        lb = 1.0 / (1.0 + jnp.exp(lg_ref[1:2, :] - lg_ref[0:1, :]))
        d0 = g[9:10, :] * (lb * (1.0 - lb))
        o_ref[9:10, :] = d0
        o_ref[10:11, :] = -d0

    return pl.pallas_call(
        body, name="small_finish",
        in_specs=[_full((SMALL_ROWS, D_MODEL)), _full((7, SMALL_ROWS, D_MODEL)), _full((1, D_MODEL)),
                  _full((2, D_MODEL))],
        out_specs=_full((SMALL_ROWS, D_MODEL)),
        out_shape=jax.ShapeDtypeStruct((SMALL_ROWS, D_MODEL), F32),
    )(own, others, lam, logits)


def _adamw_small(ws, gs, ms, vs):
    n = len(ws)

    def body(*refs):
        w_r, g_r, m_r, v_r = refs[:n], refs[n:2 * n], refs[2 * n:3 * n], refs[3 * n:4 * n]
        d_o, m_o, v_o = refs[4 * n:5 * n], refs[5 * n:6 * n], refs[6 * n:7 * n]
        for j in range(n):
            d_o[j][...], m_o[j][...], v_o[j][...] = _adamw(w_r[j][...], g_r[j][...], m_r[j][...], v_r[j][...])

    specs = [_full(w.shape) for w in ws]
    shapes = [jax.ShapeDtypeStruct(w.shape, F32) for w in ws]
    outs = pl.pallas_call(
        body, name="adamw_small",
        in_specs=specs * 4, out_specs=specs * 3, out_shape=shapes * 3,
    )(*ws, *gs, *ms, *vs)
    return outs[:n], outs[n:2 * n], outs[2 * n:]


def _local_step(x, tgt, ut, p, pre_w, wg, cw, cb, wa, wx, ba, bx, lam, logits, gw, wo, post_w, tm, tmm,
                prepare=None):
    h, y2 = _lru_fwd(p, wa, wx, ba, bx, lam, cw, cb, tm)
    y2, o, states = _hgrn_fwd(p, logits, gw, y2, tm)
    dout, dy, sq, d_post = _out_fused(y2, wo, x, tgt, post_w, tmm)
    g_out = _dw_out(y2, dy, min(2048, x.shape[0]))
    dp, d_wa, d_wx, d_ba, d_bx, d_sp, d_cw, d_cb = _lru_bwd(dy, wo, p, h, wa, wx, ba, bx, lam, cw, cb, tm)
    dp, d_gw, d_lb = _hgrn_bwd(dy, wo, p, o, states, logits, gw, dp, tm)
    g_in = _dw_in(ut, dp, min(2048, x.shape[0]))
    parts, scatter = prepare(g_in, g_out, d_wa, d_wx) if prepare is not None else ([], [])
    grad_x, d_pre, received = _dx(dp, wg, x, dout, pre_w, tm, parts, scatter)
    small = jnp.concatenate([d_pre, d_cw, d_cb, d_ba, d_bx, d_sp, d_lb,
                             jnp.zeros((1, D_MODEL), F32), d_gw, d_post, sq,
                             jnp.zeros((SMALL_ROWS - 14, D_MODEL), F32)], axis=0)
    return grad_x, g_in, g_out, d_wa, d_wx, small, parts, received


def kernel(x, pre_norm_w, w_in, conv_w, conv_b, lru_w_a, lru_b_a, lru_w_x, lru_b_x, lru_lambda, hgrn_lb_logits, hgrn_gnorm_w, w_out, post_norm_w, loss_target, m_pre_norm_w, m_w_in, m_conv_w, m_conv_b, m_lru_w_a, m_lru_b_a, m_lru_w_x, m_lru_b_x, m_lru_lambda, m_hgrn_lb_logits, m_hgrn_gnorm_w, m_w_out, m_post_norm_w, v_pre_norm_w, v_w_in, v_conv_w, v_conv_b, v_lru_w_a, v_lru_b_a, v_lru_w_x, v_lru_b_x, v_lru_lambda, v_hgrn_lb_logits, v_hgrn_gnorm_w, v_w_out, v_post_norm_w):
    t = x.shape[1]
    tm = min(256, t)
    tmm = min(512, t)
    chip = 2 * lax.axis_index("x") + lax.axis_index("y")

    win_units = w_in[0].astype(_MXU_DTYPE).reshape(D_MODEL, 3, UNIT).transpose(1, 0, 2)
    small_w = jnp.concatenate([conv_w[0], lru_b_a[0], lru_b_x[0]], axis=1)
    ut, p, wg4, (wo4, wa4, wx4), sw4 = _in_proj_gather(
        x[0], pre_norm_w, win_units,
        [w_out[0].astype(_MXU_DTYPE).reshape(2, 256, D_MODEL), lru_w_a[0].astype(_MXU_DTYPE).reshape(2, 128, LRU_BW),
         lru_w_x[0].astype(_MXU_DTYPE).reshape(2, 128, LRU_BW)],
        small_w, min(1024, t))
    wg = wg4.reshape(N_UNIT, D_MODEL, UNIT)
    wo = wo4.reshape(2 * D_MODEL, D_MODEL)
    by_block = lambda w4: w4.reshape(N_CHIPS, LRU_BLOCKS, 64, LRU_BW).transpose(1, 0, 2, 3).reshape(
        LRU_BLOCKS, LRU_BW, LRU_BW)
    wa, wx = by_block(wa4), by_block(wx4)
    sw4 = sw4.reshape(N_CHIPS, 4, 384)
    cw = sw4[:, :, 0:256].transpose(1, 0, 2).reshape(4, D_MODEL)
    ba = sw4[:, :, 256:320].transpose(1, 0, 2).reshape(1, D_MODEL)
    bx = sw4[:, :, 320:384].transpose(1, 0, 2).reshape(1, D_MODEL)

    def lru_layout(g):
        g = g.reshape(2, 2, N_CHIPS, 64, LRU_BW).transpose(0, 2, 1, 3, 4)
        return g.reshape(2, N_CHIPS * 128, LRU_BW)

    def prepare(g_in, g_out, d_wa, d_wx):
        g_in = g_in.reshape(2, N_UNIT * 512, UNIT)
        g_out = g_out.reshape(2, N_CHIPS * 256, D_MODEL)
        g_wa, g_wx = lru_layout(d_wa), lru_layout(d_wx)
        b_in, b_out, b_wa, b_wx = _core_swap([g_in, g_out, g_wa, g_wx], "reduce_swap", halved=True)
        parts = [_sum_own_half(g_in, b_in, "presum_in", _WIRE_DTYPE).reshape(N_CHIPS, 3 * 512, UNIT),
                 _sum_own_half(g_out, b_out, "presum_out", _WIRE_DTYPE).reshape(N_CHIPS, 256, D_MODEL),
                 _sum_own_half(g_wa, b_wa, "presum_wa", _WIRE_DTYPE).reshape(N_CHIPS, 128, LRU_BW),
                 _sum_own_half(g_wx, b_wx, "presum_wx", _WIRE_DTYPE).reshape(N_CHIPS, 128, LRU_BW)]
        return parts, [True] * 4

    grad_x, _, _, _, _, small, (p_in, p_out, p_wa, p_wx), (r_in, r_out, r_wa, r_wx) = _local_step(
        x[0], loss_target[0], ut, p, pre_norm_w, wg, cw, conv_b, wa, wx, ba, bx, lru_lambda, hgrn_lb_logits,
        hgrn_gnorm_w, wo, post_norm_w, tm, tmm, prepare)
    s_in = _sum_chips(p_in, r_in, True, "sum_in")
    s_out = _sum_chips(p_out, r_out, True, "sum_out")
    s_wa = _sum_chips(p_wa, r_wa, True, "sum_wa")
    s_wx = _sum_chips(p_wx, r_wx, True, "sum_wx")
    (o_in, o_out, o_wa, o_wx), small_others = _finish_exchange([s_in, s_out, s_wa, s_wx], small)

    g_w_in, d_w_in, nm_w_in, nv_w_in = _adamw_big(
        s_in.reshape(3, 512, UNIT), o_in.reshape(3, 512, UNIT), w_in[0], m_w_in[0], v_w_in[0], "adamw_w_in", (3, 2),
        (None, 512, UNIT), lambda k: (k, 0, 0), (512, UNIT), lambda k, h: (h, k))
    g_w_out, d_w_out, nm_w_out, nv_w_out = _adamw_big(
        s_out, o_out, w_out[0], m_w_out[0], v_w_out[0], "adamw_w_out", (2,),
        (256, D_MODEL), lambda: (0, 0), (256, D_MODEL), lambda h: (h, 0))
    sq2 = lambda a: a.reshape(LRU_BW, LRU_BW)
    lru_specs = ((2,), (128, LRU_BW), lambda: (0, 0), (128, LRU_BW), lambda h: (h, 0))
    g_wa, d_wa2, nm_wa, nv_wa = _adamw_big(s_wa, o_wa, sq2(lru_w_a), sq2(m_lru_w_a), sq2(v_lru_w_a), "adamw_wa",
                                           *lru_specs)
    g_wx, d_wx2, nm_wx, nv_wx = _adamw_big(s_wx, o_wx, sq2(lru_w_x), sq2(m_lru_w_x), sq2(v_lru_w_x), "adamw_wx",
                                           *lru_specs)

    gs = _small_finish(small, small_others, lru_lambda, hgrn_lb_logits)
    loss = jnp.sum(gs[13]) * (0.5 / D_MODEL)
    g_pre = gs[0:1]
    g_cw = lax.dynamic_slice(gs[1:5], (0, chip * 256), (4, 256))
    g_cb = gs[5:6]
    g_ba = lax.dynamic_slice(gs[6].reshape(LRU_BLOCKS, N_CHIPS, 64), (0, chip, 0), (LRU_BLOCKS, 1, 64)).reshape(4, 64)
    g_bx = lax.dynamic_slice(gs[7].reshape(LRU_BLOCKS, N_CHIPS, 64), (0, chip, 0), (LRU_BLOCKS, 1, 64)).reshape(4, 64)
    g_lam = gs[8:9]
    g_lb = gs[9:11]
    g_gw = gs[11:12]
    g_post = gs[12:13]
    small_g = [g_pre, g_cw, g_cb, g_ba, g_bx, g_lam, g_lb, g_gw, g_post]
    two_d = lambda a: a.reshape(a.shape[-2:])
    small_w_list = [pre_norm_w, conv_w, conv_b, lru_b_a, lru_b_x, lru_lambda, hgrn_lb_logits, hgrn_gnorm_w, post_norm_w]
    small_m_list = [m_pre_norm_w, m_conv_w, m_conv_b, m_lru_b_a, m_lru_b_x, m_lru_lambda, m_hgrn_lb_logits,
                    m_hgrn_gnorm_w, m_post_norm_w]
    small_v_list = [v_pre_norm_w, v_conv_w, v_conv_b, v_lru_b_a, v_lru_b_x, v_lru_lambda, v_hgrn_lb_logits,
                    v_hgrn_gnorm_w, v_post_norm_w]
    sd, sm, sv = _adamw_small([two_d(a) for a in small_w_list], small_g,
                              [two_d(a) for a in small_m_list], [two_d(a) for a in small_v_list])

    def shaped(vals, refs):
        return [val.reshape(ref.shape) for val, ref in zip(vals, refs)]

    s_g = shaped(small_g, small_w_list)
    s_d = shaped(sd, small_w_list)
    s_m = shaped(sm, small_w_list)
    s_v = shaped(sv, small_w_list)

    def ordered(small, big_in, big_wa, big_wx, big_out):
        pre, cw_, cb_, ba_, bx_, lam_, lb_, gw_, post_ = small
        return [pre, big_in.reshape(w_in.shape), cw_, cb_, big_wa.reshape(lru_w_a.shape), ba_,
                big_wx.reshape(lru_w_x.shape), bx_, lam_, lb_, gw_, big_out.reshape(w_out.shape), post_]

    grads = ordered(s_g, g_w_in, g_wa, g_wx, g_w_out)
    deltas = ordered(s_d, d_w_in, d_wa2, d_wx2, d_w_out)
    new_m = ordered(s_m, nm_w_in, nm_wa, nm_wx, nm_w_out)
    new_v = ordered(s_v, nv_w_in, nv_wa, nv_wx, nv_w_out)
    return (loss, grad_x.reshape(x.shape), *grads, *deltas, *new_m, *new_v)
```

```python
import functools

import jax
import jax.numpy as jnp
from jax import lax
from jax.experimental import pallas as pl
from jax.experimental.pallas import tpu as pltpu

F32 = jnp.float32
BF16 = jnp.bfloat16
_MXU_DTYPE = jnp.bfloat16
_WIRE_DTYPE = jnp.bfloat16

D_MODEL = 1024
N_SEG = 6
UNIT = 512
N_UNIT = 12
LRU_BLOCKS = 4
LRU_BW = 256
LRU_C = 8.0
HEADS = 8
HEAD_D = 128
CHUNK = 64
EPS = 1e-6
N_CHIPS = 4
EXP_CLAMP = 80.0

ADAM_LR = 0.001
ADAM_B1 = 0.9
ADAM_B2 = 0.999
ADAM_EPS = 1e-08
ADAM_WD = 0.01
ADAM_STEP = 10

SMALL_ROWS = 16
FWD_PIECES = 4
MESH = pl.DeviceIdType.MESH


def _sigmoid(x):
    return 0.5 * jnp.tanh(0.5 * x) + 0.5


def _mx(x):
    return x.astype(_MXU_DTYPE)


def _dot(a, b):
    return jnp.dot(_mx(a), _mx(b), preferred_element_type=F32)


def _dot_nt(a, b):
    return lax.dot_general(_mx(a), _mx(b), (((1,), (1,)), ((), ())), preferred_element_type=F32)


def _dot_tn(a, b):
    return lax.dot_general(_mx(a), _mx(b), (((0,), (0,)), ((), ())), preferred_element_type=F32)


def _rows(shape):
    return lax.broadcasted_iota(jnp.int32, shape, 0)


def _cols(shape):
    return lax.broadcasted_iota(jnp.int32, shape, 1)


def _softplus_neg(lam):
    z = -lam
    e = jnp.exp(-jnp.abs(z))
    series = e * (1.0 - e * (0.5 - e * (1.0 / 3.0 - 0.25 * e)))
    return jnp.maximum(z, 0.0) + jnp.where(e < 1e-2, series, jnp.log(1.0 + e))


def _one_minus_sq(a, y):
    series = -y * (1.0 + y * (0.5 + y * (1.0 / 6.0)))
    return jnp.where(y > -0.01, series, 1.0 - a * a)


def _cumsum_rows(x, reverse=False):
    n = x.shape[0] // 8
    row8 = _rows((8, x.shape[1]))
    out = [None] * n
    carry = None
    for g in (reversed(range(n)) if reverse else range(n)):
        blk = x[8 * g:8 * g + 8]
        for s in (1, 2, 4):
            if reverse:
                blk = blk + jnp.where(row8 < 8 - s, pltpu.roll(blk, 8 - s, 0), 0.0)
            else:
                blk = blk + jnp.where(row8 >= s, pltpu.roll(blk, s, 0), 0.0)
        if carry is not None:
            blk = blk + carry
        carry = blk[0:1] if reverse else blk[7:8]
        out[g] = blk
    return jnp.concatenate(out, axis=0)


def _shift_down(x, halo, s):
    if s == 0:
        return x
    r = pltpu.roll(x, s, 0)
    top = jnp.where(_rows(halo.shape) < s, pltpu.roll(halo, s, 0), r[0:8])
    return jnp.concatenate([top, r[8:]], axis=0)


def _shift_up(x, head, s):
    if s == 0:
        return x
    n = x.shape[0]
    r = pltpu.roll(x, n - s, 0)
    bottom = jnp.where(_rows(head.shape) >= 8 - s, pltpu.roll(head, 8 - s, 0), r[n - 8:n])
    return jnp.concatenate([r[:n - 8], bottom], axis=0)


def _lru_gates(lx, halo, cw, cb, wa, wx, ba, bx, sp):
    xs = [_shift_down(lx, halo, 3 - k) for k in range(4)]
    xc = cb + cw[0:1] * xs[0] + cw[1:2] * xs[1] + cw[2:3] * xs[2] + cw[3:4] * xs[3]
    zr, zi = [], []
    for n in range(LRU_BLOCKS):
        xb = _mx(xc[:, n * LRU_BW:(n + 1) * LRU_BW])
        zr.append(jnp.dot(xb, wa[n], preferred_element_type=F32))
        zi.append(jnp.dot(xb, wx[n], preferred_element_type=F32))
    r = _sigmoid(jnp.concatenate(zr, axis=1) + ba)
    ig = _sigmoid(jnp.concatenate(zi, axis=1) + bx)
    la = (-LRU_C * sp) * r
    a = jnp.exp(la)
    om = _one_minus_sq(a, 2.0 * la)
    return xs, xc, r, ig, a, om


def _hgrn_prep(q, fr, lb):
    sig = _sigmoid(fr)
    f = lb + (1.0 - lb) * sig
    sq = _sigmoid(q)
    b = _cumsum_rows(jnp.log(f))
    bm = b[CHUNK // 2 - 1:CHUNK // 2]
    bc = b[CHUNK - 1:CHUNK]
    em1 = jnp.exp(jnp.minimum(b - bm, EXP_CLAMP))
    em2 = jnp.exp(jnp.minimum(bm - b, EXP_CLAMP))
    qm = _mx((q * sq) * em1)
    km = _mx((1.0 - f) * em2)
    return sig, f, sq, em1, em2, qm, km, jnp.exp(bm), jnp.exp(bc - bm), jnp.exp(bc)


def _seg(ref, s, rows=slice(None)):
    return jnp.concatenate([ref[2 * s, rows, :], ref[2 * s + 1, rows, :]], axis=1)


def _pad_lanes(parts, width):
    have = sum(p.shape[1] for p in parts)
    pad = [jnp.zeros((parts[0].shape[0], width - have), parts[0].dtype)] if width > have else []
    return jnp.concatenate(list(parts) + pad, axis=1)


def _pad_rows(parts, height):
    have = sum(p.shape[0] for p in parts)
    pad = [jnp.zeros((height - have, parts[0].shape[1]), parts[0].dtype)] if height > have else []
    return jnp.concatenate(list(parts) + pad, axis=0)


def _adamw(w, g, m, v):
    m = ADAM_B1 * m + (1.0 - ADAM_B1) * g
    v = ADAM_B2 * v + (1.0 - ADAM_B2) * jnp.square(g)
    m_hat = m / (1.0 - ADAM_B1 ** ADAM_STEP)
    v_hat = v / (1.0 - ADAM_B2 ** ADAM_STEP)
    delta = -ADAM_LR * (m_hat / (jnp.sqrt(v_hat) + ADAM_EPS) + ADAM_WD * w)
    return delta, m, v


def _full(shape):
    nd = len(shape)
    return pl.BlockSpec(shape, lambda *_: (0,) * nd)


def _seq_params():
    return pltpu.CompilerParams(dimension_semantics=("arbitrary",))


def _resident(shape):
    nd = len(shape)
    return pl.BlockSpec(shape, lambda *_: (0,) * nd, pipeline_mode=pl.Buffered(1))


def _in_proj_gather(x, w, win_units, halved, small_w, tm):
    t = x.shape[0]
    nt = t // tm
    n_h = len(halved)
    n_split = 3 + n_h
    n_items = n_split + 1
    hu = D_MODEL // 2
    sub = min(256, tm)
    chip = 2 * lax.axis_index("x") + lax.axis_index("y")
    order = jnp.stack([chip] + [jnp.bitwise_xor(chip, flip) for flip in (2, 1, 3)])

    def body(order_ref, x_ref, w_ref, win_ref, *rest):
        del order_ref
        h_ins, small_ref = rest[:n_h], rest[n_h]
        ut_ref, p_ref, wg_ref = rest[n_h + 1:n_h + 4]
        h_outs, sw_ref = rest[n_h + 4:2 * n_h + 4], rest[2 * n_h + 4]
        u_s, wbuf, send, recv, fsend, frecv, osend, orecv, wsem = rest[2 * n_h + 5:]
        g, i = pl.program_id(0), pl.program_id(1)
        mx, my, c = lax.axis_index("x"), lax.axis_index("y"), lax.axis_index("c")
        me = 2 * mx + my
        peers = [(1 - mx, my), (mx, 1 - my), (1 - mx, 1 - my)]
        qs = [2 * px + py for px, py in peers]
        sibling = (mx, my, 1 - c)

        def remote(src, dst, ssem, rsem, dev):
            return pltpu.make_async_remote_copy(src_ref=src, dst_ref=dst, send_sem=ssem, recv_sem=rsem,
                                                device_id=dev, device_id_type=MESH)

        def mine(item):
            if item < 3:
                return win_ref.at[item, pl.ds(c * hu, hu)]
            return h_ins[item - 3].at[c] if item < n_split else small_ref

        def piece(item, q, h):
            if item < 3:
                return wg_ref.at[q, item, pl.ds(h * hu, hu)]
            return h_outs[item - 3].at[q, h] if item < n_split else sw_ref.at[q]

        def fetch(item, jj, q):
            px, py = peers[jj]
            return remote(mine(item), piece(item, q, c), send.at[item * 3 + jj], recv.at[item * 3 + jj], (px, py, c))

        def forward(item, jj, h):
            whole = piece(item, qs[jj], h)
            step = whole.shape[0] // FWD_PIECES
            copies = []
            for k in range(FWD_PIECES):
                part = whole.at[pl.ds(k * step, step)]
                sem = (item * 3 + jj) * FWD_PIECES + k
                copies.append(remote(part, part, fsend.at[sem], frecv.at[sem], sibling))
            return copies

        def own(item):
            if item < 3:
                src, dst = win_ref.at[item], wg_ref.at[me, item]
            elif item < n_split:
                src, dst = h_ins[item - 3], h_outs[item - 3].at[me]
            else:
                src, dst = small_ref, sw_ref.at[me]
            return remote(src, dst, osend.at[item], orecv.at[item], sibling)

        def load_units(gg):
            src = lambda k: win_ref.at[k] if gg == 0 else wg_ref.at[qs[gg - 1], k]
            return [pltpu.make_async_copy(src(k), wbuf.at[gg % 2, k], wsem.at[(gg % 2) * 3 + k]) for k in range(3)]

        def in_pass(gg):
            return pl.when(g == gg)

        def pass_on(gg):
            for item in range(3):
                fetch(item, gg, qs[gg]).wait_recv()
                for cp in forward(item, gg, c):
                    cp.start()

        @pl.when(i == 0)
        def _():
            @in_pass(0)
            def _():
                for item in range(3):
                    for jj in range(2):
                        fetch(item, jj, me).start()
                for item in range(n_items):
                    own(item).start()
                for cp in load_units(0):
                    cp.start()

            @in_pass(1)
            def _():
                for item in range(3):
                    fetch(item, 2, me).start()

            @in_pass(2)
            def _():
                for item in range(3, n_items):
                    for jj in range(3):
                        fetch(item, jj, me).start()

            for gg in range(N_CHIPS):
                @in_pass(gg)
                def _(gg=gg):
                    for cp in load_units(gg):
                        cp.wait()

        @pl.when(g == 0)
        def _():
            xv = x_ref[...]
            rstd = lax.rsqrt(jnp.mean(xv * xv, axis=-1, keepdims=True) + EPS)
            u = xv * rstd * w_ref[...]
            ut_ref[...] = u.T.astype(_MXU_DTYPE)
            u_s[pl.ds(pl.multiple_of(i * tm, tm), tm), :] = _mx(u)

        slot = g % 2
        for r in range(tm // sub):
            ur = u_s[pl.ds(pl.multiple_of(i * tm + r * sub, sub), sub), :]
            for k in range(3):
                p_ref[k, r * sub:(r + 1) * sub, :] = jnp.dot(ur, wbuf[slot, k], preferred_element_type=F32)

        @pl.when(i == nt // 2)
        def _():
            for gg in (1, 2):
                in_pass(gg)(functools.partial(pass_on, gg))

        @pl.when(i == nt - 1)
        def _():
            in_pass(0)(functools.partial(pass_on, 0))
            for gg in range(3):
                @in_pass(gg)
                def _(gg=gg):
                    for item in range(3):
                        for cp in forward(item, gg, 1 - c):
                            cp.wait_recv()
                    for cp in load_units(gg + 1):
                        cp.start()

            @in_pass(N_CHIPS - 1)
            def _():
                for item in range(3, n_items):
                    for jj in range(3):
                        fetch(item, jj, qs[jj]).wait_recv()
                        if item < n_split:
                            for cp in forward(item, jj, c):
                                cp.start()
                for item in range(3, n_split):
                    for jj in range(3):
                        for cp in forward(item, jj, 1 - c):
                            cp.wait_recv()
                for item in range(n_items):
                    own(item).wait_recv()
                for item in range(n_items):
                    for jj in range(3):
                        fetch(item, jj, me).wait_send()
                        if item < n_split:
                            for cp in forward(item, jj, c):
                                cp.wait_send()
                    own(item).wait_send()

    any_spec = pl.BlockSpec(memory_space=pl.ANY)
    first_pass = lambda g, i: jnp.where(g == 0, i, nt - 1)
    res = pl.pallas_call(
        body, name="in_proj",
        grid_spec=pltpu.PrefetchScalarGridSpec(
            num_scalar_prefetch=1, grid=(N_CHIPS, nt),
            in_specs=[pl.BlockSpec((tm, D_MODEL), lambda g, i, o: (first_pass(g, i), 0)),
                      pl.BlockSpec((1, D_MODEL), lambda g, i, o: (0, 0))] + [any_spec] * (n_h + 2),
            out_specs=[pl.BlockSpec((D_MODEL, tm), lambda g, i, o: (0, first_pass(g, i))),
                       pl.BlockSpec((3, tm, UNIT), lambda g, i, o: (o[g], i, 0))]
            + [any_spec] * (n_h + 2),
            scratch_shapes=[pltpu.VMEM((t, D_MODEL), _MXU_DTYPE), pltpu.VMEM((2, 3, D_MODEL, UNIT), _MXU_DTYPE),
                            pltpu.SemaphoreType.DMA((3 * n_items,)), pltpu.SemaphoreType.DMA((3 * n_items,)),
                            pltpu.SemaphoreType.DMA((3 * n_split * FWD_PIECES,)),
                            pltpu.SemaphoreType.DMA((3 * n_split * FWD_PIECES,)),
                            pltpu.SemaphoreType.DMA((n_items,)), pltpu.SemaphoreType.DMA((n_items,)),
                            pltpu.SemaphoreType.DMA((6,))]),
        out_shape=[jax.ShapeDtypeStruct((D_MODEL, t), _MXU_DTYPE), jax.ShapeDtypeStruct((N_UNIT, t, UNIT), F32),
                   jax.ShapeDtypeStruct((N_CHIPS,) + win_units.shape, win_units.dtype)]
        + [jax.ShapeDtypeStruct((N_CHIPS,) + a.shape, a.dtype) for a in halved]
        + [jax.ShapeDtypeStruct((N_CHIPS,) + small_w.shape, small_w.dtype)],
        compiler_params=pltpu.CompilerParams(dimension_semantics=("arbitrary", "arbitrary")),
    )(order, x, w, win_units, *halved, small_w)
    return res[0], res[1], res[2], res[3:3 + n_h], res[3 + n_h]


def _lru_fwd(p, wa, wx, ba, bx, lam, cw, cb, tm):
    t = p.shape[1]
    ng = tm // 8

    def body(lx_ref, halo_ref, lg_ref, wa_ref, wx_ref, ba_ref, bx_ref, lam_ref, cw_ref, cb_ref,
             h_ref, y_ref, a_s, u_s, hc_s):
        i = pl.program_id(0)

        @pl.when(i == 0)
        def _():
            hc_s[...] = jnp.zeros_like(hc_s)

        sp = _softplus_neg(lam_ref[...])
        halo = jnp.where(i == 0, 0.0, _seg(halo_ref, 0))
        _, xc, _, ig, a, om = _lru_gates(_seg(lx_ref, 0), halo, cw_ref[...], cb_ref[...], wa_ref[...],
                                         wx_ref[...], ba_ref[...], bx_ref[...], sp)
        a_s[...] = a
        u_s[...] = jnp.sqrt(om) * (ig * xc)
        row8 = _rows((8, D_MODEL))

        def group(g, hc):
            rows = pl.ds(pl.multiple_of(g * 8, 8), 8)
            av = a_s[rows, :]
            uv = u_s[rows, :]
            for s in (1, 2, 4):
                a_sh = jnp.where(row8 >= s, pltpu.roll(av, s, 0), 1.0)
                u_sh = jnp.where(row8 >= s, pltpu.roll(uv, s, 0), 0.0)
                uv = av * u_sh + uv
                av = av * a_sh
            hh = av * hc + uv
            h_ref[rows, :] = hh
            return jnp.broadcast_to(hh[7:8, :], (8, D_MODEL))

        hc_s[...] = lax.fori_loop(0, ng, group, hc_s[...])
        lg = _seg(lg_ref, 0)
        y_ref[...] = (h_ref[...] * (lg * _sigmoid(lg))).astype(_MXU_DTYPE)

    seg = lambda s: pl.BlockSpec((2, tm, UNIT), lambda i: (s, i, 0))
    return pl.pallas_call(
        body, name="lru_fwd", grid=(t // tm,),
        in_specs=[seg(0),
                  pl.BlockSpec((2, 8, UNIT), lambda i: (0, jnp.maximum(i * (tm // 8) - 1, 0), 0)),
                  seg(1),
                  _full((LRU_BLOCKS, LRU_BW, LRU_BW)), _full((LRU_BLOCKS, LRU_BW, LRU_BW)),
                  _full((1, D_MODEL)), _full((1, D_MODEL)), _full((1, D_MODEL)),
                  _full((4, D_MODEL)), _full((1, D_MODEL))],
        out_specs=[pl.BlockSpec((tm, D_MODEL), lambda i: (i, 0)),
                   pl.BlockSpec((None, tm, D_MODEL), lambda i: (0, i, 0))],
        out_shape=[jax.ShapeDtypeStruct((t, D_MODEL), F32),
                   jax.ShapeDtypeStruct((2, t, D_MODEL), _MXU_DTYPE)],
        scratch_shapes=[pltpu.VMEM((tm, D_MODEL), F32), pltpu.VMEM((tm, D_MODEL), F32),
                        pltpu.VMEM((8, D_MODEL), F32)],
        compiler_params=_seq_params(),
    )(p, p, p, wa, wx, ba, bx, lam, cw, cb)


def _hgrn_fwd(p, logits, gw, y2, tm):
    t = p.shape[1]
    nc = tm // CHUNK

    def body(qf_ref, vh_ref, lg_ref, gw_ref, y_in, y_ref, o_ref, st_ref, state):
        del y_in
        i = pl.program_id(0)

        @pl.when(i == 0)
        def _():
            state[...] = jnp.zeros_like(state)

        l0 = lg_ref[0:1, :]
        l1 = lg_ref[1:2, :]
        lb = 1.0 / (1.0 + jnp.exp(l1 - l0))
        gwv = gw_ref[...]
        causal = _rows((CHUNK, CHUNK)) >= _cols((CHUNK, CHUNK))

        for c in range(nc):
            rows = pl.ds(c * CHUNK, CHUNK)
            v = _seg(vh_ref, 0, rows)
            hg = _seg(vh_ref, 1, rows)
            _, _, _, _, _, qm, km, ebm, ecm, ebc = _hgrn_prep(_seg(qf_ref, 0, rows), _seg(qf_ref, 1, rows), lb)
            ke = _mx(km.astype(F32) * ecm)
            gate = gwv * (hg * _sigmoid(hg))
            o_parts, y_parts = [], []
            for h in range(HEADS):
                ln = slice(h * HEAD_D, (h + 1) * HEAD_D)
                s0 = state[h]
                st_ref[c, h] = s0
                vt = _mx(v[:, ln].T)
                att = _mx(jnp.where(causal, _dot_nt(qm[:, ln], km[:, ln]), 0.0))
                o = _dot_nt(_pad_lanes([qm[:, ln], att], 2 * HEAD_D),
                            _pad_lanes([_mx(s0 * ebm[:, ln]), vt], 2 * HEAD_D))
                state[h] = s0 * ebc[:, ln] + jnp.dot(vt, ke[:, ln], preferred_element_type=F32)
                rstd = lax.rsqrt(jnp.mean(o * o, axis=-1, keepdims=True) + EPS)
                o_parts.append(o)
                y_parts.append((o * rstd * gate[:, ln]).astype(_MXU_DTYPE))
            o_ref[rows, :] = jnp.concatenate(o_parts, axis=1)
            y_ref[rows, :] = jnp.concatenate(y_parts, axis=1)

    pair = lambda s: pl.BlockSpec((4, tm, UNIT), lambda i: (s, i, 0))
    return pl.pallas_call(
        body, name="hgrn_fwd", grid=(t // tm,),
        in_specs=[pair(1), pair(2), _full((2, D_MODEL)), _full((1, D_MODEL)),
                  pl.BlockSpec(memory_space=pl.ANY)],
        out_specs=[pl.BlockSpec((None, tm, D_MODEL), lambda i: (1, i, 0)),
                   pl.BlockSpec((tm, D_MODEL), lambda i: (i, 0)),
                   pl.BlockSpec((nc, HEADS, HEAD_D, HEAD_D), lambda i: (i, 0, 0, 0))],
        out_shape=[jax.ShapeDtypeStruct((2, t, D_MODEL), _MXU_DTYPE),
                   jax.ShapeDtypeStruct((t, D_MODEL), F32),
                   jax.ShapeDtypeStruct((t // CHUNK, HEADS, HEAD_D, HEAD_D), F32)],
        scratch_shapes=[pltpu.VMEM((HEADS, HEAD_D, HEAD_D), F32)],
        input_output_aliases={4: 0},
        compiler_params=_seq_params(),
    )(p, p, logits, gw, y2)


def _out_fused(y2, wo, x, tgt, wpost, tm):
    t = x.shape[0]
    inv_d = 1.0 / D_MODEL

    def body(y_ref, wo_ref, x_ref, t_ref, w_ref, dout_ref, dy_ref, sq_ref, dw_ref):
        i = pl.program_id(0)

        @pl.when(i == 0)
        def _():
            sq_ref[...] = jnp.zeros_like(sq_ref)
            dw_ref[...] = jnp.zeros_like(dw_ref)

        y = (jnp.dot(y_ref[0], wo_ref[0:D_MODEL, :], preferred_element_type=F32)
             + jnp.dot(y_ref[1], wo_ref[D_MODEL:2 * D_MODEL, :], preferred_element_type=F32))
        w = w_ref[...]
        rstd = lax.rsqrt(jnp.mean(y * y, axis=-1, keepdims=True) + EPS)
        n = y * rstd
        err = (x_ref[...] + n * w) - t_ref[...]
        sq_ref[...] += jnp.sum(err * err, axis=0, keepdims=True)
        dout = err * inv_d
        dout_ref[...] = dout
        dw_ref[...] += jnp.sum(dout * n, axis=0, keepdims=True)
        dn = dout * w
        dy_ref[...] = (rstd * (dn - n * jnp.mean(dn * n, axis=-1, keepdims=True))).astype(_MXU_DTYPE)

    row = pl.BlockSpec((tm, D_MODEL), lambda i: (i, 0))
    return pl.pallas_call(
        body, name="out_fused", grid=(t // tm,),
        in_specs=[pl.BlockSpec((2, tm, D_MODEL), lambda i: (0, i, 0)), _full((2 * D_MODEL, D_MODEL)),
                  row, row, _full((1, D_MODEL))],
        out_specs=[row, row, _full((1, D_MODEL)), _full((1, D_MODEL))],
        out_shape=[jax.ShapeDtypeStruct((t, D_MODEL), F32), jax.ShapeDtypeStruct((t, D_MODEL), _MXU_DTYPE),
                   jax.ShapeDtypeStruct((1, D_MODEL), F32), jax.ShapeDtypeStruct((1, D_MODEL), F32)],
        compiler_params=_seq_params(),
    )(y2, wo, x, tgt, wpost)


def _dw_out(y2, dy, tt):
    t = dy.shape[0]
    hr = UNIT // 2

    def body(y_ref, dy_ref, g_ref):
        @pl.when(pl.program_id(1) == 0)
        def _():
            g_ref[...] = jnp.zeros_like(g_ref)

        r = _dot_tn(y_ref[...], dy_ref[...])
        g_ref[0] += r[0:hr]
        g_ref[1] += r[hr:UNIT]

    return pl.pallas_call(
        body, name="dw_out", grid=(N_CHIPS, t // tt),
        in_specs=[pl.BlockSpec((None, tt, UNIT), lambda c, k: (c // 2, k, c % 2)),
                  pl.BlockSpec((tt, D_MODEL), lambda c, k: (k, 0))],
        out_specs=pl.BlockSpec((2, None, hr, D_MODEL), lambda c, k: (0, c, 0, 0)),
        out_shape=jax.ShapeDtypeStruct((2, N_CHIPS, hr, D_MODEL), F32),
        compiler_params=pltpu.CompilerParams(dimension_semantics=("arbitrary", "arbitrary")),
    )(y2, dy)


def _lru_bwd(dy, wo, p, h, wa, wx, ba, bx, lam, cw, cb, tm):
    t = dy.shape[0]
    nt = t // tm
    ng = tm // 8

    def body(dy_ref, wo_ref, lx_ref, halo_ref, lg_ref, h_ref, hhalo_ref, wa_ref, wx_ref, ba_ref, bx_ref,
             lam_ref, cw_ref, cb_ref,
             dp_ref, dwa_ref, dwx_ref, dba_ref, dbx_ref, dsp_ref, dcw_ref, dcb_ref,
             c_s, in_s, dh_s, dhc_s, afirst_s, dxc_s):
        i = pl.program_id(0)
        first = i == nt - 1

        @pl.when(i == 0)
        def _():
            for ref in (dwa_ref, dwx_ref, dba_ref, dbx_ref, dsp_ref, dcw_ref, dcb_ref, dhc_s, afirst_s, dxc_s):
                ref[...] = jnp.zeros_like(ref)

        sp = _softplus_neg(lam_ref[...])
        cwv = cw_ref[...]
        wav = wa_ref[...]
        wxv = wx_ref[...]
        halo = jnp.where(first, 0.0, _seg(halo_ref, 0))
        xs, xc, r, ig, a, om = _lru_gates(_seg(lx_ref, 0), halo, cwv, cb_ref[...], wav, wxv,
                                          ba_ref[...], bx_ref[...], sp)
        inv_mult = lax.rsqrt(om)
        mult = om * inv_mult
        d_y = _dot_nt(dy_ref[...], wo_ref[...])
        lg = _seg(lg_ref, 0)
        sl = _sigmoid(lg)
        hv = h_ref[...]
        d_lg = d_y * hv * (sl * (1.0 + lg * (1.0 - sl)))

        c_s[...] = _shift_up(a, afirst_s[...], 1)
        in_s[...] = d_y * (lg * sl)
        row8 = _rows((8, D_MODEL))

        def group(gg, carry):
            rows = pl.ds(pl.multiple_of((ng - 1 - gg) * 8, 8), 8)
            cv = c_s[rows, :]
            uv = in_s[rows, :]
            for s in (1, 2, 4):
                c_sh = jnp.where(row8 < 8 - s, pltpu.roll(cv, 8 - s, 0), 1.0)
                u_sh = jnp.where(row8 < 8 - s, pltpu.roll(uv, 8 - s, 0), 0.0)
                uv = cv * u_sh + uv
                cv = cv * c_sh
            hh = cv * carry + uv
            dh_s[rows, :] = hh
            return jnp.broadcast_to(hh[0:1, :], (8, D_MODEL))

        dhc_s[...] = lax.fori_loop(0, ng, group, dhc_s[...])
        afirst_s[...] = a[0:8]
        dh = dh_s[...]

        h_prev = _shift_down(hv, jnp.where(first, 0.0, hhalo_ref[...]), 1)
        gx = ig * xc
        d_mult = dh * gx
        d_i = dh * (mult * xc)
        d_xc = dh * (mult * ig)
        d_la = (dh * h_prev) * a - d_mult * ((a * a) * inv_mult)
        dsp_ref[...] += jnp.sum(d_la * r, axis=0, keepdims=True) * (-LRU_C)
        d_zr = (d_la * (-LRU_C * sp)) * (r * (1.0 - r))
        d_zi = d_i * (ig * (1.0 - ig))
        dba_ref[...] += jnp.sum(d_zr, axis=0, keepdims=True)
        dbx_ref[...] += jnp.sum(d_zi, axis=0, keepdims=True)
        back = []
        for n in range(LRU_BLOCKS):
            ln = slice(n * LRU_BW, (n + 1) * LRU_BW)
            xb = _mx(xc[:, ln])
            zr_n = _mx(d_zr[:, ln])
            zi_n = _mx(d_zi[:, ln])
            dwa_ref[n] += _dot_tn(xb, zr_n)
            dwx_ref[n] += _dot_tn(xb, zi_n)
            back.append(_dot_nt(zr_n, wav[n]) + _dot_nt(zi_n, wxv[n]))
        d_xc = d_xc + jnp.concatenate(back, axis=1)
        dcb_ref[...] += jnp.sum(d_xc, axis=0, keepdims=True)
        for k in range(4):
            dcw_ref[k:k + 1, :] += jnp.sum(d_xc * xs[k], axis=0, keepdims=True)
        head = dxc_s[...]
        d_lx = cwv[3:4] * d_xc
        for k in range(3):
            d_lx = d_lx + cwv[k:k + 1] * _shift_up(d_xc, head, 3 - k)
        dxc_s[...] = d_xc[0:8]
        dp_ref[0] = d_lx.astype(_MXU_DTYPE)
        dp_ref[1] = d_lg.astype(_MXU_DTYPE)

    rev = lambda i: nt - 1 - i
    seg = lambda s: pl.BlockSpec((2, tm, UNIT), lambda i: (s, rev(i), 0))
    halo_row = lambda i: jnp.maximum(rev(i) * (tm // 8) - 1, 0)
    row = pl.BlockSpec((tm, D_MODEL), lambda i: (rev(i), 0))
    vec = _full((1, D_MODEL))
    wblk = _full((LRU_BLOCKS, LRU_BW, LRU_BW))
    return pl.pallas_call(
        body, name="lru_bwd", grid=(nt,),
        in_specs=[row, pl.BlockSpec((D_MODEL, D_MODEL), lambda i: (0, 0)),
                  seg(0), pl.BlockSpec((2, 8, UNIT), lambda i: (0, halo_row(i), 0)), seg(1),
                  row, pl.BlockSpec((8, D_MODEL), lambda i: (halo_row(i), 0)),
                  wblk, wblk, vec, vec, vec, _full((4, D_MODEL)), vec],
        out_specs=[pl.BlockSpec((2, tm, D_MODEL), lambda i: (2, rev(i), 0)),
                   wblk, wblk, vec, vec, vec, _full((4, D_MODEL)), vec],
        out_shape=[jax.ShapeDtypeStruct((N_SEG, t, D_MODEL), _MXU_DTYPE),
                   jax.ShapeDtypeStruct((LRU_BLOCKS, LRU_BW, LRU_BW), F32),
                   jax.ShapeDtypeStruct((LRU_BLOCKS, LRU_BW, LRU_BW), F32),
                   jax.ShapeDtypeStruct((1, D_MODEL), F32), jax.ShapeDtypeStruct((1, D_MODEL), F32),
                   jax.ShapeDtypeStruct((1, D_MODEL), F32), jax.ShapeDtypeStruct((4, D_MODEL), F32),
                   jax.ShapeDtypeStruct((1, D_MODEL), F32)],
        scratch_shapes=[pltpu.VMEM((tm, D_MODEL), F32), pltpu.VMEM((tm, D_MODEL), F32),
                        pltpu.VMEM((tm, D_MODEL), F32), pltpu.VMEM((8, D_MODEL), F32),
                        pltpu.VMEM((8, D_MODEL), F32), pltpu.VMEM((8, D_MODEL), F32)],
        compiler_params=_seq_params(),
    )(dy, wo, p, p, p, h, h, wa, wx, ba, bx, lam, cw, cb)


def _hgrn_bwd(dy, wo, p, o, states, logits, gw, dp, tm):
    t = dy.shape[0]
    nt = t // tm
    nc = tm // CHUNK

    def body(dy_ref, wo_ref, qf_ref, vh_ref, o_ref, st_ref, lg_ref, gw_ref, dp_in,
             dp_ref, dgw_ref, dlb_ref, dyh_s, dstate, snext):
        del dp_in
        i = pl.program_id(0)

        @pl.when(i == 0)
        def _():
            for ref in (dgw_ref, dlb_ref, dstate, snext):
                ref[...] = jnp.zeros_like(ref)

        dyh_s[...] = _dot_nt(dy_ref[...], wo_ref[...])
        l0 = lg_ref[0:1, :]
        l1 = lg_ref[1:2, :]
        lb = 1.0 / (1.0 + jnp.exp(l1 - l0))
        gwv = gw_ref[...]
        causal = _rows((CHUNK, CHUNK)) >= _cols((CHUNK, CHUNK))
        wide = 2 * HEAD_D
        zeros_h = jnp.zeros((CHUNK, HEAD_D), _MXU_DTYPE)

        for c in reversed(range(nc)):
            rows = pl.ds(c * CHUNK, CHUNK)
            q = _seg(qf_ref, 0, rows)
            hg = _seg(vh_ref, 1, rows)
            ov = o_ref[rows, :]
            dyh = dyh_s[rows, :]
            sig, f, sq, em1, em2, qm, km, ebm, ecm, ebc = _hgrn_prep(q, _seg(qf_ref, 1, rows), lb)
            ke = _mx(km.astype(F32) * ecm)
            vb = _mx(_seg(vh_ref, 0, rows))
            sg = _sigmoid(hg)
            d_on = dyh * (hg * sg)
            x_p, y_p, dv_p, n_p, ex_p = [], [], [], [], []
            for h in range(HEADS):
                ln = slice(h * HEAD_D, (h + 1) * HEAD_D)
                o_h = ov[:, ln]
                rstd = lax.rsqrt(jnp.mean(o_h * o_h, axis=-1, keepdims=True) + EPS)
                n = o_h * rstd
                dn = d_on[:, ln] * gwv[:, ln]
                d_o = rstd * (dn - n * jnp.mean(dn * n, axis=-1, keepdims=True))
                d_ob = _mx(d_o)
                d_ot = _mx(d_o.T)
                s0 = st_ref[c, h]
                ds = dstate[h]
                both = _dot_nt(jnp.concatenate([_pad_lanes([qm[:, ln]], wide), _pad_lanes([zeros_h, d_ob], wide)], axis=0),
                               jnp.concatenate([km[:, ln], vb[:, ln]], axis=1))
                att = jnp.where(causal, both[0:CHUNK], 0.0)
                d_att = jnp.where(causal, both[CHUNK:2 * CHUNK], 0.0)
                x_p.append(jnp.dot(_pad_lanes([d_ob, _mx(d_att)], wide),
                                   _pad_rows([_mx(s0 * ebm[:, ln]), km[:, ln]], wide), preferred_element_type=F32))
                y_p.append(jnp.dot(_pad_lanes([vb[:, ln], _mx(d_att.T)], wide),
                                   _pad_rows([_mx(ds * ecm[:, ln]), qm[:, ln]], wide), preferred_element_type=F32))
                dv_p.append(_dot_nt(_pad_lanes([ke[:, ln], _mx(att.T)], wide), _pad_lanes([_mx(ds), d_ot], wide)))
                ex_p.append(jnp.sum(ds * snext[h], axis=0, keepdims=True))
                dstate[h] = ds * ebc[:, ln] + jnp.dot(d_ot, qm[:, ln], preferred_element_type=F32) * ebm[:, ln]
                snext[h] = s0
                n_p.append(n)
            cat = lambda parts: jnp.concatenate(parts, axis=1)
            x_all, y_all, n_all = cat(x_p), cat(y_p), cat(n_p)
            dg = _cumsum_rows(qm.astype(F32) * x_all - km.astype(F32) * y_all, reverse=True) + cat(ex_p)
            df = dg / f - em2 * y_all
            dgw_ref[...] += jnp.sum(d_on * n_all, axis=0, keepdims=True)
            dlb_ref[...] += jnp.sum(df * (1.0 - sig), axis=0, keepdims=True)
            dp_ref[0, rows, :] = ((em1 * x_all) * (sq * (1.0 + q * (1.0 - sq)))).astype(_MXU_DTYPE)
            dp_ref[1, rows, :] = (df * ((1.0 - lb) * sig * (1.0 - sig))).astype(_MXU_DTYPE)
            dp_ref[2, rows, :] = cat(dv_p).astype(_MXU_DTYPE)
            dp_ref[3, rows, :] = (dyh * (n_all * gwv) * (sg * (1.0 + hg * (1.0 - sg)))).astype(_MXU_DTYPE)

    rev = lambda i: nt - 1 - i
    pair = lambda s: pl.BlockSpec((4, tm, UNIT), lambda i: (s, rev(i), 0))
    row = pl.BlockSpec((tm, D_MODEL), lambda i: (rev(i), 0))
    vec = _full((1, D_MODEL))
    return pl.pallas_call(
        body, name="hgrn_bwd", grid=(nt,),
        in_specs=[row, pl.BlockSpec((D_MODEL, D_MODEL), lambda i: (1, 0)), pair(1), pair(2), row,
                  pl.BlockSpec((nc, HEADS, HEAD_D, HEAD_D), lambda i: (rev(i), 0, 0, 0)),
                  _full((2, D_MODEL)), vec, pl.BlockSpec(memory_space=pl.ANY)],
        out_specs=[pl.BlockSpec((4, tm, D_MODEL), lambda i: (0, rev(i), 0)), vec, vec],
        out_shape=[jax.ShapeDtypeStruct((N_SEG, t, D_MODEL), _MXU_DTYPE),
                   jax.ShapeDtypeStruct((1, D_MODEL), F32), jax.ShapeDtypeStruct((1, D_MODEL), F32)],
        scratch_shapes=[pltpu.VMEM((tm, D_MODEL), F32), pltpu.VMEM((HEADS, HEAD_D, HEAD_D), F32),
                        pltpu.VMEM((HEADS, HEAD_D, HEAD_D), F32)],
        input_output_aliases={8: 0},
        compiler_params=_seq_params(),
    )(dy, wo, p, p, o, states, logits, gw, dp)


def _dp_unit(m):
    return ((m // 2 + 2) % N_SEG) * 2 + m % 2


def _dw_in(ut, dp, tt):
    t = ut.shape[1]
    hr = D_MODEL // 2

    def body(ut_ref, dp_ref, g_ref):
        @pl.when(pl.program_id(1) == 0)
        def _():
            g_ref[...] = jnp.zeros_like(g_ref)

        r = jnp.dot(ut_ref[...], dp_ref[...], preferred_element_type=F32)
        for h in range(2):
            for half in range(2):
                g_ref[h, half] += r[h * hr:(h + 1) * hr, half * UNIT:(half + 1) * UNIT]

    return pl.pallas_call(
        body, name="dw_in", grid=(N_SEG, t // tt),
        in_specs=[pl.BlockSpec((D_MODEL, tt), lambda s, k: (0, k)),
                  pl.BlockSpec((None, tt, D_MODEL), lambda s, k: (s, k, 0))],
        out_specs=pl.BlockSpec((2, 2, hr, UNIT), lambda s, k: (0, (s + 2) % N_SEG, 0, 0)),
        out_shape=jax.ShapeDtypeStruct((2, N_UNIT, hr, UNIT), F32),
        compiler_params=pltpu.CompilerParams(dimension_semantics=("arbitrary", "arbitrary")),
    )(ut, dp)


def _dx(dp, wg, x, dout, wpre, tm, parts, scatter):
    t = x.shape[0]
    n = len(parts)
    nt = t // tm

    def body(*refs):
        dp_ref, w_ref, x_ref, dout_ref, wpre_ref = refs[:5]
        ins = refs[5:5 + n]
        gx_ref, dw_ref = refs[5 + n:7 + n]
        outs = refs[7 + n:7 + 2 * n]
        send_sems, recv_sems = refs[7 + 2 * n:]
        i = pl.program_id(0)

        @pl.when(i == 0)
        def _():
            dw_ref[...] = jnp.zeros_like(dw_ref)
            for cp in _chip_copies(ins, outs, send_sems, recv_sems, scatter):
                cp.start()

        du = None
        for m in range(N_UNIT):
            part = _dot_nt(dp_ref[m // 2, :, (m % 2) * UNIT:(m % 2 + 1) * UNIT], w_ref[_dp_unit(m)])
            du = part if du is None else du + part
        xv = x_ref[...]
        rstd = lax.rsqrt(jnp.mean(xv * xv, axis=-1, keepdims=True) + EPS)
        xn = xv * rstd
        dw_ref[...] += jnp.sum(du * xn, axis=0, keepdims=True)
        dn = du * wpre_ref[...]
        gx_ref[...] = dout_ref[...] + rstd * (dn - xn * jnp.mean(dn * xn, axis=-1, keepdims=True))

        @pl.when(i == nt - 1)
        def _():
            copies = _chip_copies(ins, outs, send_sems, recv_sems, scatter)
            for cp in copies:
                cp.wait_recv()
            for cp in copies:
                cp.wait_send()

    row = pl.BlockSpec((tm, D_MODEL), lambda i: (i, 0))
    any_spec = pl.BlockSpec(memory_space=pl.ANY)
    res = pl.pallas_call(
        body, name="dx", grid=(nt,),
        in_specs=[pl.BlockSpec((N_SEG, tm, D_MODEL), lambda i: (0, i, 0)), _resident((N_UNIT, D_MODEL, UNIT)),
                  row, row, _full((1, D_MODEL))] + [any_spec] * n,
        out_specs=[row, _full((1, D_MODEL))] + [any_spec] * n,
        out_shape=[jax.ShapeDtypeStruct((t, D_MODEL), F32), jax.ShapeDtypeStruct((1, D_MODEL), F32)]
        + _chip_exchange_shapes(parts, scatter),
        scratch_shapes=[pltpu.SemaphoreType.DMA((max(3 * n, 1),)), pltpu.SemaphoreType.DMA((max(3 * n, 1),))],
        compiler_params=_seq_params(),
    )(dp, wg, x, dout, wpre, *parts)
    return res[0], res[1], res[2:]


CHUNK_BYTES = 256 * 1024
MAX_CHUNKS = 16


def _n_chunks(shape, dtype, rows):
    nbytes = jnp.dtype(dtype).itemsize
    for d in shape:
        nbytes *= d
    k = max(1, min(MAX_CHUNKS, rows, nbytes // CHUNK_BYTES))
    while rows % k:
        k -= 1
    return k


def _chip_exchange_shapes(arrays, scatter):
    return [jax.ShapeDtypeStruct((3,) + tuple(a.shape[1:] if sc else a.shape), a.dtype)
            for a, sc in zip(arrays, scatter)]


def _chip_copies(ins, outs, send_sems, recv_sems, scatter):
    if not ins:
        return []
    x, y, c = lax.axis_index("x"), lax.axis_index("y"), lax.axis_index("c")
    peers = [(1 - x, y), (x, 1 - y), (1 - x, 1 - y)]
    copies = []
    for a in range(len(ins)):
        for j, (px, py) in enumerate(peers):
            copies.append(pltpu.make_async_remote_copy(
                src_ref=ins[a].at[2 * px + py] if scatter[a] else ins[a], dst_ref=outs[a].at[j],
                send_sem=send_sems.at[a * 3 + j], recv_sem=recv_sems.at[a * 3 + j],
                device_id=(px, py, c), device_id_type=MESH))
    return copies


def _core_swap(arrays, name, halved):
    n = len(arrays)
    shapes, n_sems = _core_swap_plan(arrays, halved)

    def body(*refs):
        copies = _core_swap_copies(arrays, halved, refs[:n], refs[n:2 * n], refs[2 * n], refs[2 * n + 1])
        for cp in copies:
            cp.start()
        for cp in copies:
            cp.wait_recv()
        for cp in copies:
            cp.wait_send()

    any_spec = pl.BlockSpec(memory_space=pl.ANY)
    return pl.pallas_call(
        body, name=name,
        in_specs=[any_spec] * n, out_specs=[any_spec] * n,
        out_shape=shapes,
        scratch_shapes=[pltpu.SemaphoreType.DMA((n_sems,)), pltpu.SemaphoreType.DMA((n_sems,))],
    )(*arrays)


def _core_swap_plan(arrays, halved):
    shapes = [a.shape[1:] if halved else a.shape for a in arrays]
    n_sems = sum(_n_chunks(s, a.dtype, s[0]) for s, a in zip(shapes, arrays))
    return [jax.ShapeDtypeStruct(s, a.dtype) for s, a in zip(shapes, arrays)], max(n_sems, 1)


def _core_swap_copies(arrays, halved, ins, outs, send_sems, recv_sems):
    if not arrays:
        return []
    x, y, c = lax.axis_index("x"), lax.axis_index("y"), lax.axis_index("c")
    copies, sem = [], 0
    for a, arr in enumerate(arrays):
        shape = arr.shape[1:] if halved else arr.shape
        k = _n_chunks(shape, arr.dtype, shape[0])
        step = shape[0] // k
        for j in range(k):
            rows = pl.ds(j * step, step)
            copies.append(pltpu.make_async_remote_copy(
                src_ref=ins[a].at[1 - c, rows] if halved else ins[a].at[rows], dst_ref=outs[a].at[rows],
                send_sem=send_sems.at[sem], recv_sem=recv_sems.at[sem],
                device_id=(x, y, 1 - c), device_id_type=MESH))
            sem += 1
    return copies


def _finish_exchange(arrays, small):
    n = len(arrays)
    ks = [_n_chunks(a.shape, a.dtype, a.shape[0]) for a in arrays]
    offs = [sum(ks[:a]) for a in range(n)]
    total = sum(ks)

    def body(*refs):
        ins, small_ref = refs[:n], refs[n]
        outs, gathered = refs[n + 1:2 * n + 1], refs[2 * n + 1]
        send_sems, recv_sems = refs[2 * n + 2:]
        x, y, c = lax.axis_index("x"), lax.axis_index("y"), lax.axis_index("c")
        copies = []
        for a in range(n):
            step = arrays[a].shape[0] // ks[a]
            for j in range(ks[a]):
                rows = pl.ds(j * step, step)
                copies.append(pltpu.make_async_remote_copy(
                    src_ref=ins[a].at[rows], dst_ref=outs[a].at[rows],
                    send_sem=send_sems.at[offs[a] + j], recv_sem=recv_sems.at[offs[a] + j],
                    device_id=(x, y, 1 - c), device_id_type=MESH))
        for slot in range(7):
            dx, dy, dc = (slot + 1) % 2, ((slot + 1) // 2) % 2, (slot + 1) // 4
            copies.append(pltpu.make_async_remote_copy(
                src_ref=small_ref, dst_ref=gathered.at[slot],
                send_sem=send_sems.at[total + slot], recv_sem=recv_sems.at[total + slot],
                device_id=(x + dx - 2 * x * dx, y + dy - 2 * y * dy, c + dc - 2 * c * dc), device_id_type=MESH))
        for cp in copies:
            cp.start()
        for cp in copies:
            cp.wait_recv()
        for cp in copies:
            cp.wait_send()

    any_spec = pl.BlockSpec(memory_space=pl.ANY)
    res = pl.pallas_call(
        body, name="reduce_cores",
        in_specs=[any_spec] * (n + 1), out_specs=[any_spec] * (n + 1),
        out_shape=[jax.ShapeDtypeStruct(a.shape, a.dtype) for a in arrays]
        + [jax.ShapeDtypeStruct((7,) + small.shape, small.dtype)],
        scratch_shapes=[pltpu.SemaphoreType.DMA((total + 7,)), pltpu.SemaphoreType.DMA((total + 7,))],
    )(*arrays, small)
    return res[:n], res[n]


def _sum_own_half(g, b, name, out_dtype, tr=512):
    _, rows, cols = g.shape
    tr = min(tr, rows)
    core = lax.axis_index("c").reshape(1)

    def body(c_ref, g_ref, b_ref, o_ref):
        del c_ref
        o_ref[...] = (g_ref[...] + b_ref[...]).astype(out_dtype)

    return pl.pallas_call(
        body, name=name,
        grid_spec=pltpu.PrefetchScalarGridSpec(
            num_scalar_prefetch=1, grid=(rows // tr,),
            in_specs=[pl.BlockSpec((None, tr, cols), lambda i, c_ref: (c_ref[0], i, 0)),
                      pl.BlockSpec((tr, cols), lambda i, c_ref: (i, 0))],
            out_specs=pl.BlockSpec((tr, cols), lambda i, c_ref: (i, 0))),
        out_shape=jax.ShapeDtypeStruct((rows, cols), out_dtype),
        compiler_params=pltpu.CompilerParams(dimension_semantics=("parallel",)),
    )(core, g, b)


def _sum_chips(own, r, slabbed, name, tr=512):
    _, rows, cols = r.shape
    tr = min(tr, rows)
    chip = (2 * lax.axis_index("x") + lax.axis_index("y")).reshape(1)

    def body(c_ref, own_ref, r_ref, o_ref):
        del c_ref
        f = lambda val: val.astype(F32)
        o_ref[...] = (f(own_ref[...]) + f(r_ref[0])) + (f(r_ref[1]) + f(r_ref[2]))

    if slabbed:
        own_spec = pl.BlockSpec((None, tr, cols), lambda i, c_ref: (c_ref[0], i, 0))
    else:
        own_spec = pl.BlockSpec((tr, cols), lambda i, c_ref: (i, 0))
    return pl.pallas_call(
        body, name=name,
        grid_spec=pltpu.PrefetchScalarGridSpec(
            num_scalar_prefetch=1, grid=(rows // tr,),
            in_specs=[own_spec, pl.BlockSpec((3, tr, cols), lambda i, c_ref: (0, i, 0))],
            out_specs=pl.BlockSpec((tr, cols), lambda i, c_ref: (i, 0))),
        out_shape=jax.ShapeDtypeStruct((rows, cols), F32),
        compiler_params=pltpu.CompilerParams(dimension_semantics=("parallel",)),
    )(chip, own, r)


def _adamw_big(own, other, w, m, v, name, grid, g_block, g_index, w_block, w_index):
    core = lax.axis_index("c").reshape(1)
    half_axis = len(grid) - 1

    def body(c_ref, own_ref, oth_ref, w_ref, m_ref, v_ref, g_ref, d_ref, nm_ref, nv_ref):
        mine = pl.program_id(half_axis) == c_ref[0]
        gv = jnp.where(mine, own_ref[...], oth_ref[...])
        g_ref[...] = gv
        d_ref[...], nm_ref[...], nv_ref[...] = _adamw(w_ref[...], gv, m_ref[...], v_ref[...])

    g_spec = pl.BlockSpec(g_block, lambda *a: g_index(*a[:half_axis]))
    w_spec = pl.BlockSpec(w_block, lambda *a: w_index(*a[:-1]))
    shp = jax.ShapeDtypeStruct(w.shape, F32)
    return pl.pallas_call(
        body, name=name,
        grid_spec=pltpu.PrefetchScalarGridSpec(
            num_scalar_prefetch=1, grid=grid,
            in_specs=[g_spec, g_spec, w_spec, w_spec, w_spec], out_specs=[w_spec] * 4),
        out_shape=[shp] * 4,
        compiler_params=pltpu.CompilerParams(dimension_semantics=("parallel",) * len(grid)),
    )(core, own, other, w, m, v)


def _small_finish(own, others, lam, logits):
    def body(a_ref, b_ref, lam_ref, lg_ref, o_ref):
        g = (((a_ref[...] + b_ref[0]) + (b_ref[1] + b_ref[2]))
             + ((b_ref[3] + b_ref[4]) + (b_ref[5] + b_ref[6])))
        o_ref[...] = g
        o_ref[8:9, :] = g[8:9, :] * (-_sigmoid(-lam_ref[...]))
        lb = 1.0 / (1.0 + jnp.exp(lg_ref[1:2, :] - lg_ref[0:1, :]))
        d0 = g[9:10, :] * (lb * (1.0 - lb))
        o_ref[9:10, :] = d0
        o_ref[10:11, :] = -d0

    return pl.pallas_call(
        body, name="small_finish",
        in_specs=[_full((SMALL_ROWS, D_MODEL)), _full((7, SMALL_ROWS, D_MODEL)), _full((1, D_MODEL)),
                  _full((2, D_MODEL))],
        out_specs=_full((SMALL_ROWS, D_MODEL)),
        out_shape=jax.ShapeDtypeStruct((SMALL_ROWS, D_MODEL), F32),
    )(own, others, lam, logits)


def _adamw_small(ws, gs, ms, vs):
    n = len(ws)

    def body(*refs):
        w_r, g_r, m_r, v_r = refs[:n], refs[n:2 * n], refs[2 * n:3 * n], refs[3 * n:4 * n]
        d_o, m_o, v_o = refs[4 * n:5 * n], refs[5 * n:6 * n], refs[6 * n:7 * n]
        for j in range(n):
            d_o[j][...], m_o[j][...], v_o[j][...] = _adamw(w_r[j][...], g_r[j][...], m_r[j][...], v_r[j][...])

    specs = [_full(w.shape) for w in ws]
    shapes = [jax.ShapeDtypeStruct(w.shape, F32) for w in ws]
    outs = pl.pallas_call(
        body, name="adamw_small",
        in_specs=specs * 4, out_specs=specs * 3, out_shape=shapes * 3,
    )(*ws, *gs, *ms, *vs)
    return outs[:n], outs[n:2 * n], outs[2 * n:]


def _local_step(x, tgt, ut, p, pre_w, wg, cw, cb, wa, wx, ba, bx, lam, logits, gw, wo, post_w, tm, tmm,
                prepare=None):
    h, y2 = _lru_fwd(p, wa, wx, ba, bx, lam, cw, cb, tmm)
    y2, o, states = _hgrn_fwd(p, logits, gw, y2, tmm)
    dout, dy, sq, d_post = _out_fused(y2, wo, x, tgt, post_w, tmm)
    g_out = _dw_out(y2, dy, min(2048, x.shape[0]))
    dp, d_wa, d_wx, d_ba, d_bx, d_sp, d_cw, d_cb = _lru_bwd(dy, wo, p, h, wa, wx, ba, bx, lam, cw, cb, tmm)
    dp, d_gw, d_lb = _hgrn_bwd(dy, wo, p, o, states, logits, gw, dp, tmm)
    g_in = _dw_in(ut, dp, min(2048, x.shape[0]))
    parts, scatter = prepare(g_in, g_out, d_wa, d_wx) if prepare is not None else ([], [])
    grad_x, d_pre, received = _dx(dp, wg, x, dout, pre_w, tmm, parts, scatter)
    small = jnp.concatenate([d_pre, d_cw, d_cb, d_ba, d_bx, d_sp, d_lb,
                             jnp.zeros((1, D_MODEL), F32), d_gw, d_post, sq,
                             jnp.zeros((SMALL_ROWS - 14, D_MODEL), F32)], axis=0)
    return grad_x, g_in, g_out, d_wa, d_wx, small, parts, received


def kernel(x, pre_norm_w, w_in, conv_w, conv_b, lru_w_a, lru_b_a, lru_w_x, lru_b_x, lru_lambda, hgrn_lb_logits, hgrn_gnorm_w, w_out, post_norm_w, loss_target, m_pre_norm_w, m_w_in, m_conv_w, m_conv_b, m_lru_w_a, m_lru_b_a, m_lru_w_x, m_lru_b_x, m_lru_lambda, m_hgrn_lb_logits, m_hgrn_gnorm_w, m_w_out, m_post_norm_w, v_pre_norm_w, v_w_in, v_conv_w, v_conv_b, v_lru_w_a, v_lru_b_a, v_lru_w_x, v_lru_b_x, v_lru_lambda, v_hgrn_lb_logits, v_hgrn_gnorm_w, v_w_out, v_post_norm_w):
    t = x.shape[1]
    tm = min(256, t)
    tmm = min(512, t)
    chip = 2 * lax.axis_index("x") + lax.axis_index("y")

    win_units = w_in[0].astype(_MXU_DTYPE).reshape(D_MODEL, 3, UNIT).transpose(1, 0, 2)
    small_w = jnp.concatenate([conv_w[0], lru_b_a[0], lru_b_x[0]], axis=1)
    ut, p, wg4, (wo4, wa4, wx4), sw4 = _in_proj_gather(
        x[0], pre_norm_w, win_units,
        [w_out[0].astype(_MXU_DTYPE).reshape(2, 256, D_MODEL), lru_w_a[0].astype(_MXU_DTYPE).reshape(2, 128, LRU_BW),
         lru_w_x[0].astype(_MXU_DTYPE).reshape(2, 128, LRU_BW)],
        small_w, min(1024, t))
    wg = wg4.reshape(N_UNIT, D_MODEL, UNIT)
    wo = wo4.reshape(2 * D_MODEL, D_MODEL)
    by_block = lambda w4: w4.reshape(N_CHIPS, LRU_BLOCKS, 64, LRU_BW).transpose(1, 0, 2, 3).reshape(
        LRU_BLOCKS, LRU_BW, LRU_BW)
    wa, wx = by_block(wa4), by_block(wx4)
    sw4 = sw4.reshape(N_CHIPS, 4, 384)
    cw = sw4[:, :, 0:256].transpose(1, 0, 2).reshape(4, D_MODEL)
    ba = sw4[:, :, 256:320].transpose(1, 0, 2).reshape(1, D_MODEL)
    bx = sw4[:, :, 320:384].transpose(1, 0, 2).reshape(1, D_MODEL)

    def lru_layout(g):
        g = g.reshape(2, 2, N_CHIPS, 64, LRU_BW).transpose(0, 2, 1, 3, 4)
        return g.reshape(2, N_CHIPS * 128, LRU_BW)

    def prepare(g_in, g_out, d_wa, d_wx):
        g_in = g_in.reshape(2, N_UNIT * 512, UNIT)
        g_out = g_out.reshape(2, N_CHIPS * 256, D_MODEL)
        g_wa, g_wx = lru_layout(d_wa), lru_layout(d_wx)
        b_in, b_out, b_wa, b_wx = _core_swap([g_in, g_out, g_wa, g_wx], "reduce_swap", halved=True)
        parts = [_sum_own_half(g_in, b_in, "presum_in", _WIRE_DTYPE).reshape(N_CHIPS, 3 * 512, UNIT),
                 _sum_own_half(g_out, b_out, "presum_out", _WIRE_DTYPE).reshape(N_CHIPS, 256, D_MODEL),
                 _sum_own_half(g_wa, b_wa, "presum_wa", _WIRE_DTYPE).reshape(N_CHIPS, 128, LRU_BW),
                 _sum_own_half(g_wx, b_wx, "presum_wx", _WIRE_DTYPE).reshape(N_CHIPS, 128, LRU_BW)]
        return parts, [True] * 4

    grad_x, _, _, _, _, small, (p_in, p_out, p_wa, p_wx), (r_in, r_out, r_wa, r_wx) = _local_step(
        x[0], loss_target[0], ut, p, pre_norm_w, wg, cw, conv_b, wa, wx, ba, bx, lru_lambda, hgrn_lb_logits,
        hgrn_gnorm_w, wo, post_norm_w, tm, tmm, prepare)
    s_in = _sum_chips(p_in, r_in, True, "sum_in")
    s_out = _sum_chips(p_out, r_out, True, "sum_out")
    s_wa = _sum_chips(p_wa, r_wa, True, "sum_wa")
    s_wx = _sum_chips(p_wx, r_wx, True, "sum_wx")
    (o_in, o_out, o_wa, o_wx), small_others = _finish_exchange([s_in, s_out, s_wa, s_wx], small)

    g_w_in, d_w_in, nm_w_in, nv_w_in = _adamw_big(
        s_in.reshape(3, 512, UNIT), o_in.reshape(3, 512, UNIT), w_in[0], m_w_in[0], v_w_in[0], "adamw_w_in", (3, 2),
        (None, 512, UNIT), lambda k: (k, 0, 0), (512, UNIT), lambda k, h: (h, k))
    g_w_out, d_w_out, nm_w_out, nv_w_out = _adamw_big(
        s_out, o_out, w_out[0], m_w_out[0], v_w_out[0], "adamw_w_out", (2,),
        (256, D_MODEL), lambda: (0, 0), (256, D_MODEL), lambda h: (h, 0))
    sq2 = lambda a: a.reshape(LRU_BW, LRU_BW)
    lru_specs = ((2,), (128, LRU_BW), lambda: (0, 0), (128, LRU_BW), lambda h: (h, 0))
    g_wa, d_wa2, nm_wa, nv_wa = _adamw_big(s_wa, o_wa, sq2(lru_w_a), sq2(m_lru_w_a), sq2(v_lru_w_a), "adamw_wa",
                                           *lru_specs)
    g_wx, d_wx2, nm_wx, nv_wx = _adamw_big(s_wx, o_wx, sq2(lru_w_x), sq2(m_lru_w_x), sq2(v_lru_w_x), "adamw_wx",
                                           *lru_specs)

    gs = _small_finish(small, small_others, lru_lambda, hgrn_lb_logits)
    loss = jnp.sum(gs[13]) * (0.5 / D_MODEL)
    g_pre = gs[0:1]
    g_cw = lax.dynamic_slice(gs[1:5], (0, chip * 256), (4, 256))
    g_cb = gs[5:6]
    g_ba = lax.dynamic_slice(gs[6].reshape(LRU_BLOCKS, N_CHIPS, 64), (0, chip, 0), (LRU_BLOCKS, 1, 64)).reshape(4, 64)
    g_bx = lax.dynamic_slice(gs[7].reshape(LRU_BLOCKS, N_CHIPS, 64), (0, chip, 0), (LRU_BLOCKS, 1, 64)).reshape(4, 64)
    g_lam = gs[8:9]
    g_lb = gs[9:11]
    g_gw = gs[11:12]
    g_post = gs[12:13]
    small_g = [g_pre, g_cw, g_cb, g_ba, g_bx, g_lam, g_lb, g_gw, g_post]
    two_d = lambda a: a.reshape(a.shape[-2:])
    small_w_list = [pre_norm_w, conv_w, conv_b, lru_b_a, lru_b_x, lru_lambda, hgrn_lb_logits, hgrn_gnorm_w, post_norm_w]
    small_m_list = [m_pre_norm_w, m_conv_w, m_conv_b, m_lru_b_a, m_lru_b_x, m_lru_lambda, m_hgrn_lb_logits,
                    m_hgrn_gnorm_w, m_post_norm_w]
    small_v_list = [v_pre_norm_w, v_conv_w, v_conv_b, v_lru_b_a, v_lru_b_x, v_lru_lambda, v_hgrn_lb_logits,
                    v_hgrn_gnorm_w, v_post_norm_w]
    sd, sm, sv = _adamw_small([two_d(a) for a in small_w_list], small_g,
                              [two_d(a) for a in small_m_list], [two_d(a) for a in small_v_list])

    def shaped(vals, refs):
        return [val.reshape(ref.shape) for val, ref in zip(vals, refs)]

    s_g = shaped(small_g, small_w_list)
    s_d = shaped(sd, small_w_list)
    s_m = shaped(sm, small_w_list)
    s_v = shaped(sv, small_w_list)

    def ordered(small, big_in, big_wa, big_wx, big_out):
        pre, cw_, cb_, ba_, bx_, lam_, lb_, gw_, post_ = small
        return [pre, big_in.reshape(w_in.shape), cw_, cb_, big_wa.reshape(lru_w_a.shape), ba_,
                big_wx.reshape(lru_w_x.shape), bx_, lam_, lb_, gw_, big_out.reshape(w_out.shape), post_]

    grads = ordered(s_g, g_w_in, g_wa, g_wx, g_w_out)
    deltas = ordered(s_d, d_w_in, d_wa2, d_wx2, d_w_out)
    new_m = ordered(s_m, nm_w_in, nm_wa, nm_wx, nm_w_out)
    new_v = ordered(s_v, nv_w_in, nv_wa, nv_wx, nv_w_out)
    return (loss, grad_x.reshape(x.shape), *grads, *deltas, *new_m, *new_v)
```

```python
import functools

import jax
import jax.numpy as jnp
from jax import lax
from jax.experimental import pallas as pl
from jax.experimental.pallas import tpu as pltpu

F32 = jnp.float32
BF16 = jnp.bfloat16
_MXU_DTYPE = jnp.bfloat16
_WIRE_DTYPE = jnp.bfloat16

D_MODEL = 1024
N_SEG = 6
UNIT = 512
N_UNIT = 12
LRU_BLOCKS = 4
LRU_BW = 256
LRU_C = 8.0
HEADS = 8
HEAD_D = 128
CHUNK = 64
EPS = 1e-6
N_CHIPS = 4
EXP_CLAMP = 80.0

ADAM_LR = 0.001
ADAM_B1 = 0.9
ADAM_B2 = 0.999
ADAM_EPS = 1e-08
ADAM_WD = 0.01
ADAM_STEP = 10

SMALL_ROWS = 16
FWD_PIECES = 4
MESH = pl.DeviceIdType.MESH


def _sigmoid(x):
    return 0.5 * jnp.tanh(0.5 * x) + 0.5


def _mx(x):
    return x.astype(_MXU_DTYPE)


def _dot(a, b):
    return jnp.dot(_mx(a), _mx(b), preferred_element_type=F32)


def _dot_nt(a, b):
    return lax.dot_general(_mx(a), _mx(b), (((1,), (1,)), ((), ())), preferred_element_type=F32)


def _dot_tn(a, b):
    return lax.dot_general(_mx(a), _mx(b), (((0,), (0,)), ((), ())), preferred_element_type=F32)


def _rows(shape):
    return lax.broadcasted_iota(jnp.int32, shape, 0)


def _cols(shape):
    return lax.broadcasted_iota(jnp.int32, shape, 1)


def _softplus_neg(lam):
    z = -lam
    e = jnp.exp(-jnp.abs(z))
    series = e * (1.0 - e * (0.5 - e * (1.0 / 3.0 - 0.25 * e)))
    return jnp.maximum(z, 0.0) + jnp.where(e < 1e-2, series, jnp.log(1.0 + e))


def _one_minus_sq(a, y):
    series = -y * (1.0 + y * (0.5 + y * (1.0 / 6.0)))
    return jnp.where(y > -0.01, series, 1.0 - a * a)


def _cumsum_rows(x, reverse=False):
    n = x.shape[0] // 8
    row8 = _rows((8, x.shape[1]))
    out = [None] * n
    carry = None
    for g in (reversed(range(n)) if reverse else range(n)):
        blk = x[8 * g:8 * g + 8]
        for s in (1, 2, 4):
            if reverse:
                blk = blk + jnp.where(row8 < 8 - s, pltpu.roll(blk, 8 - s, 0), 0.0)
            else:
                blk = blk + jnp.where(row8 >= s, pltpu.roll(blk, s, 0), 0.0)
        if carry is not None:
            blk = blk + carry
        carry = blk[0:1] if reverse else blk[7:8]
        out[g] = blk
    return jnp.concatenate(out, axis=0)


def _shift_down(x, halo, s):
    if s == 0:
        return x
    r = pltpu.roll(x, s, 0)
    top = jnp.where(_rows(halo.shape) < s, pltpu.roll(halo, s, 0), r[0:8])
    return jnp.concatenate([top, r[8:]], axis=0)


def _shift_up(x, head, s):
    if s == 0:
        return x
    n = x.shape[0]
    r = pltpu.roll(x, n - s, 0)
    bottom = jnp.where(_rows(head.shape) >= 8 - s, pltpu.roll(head, 8 - s, 0), r[n - 8:n])
    return jnp.concatenate([r[:n - 8], bottom], axis=0)


def _lru_gates(lx, halo, cw, cb, wa, wx, ba, bx, sp):
    xs = [_shift_down(lx, halo, 3 - k) for k in range(4)]
    xc = cb + cw[0:1] * xs[0] + cw[1:2] * xs[1] + cw[2:3] * xs[2] + cw[3:4] * xs[3]
    zr, zi = [], []
    for n in range(LRU_BLOCKS):
        xb = _mx(xc[:, n * LRU_BW:(n + 1) * LRU_BW])
        zr.append(jnp.dot(xb, wa[n], preferred_element_type=F32))
        zi.append(jnp.dot(xb, wx[n], preferred_element_type=F32))
    r = _sigmoid(jnp.concatenate(zr, axis=1) + ba)
    ig = _sigmoid(jnp.concatenate(zi, axis=1) + bx)
    la = (-LRU_C * sp) * r
    a = jnp.exp(la)
    om = _one_minus_sq(a, 2.0 * la)
    return xs, xc, r, ig, a, om


def _hgrn_prep(q, fr, lb):
    sig = _sigmoid(fr)
    f = lb + (1.0 - lb) * sig
    sq = _sigmoid(q)
    b = _cumsum_rows(jnp.log(f))
    bm = b[CHUNK // 2 - 1:CHUNK // 2]
    bc = b[CHUNK - 1:CHUNK]
    em1 = jnp.exp(jnp.minimum(b - bm, EXP_CLAMP))
    em2 = jnp.exp(jnp.minimum(bm - b, EXP_CLAMP))
    qm = _mx((q * sq) * em1)
    km = _mx((1.0 - f) * em2)
    return sig, f, sq, em1, em2, qm, km, jnp.exp(bm), jnp.exp(bc - bm), jnp.exp(bc)


def _seg(ref, s, rows=slice(None)):
    return jnp.concatenate([ref[2 * s, rows, :], ref[2 * s + 1, rows, :]], axis=1)


def _pad_lanes(parts, width):
    have = sum(p.shape[1] for p in parts)
    pad = [jnp.zeros((parts[0].shape[0], width - have), parts[0].dtype)] if width > have else []
    return jnp.concatenate(list(parts) + pad, axis=1)


def _pad_rows(parts, height):
    have = sum(p.shape[0] for p in parts)
    pad = [jnp.zeros((height - have, parts[0].shape[1]), parts[0].dtype)] if height > have else []
    return jnp.concatenate(list(parts) + pad, axis=0)


def _adamw(w, g, m, v):
    m = ADAM_B1 * m + (1.0 - ADAM_B1) * g
    v = ADAM_B2 * v + (1.0 - ADAM_B2) * jnp.square(g)
    m_hat = m / (1.0 - ADAM_B1 ** ADAM_STEP)
    v_hat = v / (1.0 - ADAM_B2 ** ADAM_STEP)
    delta = -ADAM_LR * (m_hat / (jnp.sqrt(v_hat) + ADAM_EPS) + ADAM_WD * w)
    return delta, m, v


def _full(shape):
    nd = len(shape)
    return pl.BlockSpec(shape, lambda *_: (0,) * nd)


def _seq_params():
    return pltpu.CompilerParams(dimension_semantics=("arbitrary",))


def _resident(shape):
    nd = len(shape)
    return pl.BlockSpec(shape, lambda *_: (0,) * nd, pipeline_mode=pl.Buffered(1))


def _in_proj_gather(x, w, win_units, halved, small_w, tm):
    t = x.shape[0]
    nt = t // tm
    n_h = len(halved)
    n_split = 3 + n_h
    n_items = n_split + 1
    hu = D_MODEL // 2
    sub = min(256, tm)
    chip = 2 * lax.axis_index("x") + lax.axis_index("y")
    order = jnp.stack([chip] + [jnp.bitwise_xor(chip, flip) for flip in (2, 1, 3)])

    def body(order_ref, x_ref, w_ref, win_ref, *rest):
        del order_ref
        h_ins, small_ref = rest[:n_h], rest[n_h]
        ut_ref, p_ref, wg_ref = rest[n_h + 1:n_h + 4]
        h_outs, sw_ref = rest[n_h + 4:2 * n_h + 4], rest[2 * n_h + 4]
        u_s, wbuf, send, recv, fsend, frecv, osend, orecv, wsem = rest[2 * n_h + 5:]
        g, i = pl.program_id(0), pl.program_id(1)
        mx, my, c = lax.axis_index("x"), lax.axis_index("y"), lax.axis_index("c")
        me = 2 * mx + my
        peers = [(1 - mx, my), (mx, 1 - my), (1 - mx, 1 - my)]
        qs = [2 * px + py for px, py in peers]
        sibling = (mx, my, 1 - c)

        def remote(src, dst, ssem, rsem, dev):
            return pltpu.make_async_remote_copy(src_ref=src, dst_ref=dst, send_sem=ssem, recv_sem=rsem,
                                                device_id=dev, device_id_type=MESH)

        def mine(item):
            if item < 3:
                return win_ref.at[item, pl.ds(c * hu, hu)]
            return h_ins[item - 3].at[c] if item < n_split else small_ref

        def piece(item, q, h):
            if item < 3:
                return wg_ref.at[q, item, pl.ds(h * hu, hu)]
            return h_outs[item - 3].at[q, h] if item < n_split else sw_ref.at[q]

        def fetch(item, jj, q):
            px, py = peers[jj]
            return remote(mine(item), piece(item, q, c), send.at[item * 3 + jj], recv.at[item * 3 + jj], (px, py, c))

        def forward(item, jj, h):
            whole = piece(item, qs[jj], h)
            step = whole.shape[0] // FWD_PIECES
            copies = []
            for k in range(FWD_PIECES):
                part = whole.at[pl.ds(k * step, step)]
                sem = (item * 3 + jj) * FWD_PIECES + k
                copies.append(remote(part, part, fsend.at[sem], frecv.at[sem], sibling))
            return copies

        def own(item):
            if item < 3:
                src, dst = win_ref.at[item], wg_ref.at[me, item]
            elif item < n_split:
                src, dst = h_ins[item - 3], h_outs[item - 3].at[me]
            else:
                src, dst = small_ref, sw_ref.at[me]
            return remote(src, dst, osend.at[item], orecv.at[item], sibling)

        def load_units(gg):
            src = lambda k: win_ref.at[k] if gg == 0 else wg_ref.at[qs[gg - 1], k]
            return [pltpu.make_async_copy(src(k), wbuf.at[gg % 2, k], wsem.at[(gg % 2) * 3 + k]) for k in range(3)]

        def in_pass(gg):
            return pl.when(g == gg)

        def pass_on(gg):
            for item in range(3):
                fetch(item, gg, qs[gg]).wait_recv()
                for cp in forward(item, gg, c):
                    cp.start()

        @pl.when(i == 0)
        def _():
            @in_pass(0)
            def _():
                for item in range(3):
                    for jj in range(2):
                        fetch(item, jj, me).start()
                for item in range(n_items):
                    own(item).start()
                for cp in load_units(0):
                    cp.start()

            @in_pass(1)
            def _():
                for item in range(3):
                    fetch(item, 2, me).start()

            @in_pass(2)
            def _():
                for item in range(3, n_items):
                    for jj in range(3):
                        fetch(item, jj, me).start()

            for gg in range(N_CHIPS):
                @in_pass(gg)
                def _(gg=gg):
                    for cp in load_units(gg):
                        cp.wait()

        @pl.when(g == 0)
        def _():
            xv = x_ref[...]
            rstd = lax.rsqrt(jnp.mean(xv * xv, axis=-1, keepdims=True) + EPS)
            u = xv * rstd * w_ref[...]
            ut_ref[...] = u.T.astype(_MXU_DTYPE)
            u_s[pl.ds(pl.multiple_of(i * tm, tm), tm), :] = _mx(u)

        slot = g % 2
        for r in range(tm // sub):
            ur = u_s[pl.ds(pl.multiple_of(i * tm + r * sub, sub), sub), :]
            for k in range(3):
                p_ref[k, r * sub:(r + 1) * sub, :] = jnp.dot(ur, wbuf[slot, k], preferred_element_type=F32)

        @pl.when(i == nt // 2)
        def _():
            for gg in (1, 2):
                in_pass(gg)(functools.partial(pass_on, gg))

        @pl.when(i == nt - 1)
        def _():
            in_pass(0)(functools.partial(pass_on, 0))
            for gg in range(3):
                @in_pass(gg)
                def _(gg=gg):
                    for item in range(3):
                        for cp in forward(item, gg, 1 - c):
                            cp.wait_recv()
                    for cp in load_units(gg + 1):
                        cp.start()

            @in_pass(N_CHIPS - 1)
            def _():
                for item in range(3, n_items):
                    for jj in range(3):
                        fetch(item, jj, qs[jj]).wait_recv()
                        if item < n_split:
                            for cp in forward(item, jj, c):
                                cp.start()
                for item in range(3, n_split):
                    for jj in range(3):
                        for cp in forward(item, jj, 1 - c):
                            cp.wait_recv()
                for item in range(n_items):
                    own(item).wait_recv()
                for item in range(n_items):
                    for jj in range(3):
                        fetch(item, jj, me).wait_send()
                        if item < n_split:
                            for cp in forward(item, jj, c):
                                cp.wait_send()
                    own(item).wait_send()

    any_spec = pl.BlockSpec(memory_space=pl.ANY)
    first_pass = lambda g, i: jnp.where(g == 0, i, nt - 1)
    res = pl.pallas_call(
        body, name="in_proj",
        grid_spec=pltpu.PrefetchScalarGridSpec(
            num_scalar_prefetch=1, grid=(N_CHIPS, nt),
            in_specs=[pl.BlockSpec((tm, D_MODEL), lambda g, i, o: (first_pass(g, i), 0)),
                      pl.BlockSpec((1, D_MODEL), lambda g, i, o: (0, 0))] + [any_spec] * (n_h + 2),
            out_specs=[pl.BlockSpec((D_MODEL, tm), lambda g, i, o: (0, first_pass(g, i))),
                       pl.BlockSpec((3, tm, UNIT), lambda g, i, o: (o[g], i, 0))]
            + [any_spec] * (n_h + 2),
            scratch_shapes=[pltpu.VMEM((t, D_MODEL), _MXU_DTYPE), pltpu.VMEM((2, 3, D_MODEL, UNIT), _MXU_DTYPE),
                            pltpu.SemaphoreType.DMA((3 * n_items,)), pltpu.SemaphoreType.DMA((3 * n_items,)),
                            pltpu.SemaphoreType.DMA((3 * n_split * FWD_PIECES,)),
                            pltpu.SemaphoreType.DMA((3 * n_split * FWD_PIECES,)),
                            pltpu.SemaphoreType.DMA((n_items,)), pltpu.SemaphoreType.DMA((n_items,)),
                            pltpu.SemaphoreType.DMA((6,))]),
        out_shape=[jax.ShapeDtypeStruct((D_MODEL, t), _MXU_DTYPE), jax.ShapeDtypeStruct((N_UNIT, t, UNIT), F32),
                   jax.ShapeDtypeStruct((N_CHIPS,) + win_units.shape, win_units.dtype)]
        + [jax.ShapeDtypeStruct((N_CHIPS,) + a.shape, a.dtype) for a in halved]
        + [jax.ShapeDtypeStruct((N_CHIPS,) + small_w.shape, small_w.dtype)],
        compiler_params=pltpu.CompilerParams(dimension_semantics=("arbitrary", "arbitrary")),
    )(order, x, w, win_units, *halved, small_w)
    return res[0], res[1], res[2], res[3:3 + n_h], res[3 + n_h]


def _lru_fwd(p, wa, wx, ba, bx, lam, cw, cb, tm):
    t = p.shape[1]
    ng = tm // 8

    def body(lx_ref, halo_ref, lg_ref, wa_ref, wx_ref, ba_ref, bx_ref, lam_ref, cw_ref, cb_ref,
             h_ref, y_ref, a_s, u_s, hc_s):
        i = pl.program_id(0)

        @pl.when(i == 0)
        def _():
            hc_s[...] = jnp.zeros_like(hc_s)

        sp = _softplus_neg(lam_ref[...])
        halo = jnp.where(i == 0, 0.0, _seg(halo_ref, 0))
        _, xc, _, ig, a, om = _lru_gates(_seg(lx_ref, 0), halo, cw_ref[...], cb_ref[...], wa_ref[...],
                                         wx_ref[...], ba_ref[...], bx_ref[...], sp)
        a_s[...] = a
        u_s[...] = jnp.sqrt(om) * (ig * xc)
        row8 = _rows((8, D_MODEL))

        def group(g, hc):
            rows = pl.ds(pl.multiple_of(g * 8, 8), 8)
            av = a_s[rows, :]
            uv = u_s[rows, :]
            for s in (1, 2, 4):
                a_sh = jnp.where(row8 >= s, pltpu.roll(av, s, 0), 1.0)
                u_sh = jnp.where(row8 >= s, pltpu.roll(uv, s, 0), 0.0)
                uv = av * u_sh + uv
                av = av * a_sh
            hh = av * hc + uv
            h_ref[rows, :] = hh
            return jnp.broadcast_to(hh[7:8, :], (8, D_MODEL))

        hc_s[...] = lax.fori_loop(0, ng, group, hc_s[...])
        lg = _seg(lg_ref, 0)
        y_ref[...] = (h_ref[...] * (lg * _sigmoid(lg))).astype(_MXU_DTYPE)

    seg = lambda s: pl.BlockSpec((2, tm, UNIT), lambda i: (s, i, 0))
    return pl.pallas_call(
        body, name="lru_fwd", grid=(t // tm,),
        in_specs=[seg(0),
                  pl.BlockSpec((2, 8, UNIT), lambda i: (0, jnp.maximum(i * (tm // 8) - 1, 0), 0)),
                  seg(1),
                  _full((LRU_BLOCKS, LRU_BW, LRU_BW)), _full((LRU_BLOCKS, LRU_BW, LRU_BW)),
                  _full((1, D_MODEL)), _full((1, D_MODEL)), _full((1, D_MODEL)),
                  _full((4, D_MODEL)), _full((1, D_MODEL))],
        out_specs=[pl.BlockSpec((tm, D_MODEL), lambda i: (i, 0)),
                   pl.BlockSpec((None, tm, D_MODEL), lambda i: (0, i, 0))],
        out_shape=[jax.ShapeDtypeStruct((t, D_MODEL), F32),
                   jax.ShapeDtypeStruct((2, t, D_MODEL), _MXU_DTYPE)],
        scratch_shapes=[pltpu.VMEM((tm, D_MODEL), F32), pltpu.VMEM((tm, D_MODEL), F32),
                        pltpu.VMEM((8, D_MODEL), F32)],
        compiler_params=_seq_params(),
    )(p, p, p, wa, wx, ba, bx, lam, cw, cb)


def _hgrn_fwd(p, logits, gw, y2, tm):
    t = p.shape[1]
    nc = tm // CHUNK

    def body(qf_ref, vh_ref, lg_ref, gw_ref, y_in, y_ref, o_ref, st_ref, state):
        del y_in
        i = pl.program_id(0)

        @pl.when(i == 0)
        def _():
            state[...] = jnp.zeros_like(state)

        l0 = lg_ref[0:1, :]
        l1 = lg_ref[1:2, :]
        lb = 1.0 / (1.0 + jnp.exp(l1 - l0))
        gwv = gw_ref[...]
        causal = _rows((CHUNK, CHUNK)) >= _cols((CHUNK, CHUNK))

        for c in range(nc):
            rows = pl.ds(c * CHUNK, CHUNK)
            v = _seg(vh_ref, 0, rows)
            hg = _seg(vh_ref, 1, rows)
            _, _, _, _, _, qm, km, ebm, ecm, ebc = _hgrn_prep(_seg(qf_ref, 0, rows), _seg(qf_ref, 1, rows), lb)
            ke = _mx(km.astype(F32) * ecm)
            gate = gwv * (hg * _sigmoid(hg))
            o_parts, y_parts = [], []
            for h in range(HEADS):
                ln = slice(h * HEAD_D, (h + 1) * HEAD_D)
                s0 = state[h]
                st_ref[c, h] = s0
                vt = _mx(v[:, ln].T)
                att = _mx(jnp.where(causal, _dot_nt(qm[:, ln], km[:, ln]), 0.0))
                o = _dot_nt(_pad_lanes([qm[:, ln], att], 2 * HEAD_D),
                            _pad_lanes([_mx(s0 * ebm[:, ln]), vt], 2 * HEAD_D))
                state[h] = s0 * ebc[:, ln] + jnp.dot(vt, ke[:, ln], preferred_element_type=F32)
                rstd = lax.rsqrt(jnp.mean(o * o, axis=-1, keepdims=True) + EPS)
                o_parts.append(o)
                y_parts.append((o * rstd * gate[:, ln]).astype(_MXU_DTYPE))
            o_ref[rows, :] = jnp.concatenate(o_parts, axis=1)
            y_ref[rows, :] = jnp.concatenate(y_parts, axis=1)

    pair = lambda s: pl.BlockSpec((4, tm, UNIT), lambda i: (s, i, 0))
    return pl.pallas_call(
        body, name="hgrn_fwd", grid=(t // tm,),
        in_specs=[pair(1), pair(2), _full((2, D_MODEL)), _full((1, D_MODEL)),
                  pl.BlockSpec(memory_space=pl.ANY)],
        out_specs=[pl.BlockSpec((None, tm, D_MODEL), lambda i: (1, i, 0)),
                   pl.BlockSpec((tm, D_MODEL), lambda i: (i, 0)),
                   pl.BlockSpec((nc, HEADS, HEAD_D, HEAD_D), lambda i: (i, 0, 0, 0))],
        out_shape=[jax.ShapeDtypeStruct((2, t, D_MODEL), _MXU_DTYPE),
                   jax.ShapeDtypeStruct((t, D_MODEL), F32),
                   jax.ShapeDtypeStruct((t // CHUNK, HEADS, HEAD_D, HEAD_D), F32)],
        scratch_shapes=[pltpu.VMEM((HEADS, HEAD_D, HEAD_D), F32)],
        input_output_aliases={4: 0},
        compiler_params=_seq_params(),
    )(p, p, logits, gw, y2)


def _out_fused(y2, wo, x, tgt, wpost, tm):
    t = x.shape[0]
    inv_d = 1.0 / D_MODEL

    def body(y_ref, wo_ref, x_ref, t_ref, w_ref, dout_ref, dy_ref, sq_ref, dw_ref):
        i = pl.program_id(0)

        @pl.when(i == 0)
        def _():
            sq_ref[...] = jnp.zeros_like(sq_ref)
            dw_ref[...] = jnp.zeros_like(dw_ref)

        y = (jnp.dot(y_ref[0], wo_ref[0:D_MODEL, :], preferred_element_type=F32)
             + jnp.dot(y_ref[1], wo_ref[D_MODEL:2 * D_MODEL, :], preferred_element_type=F32))
        w = w_ref[...]
        rstd = lax.rsqrt(jnp.mean(y * y, axis=-1, keepdims=True) + EPS)
        n = y * rstd
        err = (x_ref[...] + n * w) - t_ref[...]
        sq_ref[...] += jnp.sum(err * err, axis=0, keepdims=True)
        dout = err * inv_d
        dout_ref[...] = dout
        dw_ref[...] += jnp.sum(dout * n, axis=0, keepdims=True)
        dn = dout * w
        dy_ref[...] = (rstd * (dn - n * jnp.mean(dn * n, axis=-1, keepdims=True))).astype(_MXU_DTYPE)

    row = pl.BlockSpec((tm, D_MODEL), lambda i: (i, 0))
    return pl.pallas_call(
        body, name="out_fused", grid=(t // tm,),
        in_specs=[pl.BlockSpec((2, tm, D_MODEL), lambda i: (0, i, 0)), _full((2 * D_MODEL, D_MODEL)),
                  row, row, _full((1, D_MODEL))],
        out_specs=[row, row, _full((1, D_MODEL)), _full((1, D_MODEL))],
        out_shape=[jax.ShapeDtypeStruct((t, D_MODEL), F32), jax.ShapeDtypeStruct((t, D_MODEL), _MXU_DTYPE),
                   jax.ShapeDtypeStruct((1, D_MODEL), F32), jax.ShapeDtypeStruct((1, D_MODEL), F32)],
        compiler_params=_seq_params(),
    )(y2, wo, x, tgt, wpost)


def _dw_out(y2, dy, tt):
    t = dy.shape[0]
    hr = UNIT // 2

    def body(y_ref, dy_ref, g_ref):
        @pl.when(pl.program_id(1) == 0)
        def _():
            g_ref[...] = jnp.zeros_like(g_ref)

        r = _dot_tn(y_ref[...], dy_ref[...])
        g_ref[0] += r[0:hr]
        g_ref[1] += r[hr:UNIT]

    return pl.pallas_call(
        body, name="dw_out", grid=(N_CHIPS, t // tt),
        in_specs=[pl.BlockSpec((None, tt, UNIT), lambda c, k: (c // 2, k, c % 2)),
                  pl.BlockSpec((tt, D_MODEL), lambda c, k: (k, 0))],
        out_specs=pl.BlockSpec((2, None, hr, D_MODEL), lambda c, k: (0, c, 0, 0)),
        out_shape=jax.ShapeDtypeStruct((2, N_CHIPS, hr, D_MODEL), F32),
        compiler_params=pltpu.CompilerParams(dimension_semantics=("arbitrary", "arbitrary")),
    )(y2, dy)


def _lru_bwd(dy, wo, p, h, wa, wx, ba, bx, lam, cw, cb, tm):
    t = dy.shape[0]
    nt = t // tm
    ng = tm // 8

    def body(dy_ref, wo_ref, lx_ref, halo_ref, lg_ref, h_ref, hhalo_ref, wa_ref, wx_ref, ba_ref, bx_ref,
             lam_ref, cw_ref, cb_ref,
             dp_ref, dwa_ref, dwx_ref, dba_ref, dbx_ref, dsp_ref, dcw_ref, dcb_ref,
             c_s, in_s, dh_s, dhc_s, afirst_s, dxc_s):
        i = pl.program_id(0)
        first = i == nt - 1

        @pl.when(i == 0)
        def _():
            for ref in (dwa_ref, dwx_ref, dba_ref, dbx_ref, dsp_ref, dcw_ref, dcb_ref, dhc_s, afirst_s, dxc_s):
                ref[...] = jnp.zeros_like(ref)

        sp = _softplus_neg(lam_ref[...])
        cwv = cw_ref[...]
        wav = wa_ref[...]
        wxv = wx_ref[...]
        halo = jnp.where(first, 0.0, _seg(halo_ref, 0))
        xs, xc, r, ig, a, om = _lru_gates(_seg(lx_ref, 0), halo, cwv, cb_ref[...], wav, wxv,
                                          ba_ref[...], bx_ref[...], sp)
        inv_mult = lax.rsqrt(om)
        mult = om * inv_mult
        d_y = _dot_nt(dy_ref[...], wo_ref[...])
        lg = _seg(lg_ref, 0)
        sl = _sigmoid(lg)
        hv = h_ref[...]
        d_lg = d_y * hv * (sl * (1.0 + lg * (1.0 - sl)))

        c_s[...] = _shift_up(a, afirst_s[...], 1)
        in_s[...] = d_y * (lg * sl)
        row8 = _rows((8, D_MODEL))

        def group(gg, carry):
            rows = pl.ds(pl.multiple_of((ng - 1 - gg) * 8, 8), 8)
            cv = c_s[rows, :]
            uv = in_s[rows, :]
            for s in (1, 2, 4):
                c_sh = jnp.where(row8 < 8 - s, pltpu.roll(cv, 8 - s, 0), 1.0)
                u_sh = jnp.where(row8 < 8 - s, pltpu.roll(uv, 8 - s, 0), 0.0)
                uv = cv * u_sh + uv
                cv = cv * c_sh
            hh = cv * carry + uv
            dh_s[rows, :] = hh
            return jnp.broadcast_to(hh[0:1, :], (8, D_MODEL))

        dhc_s[...] = lax.fori_loop(0, ng, group, dhc_s[...])
        afirst_s[...] = a[0:8]
        dh = dh_s[...]

        h_prev = _shift_down(hv, jnp.where(first, 0.0, hhalo_ref[...]), 1)
        gx = ig * xc
        d_mult = dh * gx
        d_i = dh * (mult * xc)
        d_xc = dh * (mult * ig)
        d_la = (dh * h_prev) * a - d_mult * ((a * a) * inv_mult)
        dsp_ref[...] += jnp.sum(d_la * r, axis=0, keepdims=True) * (-LRU_C)
        d_zr = (d_la * (-LRU_C * sp)) * (r * (1.0 - r))
        d_zi = d_i * (ig * (1.0 - ig))
        dba_ref[...] += jnp.sum(d_zr, axis=0, keepdims=True)
        dbx_ref[...] += jnp.sum(d_zi, axis=0, keepdims=True)
        back = []
        for n in range(LRU_BLOCKS):
            ln = slice(n * LRU_BW, (n + 1) * LRU_BW)
            xb = _mx(xc[:, ln])
            zr_n = _mx(d_zr[:, ln])
            zi_n = _mx(d_zi[:, ln])
            dwa_ref[n] += _dot_tn(xb, zr_n)
            dwx_ref[n] += _dot_tn(xb, zi_n)
            back.append(_dot_nt(zr_n, wav[n]) + _dot_nt(zi_n, wxv[n]))
        d_xc = d_xc + jnp.concatenate(back, axis=1)
        dcb_ref[...] += jnp.sum(d_xc, axis=0, keepdims=True)
        for k in range(4):
            dcw_ref[k:k + 1, :] += jnp.sum(d_xc * xs[k], axis=0, keepdims=True)
        head = dxc_s[...]
        d_lx = cwv[3:4] * d_xc
        for k in range(3):
            d_lx = d_lx + cwv[k:k + 1] * _shift_up(d_xc, head, 3 - k)
        dxc_s[...] = d_xc[0:8]
        dp_ref[0] = d_lx.astype(_MXU_DTYPE)
        dp_ref[1] = d_lg.astype(_MXU_DTYPE)

    rev = lambda i: nt - 1 - i
    seg = lambda s: pl.BlockSpec((2, tm, UNIT), lambda i: (s, rev(i), 0))
    halo_row = lambda i: jnp.maximum(rev(i) * (tm // 8) - 1, 0)
    row = pl.BlockSpec((tm, D_MODEL), lambda i: (rev(i), 0))
    vec = _full((1, D_MODEL))
    wblk = _full((LRU_BLOCKS, LRU_BW, LRU_BW))
    return pl.pallas_call(
        body, name="lru_bwd", grid=(nt,),
        in_specs=[row, pl.BlockSpec((D_MODEL, D_MODEL), lambda i: (0, 0)),
                  seg(0), pl.BlockSpec((2, 8, UNIT), lambda i: (0, halo_row(i), 0)), seg(1),
                  row, pl.BlockSpec((8, D_MODEL), lambda i: (halo_row(i), 0)),
                  wblk, wblk, vec, vec, vec, _full((4, D_MODEL)), vec],
        out_specs=[pl.BlockSpec((2, tm, D_MODEL), lambda i: (2, rev(i), 0)),
                   wblk, wblk, vec, vec, vec, _full((4, D_MODEL)), vec],
        out_shape=[jax.ShapeDtypeStruct((N_SEG, t, D_MODEL), _MXU_DTYPE),
                   jax.ShapeDtypeStruct((LRU_BLOCKS, LRU_BW, LRU_BW), F32),
                   jax.ShapeDtypeStruct((LRU_BLOCKS, LRU_BW, LRU_BW), F32),
                   jax.ShapeDtypeStruct((1, D_MODEL), F32), jax.ShapeDtypeStruct((1, D_MODEL), F32),
                   jax.ShapeDtypeStruct((1, D_MODEL), F32), jax.ShapeDtypeStruct((4, D_MODEL), F32),
                   jax.ShapeDtypeStruct((1, D_MODEL), F32)],
        scratch_shapes=[pltpu.VMEM((tm, D_MODEL), F32), pltpu.VMEM((tm, D_MODEL), F32),
                        pltpu.VMEM((tm, D_MODEL), F32), pltpu.VMEM((8, D_MODEL), F32),
                        pltpu.VMEM((8, D_MODEL), F32), pltpu.VMEM((8, D_MODEL), F32)],
        compiler_params=_seq_params(),
    )(dy, wo, p, p, p, h, h, wa, wx, ba, bx, lam, cw, cb)


def _hgrn_bwd(dy, wo, p, o, states, logits, gw, dp, tm):
    t = dy.shape[0]
    nt = t // tm
    nc = tm // CHUNK

    def body(dy_ref, wo_ref, qf_ref, vh_ref, o_ref, st_ref, lg_ref, gw_ref, dp_in,
             dp_ref, dgw_ref, dlb_ref, dyh_s, dstate, snext):
        del dp_in
        i = pl.program_id(0)

        @pl.when(i == 0)
        def _():
            for ref in (dgw_ref, dlb_ref, dstate, snext):
                ref[...] = jnp.zeros_like(ref)

        dyh_s[...] = _dot_nt(dy_ref[...], wo_ref[...])
        l0 = lg_ref[0:1, :]
        l1 = lg_ref[1:2, :]
        lb = 1.0 / (1.0 + jnp.exp(l1 - l0))
        gwv = gw_ref[...]
        causal = _rows((CHUNK, CHUNK)) >= _cols((CHUNK, CHUNK))
        wide = 2 * HEAD_D
        zeros_h = jnp.zeros((CHUNK, HEAD_D), _MXU_DTYPE)

        for c in reversed(range(nc)):
            rows = pl.ds(c * CHUNK, CHUNK)
            q = _seg(qf_ref, 0, rows)
            hg = _seg(vh_ref, 1, rows)
            ov = o_ref[rows, :]
            dyh = dyh_s[rows, :]
            sig, f, sq, em1, em2, qm, km, ebm, ecm, ebc = _hgrn_prep(q, _seg(qf_ref, 1, rows), lb)
            ke = _mx(km.astype(F32) * ecm)
            vb = _mx(_seg(vh_ref, 0, rows))
            sg = _sigmoid(hg)
            d_on = dyh * (hg * sg)
            x_p, y_p, dv_p, n_p, ex_p = [], [], [], [], []
            for h in range(HEADS):
                ln = slice(h * HEAD_D, (h + 1) * HEAD_D)
                o_h = ov[:, ln]
                rstd = lax.rsqrt(jnp.mean(o_h * o_h, axis=-1, keepdims=True) + EPS)
                n = o_h * rstd
                dn = d_on[:, ln] * gwv[:, ln]
                d_o = rstd * (dn - n * jnp.mean(dn * n, axis=-1, keepdims=True))
                d_ob = _mx(d_o)
                d_ot = _mx(d_o.T)
                s0 = st_ref[c, h]
                ds = dstate[h]
                both = _dot_nt(jnp.concatenate([_pad_lanes([qm[:, ln]], wide), _pad_lanes([zeros_h, d_ob], wide)], axis=0),
                               jnp.concatenate([km[:, ln], vb[:, ln]], axis=1))
                att = jnp.where(causal, both[0:CHUNK], 0.0)
                d_att = jnp.where(causal, both[CHUNK:2 * CHUNK], 0.0)
                x_p.append(jnp.dot(_pad_lanes([d_ob, _mx(d_att)], wide),
                                   _pad_rows([_mx(s0 * ebm[:, ln]), km[:, ln]], wide), preferred_element_type=F32))
                ds_c = ds * ecm[:, ln]
                y_p.append(jnp.dot(_pad_lanes([vb[:, ln], _mx(d_att.T)], wide),
                                   _pad_rows([_mx(ds_c), qm[:, ln]], wide), preferred_element_type=F32))
                dv_p.append(_dot_nt(_pad_lanes([ke[:, ln], _mx(att.T)], wide), _pad_lanes([_mx(ds), d_ot], wide)))
                ex_p.append(jnp.sum(ds * snext[h], axis=0, keepdims=True))
                dstate[h] = (ds_c + jnp.dot(d_ot, qm[:, ln], preferred_element_type=F32)) * ebm[:, ln]
                snext[h] = s0
                n_p.append(n)
            cat = lambda parts: jnp.concatenate(parts, axis=1)
            x_all, y_all, n_all = cat(x_p), cat(y_p), cat(n_p)
            dg = _cumsum_rows(qm.astype(F32) * x_all - km.astype(F32) * y_all, reverse=True) + cat(ex_p)
            df = dg / f - em2 * y_all
            dgw_ref[...] += jnp.sum(d_on * n_all, axis=0, keepdims=True)
            dlb_ref[...] += jnp.sum(df * (1.0 - sig), axis=0, keepdims=True)
            dp_ref[0, rows, :] = ((em1 * x_all) * (sq * (1.0 + q * (1.0 - sq)))).astype(_MXU_DTYPE)
            dp_ref[1, rows, :] = (df * ((1.0 - lb) * sig * (1.0 - sig))).astype(_MXU_DTYPE)
            dp_ref[2, rows, :] = cat(dv_p).astype(_MXU_DTYPE)
            dp_ref[3, rows, :] = (dyh * (n_all * gwv) * (sg * (1.0 + hg * (1.0 - sg)))).astype(_MXU_DTYPE)

    rev = lambda i: nt - 1 - i
    pair = lambda s: pl.BlockSpec((4, tm, UNIT), lambda i: (s, rev(i), 0))
    row = pl.BlockSpec((tm, D_MODEL), lambda i: (rev(i), 0))
    vec = _full((1, D_MODEL))
    return pl.pallas_call(
        body, name="hgrn_bwd", grid=(nt,),
        in_specs=[row, pl.BlockSpec((D_MODEL, D_MODEL), lambda i: (1, 0)), pair(1), pair(2), row,
                  pl.BlockSpec((nc, HEADS, HEAD_D, HEAD_D), lambda i: (rev(i), 0, 0, 0)),
                  _full((2, D_MODEL)), vec, pl.BlockSpec(memory_space=pl.ANY)],
        out_specs=[pl.BlockSpec((4, tm, D_MODEL), lambda i: (0, rev(i), 0)), vec, vec],
        out_shape=[jax.ShapeDtypeStruct((N_SEG, t, D_MODEL), _MXU_DTYPE),
                   jax.ShapeDtypeStruct((1, D_MODEL), F32), jax.ShapeDtypeStruct((1, D_MODEL), F32)],
        scratch_shapes=[pltpu.VMEM((tm, D_MODEL), F32), pltpu.VMEM((HEADS, HEAD_D, HEAD_D), F32),
                        pltpu.VMEM((HEADS, HEAD_D, HEAD_D), F32)],
        input_output_aliases={8: 0},
        compiler_params=_seq_params(),
    )(dy, wo, p, p, o, states, logits, gw, dp)


def _dp_unit(m):
    return ((m // 2 + 2) % N_SEG) * 2 + m % 2


def _dw_in(ut, dp, tt):
    t = ut.shape[1]
    hr = D_MODEL // 2

    def body(ut_ref, dp_ref, g_ref):
        @pl.when(pl.program_id(1) == 0)
        def _():
            g_ref[...] = jnp.zeros_like(g_ref)

        r = jnp.dot(ut_ref[...], dp_ref[...], preferred_element_type=F32)
        for h in range(2):
            for half in range(2):
                g_ref[h, half] += r[h * hr:(h + 1) * hr, half * UNIT:(half + 1) * UNIT]

    return pl.pallas_call(
        body, name="dw_in", grid=(N_SEG, t // tt),
        in_specs=[pl.BlockSpec((D_MODEL, tt), lambda s, k: (0, k)),
                  pl.BlockSpec((None, tt, D_MODEL), lambda s, k: (s, k, 0))],
        out_specs=pl.BlockSpec((2, 2, hr, UNIT), lambda s, k: (0, (s + 2) % N_SEG, 0, 0)),
        out_shape=jax.ShapeDtypeStruct((2, N_UNIT, hr, UNIT), F32),
        compiler_params=pltpu.CompilerParams(dimension_semantics=("arbitrary", "arbitrary")),
    )(ut, dp)


def _dx(dp, wg, x, dout, wpre, tm, parts, scatter):
    t = x.shape[0]
    n = len(parts)
    nt = t // tm

    def body(*refs):
        dp_ref, w_ref, x_ref, dout_ref, wpre_ref = refs[:5]
        ins = refs[5:5 + n]
        gx_ref, dw_ref = refs[5 + n:7 + n]
        outs = refs[7 + n:7 + 2 * n]
        send_sems, recv_sems = refs[7 + 2 * n:]
        i = pl.program_id(0)

        @pl.when(i == 0)
        def _():
            dw_ref[...] = jnp.zeros_like(dw_ref)
            for cp in _chip_copies(ins, outs, send_sems, recv_sems, scatter):
                cp.start()

        du = None
        for m in range(N_UNIT):
            part = _dot_nt(dp_ref[m // 2, :, (m % 2) * UNIT:(m % 2 + 1) * UNIT], w_ref[_dp_unit(m)])
            du = part if du is None else du + part
        xv = x_ref[...]
        rstd = lax.rsqrt(jnp.mean(xv * xv, axis=-1, keepdims=True) + EPS)
        xn = xv * rstd
        dw_ref[...] += jnp.sum(du * xn, axis=0, keepdims=True)
        dn = du * wpre_ref[...]
        gx_ref[...] = dout_ref[...] + rstd * (dn - xn * jnp.mean(dn * xn, axis=-1, keepdims=True))

        @pl.when(i == nt - 1)
        def _():
            copies = _chip_copies(ins, outs, send_sems, recv_sems, scatter)
            for cp in copies:
                cp.wait_recv()
            for cp in copies:
                cp.wait_send()

    row = pl.BlockSpec((tm, D_MODEL), lambda i: (i, 0))
    any_spec = pl.BlockSpec(memory_space=pl.ANY)
    res = pl.pallas_call(
        body, name="dx", grid=(nt,),
        in_specs=[pl.BlockSpec((N_SEG, tm, D_MODEL), lambda i: (0, i, 0)), _resident((N_UNIT, D_MODEL, UNIT)),
                  row, row, _full((1, D_MODEL))] + [any_spec] * n,
        out_specs=[row, _full((1, D_MODEL))] + [any_spec] * n,
        out_shape=[jax.ShapeDtypeStruct((t, D_MODEL), F32), jax.ShapeDtypeStruct((1, D_MODEL), F32)]
        + _chip_exchange_shapes(parts, scatter),
        scratch_shapes=[pltpu.SemaphoreType.DMA((max(3 * n, 1),)), pltpu.SemaphoreType.DMA((max(3 * n, 1),))],
        compiler_params=_seq_params(),
    )(dp, wg, x, dout, wpre, *parts)
    return res[0], res[1], res[2:]


CHUNK_BYTES = 256 * 1024
MAX_CHUNKS = 16


def _n_chunks(shape, dtype, rows):
    nbytes = jnp.dtype(dtype).itemsize
    for d in shape:
        nbytes *= d
    k = max(1, min(MAX_CHUNKS, rows, nbytes // CHUNK_BYTES))
    while rows % k:
        k -= 1
    return k


def _chip_exchange_shapes(arrays, scatter):
    return [jax.ShapeDtypeStruct((3,) + tuple(a.shape[1:] if sc else a.shape), a.dtype)
            for a, sc in zip(arrays, scatter)]


def _chip_copies(ins, outs, send_sems, recv_sems, scatter):
    if not ins:
        return []
    x, y, c = lax.axis_index("x"), lax.axis_index("y"), lax.axis_index("c")
    peers = [(1 - x, y), (x, 1 - y), (1 - x, 1 - y)]
    copies = []
    for a in range(len(ins)):
        for j, (px, py) in enumerate(peers):
            copies.append(pltpu.make_async_remote_copy(
                src_ref=ins[a].at[2 * px + py] if scatter[a] else ins[a], dst_ref=outs[a].at[j],
                send_sem=send_sems.at[a * 3 + j], recv_sem=recv_sems.at[a * 3 + j],
                device_id=(px, py, c), device_id_type=MESH))
    return copies


def _core_swap(arrays, name, halved):
    n = len(arrays)
    shapes, n_sems = _core_swap_plan(arrays, halved)

    def body(*refs):
        copies = _core_swap_copies(arrays, halved, refs[:n], refs[n:2 * n], refs[2 * n], refs[2 * n + 1])
        for cp in copies:
            cp.start()
        for cp in copies:
            cp.wait_recv()
        for cp in copies:
            cp.wait_send()

    any_spec = pl.BlockSpec(memory_space=pl.ANY)
    return pl.pallas_call(
        body, name=name,
        in_specs=[any_spec] * n, out_specs=[any_spec] * n,
        out_shape=shapes,
        scratch_shapes=[pltpu.SemaphoreType.DMA((n_sems,)), pltpu.SemaphoreType.DMA((n_sems,))],
    )(*arrays)


def _core_swap_plan(arrays, halved):
    shapes = [a.shape[1:] if halved else a.shape for a in arrays]
    n_sems = sum(_n_chunks(s, a.dtype, s[0]) for s, a in zip(shapes, arrays))
    return [jax.ShapeDtypeStruct(s, a.dtype) for s, a in zip(shapes, arrays)], max(n_sems, 1)


def _core_swap_copies(arrays, halved, ins, outs, send_sems, recv_sems):
    if not arrays:
        return []
    x, y, c = lax.axis_index("x"), lax.axis_index("y"), lax.axis_index("c")
    copies, sem = [], 0
    for a, arr in enumerate(arrays):
        shape = arr.shape[1:] if halved else arr.shape
        k = _n_chunks(shape, arr.dtype, shape[0])
        step = shape[0] // k
        for j in range(k):
            rows = pl.ds(j * step, step)
            copies.append(pltpu.make_async_remote_copy(
                src_ref=ins[a].at[1 - c, rows] if halved else ins[a].at[rows], dst_ref=outs[a].at[rows],
                send_sem=send_sems.at[sem], recv_sem=recv_sems.at[sem],
                device_id=(x, y, 1 - c), device_id_type=MESH))
            sem += 1
    return copies


def _finish_exchange(arrays, small):
    n = len(arrays)
    ks = [_n_chunks(a.shape, a.dtype, a.shape[0]) for a in arrays]
    offs = [sum(ks[:a]) for a in range(n)]
    total = sum(ks)

    def body(*refs):
        ins, small_ref = refs[:n], refs[n]
        outs, gathered = refs[n + 1:2 * n + 1], refs[2 * n + 1]
        send_sems, recv_sems = refs[2 * n + 2:]
        x, y, c = lax.axis_index("x"), lax.axis_index("y"), lax.axis_index("c")
        copies = []
        for a in range(n):
            step = arrays[a].shape[0] // ks[a]
            for j in range(ks[a]):
                rows = pl.ds(j * step, step)
                copies.append(pltpu.make_async_remote_copy(
                    src_ref=ins[a].at[rows], dst_ref=outs[a].at[rows],
                    send_sem=send_sems.at[offs[a] + j], recv_sem=recv_sems.at[offs[a] + j],
                    device_id=(x, y, 1 - c), device_id_type=MESH))
        for slot in range(7):
            dx, dy, dc = (slot + 1) % 2, ((slot + 1) // 2) % 2, (slot + 1) // 4
            copies.append(pltpu.make_async_remote_copy(
                src_ref=small_ref, dst_ref=gathered.at[slot],
                send_sem=send_sems.at[total + slot], recv_sem=recv_sems.at[total + slot],
                device_id=(x + dx - 2 * x * dx, y + dy - 2 * y * dy, c + dc - 2 * c * dc), device_id_type=MESH))
        for cp in copies:
            cp.start()
        for cp in copies:
            cp.wait_recv()
        for cp in copies:
            cp.wait_send()

    any_spec = pl.BlockSpec(memory_space=pl.ANY)
    res = pl.pallas_call(
        body, name="reduce_cores",
        in_specs=[any_spec] * (n + 1), out_specs=[any_spec] * (n + 1),
        out_shape=[jax.ShapeDtypeStruct(a.shape, a.dtype) for a in arrays]
        + [jax.ShapeDtypeStruct((7,) + small.shape, small.dtype)],
        scratch_shapes=[pltpu.SemaphoreType.DMA((total + 7,)), pltpu.SemaphoreType.DMA((total + 7,))],
    )(*arrays, small)
    return res[:n], res[n]


def _sum_own_half(g, b, name, out_dtype, tr=2048):
    _, rows, cols = g.shape
    tr = min(tr, rows)
    core = lax.axis_index("c").reshape(1)

    def body(c_ref, g_ref, b_ref, o_ref):
        del c_ref
        o_ref[...] = (g_ref[...] + b_ref[...]).astype(out_dtype)

    return pl.pallas_call(
        body, name=name,
        grid_spec=pltpu.PrefetchScalarGridSpec(
            num_scalar_prefetch=1, grid=(rows // tr,),
            in_specs=[pl.BlockSpec((None, tr, cols), lambda i, c_ref: (c_ref[0], i, 0)),
                      pl.BlockSpec((tr, cols), lambda i, c_ref: (i, 0))],
            out_specs=pl.BlockSpec((tr, cols), lambda i, c_ref: (i, 0))),
        out_shape=jax.ShapeDtypeStruct((rows, cols), out_dtype),
        compiler_params=pltpu.CompilerParams(dimension_semantics=("parallel",)),
    )(core, g, b)


def _sum_chips(own, r, slabbed, name, tr=2048):
    _, rows, cols = r.shape
    tr = min(tr, rows)
    chip = (2 * lax.axis_index("x") + lax.axis_index("y")).reshape(1)

    def body(c_ref, own_ref, r_ref, o_ref):
        del c_ref
        f = lambda val: val.astype(F32)
        o_ref[...] = (f(own_ref[...]) + f(r_ref[0])) + (f(r_ref[1]) + f(r_ref[2]))

    if slabbed:
        own_spec = pl.BlockSpec((None, tr, cols), lambda i, c_ref: (c_ref[0], i, 0))
    else:
        own_spec = pl.BlockSpec((tr, cols), lambda i, c_ref: (i, 0))
    return pl.pallas_call(
        body, name=name,
        grid_spec=pltpu.PrefetchScalarGridSpec(
            num_scalar_prefetch=1, grid=(rows // tr,),
            in_specs=[own_spec, pl.BlockSpec((3, tr, cols), lambda i, c_ref: (0, i, 0))],
            out_specs=pl.BlockSpec((tr, cols), lambda i, c_ref: (i, 0))),
        out_shape=jax.ShapeDtypeStruct((rows, cols), F32),
        compiler_params=pltpu.CompilerParams(dimension_semantics=("parallel",)),
    )(chip, own, r)


def _adamw_big(own, other, w, m, v, name, grid, g_block, g_index, w_block, w_index):
    core = lax.axis_index("c").reshape(1)
    half_axis = len(grid) - 1

    def body(c_ref, own_ref, oth_ref, w_ref, m_ref, v_ref, g_ref, d_ref, nm_ref, nv_ref):
        mine = pl.program_id(half_axis) == c_ref[0]
        gv = jnp.where(mine, own_ref[...], oth_ref[...])
        g_ref[...] = gv
        d_ref[...], nm_ref[...], nv_ref[...] = _adamw(w_ref[...], gv, m_ref[...], v_ref[...])

    g_spec = pl.BlockSpec(g_block, lambda *a: g_index(*a[:half_axis]))
    w_spec = pl.BlockSpec(w_block, lambda *a: w_index(*a[:-1]))
    shp = jax.ShapeDtypeStruct(w.shape, F32)
    return pl.pallas_call(
        body, name=name,
        grid_spec=pltpu.PrefetchScalarGridSpec(
            num_scalar_prefetch=1, grid=grid,
            in_specs=[g_spec, g_spec, w_spec, w_spec, w_spec], out_specs=[w_spec] * 4),
        out_shape=[shp] * 4,
        compiler_params=pltpu.CompilerParams(dimension_semantics=("parallel",) * len(grid)),
    )(core, own, other, w, m, v)


def _small_finish(own, others, lam, logits):
    def body(a_ref, b_ref, lam_ref, lg_ref, o_ref):
        g = (((a_ref[...] + b_ref[0]) + (b_ref[1] + b_ref[2]))
             + ((b_ref[3] + b_ref[4]) + (b_ref[5] + b_ref[6])))
        o_ref[...] = g
        o_ref[8:9, :] = g[8:9, :] * (-_sigmoid(-lam_ref[...]))
        lb = 1.0 / (1.0 + jnp.exp(lg_ref[1:2, :] - lg_ref[0:1, :]))
        d0 = g[9:10, :] * (lb * (1.0 - lb))
        o_ref[9:10, :] = d0
        o_ref[10:11, :] = -d0

    return pl.pallas_call(
        body, name="small_finish",
        in_specs=[_full((SMALL_ROWS, D_MODEL)), _full((7, SMALL_ROWS, D_MODEL)), _full((1, D_MODEL)),
                  _full((2, D_MODEL))],
        out_specs=_full((SMALL_ROWS, D_MODEL)),
        out_shape=jax.ShapeDtypeStruct((SMALL_ROWS, D_MODEL), F32),
    )(own, others, lam, logits)


def _adamw_small(ws, gs, ms, vs):
    n = len(ws)

    def body(*refs):
        w_r, g_r, m_r, v_r = refs[:n], refs[n:2 * n], refs[2 * n:3 * n], refs[3 * n:4 * n]
        d_o, m_o, v_o = refs[4 * n:5 * n], refs[5 * n:6 * n], refs[6 * n:7 * n]
        for j in range(n):
            d_o[j][...], m_o[j][...], v_o[j][...] = _adamw(w_r[j][...], g_r[j][...], m_r[j][...], v_r[j][...])

    specs = [_full(w.shape) for w in ws]
    shapes = [jax.ShapeDtypeStruct(w.shape, F32) for w in ws]
    outs = pl.pallas_call(
        body, name="adamw_small",
        in_specs=specs * 4, out_specs=specs * 3, out_shape=shapes * 3,
    )(*ws, *gs, *ms, *vs)
    return outs[:n], outs[n:2 * n], outs[2 * n:]


def _local_step(x, tgt, ut, p, pre_w, wg, cw, cb, wa, wx, ba, bx, lam, logits, gw, wo, post_w, tm, tmm,
                prepare=None):
    h, y2 = _lru_fwd(p, wa, wx, ba, bx, lam, cw, cb, tmm)
    y2, o, states = _hgrn_fwd(p, logits, gw, y2, tmm)
    dout, dy, sq, d_post = _out_fused(y2, wo, x, tgt, post_w, tmm)
    g_out = _dw_out(y2, dy, min(2048, x.shape[0]))
    dp, d_wa, d_wx, d_ba, d_bx, d_sp, d_cw, d_cb = _lru_bwd(dy, wo, p, h, wa, wx, ba, bx, lam, cw, cb, tmm)
    dp, d_gw, d_lb = _hgrn_bwd(dy, wo, p, o, states, logits, gw, dp, tmm)
    g_in = _dw_in(ut, dp, min(2048, x.shape[0]))
    parts, scatter = prepare(g_in, g_out, d_wa, d_wx) if prepare is not None else ([], [])
    grad_x, d_pre, received = _dx(dp, wg, x, dout, pre_w, tmm, parts, scatter)
    small = jnp.concatenate([d_pre, d_cw, d_cb, d_ba, d_bx, d_sp, d_lb,
                             jnp.zeros((1, D_MODEL), F32), d_gw, d_post, sq,
                             jnp.zeros((SMALL_ROWS - 14, D_MODEL), F32)], axis=0)
    return grad_x, g_in, g_out, d_wa, d_wx, small, parts, received


def kernel(x, pre_norm_w, w_in, conv_w, conv_b, lru_w_a, lru_b_a, lru_w_x, lru_b_x, lru_lambda, hgrn_lb_logits, hgrn_gnorm_w, w_out, post_norm_w, loss_target, m_pre_norm_w, m_w_in, m_conv_w, m_conv_b, m_lru_w_a, m_lru_b_a, m_lru_w_x, m_lru_b_x, m_lru_lambda, m_hgrn_lb_logits, m_hgrn_gnorm_w, m_w_out, m_post_norm_w, v_pre_norm_w, v_w_in, v_conv_w, v_conv_b, v_lru_w_a, v_lru_b_a, v_lru_w_x, v_lru_b_x, v_lru_lambda, v_hgrn_lb_logits, v_hgrn_gnorm_w, v_w_out, v_post_norm_w):
    t = x.shape[1]
    tm = min(256, t)
    tmm = min(512, t)
    chip = 2 * lax.axis_index("x") + lax.axis_index("y")

    win_units = w_in[0].astype(_MXU_DTYPE).reshape(D_MODEL, 3, UNIT).transpose(1, 0, 2)
    small_w = jnp.concatenate([conv_w[0], lru_b_a[0], lru_b_x[0]], axis=1)
    ut, p, wg4, (wo4, wa4, wx4), sw4 = _in_proj_gather(
        x[0], pre_norm_w, win_units,
        [w_out[0].astype(_MXU_DTYPE).reshape(2, 256, D_MODEL), lru_w_a[0].astype(_MXU_DTYPE).reshape(2, 128, LRU_BW),
         lru_w_x[0].astype(_MXU_DTYPE).reshape(2, 128, LRU_BW)],
        small_w, min(1024, t))
    wg = wg4.reshape(N_UNIT, D_MODEL, UNIT)
    wo = wo4.reshape(2 * D_MODEL, D_MODEL)
    by_block = lambda w4: w4.reshape(N_CHIPS, LRU_BLOCKS, 64, LRU_BW).transpose(1, 0, 2, 3).reshape(
        LRU_BLOCKS, LRU_BW, LRU_BW)
    wa, wx = by_block(wa4), by_block(wx4)
    sw4 = sw4.reshape(N_CHIPS, 4, 384)
    cw = sw4[:, :, 0:256].transpose(1, 0, 2).reshape(4, D_MODEL)
    ba = sw4[:, :, 256:320].transpose(1, 0, 2).reshape(1, D_MODEL)
    bx = sw4[:, :, 320:384].transpose(1, 0, 2).reshape(1, D_MODEL)

    def lru_layout(g):
        g = g.reshape(2, 2, N_CHIPS, 64, LRU_BW).transpose(0, 2, 1, 3, 4)
        return g.reshape(2, N_CHIPS * 128, LRU_BW)

    def prepare(g_in, g_out, d_wa, d_wx):
        g_in = g_in.reshape(2, N_UNIT * 512, UNIT)
        g_out = g_out.reshape(2, N_CHIPS * 256, D_MODEL)
        g_wa, g_wx = lru_layout(d_wa), lru_layout(d_wx)
        b_in, b_out, b_wa, b_wx = _core_swap([g_in, g_out, g_wa, g_wx], "reduce_swap", halved=True)
        parts = [_sum_own_half(g_in, b_in, "presum_in", _WIRE_DTYPE).reshape(N_CHIPS, 3 * 512, UNIT),
                 _sum_own_half(g_out, b_out, "presum_out", _WIRE_DTYPE).reshape(N_CHIPS, 256, D_MODEL),
                 _sum_own_half(g_wa, b_wa, "presum_wa", _WIRE_DTYPE).reshape(N_CHIPS, 128, LRU_BW),
                 _sum_own_half(g_wx, b_wx, "presum_wx", _WIRE_DTYPE).reshape(N_CHIPS, 128, LRU_BW)]
        return parts, [True] * 4

    grad_x, _, _, _, _, small, (p_in, p_out, p_wa, p_wx), (r_in, r_out, r_wa, r_wx) = _local_step(
        x[0], loss_target[0], ut, p, pre_norm_w, wg, cw, conv_b, wa, wx, ba, bx, lru_lambda, hgrn_lb_logits,
        hgrn_gnorm_w, wo, post_norm_w, tm, tmm, prepare)
    s_in = _sum_chips(p_in, r_in, True, "sum_in")
    s_out = _sum_chips(p_out, r_out, True, "sum_out")
    s_wa = _sum_chips(p_wa, r_wa, True, "sum_wa")
    s_wx = _sum_chips(p_wx, r_wx, True, "sum_wx")
    (o_in, o_out, o_wa, o_wx), small_others = _finish_exchange([s_in, s_out, s_wa, s_wx], small)

    g_w_in, d_w_in, nm_w_in, nv_w_in = _adamw_big(
        s_in.reshape(3, 512, UNIT), o_in.reshape(3, 512, UNIT), w_in[0], m_w_in[0], v_w_in[0], "adamw_w_in", (3, 2),
        (None, 512, UNIT), lambda k: (k, 0, 0), (512, UNIT), lambda k, h: (h, k))
    g_w_out, d_w_out, nm_w_out, nv_w_out = _adamw_big(
        s_out, o_out, w_out[0], m_w_out[0], v_w_out[0], "adamw_w_out", (2,),
        (256, D_MODEL), lambda: (0, 0), (256, D_MODEL), lambda h: (h, 0))
    sq2 = lambda a: a.reshape(LRU_BW, LRU_BW)
    lru_specs = ((2,), (128, LRU_BW), lambda: (0, 0), (128, LRU_BW), lambda h: (h, 0))
    g_wa, d_wa2, nm_wa, nv_wa = _adamw_big(s_wa, o_wa, sq2(lru_w_a), sq2(m_lru_w_a), sq2(v_lru_w_a), "adamw_wa",
                                           *lru_specs)
    g_wx, d_wx2, nm_wx, nv_wx = _adamw_big(s_wx, o_wx, sq2(lru_w_x), sq2(m_lru_w_x), sq2(v_lru_w_x), "adamw_wx",
                                           *lru_specs)

    gs = _small_finish(small, small_others, lru_lambda, hgrn_lb_logits)
    loss = jnp.sum(gs[13]) * (0.5 / D_MODEL)
    g_pre = gs[0:1]
    g_cw = lax.dynamic_slice(gs[1:5], (0, chip * 256), (4, 256))
    g_cb = gs[5:6]
    g_ba = lax.dynamic_slice(gs[6].reshape(LRU_BLOCKS, N_CHIPS, 64), (0, chip, 0), (LRU_BLOCKS, 1, 64)).reshape(4, 64)
    g_bx = lax.dynamic_slice(gs[7].reshape(LRU_BLOCKS, N_CHIPS, 64), (0, chip, 0), (LRU_BLOCKS, 1, 64)).reshape(4, 64)
    g_lam = gs[8:9]
    g_lb = gs[9:11]
    g_gw = gs[11:12]
    g_post = gs[12:13]
    small_g = [g_pre, g_cw, g_cb, g_ba, g_bx, g_lam, g_lb, g_gw, g_post]
    two_d = lambda a: a.reshape(a.shape[-2:])
    small_w_list = [pre_norm_w, conv_w, conv_b, lru_b_a, lru_b_x, lru_lambda, hgrn_lb_logits, hgrn_gnorm_w, post_norm_w]
    small_m_list = [m_pre_norm_w, m_conv_w, m_conv_b, m_lru_b_a, m_lru_b_x, m_lru_lambda, m_hgrn_lb_logits,
                    m_hgrn_gnorm_w, m_post_norm_w]
    small_v_list = [v_pre_norm_w, v_conv_w, v_conv_b, v_lru_b_a, v_lru_b_x, v_lru_lambda, v_hgrn_lb_logits,
                    v_hgrn_gnorm_w, v_post_norm_w]
    sd, sm, sv = _adamw_small([two_d(a) for a in small_w_list], small_g,
                              [two_d(a) for a in small_m_list], [two_d(a) for a in small_v_list])

    def shaped(vals, refs):
        return [val.reshape(ref.shape) for val, ref in zip(vals, refs)]

    s_g = shaped(small_g, small_w_list)
    s_d = shaped(sd, small_w_list)
    s_m = shaped(sm, small_w_list)
    s_v = shaped(sv, small_w_list)

    def ordered(small, big_in, big_wa, big_wx, big_out):
        pre, cw_, cb_, ba_, bx_, lam_, lb_, gw_, post_ = small
        return [pre, big_in.reshape(w_in.shape), cw_, cb_, big_wa.reshape(lru_w_a.shape), ba_,
                big_wx.reshape(lru_w_x.shape), bx_, lam_, lb_, gw_, big_out.reshape(w_out.shape), post_]

    grads = ordered(s_g, g_w_in, g_wa, g_wx, g_w_out)
    deltas = ordered(s_d, d_w_in, d_wa2, d_wx2, d_w_out)
    new_m = ordered(s_m, nm_w_in, nm_wa, nm_wx, nm_w_out)
    new_v = ordered(s_v, nv_w_in, nv_wa, nv_wx, nv_w_out)
    return (loss, grad_x.reshape(x.shape), *grads, *deltas, *new_m, *new_v)
```

```python
import functools

import jax
import jax.numpy as jnp
from jax import lax
from jax.experimental import pallas as pl
from jax.experimental.pallas import tpu as pltpu

F32 = jnp.float32
_MXU_DTYPE = jnp.bfloat16
_WIRE_DTYPE = jnp.bfloat16

D_MODEL = 1024
N_SEG = 6
UNIT = 512
N_UNIT = 12
LRU_BLOCKS = 4
LRU_BW = 256
LRU_C = 8.0
HEADS = 8
HEAD_D = 128
CHUNK = 64
EPS = 1e-6
N_CHIPS = 4
EXP_CLAMP = 80.0

ADAM_LR = 0.001
ADAM_B1 = 0.9
ADAM_B2 = 0.999
ADAM_EPS = 1e-08
ADAM_WD = 0.01
ADAM_STEP = 10

SMALL_ROWS = 16
FWD_PIECES = 4
ROW_TILE = 512
PROJ_TILE = 1024
GRAD_K_TILE = 2048
MESH = pl.DeviceIdType.MESH


def _sigmoid(x):
    return 0.5 * jnp.tanh(0.5 * x) + 0.5


def _mx(x):
    return x.astype(_MXU_DTYPE)


def _dot_nt(a, b):
    return lax.dot_general(_mx(a), _mx(b), (((1,), (1,)), ((), ())), preferred_element_type=F32)


def _dot_tn(a, b):
    return lax.dot_general(_mx(a), _mx(b), (((0,), (0,)), ((), ())), preferred_element_type=F32)


def _rows(shape):
    return lax.broadcasted_iota(jnp.int32, shape, 0)


def _cols(shape):
    return lax.broadcasted_iota(jnp.int32, shape, 1)


def _softplus_neg(lam):
    z = -lam
    e = jnp.exp(-jnp.abs(z))
    series = e * (1.0 - e * (0.5 - e * (1.0 / 3.0 - 0.25 * e)))
    return jnp.maximum(z, 0.0) + jnp.where(e < 1e-2, series, jnp.log(1.0 + e))


def _one_minus_sq(a, y):
    series = -y * (1.0 + y * (0.5 + y * (1.0 / 6.0)))
    return jnp.where(y > -0.01, series, 1.0 - a * a)


def _cumsum_rows(x, reverse=False):
    n = x.shape[0] // 8
    row8 = _rows((8, x.shape[1]))
    out = [None] * n
    carry = None
    for g in (reversed(range(n)) if reverse else range(n)):
        blk = x[8 * g:8 * g + 8]
        for s in (1, 2, 4):
            if reverse:
                blk = blk + jnp.where(row8 < 8 - s, pltpu.roll(blk, 8 - s, 0), 0.0)
            else:
                blk = blk + jnp.where(row8 >= s, pltpu.roll(blk, s, 0), 0.0)
        if carry is not None:
            blk = blk + carry
        carry = blk[0:1] if reverse else blk[7:8]
        out[g] = blk
    return jnp.concatenate(out, axis=0)


def _shift_down(x, halo, s):
    if s == 0:
        return x
    r = pltpu.roll(x, s, 0)
    top = jnp.where(_rows(halo.shape) < s, pltpu.roll(halo, s, 0), r[0:8])
    return jnp.concatenate([top, r[8:]], axis=0)


def _shift_up(x, head, s):
    if s == 0:
        return x
    n = x.shape[0]
    r = pltpu.roll(x, n - s, 0)
    bottom = jnp.where(_rows(head.shape) >= 8 - s, pltpu.roll(head, 8 - s, 0), r[n - 8:n])
    return jnp.concatenate([r[:n - 8], bottom], axis=0)


def _lru_gates(lx, halo, cw, cb, wa, wx, ba, bx, sp):
    xs = [_shift_down(lx, halo, 3 - k) for k in range(4)]
    xc = cb + cw[0:1] * xs[0] + cw[1:2] * xs[1] + cw[2:3] * xs[2] + cw[3:4] * xs[3]
    zr, zi = [], []
    for n in range(LRU_BLOCKS):
        xb = _mx(xc[:, n * LRU_BW:(n + 1) * LRU_BW])
        zr.append(jnp.dot(xb, wa[n], preferred_element_type=F32))
        zi.append(jnp.dot(xb, wx[n], preferred_element_type=F32))
    r = _sigmoid(jnp.concatenate(zr, axis=1) + ba)
    ig = _sigmoid(jnp.concatenate(zi, axis=1) + bx)
    la = (-LRU_C * sp) * r
    a = jnp.exp(la)
    om = _one_minus_sq(a, 2.0 * la)
    return xs, xc, r, ig, a, om


def _hgrn_prep(q, fr, lb):
    sig = _sigmoid(fr)
    f = lb + (1.0 - lb) * sig
    sq = _sigmoid(q)
    b = _cumsum_rows(jnp.log(f))
    bm = b[CHUNK // 2 - 1:CHUNK // 2]
    bc = b[CHUNK - 1:CHUNK]
    em1 = jnp.exp(jnp.minimum(b - bm, EXP_CLAMP))
    em2 = jnp.exp(jnp.minimum(bm - b, EXP_CLAMP))
    qm = _mx((q * sq) * em1)
    km = _mx((1.0 - f) * em2)
    return sig, f, sq, em1, em2, qm, km, jnp.exp(bm), jnp.exp(bc - bm), jnp.exp(bc)


def _seg(ref, s, rows=slice(None)):
    return jnp.concatenate([ref[2 * s, rows, :], ref[2 * s + 1, rows, :]], axis=1)


def _pad_lanes(parts, width):
    have = sum(p.shape[1] for p in parts)
    pad = [jnp.zeros((parts[0].shape[0], width - have), parts[0].dtype)] if width > have else []
    return jnp.concatenate(list(parts) + pad, axis=1)


def _pad_rows(parts, height):
    have = sum(p.shape[0] for p in parts)
    pad = [jnp.zeros((height - have, parts[0].shape[1]), parts[0].dtype)] if height > have else []
    return jnp.concatenate(list(parts) + pad, axis=0)


def _adamw(w, g, m, v):
    m = ADAM_B1 * m + (1.0 - ADAM_B1) * g
    v = ADAM_B2 * v + (1.0 - ADAM_B2) * jnp.square(g)
    m_hat = m / (1.0 - ADAM_B1 ** ADAM_STEP)
    v_hat = v / (1.0 - ADAM_B2 ** ADAM_STEP)
    delta = -ADAM_LR * (m_hat / (jnp.sqrt(v_hat) + ADAM_EPS) + ADAM_WD * w)
    return delta, m, v


def _full(shape):
    nd = len(shape)
    return pl.BlockSpec(shape, lambda *_: (0,) * nd)


def _seq_params():
    return pltpu.CompilerParams(dimension_semantics=("arbitrary",))


def _resident(shape):
    nd = len(shape)
    return pl.BlockSpec(shape, lambda *_: (0,) * nd, pipeline_mode=pl.Buffered(1))


def _in_proj_gather(x, w, win_units, halved, small_w, tm):
    t = x.shape[0]
    nt = t // tm
    n_h = len(halved)
    n_split = 3 + n_h
    n_items = n_split + 1
    hu = D_MODEL // 2
    sub = min(256, tm)
    chip = 2 * lax.axis_index("x") + lax.axis_index("y")
    order = jnp.stack([chip] + [jnp.bitwise_xor(chip, flip) for flip in (2, 1, 3)])

    def body(order_ref, x_ref, w_ref, win_ref, *rest):
        del order_ref
        h_ins, small_ref = rest[:n_h], rest[n_h]
        ut_ref, p_ref, wg_ref = rest[n_h + 1:n_h + 4]
        h_outs, sw_ref = rest[n_h + 4:2 * n_h + 4], rest[2 * n_h + 4]
        u_s, wbuf, send, recv, fsend, frecv, osend, orecv, wsem = rest[2 * n_h + 5:]
        g, i = pl.program_id(0), pl.program_id(1)
        mx, my, c = lax.axis_index("x"), lax.axis_index("y"), lax.axis_index("c")
        me = 2 * mx + my
        peers = [(1 - mx, my), (mx, 1 - my), (1 - mx, 1 - my)]
        qs = [2 * px + py for px, py in peers]
        sibling = (mx, my, 1 - c)

        def remote(src, dst, ssem, rsem, dev):
            return pltpu.make_async_remote_copy(src_ref=src, dst_ref=dst, send_sem=ssem, recv_sem=rsem,
                                                device_id=dev, device_id_type=MESH)

        def mine(item):
            if item < 3:
                return win_ref.at[item, pl.ds(c * hu, hu)]
            return h_ins[item - 3].at[c] if item < n_split else small_ref

        def piece(item, q, h):
            if item < 3:
                return wg_ref.at[q, item, pl.ds(h * hu, hu)]
            return h_outs[item - 3].at[q, h] if item < n_split else sw_ref.at[q]

        def fetch(item, jj, q):
            px, py = peers[jj]
            return remote(mine(item), piece(item, q, c), send.at[item * 3 + jj], recv.at[item * 3 + jj], (px, py, c))

        def forward(item, jj, h):
            whole = piece(item, qs[jj], h)
            step = whole.shape[0] // FWD_PIECES
            copies = []
            for k in range(FWD_PIECES):
                part = whole.at[pl.ds(k * step, step)]
                sem = (item * 3 + jj) * FWD_PIECES + k
                copies.append(remote(part, part, fsend.at[sem], frecv.at[sem], sibling))
            return copies

        def own(item):
            if item < 3:
                src, dst = win_ref.at[item], wg_ref.at[me, item]
            elif item < n_split:
                src, dst = h_ins[item - 3], h_outs[item - 3].at[me]
            else:
                src, dst = small_ref, sw_ref.at[me]
            return remote(src, dst, osend.at[item], orecv.at[item], sibling)

        def load_units(gg):
            src = lambda k: win_ref.at[k] if gg == 0 else wg_ref.at[qs[gg - 1], k]
            return [pltpu.make_async_copy(src(k), wbuf.at[gg % 2, k], wsem.at[(gg % 2) * 3 + k]) for k in range(3)]

        def in_pass(gg):
            return pl.when(g == gg)

        def pass_on(gg):
            for item in range(3):
                fetch(item, gg, qs[gg]).wait_recv()
                for cp in forward(item, gg, c):
                    cp.start()

        @pl.when(i == 0)
        def _():
            @in_pass(0)
            def _():
                for item in range(3):
                    for jj in range(2):
                        fetch(item, jj, me).start()
                for item in range(n_items):
                    own(item).start()
                for cp in load_units(0):
                    cp.start()

            @in_pass(1)
            def _():
                for item in range(3):
                    fetch(item, 2, me).start()

            @in_pass(2)
            def _():
                for item in range(3, n_items):
                    for jj in range(3):
                        fetch(item, jj, me).start()

            for gg in range(N_CHIPS):
                @in_pass(gg)
                def _(gg=gg):
                    for cp in load_units(gg):
                        cp.wait()

        @pl.when(g == 0)
        def _():
            xv = x_ref[...]
            rstd = lax.rsqrt(jnp.mean(xv * xv, axis=-1, keepdims=True) + EPS)
            u = xv * rstd * w_ref[...]
            ut_ref[...] = u.T.astype(_MXU_DTYPE)
            u_s[pl.ds(pl.multiple_of(i * tm, tm), tm), :] = _mx(u)

        slot = g % 2
        for r in range(tm // sub):
            ur = u_s[pl.ds(pl.multiple_of(i * tm + r * sub, sub), sub), :]
            for k in range(3):
                p_ref[k, r * sub:(r + 1) * sub, :] = jnp.dot(ur, wbuf[slot, k], preferred_element_type=F32)

        @pl.when(i == nt // 2)
        def _():
            for gg in (1, 2):
                in_pass(gg)(functools.partial(pass_on, gg))

            @in_pass(N_CHIPS - 1)
            def _():
                for item in range(3, n_items):
                    for jj in range(3):
                        fetch(item, jj, qs[jj]).wait_recv()
                        if item < n_split:
                            for cp in forward(item, jj, c):
                                cp.start()

        @pl.when(i == nt - 1)
        def _():
            in_pass(0)(functools.partial(pass_on, 0))
            for gg in range(3):
                @in_pass(gg)
                def _(gg=gg):
                    for item in range(3):
                        for cp in forward(item, gg, 1 - c):
                            cp.wait_recv()
                    for cp in load_units(gg + 1):
                        cp.start()

            @in_pass(N_CHIPS - 1)
            def _():
                for item in range(3, n_split):
                    for jj in range(3):
                        for cp in forward(item, jj, 1 - c):
                            cp.wait_recv()
                for item in range(n_items):
                    own(item).wait_recv()
                for item in range(n_items):
                    for jj in range(3):
                        fetch(item, jj, me).wait_send()
                        if item < n_split:
                            for cp in forward(item, jj, c):
                                cp.wait_send()
                    own(item).wait_send()

    any_spec = pl.BlockSpec(memory_space=pl.ANY)
    first_pass = lambda g, i: jnp.where(g == 0, i, nt - 1)
    res = pl.pallas_call(
        body, name="in_proj",
        grid_spec=pltpu.PrefetchScalarGridSpec(
            num_scalar_prefetch=1, grid=(N_CHIPS, nt),
            in_specs=[pl.BlockSpec((tm, D_MODEL), lambda g, i, o: (first_pass(g, i), 0)),
                      pl.BlockSpec((1, D_MODEL), lambda g, i, o: (0, 0))] + [any_spec] * (n_h + 2),
            out_specs=[pl.BlockSpec((D_MODEL, tm), lambda g, i, o: (0, first_pass(g, i))),
                       pl.BlockSpec((3, tm, UNIT), lambda g, i, o: (o[g], i, 0))]
            + [any_spec] * (n_h + 2),
            scratch_shapes=[pltpu.VMEM((t, D_MODEL), _MXU_DTYPE), pltpu.VMEM((2, 3, D_MODEL, UNIT), _MXU_DTYPE),
                            pltpu.SemaphoreType.DMA((3 * n_items,)), pltpu.SemaphoreType.DMA((3 * n_items,)),
                            pltpu.SemaphoreType.DMA((3 * n_split * FWD_PIECES,)),
                            pltpu.SemaphoreType.DMA((3 * n_split * FWD_PIECES,)),
                            pltpu.SemaphoreType.DMA((n_items,)), pltpu.SemaphoreType.DMA((n_items,)),
                            pltpu.SemaphoreType.DMA((6,))]),
        out_shape=[jax.ShapeDtypeStruct((D_MODEL, t), _MXU_DTYPE), jax.ShapeDtypeStruct((N_UNIT, t, UNIT), F32),
                   jax.ShapeDtypeStruct((N_CHIPS,) + win_units.shape, win_units.dtype)]
        + [jax.ShapeDtypeStruct((N_CHIPS,) + a.shape, a.dtype) for a in halved]
        + [jax.ShapeDtypeStruct((N_CHIPS,) + small_w.shape, small_w.dtype)],
        compiler_params=pltpu.CompilerParams(dimension_semantics=("arbitrary", "arbitrary")),
    )(order, x, w, win_units, *halved, small_w)
    return res[0], res[1], res[2], res[3:3 + n_h], res[3 + n_h]


def _lru_fwd(p, wa, wx, ba, bx, lam, cw, cb, tm):
    t = p.shape[1]
    ng = tm // 8

    def body(lx_ref, halo_ref, lg_ref, wa_ref, wx_ref, ba_ref, bx_ref, lam_ref, cw_ref, cb_ref,
             h_ref, y_ref, a_s, u_s, hc_s):
        i = pl.program_id(0)

        @pl.when(i == 0)
        def _():
            hc_s[...] = jnp.zeros_like(hc_s)

        sp = _softplus_neg(lam_ref[...])
        halo = jnp.where(i == 0, 0.0, _seg(halo_ref, 0))
        _, xc, _, ig, a, om = _lru_gates(_seg(lx_ref, 0), halo, cw_ref[...], cb_ref[...], wa_ref[...],
                                         wx_ref[...], ba_ref[...], bx_ref[...], sp)
        a_s[...] = a
        u_s[...] = jnp.sqrt(om) * (ig * xc)
        row8 = _rows((8, D_MODEL))

        def group(g, hc):
            rows = pl.ds(pl.multiple_of(g * 8, 8), 8)
            av = a_s[rows, :]
            uv = u_s[rows, :]
            for s in (1, 2, 4):
                a_sh = jnp.where(row8 >= s, pltpu.roll(av, s, 0), 1.0)
                u_sh = jnp.where(row8 >= s, pltpu.roll(uv, s, 0), 0.0)
                uv = av * u_sh + uv
                av = av * a_sh
            hh = av * hc + uv
            h_ref[rows, :] = hh
            return jnp.broadcast_to(hh[7:8, :], (8, D_MODEL))

        hc_s[...] = lax.fori_loop(0, ng, group, hc_s[...])
        lg = _seg(lg_ref, 0)
        y_ref[...] = (h_ref[...] * (lg * _sigmoid(lg))).astype(_MXU_DTYPE)

    seg = lambda s: pl.BlockSpec((2, tm, UNIT), lambda i: (s, i, 0))
    return pl.pallas_call(
        body, name="lru_fwd", grid=(t // tm,),
        in_specs=[seg(0),
                  pl.BlockSpec((2, 8, UNIT), lambda i: (0, jnp.maximum(i * (tm // 8) - 1, 0), 0)),
                  seg(1),
                  _full((LRU_BLOCKS, LRU_BW, LRU_BW)), _full((LRU_BLOCKS, LRU_BW, LRU_BW)),
                  _full((1, D_MODEL)), _full((1, D_MODEL)), _full((1, D_MODEL)),
                  _full((4, D_MODEL)), _full((1, D_MODEL))],
        out_specs=[pl.BlockSpec((tm, D_MODEL), lambda i: (i, 0)),
                   pl.BlockSpec((None, tm, D_MODEL), lambda i: (0, i, 0))],
        out_shape=[jax.ShapeDtypeStruct((t, D_MODEL), F32),
                   jax.ShapeDtypeStruct((2, t, D_MODEL), _MXU_DTYPE)],
        scratch_shapes=[pltpu.VMEM((tm, D_MODEL), F32), pltpu.VMEM((tm, D_MODEL), F32),
                        pltpu.VMEM((8, D_MODEL), F32)],
        compiler_params=_seq_params(),
    )(p, p, p, wa, wx, ba, bx, lam, cw, cb)


def _hgrn_fwd(p, logits, gw, y2, tm):
    t = p.shape[1]
    nc = tm // CHUNK

    def body(qf_ref, vh_ref, lg_ref, gw_ref, y_in, y_ref, o_ref, st_ref, state):
        del y_in
        i = pl.program_id(0)

        @pl.when(i == 0)
        def _():
            state[...] = jnp.zeros_like(state)

        l0 = lg_ref[0:1, :]
        l1 = lg_ref[1:2, :]
        lb = 1.0 / (1.0 + jnp.exp(l1 - l0))
        gwv = gw_ref[...]
        causal = _rows((CHUNK, CHUNK)) >= _cols((CHUNK, CHUNK))

        for c in range(nc):
            rows = pl.ds(c * CHUNK, CHUNK)
            v = _seg(vh_ref, 0, rows)
            hg = _seg(vh_ref, 1, rows)
            _, _, _, _, _, qm, km, ebm, ecm, ebc = _hgrn_prep(_seg(qf_ref, 0, rows), _seg(qf_ref, 1, rows), lb)
            ke = _mx(km.astype(F32) * ecm)
            gate = gwv * (hg * _sigmoid(hg))
            o_parts, y_parts = [], []
            for h in range(HEADS):
                ln = slice(h * HEAD_D, (h + 1) * HEAD_D)
                s0 = state[h]
                st_ref[c, h] = s0
                vt = _mx(v[:, ln].T)
                att = _mx(jnp.where(causal, _dot_nt(qm[:, ln], km[:, ln]), 0.0))
                o = _dot_nt(_pad_lanes([qm[:, ln], att], 2 * HEAD_D),
                            _pad_lanes([_mx(s0 * ebm[:, ln]), vt], 2 * HEAD_D))
                state[h] = s0 * ebc[:, ln] + jnp.dot(vt, ke[:, ln], preferred_element_type=F32)
                rstd = lax.rsqrt(jnp.mean(o * o, axis=-1, keepdims=True) + EPS)
                o_parts.append(o)
                y_parts.append((o * rstd * gate[:, ln]).astype(_MXU_DTYPE))
            o_ref[rows, :] = jnp.concatenate(o_parts, axis=1)
            y_ref[rows, :] = jnp.concatenate(y_parts, axis=1)

    pair = lambda s: pl.BlockSpec((4, tm, UNIT), lambda i: (s, i, 0))
    return pl.pallas_call(
        body, name="hgrn_fwd", grid=(t // tm,),
        in_specs=[pair(1), pair(2), _full((2, D_MODEL)), _full((1, D_MODEL)),
                  pl.BlockSpec(memory_space=pl.ANY)],
        out_specs=[pl.BlockSpec((None, tm, D_MODEL), lambda i: (1, i, 0)),
                   pl.BlockSpec((tm, D_MODEL), lambda i: (i, 0)),
                   pl.BlockSpec((nc, HEADS, HEAD_D, HEAD_D), lambda i: (i, 0, 0, 0))],
        out_shape=[jax.ShapeDtypeStruct((2, t, D_MODEL), _MXU_DTYPE),
                   jax.ShapeDtypeStruct((t, D_MODEL), F32),
                   jax.ShapeDtypeStruct((t // CHUNK, HEADS, HEAD_D, HEAD_D), F32)],
        scratch_shapes=[pltpu.VMEM((HEADS, HEAD_D, HEAD_D), F32)],
        input_output_aliases={4: 0},
        compiler_params=_seq_params(),
    )(p, p, logits, gw, y2)


def _out_fused(y2, wo, x, tgt, wpost, tm):
    t = x.shape[0]
    inv_d = 1.0 / D_MODEL

    def body(y_ref, wo_ref, x_ref, t_ref, w_ref, dout_ref, dy_ref, sq_ref, dw_ref):
        i = pl.program_id(0)

        @pl.when(i == 0)
        def _():
            sq_ref[...] = jnp.zeros_like(sq_ref)
            dw_ref[...] = jnp.zeros_like(dw_ref)

        y = (jnp.dot(y_ref[0], wo_ref[0:D_MODEL, :], preferred_element_type=F32)
             + jnp.dot(y_ref[1], wo_ref[D_MODEL:2 * D_MODEL, :], preferred_element_type=F32))
        w = w_ref[...]
        rstd = lax.rsqrt(jnp.mean(y * y, axis=-1, keepdims=True) + EPS)
        n = y * rstd
        err = (x_ref[...] + n * w) - t_ref[...]
        sq_ref[...] += jnp.sum(err * err, axis=0, keepdims=True)
        dout = err * inv_d
        dout_ref[...] = dout
        dw_ref[...] += jnp.sum(dout * n, axis=0, keepdims=True)
        dn = dout * w
        dy_ref[...] = (rstd * (dn - n * jnp.mean(dn * n, axis=-1, keepdims=True))).astype(_MXU_DTYPE)

    row = pl.BlockSpec((tm, D_MODEL), lambda i: (i, 0))
    return pl.pallas_call(
        body, name="out_fused", grid=(t // tm,),
        in_specs=[pl.BlockSpec((2, tm, D_MODEL), lambda i: (0, i, 0)), _full((2 * D_MODEL, D_MODEL)),
                  row, row, _full((1, D_MODEL))],
        out_specs=[row, row, _full((1, D_MODEL)), _full((1, D_MODEL))],
        out_shape=[jax.ShapeDtypeStruct((t, D_MODEL), F32), jax.ShapeDtypeStruct((t, D_MODEL), _MXU_DTYPE),
                   jax.ShapeDtypeStruct((1, D_MODEL), F32), jax.ShapeDtypeStruct((1, D_MODEL), F32)],
        compiler_params=_seq_params(),
    )(y2, wo, x, tgt, wpost)


def _dw_out(y2, dy, tt):
    t = dy.shape[0]
    hr = UNIT // 2

    def body(y_ref, dy_ref, g_ref):
        @pl.when(pl.program_id(1) == 0)
        def _():
            g_ref[...] = jnp.zeros_like(g_ref)

        r = _dot_tn(y_ref[...], dy_ref[...])
        g_ref[0] += r[0:hr]
        g_ref[1] += r[hr:UNIT]

    return pl.pallas_call(
        body, name="dw_out", grid=(N_CHIPS, t // tt),
        in_specs=[pl.BlockSpec((None, tt, UNIT), lambda c, k: (c // 2, k, c % 2)),
                  pl.BlockSpec((tt, D_MODEL), lambda c, k: (k, 0))],
        out_specs=pl.BlockSpec((2, None, hr, D_MODEL), lambda c, k: (0, c, 0, 0)),
        out_shape=jax.ShapeDtypeStruct((2, N_CHIPS, hr, D_MODEL), F32),
        compiler_params=pltpu.CompilerParams(dimension_semantics=("arbitrary", "arbitrary")),
    )(y2, dy)


def _lru_bwd(dy, wo, p, h, wa, wx, ba, bx, lam, cw, cb, tm):
    t = dy.shape[0]
    nt = t // tm
    ng = tm // 8

    def body(dy_ref, wo_ref, lx_ref, halo_ref, lg_ref, h_ref, hhalo_ref, wa_ref, wx_ref, ba_ref, bx_ref,
             lam_ref, cw_ref, cb_ref,
             dp_ref, dwa_ref, dwx_ref, dba_ref, dbx_ref, dsp_ref, dcw_ref, dcb_ref,
             c_s, in_s, dh_s, dhc_s, afirst_s, dxc_s):
        i = pl.program_id(0)
        first = i == nt - 1

        @pl.when(i == 0)
        def _():
            for ref in (dwa_ref, dwx_ref, dba_ref, dbx_ref, dsp_ref, dcw_ref, dcb_ref, dhc_s, afirst_s, dxc_s):
                ref[...] = jnp.zeros_like(ref)

        sp = _softplus_neg(lam_ref[...])
        cwv = cw_ref[...]
        wav = wa_ref[...]
        wxv = wx_ref[...]
        halo = jnp.where(first, 0.0, _seg(halo_ref, 0))
        xs, xc, r, ig, a, om = _lru_gates(_seg(lx_ref, 0), halo, cwv, cb_ref[...], wav, wxv,
                                          ba_ref[...], bx_ref[...], sp)
        inv_mult = lax.rsqrt(om)
        mult = om * inv_mult
        d_y = _dot_nt(dy_ref[...], wo_ref[...])
        lg = _seg(lg_ref, 0)
        sl = _sigmoid(lg)
        hv = h_ref[...]
        d_lg = d_y * hv * (sl * (1.0 + lg * (1.0 - sl)))

        c_s[...] = _shift_up(a, afirst_s[...], 1)
        in_s[...] = d_y * (lg * sl)
        row8 = _rows((8, D_MODEL))

        def group(gg, carry):
            rows = pl.ds(pl.multiple_of((ng - 1 - gg) * 8, 8), 8)
            cv = c_s[rows, :]
            uv = in_s[rows, :]
            for s in (1, 2, 4):
                c_sh = jnp.where(row8 < 8 - s, pltpu.roll(cv, 8 - s, 0), 1.0)
                u_sh = jnp.where(row8 < 8 - s, pltpu.roll(uv, 8 - s, 0), 0.0)
                uv = cv * u_sh + uv
                cv = cv * c_sh
            hh = cv * carry + uv
            dh_s[rows, :] = hh
            return jnp.broadcast_to(hh[0:1, :], (8, D_MODEL))

        dhc_s[...] = lax.fori_loop(0, ng, group, dhc_s[...])
        afirst_s[...] = a[0:8]
        dh = dh_s[...]

        h_prev = _shift_down(hv, jnp.where(first, 0.0, hhalo_ref[...]), 1)
        gx = ig * xc
        d_mult = dh * gx
        d_i = dh * (mult * xc)
        d_xc = dh * (mult * ig)
        d_la = (dh * h_prev) * a - d_mult * ((a * a) * inv_mult)
        dsp_ref[...] += jnp.sum(d_la * r, axis=0, keepdims=True) * (-LRU_C)
        d_zr = (d_la * (-LRU_C * sp)) * (r * (1.0 - r))
        d_zi = d_i * (ig * (1.0 - ig))
        dba_ref[...] += jnp.sum(d_zr, axis=0, keepdims=True)
        dbx_ref[...] += jnp.sum(d_zi, axis=0, keepdims=True)
        back = []
        for n in range(LRU_BLOCKS):
            ln = slice(n * LRU_BW, (n + 1) * LRU_BW)
            xb = _mx(xc[:, ln])
            zr_n = _mx(d_zr[:, ln])
            zi_n = _mx(d_zi[:, ln])
            dwa_ref[n] += _dot_tn(xb, zr_n)
            dwx_ref[n] += _dot_tn(xb, zi_n)
            back.append(_dot_nt(zr_n, wav[n]) + _dot_nt(zi_n, wxv[n]))
        d_xc = d_xc + jnp.concatenate(back, axis=1)
        dcb_ref[...] += jnp.sum(d_xc, axis=0, keepdims=True)
        for k in range(4):
            dcw_ref[k:k + 1, :] += jnp.sum(d_xc * xs[k], axis=0, keepdims=True)
        head = dxc_s[...]
        d_lx = cwv[3:4] * d_xc
        for k in range(3):
            d_lx = d_lx + cwv[k:k + 1] * _shift_up(d_xc, head, 3 - k)
        dxc_s[...] = d_xc[0:8]
        dp_ref[0] = d_lx.astype(_MXU_DTYPE)
        dp_ref[1] = d_lg.astype(_MXU_DTYPE)

    rev = lambda i: nt - 1 - i
    seg = lambda s: pl.BlockSpec((2, tm, UNIT), lambda i: (s, rev(i), 0))
    halo_row = lambda i: jnp.maximum(rev(i) * (tm // 8) - 1, 0)
    row = pl.BlockSpec((tm, D_MODEL), lambda i: (rev(i), 0))
    vec = _full((1, D_MODEL))
    wblk = _full((LRU_BLOCKS, LRU_BW, LRU_BW))
    return pl.pallas_call(
        body, name="lru_bwd", grid=(nt,),
        in_specs=[row, pl.BlockSpec((D_MODEL, D_MODEL), lambda i: (0, 0)),
                  seg(0), pl.BlockSpec((2, 8, UNIT), lambda i: (0, halo_row(i), 0)), seg(1),
                  row, pl.BlockSpec((8, D_MODEL), lambda i: (halo_row(i), 0)),
                  wblk, wblk, vec, vec, vec, _full((4, D_MODEL)), vec],
        out_specs=[pl.BlockSpec((2, tm, D_MODEL), lambda i: (2, rev(i), 0)),
                   wblk, wblk, vec, vec, vec, _full((4, D_MODEL)), vec],
        out_shape=[jax.ShapeDtypeStruct((N_SEG, t, D_MODEL), _MXU_DTYPE),
                   jax.ShapeDtypeStruct((LRU_BLOCKS, LRU_BW, LRU_BW), F32),
                   jax.ShapeDtypeStruct((LRU_BLOCKS, LRU_BW, LRU_BW), F32),
                   jax.ShapeDtypeStruct((1, D_MODEL), F32), jax.ShapeDtypeStruct((1, D_MODEL), F32),
                   jax.ShapeDtypeStruct((1, D_MODEL), F32), jax.ShapeDtypeStruct((4, D_MODEL), F32),
                   jax.ShapeDtypeStruct((1, D_MODEL), F32)],
        scratch_shapes=[pltpu.VMEM((tm, D_MODEL), F32), pltpu.VMEM((tm, D_MODEL), F32),
                        pltpu.VMEM((tm, D_MODEL), F32), pltpu.VMEM((8, D_MODEL), F32),
                        pltpu.VMEM((8, D_MODEL), F32), pltpu.VMEM((8, D_MODEL), F32)],
        compiler_params=_seq_params(),
    )(dy, wo, p, p, p, h, h, wa, wx, ba, bx, lam, cw, cb)


def _hgrn_bwd(dy, wo, p, o, states, logits, gw, dp, tm):
    t = dy.shape[0]
    nt = t // tm
    nc = tm // CHUNK

    def body(dy_ref, wo_ref, qf_ref, vh_ref, o_ref, st_ref, lg_ref, gw_ref, dp_in,
             dp_ref, dgw_ref, dlb_ref, dyh_s, dstate, snext):
        del dp_in
        i = pl.program_id(0)

        @pl.when(i == 0)
        def _():
            for ref in (dgw_ref, dlb_ref, dstate, snext):
                ref[...] = jnp.zeros_like(ref)

        dyh_s[...] = _dot_nt(dy_ref[...], wo_ref[...])
        l0 = lg_ref[0:1, :]
        l1 = lg_ref[1:2, :]
        lb = 1.0 / (1.0 + jnp.exp(l1 - l0))
        gwv = gw_ref[...]
        causal = _rows((CHUNK, CHUNK)) >= _cols((CHUNK, CHUNK))
        wide = 2 * HEAD_D
        zeros_h = jnp.zeros((CHUNK, HEAD_D), _MXU_DTYPE)

        for c in reversed(range(nc)):
            rows = pl.ds(c * CHUNK, CHUNK)
            q = _seg(qf_ref, 0, rows)
            hg = _seg(vh_ref, 1, rows)
            ov = o_ref[rows, :]
            dyh = dyh_s[rows, :]
            sig, f, sq, em1, em2, qm, km, ebm, ecm, ebc = _hgrn_prep(q, _seg(qf_ref, 1, rows), lb)
            ke = _mx(km.astype(F32) * ecm)
            vb = _mx(_seg(vh_ref, 0, rows))
            sg = _sigmoid(hg)
            d_on = dyh * (hg * sg)
            x_p, y_p, dv_p, n_p, ex_p = [], [], [], [], []
            for h in range(HEADS):
                ln = slice(h * HEAD_D, (h + 1) * HEAD_D)
                o_h = ov[:, ln]
                rstd = lax.rsqrt(jnp.mean(o_h * o_h, axis=-1, keepdims=True) + EPS)
                n = o_h * rstd
                dn = d_on[:, ln] * gwv[:, ln]
                d_o = rstd * (dn - n * jnp.mean(dn * n, axis=-1, keepdims=True))
                d_ob = _mx(d_o)
                d_ot = _mx(d_o.T)
                s0 = st_ref[c, h]
                ds = dstate[h]
                both = _dot_nt(jnp.concatenate([_pad_lanes([qm[:, ln]], wide), _pad_lanes([zeros_h, d_ob], wide)], axis=0),
                               jnp.concatenate([km[:, ln], vb[:, ln]], axis=1))
                att = jnp.where(causal, both[0:CHUNK], 0.0)
                d_att = jnp.where(causal, both[CHUNK:2 * CHUNK], 0.0)
                x_p.append(jnp.dot(_pad_lanes([d_ob, _mx(d_att)], wide),
                                   _pad_rows([_mx(s0 * ebm[:, ln]), km[:, ln]], wide), preferred_element_type=F32))
                ds_c = ds * ecm[:, ln]
                y_p.append(jnp.dot(_pad_lanes([vb[:, ln], _mx(d_att.T)], wide),
                                   _pad_rows([_mx(ds_c), qm[:, ln]], wide), preferred_element_type=F32))
                dv_p.append(_dot_nt(_pad_lanes([ke[:, ln], _mx(att.T)], wide), _pad_lanes([_mx(ds), d_ot], wide)))
                ex_p.append(jnp.sum(ds * snext[h], axis=0, keepdims=True))
                dstate[h] = (ds_c + jnp.dot(d_ot, qm[:, ln], preferred_element_type=F32)) * ebm[:, ln]
                snext[h] = s0
                n_p.append(n)
            cat = lambda parts: jnp.concatenate(parts, axis=1)
            x_all, y_all, n_all = cat(x_p), cat(y_p), cat(n_p)
            dg = _cumsum_rows(qm.astype(F32) * x_all - km.astype(F32) * y_all, reverse=True) + cat(ex_p)
            df = dg / f - em2 * y_all
            dgw_ref[...] += jnp.sum(d_on * n_all, axis=0, keepdims=True)
            dlb_ref[...] += jnp.sum(df * (1.0 - sig), axis=0, keepdims=True)
            dp_ref[0, rows, :] = ((em1 * x_all) * (sq * (1.0 + q * (1.0 - sq)))).astype(_MXU_DTYPE)
            dp_ref[1, rows, :] = (df * ((1.0 - lb) * sig * (1.0 - sig))).astype(_MXU_DTYPE)
            dp_ref[2, rows, :] = cat(dv_p).astype(_MXU_DTYPE)
            dp_ref[3, rows, :] = (dyh * (n_all * gwv) * (sg * (1.0 + hg * (1.0 - sg)))).astype(_MXU_DTYPE)

    rev = lambda i: nt - 1 - i
    pair = lambda s: pl.BlockSpec((4, tm, UNIT), lambda i: (s, rev(i), 0))
    row = pl.BlockSpec((tm, D_MODEL), lambda i: (rev(i), 0))
    vec = _full((1, D_MODEL))
    return pl.pallas_call(
        body, name="hgrn_bwd", grid=(nt,),
        in_specs=[row, pl.BlockSpec((D_MODEL, D_MODEL), lambda i: (1, 0)), pair(1), pair(2), row,
                  pl.BlockSpec((nc, HEADS, HEAD_D, HEAD_D), lambda i: (rev(i), 0, 0, 0)),
                  _full((2, D_MODEL)), vec, pl.BlockSpec(memory_space=pl.ANY)],
        out_specs=[pl.BlockSpec((4, tm, D_MODEL), lambda i: (0, rev(i), 0)), vec, vec],
        out_shape=[jax.ShapeDtypeStruct((N_SEG, t, D_MODEL), _MXU_DTYPE),
                   jax.ShapeDtypeStruct((1, D_MODEL), F32), jax.ShapeDtypeStruct((1, D_MODEL), F32)],
        scratch_shapes=[pltpu.VMEM((tm, D_MODEL), F32), pltpu.VMEM((HEADS, HEAD_D, HEAD_D), F32),
                        pltpu.VMEM((HEADS, HEAD_D, HEAD_D), F32)],
        input_output_aliases={8: 0},
        compiler_params=_seq_params(),
    )(dy, wo, p, p, o, states, logits, gw, dp)


def _dp_unit(m):
    return ((m // 2 + 2) % N_SEG) * 2 + m % 2


def _dw_in(ut, dp, tt):
    t = ut.shape[1]
    hr = D_MODEL // 2

    def body(ut_ref, dp_ref, g_ref):
        @pl.when(pl.program_id(1) == 0)
        def _():
            g_ref[...] = jnp.zeros_like(g_ref)

        r = jnp.dot(ut_ref[...], dp_ref[...], preferred_element_type=F32)
        for h in range(2):
            for half in range(2):
                g_ref[h, half] += r[h * hr:(h + 1) * hr, half * UNIT:(half + 1) * UNIT]

    return pl.pallas_call(
        body, name="dw_in", grid=(N_SEG, t // tt),
        in_specs=[pl.BlockSpec((D_MODEL, tt), lambda s, k: (0, k)),
                  pl.BlockSpec((None, tt, D_MODEL), lambda s, k: (s, k, 0))],
        out_specs=pl.BlockSpec((2, 2, hr, UNIT), lambda s, k: (0, (s + 2) % N_SEG, 0, 0)),
        out_shape=jax.ShapeDtypeStruct((2, N_UNIT, hr, UNIT), F32),
        compiler_params=pltpu.CompilerParams(dimension_semantics=("arbitrary", "arbitrary")),
    )(ut, dp)


def _dx(dp, wg, x, dout, wpre, tm, parts, scatter):
    t = x.shape[0]
    n = len(parts)
    nt = t // tm

    def body(*refs):
        dp_ref, w_ref, x_ref, dout_ref, wpre_ref = refs[:5]
        ins = refs[5:5 + n]
        gx_ref, dw_ref = refs[5 + n:7 + n]
        outs = refs[7 + n:7 + 2 * n]
        send_sems, recv_sems = refs[7 + 2 * n:]
        i = pl.program_id(0)

        @pl.when(i == 0)
        def _():
            dw_ref[...] = jnp.zeros_like(dw_ref)
            for cp in _chip_copies(ins, outs, send_sems, recv_sems, scatter):
                cp.start()

        du = None
        for m in range(N_UNIT):
            part = _dot_nt(dp_ref[m // 2, :, (m % 2) * UNIT:(m % 2 + 1) * UNIT], w_ref[_dp_unit(m)])
            du = part if du is None else du + part
        xv = x_ref[...]
        rstd = lax.rsqrt(jnp.mean(xv * xv, axis=-1, keepdims=True) + EPS)
        xn = xv * rstd
        dw_ref[...] += jnp.sum(du * xn, axis=0, keepdims=True)
        dn = du * wpre_ref[...]
        gx_ref[...] = dout_ref[...] + rstd * (dn - xn * jnp.mean(dn * xn, axis=-1, keepdims=True))

        @pl.when(i == nt - 1)
        def _():
            copies = _chip_copies(ins, outs, send_sems, recv_sems, scatter)
            for cp in copies:
                cp.wait_recv()
            for cp in copies:
                cp.wait_send()

    row = pl.BlockSpec((tm, D_MODEL), lambda i: (i, 0))
    any_spec = pl.BlockSpec(memory_space=pl.ANY)
    res = pl.pallas_call(
        body, name="dx", grid=(nt,),
        in_specs=[pl.BlockSpec((N_SEG, tm, D_MODEL), lambda i: (0, i, 0)), _resident((N_UNIT, D_MODEL, UNIT)),
                  row, row, _full((1, D_MODEL))] + [any_spec] * n,
        out_specs=[row, _full((1, D_MODEL))] + [any_spec] * n,
        out_shape=[jax.ShapeDtypeStruct((t, D_MODEL), F32), jax.ShapeDtypeStruct((1, D_MODEL), F32)]
        + _chip_exchange_shapes(parts, scatter),
        scratch_shapes=[pltpu.SemaphoreType.DMA((max(3 * n, 1),)), pltpu.SemaphoreType.DMA((max(3 * n, 1),))],
        compiler_params=_seq_params(),
    )(dp, wg, x, dout, wpre, *parts)
    return res[0], res[1], res[2:]


CHUNK_BYTES = 256 * 1024
MAX_CHUNKS = 16


def _n_chunks(shape, dtype, rows):
    nbytes = jnp.dtype(dtype).itemsize
    for d in shape:
        nbytes *= d
    k = max(1, min(MAX_CHUNKS, rows, nbytes // CHUNK_BYTES))
    while rows % k:
        k -= 1
    return k


def _chip_exchange_shapes(arrays, scatter):
    return [jax.ShapeDtypeStruct((3,) + tuple(a.shape[1:] if sc else a.shape), a.dtype)
            for a, sc in zip(arrays, scatter)]


def _chip_copies(ins, outs, send_sems, recv_sems, scatter):
    if not ins:
        return []
    x, y, c = lax.axis_index("x"), lax.axis_index("y"), lax.axis_index("c")
    peers = [(1 - x, y), (x, 1 - y), (1 - x, 1 - y)]
    copies = []
    for a in range(len(ins)):
        for j, (px, py) in enumerate(peers):
            copies.append(pltpu.make_async_remote_copy(
                src_ref=ins[a].at[2 * px + py] if scatter[a] else ins[a], dst_ref=outs[a].at[j],
                send_sem=send_sems.at[a * 3 + j], recv_sem=recv_sems.at[a * 3 + j],
                device_id=(px, py, c), device_id_type=MESH))
    return copies


def _core_swap(arrays, name, halved):
    n = len(arrays)
    shapes, n_sems = _core_swap_plan(arrays, halved)

    def body(*refs):
        copies = _core_swap_copies(arrays, halved, refs[:n], refs[n:2 * n], refs[2 * n], refs[2 * n + 1])
        for cp in copies:
            cp.start()
        for cp in copies:
            cp.wait_recv()
        for cp in copies:
            cp.wait_send()

    any_spec = pl.BlockSpec(memory_space=pl.ANY)
    return pl.pallas_call(
        body, name=name,
        in_specs=[any_spec] * n, out_specs=[any_spec] * n,
        out_shape=shapes,
        scratch_shapes=[pltpu.SemaphoreType.DMA((n_sems,)), pltpu.SemaphoreType.DMA((n_sems,))],
    )(*arrays)


def _core_swap_plan(arrays, halved):
    shapes = [a.shape[1:] if halved else a.shape for a in arrays]
    n_sems = sum(_n_chunks(s, a.dtype, s[0]) for s, a in zip(shapes, arrays))
    return [jax.ShapeDtypeStruct(s, a.dtype) for s, a in zip(shapes, arrays)], max(n_sems, 1)


def _core_swap_copies(arrays, halved, ins, outs, send_sems, recv_sems):
    if not arrays:
        return []
    x, y, c = lax.axis_index("x"), lax.axis_index("y"), lax.axis_index("c")
    copies, sem = [], 0
    for a, arr in enumerate(arrays):
        shape = arr.shape[1:] if halved else arr.shape
        k = _n_chunks(shape, arr.dtype, shape[0])
        step = shape[0] // k
        for j in range(k):
            rows = pl.ds(j * step, step)
            copies.append(pltpu.make_async_remote_copy(
                src_ref=ins[a].at[1 - c, rows] if halved else ins[a].at[rows], dst_ref=outs[a].at[rows],
                send_sem=send_sems.at[sem], recv_sem=recv_sems.at[sem],
                device_id=(x, y, 1 - c), device_id_type=MESH))
            sem += 1
    return copies


def _finish_exchange(arrays, small):
    n = len(arrays)
    ks = [_n_chunks(a.shape, a.dtype, a.shape[0]) for a in arrays]
    offs = [sum(ks[:a]) for a in range(n)]
    total = sum(ks)

    def body(*refs):
        ins, small_ref = refs[:n], refs[n]
        outs, gathered = refs[n + 1:2 * n + 1], refs[2 * n + 1]
        send_sems, recv_sems = refs[2 * n + 2:]
        x, y, c = lax.axis_index("x"), lax.axis_index("y"), lax.axis_index("c")
        copies = []
        for a in range(n):
            step = arrays[a].shape[0] // ks[a]
            for j in range(ks[a]):
                rows = pl.ds(j * step, step)
                copies.append(pltpu.make_async_remote_copy(
                    src_ref=ins[a].at[rows], dst_ref=outs[a].at[rows],
                    send_sem=send_sems.at[offs[a] + j], recv_sem=recv_sems.at[offs[a] + j],
                    device_id=(x, y, 1 - c), device_id_type=MESH))
        for slot in range(7):
            dx, dy, dc = (slot + 1) % 2, ((slot + 1) // 2) % 2, (slot + 1) // 4
            copies.append(pltpu.make_async_remote_copy(
                src_ref=small_ref, dst_ref=gathered.at[slot],
                send_sem=send_sems.at[total + slot], recv_sem=recv_sems.at[total + slot],
                device_id=(x + dx - 2 * x * dx, y + dy - 2 * y * dy, c + dc - 2 * c * dc), device_id_type=MESH))
        for cp in copies:
            cp.start()
        for cp in copies:
            cp.wait_recv()
        for cp in copies:
            cp.wait_send()

    any_spec = pl.BlockSpec(memory_space=pl.ANY)
    res = pl.pallas_call(
        body, name="reduce_cores",
        in_specs=[any_spec] * (n + 1), out_specs=[any_spec] * (n + 1),
        out_shape=[jax.ShapeDtypeStruct(a.shape, a.dtype) for a in arrays]
        + [jax.ShapeDtypeStruct((7,) + small.shape, small.dtype)],
        scratch_shapes=[pltpu.SemaphoreType.DMA((total + 7,)), pltpu.SemaphoreType.DMA((total + 7,))],
    )(*arrays, small)
    return res[:n], res[n]


def _sum_own_half(g, b, name, out_dtype, tr=2048):
    _, rows, cols = g.shape
    tr = min(tr, rows)
    core = lax.axis_index("c").reshape(1)

    def body(c_ref, g_ref, b_ref, o_ref):
        del c_ref
        o_ref[...] = (g_ref[...] + b_ref[...]).astype(out_dtype)

    return pl.pallas_call(
        body, name=name,
        grid_spec=pltpu.PrefetchScalarGridSpec(
            num_scalar_prefetch=1, grid=(rows // tr,),
            in_specs=[pl.BlockSpec((None, tr, cols), lambda i, c_ref: (c_ref[0], i, 0)),
                      pl.BlockSpec((tr, cols), lambda i, c_ref: (i, 0))],
            out_specs=pl.BlockSpec((tr, cols), lambda i, c_ref: (i, 0))),
        out_shape=jax.ShapeDtypeStruct((rows, cols), out_dtype),
        compiler_params=pltpu.CompilerParams(dimension_semantics=("parallel",)),
    )(core, g, b)


def _sum_chips(own, r, slabbed, name, tr=2048):
    _, rows, cols = r.shape
    tr = min(tr, rows)
    chip = (2 * lax.axis_index("x") + lax.axis_index("y")).reshape(1)

    def body(c_ref, own_ref, r_ref, o_ref):
        del c_ref
        f = lambda val: val.astype(F32)
        o_ref[...] = (f(own_ref[...]) + f(r_ref[0])) + (f(r_ref[1]) + f(r_ref[2]))

    if slabbed:
        own_spec = pl.BlockSpec((None, tr, cols), lambda i, c_ref: (c_ref[0], i, 0))
    else:
        own_spec = pl.BlockSpec((tr, cols), lambda i, c_ref: (i, 0))
    return pl.pallas_call(
        body, name=name,
        grid_spec=pltpu.PrefetchScalarGridSpec(
            num_scalar_prefetch=1, grid=(rows // tr,),
            in_specs=[own_spec, pl.BlockSpec((3, tr, cols), lambda i, c_ref: (0, i, 0))],
            out_specs=pl.BlockSpec((tr, cols), lambda i, c_ref: (i, 0))),
        out_shape=jax.ShapeDtypeStruct((rows, cols), F32),
        compiler_params=pltpu.CompilerParams(dimension_semantics=("parallel",)),
    )(chip, own, r)


def _adamw_big(own, other, w, m, v, name, grid, g_block, g_index, w_block, w_index):
    core = lax.axis_index("c").reshape(1)
    half_axis = len(grid) - 1

    def body(c_ref, own_ref, oth_ref, w_ref, m_ref, v_ref, g_ref, d_ref, nm_ref, nv_ref):
        mine = pl.program_id(half_axis) == c_ref[0]
        gv = jnp.where(mine, own_ref[...], oth_ref[...])
        g_ref[...] = gv
        d_ref[...], nm_ref[...], nv_ref[...] = _adamw(w_ref[...], gv, m_ref[...], v_ref[...])

    g_spec = pl.BlockSpec(g_block, lambda *a: g_index(*a[:half_axis]))
    w_spec = pl.BlockSpec(w_block, lambda *a: w_index(*a[:-1]))
    shp = jax.ShapeDtypeStruct(w.shape, F32)
    return pl.pallas_call(
        body, name=name,
        grid_spec=pltpu.PrefetchScalarGridSpec(
            num_scalar_prefetch=1, grid=grid,
            in_specs=[g_spec, g_spec, w_spec, w_spec, w_spec], out_specs=[w_spec] * 4),
        out_shape=[shp] * 4,
        compiler_params=pltpu.CompilerParams(dimension_semantics=("parallel",) * len(grid)),
    )(core, own, other, w, m, v)


def _small_finish(own, others, lam, logits):
    def body(a_ref, b_ref, lam_ref, lg_ref, o_ref):
        g = (((a_ref[...] + b_ref[0]) + (b_ref[1] + b_ref[2]))
             + ((b_ref[3] + b_ref[4]) + (b_ref[5] + b_ref[6])))
        o_ref[...] = g
        o_ref[8:9, :] = g[8:9, :] * (-_sigmoid(-lam_ref[...]))
        lb = 1.0 / (1.0 + jnp.exp(lg_ref[1:2, :] - lg_ref[0:1, :]))
        d0 = g[9:10, :] * (lb * (1.0 - lb))
        o_ref[9:10, :] = d0
        o_ref[10:11, :] = -d0

    return pl.pallas_call(
        body, name="small_finish",
        in_specs=[_full((SMALL_ROWS, D_MODEL)), _full((7, SMALL_ROWS, D_MODEL)), _full((1, D_MODEL)),
                  _full((2, D_MODEL))],
        out_specs=_full((SMALL_ROWS, D_MODEL)),
        out_shape=jax.ShapeDtypeStruct((SMALL_ROWS, D_MODEL), F32),
    )(own, others, lam, logits)


def _adamw_small(ws, gs, ms, vs):
    n = len(ws)

    def body(*refs):
        w_r, g_r, m_r, v_r = refs[:n], refs[n:2 * n], refs[2 * n:3 * n], refs[3 * n:4 * n]
        d_o, m_o, v_o = refs[4 * n:5 * n], refs[5 * n:6 * n], refs[6 * n:7 * n]
        for j in range(n):
            d_o[j][...], m_o[j][...], v_o[j][...] = _adamw(w_r[j][...], g_r[j][...], m_r[j][...], v_r[j][...])

    specs = [_full(w.shape) for w in ws]
    shapes = [jax.ShapeDtypeStruct(w.shape, F32) for w in ws]
    outs = pl.pallas_call(
        body, name="adamw_small",
        in_specs=specs * 4, out_specs=specs * 3, out_shape=shapes * 3,
    )(*ws, *gs, *ms, *vs)
    return outs[:n], outs[n:2 * n], outs[2 * n:]


def _local_step(x, tgt, ut, p, pre_w, wg, cw, cb, wa, wx, ba, bx, lam, logits, gw, wo, post_w, tm,
                prepare=None):
    h, y2 = _lru_fwd(p, wa, wx, ba, bx, lam, cw, cb, tm)
    y2, o, states = _hgrn_fwd(p, logits, gw, y2, tm)
    dout, dy, sq, d_post = _out_fused(y2, wo, x, tgt, post_w, min(2 * tm, x.shape[0]))
    g_out = _dw_out(y2, dy, min(GRAD_K_TILE, x.shape[0]))
    dp, d_wa, d_wx, d_ba, d_bx, d_sp, d_cw, d_cb = _lru_bwd(dy, wo, p, h, wa, wx, ba, bx, lam, cw, cb, tm)
    dp, d_gw, d_lb = _hgrn_bwd(dy, wo, p, o, states, logits, gw, dp, tm)
    g_in = _dw_in(ut, dp, min(GRAD_K_TILE, x.shape[0]))
    parts, scatter = prepare(g_in, g_out, d_wa, d_wx) if prepare is not None else ([], [])
    grad_x, d_pre, received = _dx(dp, wg, x, dout, pre_w, tm, parts, scatter)
    small = jnp.concatenate([d_pre, d_cw, d_cb, d_ba, d_bx, d_sp, d_lb,
                             jnp.zeros((1, D_MODEL), F32), d_gw, d_post, sq,
                             jnp.zeros((SMALL_ROWS - 14, D_MODEL), F32)], axis=0)
    return grad_x, g_in, g_out, d_wa, d_wx, small, parts, received


def kernel(x, pre_norm_w, w_in, conv_w, conv_b, lru_w_a, lru_b_a, lru_w_x, lru_b_x, lru_lambda, hgrn_lb_logits, hgrn_gnorm_w, w_out, post_norm_w, loss_target, m_pre_norm_w, m_w_in, m_conv_w, m_conv_b, m_lru_w_a, m_lru_b_a, m_lru_w_x, m_lru_b_x, m_lru_lambda, m_hgrn_lb_logits, m_hgrn_gnorm_w, m_w_out, m_post_norm_w, v_pre_norm_w, v_w_in, v_conv_w, v_conv_b, v_lru_w_a, v_lru_b_a, v_lru_w_x, v_lru_b_x, v_lru_lambda, v_hgrn_lb_logits, v_hgrn_gnorm_w, v_w_out, v_post_norm_w):
    t = x.shape[1]
    chip = 2 * lax.axis_index("x") + lax.axis_index("y")

    win_units = w_in[0].astype(_MXU_DTYPE).reshape(D_MODEL, 3, UNIT).transpose(1, 0, 2)
    small_w = jnp.concatenate([conv_w[0], lru_b_a[0], lru_b_x[0]], axis=1)
    ut, p, wg4, (wo4, wa4, wx4), sw4 = _in_proj_gather(
        x[0], pre_norm_w, win_units,
        [w_out[0].astype(_MXU_DTYPE).reshape(2, 256, D_MODEL), lru_w_a[0].astype(_MXU_DTYPE).reshape(2, 128, LRU_BW),
         lru_w_x[0].astype(_MXU_DTYPE).reshape(2, 128, LRU_BW)],
        small_w, min(PROJ_TILE, t))
    wg = wg4.reshape(N_UNIT, D_MODEL, UNIT)
    wo = wo4.reshape(2 * D_MODEL, D_MODEL)
    by_block = lambda w4: w4.reshape(N_CHIPS, LRU_BLOCKS, 64, LRU_BW).transpose(1, 0, 2, 3).reshape(
        LRU_BLOCKS, LRU_BW, LRU_BW)
    wa, wx = by_block(wa4), by_block(wx4)
    sw4 = sw4.reshape(N_CHIPS, 4, 384)
    cw = sw4[:, :, 0:256].transpose(1, 0, 2).reshape(4, D_MODEL)
    ba = sw4[:, :, 256:320].transpose(1, 0, 2).reshape(1, D_MODEL)
    bx = sw4[:, :, 320:384].transpose(1, 0, 2).reshape(1, D_MODEL)

    def lru_layout(g):
        g = g.reshape(2, 2, N_CHIPS, 64, LRU_BW).transpose(0, 2, 1, 3, 4)
        return g.reshape(2, N_CHIPS * 128, LRU_BW)

    def prepare(g_in, g_out, d_wa, d_wx):
        g_in = g_in.reshape(2, N_UNIT * 512, UNIT)
        g_out = g_out.reshape(2, N_CHIPS * 256, D_MODEL)
        g_wa, g_wx = lru_layout(d_wa), lru_layout(d_wx)
        b_in, b_out, b_wa, b_wx = _core_swap([g_in, g_out, g_wa, g_wx], "reduce_swap", halved=True)
        parts = [_sum_own_half(g_in, b_in, "presum_in", _WIRE_DTYPE).reshape(N_CHIPS, 3 * 512, UNIT),
                 _sum_own_half(g_out, b_out, "presum_out", _WIRE_DTYPE).reshape(N_CHIPS, 256, D_MODEL),
                 _sum_own_half(g_wa, b_wa, "presum_wa", _WIRE_DTYPE).reshape(N_CHIPS, 128, LRU_BW),
                 _sum_own_half(g_wx, b_wx, "presum_wx", _WIRE_DTYPE).reshape(N_CHIPS, 128, LRU_BW)]
        return parts, [True] * 4

    grad_x, _, _, _, _, small, (p_in, p_out, p_wa, p_wx), (r_in, r_out, r_wa, r_wx) = _local_step(
        x[0], loss_target[0], ut, p, pre_norm_w, wg, cw, conv_b, wa, wx, ba, bx, lru_lambda, hgrn_lb_logits,
        hgrn_gnorm_w, wo, post_norm_w, min(ROW_TILE, t), prepare)
    s_in = _sum_chips(p_in, r_in, True, "sum_in")
    s_out = _sum_chips(p_out, r_out, True, "sum_out")
    s_wa = _sum_chips(p_wa, r_wa, True, "sum_wa")
    s_wx = _sum_chips(p_wx, r_wx, True, "sum_wx")
    (o_in, o_out, o_wa, o_wx), small_others = _finish_exchange([s_in, s_out, s_wa, s_wx], small)

    g_w_in, d_w_in, nm_w_in, nv_w_in = _adamw_big(
        s_in.reshape(3, 512, UNIT), o_in.reshape(3, 512, UNIT), w_in[0], m_w_in[0], v_w_in[0], "adamw_w_in", (3, 2),
        (None, 512, UNIT), lambda k: (k, 0, 0), (512, UNIT), lambda k, h: (h, k))
    g_w_out, d_w_out, nm_w_out, nv_w_out = _adamw_big(
        s_out, o_out, w_out[0], m_w_out[0], v_w_out[0], "adamw_w_out", (2,),
        (256, D_MODEL), lambda: (0, 0), (256, D_MODEL), lambda h: (h, 0))
    sq2 = lambda a: a.reshape(LRU_BW, LRU_BW)
    lru_specs = ((2,), (128, LRU_BW), lambda: (0, 0), (128, LRU_BW), lambda h: (h, 0))
    g_wa, d_wa2, nm_wa, nv_wa = _adamw_big(s_wa, o_wa, sq2(lru_w_a), sq2(m_lru_w_a), sq2(v_lru_w_a), "adamw_wa",
                                           *lru_specs)
    g_wx, d_wx2, nm_wx, nv_wx = _adamw_big(s_wx, o_wx, sq2(lru_w_x), sq2(m_lru_w_x), sq2(v_lru_w_x), "adamw_wx",
                                           *lru_specs)

    gs = _small_finish(small, small_others, lru_lambda, hgrn_lb_logits)
    loss = jnp.sum(gs[13]) * (0.5 / D_MODEL)
    g_pre = gs[0:1]
    g_cw = lax.dynamic_slice(gs[1:5], (0, chip * 256), (4, 256))
    g_cb = gs[5:6]
    g_ba = lax.dynamic_slice(gs[6].reshape(LRU_BLOCKS, N_CHIPS, 64), (0, chip, 0), (LRU_BLOCKS, 1, 64)).reshape(4, 64)
    g_bx = lax.dynamic_slice(gs[7].reshape(LRU_BLOCKS, N_CHIPS, 64), (0, chip, 0), (LRU_BLOCKS, 1, 64)).reshape(4, 64)
    g_lam = gs[8:9]
    g_lb = gs[9:11]
    g_gw = gs[11:12]
    g_post = gs[12:13]
    small_g = [g_pre, g_cw, g_cb, g_ba, g_bx, g_lam, g_lb, g_gw, g_post]
    two_d = lambda a: a.reshape(a.shape[-2:])
    small_w_list = [pre_norm_w, conv_w, conv_b, lru_b_a, lru_b_x, lru_lambda, hgrn_lb_logits, hgrn_gnorm_w, post_norm_w]
    small_m_list = [m_pre_norm_w, m_conv_w, m_conv_b, m_lru_b_a, m_lru_b_x, m_lru_lambda, m_hgrn_lb_logits,
                    m_hgrn_gnorm_w, m_post_norm_w]
    small_v_list = [v_pre_norm_w, v_conv_w, v_conv_b, v_lru_b_a, v_lru_b_x, v_lru_lambda, v_hgrn_lb_logits,
                    v_hgrn_gnorm_w, v_post_norm_w]
    sd, sm, sv = _adamw_small([two_d(a) for a in small_w_list], small_g,
                              [two_d(a) for a in small_m_list], [two_d(a) for a in small_v_list])

    def shaped(vals, refs):
        return [val.reshape(ref.shape) for val, ref in zip(vals, refs)]

    s_g = shaped(small_g, small_w_list)
    s_d = shaped(sd, small_w_list)
    s_m = shaped(sm, small_w_list)
    s_v = shaped(sv, small_w_list)

    def ordered(small, big_in, big_wa, big_wx, big_out):
        pre, cw_, cb_, ba_, bx_, lam_, lb_, gw_, post_ = small
        return [pre, big_in.reshape(w_in.shape), cw_, cb_, big_wa.reshape(lru_w_a.shape), ba_,
                big_wx.reshape(lru_w_x.shape), bx_, lam_, lb_, gw_, big_out.reshape(w_out.shape), post_]

    grads = ordered(s_g, g_w_in, g_wa, g_wx, g_w_out)
    deltas = ordered(s_d, d_w_in, d_wa2, d_wx2, d_w_out)
    new_m = ordered(s_m, nm_w_in, nm_wa, nm_wx, nm_w_out)
    new_v = ordered(s_v, nv_w_in, nv_wa, nv_wx, nv_w_out)
    return (loss, grad_x.reshape(x.shape), *grads, *deltas, *new_m, *new_v)
```

```python
import functools

import jax
import jax.numpy as jnp
from jax import lax
from jax.experimental import pallas as pl
from jax.experimental.pallas import tpu as pltpu

F32 = jnp.float32
_MXU_DTYPE = jnp.bfloat16
_WIRE_DTYPE = jnp.bfloat16

D_MODEL = 1024
N_SEG = 6
UNIT = 512
N_UNIT = 12
LRU_BLOCKS = 4
LRU_BW = 256
LRU_C = 8.0
HEADS = 8
HEAD_D = 128
CHUNK = 64
EPS = 1e-6
N_CHIPS = 4
EXP_CLAMP = 80.0

ADAM_LR = 0.001
ADAM_B1 = 0.9
ADAM_B2 = 0.999
ADAM_EPS = 1e-08
ADAM_WD = 0.01
ADAM_STEP = 10

SMALL_ROWS = 16
FWD_PIECES = 4
ROW_TILE = 512
PROJ_TILE = 1024
GRAD_K_TILE = 2048
MESH = pl.DeviceIdType.MESH


def _sigmoid(x):
    return 0.5 * jnp.tanh(0.5 * x) + 0.5


def _mx(x):
    return x.astype(_MXU_DTYPE)


def _dot_nt(a, b):
    return lax.dot_general(_mx(a), _mx(b), (((1,), (1,)), ((), ())), preferred_element_type=F32)


def _dot_tn(a, b):
    return lax.dot_general(_mx(a), _mx(b), (((0,), (0,)), ((), ())), preferred_element_type=F32)


def _rows(shape):
    return lax.broadcasted_iota(jnp.int32, shape, 0)


def _cols(shape):
    return lax.broadcasted_iota(jnp.int32, shape, 1)


def _softplus_neg(lam):
    z = -lam
    e = jnp.exp(-jnp.abs(z))
    series = e * (1.0 - e * (0.5 - e * (1.0 / 3.0 - 0.25 * e)))
    return jnp.maximum(z, 0.0) + jnp.where(e < 1e-2, series, jnp.log(1.0 + e))


def _one_minus_sq(a, y):
    series = -y * (1.0 + y * (0.5 + y * (1.0 / 6.0)))
    return jnp.where(y > -0.01, series, 1.0 - a * a)


def _cumsum_rows(x, reverse=False):
    n = x.shape[0] // 8
    row8 = _rows((8, x.shape[1]))
    out = [None] * n
    carry = None
    for g in (reversed(range(n)) if reverse else range(n)):
        blk = x[8 * g:8 * g + 8]
        for s in (1, 2, 4):
            if reverse:
                blk = blk + jnp.where(row8 < 8 - s, pltpu.roll(blk, 8 - s, 0), 0.0)
            else:
                blk = blk + jnp.where(row8 >= s, pltpu.roll(blk, s, 0), 0.0)
        if carry is not None:
            blk = blk + carry
        carry = blk[0:1] if reverse else blk[7:8]
        out[g] = blk
    return jnp.concatenate(out, axis=0)


def _shift_down(x, halo, s):
    if s == 0:
        return x
    r = pltpu.roll(x, s, 0)
    top = jnp.where(_rows(halo.shape) < s, pltpu.roll(halo, s, 0), r[0:8])
    return jnp.concatenate([top, r[8:]], axis=0)


def _shift_up(x, head, s):
    if s == 0:
        return x
    n = x.shape[0]
    r = pltpu.roll(x, n - s, 0)
    bottom = jnp.where(_rows(head.shape) >= 8 - s, pltpu.roll(head, 8 - s, 0), r[n - 8:n])
    return jnp.concatenate([r[:n - 8], bottom], axis=0)


def _lru_gates(lx, halo, cw, cb, wa, wx, ba, bx, sp):
    xs = [_shift_down(lx, halo, 3 - k) for k in range(4)]
    xc = cb + cw[0:1] * xs[0] + cw[1:2] * xs[1] + cw[2:3] * xs[2] + cw[3:4] * xs[3]
    zr, zi = [], []
    for n in range(LRU_BLOCKS):
        xb = _mx(xc[:, n * LRU_BW:(n + 1) * LRU_BW])
        zr.append(jnp.dot(xb, wa[n], preferred_element_type=F32))
        zi.append(jnp.dot(xb, wx[n], preferred_element_type=F32))
    r = _sigmoid(jnp.concatenate(zr, axis=1) + ba)
    ig = _sigmoid(jnp.concatenate(zi, axis=1) + bx)
    la = (-LRU_C * sp) * r
    a = jnp.exp(la)
    om = _one_minus_sq(a, 2.0 * la)
    return xs, xc, r, ig, a, om


def _hgrn_prep(q, fr, lb):
    sig = _sigmoid(fr)
    f = lb + (1.0 - lb) * sig
    sq = _sigmoid(q)
    b = _cumsum_rows(jnp.log(f))
    bm = b[CHUNK // 2 - 1:CHUNK // 2]
    bc = b[CHUNK - 1:CHUNK]
    em1 = jnp.exp(jnp.minimum(b - bm, EXP_CLAMP))
    em2 = jnp.exp(jnp.minimum(bm - b, EXP_CLAMP))
    qm = _mx((q * sq) * em1)
    km = _mx((1.0 - f) * em2)
    return sig, f, sq, em1, em2, qm, km, jnp.exp(bm), jnp.exp(bc - bm), jnp.exp(bc)


def _seg(ref, s, rows=slice(None)):
    return jnp.concatenate([ref[2 * s, rows, :], ref[2 * s + 1, rows, :]], axis=1)


def _pad_lanes(parts, width):
    have = sum(p.shape[1] for p in parts)
    pad = [jnp.zeros((parts[0].shape[0], width - have), parts[0].dtype)] if width > have else []
    return jnp.concatenate(list(parts) + pad, axis=1)


def _pad_rows(parts, height):
    have = sum(p.shape[0] for p in parts)
    pad = [jnp.zeros((height - have, parts[0].shape[1]), parts[0].dtype)] if height > have else []
    return jnp.concatenate(list(parts) + pad, axis=0)


def _adamw(w, g, m, v):
    m = ADAM_B1 * m + (1.0 - ADAM_B1) * g
    v = ADAM_B2 * v + (1.0 - ADAM_B2) * jnp.square(g)
    m_hat = m / (1.0 - ADAM_B1 ** ADAM_STEP)
    v_hat = v / (1.0 - ADAM_B2 ** ADAM_STEP)
    delta = -ADAM_LR * (m_hat / (jnp.sqrt(v_hat) + ADAM_EPS) + ADAM_WD * w)
    return delta, m, v


def _full(shape):
    nd = len(shape)
    return pl.BlockSpec(shape, lambda *_: (0,) * nd)


def _seq_params():
    return pltpu.CompilerParams(dimension_semantics=("arbitrary",))


def _resident(shape):
    nd = len(shape)
    return pl.BlockSpec(shape, lambda *_: (0,) * nd, pipeline_mode=pl.Buffered(1))


def _in_proj_gather(x, w, win_units, halved, small_w, tm):
    t = x.shape[0]
    nt = t // tm
    n_h = len(halved)
    n_split = 3 + n_h
    n_items = n_split + 1
    hu = D_MODEL // 2
    sub = min(256, tm)
    chip = 2 * lax.axis_index("x") + lax.axis_index("y")
    order = jnp.stack([chip] + [jnp.bitwise_xor(chip, flip) for flip in (2, 1, 3)])

    def body(order_ref, x_ref, w_ref, win_ref, *rest):
        del order_ref
        h_ins, small_ref = rest[:n_h], rest[n_h]
        ut_ref, p_ref, wg_ref = rest[n_h + 1:n_h + 4]
        h_outs, sw_ref = rest[n_h + 4:2 * n_h + 4], rest[2 * n_h + 4]
        u_s, wbuf, send, recv, fsend, frecv, osend, orecv, wsem = rest[2 * n_h + 5:]
        g, i = pl.program_id(0), pl.program_id(1)
        mx, my, c = lax.axis_index("x"), lax.axis_index("y"), lax.axis_index("c")
        me = 2 * mx + my
        peers = [(1 - mx, my), (mx, 1 - my), (1 - mx, 1 - my)]
        qs = [2 * px + py for px, py in peers]
        sibling = (mx, my, 1 - c)

        def remote(src, dst, ssem, rsem, dev):
            return pltpu.make_async_remote_copy(src_ref=src, dst_ref=dst, send_sem=ssem, recv_sem=rsem,
                                                device_id=dev, device_id_type=MESH)

        def mine(item):
            if item < 3:
                return win_ref.at[item, pl.ds(c * hu, hu)]
            return h_ins[item - 3].at[c] if item < n_split else small_ref

        def piece(item, q, h):
            if item < 3:
                return wg_ref.at[q, item, pl.ds(h * hu, hu)]
            return h_outs[item - 3].at[q, h] if item < n_split else sw_ref.at[q]

        def fetch(item, jj, q):
            px, py = peers[jj]
            return remote(mine(item), piece(item, q, c), send.at[item * 3 + jj], recv.at[item * 3 + jj], (px, py, c))

        def forward(item, jj, h):
            whole = piece(item, qs[jj], h)
            step = whole.shape[0] // FWD_PIECES
            copies = []
            for k in range(FWD_PIECES):
                part = whole.at[pl.ds(k * step, step)]
                sem = (item * 3 + jj) * FWD_PIECES + k
                copies.append(remote(part, part, fsend.at[sem], frecv.at[sem], sibling))
            return copies

        def own(item):
            if item < 3:
                src, dst = win_ref.at[item], wg_ref.at[me, item]
            elif item < n_split:
                src, dst = h_ins[item - 3], h_outs[item - 3].at[me]
            else:
                src, dst = small_ref, sw_ref.at[me]
            return remote(src, dst, osend.at[item], orecv.at[item], sibling)

        def load_units(gg):
            src = lambda k: win_ref.at[k] if gg == 0 else wg_ref.at[qs[gg - 1], k]
            return [pltpu.make_async_copy(src(k), wbuf.at[gg % 2, k], wsem.at[(gg % 2) * 3 + k]) for k in range(3)]

        def in_pass(gg):
            return pl.when(g == gg)

        def pass_on(gg):
            for item in range(3):
                fetch(item, gg, qs[gg]).wait_recv()
                for cp in forward(item, gg, c):
                    cp.start()

        @pl.when(i == 0)
        def _():
            @in_pass(0)
            def _():
                for item in range(3):
                    for jj in range(2):
                        fetch(item, jj, me).start()
                for item in range(n_items):
                    own(item).start()
                for cp in load_units(0):
                    cp.start()

            @in_pass(1)
            def _():
                for item in range(3):
                    fetch(item, 2, me).start()

            @in_pass(2)
            def _():
                for item in range(3, n_items):
                    for jj in range(3):
                        fetch(item, jj, me).start()

            for gg in range(N_CHIPS):
                @in_pass(gg)
                def _(gg=gg):
                    for cp in load_units(gg):
                        cp.wait()

        @pl.when(g == 0)
        def _():
            xv = x_ref[...]
            rstd = lax.rsqrt(jnp.mean(xv * xv, axis=-1, keepdims=True) + EPS)
            u = xv * rstd * w_ref[...]
            ut_ref[...] = u.T.astype(_MXU_DTYPE)
            u_s[pl.ds(pl.multiple_of(i * tm, tm), tm), :] = _mx(u)

        slot = g % 2
        for r in range(tm // sub):
            ur = u_s[pl.ds(pl.multiple_of(i * tm + r * sub, sub), sub), :]
            for k in range(3):
                p_ref[k, r * sub:(r + 1) * sub, :] = jnp.dot(ur, wbuf[slot, k], preferred_element_type=F32)

        @pl.when(i == nt // 2)
        def _():
            for gg in (1, 2):
                in_pass(gg)(functools.partial(pass_on, gg))

            @in_pass(N_CHIPS - 1)
            def _():
                for item in range(3, n_items):
                    for jj in range(3):
                        fetch(item, jj, qs[jj]).wait_recv()
                        if item < n_split:
                            for cp in forward(item, jj, c):
                                cp.start()

        @pl.when(i == nt - 1)
        def _():
            in_pass(0)(functools.partial(pass_on, 0))
            for gg in range(3):
                @in_pass(gg)
                def _(gg=gg):
                    for item in range(3):
                        for cp in forward(item, gg, 1 - c):
                            cp.wait_recv()
                    for cp in load_units(gg + 1):
                        cp.start()

            @in_pass(N_CHIPS - 1)
            def _():
                for item in range(3, n_split):
                    for jj in range(3):
                        for cp in forward(item, jj, 1 - c):
                            cp.wait_recv()
                for item in range(n_items):
                    own(item).wait_recv()
                for item in range(n_items):
                    for jj in range(3):
                        fetch(item, jj, me).wait_send()
                        if item < n_split:
                            for cp in forward(item, jj, c):
                                cp.wait_send()
                    own(item).wait_send()

    any_spec = pl.BlockSpec(memory_space=pl.ANY)
    first_pass = lambda g, i: jnp.where(g == 0, i, nt - 1)
    res = pl.pallas_call(
        body, name="in_proj",
        grid_spec=pltpu.PrefetchScalarGridSpec(
            num_scalar_prefetch=1, grid=(N_CHIPS, nt),
            in_specs=[pl.BlockSpec((tm, D_MODEL), lambda g, i, o: (first_pass(g, i), 0)),
                      pl.BlockSpec((1, D_MODEL), lambda g, i, o: (0, 0))] + [any_spec] * (n_h + 2),
            out_specs=[pl.BlockSpec((D_MODEL, tm), lambda g, i, o: (0, first_pass(g, i))),
                       pl.BlockSpec((3, tm, UNIT), lambda g, i, o: (o[g], i, 0))]
            + [any_spec] * (n_h + 2),
            scratch_shapes=[pltpu.VMEM((t, D_MODEL), _MXU_DTYPE), pltpu.VMEM((2, 3, D_MODEL, UNIT), _MXU_DTYPE),
                            pltpu.SemaphoreType.DMA((3 * n_items,)), pltpu.SemaphoreType.DMA((3 * n_items,)),
                            pltpu.SemaphoreType.DMA((3 * n_split * FWD_PIECES,)),
                            pltpu.SemaphoreType.DMA((3 * n_split * FWD_PIECES,)),
                            pltpu.SemaphoreType.DMA((n_items,)), pltpu.SemaphoreType.DMA((n_items,)),
                            pltpu.SemaphoreType.DMA((6,))]),
        out_shape=[jax.ShapeDtypeStruct((D_MODEL, t), _MXU_DTYPE), jax.ShapeDtypeStruct((N_UNIT, t, UNIT), F32),
                   jax.ShapeDtypeStruct((N_CHIPS,) + win_units.shape, win_units.dtype)]
        + [jax.ShapeDtypeStruct((N_CHIPS,) + a.shape, a.dtype) for a in halved]
        + [jax.ShapeDtypeStruct((N_CHIPS,) + small_w.shape, small_w.dtype)],
        compiler_params=pltpu.CompilerParams(dimension_semantics=("arbitrary", "arbitrary")),
    )(order, x, w, win_units, *halved, small_w)
    return res[0], res[1], res[2], res[3:3 + n_h], res[3 + n_h]


def _lru_fwd(p, wa, wx, ba, bx, lam, cw, cb, tm):
    t = p.shape[1]
    ng = tm // 8

    def body(lx_ref, halo_ref, lg_ref, wa_ref, wx_ref, ba_ref, bx_ref, lam_ref, cw_ref, cb_ref,
             h_ref, y_ref, a_s, u_s, hc_s):
        i = pl.program_id(0)

        @pl.when(i == 0)
        def _():
            hc_s[...] = jnp.zeros_like(hc_s)

        sp = _softplus_neg(lam_ref[...])
        halo = jnp.where(i == 0, 0.0, _seg(halo_ref, 0))
        _, xc, _, ig, a, om = _lru_gates(_seg(lx_ref, 0), halo, cw_ref[...], cb_ref[...], wa_ref[...],
                                         wx_ref[...], ba_ref[...], bx_ref[...], sp)
        a_s[...] = a
        u_s[...] = jnp.sqrt(om) * (ig * xc)
        row8 = _rows((8, D_MODEL))

        def group(g, hc):
            rows = pl.ds(pl.multiple_of(g * 8, 8), 8)
            av = a_s[rows, :]
            uv = u_s[rows, :]
            for s in (1, 2, 4):
                a_sh = jnp.where(row8 >= s, pltpu.roll(av, s, 0), 1.0)
                u_sh = jnp.where(row8 >= s, pltpu.roll(uv, s, 0), 0.0)
                uv = av * u_sh + uv
                av = av * a_sh
            hh = av * hc + uv
            h_ref[rows, :] = hh
            return jnp.broadcast_to(hh[7:8, :], (8, D_MODEL))

        hc_s[...] = lax.fori_loop(0, ng, group, hc_s[...])
        lg = _seg(lg_ref, 0)
        y_ref[...] = (h_ref[...] * (lg * _sigmoid(lg))).astype(_MXU_DTYPE)

    seg = lambda s: pl.BlockSpec((2, tm, UNIT), lambda i: (s, i, 0))
    return pl.pallas_call(
        body, name="lru_fwd", grid=(t // tm,),
        in_specs=[seg(0),
                  pl.BlockSpec((2, 8, UNIT), lambda i: (0, jnp.maximum(i * (tm // 8) - 1, 0), 0)),
                  seg(1),
                  _full((LRU_BLOCKS, LRU_BW, LRU_BW)), _full((LRU_BLOCKS, LRU_BW, LRU_BW)),
                  _full((1, D_MODEL)), _full((1, D_MODEL)), _full((1, D_MODEL)),
                  _full((4, D_MODEL)), _full((1, D_MODEL))],
        out_specs=[pl.BlockSpec((tm, D_MODEL), lambda i: (i, 0)),
                   pl.BlockSpec((None, tm, D_MODEL), lambda i: (0, i, 0))],
        out_shape=[jax.ShapeDtypeStruct((t, D_MODEL), F32),
                   jax.ShapeDtypeStruct((2, t, D_MODEL), _MXU_DTYPE)],
        scratch_shapes=[pltpu.VMEM((tm, D_MODEL), F32), pltpu.VMEM((tm, D_MODEL), F32),
                        pltpu.VMEM((8, D_MODEL), F32)],
        compiler_params=_seq_params(),
    )(p, p, p, wa, wx, ba, bx, lam, cw, cb)


def _hgrn_fwd(p, logits, gw, y2, tm):
    t = p.shape[1]
    nc = tm // CHUNK

    def body(qf_ref, vh_ref, lg_ref, gw_ref, y_in, y_ref, o_ref, st_ref, state):
        del y_in
        i = pl.program_id(0)

        @pl.when(i == 0)
        def _():
            state[...] = jnp.zeros_like(state)

        l0 = lg_ref[0:1, :]
        l1 = lg_ref[1:2, :]
        lb = 1.0 / (1.0 + jnp.exp(l1 - l0))
        gwv = gw_ref[...]
        causal = _rows((CHUNK, CHUNK)) >= _cols((CHUNK, CHUNK))

        for c in range(nc):
            rows = pl.ds(c * CHUNK, CHUNK)
            v = _seg(vh_ref, 0, rows)
            hg = _seg(vh_ref, 1, rows)
            _, _, _, _, _, qm, km, ebm, ecm, ebc = _hgrn_prep(_seg(qf_ref, 0, rows), _seg(qf_ref, 1, rows), lb)
            ke = _mx(km.astype(F32) * ecm)
            gate = gwv * (hg * _sigmoid(hg))
            o_parts, y_parts = [], []
            for h in range(HEADS):
                ln = slice(h * HEAD_D, (h + 1) * HEAD_D)
                s0 = state[h]
                st_ref[c, h] = s0
                vt = _mx(v[:, ln].T)
                att = _mx(jnp.where(causal, _dot_nt(qm[:, ln], km[:, ln]), 0.0))
                o = _dot_nt(_pad_lanes([qm[:, ln], att], 2 * HEAD_D),
                            _pad_lanes([_mx(s0 * ebm[:, ln]), vt], 2 * HEAD_D))
                state[h] = s0 * ebc[:, ln] + jnp.dot(vt, ke[:, ln], preferred_element_type=F32)
                rstd = lax.rsqrt(jnp.mean(o * o, axis=-1, keepdims=True) + EPS)
                o_parts.append(o)
                y_parts.append((o * rstd * gate[:, ln]).astype(_MXU_DTYPE))
            o_ref[rows, :] = jnp.concatenate(o_parts, axis=1)
            y_ref[rows, :] = jnp.concatenate(y_parts, axis=1)

    pair = lambda s: pl.BlockSpec((4, tm, UNIT), lambda i: (s, i, 0))
    return pl.pallas_call(
        body, name="hgrn_fwd", grid=(t // tm,),
        in_specs=[pair(1), pair(2), _full((2, D_MODEL)), _full((1, D_MODEL)),
                  pl.BlockSpec(memory_space=pl.ANY)],
        out_specs=[pl.BlockSpec((None, tm, D_MODEL), lambda i: (1, i, 0)),
                   pl.BlockSpec((tm, D_MODEL), lambda i: (i, 0)),
                   pl.BlockSpec((nc, HEADS, HEAD_D, HEAD_D), lambda i: (i, 0, 0, 0))],
        out_shape=[jax.ShapeDtypeStruct((2, t, D_MODEL), _MXU_DTYPE),
                   jax.ShapeDtypeStruct((t, D_MODEL), F32),
                   jax.ShapeDtypeStruct((t // CHUNK, HEADS, HEAD_D, HEAD_D), F32)],
        scratch_shapes=[pltpu.VMEM((HEADS, HEAD_D, HEAD_D), F32)],
        input_output_aliases={4: 0},
        compiler_params=_seq_params(),
    )(p, p, logits, gw, y2)


def _out_fused(y2, wo, x, tgt, wpost, tm):
    t = x.shape[0]
    inv_d = 1.0 / D_MODEL

    def body(y_ref, wo_ref, x_ref, t_ref, w_ref, dout_ref, dy_ref, sq_ref, dw_ref):
        i = pl.program_id(0)

        @pl.when(i == 0)
        def _():
            sq_ref[...] = jnp.zeros_like(sq_ref)
            dw_ref[...] = jnp.zeros_like(dw_ref)

        y = (jnp.dot(y_ref[0], wo_ref[0:D_MODEL, :], preferred_element_type=F32)
             + jnp.dot(y_ref[1], wo_ref[D_MODEL:2 * D_MODEL, :], preferred_element_type=F32))
        w = w_ref[...]
        rstd = lax.rsqrt(jnp.mean(y * y, axis=-1, keepdims=True) + EPS)
        n = y * rstd
        err = (x_ref[...] + n * w) - t_ref[...]
        sq_ref[...] += jnp.sum(err * err, axis=0, keepdims=True)
        dout = err * inv_d
        dout_ref[...] = dout
        dw_ref[...] += jnp.sum(dout * n, axis=0, keepdims=True)
        dn = dout * w
        dy_ref[...] = (rstd * (dn - n * jnp.mean(dn * n, axis=-1, keepdims=True))).astype(_MXU_DTYPE)

    row = pl.BlockSpec((tm, D_MODEL), lambda i: (i, 0))
    return pl.pallas_call(
        body, name="out_fused", grid=(t // tm,),
        in_specs=[pl.BlockSpec((2, tm, D_MODEL), lambda i: (0, i, 0)), _full((2 * D_MODEL, D_MODEL)),
                  row, row, _full((1, D_MODEL))],
        out_specs=[row, row, _full((1, D_MODEL)), _full((1, D_MODEL))],
        out_shape=[jax.ShapeDtypeStruct((t, D_MODEL), F32), jax.ShapeDtypeStruct((t, D_MODEL), _MXU_DTYPE),
                   jax.ShapeDtypeStruct((1, D_MODEL), F32), jax.ShapeDtypeStruct((1, D_MODEL), F32)],
        compiler_params=_seq_params(),
    )(y2, wo, x, tgt, wpost)


def _dw_out(y2, dy, tt):
    t = dy.shape[0]
    hr = UNIT // 2

    def body(y_ref, dy_ref, g_ref):
        @pl.when(pl.program_id(1) == 0)
        def _():
            g_ref[...] = jnp.zeros_like(g_ref)

        r = _dot_tn(y_ref[...], dy_ref[...])
        g_ref[0] += r[0:hr]
        g_ref[1] += r[hr:UNIT]

    return pl.pallas_call(
        body, name="dw_out", grid=(N_CHIPS, t // tt),
        in_specs=[pl.BlockSpec((None, tt, UNIT), lambda c, k: (c // 2, k, c % 2)),
                  pl.BlockSpec((tt, D_MODEL), lambda c, k: (k, 0))],
        out_specs=pl.BlockSpec((2, None, hr, D_MODEL), lambda c, k: (0, c, 0, 0)),
        out_shape=jax.ShapeDtypeStruct((2, N_CHIPS, hr, D_MODEL), F32),
        compiler_params=pltpu.CompilerParams(dimension_semantics=("arbitrary", "arbitrary")),
    )(y2, dy)


def _lru_bwd(dy, wo, p, h, wa, wx, ba, bx, lam, cw, cb, tm):
    t = dy.shape[0]
    nt = t // tm
    ng = tm // 8

    def body(dy_ref, wo_ref, lx_ref, halo_ref, lg_ref, h_ref, hhalo_ref, wa_ref, wx_ref, ba_ref, bx_ref,
             lam_ref, cw_ref, cb_ref,
             dp_ref, dwa_ref, dwx_ref, dba_ref, dbx_ref, dsp_ref, dcw_ref, dcb_ref,
             c_s, in_s, dh_s, dhc_s, afirst_s, dxc_s):
        i = pl.program_id(0)
        first = i == nt - 1

        @pl.when(i == 0)
        def _():
            for ref in (dwa_ref, dwx_ref, dba_ref, dbx_ref, dsp_ref, dcw_ref, dcb_ref, dhc_s, afirst_s, dxc_s):
                ref[...] = jnp.zeros_like(ref)

        sp = _softplus_neg(lam_ref[...])
        cwv = cw_ref[...]
        wav = wa_ref[...]
        wxv = wx_ref[...]
        halo = jnp.where(first, 0.0, _seg(halo_ref, 0))
        xs, xc, r, ig, a, om = _lru_gates(_seg(lx_ref, 0), halo, cwv, cb_ref[...], wav, wxv,
                                          ba_ref[...], bx_ref[...], sp)
        inv_mult = lax.rsqrt(om)
        mult = om * inv_mult
        d_y = _dot_nt(dy_ref[...], wo_ref[...])
        lg = _seg(lg_ref, 0)
        sl = _sigmoid(lg)
        hv = h_ref[...]
        d_lg = d_y * hv * (sl * (1.0 + lg * (1.0 - sl)))

        c_s[...] = _shift_up(a, afirst_s[...], 1)
        in_s[...] = d_y * (lg * sl)
        row8 = _rows((8, D_MODEL))

        def group(gg, carry):
            rows = pl.ds(pl.multiple_of((ng - 1 - gg) * 8, 8), 8)
            cv = c_s[rows, :]
            uv = in_s[rows, :]
            for s in (1, 2, 4):
                c_sh = jnp.where(row8 < 8 - s, pltpu.roll(cv, 8 - s, 0), 1.0)
                u_sh = jnp.where(row8 < 8 - s, pltpu.roll(uv, 8 - s, 0), 0.0)
                uv = cv * u_sh + uv
                cv = cv * c_sh
            hh = cv * carry + uv
            dh_s[rows, :] = hh
            return jnp.broadcast_to(hh[0:1, :], (8, D_MODEL))

        dhc_s[...] = lax.fori_loop(0, ng, group, dhc_s[...])
        afirst_s[...] = a[0:8]
        dh = dh_s[...]

        h_prev = _shift_down(hv, jnp.where(first, 0.0, hhalo_ref[...]), 1)
        gx = ig * xc
        d_mult = dh * gx
        d_i = dh * (mult * xc)
        d_xc = dh * (mult * ig)
        d_la = (dh * h_prev) * a - d_mult * ((a * a) * inv_mult)
        dsp_ref[...] += jnp.sum(d_la * r, axis=0, keepdims=True) * (-LRU_C)
        d_zr = (d_la * (-LRU_C * sp)) * (r * (1.0 - r))
        d_zi = d_i * (ig * (1.0 - ig))
        dba_ref[...] += jnp.sum(d_zr, axis=0, keepdims=True)
        dbx_ref[...] += jnp.sum(d_zi, axis=0, keepdims=True)
        back = []
        for n in range(LRU_BLOCKS):
            ln = slice(n * LRU_BW, (n + 1) * LRU_BW)
            xb = _mx(xc[:, ln])
            zr_n = _mx(d_zr[:, ln])
            zi_n = _mx(d_zi[:, ln])
            dwa_ref[n] += _dot_tn(xb, zr_n)
            dwx_ref[n] += _dot_tn(xb, zi_n)
            back.append(_dot_nt(zr_n, wav[n]) + _dot_nt(zi_n, wxv[n]))
        d_xc = d_xc + jnp.concatenate(back, axis=1)
        dcb_ref[...] += jnp.sum(d_xc, axis=0, keepdims=True)
        for k in range(4):
            dcw_ref[k:k + 1, :] += jnp.sum(d_xc * xs[k], axis=0, keepdims=True)
        head = dxc_s[...]
        d_lx = cwv[3:4] * d_xc
        for k in range(3):
            d_lx = d_lx + cwv[k:k + 1] * _shift_up(d_xc, head, 3 - k)
        dxc_s[...] = d_xc[0:8]
        dp_ref[0] = d_lx.astype(_MXU_DTYPE)
        dp_ref[1] = d_lg.astype(_MXU_DTYPE)

    rev = lambda i: nt - 1 - i
    seg = lambda s: pl.BlockSpec((2, tm, UNIT), lambda i: (s, rev(i), 0))
    halo_row = lambda i: jnp.maximum(rev(i) * (tm // 8) - 1, 0)
    row = pl.BlockSpec((tm, D_MODEL), lambda i: (rev(i), 0))
    vec = _full((1, D_MODEL))
    wblk = _full((LRU_BLOCKS, LRU_BW, LRU_BW))
    return pl.pallas_call(
        body, name="lru_bwd", grid=(nt,),
        in_specs=[row, pl.BlockSpec((D_MODEL, D_MODEL), lambda i: (0, 0)),
                  seg(0), pl.BlockSpec((2, 8, UNIT), lambda i: (0, halo_row(i), 0)), seg(1),
                  row, pl.BlockSpec((8, D_MODEL), lambda i: (halo_row(i), 0)),
                  wblk, wblk, vec, vec, vec, _full((4, D_MODEL)), vec],
        out_specs=[pl.BlockSpec((2, tm, D_MODEL), lambda i: (2, rev(i), 0)),
                   wblk, wblk, vec, vec, vec, _full((4, D_MODEL)), vec],
        out_shape=[jax.ShapeDtypeStruct((N_SEG, t, D_MODEL), _MXU_DTYPE),
                   jax.ShapeDtypeStruct((LRU_BLOCKS, LRU_BW, LRU_BW), F32),
                   jax.ShapeDtypeStruct((LRU_BLOCKS, LRU_BW, LRU_BW), F32),
                   jax.ShapeDtypeStruct((1, D_MODEL), F32), jax.ShapeDtypeStruct((1, D_MODEL), F32),
                   jax.ShapeDtypeStruct((1, D_MODEL), F32), jax.ShapeDtypeStruct((4, D_MODEL), F32),
                   jax.ShapeDtypeStruct((1, D_MODEL), F32)],
        scratch_shapes=[pltpu.VMEM((tm, D_MODEL), F32), pltpu.VMEM((tm, D_MODEL), F32),
                        pltpu.VMEM((tm, D_MODEL), F32), pltpu.VMEM((8, D_MODEL), F32),
                        pltpu.VMEM((8, D_MODEL), F32), pltpu.VMEM((8, D_MODEL), F32)],
        compiler_params=_seq_params(),
    )(dy, wo, p, p, p, h, h, wa, wx, ba, bx, lam, cw, cb)


def _hgrn_bwd(dy, wo, p, o, states, logits, gw, dp, tm):
    t = dy.shape[0]
    nt = t // tm
    nc = tm // CHUNK

    def body(dy_ref, wo_ref, qf_ref, vh_ref, o_ref, st_ref, lg_ref, gw_ref, dp_in,
             dp_ref, dgw_ref, dlb_ref, dyh_s, dstate, snext):
        del dp_in
        i = pl.program_id(0)

        @pl.when(i == 0)
        def _():
            for ref in (dgw_ref, dlb_ref, dstate, snext):
                ref[...] = jnp.zeros_like(ref)

        dyh_s[...] = _dot_nt(dy_ref[...], wo_ref[...])
        l0 = lg_ref[0:1, :]
        l1 = lg_ref[1:2, :]
        lb = 1.0 / (1.0 + jnp.exp(l1 - l0))
        gwv = gw_ref[...]
        causal = _rows((CHUNK, CHUNK)) >= _cols((CHUNK, CHUNK))
        wide = 2 * HEAD_D
        zeros_h = jnp.zeros((CHUNK, HEAD_D), _MXU_DTYPE)

        for c in reversed(range(nc)):
            rows = pl.ds(c * CHUNK, CHUNK)
            q = _seg(qf_ref, 0, rows)
            hg = _seg(vh_ref, 1, rows)
            ov = o_ref[rows, :]
            dyh = dyh_s[rows, :]
            sig, f, sq, em1, em2, qm, km, ebm, ecm, ebc = _hgrn_prep(q, _seg(qf_ref, 1, rows), lb)
            ke = _mx(km.astype(F32) * ecm)
            vb = _mx(_seg(vh_ref, 0, rows))
            sg = _sigmoid(hg)
            d_on = dyh * (hg * sg)
            x_p, y_p, dv_p, n_p, ex_p = [], [], [], [], []
            for h in range(HEADS):
                ln = slice(h * HEAD_D, (h + 1) * HEAD_D)
                o_h = ov[:, ln]
                rstd = lax.rsqrt(jnp.mean(o_h * o_h, axis=-1, keepdims=True) + EPS)
                n = o_h * rstd
                dn = d_on[:, ln] * gwv[:, ln]
                d_o = rstd * (dn - n * jnp.mean(dn * n, axis=-1, keepdims=True))
                d_ob = _mx(d_o)
                d_ot = _mx(d_o.T)
                s0 = st_ref[c, h]
                ds = dstate[h]
                both = _dot_nt(jnp.concatenate([_pad_lanes([qm[:, ln]], wide), _pad_lanes([zeros_h, d_ob], wide)], axis=0),
                               jnp.concatenate([km[:, ln], vb[:, ln]], axis=1))
                att = jnp.where(causal, both[0:CHUNK], 0.0)
                d_att = jnp.where(causal, both[CHUNK:2 * CHUNK], 0.0)
                x_p.append(jnp.dot(_pad_lanes([d_ob, _mx(d_att)], wide),
                                   _pad_rows([_mx(s0 * ebm[:, ln]), km[:, ln]], wide), preferred_element_type=F32))
                ds_c = ds * ecm[:, ln]
                y_p.append(jnp.dot(_pad_lanes([vb[:, ln], _mx(d_att.T)], wide),
                                   _pad_rows([_mx(ds_c), qm[:, ln]], wide), preferred_element_type=F32))
                dv_p.append(_dot_nt(_pad_lanes([ke[:, ln], _mx(att.T)], wide), _pad_lanes([_mx(ds), d_ot], wide)))
                ex_p.append(jnp.sum(ds * snext[h], axis=0, keepdims=True))
                dstate[h] = (ds_c + jnp.dot(d_ot, qm[:, ln], preferred_element_type=F32)) * ebm[:, ln]
                snext[h] = s0
                n_p.append(n)
            cat = lambda parts: jnp.concatenate(parts, axis=1)
            x_all, y_all, n_all = cat(x_p), cat(y_p), cat(n_p)
            dg = _cumsum_rows(qm.astype(F32) * x_all - km.astype(F32) * y_all, reverse=True) + cat(ex_p)
            df = dg / f - em2 * y_all
            dgw_ref[...] += jnp.sum(d_on * n_all, axis=0, keepdims=True)
            dlb_ref[...] += jnp.sum(df * (1.0 - sig), axis=0, keepdims=True)
            dp_ref[0, rows, :] = ((em1 * x_all) * (sq * (1.0 + q * (1.0 - sq)))).astype(_MXU_DTYPE)
            dp_ref[1, rows, :] = (df * ((1.0 - lb) * sig * (1.0 - sig))).astype(_MXU_DTYPE)
            dp_ref[2, rows, :] = cat(dv_p).astype(_MXU_DTYPE)
            dp_ref[3, rows, :] = (dyh * (n_all * gwv) * (sg * (1.0 + hg * (1.0 - sg)))).astype(_MXU_DTYPE)

    rev = lambda i: nt - 1 - i
    pair = lambda s: pl.BlockSpec((4, tm, UNIT), lambda i: (s, rev(i), 0))
    row = pl.BlockSpec((tm, D_MODEL), lambda i: (rev(i), 0))
    vec = _full((1, D_MODEL))
    return pl.pallas_call(
        body, name="hgrn_bwd", grid=(nt,),
        in_specs=[row, pl.BlockSpec((D_MODEL, D_MODEL), lambda i: (1, 0)), pair(1), pair(2), row,
                  pl.BlockSpec((nc, HEADS, HEAD_D, HEAD_D), lambda i: (rev(i), 0, 0, 0)),
                  _full((2, D_MODEL)), vec, pl.BlockSpec(memory_space=pl.ANY)],
        out_specs=[pl.BlockSpec((4, tm, D_MODEL), lambda i: (0, rev(i), 0)), vec, vec],
        out_shape=[jax.ShapeDtypeStruct((N_SEG, t, D_MODEL), _MXU_DTYPE),
                   jax.ShapeDtypeStruct((1, D_MODEL), F32), jax.ShapeDtypeStruct((1, D_MODEL), F32)],
        scratch_shapes=[pltpu.VMEM((tm, D_MODEL), F32), pltpu.VMEM((HEADS, HEAD_D, HEAD_D), F32),
                        pltpu.VMEM((HEADS, HEAD_D, HEAD_D), F32)],
        input_output_aliases={8: 0},
        compiler_params=_seq_params(),
    )(dy, wo, p, p, o, states, logits, gw, dp)


def _dp_unit(m):
    return ((m // 2 + 2) % N_SEG) * 2 + m % 2


def _dw_in(ut, dp, tt):
    t = ut.shape[1]
    hr = D_MODEL // 2

    def body(ut_ref, dp_ref, g_ref):
        @pl.when(pl.program_id(1) == 0)
        def _():
            g_ref[...] = jnp.zeros_like(g_ref)

        r = jnp.dot(ut_ref[...], dp_ref[...], preferred_element_type=F32)
        for h in range(2):
            for half in range(2):
                g_ref[h, half] += r[h * hr:(h + 1) * hr, half * UNIT:(half + 1) * UNIT]

    return pl.pallas_call(
        body, name="dw_in", grid=(N_SEG, t // tt),
        in_specs=[pl.BlockSpec((D_MODEL, tt), lambda s, k: (0, k)),
                  pl.BlockSpec((None, tt, D_MODEL), lambda s, k: (s, k, 0))],
        out_specs=pl.BlockSpec((2, 2, hr, UNIT), lambda s, k: (0, (s + 2) % N_SEG, 0, 0)),
        out_shape=jax.ShapeDtypeStruct((2, N_UNIT, hr, UNIT), F32),
        compiler_params=pltpu.CompilerParams(dimension_semantics=("arbitrary", "arbitrary")),
    )(ut, dp)


def _dx(dp, wg, x, dout, wpre, tm, parts, scatter):
    t = x.shape[0]
    n = len(parts)
    nt = t // tm

    def body(*refs):
        dp_ref, w_ref, x_ref, dout_ref, wpre_ref = refs[:5]
        ins = refs[5:5 + n]
        gx_ref, dw_ref = refs[5 + n:7 + n]
        outs = refs[7 + n:7 + 2 * n]
        send_sems, recv_sems = refs[7 + 2 * n:]
        i = pl.program_id(0)

        @pl.when(i == 0)
        def _():
            dw_ref[...] = jnp.zeros_like(dw_ref)
            for cp in _chip_copies(ins, outs, send_sems, recv_sems, scatter):
                cp.start()

        du = None
        for m in range(N_UNIT):
            part = _dot_nt(dp_ref[m // 2, :, (m % 2) * UNIT:(m % 2 + 1) * UNIT], w_ref[_dp_unit(m)])
            du = part if du is None else du + part
        xv = x_ref[...]
        rstd = lax.rsqrt(jnp.mean(xv * xv, axis=-1, keepdims=True) + EPS)
        xn = xv * rstd
        dw_ref[...] += jnp.sum(du * xn, axis=0, keepdims=True)
        dn = du * wpre_ref[...]
        gx_ref[...] = dout_ref[...] + rstd * (dn - xn * jnp.mean(dn * xn, axis=-1, keepdims=True))

        @pl.when(i == nt - 1)
        def _():
            copies = _chip_copies(ins, outs, send_sems, recv_sems, scatter)
            for cp in copies:
                cp.wait_recv()
            for cp in copies:
                cp.wait_send()

    row = pl.BlockSpec((tm, D_MODEL), lambda i: (i, 0))
    any_spec = pl.BlockSpec(memory_space=pl.ANY)
    res = pl.pallas_call(
        body, name="dx", grid=(nt,),
        in_specs=[pl.BlockSpec((N_SEG, tm, D_MODEL), lambda i: (0, i, 0)), _resident((N_UNIT, D_MODEL, UNIT)),
                  row, row, _full((1, D_MODEL))] + [any_spec] * n,
        out_specs=[row, _full((1, D_MODEL))] + [any_spec] * n,
        out_shape=[jax.ShapeDtypeStruct((t, D_MODEL), F32), jax.ShapeDtypeStruct((1, D_MODEL), F32)]
        + _chip_exchange_shapes(parts, scatter),
        scratch_shapes=[pltpu.SemaphoreType.DMA((max(3 * n, 1),)), pltpu.SemaphoreType.DMA((max(3 * n, 1),))],
        compiler_params=_seq_params(),
    )(dp, wg, x, dout, wpre, *parts)
    return res[0], res[1], res[2:]


CHUNK_BYTES = 256 * 1024
MAX_CHUNKS = 16


def _n_chunks(shape, dtype, rows):
    nbytes = jnp.dtype(dtype).itemsize
    for d in shape:
        nbytes *= d
    k = max(1, min(MAX_CHUNKS, rows, nbytes // CHUNK_BYTES))
    while rows % k:
        k -= 1
    return k


def _chip_exchange_shapes(arrays, scatter):
    return [jax.ShapeDtypeStruct((3,) + tuple(a.shape[1:] if sc else a.shape), a.dtype)
            for a, sc in zip(arrays, scatter)]


def _chip_copies(ins, outs, send_sems, recv_sems, scatter):
    if not ins:
        return []
    x, y, c = lax.axis_index("x"), lax.axis_index("y"), lax.axis_index("c")
    peers = [(1 - x, y), (x, 1 - y), (1 - x, 1 - y)]
    copies = []
    for a in range(len(ins)):
        for j, (px, py) in enumerate(peers):
            copies.append(pltpu.make_async_remote_copy(
                src_ref=ins[a].at[2 * px + py] if scatter[a] else ins[a], dst_ref=outs[a].at[j],
                send_sem=send_sems.at[a * 3 + j], recv_sem=recv_sems.at[a * 3 + j],
                device_id=(px, py, c), device_id_type=MESH))
    return copies


def _core_swap(arrays, name, halved):
    n = len(arrays)
    shapes, n_sems = _core_swap_plan(arrays, halved)

    def body(*refs):
        copies = _core_swap_copies(arrays, halved, refs[:n], refs[n:2 * n], refs[2 * n], refs[2 * n + 1])
        for cp in copies:
            cp.start()
        for cp in copies:
            cp.wait_recv()
        for cp in copies:
            cp.wait_send()

    any_spec = pl.BlockSpec(memory_space=pl.ANY)
    return pl.pallas_call(
        body, name=name,
        in_specs=[any_spec] * n, out_specs=[any_spec] * n,
        out_shape=shapes,
        scratch_shapes=[pltpu.SemaphoreType.DMA((n_sems,)), pltpu.SemaphoreType.DMA((n_sems,))],
    )(*arrays)


def _core_swap_plan(arrays, halved):
    shapes = [a.shape[1:] if halved else a.shape for a in arrays]
    n_sems = sum(_n_chunks(s, a.dtype, s[0]) for s, a in zip(shapes, arrays))
    return [jax.ShapeDtypeStruct(s, a.dtype) for s, a in zip(shapes, arrays)], max(n_sems, 1)


def _core_swap_copies(arrays, halved, ins, outs, send_sems, recv_sems):
    if not arrays:
        return []
    x, y, c = lax.axis_index("x"), lax.axis_index("y"), lax.axis_index("c")
    copies, sem = [], 0
    for a, arr in enumerate(arrays):
        shape = arr.shape[1:] if halved else arr.shape
        k = _n_chunks(shape, arr.dtype, shape[0])
        step = shape[0] // k
        for j in range(k):
            rows = pl.ds(j * step, step)
            copies.append(pltpu.make_async_remote_copy(
                src_ref=ins[a].at[1 - c, rows] if halved else ins[a].at[rows], dst_ref=outs[a].at[rows],
                send_sem=send_sems.at[sem], recv_sem=recv_sems.at[sem],
                device_id=(x, y, 1 - c), device_id_type=MESH))
            sem += 1
    return copies


def _finish_exchange(arrays, small):
    n = len(arrays)
    ks = [_n_chunks(a.shape, a.dtype, a.shape[0]) for a in arrays]
    offs = [sum(ks[:a]) for a in range(n)]
    total = sum(ks)

    def body(*refs):
        ins, small_ref = refs[:n], refs[n]
        outs, gathered = refs[n + 1:2 * n + 1], refs[2 * n + 1]
        send_sems, recv_sems = refs[2 * n + 2:]
        x, y, c = lax.axis_index("x"), lax.axis_index("y"), lax.axis_index("c")
        copies = []
        for a in range(n):
            step = arrays[a].shape[0] // ks[a]
            for j in range(ks[a]):
                rows = pl.ds(j * step, step)
                copies.append(pltpu.make_async_remote_copy(
                    src_ref=ins[a].at[rows], dst_ref=outs[a].at[rows],
                    send_sem=send_sems.at[offs[a] + j], recv_sem=recv_sems.at[offs[a] + j],
                    device_id=(x, y, 1 - c), device_id_type=MESH))
        for slot in range(7):
            dx, dy, dc = (slot + 1) % 2, ((slot + 1) // 2) % 2, (slot + 1) // 4
            copies.append(pltpu.make_async_remote_copy(
                src_ref=small_ref, dst_ref=gathered.at[slot],
                send_sem=send_sems.at[total + slot], recv_sem=recv_sems.at[total + slot],
                device_id=(x + dx - 2 * x * dx, y + dy - 2 * y * dy, c + dc - 2 * c * dc), device_id_type=MESH))
        for cp in copies:
            cp.start()
        for cp in copies:
            cp.wait_recv()
        for cp in copies:
            cp.wait_send()

    any_spec = pl.BlockSpec(memory_space=pl.ANY)
    res = pl.pallas_call(
        body, name="reduce_cores",
        in_specs=[any_spec] * (n + 1), out_specs=[any_spec] * (n + 1),
        out_shape=[jax.ShapeDtypeStruct(a.shape, a.dtype) for a in arrays]
        + [jax.ShapeDtypeStruct((7,) + small.shape, small.dtype)],
        scratch_shapes=[pltpu.SemaphoreType.DMA((total + 7,)), pltpu.SemaphoreType.DMA((total + 7,))],
    )(*arrays, small)
    return res[:n], res[n]


def _sum_own_half(g, b, name, out_dtype, tr=2048):
    _, rows, cols = g.shape
    tr = min(tr, rows)
    core = lax.axis_index("c").reshape(1)

    def body(c_ref, g_ref, b_ref, o_ref):
        del c_ref
        o_ref[...] = (g_ref[...] + b_ref[...]).astype(out_dtype)

    return pl.pallas_call(
        body, name=name,
        grid_spec=pltpu.PrefetchScalarGridSpec(
            num_scalar_prefetch=1, grid=(rows // tr,),
            in_specs=[pl.BlockSpec((None, tr, cols), lambda i, c_ref: (c_ref[0], i, 0)),
                      pl.BlockSpec((tr, cols), lambda i, c_ref: (i, 0))],
            out_specs=pl.BlockSpec((tr, cols), lambda i, c_ref: (i, 0))),
        out_shape=jax.ShapeDtypeStruct((rows, cols), out_dtype),
        compiler_params=pltpu.CompilerParams(dimension_semantics=("parallel",)),
    )(core, g, b)


def _sum_chips(own, r, slabbed, name, tr=2048):
    _, rows, cols = r.shape
    tr = min(tr, rows)
    chip = (2 * lax.axis_index("x") + lax.axis_index("y")).reshape(1)

    def body(c_ref, own_ref, r_ref, o_ref):
        del c_ref
        f = lambda val: val.astype(F32)
        o_ref[...] = (f(own_ref[...]) + f(r_ref[0])) + (f(r_ref[1]) + f(r_ref[2]))

    if slabbed:
        own_spec = pl.BlockSpec((None, tr, cols), lambda i, c_ref: (c_ref[0], i, 0))
    else:
        own_spec = pl.BlockSpec((tr, cols), lambda i, c_ref: (i, 0))
    return pl.pallas_call(
        body, name=name,
        grid_spec=pltpu.PrefetchScalarGridSpec(
            num_scalar_prefetch=1, grid=(rows // tr,),
            in_specs=[own_spec, pl.BlockSpec((3, tr, cols), lambda i, c_ref: (0, i, 0))],
            out_specs=pl.BlockSpec((tr, cols), lambda i, c_ref: (i, 0))),
        out_shape=jax.ShapeDtypeStruct((rows, cols), F32),
        compiler_params=pltpu.CompilerParams(dimension_semantics=("parallel",)),
    )(chip, own, r)


def _adamw_big(own, other, w, m, v, name, grid, g_block, g_index, w_block, w_index):
    core = lax.axis_index("c").reshape(1)
    half_axis = len(grid) - 1

    def body(c_ref, own_ref, oth_ref, w_ref, m_ref, v_ref, g_ref, d_ref, nm_ref, nv_ref):
        mine = pl.program_id(half_axis) == c_ref[0]
        gv = jnp.where(mine, own_ref[...], oth_ref[...])
        g_ref[...] = gv
        d_ref[...], nm_ref[...], nv_ref[...] = _adamw(w_ref[...], gv, m_ref[...], v_ref[...])

    g_spec = pl.BlockSpec(g_block, lambda *a: g_index(*a[:half_axis]))
    w_spec = pl.BlockSpec(w_block, lambda *a: w_index(*a[:-1]))
    shp = jax.ShapeDtypeStruct(w.shape, F32)
    return pl.pallas_call(
        body, name=name,
        grid_spec=pltpu.PrefetchScalarGridSpec(
            num_scalar_prefetch=1, grid=grid,
            in_specs=[g_spec, g_spec, w_spec, w_spec, w_spec], out_specs=[w_spec] * 4),
        out_shape=[shp] * 4,
        compiler_params=pltpu.CompilerParams(dimension_semantics=("parallel",) * len(grid)),
    )(core, own, other, w, m, v)


def _small_finish(own, others, lam, logits):
    def body(a_ref, b_ref, lam_ref, lg_ref, o_ref):
        g = (((a_ref[...] + b_ref[0]) + (b_ref[1] + b_ref[2]))
             + ((b_ref[3] + b_ref[4]) + (b_ref[5] + b_ref[6])))
        o_ref[...] = g
        o_ref[8:9, :] = g[8:9, :] * (-_sigmoid(-lam_ref[...]))
        lb = 1.0 / (1.0 + jnp.exp(lg_ref[1:2, :] - lg_ref[0:1, :]))
        d0 = g[9:10, :] * (lb * (1.0 - lb))
        o_ref[9:10, :] = d0
        o_ref[10:11, :] = -d0

    return pl.pallas_call(
        body, name="small_finish",
        in_specs=[_full((SMALL_ROWS, D_MODEL)), _full((7, SMALL_ROWS, D_MODEL)), _full((1, D_MODEL)),
                  _full((2, D_MODEL))],
        out_specs=_full((SMALL_ROWS, D_MODEL)),
        out_shape=jax.ShapeDtypeStruct((SMALL_ROWS, D_MODEL), F32),
    )(own, others, lam, logits)


def _adamw_small(ws, gs, ms, vs):
    n = len(ws)

    def body(*refs):
        w_r, g_r, m_r, v_r = refs[:n], refs[n:2 * n], refs[2 * n:3 * n], refs[3 * n:4 * n]
        d_o, m_o, v_o = refs[4 * n:5 * n], refs[5 * n:6 * n], refs[6 * n:7 * n]
        for j in range(n):
            d_o[j][...], m_o[j][...], v_o[j][...] = _adamw(w_r[j][...], g_r[j][...], m_r[j][...], v_r[j][...])

    specs = [_full(w.shape) for w in ws]
    shapes = [jax.ShapeDtypeStruct(w.shape, F32) for w in ws]
    outs = pl.pallas_call(
        body, name="adamw_small",
        in_specs=specs * 4, out_specs=specs * 3, out_shape=shapes * 3,
    )(*ws, *gs, *ms, *vs)
    return outs[:n], outs[n:2 * n], outs[2 * n:]


def _local_step(x, tgt, ut, p, pre_w, wg, cw, cb, wa, wx, ba, bx, lam, logits, gw, wo, post_w, tm,
                prepare=None):
    h, y2 = _lru_fwd(p, wa, wx, ba, bx, lam, cw, cb, min(2 * tm, x.shape[0]))
    y2, o, states = _hgrn_fwd(p, logits, gw, y2, tm)
    dout, dy, sq, d_post = _out_fused(y2, wo, x, tgt, post_w, min(2 * tm, x.shape[0]))
    g_out = _dw_out(y2, dy, min(GRAD_K_TILE, x.shape[0]))
    dp, d_wa, d_wx, d_ba, d_bx, d_sp, d_cw, d_cb = _lru_bwd(dy, wo, p, h, wa, wx, ba, bx, lam, cw, cb, tm)
    dp, d_gw, d_lb = _hgrn_bwd(dy, wo, p, o, states, logits, gw, dp, tm)
    g_in = _dw_in(ut, dp, min(GRAD_K_TILE, x.shape[0]))
    parts, scatter = prepare(g_in, g_out, d_wa, d_wx) if prepare is not None else ([], [])
    grad_x, d_pre, received = _dx(dp, wg, x, dout, pre_w, tm, parts, scatter)
    small = jnp.concatenate([d_pre, d_cw, d_cb, d_ba, d_bx, d_sp, d_lb,
                             jnp.zeros((1, D_MODEL), F32), d_gw, d_post, sq,
                             jnp.zeros((SMALL_ROWS - 14, D_MODEL), F32)], axis=0)
    return grad_x, g_in, g_out, d_wa, d_wx, small, parts, received


def kernel(x, pre_norm_w, w_in, conv_w, conv_b, lru_w_a, lru_b_a, lru_w_x, lru_b_x, lru_lambda, hgrn_lb_logits, hgrn_gnorm_w, w_out, post_norm_w, loss_target, m_pre_norm_w, m_w_in, m_conv_w, m_conv_b, m_lru_w_a, m_lru_b_a, m_lru_w_x, m_lru_b_x, m_lru_lambda, m_hgrn_lb_logits, m_hgrn_gnorm_w, m_w_out, m_post_norm_w, v_pre_norm_w, v_w_in, v_conv_w, v_conv_b, v_lru_w_a, v_lru_b_a, v_lru_w_x, v_lru_b_x, v_lru_lambda, v_hgrn_lb_logits, v_hgrn_gnorm_w, v_w_out, v_post_norm_w):
    t = x.shape[1]
    chip = 2 * lax.axis_index("x") + lax.axis_index("y")

    win_units = w_in[0].astype(_MXU_DTYPE).reshape(D_MODEL, 3, UNIT).transpose(1, 0, 2)
    small_w = jnp.concatenate([conv_w[0], lru_b_a[0], lru_b_x[0]], axis=1)
    ut, p, wg4, (wo4, wa4, wx4), sw4 = _in_proj_gather(
        x[0], pre_norm_w, win_units,
        [w_out[0].astype(_MXU_DTYPE).reshape(2, 256, D_MODEL), lru_w_a[0].astype(_MXU_DTYPE).reshape(2, 128, LRU_BW),
         lru_w_x[0].astype(_MXU_DTYPE).reshape(2, 128, LRU_BW)],
        small_w, min(PROJ_TILE, t))
    wg = wg4.reshape(N_UNIT, D_MODEL, UNIT)
    wo = wo4.reshape(2 * D_MODEL, D_MODEL)
    by_block = lambda w4: w4.reshape(N_CHIPS, LRU_BLOCKS, 64, LRU_BW).transpose(1, 0, 2, 3).reshape(
        LRU_BLOCKS, LRU_BW, LRU_BW)
    wa, wx = by_block(wa4), by_block(wx4)
    sw4 = sw4.reshape(N_CHIPS, 4, 384)
    cw = sw4[:, :, 0:256].transpose(1, 0, 2).reshape(4, D_MODEL)
    ba = sw4[:, :, 256:320].transpose(1, 0, 2).reshape(1, D_MODEL)
    bx = sw4[:, :, 320:384].transpose(1, 0, 2).reshape(1, D_MODEL)

    def lru_layout(g):
        g = g.reshape(2, 2, N_CHIPS, 64, LRU_BW).transpose(0, 2, 1, 3, 4)
        return g.reshape(2, N_CHIPS * 128, LRU_BW)

    def prepare(g_in, g_out, d_wa, d_wx):
        g_in = g_in.reshape(2, N_UNIT * 512, UNIT)
        g_out = g_out.reshape(2, N_CHIPS * 256, D_MODEL)
        g_wa, g_wx = lru_layout(d_wa), lru_layout(d_wx)
        b_in, b_out, b_wa, b_wx = _core_swap([g_in, g_out, g_wa, g_wx], "reduce_swap", halved=True)
        parts = [_sum_own_half(g_in, b_in, "presum_in", _WIRE_DTYPE).reshape(N_CHIPS, 3 * 512, UNIT),
                 _sum_own_half(g_out, b_out, "presum_out", _WIRE_DTYPE).reshape(N_CHIPS, 256, D_MODEL),
                 _sum_own_half(g_wa, b_wa, "presum_wa", _WIRE_DTYPE).reshape(N_CHIPS, 128, LRU_BW),
                 _sum_own_half(g_wx, b_wx, "presum_wx", _WIRE_DTYPE).reshape(N_CHIPS, 128, LRU_BW)]
        return parts, [True] * 4

    grad_x, _, _, _, _, small, (p_in, p_out, p_wa, p_wx), (r_in, r_out, r_wa, r_wx) = _local_step(
        x[0], loss_target[0], ut, p, pre_norm_w, wg, cw, conv_b, wa, wx, ba, bx, lru_lambda, hgrn_lb_logits,
        hgrn_gnorm_w, wo, post_norm_w, min(ROW_TILE, t), prepare)
    s_in = _sum_chips(p_in, r_in, True, "sum_in")
    s_out = _sum_chips(p_out, r_out, True, "sum_out")
    s_wa = _sum_chips(p_wa, r_wa, True, "sum_wa")
    s_wx = _sum_chips(p_wx, r_wx, True, "sum_wx")
    (o_in, o_out, o_wa, o_wx), small_others = _finish_exchange([s_in, s_out, s_wa, s_wx], small)

    g_w_in, d_w_in, nm_w_in, nv_w_in = _adamw_big(
        s_in.reshape(3, 512, UNIT), o_in.reshape(3, 512, UNIT), w_in[0], m_w_in[0], v_w_in[0], "adamw_w_in", (3, 2),
        (None, 512, UNIT), lambda k: (k, 0, 0), (512, UNIT), lambda k, h: (h, k))
    g_w_out, d_w_out, nm_w_out, nv_w_out = _adamw_big(
        s_out, o_out, w_out[0], m_w_out[0], v_w_out[0], "adamw_w_out", (2,),
        (256, D_MODEL), lambda: (0, 0), (256, D_MODEL), lambda h: (h, 0))
    sq2 = lambda a: a.reshape(LRU_BW, LRU_BW)
    lru_specs = ((2,), (128, LRU_BW), lambda: (0, 0), (128, LRU_BW), lambda h: (h, 0))
    g_wa, d_wa2, nm_wa, nv_wa = _adamw_big(s_wa, o_wa, sq2(lru_w_a), sq2(m_lru_w_a), sq2(v_lru_w_a), "adamw_wa",
                                           *lru_specs)
    g_wx, d_wx2, nm_wx, nv_wx = _adamw_big(s_wx, o_wx, sq2(lru_w_x), sq2(m_lru_w_x), sq2(v_lru_w_x), "adamw_wx",
                                           *lru_specs)

    gs = _small_finish(small, small_others, lru_lambda, hgrn_lb_logits)
    loss = jnp.sum(gs[13]) * (0.5 / D_MODEL)
    g_pre = gs[0:1]
    g_cw = lax.dynamic_slice(gs[1:5], (0, chip * 256), (4, 256))
    g_cb = gs[5:6]
    g_ba = lax.dynamic_slice(gs[6].reshape(LRU_BLOCKS, N_CHIPS, 64), (0, chip, 0), (LRU_BLOCKS, 1, 64)).reshape(4, 64)
    g_bx = lax.dynamic_slice(gs[7].reshape(LRU_BLOCKS, N_CHIPS, 64), (0, chip, 0), (LRU_BLOCKS, 1, 64)).reshape(4, 64)
    g_lam = gs[8:9]
    g_lb = gs[9:11]
    g_gw = gs[11:12]
    g_post = gs[12:13]
    small_g = [g_pre, g_cw, g_cb, g_ba, g_bx, g_lam, g_lb, g_gw, g_post]
    two_d = lambda a: a.reshape(a.shape[-2:])
    small_w_list = [pre_norm_w, conv_w, conv_b, lru_b_a, lru_b_x, lru_lambda, hgrn_lb_logits, hgrn_gnorm_w, post_norm_w]
    small_m_list = [m_pre_norm_w, m_conv_w, m_conv_b, m_lru_b_a, m_lru_b_x, m_lru_lambda, m_hgrn_lb_logits,
                    m_hgrn_gnorm_w, m_post_norm_w]
    small_v_list = [v_pre_norm_w, v_conv_w, v_conv_b, v_lru_b_a, v_lru_b_x, v_lru_lambda, v_hgrn_lb_logits,
                    v_hgrn_gnorm_w, v_post_norm_w]
    sd, sm, sv = _adamw_small([two_d(a) for a in small_w_list], small_g,
                              [two_d(a) for a in small_m_list], [two_d(a) for a in small_v_list])

    def shaped(vals, refs):
        return [val.reshape(ref.shape) for val, ref in zip(vals, refs)]

    s_g = shaped(small_g, small_w_list)
    s_d = shaped(sd, small_w_list)
    s_m = shaped(sm, small_w_list)
    s_v = shaped(sv, small_w_list)

    def ordered(small, big_in, big_wa, big_wx, big_out):
        pre, cw_, cb_, ba_, bx_, lam_, lb_, gw_, post_ = small
        return [pre, big_in.reshape(w_in.shape), cw_, cb_, big_wa.reshape(lru_w_a.shape), ba_,
                big_wx.reshape(lru_w_x.shape), bx_, lam_, lb_, gw_, big_out.reshape(w_out.shape), post_]

    grads = ordered(s_g, g_w_in, g_wa, g_wx, g_w_out)
    deltas = ordered(s_d, d_w_in, d_wa2, d_wx2, d_w_out)
    new_m = ordered(s_m, nm_w_in, nm_wa, nm_wx, nm_w_out)
    new_v = ordered(s_v, nv_w_in, nv_wa, nv_wx, nv_w_out)
    return (loss, grad_x.reshape(x.shape), *grads, *deltas, *new_m, *new_v)
```

```python
import functools

import jax
import jax.numpy as jnp
from jax import lax
from jax.experimental import pallas as pl
from jax.experimental.pallas import tpu as pltpu

F32 = jnp.float32
_MXU_DTYPE = jnp.bfloat16
_WIRE_DTYPE = jnp.bfloat16

D_MODEL = 1024
N_SEG = 6
UNIT = 512
N_UNIT = 12
LRU_BLOCKS = 4
LRU_BW = 256
LRU_C = 8.0
HEADS = 8
HEAD_D = 128
CHUNK = 64
EPS = 1e-6
N_CHIPS = 4
EXP_CLAMP = 80.0

ADAM_LR = 0.001
ADAM_B1 = 0.9
ADAM_B2 = 0.999
ADAM_EPS = 1e-08
ADAM_WD = 0.01
ADAM_STEP = 10

SMALL_ROWS = 16
FWD_PIECES = 4
ROW_TILE = 512
PROJ_TILE = 1024
GRAD_K_TILE = 4096
MESH = pl.DeviceIdType.MESH


def _sigmoid(x):
    return 0.5 * jnp.tanh(0.5 * x) + 0.5


def _mx(x):
    return x.astype(_MXU_DTYPE)


def _dot_nt(a, b):
    return lax.dot_general(_mx(a), _mx(b), (((1,), (1,)), ((), ())), preferred_element_type=F32)


def _dot_tn(a, b):
    return lax.dot_general(_mx(a), _mx(b), (((0,), (0,)), ((), ())), preferred_element_type=F32)


def _rows(shape):
    return lax.broadcasted_iota(jnp.int32, shape, 0)


def _cols(shape):
    return lax.broadcasted_iota(jnp.int32, shape, 1)


def _softplus_neg(lam):
    z = -lam
    e = jnp.exp(-jnp.abs(z))
    series = e * (1.0 - e * (0.5 - e * (1.0 / 3.0 - 0.25 * e)))
    return jnp.maximum(z, 0.0) + jnp.where(e < 1e-2, series, jnp.log(1.0 + e))


def _one_minus_sq(a, y):
    series = -y * (1.0 + y * (0.5 + y * (1.0 / 6.0)))
    return jnp.where(y > -0.01, series, 1.0 - a * a)


def _cumsum_rows(x, reverse=False):
    n = x.shape[0] // 8
    row8 = _rows((8, x.shape[1]))
    out = [None] * n
    carry = None
    for g in (reversed(range(n)) if reverse else range(n)):
        blk = x[8 * g:8 * g + 8]
        for s in (1, 2, 4):
            if reverse:
                blk = blk + jnp.where(row8 < 8 - s, pltpu.roll(blk, 8 - s, 0), 0.0)
            else:
                blk = blk + jnp.where(row8 >= s, pltpu.roll(blk, s, 0), 0.0)
        if carry is not None:
            blk = blk + carry
        carry = blk[0:1] if reverse else blk[7:8]
        out[g] = blk
    return jnp.concatenate(out, axis=0)


def _shift_down(x, halo, s):
    if s == 0:
        return x
    r = pltpu.roll(x, s, 0)
    top = jnp.where(_rows(halo.shape) < s, pltpu.roll(halo, s, 0), r[0:8])
    return jnp.concatenate([top, r[8:]], axis=0)


def _shift_up(x, head, s):
    if s == 0:
        return x
    n = x.shape[0]
    r = pltpu.roll(x, n - s, 0)
    bottom = jnp.where(_rows(head.shape) >= 8 - s, pltpu.roll(head, 8 - s, 0), r[n - 8:n])
    return jnp.concatenate([r[:n - 8], bottom], axis=0)


def _lru_gates(lx, halo, cw, cb, wa, wx, ba, bx, sp):
    xs = [_shift_down(lx, halo, 3 - k) for k in range(4)]
    xc = cb + cw[0:1] * xs[0] + cw[1:2] * xs[1] + cw[2:3] * xs[2] + cw[3:4] * xs[3]
    zr, zi = [], []
    for n in range(LRU_BLOCKS):
        xb = _mx(xc[:, n * LRU_BW:(n + 1) * LRU_BW])
        zr.append(jnp.dot(xb, wa[n], preferred_element_type=F32))
        zi.append(jnp.dot(xb, wx[n], preferred_element_type=F32))
    r = _sigmoid(jnp.concatenate(zr, axis=1) + ba)
    ig = _sigmoid(jnp.concatenate(zi, axis=1) + bx)
    la = (-LRU_C * sp) * r
    a = jnp.exp(la)
    om = _one_minus_sq(a, 2.0 * la)
    return xs, xc, r, ig, a, om


def _hgrn_prep(q, fr, lb):
    sig = _sigmoid(fr)
    f = lb + (1.0 - lb) * sig
    sq = _sigmoid(q)
    b = _cumsum_rows(jnp.log(f))
    bm = b[CHUNK // 2 - 1:CHUNK // 2]
    bc = b[CHUNK - 1:CHUNK]
    em1 = jnp.exp(jnp.minimum(b - bm, EXP_CLAMP))
    em2 = jnp.exp(jnp.minimum(bm - b, EXP_CLAMP))
    qm = _mx((q * sq) * em1)
    km = _mx((1.0 - f) * em2)
    return sig, f, sq, em1, em2, qm, km, jnp.exp(bm), jnp.exp(bc - bm), jnp.exp(bc)


def _seg(ref, s, rows=slice(None)):
    return jnp.concatenate([ref[2 * s, rows, :], ref[2 * s + 1, rows, :]], axis=1)


def _pad_lanes(parts, width):
    have = sum(p.shape[1] for p in parts)
    pad = [jnp.zeros((parts[0].shape[0], width - have), parts[0].dtype)] if width > have else []
    return jnp.concatenate(list(parts) + pad, axis=1)


def _pad_rows(parts, height):
    have = sum(p.shape[0] for p in parts)
    pad = [jnp.zeros((height - have, parts[0].shape[1]), parts[0].dtype)] if height > have else []
    return jnp.concatenate(list(parts) + pad, axis=0)


def _adamw(w, g, m, v):
    m = ADAM_B1 * m + (1.0 - ADAM_B1) * g
    v = ADAM_B2 * v + (1.0 - ADAM_B2) * jnp.square(g)
    m_hat = m / (1.0 - ADAM_B1 ** ADAM_STEP)
    v_hat = v / (1.0 - ADAM_B2 ** ADAM_STEP)
    delta = -ADAM_LR * (m_hat / (jnp.sqrt(v_hat) + ADAM_EPS) + ADAM_WD * w)
    return delta, m, v


def _full(shape):
    nd = len(shape)
    return pl.BlockSpec(shape, lambda *_: (0,) * nd)


def _seq_params():
    return pltpu.CompilerParams(dimension_semantics=("arbitrary",))


def _resident(shape):
    nd = len(shape)
    return pl.BlockSpec(shape, lambda *_: (0,) * nd, pipeline_mode=pl.Buffered(1))


def _in_proj_gather(x, w, win_units, halved, small_w, tm):
    t = x.shape[0]
    nt = t // tm
    n_h = len(halved)
    n_split = 3 + n_h
    n_items = n_split + 1
    hu = D_MODEL // 2
    sub = min(256, tm)
    chip = 2 * lax.axis_index("x") + lax.axis_index("y")
    order = jnp.stack([chip] + [jnp.bitwise_xor(chip, flip) for flip in (2, 1, 3)])

    def body(order_ref, x_ref, w_ref, win_ref, *rest):
        del order_ref
        h_ins, small_ref = rest[:n_h], rest[n_h]
        ut_ref, p_ref, wg_ref = rest[n_h + 1:n_h + 4]
        h_outs, sw_ref = rest[n_h + 4:2 * n_h + 4], rest[2 * n_h + 4]
        u_s, wbuf, send, recv, fsend, frecv, osend, orecv, wsem = rest[2 * n_h + 5:]
        g, i = pl.program_id(0), pl.program_id(1)
        mx, my, c = lax.axis_index("x"), lax.axis_index("y"), lax.axis_index("c")
        me = 2 * mx + my
        peers = [(1 - mx, my), (mx, 1 - my), (1 - mx, 1 - my)]
        qs = [2 * px + py for px, py in peers]
        sibling = (mx, my, 1 - c)

        def remote(src, dst, ssem, rsem, dev):
            return pltpu.make_async_remote_copy(src_ref=src, dst_ref=dst, send_sem=ssem, recv_sem=rsem,
                                                device_id=dev, device_id_type=MESH)

        def mine(item):
            if item < 3:
                return win_ref.at[item, pl.ds(c * hu, hu)]
            return h_ins[item - 3].at[c] if item < n_split else small_ref

        def piece(item, q, h):
            if item < 3:
                return wg_ref.at[q, item, pl.ds(h * hu, hu)]
            return h_outs[item - 3].at[q, h] if item < n_split else sw_ref.at[q]

        def fetch(item, jj, q):
            px, py = peers[jj]
            return remote(mine(item), piece(item, q, c), send.at[item * 3 + jj], recv.at[item * 3 + jj], (px, py, c))

        def forward(item, jj, h):
            whole = piece(item, qs[jj], h)
            step = whole.shape[0] // FWD_PIECES
            copies = []
            for k in range(FWD_PIECES):
                part = whole.at[pl.ds(k * step, step)]
                sem = (item * 3 + jj) * FWD_PIECES + k
                copies.append(remote(part, part, fsend.at[sem], frecv.at[sem], sibling))
            return copies

        def own(item):
            if item < 3:
                src, dst = win_ref.at[item], wg_ref.at[me, item]
            elif item < n_split:
                src, dst = h_ins[item - 3], h_outs[item - 3].at[me]
            else:
                src, dst = small_ref, sw_ref.at[me]
            return remote(src, dst, osend.at[item], orecv.at[item], sibling)

        def load_units(gg):
            src = lambda k: win_ref.at[k] if gg == 0 else wg_ref.at[qs[gg - 1], k]
            return [pltpu.make_async_copy(src(k), wbuf.at[gg % 2, k], wsem.at[(gg % 2) * 3 + k]) for k in range(3)]

        def in_pass(gg):
            return pl.when(g == gg)

        def pass_on(gg):
            for item in range(3):
                fetch(item, gg, qs[gg]).wait_recv()
                for cp in forward(item, gg, c):
                    cp.start()

        @pl.when(i == 0)
        def _():
            @in_pass(0)
            def _():
                for item in range(3):
                    for jj in range(2):
                        fetch(item, jj, me).start()
                for item in range(n_items):
                    own(item).start()
                for cp in load_units(0):
                    cp.start()

            @in_pass(1)
            def _():
                for item in range(3):
                    fetch(item, 2, me).start()

            @in_pass(2)
            def _():
                for item in range(3, n_items):
                    for jj in range(3):
                        fetch(item, jj, me).start()

            for gg in range(N_CHIPS):
                @in_pass(gg)
                def _(gg=gg):
                    for cp in load_units(gg):
                        cp.wait()

        @pl.when(g == 0)
        def _():
            xv = x_ref[...]
            rstd = lax.rsqrt(jnp.mean(xv * xv, axis=-1, keepdims=True) + EPS)
            u = xv * rstd * w_ref[...]
            ut_ref[...] = u.T.astype(_MXU_DTYPE)
            u_s[pl.ds(pl.multiple_of(i * tm, tm), tm), :] = _mx(u)

        slot = g % 2
        for r in range(tm // sub):
            ur = u_s[pl.ds(pl.multiple_of(i * tm + r * sub, sub), sub), :]
            for k in range(3):
                p_ref[k, r * sub:(r + 1) * sub, :] = jnp.dot(ur, wbuf[slot, k], preferred_element_type=F32)

        @pl.when(i == nt // 2)
        def _():
            for gg in (1, 2):
                in_pass(gg)(functools.partial(pass_on, gg))

            @in_pass(N_CHIPS - 1)
            def _():
                for item in range(3, n_items):
                    for jj in range(3):
                        fetch(item, jj, qs[jj]).wait_recv()
                        if item < n_split:
                            for cp in forward(item, jj, c):
                                cp.start()

        @pl.when(i == nt - 1)
        def _():
            in_pass(0)(functools.partial(pass_on, 0))
            for gg in range(3):
                @in_pass(gg)
                def _(gg=gg):
                    for item in range(3):
                        for cp in forward(item, gg, 1 - c):
                            cp.wait_recv()
                    for cp in load_units(gg + 1):
                        cp.start()

            @in_pass(N_CHIPS - 1)
            def _():
                for item in range(3, n_split):
                    for jj in range(3):
                        for cp in forward(item, jj, 1 - c):
                            cp.wait_recv()
                for item in range(n_items):
                    own(item).wait_recv()
                for item in range(n_items):
                    for jj in range(3):
                        fetch(item, jj, me).wait_send()
                        if item < n_split:
                            for cp in forward(item, jj, c):
                                cp.wait_send()
                    own(item).wait_send()

    any_spec = pl.BlockSpec(memory_space=pl.ANY)
    first_pass = lambda g, i: jnp.where(g == 0, i, nt - 1)
    res = pl.pallas_call(
        body, name="in_proj",
        grid_spec=pltpu.PrefetchScalarGridSpec(
            num_scalar_prefetch=1, grid=(N_CHIPS, nt),
            in_specs=[pl.BlockSpec((tm, D_MODEL), lambda g, i, o: (first_pass(g, i), 0)),
                      pl.BlockSpec((1, D_MODEL), lambda g, i, o: (0, 0))] + [any_spec] * (n_h + 2),
            out_specs=[pl.BlockSpec((D_MODEL, tm), lambda g, i, o: (0, first_pass(g, i))),
                       pl.BlockSpec((3, tm, UNIT), lambda g, i, o: (o[g], i, 0))]
            + [any_spec] * (n_h + 2),
            scratch_shapes=[pltpu.VMEM((t, D_MODEL), _MXU_DTYPE), pltpu.VMEM((2, 3, D_MODEL, UNIT), _MXU_DTYPE),
                            pltpu.SemaphoreType.DMA((3 * n_items,)), pltpu.SemaphoreType.DMA((3 * n_items,)),
                            pltpu.SemaphoreType.DMA((3 * n_split * FWD_PIECES,)),
                            pltpu.SemaphoreType.DMA((3 * n_split * FWD_PIECES,)),
                            pltpu.SemaphoreType.DMA((n_items,)), pltpu.SemaphoreType.DMA((n_items,)),
                            pltpu.SemaphoreType.DMA((6,))]),
        out_shape=[jax.ShapeDtypeStruct((D_MODEL, t), _MXU_DTYPE), jax.ShapeDtypeStruct((N_UNIT, t, UNIT), F32),
                   jax.ShapeDtypeStruct((N_CHIPS,) + win_units.shape, win_units.dtype)]
        + [jax.ShapeDtypeStruct((N_CHIPS,) + a.shape, a.dtype) for a in halved]
        + [jax.ShapeDtypeStruct((N_CHIPS,) + small_w.shape, small_w.dtype)],
        compiler_params=pltpu.CompilerParams(dimension_semantics=("arbitrary", "arbitrary")),
    )(order, x, w, win_units, *halved, small_w)
    return res[0], res[1], res[2], res[3:3 + n_h], res[3 + n_h]


def _lru_fwd(p, wa, wx, ba, bx, lam, cw, cb, tm):
    t = p.shape[1]
    ng = tm // 8

    def body(lx_ref, halo_ref, lg_ref, wa_ref, wx_ref, ba_ref, bx_ref, lam_ref, cw_ref, cb_ref,
             h_ref, y_ref, a_s, u_s, hc_s):
        i = pl.program_id(0)

        @pl.when(i == 0)
        def _():
            hc_s[...] = jnp.zeros_like(hc_s)

        sp = _softplus_neg(lam_ref[...])
        halo = jnp.where(i == 0, 0.0, _seg(halo_ref, 0))
        _, xc, _, ig, a, om = _lru_gates(_seg(lx_ref, 0), halo, cw_ref[...], cb_ref[...], wa_ref[...],
                                         wx_ref[...], ba_ref[...], bx_ref[...], sp)
        a_s[...] = a
        u_s[...] = jnp.sqrt(om) * (ig * xc)
        row8 = _rows((8, D_MODEL))

        def group(g, hc):
            rows = pl.ds(pl.multiple_of(g * 8, 8), 8)
            av = a_s[rows, :]
            uv = u_s[rows, :]
            for s in (1, 2, 4):
                a_sh = jnp.where(row8 >= s, pltpu.roll(av, s, 0), 1.0)
                u_sh = jnp.where(row8 >= s, pltpu.roll(uv, s, 0), 0.0)
                uv = av * u_sh + uv
                av = av * a_sh
            hh = av * hc + uv
            h_ref[rows, :] = hh
            return jnp.broadcast_to(hh[7:8, :], (8, D_MODEL))

        hc_s[...] = lax.fori_loop(0, ng, group, hc_s[...])
        lg = _seg(lg_ref, 0)
        y_ref[...] = (h_ref[...] * (lg * _sigmoid(lg))).astype(_MXU_DTYPE)

    seg = lambda s: pl.BlockSpec((2, tm, UNIT), lambda i: (s, i, 0))
    return pl.pallas_call(
        body, name="lru_fwd", grid=(t // tm,),
        in_specs=[seg(0),
                  pl.BlockSpec((2, 8, UNIT), lambda i: (0, jnp.maximum(i * (tm // 8) - 1, 0), 0)),
                  seg(1),
                  _full((LRU_BLOCKS, LRU_BW, LRU_BW)), _full((LRU_BLOCKS, LRU_BW, LRU_BW)),
                  _full((1, D_MODEL)), _full((1, D_MODEL)), _full((1, D_MODEL)),
                  _full((4, D_MODEL)), _full((1, D_MODEL))],
        out_specs=[pl.BlockSpec((tm, D_MODEL), lambda i: (i, 0)),
                   pl.BlockSpec((None, tm, D_MODEL), lambda i: (0, i, 0))],
        out_shape=[jax.ShapeDtypeStruct((t, D_MODEL), F32),
                   jax.ShapeDtypeStruct((2, t, D_MODEL), _MXU_DTYPE)],
        scratch_shapes=[pltpu.VMEM((tm, D_MODEL), F32), pltpu.VMEM((tm, D_MODEL), F32),
                        pltpu.VMEM((8, D_MODEL), F32)],
        compiler_params=_seq_params(),
    )(p, p, p, wa, wx, ba, bx, lam, cw, cb)


def _hgrn_fwd(p, logits, gw, y2, tm):
    t = p.shape[1]
    nc = tm // CHUNK

    def body(qf_ref, vh_ref, lg_ref, gw_ref, y_in, y_ref, o_ref, st_ref, state):
        del y_in
        i = pl.program_id(0)

        @pl.when(i == 0)
        def _():
            state[...] = jnp.zeros_like(state)

        l0 = lg_ref[0:1, :]
        l1 = lg_ref[1:2, :]
        lb = 1.0 / (1.0 + jnp.exp(l1 - l0))
        gwv = gw_ref[...]
        causal = _rows((CHUNK, CHUNK)) >= _cols((CHUNK, CHUNK))

        for c in range(nc):
            rows = pl.ds(c * CHUNK, CHUNK)
            v = _seg(vh_ref, 0, rows)
            hg = _seg(vh_ref, 1, rows)
            _, _, _, _, _, qm, km, ebm, ecm, ebc = _hgrn_prep(_seg(qf_ref, 0, rows), _seg(qf_ref, 1, rows), lb)
            ke = _mx(km.astype(F32) * ecm)
            gate = gwv * (hg * _sigmoid(hg))
            o_parts, y_parts = [], []
            for h in range(HEADS):
                ln = slice(h * HEAD_D, (h + 1) * HEAD_D)
                s0 = state[h]
                st_ref[c, h] = s0
                vt = _mx(v[:, ln].T)
                att = _mx(jnp.where(causal, _dot_nt(qm[:, ln], km[:, ln]), 0.0))
                o = _dot_nt(_pad_lanes([qm[:, ln], att], 2 * HEAD_D),
                            _pad_lanes([_mx(s0 * ebm[:, ln]), vt], 2 * HEAD_D))
                state[h] = s0 * ebc[:, ln] + jnp.dot(vt, ke[:, ln], preferred_element_type=F32)
                rstd = lax.rsqrt(jnp.mean(o * o, axis=-1, keepdims=True) + EPS)
                o_parts.append(o)
                y_parts.append((o * rstd * gate[:, ln]).astype(_MXU_DTYPE))
            o_ref[rows, :] = jnp.concatenate(o_parts, axis=1)
            y_ref[rows, :] = jnp.concatenate(y_parts, axis=1)

    pair = lambda s: pl.BlockSpec((4, tm, UNIT), lambda i: (s, i, 0))
    return pl.pallas_call(
        body, name="hgrn_fwd", grid=(t // tm,),
        in_specs=[pair(1), pair(2), _full((2, D_MODEL)), _full((1, D_MODEL)),
                  pl.BlockSpec(memory_space=pl.ANY)],
        out_specs=[pl.BlockSpec((None, tm, D_MODEL), lambda i: (1, i, 0)),
                   pl.BlockSpec((tm, D_MODEL), lambda i: (i, 0)),
                   pl.BlockSpec((nc, HEADS, HEAD_D, HEAD_D), lambda i: (i, 0, 0, 0))],
        out_shape=[jax.ShapeDtypeStruct((2, t, D_MODEL), _MXU_DTYPE),
                   jax.ShapeDtypeStruct((t, D_MODEL), F32),
                   jax.ShapeDtypeStruct((t // CHUNK, HEADS, HEAD_D, HEAD_D), F32)],
        scratch_shapes=[pltpu.VMEM((HEADS, HEAD_D, HEAD_D), F32)],
        input_output_aliases={4: 0},
        compiler_params=_seq_params(),
    )(p, p, logits, gw, y2)


def _out_fused(y2, wo, x, tgt, wpost, tm):
    t = x.shape[0]
    inv_d = 1.0 / D_MODEL

    def body(y_ref, wo_ref, x_ref, t_ref, w_ref, dout_ref, dy_ref, sq_ref, dw_ref):
        i = pl.program_id(0)

        @pl.when(i == 0)
        def _():
            sq_ref[...] = jnp.zeros_like(sq_ref)
            dw_ref[...] = jnp.zeros_like(dw_ref)

        y = (jnp.dot(y_ref[0], wo_ref[0:D_MODEL, :], preferred_element_type=F32)
             + jnp.dot(y_ref[1], wo_ref[D_MODEL:2 * D_MODEL, :], preferred_element_type=F32))
        w = w_ref[...]
        rstd = lax.rsqrt(jnp.mean(y * y, axis=-1, keepdims=True) + EPS)
        n = y * rstd
        err = (x_ref[...] + n * w) - t_ref[...]
        sq_ref[...] += jnp.sum(err * err, axis=0, keepdims=True)
        dout = err * inv_d
        dout_ref[...] = dout
        dw_ref[...] += jnp.sum(dout * n, axis=0, keepdims=True)
        dn = dout * w
        dy_ref[...] = (rstd * (dn - n * jnp.mean(dn * n, axis=-1, keepdims=True))).astype(_MXU_DTYPE)

    row = pl.BlockSpec((tm, D_MODEL), lambda i: (i, 0))
    return pl.pallas_call(
        body, name="out_fused", grid=(t // tm,),
        in_specs=[pl.BlockSpec((2, tm, D_MODEL), lambda i: (0, i, 0)), _full((2 * D_MODEL, D_MODEL)),
                  row, row, _full((1, D_MODEL))],
        out_specs=[row, row, _full((1, D_MODEL)), _full((1, D_MODEL))],
        out_shape=[jax.ShapeDtypeStruct((t, D_MODEL), F32), jax.ShapeDtypeStruct((t, D_MODEL), _MXU_DTYPE),
                   jax.ShapeDtypeStruct((1, D_MODEL), F32), jax.ShapeDtypeStruct((1, D_MODEL), F32)],
        compiler_params=_seq_params(),
    )(y2, wo, x, tgt, wpost)


def _dw_out(y2, dy, tt):
    t = dy.shape[0]
    hr = UNIT // 2

    def body(y_ref, dy_ref, g_ref):
        @pl.when(pl.program_id(1) == 0)
        def _():
            g_ref[...] = jnp.zeros_like(g_ref)

        r = _dot_tn(y_ref[...], dy_ref[...])
        g_ref[0] += r[0:hr]
        g_ref[1] += r[hr:UNIT]

    return pl.pallas_call(
        body, name="dw_out", grid=(N_CHIPS, t // tt),
        in_specs=[pl.BlockSpec((None, tt, UNIT), lambda c, k: (c // 2, k, c % 2)),
                  pl.BlockSpec((tt, D_MODEL), lambda c, k: (k, 0))],
        out_specs=pl.BlockSpec((2, None, hr, D_MODEL), lambda c, k: (0, c, 0, 0)),
        out_shape=jax.ShapeDtypeStruct((2, N_CHIPS, hr, D_MODEL), F32),
        compiler_params=pltpu.CompilerParams(dimension_semantics=("arbitrary", "arbitrary")),
    )(y2, dy)


def _lru_bwd(dy, wo, p, h, wa, wx, ba, bx, lam, cw, cb, tm):
    t = dy.shape[0]
    nt = t // tm
    ng = tm // 8

    def body(dy_ref, wo_ref, lx_ref, halo_ref, lg_ref, h_ref, hhalo_ref, wa_ref, wx_ref, ba_ref, bx_ref,
             lam_ref, cw_ref, cb_ref,
             dp_ref, dwa_ref, dwx_ref, dba_ref, dbx_ref, dsp_ref, dcw_ref, dcb_ref,
             c_s, in_s, dh_s, dhc_s, afirst_s, dxc_s):
        i = pl.program_id(0)
        first = i == nt - 1

        @pl.when(i == 0)
        def _():
            for ref in (dwa_ref, dwx_ref, dba_ref, dbx_ref, dsp_ref, dcw_ref, dcb_ref, dhc_s, afirst_s, dxc_s):
                ref[...] = jnp.zeros_like(ref)

        sp = _softplus_neg(lam_ref[...])
        cwv = cw_ref[...]
        wav = wa_ref[...]
        wxv = wx_ref[...]
        halo = jnp.where(first, 0.0, _seg(halo_ref, 0))
        xs, xc, r, ig, a, om = _lru_gates(_seg(lx_ref, 0), halo, cwv, cb_ref[...], wav, wxv,
                                          ba_ref[...], bx_ref[...], sp)
        inv_mult = lax.rsqrt(om)
        mult = om * inv_mult
        d_y = _dot_nt(dy_ref[...], wo_ref[...])
        lg = _seg(lg_ref, 0)
        sl = _sigmoid(lg)
        hv = h_ref[...]
        d_lg = d_y * hv * (sl * (1.0 + lg * (1.0 - sl)))

        c_s[...] = _shift_up(a, afirst_s[...], 1)
        in_s[...] = d_y * (lg * sl)
        row8 = _rows((8, D_MODEL))

        def group(gg, carry):
            rows = pl.ds(pl.multiple_of((ng - 1 - gg) * 8, 8), 8)
            cv = c_s[rows, :]
            uv = in_s[rows, :]
            for s in (1, 2, 4):
                c_sh = jnp.where(row8 < 8 - s, pltpu.roll(cv, 8 - s, 0), 1.0)
                u_sh = jnp.where(row8 < 8 - s, pltpu.roll(uv, 8 - s, 0), 0.0)
                uv = cv * u_sh + uv
                cv = cv * c_sh
            hh = cv * carry + uv
            dh_s[rows, :] = hh
            return jnp.broadcast_to(hh[0:1, :], (8, D_MODEL))

        dhc_s[...] = lax.fori_loop(0, ng, group, dhc_s[...])
        afirst_s[...] = a[0:8]
        dh = dh_s[...]

        h_prev = _shift_down(hv, jnp.where(first, 0.0, hhalo_ref[...]), 1)
        gx = ig * xc
        d_mult = dh * gx
        d_i = dh * (mult * xc)
        d_xc = dh * (mult * ig)
        d_la = (dh * h_prev) * a - d_mult * ((a * a) * inv_mult)
        dsp_ref[...] += jnp.sum(d_la * r, axis=0, keepdims=True) * (-LRU_C)
        d_zr = (d_la * (-LRU_C * sp)) * (r * (1.0 - r))
        d_zi = d_i * (ig * (1.0 - ig))
        dba_ref[...] += jnp.sum(d_zr, axis=0, keepdims=True)
        dbx_ref[...] += jnp.sum(d_zi, axis=0, keepdims=True)
        back = []
        for n in range(LRU_BLOCKS):
            ln = slice(n * LRU_BW, (n + 1) * LRU_BW)
            xb = _mx(xc[:, ln])
            zr_n = _mx(d_zr[:, ln])
            zi_n = _mx(d_zi[:, ln])
            dwa_ref[n] += _dot_tn(xb, zr_n)
            dwx_ref[n] += _dot_tn(xb, zi_n)
            back.append(_dot_nt(zr_n, wav[n]) + _dot_nt(zi_n, wxv[n]))
        d_xc = d_xc + jnp.concatenate(back, axis=1)
        dcb_ref[...] += jnp.sum(d_xc, axis=0, keepdims=True)
        for k in range(4):
            dcw_ref[k:k + 1, :] += jnp.sum(d_xc * xs[k], axis=0, keepdims=True)
        head = dxc_s[...]
        d_lx = cwv[3:4] * d_xc
        for k in range(3):
            d_lx = d_lx + cwv[k:k + 1] * _shift_up(d_xc, head, 3 - k)
        dxc_s[...] = d_xc[0:8]
        dp_ref[0] = d_lx.astype(_MXU_DTYPE)
        dp_ref[1] = d_lg.astype(_MXU_DTYPE)

    rev = lambda i: nt - 1 - i
    seg = lambda s: pl.BlockSpec((2, tm, UNIT), lambda i: (s, rev(i), 0))
    halo_row = lambda i: jnp.maximum(rev(i) * (tm // 8) - 1, 0)
    row = pl.BlockSpec((tm, D_MODEL), lambda i: (rev(i), 0))
    vec = _full((1, D_MODEL))
    wblk = _full((LRU_BLOCKS, LRU_BW, LRU_BW))
    return pl.pallas_call(
        body, name="lru_bwd", grid=(nt,),
        in_specs=[row, pl.BlockSpec((D_MODEL, D_MODEL), lambda i: (0, 0)),
                  seg(0), pl.BlockSpec((2, 8, UNIT), lambda i: (0, halo_row(i), 0)), seg(1),
                  row, pl.BlockSpec((8, D_MODEL), lambda i: (halo_row(i), 0)),
                  wblk, wblk, vec, vec, vec, _full((4, D_MODEL)), vec],
        out_specs=[pl.BlockSpec((2, tm, D_MODEL), lambda i: (2, rev(i), 0)),
                   wblk, wblk, vec, vec, vec, _full((4, D_MODEL)), vec],
        out_shape=[jax.ShapeDtypeStruct((N_SEG, t, D_MODEL), _MXU_DTYPE),
                   jax.ShapeDtypeStruct((LRU_BLOCKS, LRU_BW, LRU_BW), F32),
                   jax.ShapeDtypeStruct((LRU_BLOCKS, LRU_BW, LRU_BW), F32),
                   jax.ShapeDtypeStruct((1, D_MODEL), F32), jax.ShapeDtypeStruct((1, D_MODEL), F32),
                   jax.ShapeDtypeStruct((1, D_MODEL), F32), jax.ShapeDtypeStruct((4, D_MODEL), F32),
                   jax.ShapeDtypeStruct((1, D_MODEL), F32)],
        scratch_shapes=[pltpu.VMEM((tm, D_MODEL), F32), pltpu.VMEM((tm, D_MODEL), F32),
                        pltpu.VMEM((tm, D_MODEL), F32), pltpu.VMEM((8, D_MODEL), F32),
                        pltpu.VMEM((8, D_MODEL), F32), pltpu.VMEM((8, D_MODEL), F32)],
        compiler_params=_seq_params(),
    )(dy, wo, p, p, p, h, h, wa, wx, ba, bx, lam, cw, cb)


def _hgrn_bwd(dy, wo, p, o, states, logits, gw, dp, tm):
    t = dy.shape[0]
    nt = t // tm
    nc = tm // CHUNK

    def body(dy_ref, wo_ref, qf_ref, vh_ref, o_ref, st_ref, lg_ref, gw_ref, dp_in,
             dp_ref, dgw_ref, dlb_ref, dyh_s, dstate, snext):
        del dp_in
        i = pl.program_id(0)

        @pl.when(i == 0)
        def _():
            for ref in (dgw_ref, dlb_ref, dstate, snext):
                ref[...] = jnp.zeros_like(ref)

        dyh_s[...] = _dot_nt(dy_ref[...], wo_ref[...])
        l0 = lg_ref[0:1, :]
        l1 = lg_ref[1:2, :]
        lb = 1.0 / (1.0 + jnp.exp(l1 - l0))
        gwv = gw_ref[...]
        causal = _rows((CHUNK, CHUNK)) >= _cols((CHUNK, CHUNK))
        wide = 2 * HEAD_D
        zeros_h = jnp.zeros((CHUNK, HEAD_D), _MXU_DTYPE)

        for c in reversed(range(nc)):
            rows = pl.ds(c * CHUNK, CHUNK)
            q = _seg(qf_ref, 0, rows)
            hg = _seg(vh_ref, 1, rows)
            ov = o_ref[rows, :]
            dyh = dyh_s[rows, :]
            sig, f, sq, em1, em2, qm, km, ebm, ecm, ebc = _hgrn_prep(q, _seg(qf_ref, 1, rows), lb)
            ke = _mx(km.astype(F32) * ecm)
            vb = _mx(_seg(vh_ref, 0, rows))
            sg = _sigmoid(hg)
            d_on = dyh * (hg * sg)
            x_p, y_p, dv_p, n_p, ex_p = [], [], [], [], []
            for h in range(HEADS):
                ln = slice(h * HEAD_D, (h + 1) * HEAD_D)
                o_h = ov[:, ln]
                rstd = lax.rsqrt(jnp.mean(o_h * o_h, axis=-1, keepdims=True) + EPS)
                n = o_h * rstd
                dn = d_on[:, ln] * gwv[:, ln]
                d_o = rstd * (dn - n * jnp.mean(dn * n, axis=-1, keepdims=True))
                d_ob = _mx(d_o)
                d_ot = _mx(d_o.T)
                s0 = st_ref[c, h]
                ds = dstate[h]
                both = _dot_nt(jnp.concatenate([_pad_lanes([qm[:, ln]], wide), _pad_lanes([zeros_h, d_ob], wide)], axis=0),
                               jnp.concatenate([km[:, ln], vb[:, ln]], axis=1))
                att = jnp.where(causal, both[0:CHUNK], 0.0)
                d_att = jnp.where(causal, both[CHUNK:2 * CHUNK], 0.0)
                x_p.append(jnp.dot(_pad_lanes([d_ob, _mx(d_att)], wide),
                                   _pad_rows([_mx(s0 * ebm[:, ln]), km[:, ln]], wide), preferred_element_type=F32))
                ds_c = ds * ecm[:, ln]
                y_p.append(jnp.dot(_pad_lanes([vb[:, ln], _mx(d_att.T)], wide),
                                   _pad_rows([_mx(ds_c), qm[:, ln]], wide), preferred_element_type=F32))
                dv_p.append(_dot_nt(_pad_lanes([ke[:, ln], _mx(att.T)], wide), _pad_lanes([_mx(ds), d_ot], wide)))
                ex_p.append(jnp.sum(ds * snext[h], axis=0, keepdims=True))
                dstate[h] = (ds_c + jnp.dot(d_ot, qm[:, ln], preferred_element_type=F32)) * ebm[:, ln]
                snext[h] = s0
                n_p.append(n)
            cat = lambda parts: jnp.concatenate(parts, axis=1)
            x_all, y_all, n_all = cat(x_p), cat(y_p), cat(n_p)
            dg = _cumsum_rows(qm.astype(F32) * x_all - km.astype(F32) * y_all, reverse=True) + cat(ex_p)
            df = dg / f - em2 * y_all
            dgw_ref[...] += jnp.sum(d_on * n_all, axis=0, keepdims=True)
            dlb_ref[...] += jnp.sum(df * (1.0 - sig), axis=0, keepdims=True)
            dp_ref[0, rows, :] = ((em1 * x_all) * (sq * (1.0 + q * (1.0 - sq)))).astype(_MXU_DTYPE)
            dp_ref[1, rows, :] = (df * ((1.0 - lb) * sig * (1.0 - sig))).astype(_MXU_DTYPE)
            dp_ref[2, rows, :] = cat(dv_p).astype(_MXU_DTYPE)
            dp_ref[3, rows, :] = (dyh * (n_all * gwv) * (sg * (1.0 + hg * (1.0 - sg)))).astype(_MXU_DTYPE)

    rev = lambda i: nt - 1 - i
    pair = lambda s: pl.BlockSpec((4, tm, UNIT), lambda i: (s, rev(i), 0))
    row = pl.BlockSpec((tm, D_MODEL), lambda i: (rev(i), 0))
    vec = _full((1, D_MODEL))
    return pl.pallas_call(
        body, name="hgrn_bwd", grid=(nt,),
        in_specs=[row, pl.BlockSpec((D_MODEL, D_MODEL), lambda i: (1, 0)), pair(1), pair(2), row,
                  pl.BlockSpec((nc, HEADS, HEAD_D, HEAD_D), lambda i: (rev(i), 0, 0, 0)),
                  _full((2, D_MODEL)), vec, pl.BlockSpec(memory_space=pl.ANY)],
        out_specs=[pl.BlockSpec((4, tm, D_MODEL), lambda i: (0, rev(i), 0)), vec, vec],
        out_shape=[jax.ShapeDtypeStruct((N_SEG, t, D_MODEL), _MXU_DTYPE),
                   jax.ShapeDtypeStruct((1, D_MODEL), F32), jax.ShapeDtypeStruct((1, D_MODEL), F32)],
        scratch_shapes=[pltpu.VMEM((tm, D_MODEL), F32), pltpu.VMEM((HEADS, HEAD_D, HEAD_D), F32),
                        pltpu.VMEM((HEADS, HEAD_D, HEAD_D), F32)],
        input_output_aliases={8: 0},
        compiler_params=_seq_params(),
    )(dy, wo, p, p, o, states, logits, gw, dp)


def _dp_unit(m):
    return ((m // 2 + 2) % N_SEG) * 2 + m % 2


def _dw_in(ut, dp, tt):
    t = ut.shape[1]
    hr = D_MODEL // 2

    def body(ut_ref, dp_ref, g_ref):
        @pl.when(pl.program_id(1) == 0)
        def _():
            g_ref[...] = jnp.zeros_like(g_ref)

        r = jnp.dot(ut_ref[...], dp_ref[...], preferred_element_type=F32)
        for h in range(2):
            for half in range(2):
                g_ref[h, half] += r[h * hr:(h + 1) * hr, half * UNIT:(half + 1) * UNIT]

    return pl.pallas_call(
        body, name="dw_in", grid=(N_SEG, t // tt),
        in_specs=[pl.BlockSpec((D_MODEL, tt), lambda s, k: (0, k)),
                  pl.BlockSpec((None, tt, D_MODEL), lambda s, k: (s, k, 0))],
        out_specs=pl.BlockSpec((2, 2, hr, UNIT), lambda s, k: (0, (s + 2) % N_SEG, 0, 0)),
        out_shape=jax.ShapeDtypeStruct((2, N_UNIT, hr, UNIT), F32),
        compiler_params=pltpu.CompilerParams(dimension_semantics=("arbitrary", "arbitrary")),
    )(ut, dp)


def _dx(dp, wg, x, dout, wpre, tm, parts, scatter):
    t = x.shape[0]
    n = len(parts)
    nt = t // tm

    def body(*refs):
        dp_ref, w_ref, x_ref, dout_ref, wpre_ref = refs[:5]
        ins = refs[5:5 + n]
        gx_ref, dw_ref = refs[5 + n:7 + n]
        outs = refs[7 + n:7 + 2 * n]
        send_sems, recv_sems = refs[7 + 2 * n:]
        i = pl.program_id(0)

        @pl.when(i == 0)
        def _():
            dw_ref[...] = jnp.zeros_like(dw_ref)
            for cp in _chip_copies(ins, outs, send_sems, recv_sems, scatter):
                cp.start()

        du = None
        for m in range(N_UNIT):
            part = _dot_nt(dp_ref[m // 2, :, (m % 2) * UNIT:(m % 2 + 1) * UNIT], w_ref[_dp_unit(m)])
            du = part if du is None else du + part
        xv = x_ref[...]
        rstd = lax.rsqrt(jnp.mean(xv * xv, axis=-1, keepdims=True) + EPS)
        xn = xv * rstd
        dw_ref[...] += jnp.sum(du * xn, axis=0, keepdims=True)
        dn = du * wpre_ref[...]
        gx_ref[...] = dout_ref[...] + rstd * (dn - xn * jnp.mean(dn * xn, axis=-1, keepdims=True))

        @pl.when(i == nt - 1)
        def _():
            copies = _chip_copies(ins, outs, send_sems, recv_sems, scatter)
            for cp in copies:
                cp.wait_recv()
            for cp in copies:
                cp.wait_send()

    row = pl.BlockSpec((tm, D_MODEL), lambda i: (i, 0))
    any_spec = pl.BlockSpec(memory_space=pl.ANY)
    res = pl.pallas_call(
        body, name="dx", grid=(nt,),
        in_specs=[pl.BlockSpec((N_SEG, tm, D_MODEL), lambda i: (0, i, 0)), _resident((N_UNIT, D_MODEL, UNIT)),
                  row, row, _full((1, D_MODEL))] + [any_spec] * n,
        out_specs=[row, _full((1, D_MODEL))] + [any_spec] * n,
        out_shape=[jax.ShapeDtypeStruct((t, D_MODEL), F32), jax.ShapeDtypeStruct((1, D_MODEL), F32)]
        + _chip_exchange_shapes(parts, scatter),
        scratch_shapes=[pltpu.SemaphoreType.DMA((max(3 * n, 1),)), pltpu.SemaphoreType.DMA((max(3 * n, 1),))],
        compiler_params=_seq_params(),
    )(dp, wg, x, dout, wpre, *parts)
    return res[0], res[1], res[2:]


CHUNK_BYTES = 256 * 1024
MAX_CHUNKS = 16


def _n_chunks(shape, dtype, rows):
    nbytes = jnp.dtype(dtype).itemsize
    for d in shape:
        nbytes *= d
    k = max(1, min(MAX_CHUNKS, rows, nbytes // CHUNK_BYTES))
    while rows % k:
        k -= 1
    return k


def _chip_exchange_shapes(arrays, scatter):
    return [jax.ShapeDtypeStruct((3,) + tuple(a.shape[1:] if sc else a.shape), a.dtype)
            for a, sc in zip(arrays, scatter)]


def _chip_copies(ins, outs, send_sems, recv_sems, scatter):
    if not ins:
        return []
    x, y, c = lax.axis_index("x"), lax.axis_index("y"), lax.axis_index("c")
    peers = [(1 - x, y), (x, 1 - y), (1 - x, 1 - y)]
    copies = []
    for a in range(len(ins)):
        for j, (px, py) in enumerate(peers):
            copies.append(pltpu.make_async_remote_copy(
                src_ref=ins[a].at[2 * px + py] if scatter[a] else ins[a], dst_ref=outs[a].at[j],
                send_sem=send_sems.at[a * 3 + j], recv_sem=recv_sems.at[a * 3 + j],
                device_id=(px, py, c), device_id_type=MESH))
    return copies


def _core_swap(arrays, name, halved):
    n = len(arrays)
    shapes, n_sems = _core_swap_plan(arrays, halved)

    def body(*refs):
        copies = _core_swap_copies(arrays, halved, refs[:n], refs[n:2 * n], refs[2 * n], refs[2 * n + 1])
        for cp in copies:
            cp.start()
        for cp in copies:
            cp.wait_recv()
        for cp in copies:
            cp.wait_send()

    any_spec = pl.BlockSpec(memory_space=pl.ANY)
    return pl.pallas_call(
        body, name=name,
        in_specs=[any_spec] * n, out_specs=[any_spec] * n,
        out_shape=shapes,
        scratch_shapes=[pltpu.SemaphoreType.DMA((n_sems,)), pltpu.SemaphoreType.DMA((n_sems,))],
    )(*arrays)


def _core_swap_plan(arrays, halved):
    shapes = [a.shape[1:] if halved else a.shape for a in arrays]
    n_sems = sum(_n_chunks(s, a.dtype, s[0]) for s, a in zip(shapes, arrays))
    return [jax.ShapeDtypeStruct(s, a.dtype) for s, a in zip(shapes, arrays)], max(n_sems, 1)


def _core_swap_copies(arrays, halved, ins, outs, send_sems, recv_sems):
    if not arrays:
        return []
    x, y, c = lax.axis_index("x"), lax.axis_index("y"), lax.axis_index("c")
    copies, sem = [], 0
    for a, arr in enumerate(arrays):
        shape = arr.shape[1:] if halved else arr.shape
        k = _n_chunks(shape, arr.dtype, shape[0])
        step = shape[0] // k
        for j in range(k):
            rows = pl.ds(j * step, step)
            copies.append(pltpu.make_async_remote_copy(
                src_ref=ins[a].at[1 - c, rows] if halved else ins[a].at[rows], dst_ref=outs[a].at[rows],
                send_sem=send_sems.at[sem], recv_sem=recv_sems.at[sem],
                device_id=(x, y, 1 - c), device_id_type=MESH))
            sem += 1
    return copies


def _finish_exchange(arrays, small):
    n = len(arrays)
    ks = [_n_chunks(a.shape, a.dtype, a.shape[0]) for a in arrays]
    offs = [sum(ks[:a]) for a in range(n)]
    total = sum(ks)

    def body(*refs):
        ins, small_ref = refs[:n], refs[n]
        outs, gathered = refs[n + 1:2 * n + 1], refs[2 * n + 1]
        send_sems, recv_sems = refs[2 * n + 2:]
        x, y, c = lax.axis_index("x"), lax.axis_index("y"), lax.axis_index("c")
        copies = []
        for a in range(n):
            step = arrays[a].shape[0] // ks[a]
            for j in range(ks[a]):
                rows = pl.ds(j * step, step)
                copies.append(pltpu.make_async_remote_copy(
                    src_ref=ins[a].at[rows], dst_ref=outs[a].at[rows],
                    send_sem=send_sems.at[offs[a] + j], recv_sem=recv_sems.at[offs[a] + j],
                    device_id=(x, y, 1 - c), device_id_type=MESH))
        for slot in range(7):
            dx, dy, dc = (slot + 1) % 2, ((slot + 1) // 2) % 2, (slot + 1) // 4
            copies.append(pltpu.make_async_remote_copy(
                src_ref=small_ref, dst_ref=gathered.at[slot],
                send_sem=send_sems.at[total + slot], recv_sem=recv_sems.at[total + slot],
                device_id=(x + dx - 2 * x * dx, y + dy - 2 * y * dy, c + dc - 2 * c * dc), device_id_type=MESH))
        for cp in copies:
            cp.start()
        for cp in copies:
            cp.wait_recv()
        for cp in copies:
            cp.wait_send()

    any_spec = pl.BlockSpec(memory_space=pl.ANY)
    res = pl.pallas_call(
        body, name="reduce_cores",
        in_specs=[any_spec] * (n + 1), out_specs=[any_spec] * (n + 1),
        out_shape=[jax.ShapeDtypeStruct(a.shape, a.dtype) for a in arrays]
        + [jax.ShapeDtypeStruct((7,) + small.shape, small.dtype)],
        scratch_shapes=[pltpu.SemaphoreType.DMA((total + 7,)), pltpu.SemaphoreType.DMA((total + 7,))],
    )(*arrays, small)
    return res[:n], res[n]


def _sum_own_half(g, b, name, out_dtype, tr=2048):
    _, rows, cols = g.shape
    tr = min(tr, rows)
    core = lax.axis_index("c").reshape(1)

    def body(c_ref, g_ref, b_ref, o_ref):
        del c_ref
        o_ref[...] = (g_ref[...] + b_ref[...]).astype(out_dtype)

    return pl.pallas_call(
        body, name=name,
        grid_spec=pltpu.PrefetchScalarGridSpec(
            num_scalar_prefetch=1, grid=(rows // tr,),
            in_specs=[pl.BlockSpec((None, tr, cols), lambda i, c_ref: (c_ref[0], i, 0)),
                      pl.BlockSpec((tr, cols), lambda i, c_ref: (i, 0))],
            out_specs=pl.BlockSpec((tr, cols), lambda i, c_ref: (i, 0))),
        out_shape=jax.ShapeDtypeStruct((rows, cols), out_dtype),
        compiler_params=pltpu.CompilerParams(dimension_semantics=("parallel",)),
    )(core, g, b)


def _sum_chips(own, r, slabbed, name, tr=2048):
    _, rows, cols = r.shape
    tr = min(tr, rows)
    chip = (2 * lax.axis_index("x") + lax.axis_index("y")).reshape(1)

    def body(c_ref, own_ref, r_ref, o_ref):
        del c_ref
        f = lambda val: val.astype(F32)
        o_ref[...] = (f(own_ref[...]) + f(r_ref[0])) + (f(r_ref[1]) + f(r_ref[2]))

    if slabbed:
        own_spec = pl.BlockSpec((None, tr, cols), lambda i, c_ref: (c_ref[0], i, 0))
    else:
        own_spec = pl.BlockSpec((tr, cols), lambda i, c_ref: (i, 0))
    return pl.pallas_call(
        body, name=name,
        grid_spec=pltpu.PrefetchScalarGridSpec(
            num_scalar_prefetch=1, grid=(rows // tr,),
            in_specs=[own_spec, pl.BlockSpec((3, tr, cols), lambda i, c_ref: (0, i, 0))],
            out_specs=pl.BlockSpec((tr, cols), lambda i, c_ref: (i, 0))),
        out_shape=jax.ShapeDtypeStruct((rows, cols), F32),
        compiler_params=pltpu.CompilerParams(dimension_semantics=("parallel",)),
    )(chip, own, r)


def _adamw_big(own, other, w, m, v, name, grid, g_block, g_index, w_block, w_index):
    core = lax.axis_index("c").reshape(1)
    half_axis = len(grid) - 1

    def body(c_ref, own_ref, oth_ref, w_ref, m_ref, v_ref, g_ref, d_ref, nm_ref, nv_ref):
        mine = pl.program_id(half_axis) == c_ref[0]
        gv = jnp.where(mine, own_ref[...], oth_ref[...])
        g_ref[...] = gv
        d_ref[...], nm_ref[...], nv_ref[...] = _adamw(w_ref[...], gv, m_ref[...], v_ref[...])

    g_spec = pl.BlockSpec(g_block, lambda *a: g_index(*a[:half_axis]))
    w_spec = pl.BlockSpec(w_block, lambda *a: w_index(*a[:-1]))
    shp = jax.ShapeDtypeStruct(w.shape, F32)
    return pl.pallas_call(
        body, name=name,
        grid_spec=pltpu.PrefetchScalarGridSpec(
            num_scalar_prefetch=1, grid=grid,
            in_specs=[g_spec, g_spec, w_spec, w_spec, w_spec], out_specs=[w_spec] * 4),
        out_shape=[shp] * 4,
        compiler_params=pltpu.CompilerParams(dimension_semantics=("parallel",) * len(grid)),
    )(core, own, other, w, m, v)


def _small_finish(own, others, lam, logits):
    def body(a_ref, b_ref, lam_ref, lg_ref, o_ref):
        g = (((a_ref[...] + b_ref[0]) + (b_ref[1] + b_ref[2]))
             + ((b_ref[3] + b_ref[4]) + (b_ref[5] + b_ref[6])))
        o_ref[...] = g
        o_ref[8:9, :] = g[8:9, :] * (-_sigmoid(-lam_ref[...]))
        lb = 1.0 / (1.0 + jnp.exp(lg_ref[1:2, :] - lg_ref[0:1, :]))
        d0 = g[9:10, :] * (lb * (1.0 - lb))
        o_ref[9:10, :] = d0
        o_ref[10:11, :] = -d0

    return pl.pallas_call(
        body, name="small_finish",
        in_specs=[_full((SMALL_ROWS, D_MODEL)), _full((7, SMALL_ROWS, D_MODEL)), _full((1, D_MODEL)),
                  _full((2, D_MODEL))],
        out_specs=_full((SMALL_ROWS, D_MODEL)),
        out_shape=jax.ShapeDtypeStruct((SMALL_ROWS, D_MODEL), F32),
    )(own, others, lam, logits)


def _adamw_small(ws, gs, ms, vs):
    n = len(ws)

    def body(*refs):
        w_r, g_r, m_r, v_r = refs[:n], refs[n:2 * n], refs[2 * n:3 * n], refs[3 * n:4 * n]
        d_o, m_o, v_o = refs[4 * n:5 * n], refs[5 * n:6 * n], refs[6 * n:7 * n]
        for j in range(n):
            d_o[j][...], m_o[j][...], v_o[j][...] = _adamw(w_r[j][...], g_r[j][...], m_r[j][...], v_r[j][...])

    specs = [_full(w.shape) for w in ws]
    shapes = [jax.ShapeDtypeStruct(w.shape, F32) for w in ws]
    outs = pl.pallas_call(
        body, name="adamw_small",
        in_specs=specs * 4, out_specs=specs * 3, out_shape=shapes * 3,
    )(*ws, *gs, *ms, *vs)
    return outs[:n], outs[n:2 * n], outs[2 * n:]


def _local_step(x, tgt, ut, p, pre_w, wg, cw, cb, wa, wx, ba, bx, lam, logits, gw, wo, post_w, tm,
                prepare=None):
    h, y2 = _lru_fwd(p, wa, wx, ba, bx, lam, cw, cb, tm)
    y2, o, states = _hgrn_fwd(p, logits, gw, y2, tm)
    dout, dy, sq, d_post = _out_fused(y2, wo, x, tgt, post_w, min(2 * tm, x.shape[0]))
    g_out = _dw_out(y2, dy, min(GRAD_K_TILE, x.shape[0]))
    dp, d_wa, d_wx, d_ba, d_bx, d_sp, d_cw, d_cb = _lru_bwd(dy, wo, p, h, wa, wx, ba, bx, lam, cw, cb, tm)
    dp, d_gw, d_lb = _hgrn_bwd(dy, wo, p, o, states, logits, gw, dp, tm)
    g_in = _dw_in(ut, dp, min(GRAD_K_TILE, x.shape[0]))
    parts, scatter = prepare(g_in, g_out, d_wa, d_wx) if prepare is not None else ([], [])
    grad_x, d_pre, received = _dx(dp, wg, x, dout, pre_w, tm, parts, scatter)
    small = jnp.concatenate([d_pre, d_cw, d_cb, d_ba, d_bx, d_sp, d_lb,
                             jnp.zeros((1, D_MODEL), F32), d_gw, d_post, sq,
                             jnp.zeros((SMALL_ROWS - 14, D_MODEL), F32)], axis=0)
    return grad_x, g_in, g_out, d_wa, d_wx, small, parts, received


def kernel(x, pre_norm_w, w_in, conv_w, conv_b, lru_w_a, lru_b_a, lru_w_x, lru_b_x, lru_lambda, hgrn_lb_logits, hgrn_gnorm_w, w_out, post_norm_w, loss_target, m_pre_norm_w, m_w_in, m_conv_w, m_conv_b, m_lru_w_a, m_lru_b_a, m_lru_w_x, m_lru_b_x, m_lru_lambda, m_hgrn_lb_logits, m_hgrn_gnorm_w, m_w_out, m_post_norm_w, v_pre_norm_w, v_w_in, v_conv_w, v_conv_b, v_lru_w_a, v_lru_b_a, v_lru_w_x, v_lru_b_x, v_lru_lambda, v_hgrn_lb_logits, v_hgrn_gnorm_w, v_w_out, v_post_norm_w):
    t = x.shape[1]
    chip = 2 * lax.axis_index("x") + lax.axis_index("y")

    win_units = w_in[0].astype(_MXU_DTYPE).reshape(D_MODEL, 3, UNIT).transpose(1, 0, 2)
    small_w = jnp.concatenate([conv_w[0], lru_b_a[0], lru_b_x[0]], axis=1)
    ut, p, wg4, (wo4, wa4, wx4), sw4 = _in_proj_gather(
        x[0], pre_norm_w, win_units,
        [w_out[0].astype(_MXU_DTYPE).reshape(2, 256, D_MODEL), lru_w_a[0].astype(_MXU_DTYPE).reshape(2, 128, LRU_BW),
         lru_w_x[0].astype(_MXU_DTYPE).reshape(2, 128, LRU_BW)],
        small_w, min(PROJ_TILE, t))
    wg = wg4.reshape(N_UNIT, D_MODEL, UNIT)
    wo = wo4.reshape(2 * D_MODEL, D_MODEL)
    by_block = lambda w4: w4.reshape(N_CHIPS, LRU_BLOCKS, 64, LRU_BW).transpose(1, 0, 2, 3).reshape(
        LRU_BLOCKS, LRU_BW, LRU_BW)
    wa, wx = by_block(wa4), by_block(wx4)
    sw4 = sw4.reshape(N_CHIPS, 4, 384)
    cw = sw4[:, :, 0:256].transpose(1, 0, 2).reshape(4, D_MODEL)
    ba = sw4[:, :, 256:320].transpose(1, 0, 2).reshape(1, D_MODEL)
    bx = sw4[:, :, 320:384].transpose(1, 0, 2).reshape(1, D_MODEL)

    def lru_layout(g):
        g = g.reshape(2, 2, N_CHIPS, 64, LRU_BW).transpose(0, 2, 1, 3, 4)
        return g.reshape(2, N_CHIPS * 128, LRU_BW)

    def prepare(g_in, g_out, d_wa, d_wx):
        g_in = g_in.reshape(2, N_UNIT * 512, UNIT)
        g_out = g_out.reshape(2, N_CHIPS * 256, D_MODEL)
        g_wa, g_wx = lru_layout(d_wa), lru_layout(d_wx)
        b_in, b_out, b_wa, b_wx = _core_swap([g_in, g_out, g_wa, g_wx], "reduce_swap", halved=True)
        parts = [_sum_own_half(g_in, b_in, "presum_in", _WIRE_DTYPE).reshape(N_CHIPS, 3 * 512, UNIT),
                 _sum_own_half(g_out, b_out, "presum_out", _WIRE_DTYPE).reshape(N_CHIPS, 256, D_MODEL),
                 _sum_own_half(g_wa, b_wa, "presum_wa", _WIRE_DTYPE).reshape(N_CHIPS, 128, LRU_BW),
                 _sum_own_half(g_wx, b_wx, "presum_wx", _WIRE_DTYPE).reshape(N_CHIPS, 128, LRU_BW)]
        return parts, [True] * 4

    grad_x, _, _, _, _, small, (p_in, p_out, p_wa, p_wx), (r_in, r_out, r_wa, r_wx) = _local_step(
        x[0], loss_target[0], ut, p, pre_norm_w, wg, cw, conv_b, wa, wx, ba, bx, lru_lambda, hgrn_lb_logits,
        hgrn_gnorm_w, wo, post_norm_w, min(ROW_TILE, t), prepare)
    s_in = _sum_chips(p_in, r_in, True, "sum_in")
    s_out = _sum_chips(p_out, r_out, True, "sum_out")
    s_wa = _sum_chips(p_wa, r_wa, True, "sum_wa")
    s_wx = _sum_chips(p_wx, r_wx, True, "sum_wx")
    (o_in, o_out, o_wa, o_wx), small_others = _finish_exchange([s_in, s_out, s_wa, s_wx], small)

    g_w_in, d_w_in, nm_w_in, nv_w_in = _adamw_big(
        s_in.reshape(3, 512, UNIT), o_in.reshape(3, 512, UNIT), w_in[0], m_w_in[0], v_w_in[0], "adamw_w_in", (3, 2),
        (None, 512, UNIT), lambda k: (k, 0, 0), (512, UNIT), lambda k, h: (h, k))
    g_w_out, d_w_out, nm_w_out, nv_w_out = _adamw_big(
        s_out, o_out, w_out[0], m_w_out[0], v_w_out[0], "adamw_w_out", (2,),
        (256, D_MODEL), lambda: (0, 0), (256, D_MODEL), lambda h: (h, 0))
    sq2 = lambda a: a.reshape(LRU_BW, LRU_BW)
    lru_specs = ((2,), (128, LRU_BW), lambda: (0, 0), (128, LRU_BW), lambda h: (h, 0))
    g_wa, d_wa2, nm_wa, nv_wa = _adamw_big(s_wa, o_wa, sq2(lru_w_a), sq2(m_lru_w_a), sq2(v_lru_w_a), "adamw_wa",
                                           *lru_specs)
    g_wx, d_wx2, nm_wx, nv_wx = _adamw_big(s_wx, o_wx, sq2(lru_w_x), sq2(m_lru_w_x), sq2(v_lru_w_x), "adamw_wx",
                                           *lru_specs)

    gs = _small_finish(small, small_others, lru_lambda, hgrn_lb_logits)
    loss = jnp.sum(gs[13]) * (0.5 / D_MODEL)
    g_pre = gs[0:1]
    g_cw = lax.dynamic_slice(gs[1:5], (0, chip * 256), (4, 256))
    g_cb = gs[5:6]
    g_ba = lax.dynamic_slice(gs[6].reshape(LRU_BLOCKS, N_CHIPS, 64), (0, chip, 0), (LRU_BLOCKS, 1, 64)).reshape(4, 64)
    g_bx = lax.dynamic_slice(gs[7].reshape(LRU_BLOCKS, N_CHIPS, 64), (0, chip, 0), (LRU_BLOCKS, 1, 64)).reshape(4, 64)
    g_lam = gs[8:9]
    g_lb = gs[9:11]
    g_gw = gs[11:12]
    g_post = gs[12:13]
    small_g = [g_pre, g_cw, g_cb, g_ba, g_bx, g_lam, g_lb, g_gw, g_post]
    two_d = lambda a: a.reshape(a.shape[-2:])
    small_w_list = [pre_norm_w, conv_w, conv_b, lru_b_a, lru_b_x, lru_lambda, hgrn_lb_logits, hgrn_gnorm_w, post_norm_w]
    small_m_list = [m_pre_norm_w, m_conv_w, m_conv_b, m_lru_b_a, m_lru_b_x, m_lru_lambda, m_hgrn_lb_logits,
                    m_hgrn_gnorm_w, m_post_norm_w]
    small_v_list = [v_pre_norm_w, v_conv_w, v_conv_b, v_lru_b_a, v_lru_b_x, v_lru_lambda, v_hgrn_lb_logits,
                    v_hgrn_gnorm_w, v_post_norm_w]
    sd, sm, sv = _adamw_small([two_d(a) for a in small_w_list], small_g,
                              [two_d(a) for a in small_m_list], [two_d(a) for a in small_v_list])

    def shaped(vals, refs):
        return [val.reshape(ref.shape) for val, ref in zip(vals, refs)]

    s_g = shaped(small_g, small_w_list)
    s_d = shaped(sd, small_w_list)
    s_m = shaped(sm, small_w_list)
    s_v = shaped(sv, small_w_list)

    def ordered(small, big_in, big_wa, big_wx, big_out):
        pre, cw_, cb_, ba_, bx_, lam_, lb_, gw_, post_ = small
        return [pre, big_in.reshape(w_in.shape), cw_, cb_, big_wa.reshape(lru_w_a.shape), ba_,
                big_wx.reshape(lru_w_x.shape), bx_, lam_, lb_, gw_, big_out.reshape(w_out.shape), post_]

    grads = ordered(s_g, g_w_in, g_wa, g_wx, g_w_out)
    deltas = ordered(s_d, d_w_in, d_wa2, d_wx2, d_w_out)
    new_m = ordered(s_m, nm_w_in, nm_wa, nm_wx, nm_w_out)
    new_v = ordered(s_v, nv_w_in, nv_wa, nv_wx, nv_w_out)
    return (loss, grad_x.reshape(x.shape), *grads, *deltas, *new_m, *new_v)
```

```python
import functools

import jax
import jax.numpy as jnp
from jax import lax
from jax.experimental import pallas as pl
from jax.experimental.pallas import tpu as pltpu

F32 = jnp.float32
_MXU_DTYPE = jnp.bfloat16
_WIRE_DTYPE = jnp.bfloat16

D_MODEL = 1024
N_SEG = 6
UNIT = 512
N_UNIT = 12
LRU_BLOCKS = 4
LRU_BW = 256
LRU_C = 8.0
HEADS = 8
HEAD_D = 128
CHUNK = 64
EPS = 1e-6
N_CHIPS = 4
EXP_CLAMP = 80.0

ADAM_LR = 0.001
ADAM_B1 = 0.9
ADAM_B2 = 0.999
ADAM_EPS = 1e-08
ADAM_WD = 0.01
ADAM_STEP = 10

SMALL_ROWS = 16
FWD_PIECES = 4
ROW_TILE = 512
PROJ_TILE = 1024
GRAD_K_TILE = 4096
MESH = pl.DeviceIdType.MESH


def _sigmoid(x):
    return 0.5 * jnp.tanh(0.5 * x) + 0.5


def _mx(x):
    return x.astype(_MXU_DTYPE)


def _dot_nt(a, b):
    return lax.dot_general(_mx(a), _mx(b), (((1,), (1,)), ((), ())), preferred_element_type=F32)


def _dot_tn(a, b):
    return lax.dot_general(_mx(a), _mx(b), (((0,), (0,)), ((), ())), preferred_element_type=F32)


def _rows(shape):
    return lax.broadcasted_iota(jnp.int32, shape, 0)


def _cols(shape):
    return lax.broadcasted_iota(jnp.int32, shape, 1)


def _softplus_neg(lam):
    z = -lam
    e = jnp.exp(-jnp.abs(z))
    series = e * (1.0 - e * (0.5 - e * (1.0 / 3.0 - 0.25 * e)))
    return jnp.maximum(z, 0.0) + jnp.where(e < 1e-2, series, jnp.log(1.0 + e))


def _one_minus_sq(a, y):
    series = -y * (1.0 + y * (0.5 + y * (1.0 / 6.0)))
    return jnp.where(y > -0.01, series, 1.0 - a * a)


def _cumsum_rows(x, reverse=False):
    n = x.shape[0] // 8
    row8 = _rows((8, x.shape[1]))
    out = [None] * n
    carry = None
    for g in (reversed(range(n)) if reverse else range(n)):
        blk = x[8 * g:8 * g + 8]
        for s in (1, 2, 4):
            if reverse:
                blk = blk + jnp.where(row8 < 8 - s, pltpu.roll(blk, 8 - s, 0), 0.0)
            else:
                blk = blk + jnp.where(row8 >= s, pltpu.roll(blk, s, 0), 0.0)
        if carry is not None:
            blk = blk + carry
        carry = blk[0:1] if reverse else blk[7:8]
        out[g] = blk
    return jnp.concatenate(out, axis=0)


def _shift_down(x, halo, s):
    if s == 0:
        return x
    r = pltpu.roll(x, s, 0)
    top = jnp.where(_rows(halo.shape) < s, pltpu.roll(halo, s, 0), r[0:8])
    return jnp.concatenate([top, r[8:]], axis=0)


def _shift_up(x, head, s):
    if s == 0:
        return x
    n = x.shape[0]
    r = pltpu.roll(x, n - s, 0)
    bottom = jnp.where(_rows(head.shape) >= 8 - s, pltpu.roll(head, 8 - s, 0), r[n - 8:n])
    return jnp.concatenate([r[:n - 8], bottom], axis=0)


def _lru_gates(lx, halo, cw, cb, wa, wx, ba, bx, sp):
    xs = [_shift_down(lx, halo, 3 - k) for k in range(4)]
    xc = cb + cw[0:1] * xs[0] + cw[1:2] * xs[1] + cw[2:3] * xs[2] + cw[3:4] * xs[3]
    zr, zi = [], []
    for n in range(LRU_BLOCKS):
        xb = _mx(xc[:, n * LRU_BW:(n + 1) * LRU_BW])
        zr.append(jnp.dot(xb, wa[n], preferred_element_type=F32))
        zi.append(jnp.dot(xb, wx[n], preferred_element_type=F32))
    r = _sigmoid(jnp.concatenate(zr, axis=1) + ba)
    ig = _sigmoid(jnp.concatenate(zi, axis=1) + bx)
    la = (-LRU_C * sp) * r
    a = jnp.exp(la)
    om = _one_minus_sq(a, 2.0 * la)
    return xs, xc, r, ig, a, om


def _hgrn_prep(q, fr, lb):
    sig = _sigmoid(fr)
    f = lb + (1.0 - lb) * sig
    sq = _sigmoid(q)
    b = _cumsum_rows(jnp.log(f))
    bm = b[CHUNK // 2 - 1:CHUNK // 2]
    bc = b[CHUNK - 1:CHUNK]
    em1 = jnp.exp(jnp.minimum(b - bm, EXP_CLAMP))
    em2 = jnp.exp(jnp.minimum(bm - b, EXP_CLAMP))
    qm = _mx((q * sq) * em1)
    km = _mx((1.0 - f) * em2)
    return sig, f, sq, em1, em2, qm, km, jnp.exp(bm), jnp.exp(bc - bm), jnp.exp(bc)


def _seg(ref, s, rows=slice(None)):
    return jnp.concatenate([ref[2 * s, rows, :], ref[2 * s + 1, rows, :]], axis=1)


def _pad_lanes(parts, width):
    have = sum(p.shape[1] for p in parts)
    pad = [jnp.zeros((parts[0].shape[0], width - have), parts[0].dtype)] if width > have else []
    return jnp.concatenate(list(parts) + pad, axis=1)


def _pad_rows(parts, height):
    have = sum(p.shape[0] for p in parts)
    pad = [jnp.zeros((height - have, parts[0].shape[1]), parts[0].dtype)] if height > have else []
    return jnp.concatenate(list(parts) + pad, axis=0)


def _adamw(w, g, m, v):
    m = ADAM_B1 * m + (1.0 - ADAM_B1) * g
    v = ADAM_B2 * v + (1.0 - ADAM_B2) * jnp.square(g)
    m_hat = m / (1.0 - ADAM_B1 ** ADAM_STEP)
    v_hat = v / (1.0 - ADAM_B2 ** ADAM_STEP)
    delta = -ADAM_LR * (m_hat / (jnp.sqrt(v_hat) + ADAM_EPS) + ADAM_WD * w)
    return delta, m, v


def _full(shape):
    nd = len(shape)
    return pl.BlockSpec(shape, lambda *_: (0,) * nd)


def _seq_params():
    return pltpu.CompilerParams(dimension_semantics=("arbitrary",))


def _resident(shape):
    nd = len(shape)
    return pl.BlockSpec(shape, lambda *_: (0,) * nd, pipeline_mode=pl.Buffered(1))


def _in_proj_gather(x, w, win_units, halved, small_w, tm):
    t = x.shape[0]
    nt = t // tm
    n_h = len(halved)
    n_split = 3 + n_h
    n_items = n_split + 1
    hu = D_MODEL // 2
    sub = min(256, tm)
    chip = 2 * lax.axis_index("x") + lax.axis_index("y")
    order = jnp.stack([chip] + [jnp.bitwise_xor(chip, flip) for flip in (2, 1, 3)])

    def body(order_ref, x_ref, w_ref, win_ref, *rest):
        del order_ref
        h_ins, small_ref = rest[:n_h], rest[n_h]
        ut_ref, p_ref, wg_ref = rest[n_h + 1:n_h + 4]
        h_outs, sw_ref = rest[n_h + 4:2 * n_h + 4], rest[2 * n_h + 4]
        u_s, wbuf, send, recv, fsend, frecv, osend, orecv, wsem = rest[2 * n_h + 5:]
        g, i = pl.program_id(0), pl.program_id(1)
        mx, my, c = lax.axis_index("x"), lax.axis_index("y"), lax.axis_index("c")
        me = 2 * mx + my
        peers = [(1 - mx, my), (mx, 1 - my), (1 - mx, 1 - my)]
        qs = [2 * px + py for px, py in peers]
        sibling = (mx, my, 1 - c)

        def remote(src, dst, ssem, rsem, dev):
            return pltpu.make_async_remote_copy(src_ref=src, dst_ref=dst, send_sem=ssem, recv_sem=rsem,
                                                device_id=dev, device_id_type=MESH)

        def mine(item):
            if item < 3:
                return win_ref.at[item, pl.ds(c * hu, hu)]
            return h_ins[item - 3].at[c] if item < n_split else small_ref

        def piece(item, q, h):
            if item < 3:
                return wg_ref.at[q, item, pl.ds(h * hu, hu)]
            return h_outs[item - 3].at[q, h] if item < n_split else sw_ref.at[q]

        def fetch(item, jj, q):
            px, py = peers[jj]
            return remote(mine(item), piece(item, q, c), send.at[item * 3 + jj], recv.at[item * 3 + jj], (px, py, c))

        def forward(item, jj, h):
            whole = piece(item, qs[jj], h)
            step = whole.shape[0] // FWD_PIECES
            copies = []
            for k in range(FWD_PIECES):
                part = whole.at[pl.ds(k * step, step)]
                sem = (item * 3 + jj) * FWD_PIECES + k
                copies.append(remote(part, part, fsend.at[sem], frecv.at[sem], sibling))
            return copies

        def own(item):
            if item < 3:
                src, dst = win_ref.at[item], wg_ref.at[me, item]
            elif item < n_split:
                src, dst = h_ins[item - 3], h_outs[item - 3].at[me]
            else:
                src, dst = small_ref, sw_ref.at[me]
            return remote(src, dst, osend.at[item], orecv.at[item], sibling)

        def load_units(gg):
            src = lambda k: win_ref.at[k] if gg == 0 else wg_ref.at[qs[gg - 1], k]
            return [pltpu.make_async_copy(src(k), wbuf.at[gg % 2, k], wsem.at[(gg % 2) * 3 + k]) for k in range(3)]

        def in_pass(gg):
            return pl.when(g == gg)

        def pass_on(gg):
            for item in range(3):
                fetch(item, gg, qs[gg]).wait_recv()
                for cp in forward(item, gg, c):
                    cp.start()

        @pl.when(i == 0)
        def _():
            @in_pass(0)
            def _():
                for item in range(3):
                    for jj in range(2):
                        fetch(item, jj, me).start()
                for item in range(n_items):
                    own(item).start()
                for cp in load_units(0):
                    cp.start()

            @in_pass(1)
            def _():
                for item in range(3):
                    fetch(item, 2, me).start()

            @in_pass(2)
            def _():
                for item in range(3, n_items):
                    for jj in range(3):
                        fetch(item, jj, me).start()

            for gg in range(N_CHIPS):
                @in_pass(gg)
                def _(gg=gg):
                    for cp in load_units(gg):
                        cp.wait()

        @pl.when(g == 0)
        def _():
            xv = x_ref[...]
            rstd = lax.rsqrt(jnp.mean(xv * xv, axis=-1, keepdims=True) + EPS)
            u = xv * rstd * w_ref[...]
            ut_ref[...] = u.T.astype(_MXU_DTYPE)
            u_s[pl.ds(pl.multiple_of(i * tm, tm), tm), :] = _mx(u)

        slot = g % 2
        for r in range(tm // sub):
            ur = u_s[pl.ds(pl.multiple_of(i * tm + r * sub, sub), sub), :]
            for k in range(3):
                p_ref[k, r * sub:(r + 1) * sub, :] = jnp.dot(ur, wbuf[slot, k], preferred_element_type=F32)

        @pl.when(i == nt // 2)
        def _():
            for gg in (1, 2):
                in_pass(gg)(functools.partial(pass_on, gg))

            @in_pass(N_CHIPS - 1)
            def _():
                for item in range(3, n_items):
                    for jj in range(3):
                        fetch(item, jj, qs[jj]).wait_recv()
                        if item < n_split:
                            for cp in forward(item, jj, c):
                                cp.start()

        @pl.when(i == max(nt - 2, 0))
        def _():
            in_pass(0)(functools.partial(pass_on, 0))

        @pl.when(i == nt - 1)
        def _():
            for gg in range(3):
                @in_pass(gg)
                def _(gg=gg):
                    for item in range(3):
                        for cp in forward(item, gg, 1 - c):
                            cp.wait_recv()
                    for cp in load_units(gg + 1):
                        cp.start()

            @in_pass(N_CHIPS - 1)
            def _():
                for item in range(3, n_split):
                    for jj in range(3):
                        for cp in forward(item, jj, 1 - c):
                            cp.wait_recv()
                for item in range(n_items):
                    own(item).wait_recv()
                for item in range(n_items):
                    for jj in range(3):
                        fetch(item, jj, me).wait_send()
                        if item < n_split:
                            for cp in forward(item, jj, c):
                                cp.wait_send()
                    own(item).wait_send()

    any_spec = pl.BlockSpec(memory_space=pl.ANY)
    first_pass = lambda g, i: jnp.where(g == 0, i, nt - 1)
    res = pl.pallas_call(
        body, name="in_proj",
        grid_spec=pltpu.PrefetchScalarGridSpec(
            num_scalar_prefetch=1, grid=(N_CHIPS, nt),
            in_specs=[pl.BlockSpec((tm, D_MODEL), lambda g, i, o: (first_pass(g, i), 0)),
                      pl.BlockSpec((1, D_MODEL), lambda g, i, o: (0, 0))] + [any_spec] * (n_h + 2),
            out_specs=[pl.BlockSpec((D_MODEL, tm), lambda g, i, o: (0, first_pass(g, i))),
                       pl.BlockSpec((3, tm, UNIT), lambda g, i, o: (o[g], i, 0))]
            + [any_spec] * (n_h + 2),
            scratch_shapes=[pltpu.VMEM((t, D_MODEL), _MXU_DTYPE), pltpu.VMEM((2, 3, D_MODEL, UNIT), _MXU_DTYPE),
                            pltpu.SemaphoreType.DMA((3 * n_items,)), pltpu.SemaphoreType.DMA((3 * n_items,)),
                            pltpu.SemaphoreType.DMA((3 * n_split * FWD_PIECES,)),
                            pltpu.SemaphoreType.DMA((3 * n_split * FWD_PIECES,)),
                            pltpu.SemaphoreType.DMA((n_items,)), pltpu.SemaphoreType.DMA((n_items,)),
                            pltpu.SemaphoreType.DMA((6,))]),
        out_shape=[jax.ShapeDtypeStruct((D_MODEL, t), _MXU_DTYPE), jax.ShapeDtypeStruct((N_UNIT, t, UNIT), F32),
                   jax.ShapeDtypeStruct((N_CHIPS,) + win_units.shape, win_units.dtype)]
        + [jax.ShapeDtypeStruct((N_CHIPS,) + a.shape, a.dtype) for a in halved]
        + [jax.ShapeDtypeStruct((N_CHIPS,) + small_w.shape, small_w.dtype)],
        compiler_params=pltpu.CompilerParams(dimension_semantics=("arbitrary", "arbitrary")),
    )(order, x, w, win_units, *halved, small_w)
    return res[0], res[1], res[2], res[3:3 + n_h], res[3 + n_h]


def _lru_fwd(p, wa, wx, ba, bx, lam, cw, cb, tm):
    t = p.shape[1]
    ng = tm // 8

    def body(lx_ref, halo_ref, lg_ref, wa_ref, wx_ref, ba_ref, bx_ref, lam_ref, cw_ref, cb_ref,
             h_ref, y_ref, a_s, u_s, hc_s):
        i = pl.program_id(0)

        @pl.when(i == 0)
        def _():
            hc_s[...] = jnp.zeros_like(hc_s)

        sp = _softplus_neg(lam_ref[...])
        halo = jnp.where(i == 0, 0.0, _seg(halo_ref, 0))
        _, xc, _, ig, a, om = _lru_gates(_seg(lx_ref, 0), halo, cw_ref[...], cb_ref[...], wa_ref[...],
                                         wx_ref[...], ba_ref[...], bx_ref[...], sp)
        a_s[...] = a
        u_s[...] = jnp.sqrt(om) * (ig * xc)
        row8 = _rows((8, D_MODEL))

        def group(g, hc):
            rows = pl.ds(pl.multiple_of(g * 8, 8), 8)
            av = a_s[rows, :]
            uv = u_s[rows, :]
            for s in (1, 2, 4):
                a_sh = jnp.where(row8 >= s, pltpu.roll(av, s, 0), 1.0)
                u_sh = jnp.where(row8 >= s, pltpu.roll(uv, s, 0), 0.0)
                uv = av * u_sh + uv
                av = av * a_sh
            hh = av * hc + uv
            h_ref[rows, :] = hh
            return jnp.broadcast_to(hh[7:8, :], (8, D_MODEL))

        hc_s[...] = lax.fori_loop(0, ng, group, hc_s[...])
        lg = _seg(lg_ref, 0)
        y_ref[...] = (h_ref[...] * (lg * _sigmoid(lg))).astype(_MXU_DTYPE)

    seg = lambda s: pl.BlockSpec((2, tm, UNIT), lambda i: (s, i, 0))
    return pl.pallas_call(
        body, name="lru_fwd", grid=(t // tm,),
        in_specs=[seg(0),
                  pl.BlockSpec((2, 8, UNIT), lambda i: (0, jnp.maximum(i * (tm // 8) - 1, 0), 0)),
                  seg(1),
                  _full((LRU_BLOCKS, LRU_BW, LRU_BW)), _full((LRU_BLOCKS, LRU_BW, LRU_BW)),
                  _full((1, D_MODEL)), _full((1, D_MODEL)), _full((1, D_MODEL)),
                  _full((4, D_MODEL)), _full((1, D_MODEL))],
        out_specs=[pl.BlockSpec((tm, D_MODEL), lambda i: (i, 0)),
                   pl.BlockSpec((None, tm, D_MODEL), lambda i: (0, i, 0))],
        out_shape=[jax.ShapeDtypeStruct((t, D_MODEL), F32),
                   jax.ShapeDtypeStruct((2, t, D_MODEL), _MXU_DTYPE)],
        scratch_shapes=[pltpu.VMEM((tm, D_MODEL), F32), pltpu.VMEM((tm, D_MODEL), F32),
                        pltpu.VMEM((8, D_MODEL), F32)],
        compiler_params=_seq_params(),
    )(p, p, p, wa, wx, ba, bx, lam, cw, cb)


def _hgrn_fwd(p, logits, gw, y2, tm):
    t = p.shape[1]
    nc = tm // CHUNK

    def body(qf_ref, vh_ref, lg_ref, gw_ref, y_in, y_ref, o_ref, st_ref, state):
        del y_in
        i = pl.program_id(0)

        @pl.when(i == 0)
        def _():
            state[...] = jnp.zeros_like(state)

        l0 = lg_ref[0:1, :]
        l1 = lg_ref[1:2, :]
        lb = 1.0 / (1.0 + jnp.exp(l1 - l0))
        gwv = gw_ref[...]
        causal = _rows((CHUNK, CHUNK)) >= _cols((CHUNK, CHUNK))

        for c in range(nc):
            rows = pl.ds(c * CHUNK, CHUNK)
            v = _seg(vh_ref, 0, rows)
            hg = _seg(vh_ref, 1, rows)
            _, _, _, _, _, qm, km, ebm, ecm, ebc = _hgrn_prep(_seg(qf_ref, 0, rows), _seg(qf_ref, 1, rows), lb)
            ke = _mx(km.astype(F32) * ecm)
            gate = gwv * (hg * _sigmoid(hg))
            o_parts, y_parts = [], []
            for h in range(HEADS):
                ln = slice(h * HEAD_D, (h + 1) * HEAD_D)
                s0 = state[h]
                st_ref[c, h] = s0
                vt = _mx(v[:, ln].T)
                att = _mx(jnp.where(causal, _dot_nt(qm[:, ln], km[:, ln]), 0.0))
                o = _dot_nt(_pad_lanes([qm[:, ln], att], 2 * HEAD_D),
                            _pad_lanes([_mx(s0 * ebm[:, ln]), vt], 2 * HEAD_D))
                state[h] = s0 * ebc[:, ln] + jnp.dot(vt, ke[:, ln], preferred_element_type=F32)
                rstd = lax.rsqrt(jnp.mean(o * o, axis=-1, keepdims=True) + EPS)
                o_parts.append(o)
                y_parts.append((o * rstd * gate[:, ln]).astype(_MXU_DTYPE))
            o_ref[rows, :] = jnp.concatenate(o_parts, axis=1)
            y_ref[rows, :] = jnp.concatenate(y_parts, axis=1)

    pair = lambda s: pl.BlockSpec((4, tm, UNIT), lambda i: (s, i, 0))
    return pl.pallas_call(
        body, name="hgrn_fwd", grid=(t // tm,),
        in_specs=[pair(1), pair(2), _full((2, D_MODEL)), _full((1, D_MODEL)),
                  pl.BlockSpec(memory_space=pl.ANY)],
        out_specs=[pl.BlockSpec((None, tm, D_MODEL), lambda i: (1, i, 0)),
                   pl.BlockSpec((tm, D_MODEL), lambda i: (i, 0)),
                   pl.BlockSpec((nc, HEADS, HEAD_D, HEAD_D), lambda i: (i, 0, 0, 0))],
        out_shape=[jax.ShapeDtypeStruct((2, t, D_MODEL), _MXU_DTYPE),
                   jax.ShapeDtypeStruct((t, D_MODEL), F32),
                   jax.ShapeDtypeStruct((t // CHUNK, HEADS, HEAD_D, HEAD_D), F32)],
        scratch_shapes=[pltpu.VMEM((HEADS, HEAD_D, HEAD_D), F32)],
        input_output_aliases={4: 0},
        compiler_params=_seq_params(),
    )(p, p, logits, gw, y2)


def _out_fused(y2, wo, x, tgt, wpost, tm):
    t = x.shape[0]
    inv_d = 1.0 / D_MODEL

    def body(y_ref, wo_ref, x_ref, t_ref, w_ref, dout_ref, dy_ref, sq_ref, dw_ref):
        i = pl.program_id(0)

        @pl.when(i == 0)
        def _():
            sq_ref[...] = jnp.zeros_like(sq_ref)
            dw_ref[...] = jnp.zeros_like(dw_ref)

        y = (jnp.dot(y_ref[0], wo_ref[0:D_MODEL, :], preferred_element_type=F32)
             + jnp.dot(y_ref[1], wo_ref[D_MODEL:2 * D_MODEL, :], preferred_element_type=F32))
        w = w_ref[...]
        rstd = lax.rsqrt(jnp.mean(y * y, axis=-1, keepdims=True) + EPS)
        n = y * rstd
        err = (x_ref[...] + n * w) - t_ref[...]
        sq_ref[...] += jnp.sum(err * err, axis=0, keepdims=True)
        dout = err * inv_d
        dout_ref[...] = dout
        dw_ref[...] += jnp.sum(dout * n, axis=0, keepdims=True)
        dn = dout * w
        dy_ref[...] = (rstd * (dn - n * jnp.mean(dn * n, axis=-1, keepdims=True))).astype(_MXU_DTYPE)

    row = pl.BlockSpec((tm, D_MODEL), lambda i: (i, 0))
    return pl.pallas_call(
        body, name="out_fused", grid=(t // tm,),
        in_specs=[pl.BlockSpec((2, tm, D_MODEL), lambda i: (0, i, 0)), _full((2 * D_MODEL, D_MODEL)),
                  row, row, _full((1, D_MODEL))],
        out_specs=[row, row, _full((1, D_MODEL)), _full((1, D_MODEL))],
        out_shape=[jax.ShapeDtypeStruct((t, D_MODEL), F32), jax.ShapeDtypeStruct((t, D_MODEL), _MXU_DTYPE),
                   jax.ShapeDtypeStruct((1, D_MODEL), F32), jax.ShapeDtypeStruct((1, D_MODEL), F32)],
        compiler_params=_seq_params(),
    )(y2, wo, x, tgt, wpost)


def _dw_out(y2, dy, tt):
    t = dy.shape[0]
    hr = UNIT // 2

    def body(y_ref, dy_ref, g_ref):
        @pl.when(pl.program_id(1) == 0)
        def _():
            g_ref[...] = jnp.zeros_like(g_ref)

        r = _dot_tn(y_ref[...], dy_ref[...])
        g_ref[0] += r[0:hr]
        g_ref[1] += r[hr:UNIT]

    return pl.pallas_call(
        body, name="dw_out", grid=(N_CHIPS, t // tt),
        in_specs=[pl.BlockSpec((None, tt, UNIT), lambda c, k: (c // 2, k, c % 2)),
                  pl.BlockSpec((tt, D_MODEL), lambda c, k: (k, 0))],
        out_specs=pl.BlockSpec((2, None, hr, D_MODEL), lambda c, k: (0, c, 0, 0)),
        out_shape=jax.ShapeDtypeStruct((2, N_CHIPS, hr, D_MODEL), F32),
        compiler_params=pltpu.CompilerParams(dimension_semantics=("arbitrary", "arbitrary")),
    )(y2, dy)


def _lru_bwd(dy, wo, p, h, wa, wx, ba, bx, lam, cw, cb, tm):
    t = dy.shape[0]
    nt = t // tm
    ng = tm // 8

    def body(dy_ref, wo_ref, lx_ref, halo_ref, lg_ref, h_ref, hhalo_ref, wa_ref, wx_ref, ba_ref, bx_ref,
             lam_ref, cw_ref, cb_ref,
             dp_ref, dwa_ref, dwx_ref, dba_ref, dbx_ref, dsp_ref, dcw_ref, dcb_ref,
             c_s, in_s, dh_s, dhc_s, afirst_s, dxc_s):
        i = pl.program_id(0)
        first = i == nt - 1

        @pl.when(i == 0)
        def _():
            for ref in (dwa_ref, dwx_ref, dba_ref, dbx_ref, dsp_ref, dcw_ref, dcb_ref, dhc_s, afirst_s, dxc_s):
                ref[...] = jnp.zeros_like(ref)

        sp = _softplus_neg(lam_ref[...])
        cwv = cw_ref[...]
        wav = wa_ref[...]
        wxv = wx_ref[...]
        halo = jnp.where(first, 0.0, _seg(halo_ref, 0))
        xs, xc, r, ig, a, om = _lru_gates(_seg(lx_ref, 0), halo, cwv, cb_ref[...], wav, wxv,
                                          ba_ref[...], bx_ref[...], sp)
        inv_mult = lax.rsqrt(om)
        mult = om * inv_mult
        d_y = _dot_nt(dy_ref[...], wo_ref[...])
        lg = _seg(lg_ref, 0)
        sl = _sigmoid(lg)
        hv = h_ref[...]
        d_lg = d_y * hv * (sl * (1.0 + lg * (1.0 - sl)))

        c_s[...] = _shift_up(a, afirst_s[...], 1)
        in_s[...] = d_y * (lg * sl)
        row8 = _rows((8, D_MODEL))

        def group(gg, carry):
            rows = pl.ds(pl.multiple_of((ng - 1 - gg) * 8, 8), 8)
            cv = c_s[rows, :]
            uv = in_s[rows, :]
            for s in (1, 2, 4):
                c_sh = jnp.where(row8 < 8 - s, pltpu.roll(cv, 8 - s, 0), 1.0)
                u_sh = jnp.where(row8 < 8 - s, pltpu.roll(uv, 8 - s, 0), 0.0)
                uv = cv * u_sh + uv
                cv = cv * c_sh
            hh = cv * carry + uv
            dh_s[rows, :] = hh
            return jnp.broadcast_to(hh[0:1, :], (8, D_MODEL))

        dhc_s[...] = lax.fori_loop(0, ng, group, dhc_s[...])
        afirst_s[...] = a[0:8]
        dh = dh_s[...]

        h_prev = _shift_down(hv, jnp.where(first, 0.0, hhalo_ref[...]), 1)
        gx = ig * xc
        d_mult = dh * gx
        d_i = dh * (mult * xc)
        d_xc = dh * (mult * ig)
        d_la = (dh * h_prev) * a - d_mult * ((a * a) * inv_mult)
        dsp_ref[...] += jnp.sum(d_la * r, axis=0, keepdims=True) * (-LRU_C)
        d_zr = (d_la * (-LRU_C * sp)) * (r * (1.0 - r))
        d_zi = d_i * (ig * (1.0 - ig))
        dba_ref[...] += jnp.sum(d_zr, axis=0, keepdims=True)
        dbx_ref[...] += jnp.sum(d_zi, axis=0, keepdims=True)
        back = []
        for n in range(LRU_BLOCKS):
            ln = slice(n * LRU_BW, (n + 1) * LRU_BW)
            xb = _mx(xc[:, ln])
            zr_n = _mx(d_zr[:, ln])
            zi_n = _mx(d_zi[:, ln])
            dwa_ref[n] += _dot_tn(xb, zr_n)
            dwx_ref[n] += _dot_tn(xb, zi_n)
            back.append(_dot_nt(zr_n, wav[n]) + _dot_nt(zi_n, wxv[n]))
        d_xc = d_xc + jnp.concatenate(back, axis=1)
        dcb_ref[...] += jnp.sum(d_xc, axis=0, keepdims=True)
        for k in range(4):
            dcw_ref[k:k + 1, :] += jnp.sum(d_xc * xs[k], axis=0, keepdims=True)
        head = dxc_s[...]
        d_lx = cwv[3:4] * d_xc
        for k in range(3):
            d_lx = d_lx + cwv[k:k + 1] * _shift_up(d_xc, head, 3 - k)
        dxc_s[...] = d_xc[0:8]
        dp_ref[0] = d_lx.astype(_MXU_DTYPE)
        dp_ref[1] = d_lg.astype(_MXU_DTYPE)

    rev = lambda i: nt - 1 - i
    seg = lambda s: pl.BlockSpec((2, tm, UNIT), lambda i: (s, rev(i), 0))
    halo_row = lambda i: jnp.maximum(rev(i) * (tm // 8) - 1, 0)
    row = pl.BlockSpec((tm, D_MODEL), lambda i: (rev(i), 0))
    vec = _full((1, D_MODEL))
    wblk = _full((LRU_BLOCKS, LRU_BW, LRU_BW))
    return pl.pallas_call(
        body, name="lru_bwd", grid=(nt,),
        in_specs=[row, pl.BlockSpec((D_MODEL, D_MODEL), lambda i: (0, 0)),
                  seg(0), pl.BlockSpec((2, 8, UNIT), lambda i: (0, halo_row(i), 0)), seg(1),
                  row, pl.BlockSpec((8, D_MODEL), lambda i: (halo_row(i), 0)),
                  wblk, wblk, vec, vec, vec, _full((4, D_MODEL)), vec],
        out_specs=[pl.BlockSpec((2, tm, D_MODEL), lambda i: (2, rev(i), 0)),
                   wblk, wblk, vec, vec, vec, _full((4, D_MODEL)), vec],
        out_shape=[jax.ShapeDtypeStruct((N_SEG, t, D_MODEL), _MXU_DTYPE),
                   jax.ShapeDtypeStruct((LRU_BLOCKS, LRU_BW, LRU_BW), F32),
                   jax.ShapeDtypeStruct((LRU_BLOCKS, LRU_BW, LRU_BW), F32),
                   jax.ShapeDtypeStruct((1, D_MODEL), F32), jax.ShapeDtypeStruct((1, D_MODEL), F32),
                   jax.ShapeDtypeStruct((1, D_MODEL), F32), jax.ShapeDtypeStruct((4, D_MODEL), F32),
                   jax.ShapeDtypeStruct((1, D_MODEL), F32)],
        scratch_shapes=[pltpu.VMEM((tm, D_MODEL), F32), pltpu.VMEM((tm, D_MODEL), F32),
                        pltpu.VMEM((tm, D_MODEL), F32), pltpu.VMEM((8, D_MODEL), F32),
                        pltpu.VMEM((8, D_MODEL), F32), pltpu.VMEM((8, D_MODEL), F32)],
        compiler_params=_seq_params(),
    )(dy, wo, p, p, p, h, h, wa, wx, ba, bx, lam, cw, cb)


def _hgrn_bwd(dy, wo, p, o, states, logits, gw, dp, tm):
    t = dy.shape[0]
    nt = t // tm
    nc = tm // CHUNK

    def body(dy_ref, wo_ref, qf_ref, vh_ref, o_ref, st_ref, lg_ref, gw_ref, dp_in,
             dp_ref, dgw_ref, dlb_ref, dyh_s, dstate, snext):
        del dp_in
        i = pl.program_id(0)

        @pl.when(i == 0)
        def _():
            for ref in (dgw_ref, dlb_ref, dstate, snext):
                ref[...] = jnp.zeros_like(ref)

        dyh_s[...] = _dot_nt(dy_ref[...], wo_ref[...])
        l0 = lg_ref[0:1, :]
        l1 = lg_ref[1:2, :]
        lb = 1.0 / (1.0 + jnp.exp(l1 - l0))
        gwv = gw_ref[...]
        causal = _rows((CHUNK, CHUNK)) >= _cols((CHUNK, CHUNK))
        wide = 2 * HEAD_D
        zeros_h = jnp.zeros((CHUNK, HEAD_D), _MXU_DTYPE)

        for c in reversed(range(nc)):
            rows = pl.ds(c * CHUNK, CHUNK)
            q = _seg(qf_ref, 0, rows)
            hg = _seg(vh_ref, 1, rows)
            ov = o_ref[rows, :]
            dyh = dyh_s[rows, :]
            sig, f, sq, em1, em2, qm, km, ebm, ecm, ebc = _hgrn_prep(q, _seg(qf_ref, 1, rows), lb)
            ke = _mx(km.astype(F32) * ecm)
            vb = _mx(_seg(vh_ref, 0, rows))
            sg = _sigmoid(hg)
            d_on = dyh * (hg * sg)
            x_p, y_p, dv_p, n_p, ex_p = [], [], [], [], []
            for h in range(HEADS):
                ln = slice(h * HEAD_D, (h + 1) * HEAD_D)
                o_h = ov[:, ln]
                rstd = lax.rsqrt(jnp.mean(o_h * o_h, axis=-1, keepdims=True) + EPS)
                n = o_h * rstd
                dn = d_on[:, ln] * gwv[:, ln]
                d_o = rstd * (dn - n * jnp.mean(dn * n, axis=-1, keepdims=True))
                d_ob = _mx(d_o)
                d_ot = _mx(d_o.T)
                s0 = st_ref[c, h]
                ds = dstate[h]
                both = _dot_nt(jnp.concatenate([_pad_lanes([qm[:, ln]], wide), _pad_lanes([zeros_h, d_ob], wide)], axis=0),
                               jnp.concatenate([km[:, ln], vb[:, ln]], axis=1))
                att = jnp.where(causal, both[0:CHUNK], 0.0)
                d_att = jnp.where(causal, both[CHUNK:2 * CHUNK], 0.0)
                x_p.append(jnp.dot(_pad_lanes([d_ob, _mx(d_att)], wide),
                                   _pad_rows([_mx(s0 * ebm[:, ln]), km[:, ln]], wide), preferred_element_type=F32))
                ds_c = ds * ecm[:, ln]
                y_p.append(jnp.dot(_pad_lanes([vb[:, ln], _mx(d_att.T)], wide),
                                   _pad_rows([_mx(ds_c), qm[:, ln]], wide), preferred_element_type=F32))
                dv_p.append(_dot_nt(_pad_lanes([ke[:, ln], _mx(att.T)], wide), _pad_lanes([_mx(ds), d_ot], wide)))
                ex_p.append(jnp.sum(ds * snext[h], axis=0, keepdims=True))
                dstate[h] = (ds_c + jnp.dot(d_ot, qm[:, ln], preferred_element_type=F32)) * ebm[:, ln]
                snext[h] = s0
                n_p.append(n)
            cat = lambda parts: jnp.concatenate(parts, axis=1)
            x_all, y_all, n_all = cat(x_p), cat(y_p), cat(n_p)
            dg = _cumsum_rows(qm.astype(F32) * x_all - km.astype(F32) * y_all, reverse=True) + cat(ex_p)
            df = dg / f - em2 * y_all
            dgw_ref[...] += jnp.sum(d_on * n_all, axis=0, keepdims=True)
            dlb_ref[...] += jnp.sum(df * (1.0 - sig), axis=0, keepdims=True)
            dp_ref[0, rows, :] = ((em1 * x_all) * (sq * (1.0 + q * (1.0 - sq)))).astype(_MXU_DTYPE)
            dp_ref[1, rows, :] = (df * ((1.0 - lb) * sig * (1.0 - sig))).astype(_MXU_DTYPE)
            dp_ref[2, rows, :] = cat(dv_p).astype(_MXU_DTYPE)
            dp_ref[3, rows, :] = (dyh * (n_all * gwv) * (sg * (1.0 + hg * (1.0 - sg)))).astype(_MXU_DTYPE)

    rev = lambda i: nt - 1 - i
    pair = lambda s: pl.BlockSpec((4, tm, UNIT), lambda i: (s, rev(i), 0))
    row = pl.BlockSpec((tm, D_MODEL), lambda i: (rev(i), 0))
    vec = _full((1, D_MODEL))
    return pl.pallas_call(
        body, name="hgrn_bwd", grid=(nt,),
        in_specs=[row, pl.BlockSpec((D_MODEL, D_MODEL), lambda i: (1, 0)), pair(1), pair(2), row,
                  pl.BlockSpec((nc, HEADS, HEAD_D, HEAD_D), lambda i: (rev(i), 0, 0, 0)),
                  _full((2, D_MODEL)), vec, pl.BlockSpec(memory_space=pl.ANY)],
        out_specs=[pl.BlockSpec((4, tm, D_MODEL), lambda i: (0, rev(i), 0)), vec, vec],
        out_shape=[jax.ShapeDtypeStruct((N_SEG, t, D_MODEL), _MXU_DTYPE),
                   jax.ShapeDtypeStruct((1, D_MODEL), F32), jax.ShapeDtypeStruct((1, D_MODEL), F32)],
        scratch_shapes=[pltpu.VMEM((tm, D_MODEL), F32), pltpu.VMEM((HEADS, HEAD_D, HEAD_D), F32),
                        pltpu.VMEM((HEADS, HEAD_D, HEAD_D), F32)],
        input_output_aliases={8: 0},
        compiler_params=_seq_params(),
    )(dy, wo, p, p, o, states, logits, gw, dp)


def _dp_unit(m):
    return ((m // 2 + 2) % N_SEG) * 2 + m % 2


def _dw_in(ut, dp, tt):
    t = ut.shape[1]
    hr = D_MODEL // 2

    def body(ut_ref, dp_ref, g_ref):
        @pl.when(pl.program_id(1) == 0)
        def _():
            g_ref[...] = jnp.zeros_like(g_ref)

        r = jnp.dot(ut_ref[...], dp_ref[...], preferred_element_type=F32)
        for h in range(2):
            for half in range(2):
                g_ref[h, half] += r[h * hr:(h + 1) * hr, half * UNIT:(half + 1) * UNIT]

    return pl.pallas_call(
        body, name="dw_in", grid=(N_SEG, t // tt),
        in_specs=[pl.BlockSpec((D_MODEL, tt), lambda s, k: (0, k)),
                  pl.BlockSpec((None, tt, D_MODEL), lambda s, k: (s, k, 0))],
        out_specs=pl.BlockSpec((2, 2, hr, UNIT), lambda s, k: (0, (s + 2) % N_SEG, 0, 0)),
        out_shape=jax.ShapeDtypeStruct((2, N_UNIT, hr, UNIT), F32),
        compiler_params=pltpu.CompilerParams(dimension_semantics=("arbitrary", "arbitrary")),
    )(ut, dp)


def _dx(dp, wg, x, dout, wpre, tm, parts, scatter):
    t = x.shape[0]
    n = len(parts)
    nt = t // tm

    def body(*refs):
        dp_ref, w_ref, x_ref, dout_ref, wpre_ref = refs[:5]
        ins = refs[5:5 + n]
        gx_ref, dw_ref = refs[5 + n:7 + n]
        outs = refs[7 + n:7 + 2 * n]
        send_sems, recv_sems = refs[7 + 2 * n:]
        i = pl.program_id(0)

        @pl.when(i == 0)
        def _():
            dw_ref[...] = jnp.zeros_like(dw_ref)
            for cp in _chip_copies(ins, outs, send_sems, recv_sems, scatter):
                cp.start()

        du = None
        for m in range(N_UNIT):
            part = _dot_nt(dp_ref[m // 2, :, (m % 2) * UNIT:(m % 2 + 1) * UNIT], w_ref[_dp_unit(m)])
            du = part if du is None else du + part
        xv = x_ref[...]
        rstd = lax.rsqrt(jnp.mean(xv * xv, axis=-1, keepdims=True) + EPS)
        xn = xv * rstd
        dw_ref[...] += jnp.sum(du * xn, axis=0, keepdims=True)
        dn = du * wpre_ref[...]
        gx_ref[...] = dout_ref[...] + rstd * (dn - xn * jnp.mean(dn * xn, axis=-1, keepdims=True))

        @pl.when(i == nt - 1)
        def _():
            copies = _chip_copies(ins, outs, send_sems, recv_sems, scatter)
            for cp in copies:
                cp.wait_recv()
            for cp in copies:
                cp.wait_send()

    row = pl.BlockSpec((tm, D_MODEL), lambda i: (i, 0))
    any_spec = pl.BlockSpec(memory_space=pl.ANY)
    res = pl.pallas_call(
        body, name="dx", grid=(nt,),
        in_specs=[pl.BlockSpec((N_SEG, tm, D_MODEL), lambda i: (0, i, 0)), _resident((N_UNIT, D_MODEL, UNIT)),
                  row, row, _full((1, D_MODEL))] + [any_spec] * n,
        out_specs=[row, _full((1, D_MODEL))] + [any_spec] * n,
        out_shape=[jax.ShapeDtypeStruct((t, D_MODEL), F32), jax.ShapeDtypeStruct((1, D_MODEL), F32)]
        + _chip_exchange_shapes(parts, scatter),
        scratch_shapes=[pltpu.SemaphoreType.DMA((max(3 * n, 1),)), pltpu.SemaphoreType.DMA((max(3 * n, 1),))],
        compiler_params=_seq_params(),
    )(dp, wg, x, dout, wpre, *parts)
    return res[0], res[1], res[2:]


CHUNK_BYTES = 256 * 1024
MAX_CHUNKS = 16


def _n_chunks(shape, dtype, rows):
    nbytes = jnp.dtype(dtype).itemsize
    for d in shape:
        nbytes *= d
    k = max(1, min(MAX_CHUNKS, rows, nbytes // CHUNK_BYTES))
    while rows % k:
        k -= 1
    return k


def _chip_exchange_shapes(arrays, scatter):
    return [jax.ShapeDtypeStruct((3,) + tuple(a.shape[1:] if sc else a.shape), a.dtype)
            for a, sc in zip(arrays, scatter)]


def _chip_copies(ins, outs, send_sems, recv_sems, scatter):
    if not ins:
        return []
    x, y, c = lax.axis_index("x"), lax.axis_index("y"), lax.axis_index("c")
    peers = [(1 - x, y), (x, 1 - y), (1 - x, 1 - y)]
    copies = []
    for a in range(len(ins)):
        for j, (px, py) in enumerate(peers):
            copies.append(pltpu.make_async_remote_copy(
                src_ref=ins[a].at[2 * px + py] if scatter[a] else ins[a], dst_ref=outs[a].at[j],
                send_sem=send_sems.at[a * 3 + j], recv_sem=recv_sems.at[a * 3 + j],
                device_id=(px, py, c), device_id_type=MESH))
    return copies


def _core_swap(arrays, name, halved):
    n = len(arrays)
    shapes, n_sems = _core_swap_plan(arrays, halved)

    def body(*refs):
        copies = _core_swap_copies(arrays, halved, refs[:n], refs[n:2 * n], refs[2 * n], refs[2 * n + 1])
        for cp in copies:
            cp.start()
        for cp in copies:
            cp.wait_recv()
        for cp in copies:
            cp.wait_send()

    any_spec = pl.BlockSpec(memory_space=pl.ANY)
    return pl.pallas_call(
        body, name=name,
        in_specs=[any_spec] * n, out_specs=[any_spec] * n,
        out_shape=shapes,
        scratch_shapes=[pltpu.SemaphoreType.DMA((n_sems,)), pltpu.SemaphoreType.DMA((n_sems,))],
    )(*arrays)


def _core_swap_plan(arrays, halved):
    shapes = [a.shape[1:] if halved else a.shape for a in arrays]
    n_sems = sum(_n_chunks(s, a.dtype, s[0]) for s, a in zip(shapes, arrays))
    return [jax.ShapeDtypeStruct(s, a.dtype) for s, a in zip(shapes, arrays)], max(n_sems, 1)


def _core_swap_copies(arrays, halved, ins, outs, send_sems, recv_sems):
    if not arrays:
        return []
    x, y, c = lax.axis_index("x"), lax.axis_index("y"), lax.axis_index("c")
    copies, sem = [], 0
    for a, arr in enumerate(arrays):
        shape = arr.shape[1:] if halved else arr.shape
        k = _n_chunks(shape, arr.dtype, shape[0])
        step = shape[0] // k
        for j in range(k):
            rows = pl.ds(j * step, step)
            copies.append(pltpu.make_async_remote_copy(
                src_ref=ins[a].at[1 - c, rows] if halved else ins[a].at[rows], dst_ref=outs[a].at[rows],
                send_sem=send_sems.at[sem], recv_sem=recv_sems.at[sem],
                device_id=(x, y, 1 - c), device_id_type=MESH))
            sem += 1
    return copies


def _finish_exchange(arrays, small):
    n = len(arrays)
    ks = [_n_chunks(a.shape, a.dtype, a.shape[0]) for a in arrays]
    offs = [sum(ks[:a]) for a in range(n)]
    total = sum(ks)

    def body(*refs):
        ins, small_ref = refs[:n], refs[n]
        outs, gathered = refs[n + 1:2 * n + 1], refs[2 * n + 1]
        send_sems, recv_sems = refs[2 * n + 2:]
        x, y, c = lax.axis_index("x"), lax.axis_index("y"), lax.axis_index("c")
        copies = []
        for a in range(n):
            step = arrays[a].shape[0] // ks[a]
            for j in range(ks[a]):
                rows = pl.ds(j * step, step)
                copies.append(pltpu.make_async_remote_copy(
                    src_ref=ins[a].at[rows], dst_ref=outs[a].at[rows],
                    send_sem=send_sems.at[offs[a] + j], recv_sem=recv_sems.at[offs[a] + j],
                    device_id=(x, y, 1 - c), device_id_type=MESH))
        for slot in range(7):
            dx, dy, dc = (slot + 1) % 2, ((slot + 1) // 2) % 2, (slot + 1) // 4
            copies.append(pltpu.make_async_remote_copy(
                src_ref=small_ref, dst_ref=gathered.at[slot],
                send_sem=send_sems.at[total + slot], recv_sem=recv_sems.at[total + slot],
                device_id=(x + dx - 2 * x * dx, y + dy - 2 * y * dy, c + dc - 2 * c * dc), device_id_type=MESH))
        for cp in copies:
            cp.start()
        for cp in copies:
            cp.wait_recv()
        for cp in copies:
            cp.wait_send()

    any_spec = pl.BlockSpec(memory_space=pl.ANY)
    res = pl.pallas_call(
        body, name="reduce_cores",
        in_specs=[any_spec] * (n + 1), out_specs=[any_spec] * (n + 1),
        out_shape=[jax.ShapeDtypeStruct(a.shape, a.dtype) for a in arrays]
        + [jax.ShapeDtypeStruct((7,) + small.shape, small.dtype)],
        scratch_shapes=[pltpu.SemaphoreType.DMA((total + 7,)), pltpu.SemaphoreType.DMA((total + 7,))],
    )(*arrays, small)
    return res[:n], res[n]


def _sum_own_half(g, b, name, out_dtype, tr=2048):
    _, rows, cols = g.shape
    tr = min(tr, rows)
    core = lax.axis_index("c").reshape(1)

    def body(c_ref, g_ref, b_ref, o_ref):
        del c_ref
        o_ref[...] = (g_ref[...] + b_ref[...]).astype(out_dtype)

    return pl.pallas_call(
        body, name=name,
        grid_spec=pltpu.PrefetchScalarGridSpec(
            num_scalar_prefetch=1, grid=(rows // tr,),
            in_specs=[pl.BlockSpec((None, tr, cols), lambda i, c_ref: (c_ref[0], i, 0)),
                      pl.BlockSpec((tr, cols), lambda i, c_ref: (i, 0))],
            out_specs=pl.BlockSpec((tr, cols), lambda i, c_ref: (i, 0))),
        out_shape=jax.ShapeDtypeStruct((rows, cols), out_dtype),
        compiler_params=pltpu.CompilerParams(dimension_semantics=("parallel",)),
    )(core, g, b)


def _sum_chips(own, r, slabbed, name, tr=2048):
    _, rows, cols = r.shape
    tr = min(tr, rows)
    chip = (2 * lax.axis_index("x") + lax.axis_index("y")).reshape(1)

    def body(c_ref, own_ref, r_ref, o_ref):
        del c_ref
        f = lambda val: val.astype(F32)
        o_ref[...] = (f(own_ref[...]) + f(r_ref[0])) + (f(r_ref[1]) + f(r_ref[2]))

    if slabbed:
        own_spec = pl.BlockSpec((None, tr, cols), lambda i, c_ref: (c_ref[0], i, 0))
    else:
        own_spec = pl.BlockSpec((tr, cols), lambda i, c_ref: (i, 0))
    return pl.pallas_call(
        body, name=name,
        grid_spec=pltpu.PrefetchScalarGridSpec(
            num_scalar_prefetch=1, grid=(rows // tr,),
            in_specs=[own_spec, pl.BlockSpec((3, tr, cols), lambda i, c_ref: (0, i, 0))],
            out_specs=pl.BlockSpec((tr, cols), lambda i, c_ref: (i, 0))),
        out_shape=jax.ShapeDtypeStruct((rows, cols), F32),
        compiler_params=pltpu.CompilerParams(dimension_semantics=("parallel",)),
    )(chip, own, r)


def _adamw_big(own, other, w, m, v, name, grid, g_block, g_index, w_block, w_index):
    core = lax.axis_index("c").reshape(1)
    half_axis = len(grid) - 1

    def body(c_ref, own_ref, oth_ref, w_ref, m_ref, v_ref, g_ref, d_ref, nm_ref, nv_ref):
        mine = pl.program_id(half_axis) == c_ref[0]
        gv = jnp.where(mine, own_ref[...], oth_ref[...])
        g_ref[...] = gv
        d_ref[...], nm_ref[...], nv_ref[...] = _adamw(w_ref[...], gv, m_ref[...], v_ref[...])

    g_spec = pl.BlockSpec(g_block, lambda *a: g_index(*a[:half_axis]))
    w_spec = pl.BlockSpec(w_block, lambda *a: w_index(*a[:-1]))
    shp = jax.ShapeDtypeStruct(w.shape, F32)
    return pl.pallas_call(
        body, name=name,
        grid_spec=pltpu.PrefetchScalarGridSpec(
            num_scalar_prefetch=1, grid=grid,
            in_specs=[g_spec, g_spec, w_spec, w_spec, w_spec], out_specs=[w_spec] * 4),
        out_shape=[shp] * 4,
        compiler_params=pltpu.CompilerParams(dimension_semantics=("parallel",) * len(grid)),
    )(core, own, other, w, m, v)


def _small_finish(own, others, lam, logits):
    def body(a_ref, b_ref, lam_ref, lg_ref, o_ref):
        g = (((a_ref[...] + b_ref[0]) + (b_ref[1] + b_ref[2]))
             + ((b_ref[3] + b_ref[4]) + (b_ref[5] + b_ref[6])))
        o_ref[...] = g
        o_ref[8:9, :] = g[8:9, :] * (-_sigmoid(-lam_ref[...]))
        lb = 1.0 / (1.0 + jnp.exp(lg_ref[1:2, :] - lg_ref[0:1, :]))
        d0 = g[9:10, :] * (lb * (1.0 - lb))
        o_ref[9:10, :] = d0
        o_ref[10:11, :] = -d0

    return pl.pallas_call(
        body, name="small_finish",
        in_specs=[_full((SMALL_ROWS, D_MODEL)), _full((7, SMALL_ROWS, D_MODEL)), _full((1, D_MODEL)),
                  _full((2, D_MODEL))],
        out_specs=_full((SMALL_ROWS, D_MODEL)),
        out_shape=jax.ShapeDtypeStruct((SMALL_ROWS, D_MODEL), F32),
    )(own, others, lam, logits)


def _adamw_small(ws, gs, ms, vs):
    n = len(ws)

    def body(*refs):
        w_r, g_r, m_r, v_r = refs[:n], refs[n:2 * n], refs[2 * n:3 * n], refs[3 * n:4 * n]
        d_o, m_o, v_o = refs[4 * n:5 * n], refs[5 * n:6 * n], refs[6 * n:7 * n]
        for j in range(n):
            d_o[j][...], m_o[j][...], v_o[j][...] = _adamw(w_r[j][...], g_r[j][...], m_r[j][...], v_r[j][...])

    specs = [_full(w.shape) for w in ws]
    shapes = [jax.ShapeDtypeStruct(w.shape, F32) for w in ws]
    outs = pl.pallas_call(
        body, name="adamw_small",
        in_specs=specs * 4, out_specs=specs * 3, out_shape=shapes * 3,
    )(*ws, *gs, *ms, *vs)
    return outs[:n], outs[n:2 * n], outs[2 * n:]


def _local_step(x, tgt, ut, p, pre_w, wg, cw, cb, wa, wx, ba, bx, lam, logits, gw, wo, post_w, tm,
                prepare=None):
    h, y2 = _lru_fwd(p, wa, wx, ba, bx, lam, cw, cb, tm)
    y2, o, states = _hgrn_fwd(p, logits, gw, y2, tm)
    dout, dy, sq, d_post = _out_fused(y2, wo, x, tgt, post_w, min(2 * tm, x.shape[0]))
    g_out = _dw_out(y2, dy, min(GRAD_K_TILE, x.shape[0]))
    dp, d_wa, d_wx, d_ba, d_bx, d_sp, d_cw, d_cb = _lru_bwd(dy, wo, p, h, wa, wx, ba, bx, lam, cw, cb, tm)
    dp, d_gw, d_lb = _hgrn_bwd(dy, wo, p, o, states, logits, gw, dp, tm)
    g_in = _dw_in(ut, dp, min(GRAD_K_TILE, x.shape[0]))
    parts, scatter = prepare(g_in, g_out, d_wa, d_wx) if prepare is not None else ([], [])
    grad_x, d_pre, received = _dx(dp, wg, x, dout, pre_w, tm, parts, scatter)
    small = jnp.concatenate([d_pre, d_cw, d_cb, d_ba, d_bx, d_sp, d_lb,
                             jnp.zeros((1, D_MODEL), F32), d_gw, d_post, sq,
                             jnp.zeros((SMALL_ROWS - 14, D_MODEL), F32)], axis=0)
    return grad_x, g_in, g_out, d_wa, d_wx, small, parts, received


def kernel(x, pre_norm_w, w_in, conv_w, conv_b, lru_w_a, lru_b_a, lru_w_x, lru_b_x, lru_lambda, hgrn_lb_logits, hgrn_gnorm_w, w_out, post_norm_w, loss_target, m_pre_norm_w, m_w_in, m_conv_w, m_conv_b, m_lru_w_a, m_lru_b_a, m_lru_w_x, m_lru_b_x, m_lru_lambda, m_hgrn_lb_logits, m_hgrn_gnorm_w, m_w_out, m_post_norm_w, v_pre_norm_w, v_w_in, v_conv_w, v_conv_b, v_lru_w_a, v_lru_b_a, v_lru_w_x, v_lru_b_x, v_lru_lambda, v_hgrn_lb_logits, v_hgrn_gnorm_w, v_w_out, v_post_norm_w):
    t = x.shape[1]
    chip = 2 * lax.axis_index("x") + lax.axis_index("y")

    win_units = w_in[0].astype(_MXU_DTYPE).reshape(D_MODEL, 3, UNIT).transpose(1, 0, 2)
    small_w = jnp.concatenate([conv_w[0], lru_b_a[0], lru_b_x[0]], axis=1)
    ut, p, wg4, (wo4, wa4, wx4), sw4 = _in_proj_gather(
        x[0], pre_norm_w, win_units,
        [w_out[0].astype(_MXU_DTYPE).reshape(2, 256, D_MODEL), lru_w_a[0].astype(_MXU_DTYPE).reshape(2, 128, LRU_BW),
         lru_w_x[0].astype(_MXU_DTYPE).reshape(2, 128, LRU_BW)],
        small_w, min(PROJ_TILE, t))
    wg = wg4.reshape(N_UNIT, D_MODEL, UNIT)
    wo = wo4.reshape(2 * D_MODEL, D_MODEL)
    by_block = lambda w4: w4.reshape(N_CHIPS, LRU_BLOCKS, 64, LRU_BW).transpose(1, 0, 2, 3).reshape(
        LRU_BLOCKS, LRU_BW, LRU_BW)
    wa, wx = by_block(wa4), by_block(wx4)
    sw4 = sw4.reshape(N_CHIPS, 4, 384)
    cw = sw4[:, :, 0:256].transpose(1, 0, 2).reshape(4, D_MODEL)
    ba = sw4[:, :, 256:320].transpose(1, 0, 2).reshape(1, D_MODEL)
    bx = sw4[:, :, 320:384].transpose(1, 0, 2).reshape(1, D_MODEL)

    def lru_layout(g):
        g = g.reshape(2, 2, N_CHIPS, 64, LRU_BW).transpose(0, 2, 1, 3, 4)
        return g.reshape(2, N_CHIPS * 128, LRU_BW)

    def prepare(g_in, g_out, d_wa, d_wx):
        g_in = g_in.reshape(2, N_UNIT * 512, UNIT)
        g_out = g_out.reshape(2, N_CHIPS * 256, D_MODEL)
        g_wa, g_wx = lru_layout(d_wa), lru_layout(d_wx)
        b_in, b_out, b_wa, b_wx = _core_swap([g_in, g_out, g_wa, g_wx], "reduce_swap", halved=True)
        parts = [_sum_own_half(g_in, b_in, "presum_in", _WIRE_DTYPE).reshape(N_CHIPS, 3 * 512, UNIT),
                 _sum_own_half(g_out, b_out, "presum_out", _WIRE_DTYPE).reshape(N_CHIPS, 256, D_MODEL),
                 _sum_own_half(g_wa, b_wa, "presum_wa", _WIRE_DTYPE).reshape(N_CHIPS, 128, LRU_BW),
                 _sum_own_half(g_wx, b_wx, "presum_wx", _WIRE_DTYPE).reshape(N_CHIPS, 128, LRU_BW)]
        return parts, [True] * 4

    grad_x, _, _, _, _, small, (p_in, p_out, p_wa, p_wx), (r_in, r_out, r_wa, r_wx) = _local_step(
        x[0], loss_target[0], ut, p, pre_norm_w, wg, cw, conv_b, wa, wx, ba, bx, lru_lambda, hgrn_lb_logits,
        hgrn_gnorm_w, wo, post_norm_w, min(ROW_TILE, t), prepare)
    s_in = _sum_chips(p_in, r_in, True, "sum_in")
    s_out = _sum_chips(p_out, r_out, True, "sum_out")
    s_wa = _sum_chips(p_wa, r_wa, True, "sum_wa")
    s_wx = _sum_chips(p_wx, r_wx, True, "sum_wx")
    (o_in, o_out, o_wa, o_wx), small_others = _finish_exchange([s_in, s_out, s_wa, s_wx], small)

    g_w_in, d_w_in, nm_w_in, nv_w_in = _adamw_big(
        s_in.reshape(3, 512, UNIT), o_in.reshape(3, 512, UNIT), w_in[0], m_w_in[0], v_w_in[0], "adamw_w_in", (3, 2),
        (None, 512, UNIT), lambda k: (k, 0, 0), (512, UNIT), lambda k, h: (h, k))
    g_w_out, d_w_out, nm_w_out, nv_w_out = _adamw_big(
        s_out, o_out, w_out[0], m_w_out[0], v_w_out[0], "adamw_w_out", (2,),
        (256, D_MODEL), lambda: (0, 0), (256, D_MODEL), lambda h: (h, 0))
    sq2 = lambda a: a.reshape(LRU_BW, LRU_BW)
    lru_specs = ((2,), (128, LRU_BW), lambda: (0, 0), (128, LRU_BW), lambda h: (h, 0))
    g_wa, d_wa2, nm_wa, nv_wa = _adamw_big(s_wa, o_wa, sq2(lru_w_a), sq2(m_lru_w_a), sq2(v_lru_w_a), "adamw_wa",
                                           *lru_specs)
    g_wx, d_wx2, nm_wx, nv_wx = _adamw_big(s_wx, o_wx, sq2(lru_w_x), sq2(m_lru_w_x), sq2(v_lru_w_x), "adamw_wx",
                                           *lru_specs)

    gs = _small_finish(small, small_others, lru_lambda, hgrn_lb_logits)
    loss = jnp.sum(gs[13]) * (0.5 / D_MODEL)
    g_pre = gs[0:1]
    g_cw = lax.dynamic_slice(gs[1:5], (0, chip * 256), (4, 256))
    g_cb = gs[5:6]
    g_ba = lax.dynamic_slice(gs[6].reshape(LRU_BLOCKS, N_CHIPS, 64), (0, chip, 0), (LRU_BLOCKS, 1, 64)).reshape(4, 64)
    g_bx = lax.dynamic_slice(gs[7].reshape(LRU_BLOCKS, N_CHIPS, 64), (0, chip, 0), (LRU_BLOCKS, 1, 64)).reshape(4, 64)
    g_lam = gs[8:9]
    g_lb = gs[9:11]
    g_gw = gs[11:12]
    g_post = gs[12:13]
    small_g = [g_pre, g_cw, g_cb, g_ba, g_bx, g_lam, g_lb, g_gw, g_post]
    two_d = lambda a: a.reshape(a.shape[-2:])
    small_w_list = [pre_norm_w, conv_w, conv_b, lru_b_a, lru_b_x, lru_lambda, hgrn_lb_logits, hgrn_gnorm_w, post_norm_w]
    small_m_list = [m_pre_norm_w, m_conv_w, m_conv_b, m_lru_b_a, m_lru_b_x, m_lru_lambda, m_hgrn_lb_logits,
                    m_hgrn_gnorm_w, m_post_norm_w]
    small_v_list = [v_pre_norm_w, v_conv_w, v_conv_b, v_lru_b_a, v_lru_b_x, v_lru_lambda, v_hgrn_lb_logits,
                    v_hgrn_gnorm_w, v_post_norm_w]
    sd, sm, sv = _adamw_small([two_d(a) for a in small_w_list], small_g,
                              [two_d(a) for a in small_m_list], [two_d(a) for a in small_v_list])

    def shaped(vals, refs):
        return [val.reshape(ref.shape) for val, ref in zip(vals, refs)]

    s_g = shaped(small_g, small_w_list)
    s_d = shaped(sd, small_w_list)
    s_m = shaped(sm, small_w_list)
    s_v = shaped(sv, small_w_list)

    def ordered(small, big_in, big_wa, big_wx, big_out):
        pre, cw_, cb_, ba_, bx_, lam_, lb_, gw_, post_ = small
        return [pre, big_in.reshape(w_in.shape), cw_, cb_, big_wa.reshape(lru_w_a.shape), ba_,
                big_wx.reshape(lru_w_x.shape), bx_, lam_, lb_, gw_, big_out.reshape(w_out.shape), post_]

    grads = ordered(s_g, g_w_in, g_wa, g_wx, g_w_out)
    deltas = ordered(s_d, d_w_in, d_wa2, d_wx2, d_w_out)
    new_m = ordered(s_m, nm_w_in, nm_wa, nm_wx, nm_w_out)
    new_v = ordered(s_v, nv_w_in, nv_wa, nv_wx, nv_w_out)
    return (loss, grad_x.reshape(x.shape), *grads, *deltas, *new_m, *new_v)
```
